```python
import math
import jax, jax.numpy as jnp
from jax import lax
import numpy as np

D_MODEL = 1024
BATCH = 8
SEQ = 8192
DEPTH = 2

N_META = 16
BLOCK = 128
EPS = 1e-6

SSD_HEADS = 8
SSD_HEAD_DIM = 64
SSD_WIDTH = SSD_HEADS * SSD_HEAD_DIM
SSD_GROUPS = 2
SSD_HEADS_PER_GROUP = SSD_HEADS // SSD_GROUPS
SSD_STATE = 128
SSD_CONV = 4
SSD_XBC = SSD_WIDTH + 2 * SSD_GROUPS * SSD_STATE

SB_HEADS = 4
SB_HEAD_DIM = 64
SB_WIDTH = SB_HEADS * SB_HEAD_DIM

MLA_HEADS = 4
MLA_NOPE = 64
MLA_ROPE = 32
MLA_V = 64
MLA_WIDTH = MLA_HEADS * MLA_V
MLA_Q_RANK = 192
MLA_KV_RANK = 128
ROPE_BASE = 10000.0

MIX_WIDTH = SSD_WIDTH + SB_WIDTH + MLA_WIDTH
IN_SIZES = (SSD_WIDTH, SSD_XBC, SSD_HEADS, SB_WIDTH, SB_WIDTH, SB_WIDTH, MLA_Q_RANK, MLA_KV_RANK, MLA_ROPE)
IN_COLS = 2664

D_FF = 2816
FFN_CONV = 3

kernel_name = "hymba_ssd_stickbreak_mla_convffn"


def rmsnorm(x, g):
    xf = x.astype(jnp.float32)
    xf = xf * lax.rsqrt(jnp.mean(xf * xf, axis=-1, keepdims=True) + EPS)
    return xf.astype(x.dtype) * g


def causal_dwconv(u, w):
    k_w, c = w.shape
    return lax.conv_general_dilated(
        u, w.reshape(k_w, 1, c).astype(u.dtype), window_strides=(1,), padding=[(k_w - 1, 0)],
        dimension_numbers=("NWC", "WIO", "NWC"), feature_group_count=c)


def rope_tables(length, dtype):
    pos = jnp.arange(length, dtype=jnp.float32)
    inv = 1.0 / (ROPE_BASE ** (jnp.arange(0, MLA_ROPE, 2, dtype=jnp.float32) / MLA_ROPE))
    ang = pos[:, None] * inv[None, :]
    ang = jnp.concatenate([ang, ang], axis=-1)
    return jnp.cos(ang).astype(dtype), jnp.sin(ang).astype(dtype)


def apply_rope(x, cos, sin):
    half = MLA_ROPE // 2
    rot = jnp.concatenate([-x[..., half:], x[..., :half]], axis=-1)
    return x * cos[None, :, None, :] + rot * sin[None, :, None, :]


def sweep_query_blocks(q, block_fn):
    bsz, length = q.shape[0], q.shape[1]
    pad = BLOCK - N_META
    qp = jnp.pad(q, ((0, 0), (pad, 0), (0, 0), (0, 0)))
    nb = qp.shape[1] // BLOCK
    qb = jnp.moveaxis(qp.reshape(bsz, nb, BLOCK, q.shape[2], q.shape[3]), 1, 0)
    starts = jnp.arange(nb, dtype=jnp.int32) * BLOCK - pad
    offs = jnp.arange(BLOCK, dtype=jnp.int32)
    out = lax.map(lambda a: block_fn(a[0], a[1] + offs), (qb, starts))
    out = jnp.moveaxis(out, 0, 1).reshape(bsz, nb * BLOCK, -1)
    return out[:, pad:]


def ssd_mixer(z, xbc, dt_raw, conv_w, conv_b, dt_bias, a_log, d_skip, norm_g):
    f32 = jnp.float32
    bsz, length, _ = z.shape
    G, R, P, N = SSD_GROUPS, SSD_HEADS_PER_GROUP, SSD_HEAD_DIM, SSD_STATE
    gn = G * N
    xbc = jax.nn.silu(causal_dwconv(xbc, conv_w) + conv_b)
    xs = xbc[..., :SSD_WIDTH].reshape(bsz, length, SSD_HEADS, P).astype(f32)
    b_in = xbc[..., SSD_WIDTH:SSD_WIDTH + gn].reshape(bsz, length, G, N).astype(f32)
    c_in = xbc[..., SSD_WIDTH + gn:].reshape(bsz, length, G, N).astype(f32)
    dt = jax.nn.softplus((dt_raw + dt_bias).astype(f32))
    a = -jnp.exp(a_log.astype(f32))

    pad = BLOCK - N_META
    nc = (length + pad) // BLOCK

    def chunked(t):
        t = jnp.pad(t, ((0, 0), (pad, 0)) + ((0, 0),) * (t.ndim - 2))
        return t.reshape((bsz, nc, BLOCK) + t.shape[2:])

    X = chunked(xs * dt[..., None]).reshape(bsz, nc, BLOCK, G, R, P)
    dA = chunked(dt * a).reshape(bsz, nc, BLOCK, G, R)
    Bc = chunked(b_in)
    Cc = chunked(c_in)
    a_cs = jnp.cumsum(dA, axis=2)

    causal = jnp.tril(jnp.ones((BLOCK, BLOCK), dtype=bool))
    seg = a_cs[:, :, :, None] - a_cs[:, :, None, :]
    decay_ls = jnp.exp(jnp.where(causal[:, :, None, None], seg, -jnp.inf))
    cb = jnp.einsum('bclgn,bcsgn->bclsg', Cc, Bc)
    y_diag = jnp.einsum('bclsg,bclsgr,bcsgrp->bclgrp', cb, decay_ls, X)

    decay_to_end = jnp.exp(a_cs[:, :, -1:] - a_cs)
    states = jnp.einsum('bclgn,bclgr,bclgrp->bcgrpn', Bc, decay_to_end, X)
    chunk_decay = jnp.exp(a_cs[:, :, -1])

    def step(h, inp):
        st, dec = inp
        return h * dec[..., None, None] + st, h

    h0 = jnp.zeros((bsz, G, R, P, N), f32)
    _, h_prev = lax.scan(step, h0, (jnp.moveaxis(states, 1, 0), jnp.moveaxis(chunk_decay, 1, 0)))
    h_prev = jnp.moveaxis(h_prev, 0, 1)
    y_off = jnp.einsum('bclgn,bcgrpn,bclgr->bclgrp', Cc, h_prev, jnp.exp(a_cs))

    y = (y_diag + y_off).reshape(bsz, nc * BLOCK, SSD_WIDTH)[:, pad:]
    y = y + (xs * d_skip.astype(f32)[:, None]).reshape(bsz, length, SSD_WIDTH)
    y = y.astype(z.dtype) * jax.nn.silu(z)
    return rmsnorm(y, norm_g)


def stick_breaking_attention(q, k, v):
    length = k.shape[1]
    key_pos = jnp.arange(length, dtype=jnp.int32)
    scale = SB_HEAD_DIM ** -0.5

    def block_fn(qi, qpos):
        zs = jnp.einsum('bqhd,bkhd->bhqk', qi, k).astype(jnp.float32) * scale
        mask = key_pos[None, :] < qpos[:, None]
        u = jnp.where(mask, jax.nn.log_sigmoid(-zs), 0.0)
        after = lax.cumsum(u, axis=3, reverse=True) - u
        w = jnp.where(mask, jnp.exp(jax.nn.log_sigmoid(zs) + after), 0.0)
        return jnp.einsum('bhqk,bkhd->bqhd', w.astype(v.dtype), v)

    return sweep_query_blocks(q, block_fn)


def mla_attention(q_a, c_kv, k_r, q_norm_g, kv_norm_g, w_uq, w_ukv, cos, sin):
    bsz, length, _ = q_a.shape
    q = (rmsnorm(q_a, q_norm_g) @ w_uq).reshape(bsz, length, MLA_HEADS, MLA_NOPE + MLA_ROPE)
    q = jnp.concatenate([q[..., :MLA_NOPE], apply_rope(q[..., MLA_NOPE:], cos, sin)], axis=-1)
    kv = (rmsnorm(c_kv, kv_norm_g) @ w_ukv).reshape(bsz, length, MLA_HEADS, MLA_NOPE + MLA_V)
    k_rope = apply_rope(k_r[:, :, None, :], cos, sin)
    k = jnp.concatenate([kv[..., :MLA_NOPE],
                         jnp.broadcast_to(k_rope, (bsz, length, MLA_HEADS, MLA_ROPE))], axis=-1)
    v = kv[..., MLA_NOPE:]
    key_pos = jnp.arange(length, dtype=jnp.int32)
    scale = (MLA_NOPE + MLA_ROPE) ** -0.5

    def block_fn(qi, qpos):
        s = jnp.einsum('bqhd,bkhd->bhqk', qi, k).astype(jnp.float32) * scale
        mask = key_pos[None, :] <= qpos[:, None]
        p = jax.nn.softmax(jnp.where(mask, s, -1e30), axis=-1)
        return jnp.einsum('bhqk,bkhd->bqhd', p.astype(v.dtype), v)

    return sweep_query_blocks(q, block_fn)


def conv_ffn(h, w_up, conv_w, conv_b, w_down):
    u = causal_dwconv(h @ w_up, conv_w) + conv_b
    return (jax.nn.silu(u[..., :D_FF]) * u[..., D_FF:]) @ w_down


def _fwd_setup_inputs(seed: int = 0) -> dict:
    key = jax.random.key(seed)
    ks = jax.random.split(key, 24)
    f32 = jnp.float32
    nrm = lambda k, shape, s: jax.random.normal(k, shape, f32) * s
    gain = lambda k, shape: 1.0 + 0.05 * jax.random.normal(k, shape, f32)
    dt = jnp.exp(jax.random.uniform(ks[5], (DEPTH, SSD_HEADS), f32) * (math.log(0.1) - math.log(1e-3)) + math.log(1e-3))
    return {
        "x": nrm(ks[0], (BATCH, SEQ, D_MODEL), 1.0),
        "meta_tokens": nrm(ks[1], (N_META, D_MODEL), 1.0),
        "norm_mix_g": gain(ks[2], (DEPTH, D_MODEL)),
        "w_in": nrm(ks[3], (DEPTH, D_MODEL, IN_COLS), D_MODEL ** -0.5),
        "ssd_conv_w": nrm(ks[4], (DEPTH, SSD_CONV, SSD_XBC), SSD_CONV ** -0.5),
        "ssd_conv_b": nrm(ks[6], (DEPTH, SSD_XBC), 0.02),
        "ssd_dt_bias": dt + jnp.log(-jnp.expm1(-dt)),
        "ssd_a_log": jnp.log(jax.random.uniform(ks[7], (DEPTH, SSD_HEADS), f32, 1.0, 16.0)),
        "ssd_d": gain(ks[8], (DEPTH, SSD_HEADS)),
        "ssd_norm_g": gain(ks[9], (DEPTH, SSD_WIDTH)),
        "sb_norm_g": gain(ks[10], (DEPTH, SB_WIDTH)),
        "mla_q_norm_g": gain(ks[11], (DEPTH, MLA_Q_RANK)),
        "mla_kv_norm_g": gain(ks[12], (DEPTH, MLA_KV_RANK)),
        "mla_w_uq": nrm(ks[13], (DEPTH, MLA_Q_RANK, MLA_HEADS * (MLA_NOPE + MLA_ROPE)), MLA_Q_RANK ** -0.5),
        "mla_w_ukv": nrm(ks[14], (DEPTH, MLA_KV_RANK, MLA_HEADS * (MLA_NOPE + MLA_V)), MLA_KV_RANK ** -0.5),
        "mla_norm_g": gain(ks[15], (DEPTH, MLA_WIDTH)),
        "w_out": nrm(ks[16], (DEPTH, MIX_WIDTH, D_MODEL), MIX_WIDTH ** -0.5),
        "norm_ffn_g": gain(ks[17], (DEPTH, D_MODEL)),
        "ffn_w_up": nrm(ks[18], (DEPTH, D_MODEL, 2 * D_FF), D_MODEL ** -0.5),
        "ffn_conv_w": nrm(ks[19], (DEPTH, FFN_CONV, 2 * D_FF), FFN_CONV ** -0.5),
        "ffn_conv_b": nrm(ks[20], (DEPTH, 2 * D_FF), 0.02),
        "ffn_w_down": nrm(ks[21], (DEPTH, D_FF, D_MODEL), D_FF ** -0.5),
        "final_norm_g": gain(ks[22], (D_MODEL,)),
    }


def _fwd_reference(x, meta_tokens, norm_mix_g, w_in, ssd_conv_w, ssd_conv_b, ssd_dt_bias, ssd_a_log,
              ssd_d, ssd_norm_g, sb_norm_g, mla_q_norm_g, mla_kv_norm_g, mla_w_uq, mla_w_ukv,
              mla_norm_g, w_out, norm_ffn_g, ffn_w_up, ffn_conv_w, ffn_conv_b, ffn_w_down,
              final_norm_g):
    bsz = x.shape[0]
    meta = jnp.broadcast_to(meta_tokens[None].astype(x.dtype), (bsz, N_META, x.shape[2]))
    h = jnp.concatenate([meta, x], axis=1)
    length = h.shape[1]
    cos, sin = rope_tables(length, x.dtype)
    cuts = [int(c) for c in np.cumsum(IN_SIZES)[:-1]]

    for l in range(DEPTH):
        u = rmsnorm(h, norm_mix_g[l]) @ w_in[l]
        z, xbc, dt_raw, q_sb, k_sb, v_sb, q_a, c_kv, k_r = jnp.split(u, cuts, axis=-1)

        y_ssd = ssd_mixer(z, xbc, dt_raw, ssd_conv_w[l], ssd_conv_b[l], ssd_dt_bias[l],
                          ssd_a_log[l], ssd_d[l], ssd_norm_g[l])
        heads = lambda t: t.reshape(bsz, length, SB_HEADS, SB_HEAD_DIM)
        y_sb = rmsnorm(stick_breaking_attention(heads(q_sb), heads(k_sb), heads(v_sb)), sb_norm_g[l])
        y_mla = rmsnorm(mla_attention(q_a, c_kv, k_r, mla_q_norm_g[l], mla_kv_norm_g[l],
                                      mla_w_uq[l], mla_w_ukv[l], cos, sin), mla_norm_g[l])

        h = h + jnp.concatenate([y_ssd, y_sb, y_mla], axis=-1) @ w_out[l]
        h = h + conv_ffn(rmsnorm(h, norm_ffn_g[l]), ffn_w_up[l], ffn_conv_w[l], ffn_conv_b[l],
                         ffn_w_down[l])

    return rmsnorm(h, final_norm_g)[:, N_META:]


import jax as _jax
import jax.numpy as _jnp

TWIN_FORMAT = 'train_step'
FWD_PARAMS = ['x', 'meta_tokens', 'norm_mix_g', 'w_in', 'ssd_conv_w', 'ssd_conv_b', 'ssd_dt_bias', 'ssd_a_log', 'ssd_d', 'ssd_norm_g', 'sb_norm_g', 'mla_q_norm_g', 'mla_kv_norm_g', 'mla_w_uq', 'mla_w_ukv', 'mla_norm_g', 'w_out', 'norm_ffn_g', 'ffn_w_up', 'ffn_conv_w', 'ffn_conv_b', 'ffn_w_down', 'final_norm_g']
TWIN_WEIGHTS = ['meta_tokens', 'norm_mix_g', 'w_in', 'ssd_conv_w', 'ssd_conv_b', 'ssd_dt_bias', 'ssd_a_log', 'ssd_d', 'ssd_norm_g', 'sb_norm_g', 'mla_q_norm_g', 'mla_kv_norm_g', 'mla_w_uq', 'mla_w_ukv', 'mla_norm_g', 'w_out', 'norm_ffn_g', 'ffn_w_up', 'ffn_conv_w', 'ffn_conv_b', 'ffn_w_down', 'final_norm_g']
TWIN_DIFF_INPUT = 'x'
TWIN_INPUTS = ['x', 'meta_tokens', 'norm_mix_g', 'w_in', 'ssd_conv_w', 'ssd_conv_b', 'ssd_dt_bias', 'ssd_a_log', 'ssd_d', 'ssd_norm_g', 'sb_norm_g', 'mla_q_norm_g', 'mla_kv_norm_g', 'mla_w_uq', 'mla_w_ukv', 'mla_norm_g', 'w_out', 'norm_ffn_g', 'ffn_w_up', 'ffn_conv_w', 'ffn_conv_b', 'ffn_w_down', 'final_norm_g', 'loss_target', 'm_meta_tokens', 'm_norm_mix_g', 'm_w_in', 'm_ssd_conv_w', 'm_ssd_conv_b', 'm_ssd_dt_bias', 'm_ssd_a_log', 'm_ssd_d', 'm_ssd_norm_g', 'm_sb_norm_g', 'm_mla_q_norm_g', 'm_mla_kv_norm_g', 'm_mla_w_uq', 'm_mla_w_ukv', 'm_mla_norm_g', 'm_w_out', 'm_norm_ffn_g', 'm_ffn_w_up', 'm_ffn_conv_w', 'm_ffn_conv_b', 'm_ffn_w_down', 'm_final_norm_g', 'v_meta_tokens', 'v_norm_mix_g', 'v_w_in', 'v_ssd_conv_w', 'v_ssd_conv_b', 'v_ssd_dt_bias', 'v_ssd_a_log', 'v_ssd_d', 'v_ssd_norm_g', 'v_sb_norm_g', 'v_mla_q_norm_g', 'v_mla_kv_norm_g', 'v_mla_w_uq', 'v_mla_w_ukv', 'v_mla_norm_g', 'v_w_out', 'v_norm_ffn_g', 'v_ffn_w_up', 'v_ffn_conv_w', 'v_ffn_conv_b', 'v_ffn_w_down', 'v_final_norm_g']
TWIN_OUTPUTS = ['loss', 'grad_x', 'grad_meta_tokens', 'grad_norm_mix_g', 'grad_w_in', 'grad_ssd_conv_w', 'grad_ssd_conv_b', 'grad_ssd_dt_bias', 'grad_ssd_a_log', 'grad_ssd_d', 'grad_ssd_norm_g', 'grad_sb_norm_g', 'grad_mla_q_norm_g', 'grad_mla_kv_norm_g', 'grad_mla_w_uq', 'grad_mla_w_ukv', 'grad_mla_norm_g', 'grad_w_out', 'grad_norm_ffn_g', 'grad_ffn_w_up', 'grad_ffn_conv_w', 'grad_ffn_conv_b', 'grad_ffn_w_down', 'grad_final_norm_g', 'delta_meta_tokens', 'delta_norm_mix_g', 'delta_w_in', 'delta_ssd_conv_w', 'delta_ssd_conv_b', 'delta_ssd_dt_bias', 'delta_ssd_a_log', 'delta_ssd_d', 'delta_ssd_norm_g', 'delta_sb_norm_g', 'delta_mla_q_norm_g', 'delta_mla_kv_norm_g', 'delta_mla_w_uq', 'delta_mla_w_ukv', 'delta_mla_norm_g', 'delta_w_out', 'delta_norm_ffn_g', 'delta_ffn_w_up', 'delta_ffn_conv_w', 'delta_ffn_conv_b', 'delta_ffn_w_down', 'delta_final_norm_g', 'new_m_meta_tokens', 'new_m_norm_mix_g', 'new_m_w_in', 'new_m_ssd_conv_w', 'new_m_ssd_conv_b', 'new_m_ssd_dt_bias', 'new_m_ssd_a_log', 'new_m_ssd_d', 'new_m_ssd_norm_g', 'new_m_sb_norm_g', 'new_m_mla_q_norm_g', 'new_m_mla_kv_norm_g', 'new_m_mla_w_uq', 'new_m_mla_w_ukv', 'new_m_mla_norm_g', 'new_m_w_out', 'new_m_norm_ffn_g', 'new_m_ffn_w_up', 'new_m_ffn_conv_w', 'new_m_ffn_conv_b', 'new_m_ffn_w_down', 'new_m_final_norm_g', 'new_v_meta_tokens', 'new_v_norm_mix_g', 'new_v_w_in', 'new_v_ssd_conv_w', 'new_v_ssd_conv_b', 'new_v_ssd_dt_bias', 'new_v_ssd_a_log', 'new_v_ssd_d', 'new_v_ssd_norm_g', 'new_v_sb_norm_g', 'new_v_mla_q_norm_g', 'new_v_mla_kv_norm_g', 'new_v_mla_w_uq', 'new_v_mla_w_ukv', 'new_v_mla_norm_g', 'new_v_w_out', 'new_v_norm_ffn_g', 'new_v_ffn_w_up', 'new_v_ffn_conv_w', 'new_v_ffn_conv_b', 'new_v_ffn_w_down', 'new_v_final_norm_g']
TWIN_LEAF_KINDS = {'loss': 'loss', 'grad_x': 'grad_x', 'grad_meta_tokens': 'grad_w', 'grad_norm_mix_g': 'grad_w', 'grad_w_in': 'grad_w', 'grad_ssd_conv_w': 'grad_w', 'grad_ssd_conv_b': 'grad_w', 'grad_ssd_dt_bias': 'grad_w', 'grad_ssd_a_log': 'grad_w', 'grad_ssd_d': 'grad_w', 'grad_ssd_norm_g': 'grad_w', 'grad_sb_norm_g': 'grad_w', 'grad_mla_q_norm_g': 'grad_w', 'grad_mla_kv_norm_g': 'grad_w', 'grad_mla_w_uq': 'grad_w', 'grad_mla_w_ukv': 'grad_w', 'grad_mla_norm_g': 'grad_w', 'grad_w_out': 'grad_w', 'grad_norm_ffn_g': 'grad_w', 'grad_ffn_w_up': 'grad_w', 'grad_ffn_conv_w': 'grad_w', 'grad_ffn_conv_b': 'grad_w', 'grad_ffn_w_down': 'grad_w', 'grad_final_norm_g': 'grad_w', 'delta_meta_tokens': 'delta_w', 'delta_norm_mix_g': 'delta_w', 'delta_w_in': 'delta_w', 'delta_ssd_conv_w': 'delta_w', 'delta_ssd_conv_b': 'delta_w', 'delta_ssd_dt_bias': 'delta_w', 'delta_ssd_a_log': 'delta_w', 'delta_ssd_d': 'delta_w', 'delta_ssd_norm_g': 'delta_w', 'delta_sb_norm_g': 'delta_w', 'delta_mla_q_norm_g': 'delta_w', 'delta_mla_kv_norm_g': 'delta_w', 'delta_mla_w_uq': 'delta_w', 'delta_mla_w_ukv': 'delta_w', 'delta_mla_norm_g': 'delta_w', 'delta_w_out': 'delta_w', 'delta_norm_ffn_g': 'delta_w', 'delta_ffn_w_up': 'delta_w', 'delta_ffn_conv_w': 'delta_w', 'delta_ffn_conv_b': 'delta_w', 'delta_ffn_w_down': 'delta_w', 'delta_final_norm_g': 'delta_w', 'new_m_meta_tokens': 'new_m', 'new_m_norm_mix_g': 'new_m', 'new_m_w_in': 'new_m', 'new_m_ssd_conv_w': 'new_m', 'new_m_ssd_conv_b': 'new_m', 'new_m_ssd_dt_bias': 'new_m', 'new_m_ssd_a_log': 'new_m', 'new_m_ssd_d': 'new_m', 'new_m_ssd_norm_g': 'new_m', 'new_m_sb_norm_g': 'new_m', 'new_m_mla_q_norm_g': 'new_m', 'new_m_mla_kv_norm_g': 'new_m', 'new_m_mla_w_uq': 'new_m', 'new_m_mla_w_ukv': 'new_m', 'new_m_mla_norm_g': 'new_m', 'new_m_w_out': 'new_m', 'new_m_norm_ffn_g': 'new_m', 'new_m_ffn_w_up': 'new_m', 'new_m_ffn_conv_w': 'new_m', 'new_m_ffn_conv_b': 'new_m', 'new_m_ffn_w_down': 'new_m', 'new_m_final_norm_g': 'new_m', 'new_v_meta_tokens': 'new_v', 'new_v_norm_mix_g': 'new_v', 'new_v_w_in': 'new_v', 'new_v_ssd_conv_w': 'new_v', 'new_v_ssd_conv_b': 'new_v', 'new_v_ssd_dt_bias': 'new_v', 'new_v_ssd_a_log': 'new_v', 'new_v_ssd_d': 'new_v', 'new_v_ssd_norm_g': 'new_v', 'new_v_sb_norm_g': 'new_v', 'new_v_mla_q_norm_g': 'new_v', 'new_v_mla_kv_norm_g': 'new_v', 'new_v_mla_w_uq': 'new_v', 'new_v_mla_w_ukv': 'new_v', 'new_v_mla_norm_g': 'new_v', 'new_v_w_out': 'new_v', 'new_v_norm_ffn_g': 'new_v', 'new_v_ffn_w_up': 'new_v', 'new_v_ffn_conv_w': 'new_v', 'new_v_ffn_conv_b': 'new_v', 'new_v_ffn_w_down': 'new_v', 'new_v_final_norm_g': 'new_v'}


def _forward(args):
    return _fwd_reference(*[args[k] for k in FWD_PARAMS])


def _output_shape():
    out = _jax.eval_shape(lambda: _forward(_fwd_setup_inputs(0)))
    return out.shape, out.dtype

N_MICROBATCH = 1
ADAM_LR = 0.001
ADAM_B1 = 0.9
ADAM_B2 = 0.999
ADAM_EPS = 1e-08
ADAM_WD = 0.01
ADAM_STEP = 10
PER_EXAMPLE_BATCH_AXIS = {'x': 0, 'loss_target': 0}
SHARED_INPUTS = []
_WEIGHT_DTYPES = {'meta_tokens': _jnp.float32, 'norm_mix_g': _jnp.float32, 'w_in': _jnp.float32, 'ssd_conv_w': _jnp.float32, 'ssd_conv_b': _jnp.float32, 'ssd_dt_bias': _jnp.float32, 'ssd_a_log': _jnp.float32, 'ssd_d': _jnp.float32, 'ssd_norm_g': _jnp.float32, 'sb_norm_g': _jnp.float32, 'mla_q_norm_g': _jnp.float32, 'mla_kv_norm_g': _jnp.float32, 'mla_w_uq': _jnp.float32, 'mla_w_ukv': _jnp.float32, 'mla_norm_g': _jnp.float32, 'w_out': _jnp.float32, 'norm_ffn_g': _jnp.float32, 'ffn_w_up': _jnp.float32, 'ffn_conv_w': _jnp.float32, 'ffn_conv_b': _jnp.float32, 'ffn_w_down': _jnp.float32, 'final_norm_g': _jnp.float32}
MOMENT_SCALE = {'meta_tokens': 2.264779e-02, 'norm_mix_g': 2.832435e-01, 'w_in': 1.704639e-01, 'ssd_conv_w': 1.489797e-01, 'ssd_conv_b': 2.575381e-01, 'ssd_dt_bias': 5.488095e-01, 'ssd_a_log': 2.145433e+00, 'ssd_d': 1.442439e+00, 'ssd_norm_g': 2.190159e-01, 'sb_norm_g': 2.091258e-01, 'mla_q_norm_g': 2.553606e-01, 'mla_kv_norm_g': 6.946630e-01, 'mla_w_uq': 1.431729e-01, 'mla_w_ukv': 2.353301e-01, 'mla_norm_g': 3.203113e-01, 'w_out': 2.310739e-01, 'norm_ffn_g': 1.439823e-01, 'ffn_w_up': 6.174419e-02, 'ffn_conv_w': 6.159726e-02, 'ffn_conv_b': 8.962701e-02, 'ffn_w_down': 1.021917e-01, 'final_norm_g': 6.411722e+01}


def _to_microbatches(a, axis):
    t = _jnp.moveaxis(a, axis, 0)
    t = t.reshape((N_MICROBATCH, t.shape[0] // N_MICROBATCH) + t.shape[1:])
    return _jnp.moveaxis(t, 1, axis + 1)


def setup_inputs(seed: int = 0) -> dict:
    inp = _fwd_setup_inputs(seed)
    key = _jax.random.fold_in(_jax.random.key(seed), 7919)
    shape, _ = _output_shape()
    out = dict(inp)
    out["loss_target"] = _jax.random.normal(_jax.random.fold_in(key, 0), shape, _jnp.float32)
    for i, name in enumerate(TWIN_WEIGHTS):
        w = inp[name].astype(_jnp.float32)
        if MOMENT_SCALE is None:
            s = _jnp.sqrt(_jnp.mean(_jnp.square(w)) + 1e-30)
        else:
            s = MOMENT_SCALE[name]
        km, kv = _jax.random.split(_jax.random.fold_in(key, i + 1))
        out[name] = w
        out["m_" + name] = s * _jax.random.normal(km, w.shape, _jnp.float32)
        out["v_" + name] = (s * s) * _jax.random.uniform(kv, w.shape, _jnp.float32, 0.5, 1.5)
    if N_MICROBATCH > 1:
        for name, axis in PER_EXAMPLE_BATCH_AXIS.items():
            out[name] = _to_microbatches(out[name], axis)
    return {'x': out['x'], 'meta_tokens': out['meta_tokens'], 'norm_mix_g': out['norm_mix_g'], 'w_in': out['w_in'], 'ssd_conv_w': out['ssd_conv_w'], 'ssd_conv_b': out['ssd_conv_b'], 'ssd_dt_bias': out['ssd_dt_bias'], 'ssd_a_log': out['ssd_a_log'], 'ssd_d': out['ssd_d'], 'ssd_norm_g': out['ssd_norm_g'], 'sb_norm_g': out['sb_norm_g'], 'mla_q_norm_g': out['mla_q_norm_g'], 'mla_kv_norm_g': out['mla_kv_norm_g'], 'mla_w_uq': out['mla_w_uq'], 'mla_w_ukv': out['mla_w_ukv'], 'mla_norm_g': out['mla_norm_g'], 'w_out': out['w_out'], 'norm_ffn_g': out['norm_ffn_g'], 'ffn_w_up': out['ffn_w_up'], 'ffn_conv_w': out['ffn_conv_w'], 'ffn_conv_b': out['ffn_conv_b'], 'ffn_w_down': out['ffn_w_down'], 'final_norm_g': out['final_norm_g'], 'loss_target': out['loss_target'], 'm_meta_tokens': out['m_meta_tokens'], 'm_norm_mix_g': out['m_norm_mix_g'], 'm_w_in': out['m_w_in'], 'm_ssd_conv_w': out['m_ssd_conv_w'], 'm_ssd_conv_b': out['m_ssd_conv_b'], 'm_ssd_dt_bias': out['m_ssd_dt_bias'], 'm_ssd_a_log': out['m_ssd_a_log'], 'm_ssd_d': out['m_ssd_d'], 'm_ssd_norm_g': out['m_ssd_norm_g'], 'm_sb_norm_g': out['m_sb_norm_g'], 'm_mla_q_norm_g': out['m_mla_q_norm_g'], 'm_mla_kv_norm_g': out['m_mla_kv_norm_g'], 'm_mla_w_uq': out['m_mla_w_uq'], 'm_mla_w_ukv': out['m_mla_w_ukv'], 'm_mla_norm_g': out['m_mla_norm_g'], 'm_w_out': out['m_w_out'], 'm_norm_ffn_g': out['m_norm_ffn_g'], 'm_ffn_w_up': out['m_ffn_w_up'], 'm_ffn_conv_w': out['m_ffn_conv_w'], 'm_ffn_conv_b': out['m_ffn_conv_b'], 'm_ffn_w_down': out['m_ffn_w_down'], 'm_final_norm_g': out['m_final_norm_g'], 'v_meta_tokens': out['v_meta_tokens'], 'v_norm_mix_g': out['v_norm_mix_g'], 'v_w_in': out['v_w_in'], 'v_ssd_conv_w': out['v_ssd_conv_w'], 'v_ssd_conv_b': out['v_ssd_conv_b'], 'v_ssd_dt_bias': out['v_ssd_dt_bias'], 'v_ssd_a_log': out['v_ssd_a_log'], 'v_ssd_d': out['v_ssd_d'], 'v_ssd_norm_g': out['v_ssd_norm_g'], 'v_sb_norm_g': out['v_sb_norm_g'], 'v_mla_q_norm_g': out['v_mla_q_norm_g'], 'v_mla_kv_norm_g': out['v_mla_kv_norm_g'], 'v_mla_w_uq': out['v_mla_w_uq'], 'v_mla_w_ukv': out['v_mla_w_ukv'], 'v_mla_norm_g': out['v_mla_norm_g'], 'v_w_out': out['v_w_out'], 'v_norm_ffn_g': out['v_norm_ffn_g'], 'v_ffn_w_up': out['v_ffn_w_up'], 'v_ffn_conv_w': out['v_ffn_conv_w'], 'v_ffn_conv_b': out['v_ffn_conv_b'], 'v_ffn_w_down': out['v_ffn_w_down'], 'v_final_norm_g': out['v_final_norm_g']}


def _loss(weights, diff, rest, loss_target):
    with _jax.named_scope("forward"):
        args = {**rest, TWIN_DIFF_INPUT: diff, **{k: w.astype(_WEIGHT_DTYPES[k]) for k, w in weights.items()}}
        y = _forward(args)
    with _jax.named_scope("loss_head"):
        err = _jnp.square(y.astype(_jnp.float32) - loss_target)
        return 0.5 * _jnp.sum(_jnp.mean(err, axis=-1)) if err.ndim else 0.5 * err


def _adamw(w, g, m, v):
    m = ADAM_B1 * m + (1.0 - ADAM_B1) * g
    v = ADAM_B2 * v + (1.0 - ADAM_B2) * _jnp.square(g)
    m_hat = m / (1.0 - ADAM_B1 ** ADAM_STEP)
    v_hat = v / (1.0 - ADAM_B2 ** ADAM_STEP)
    delta = -ADAM_LR * (m_hat / (_jnp.sqrt(v_hat) + ADAM_EPS) + ADAM_WD * w)
    return delta, m, v


def reference(x, meta_tokens, norm_mix_g, w_in, ssd_conv_w, ssd_conv_b, ssd_dt_bias, ssd_a_log, ssd_d, ssd_norm_g, sb_norm_g, mla_q_norm_g, mla_kv_norm_g, mla_w_uq, mla_w_ukv, mla_norm_g, w_out, norm_ffn_g, ffn_w_up, ffn_conv_w, ffn_conv_b, ffn_w_down, final_norm_g, loss_target, m_meta_tokens, m_norm_mix_g, m_w_in, m_ssd_conv_w, m_ssd_conv_b, m_ssd_dt_bias, m_ssd_a_log, m_ssd_d, m_ssd_norm_g, m_sb_norm_g, m_mla_q_norm_g, m_mla_kv_norm_g, m_mla_w_uq, m_mla_w_ukv, m_mla_norm_g, m_w_out, m_norm_ffn_g, m_ffn_w_up, m_ffn_conv_w, m_ffn_conv_b, m_ffn_w_down, m_final_norm_g, v_meta_tokens, v_norm_mix_g, v_w_in, v_ssd_conv_w, v_ssd_conv_b, v_ssd_dt_bias, v_ssd_a_log, v_ssd_d, v_ssd_norm_g, v_sb_norm_g, v_mla_q_norm_g, v_mla_kv_norm_g, v_mla_w_uq, v_mla_w_ukv, v_mla_norm_g, v_w_out, v_norm_ffn_g, v_ffn_w_up, v_ffn_conv_w, v_ffn_conv_b, v_ffn_w_down, v_final_norm_g):
    given = dict(x=x, meta_tokens=meta_tokens, norm_mix_g=norm_mix_g, w_in=w_in, ssd_conv_w=ssd_conv_w, ssd_conv_b=ssd_conv_b, ssd_dt_bias=ssd_dt_bias, ssd_a_log=ssd_a_log, ssd_d=ssd_d, ssd_norm_g=ssd_norm_g, sb_norm_g=sb_norm_g, mla_q_norm_g=mla_q_norm_g, mla_kv_norm_g=mla_kv_norm_g, mla_w_uq=mla_w_uq, mla_w_ukv=mla_w_ukv, mla_norm_g=mla_norm_g, w_out=w_out, norm_ffn_g=norm_ffn_g, ffn_w_up=ffn_w_up, ffn_conv_w=ffn_conv_w, ffn_conv_b=ffn_conv_b, ffn_w_down=ffn_w_down, final_norm_g=final_norm_g, loss_target=loss_target, m_meta_tokens=m_meta_tokens, m_norm_mix_g=m_norm_mix_g, m_w_in=m_w_in, m_ssd_conv_w=m_ssd_conv_w, m_ssd_conv_b=m_ssd_conv_b, m_ssd_dt_bias=m_ssd_dt_bias, m_ssd_a_log=m_ssd_a_log, m_ssd_d=m_ssd_d, m_ssd_norm_g=m_ssd_norm_g, m_sb_norm_g=m_sb_norm_g, m_mla_q_norm_g=m_mla_q_norm_g, m_mla_kv_norm_g=m_mla_kv_norm_g, m_mla_w_uq=m_mla_w_uq, m_mla_w_ukv=m_mla_w_ukv, m_mla_norm_g=m_mla_norm_g, m_w_out=m_w_out, m_norm_ffn_g=m_norm_ffn_g, m_ffn_w_up=m_ffn_w_up, m_ffn_conv_w=m_ffn_conv_w, m_ffn_conv_b=m_ffn_conv_b, m_ffn_w_down=m_ffn_w_down, m_final_norm_g=m_final_norm_g, v_meta_tokens=v_meta_tokens, v_norm_mix_g=v_norm_mix_g, v_w_in=v_w_in, v_ssd_conv_w=v_ssd_conv_w, v_ssd_conv_b=v_ssd_conv_b, v_ssd_dt_bias=v_ssd_dt_bias, v_ssd_a_log=v_ssd_a_log, v_ssd_d=v_ssd_d, v_ssd_norm_g=v_ssd_norm_g, v_sb_norm_g=v_sb_norm_g, v_mla_q_norm_g=v_mla_q_norm_g, v_mla_kv_norm_g=v_mla_kv_norm_g, v_mla_w_uq=v_mla_w_uq, v_mla_w_ukv=v_mla_w_ukv, v_mla_norm_g=v_mla_norm_g, v_w_out=v_w_out, v_norm_ffn_g=v_norm_ffn_g, v_ffn_w_up=v_ffn_w_up, v_ffn_conv_w=v_ffn_conv_w, v_ffn_conv_b=v_ffn_conv_b, v_ffn_w_down=v_ffn_w_down, v_final_norm_g=v_final_norm_g)
    weights = {n: given[n] for n in TWIN_WEIGHTS}
    shared = {n: given[n] for n in SHARED_INPUTS}
    per_example = {n: given[n] for n in ['x']}
    grad_fn = _jax.value_and_grad(_loss, argnums=(0, 1))

    def one_microbatch(ex, loss_target):
        ex = dict(ex)
        diff = ex.pop(TWIN_DIFF_INPUT)
        return grad_fn(weights, diff, {**shared, **ex}, loss_target)

    if N_MICROBATCH == 1:
        loss, (grad_w, grad_x) = one_microbatch(per_example, given["loss_target"])
    else:
        def body(carry, xs):
            loss_sum, grad_sum = carry
            l_k, (gw_k, gx_k) = one_microbatch(xs[0], xs[1])
            with _jax.named_scope("update"):
                return (loss_sum + l_k, _jax.tree.map(_jnp.add, grad_sum, gw_k)), gx_k

        init = (_jnp.zeros((), _jnp.float32), _jax.tree.map(_jnp.zeros_like, weights))
        (loss, grad_w), grad_x = _jax.lax.scan(body, init, (per_example, given["loss_target"]))
    with _jax.named_scope("update"):
        delta_w, new_m, new_v = {}, {}, {}
        for n in TWIN_WEIGHTS:
            delta_w[n], new_m[n], new_v[n] = _adamw(weights[n], grad_w[n], given["m_" + n], given["v_" + n])
    return (loss, grad_x, *[grad_w[n] for n in TWIN_WEIGHTS], *[delta_w[n] for n in TWIN_WEIGHTS],
            *[new_m[n] for n in TWIN_WEIGHTS], *[new_v[n] for n in TWIN_WEIGHTS])
```

```python
import functools
import math

import numpy as np
import jax
import jax.numpy as jnp
from jax import lax
from jax.experimental import pallas as pl
from jax.experimental.pallas import tpu as pltpu

F32 = jnp.float32
BF16 = jnp.bfloat16
HIGHEST = lax.Precision.HIGHEST
MESH_ID = pl.DeviceIdType.MESH

D_MODEL = 1024
DEPTH = 2
N_META = 16
EPS = 1e-6
SSD_HEADS = 8
SSD_WIDTH = 512
SSD_XBC = 1024
SSD_CONV = 4
SB_WIDTH = 256
SB_SCALE = 64 ** -0.5
MLA_Q_RANK = 192
MLA_KV_RANK = 128
MLA_ROPE = 32
MLA_SCALE = 96 ** -0.5
ROPE_BASE = 10000.0
D_FF = 2816
FFN_CONV = 3
IN_COLS = 2664
N_CHIPS = 4

ADAM_LR = 0.001
ADAM_B1 = 0.9
ADAM_B2 = 0.999
ADAM_EPS = 1e-08
ADAM_WD = 0.01
ADAM_STEP = 10

LANES = 128
SUBLANES = 8
ROW_TILE = 256
VMEM_LIMIT = 56 * 1024 * 1024
PACK_W = 1024

U_XBC, U_Z, U_QA, U_CKV, U_KR4, U_DT, U_MAIN = 0, 1024, 1536, 1792, 1920, 2048, 2304
NEG = -1e30


def _cp(*sem):
    return pltpu.CompilerParams(dimension_semantics=sem if sem else None, vmem_limit_bytes=VMEM_LIMIT)


def _pick(dim, pref):
    if dim <= pref:
        return dim
    best = None
    for t in range(LANES, pref + 1, LANES):
        if dim % t == 0:
            best = t
    assert best is not None, (dim, pref)
    return best


def _dot(a, b, dims="nn", precision=None):
    dn = {"nn": (((1,), (0,)), ((), ())), "nt": (((1,), (1,)), ((), ())), "tn": (((0,), (0,)), ((), ()))}[dims]
    return lax.dot_general(a, b, dn, preferred_element_type=F32, precision=precision)


def _softplus(x):
    return jnp.maximum(x, 0.0) + jnp.log1p(jnp.exp(-jnp.abs(x)))


def _silu(x):
    return x * jax.nn.sigmoid(x)


def _rms(x, g, n=None):
    n = x.shape[-1] if n is None else n
    ms = jnp.sum(x * x, axis=-1, keepdims=True) * (1.0 / n)
    return x * lax.rsqrt(ms + EPS) * g


def mm(a, b, dims, out_dtype, name, add=None, tm=512, tn=1024, tk=1536):
    if dims == "nn":
        (m, k), (k2, n) = a.shape, b.shape
    elif dims == "nt":
        (m, k), (n, k2) = a.shape, b.shape
    else:
        (k, m), (k2, n) = a.shape, b.shape
    assert k == k2, (a.shape, b.shape, dims)
    tm, tn, tk = _pick(m, tm), _pick(n, tn), _pick(k, tk)
    nk = k // tk
    if dims == "tn":
        a_spec = pl.BlockSpec((tk, tm), lambda i, j, kk: (kk, i))
    else:
        a_spec = pl.BlockSpec((tm, tk), lambda i, j, kk: (i, kk))
    if dims == "nt":
        b_spec = pl.BlockSpec((tn, tk), lambda i, j, kk: (j, kk))
    else:
        b_spec = pl.BlockSpec((tk, tn), lambda i, j, kk: (kk, j))
    o_spec = pl.BlockSpec((tm, tn), lambda i, j, kk: (i, j))
    has_add = add is not None

    def body(*refs):
        if has_add:
            a_ref, b_ref, add_ref, o_ref, acc_ref = refs
        else:
            a_ref, b_ref, o_ref, acc_ref = refs
        kk = pl.program_id(2)
        part = _dot(a_ref[...].astype(BF16), b_ref[...].astype(BF16), dims)

        @pl.when(kk == 0)
        def _():
            acc_ref[...] = part

        @pl.when(kk > 0)
        def _():
            acc_ref[...] += part

        @pl.when(kk == nk - 1)
        def _():
            r = acc_ref[...]
            if has_add:
                r = r + add_ref[...].astype(F32)
            o_ref[...] = r.astype(o_ref.dtype)

    in_specs = [a_spec, b_spec] + ([o_spec] if has_add else [])
    args = (a, b) + ((add,) if has_add else ())
    return pl.pallas_call(
        body, name=name, grid=(m // tm, n // tn, nk),
        in_specs=in_specs, out_specs=o_spec,
        out_shape=jax.ShapeDtypeStruct((m, n), out_dtype),
        scratch_shapes=[pltpu.VMEM((tm, tn), F32)],
        compiler_params=_cp("parallel", "parallel", "arbitrary"),
    )(*args)


def RI(arr, width=None, cidx=0, cv=False, rblk=0):
    return ("row" if rblk == 0 else ("row", rblk), arr, arr.shape[1] if width is None else width, cidx, cv)


def HP(arr, width=None, cidx=0, cv=False):
    return ("prev", arr, arr.shape[1] if width is None else width, cidx, cv)


def HN(arr, width=None, cidx=0, cv=False):
    return ("next", arr, arr.shape[1] if width is None else width, cidx, cv)


def PA(arr, width=None, cidx=0, cv=False):
    return ("par", arr, arr.shape[1] if width is None else width, cidx, cv)


def RO(ncols, dtype, width=None, cv=False):
    return ("row", ncols, dtype, ncols if width is None else width, cv)


def AO(nrows, ncols, width=None, cv=False):
    return ("acc", (nrows, ncols), F32, ncols if width is None else width, cv)


def rowwise(fn, ins, outs, name, rows, tm=ROW_TILE, ncol=1):
    tm = min(tm, rows)
    assert rows % tm == 0
    nrow = rows // tm
    hb = tm // SUBLANES
    last_hb = rows // SUBLANES - 1
    in_specs, args = [], []
    for kind, arr, width, cidx, cv in ins:
        def cmap(j, cidx=cidx, cv=cv):
            return cidx + j if cv else cidx
        if kind == "row":
            spec = pl.BlockSpec((tm, width), lambda j, i, cmap=cmap: (i, cmap(j)))
        elif isinstance(kind, tuple):
            spec = pl.BlockSpec((tm, width), lambda j, i, cmap=cmap, rblk=kind[1]: (i + rblk, cmap(j)))
        elif kind == "prev":
            spec = pl.BlockSpec((SUBLANES, width), lambda j, i, cmap=cmap: (jnp.maximum(i * hb - 1, 0), cmap(j)))
        elif kind == "next":
            spec = pl.BlockSpec((SUBLANES, width), lambda j, i, cmap=cmap: (jnp.minimum((i + 1) * hb, last_hb), cmap(j)))
        else:
            spec = pl.BlockSpec((arr.shape[0], width), lambda j, i, cmap=cmap: (0, cmap(j)))
        in_specs.append(spec)
        args.append(arr)
    out_specs, out_shapes, acc_cv = [], [], []
    for kind, shp, dtype, width, cv in outs:
        if kind == "row":
            out_specs.append(pl.BlockSpec((tm, width), lambda j, i, cv=cv: (i, j if cv else 0)))
            out_shapes.append(jax.ShapeDtypeStruct((rows, shp), dtype))
            acc_cv.append(None)
        else:
            out_specs.append(pl.BlockSpec((shp[0], width), lambda j, i, cv=cv: (0, j if cv else 0)))
            out_shapes.append(jax.ShapeDtypeStruct(shp, dtype))
            acc_cv.append(cv)
    n_in = len(ins)

    def body(*refs):
        j = pl.program_id(0)
        i = pl.program_id(1)
        vals = fn(i, nrow, *[r[...] for r in refs[:n_in]])
        if not isinstance(vals, (tuple, list)):
            vals = (vals,)
        for o_ref, v, cv in zip(refs[n_in:], vals, acc_cv):
            if cv is None:
                o_ref[...] = v.astype(o_ref.dtype)
            else:
                first = (i == 0) if cv else jnp.logical_and(i == 0, j == 0)

                @pl.when(first)
                def _(o_ref=o_ref, v=v):
                    o_ref[...] = v.astype(o_ref.dtype)

                @pl.when(jnp.logical_not(first))
                def _(o_ref=o_ref, v=v):
                    o_ref[...] += v.astype(o_ref.dtype)

    res = pl.pallas_call(
        body, name=name, grid=(ncol, nrow), in_specs=in_specs, out_specs=out_specs, out_shape=out_shapes,
        compiler_params=_cp("arbitrary", "arbitrary"),
    )(*args)
    return res


def _rows_iota(x):
    return lax.broadcasted_iota(jnp.int32, x.shape, 0)


def shift_down(x, halo, s):
    if s == 0:
        return x
    tm = x.shape[0]
    top = pltpu.roll(halo, s, 0)
    if tm > SUBLANES:
        top = jnp.concatenate([top, jnp.zeros((tm - SUBLANES, x.shape[1]), x.dtype)], axis=0)
    return jnp.where(_rows_iota(x) < s, top, pltpu.roll(x, s, 0))


def shift_up(x, halo, s):
    if s == 0:
        return x
    tm = x.shape[0]
    bot = pltpu.roll(halo, SUBLANES - s, 0)
    if tm > SUBLANES:
        bot = jnp.concatenate([jnp.zeros((tm - SUBLANES, x.shape[1]), x.dtype), bot], axis=0)
    return jnp.where(_rows_iota(x) >= tm - s, bot, pltpu.roll(x, tm - s, 0))


def conv_fwd(x, halo, w, i):
    kw = w.shape[0]
    halo = jnp.where(i == 0, 0.0, halo)
    out = None
    for k in range(kw):
        term = w[k:k + 1, :] * shift_down(x, halo, kw - 1 - k)
        out = term if out is None else out + term
    return out


def conv_bwd_data(dy, halo_next, w, i, n):
    kw = w.shape[0]
    halo_next = jnp.where(i == n - 1, 0.0, halo_next)
    out = None
    for k in range(kw):
        term = w[k:k + 1, :] * shift_up(dy, halo_next, kw - 1 - k)
        out = term if out is None else out + term
    return out


def conv_bwd_w(dy, x, halo, i, kw):
    halo = jnp.where(i == 0, 0.0, halo)
    rows = [jnp.sum(dy * shift_down(x, halo, kw - 1 - k), axis=0, keepdims=True) for k in range(kw)]
    return jnp.concatenate(rows, axis=0)


def _lane(shape):
    return lax.broadcasted_iota(jnp.int32, shape, 1)


def rope_rot(x):
    lane = _lane(x.shape) % MLA_ROPE
    return jnp.where(lane < MLA_ROPE // 2, -pltpu.roll(x, LANES - MLA_ROPE // 2, 1), pltpu.roll(x, MLA_ROPE // 2, 1))


def rope_rot_t(g):
    lane = _lane(g.shape) % MLA_ROPE
    return jnp.where(lane < MLA_ROPE // 2, pltpu.roll(g, LANES - MLA_ROPE // 2, 1), -pltpu.roll(g, MLA_ROPE // 2, 1))


def _ssd_common(p, xs, dt_raw, bias, alog, q):
    lane = _lane((q, LANES))
    pre = dt_raw + bias
    dt = jnp.where(lane < SSD_HEADS, _softplus(pre), 0.0)
    a_row = -jnp.exp(alog)
    d_a = dt * a_row
    ri = lax.broadcasted_iota(jnp.int32, (q, q), 0)
    ci = lax.broadcasted_iota(jnp.int32, (q, q), 1)
    causal = ri >= ci
    acs = _dot(causal.astype(F32), d_a, "nn", HIGHEST)
    acs_t = acs.T
    subl = lax.broadcasted_iota(jnp.int32, (LANES, q), 0)
    h0, h1 = 2 * p, 2 * p + 1

    def col(arr, h):
        return jnp.sum(jnp.where(lane == h, arr, 0.0), axis=1, keepdims=True)

    def row(arr_t, h):
        return jnp.sum(jnp.where(subl == h, arr_t, 0.0), axis=0, keepdims=True)

    lo = lane < 64
    cols = (col(acs, h0), col(acs, h1))
    rows = (row(acs_t, h0), row(acs_t, h1))
    acs_p = jnp.where(lo, cols[0], cols[1])
    dt_p = jnp.where(lo, col(dt, h0), col(dt, h1))
    tots = (cols[0][q - 1:q, :], cols[1][q - 1:q, :])
    tot_p = jnp.where(lo[0:1, :], tots[0], tots[1])
    lms = tuple(jnp.exp(jnp.where(causal, cols[j] - rows[j], NEG)) for j in range(2))
    return dict(lane=lane, lo=lo, pre=pre, dt=dt, a_row=a_row, h=(h0, h1), acs_p=acs_p, dt_p=dt_p, tots=tots,
                tot_p=tot_p, lms=lms, ri=ri, ci=ci, eacs=jnp.exp(acs_p), dte=jnp.exp(tot_p - acs_p), x=xs * dt_p)


def _pick_lane(row_arr, h):
    return jnp.sum(jnp.where(_lane(row_arr.shape) == h, row_arr, 0.0), axis=1, keepdims=True)


def ssd_fwd(xbc_c, u_main, bias_row, alog_row, d_row, name):
    lp = xbc_c.shape[0]
    q = min(ROW_TILE, lp)
    nc = lp // q
    dt_blk = U_DT // LANES

    def body(xs_ref, b_ref, c_ref, dt_ref, bias_ref, alog_ref, d_ref, y_ref, hp_ref, h_scr):
        p = pl.program_id(0)
        c = pl.program_id(1)

        @pl.when(c == 0)
        def _():
            h_scr[...] = jnp.zeros_like(h_scr)

        xs = xs_ref[...]
        bb = b_ref[...].astype(BF16)
        cb_ = c_ref[...].astype(BF16)
        s = _ssd_common(p, xs, dt_ref[...], bias_ref[...], alog_ref[...], q)
        lo = s["lo"]
        g = _dot(cb_, bb, "nt")
        y = jnp.zeros((q, LANES), F32)
        for j in range(2):
            m = (g * s["lms"][j]).astype(BF16)
            xj = jnp.where(lo if j == 0 else jnp.logical_not(lo), s["x"], 0.0).astype(BF16)
            y = y + _dot(m, xj)
        hp = h_scr[...]
        hp_ref[...] = hp
        y = y + _dot(cb_, hp.astype(BF16), "nt") * s["eacs"]
        d_p = jnp.where(lo[0:1, :], _pick_lane(d_ref[...], s["h"][0]), _pick_lane(d_ref[...], s["h"][1]))
        y_ref[...] = y + d_p * xs
        sub_lo = lax.broadcasted_iota(jnp.int32, (LANES, LANES), 0) < 64
        etot = jnp.where(sub_lo, jnp.exp(s["tots"][0]), jnp.exp(s["tots"][1]))
        h_scr[...] = hp * etot + _dot((s["x"] * s["dte"]).astype(BF16), bb, "tn")

    grp = lambda p: p // 2
    in_specs = [
        pl.BlockSpec((q, LANES), lambda p, c: (c, p)),
        pl.BlockSpec((q, LANES), lambda p, c: (c, 4 + grp(p))),
        pl.BlockSpec((q, LANES), lambda p, c: (c, 6 + grp(p))),
        pl.BlockSpec((q, LANES), lambda p, c: (c, dt_blk)),
        pl.BlockSpec((1, LANES), lambda p, c: (0, 0)),
        pl.BlockSpec((1, LANES), lambda p, c: (0, 0)),
        pl.BlockSpec((1, LANES), lambda p, c: (0, 0)),
    ]
    out_specs = [
        pl.BlockSpec((q, LANES), lambda p, c: (c, p)),
        pl.BlockSpec((None, None, LANES, LANES), lambda p, c: (p, c, 0, 0)),
    ]
    return pl.pallas_call(
        body, name=name, grid=(4, nc), in_specs=in_specs, out_specs=out_specs,
        out_shape=[jax.ShapeDtypeStruct((lp, SSD_WIDTH), F32), jax.ShapeDtypeStruct((4, nc, LANES, LANES), F32)],
        scratch_shapes=[pltpu.VMEM((LANES, LANES), F32)],
        compiler_params=_cp("arbitrary", "arbitrary"),
    )(xbc_c, xbc_c, xbc_c, u_main, bias_row, alog_row, d_row)


def ssd_bwd(xbc_c, u_main, bias_row, alog_row, d_row, hprev, dy, name):
    lp = xbc_c.shape[0]
    q = min(ROW_TILE, lp)
    nc = lp // q
    dt_blk = U_DT // LANES

    def body(xs_ref, b_ref, c_ref, dt_ref, bias_ref, alog_ref, d_ref, hp_ref, dy_ref,
             dxs_ref, db_ref, dc_ref, ddt_ref, pg_ref, dh_scr):
        p = pl.program_id(0)
        cc = pl.program_id(1)

        @pl.when(cc == 0)
        def _():
            dh_scr[...] = jnp.zeros_like(dh_scr)
            pg_ref[...] = jnp.zeros_like(pg_ref)

        xs = xs_ref[...]
        bb = b_ref[...].astype(BF16)
        cb_ = c_ref[...].astype(BF16)
        s = _ssd_common(p, xs, dt_ref[...], bias_ref[...], alog_ref[...], q)
        lane, lo, x = s["lane"], s["lo"], s["x"]
        h0, h1 = s["h"]
        d_y = dy_ref[...]
        hp = hp_ref[...]
        hpb = hp.astype(BF16)
        dhn = dh_scr[...]
        dhnb = dhn.astype(BF16)
        xd = x * s["dte"]
        g = _dot(cb_, bb, "nt")
        dxdiag = jnp.zeros((q, LANES), F32)
        dg = jnp.zeros((q, q), F32)
        row_part, col_part = [], []
        for j in range(2):
            mj = lo if j == 0 else jnp.logical_not(lo)
            lm = s["lms"][j]
            m32 = g * lm
            xj = jnp.where(mj, x, 0.0).astype(BF16)
            dyj = jnp.where(mj, d_y, 0.0).astype(BF16)
            dxdiag = dxdiag + _dot(m32.astype(BF16), dyj, "tn")
            dm = _dot(dyj, xj, "nt")
            dg = dg + dm * lm
            wm = dm * m32
            row_part.append(jnp.sum(wm, axis=1, keepdims=True))
            col_part.append(jnp.sum(wm, axis=0, keepdims=True))
        dgb = dg.astype(BF16)
        d_c = _dot(dgb, bb)
        d_b = _dot(dgb, cb_, "tn")
        yoff = _dot(cb_, hpb, "nt") * s["eacs"]
        d_t = (d_y * s["eacs"]).astype(BF16)
        d_c = d_c + _dot(d_t, hpb)
        d_hp = _dot(d_t, cb_, "tn")
        dxd = _dot(bb, dhnb, "nt")
        d_b = d_b + _dot(xd.astype(BF16), dhnb)
        d_x = dxdiag + dxd * s["dte"]
        r = dxd * xd
        a_terms = d_y * yoff - r

        def gsum(arr):
            return (jnp.sum(jnp.where(lo, arr, 0.0), axis=1, keepdims=True),
                    jnp.sum(jnp.where(lo, 0.0, arr), axis=1, keepdims=True))

        dacs = gsum(a_terms)
        rs = gsum(r)
        hh = dhn * hp
        sub_lo = lax.broadcasted_iota(jnp.int32, (LANES, LANES), 0) < 64
        hsum = (jnp.sum(jnp.where(sub_lo, hh, 0.0), keepdims=True), jnp.sum(jnp.where(sub_lo, 0.0, hh), keepdims=True))
        last = lax.broadcasted_iota(jnp.int32, (q, 1), 0) == q - 1
        etots = (jnp.exp(s["tots"][0]), jnp.exp(s["tots"][1]))
        ddacs = jnp.zeros((q, LANES), F32)
        for j, h in enumerate((h0, h1)):
            dtot = jnp.sum(rs[j], keepdims=True) + hsum[j] * etots[j]
            dj = dacs[j] + row_part[j] + jnp.where(last, dtot, 0.0)
            ddacs = ddacs + jnp.where(lane == h, dj, 0.0)
        subl = lax.broadcasted_iota(jnp.int32, (LANES, q), 0)
        cols_t = jnp.where(subl == h0, col_part[0], 0.0) + jnp.where(subl == h1, col_part[1], 0.0)
        ddacs = ddacs - cols_t.T
        anti = (s["ri"] <= s["ci"]).astype(F32)
        da = _dot(anti, ddacs, "nn", HIGHEST)
        ddt_own = gsum(d_x * xs)
        ddt = jnp.where(lane == h0, ddt_own[0], 0.0) + jnp.where(lane == h1, ddt_own[1], 0.0) + da * s["a_row"]
        draw = ddt * jax.nn.sigmoid(s["pre"])
        ddt_ref[...] = draw
        d_p = jnp.where(lo[0:1, :], _pick_lane(d_ref[...], h0), _pick_lane(d_ref[...], h1))
        dxs_ref[...] = d_p * d_y + d_x * s["dt_p"]
        db_ref[...] = d_b
        dc_ref[...] = d_c
        dds = gsum(d_y * xs)
        lane1 = lane[0:1, :]
        dd_row = (jnp.where(lane1 == h0, jnp.sum(dds[0], keepdims=True), 0.0)
                  + jnp.where(lane1 == h1, jnp.sum(dds[1], keepdims=True), 0.0))
        dbias_row = jnp.sum(draw, axis=0, keepdims=True)
        dalog_row = jnp.sum(da * s["dt"], axis=0, keepdims=True) * s["a_row"]
        sub8 = lax.broadcasted_iota(jnp.int32, (SUBLANES, LANES), 0)
        pg_ref[...] += (jnp.where(sub8 == 0, dbias_row, 0.0) + jnp.where(sub8 == 1, dalog_row, 0.0)
                        + jnp.where(sub8 == 2, dd_row, 0.0))
        etot = jnp.where(sub_lo, etots[0], etots[1])
        dh_scr[...] = d_hp + etot * dhn

    grp = lambda p: p // 2
    rc = lambda c: nc - 1 - c
    in_specs = [
        pl.BlockSpec((q, LANES), lambda p, c: (rc(c), p)),
        pl.BlockSpec((q, LANES), lambda p, c: (rc(c), 4 + grp(p))),
        pl.BlockSpec((q, LANES), lambda p, c: (rc(c), 6 + grp(p))),
        pl.BlockSpec((q, LANES), lambda p, c: (rc(c), dt_blk)),
        pl.BlockSpec((1, LANES), lambda p, c: (0, 0)),
        pl.BlockSpec((1, LANES), lambda p, c: (0, 0)),
        pl.BlockSpec((1, LANES), lambda p, c: (0, 0)),
        pl.BlockSpec((None, None, LANES, LANES), lambda p, c: (p, rc(c), 0, 0)),
        pl.BlockSpec((q, LANES), lambda p, c: (rc(c), p)),
    ]
    out_specs = [
        pl.BlockSpec((q, LANES), lambda p, c: (rc(c), p)),
        pl.BlockSpec((q, LANES), lambda p, c: (rc(c), p)),
        pl.BlockSpec((q, LANES), lambda p, c: (rc(c), p)),
        pl.BlockSpec((q, LANES), lambda p, c: (rc(c), p)),
        pl.BlockSpec((SUBLANES, LANES), lambda p, c: (p, 0)),
    ]
    wide = jax.ShapeDtypeStruct((lp, 4 * LANES), F32)
    return pl.pallas_call(
        body, name=name, grid=(4, nc), in_specs=in_specs, out_specs=out_specs,
        out_shape=[wide, wide, wide, wide, jax.ShapeDtypeStruct((4 * SUBLANES, LANES), F32)],
        scratch_shapes=[pltpu.VMEM((LANES, LANES), F32)],
        compiler_params=_cp("arbitrary", "arbitrary"),
    )(xbc_c, xbc_c, xbc_c, u_main, bias_row, alog_row, d_row, hprev, dy)


def _split_bf16(x):
    hi = x.astype(BF16)
    return hi, (x - hi.astype(F32)).astype(BF16)


def _tri_sum(x, tri):
    hi, lo = _split_bf16(x)
    return _dot(hi, tri) + _dot(lo, tri)


def _sb_block(qj, k, r_run, masked, bq):
    z = _dot(qj, k, "nt") * SB_SCALE
    u = -_softplus(z)
    sig = jnp.exp(z + u)
    ri = lax.broadcasted_iota(jnp.int32, (bq, bq), 0)
    ci = lax.broadcasted_iota(jnp.int32, (bq, bq), 1)
    if masked:
        mask = ci < ri
        u = jnp.where(mask, u, 0.0)
    after = _tri_sum(u, (ri > ci).astype(BF16)) + r_run
    w = sig * jnp.exp(after)
    if masked:
        w = jnp.where(mask, w, 0.0)
    return z, u, sig, w


def sb_attn_fwd(qkv, name):
    lp = qkv.shape[0]
    bq = min(ROW_TILE, lp)
    nq = lp // bq
    assert nq <= 64

    def body(q_ref, k_ref, v_ref, o_ref, rs_ref):
        qi = pl.program_id(1)
        q = q_ref[...]
        lane = _lane((bq, LANES))
        lo = lane < 64
        qs = (jnp.where(lo, q, jnp.zeros_like(q)), jnp.where(lo, jnp.zeros_like(q), q))

        def step(kb, carry, masked):
            off = pl.multiple_of(kb * bq, bq)
            k = k_ref[pl.ds(off, bq), :]
            v = v_ref[pl.ds(off, bq), :]
            heads, rs = carry
            out = []
            for j in range(2):
                acc, r_run = heads[j]
                rs = jnp.where(lane == 64 * j + kb, r_run, rs)
                _, u, _, w = _sb_block(qs[j], k, r_run, masked, bq)
                out.append((acc + _dot(w.astype(BF16), v), r_run + jnp.sum(u, axis=1, keepdims=True)))
            return tuple(out), rs

        zero = (jnp.zeros((bq, LANES), F32), jnp.zeros((bq, 1), F32))
        carry = step(qi, ((zero, zero), jnp.zeros((bq, LANES), F32)), True)
        heads, rs = lax.fori_loop(0, qi, lambda t, c: step(qi - 1 - t, c, False), carry)
        o_ref[...] = jnp.where(lo, heads[0][0], heads[1][0])
        rs_ref[...] = rs

    blk = pl.BlockSpec((bq, LANES), lambda p, i: (i, p))
    return pl.pallas_call(
        body, name=name, grid=(2, nq),
        in_specs=[blk,
                  pl.BlockSpec((lp, LANES), lambda p, i: (0, 2 + p)),
                  pl.BlockSpec((lp, LANES), lambda p, i: (0, 4 + p))],
        out_specs=[blk, blk],
        out_shape=[jax.ShapeDtypeStruct((lp, SB_WIDTH), F32), jax.ShapeDtypeStruct((lp, SB_WIDTH), F32)],
        compiler_params=_cp("arbitrary", "arbitrary"),
    )(qkv, qkv, qkv)


def sb_attn_bwd(qkv, rs, d_o, name):
    lp = qkv.shape[0]
    bq = min(ROW_TILE, lp)
    nq = lp // bq

    def body(q_ref, k_ref, v_ref, rs_ref, do_ref, dq_ref, dk_ref, dv_ref):
        qi = pl.program_id(1)

        @pl.when(qi == 0)
        def _():
            dk_ref[...] = jnp.zeros_like(dk_ref)
            dv_ref[...] = jnp.zeros_like(dv_ref)

        q = q_ref[...]
        lane = _lane((bq, LANES))
        lo = lane < 64
        zq = jnp.zeros_like(q)
        qs = (jnp.where(lo, q, zq), jnp.where(lo, zq, q))
        dob = do_ref[...].astype(BF16)
        dos = (jnp.where(lo, dob, zq), jnp.where(lo, zq, dob))
        rs_blk = rs_ref[...]
        ri = lax.broadcasted_iota(jnp.int32, (bq, bq), 0)
        ci = lax.broadcasted_iota(jnp.int32, (bq, bq), 1)
        tbefore = (ri < ci).astype(BF16)

        def step(kb, carry, masked):
            off = pl.multiple_of(kb * bq, bq)
            k = k_ref[pl.ds(off, bq), :]
            v = v_ref[pl.ds(off, bq), :]
            out = []
            dk_acc = jnp.zeros((bq, LANES), F32)
            dv_acc = jnp.zeros((bq, LANES), F32)
            for j in range(2):
                dq_acc, g_run = carry[j]
                r_right = jnp.sum(jnp.where(lane == 64 * j + kb, rs_blk, 0.0), axis=1, keepdims=True)
                _, _, sig, w = _sb_block(qs[j], k, r_right, masked, bq)
                g = w * _dot(dos[j], v, "nt")
                g_before = _tri_sum(g, tbefore) + g_run
                dz = g - sig * (g + g_before)
                if masked:
                    dz = jnp.where(ci < ri, dz, 0.0)
                dzb = (dz * SB_SCALE).astype(BF16)
                dk_acc = dk_acc + _dot(dzb, qs[j], "tn")
                dv_acc = dv_acc + _dot(w.astype(BF16), dos[j], "tn")
                out.append((dq_acc + _dot(dzb, k), g_run + jnp.sum(g, axis=1, keepdims=True)))
            dk_ref[pl.ds(off, bq), :] += dk_acc
            dv_ref[pl.ds(off, bq), :] += dv_acc
            return tuple(out)

        zero = (jnp.zeros((bq, LANES), F32), jnp.zeros((bq, 1), F32))
        carry = lax.fori_loop(0, qi, lambda t, c: step(t, c, False), (zero, zero))
        carry = step(qi, carry, True)
        dq_ref[...] = jnp.where(lo, carry[0][0], carry[1][0]).astype(dq_ref.dtype)

    blk = pl.BlockSpec((bq, LANES), lambda p, i: (i, p))
    return pl.pallas_call(
        body, name=name, grid=(2, nq),
        in_specs=[blk,
                  pl.BlockSpec((lp, LANES), lambda p, i: (0, 2 + p)),
                  pl.BlockSpec((lp, LANES), lambda p, i: (0, 4 + p)),
                  blk, blk],
        out_specs=[blk,
                   pl.BlockSpec((lp, LANES), lambda p, i: (0, p)),
                   pl.BlockSpec((lp, LANES), lambda p, i: (0, p))],
        out_shape=[jax.ShapeDtypeStruct((lp, SB_WIDTH), BF16), jax.ShapeDtypeStruct((lp, SB_WIDTH), F32),
                   jax.ShapeDtypeStruct((lp, SB_WIDTH), F32)],
        compiler_params=_cp("arbitrary", "arbitrary"),
    )(qkv, qkv, qkv, rs, d_o)


def _mla_masks(p, bq):
    lane = _lane((bq, 2 * LANES))
    out = []
    for j in range(2):
        h = 2 * p + j
        nope = jnp.logical_and(lane >= 64 * j, lane < 64 * (j + 1))
        rope = jnp.logical_and(lane >= LANES + MLA_ROPE * h, lane < LANES + MLA_ROPE * (h + 1))
        out.append(jnp.logical_or(nope, rope))
    return out


def mla_attn_fwd(qc, kc, v, name):
    lp = qc.shape[0]
    bq = min(ROW_TILE, lp)
    nq = lp // bq

    def body(q_ref, k_ref, v_ref, o_ref, lse_ref):
        p = pl.program_id(0)
        qi = pl.program_id(1)
        q = q_ref[...]
        masks = _mla_masks(p, bq)
        qs = [jnp.where(mk, q, jnp.zeros_like(q)) for mk in masks]
        lo = _lane((bq, LANES)) < 64
        ri = lax.broadcasted_iota(jnp.int32, (bq, bq), 0)
        ci = lax.broadcasted_iota(jnp.int32, (bq, bq), 1)

        def step(kb, carry, masked):
            off = pl.multiple_of(kb * bq, bq)
            k = k_ref[pl.ds(off, bq), :]
            vv = v_ref[pl.ds(off, bq), :]
            out = []
            for j in range(2):
                acc, m_run, l_run = carry[j]
                s = _dot(qs[j], k, "nt") * MLA_SCALE
                if masked:
                    s = jnp.where(ci <= ri, s, NEG)
                m_new = jnp.maximum(m_run, jnp.max(s, axis=1, keepdims=True))
                alpha = jnp.exp(m_run - m_new)
                pr = jnp.exp(s - m_new)
                out.append((acc * alpha + _dot(pr.astype(BF16), vv), m_new,
                            l_run * alpha + jnp.sum(pr, axis=1, keepdims=True)))
            return tuple(out)

        zero = (jnp.zeros((bq, LANES), F32), jnp.full((bq, 1), NEG, F32), jnp.zeros((bq, 1), F32))
        carry = step(qi, (zero, zero), True)
        carry = lax.fori_loop(0, qi, lambda t, c: step(qi - 1 - t, c, False), carry)
        (a0, m0, l0), (a1, m1, l1) = carry
        o_ref[...] = jnp.where(lo, a0 / l0, a1 / l1)
        lse_ref[...] = jnp.where(lo, m0 + jnp.log(l0), m1 + jnp.log(l1))

    return pl.pallas_call(
        body, name=name, grid=(2, nq),
        in_specs=[pl.BlockSpec((bq, 2 * LANES), lambda p, i: (i, p)),
                  pl.BlockSpec((lp, 2 * LANES), lambda p, i: (0, p)),
                  pl.BlockSpec((lp, LANES), lambda p, i: (0, p))],
        out_specs=[pl.BlockSpec((bq, LANES), lambda p, i: (i, p)),
                   pl.BlockSpec((bq, LANES), lambda p, i: (i, p))],
        out_shape=[jax.ShapeDtypeStruct((lp, 2 * LANES), F32), jax.ShapeDtypeStruct((lp, 2 * LANES), F32)],
        compiler_params=_cp("arbitrary", "arbitrary"),
    )(qc, kc, v)


def mla_attn_bwd(qc, kc, v, o, lse, d_o, name):
    lp = qc.shape[0]
    bq = min(ROW_TILE, lp)
    nq = lp // bq

    def body(q_ref, k_ref, v_ref, o_ref, lse_ref, do_ref, dq_ref, dk_ref, dv_ref):
        p = pl.program_id(0)
        qi = pl.program_id(1)

        @pl.when(qi == 0)
        def _():
            dk_ref[...] = jnp.zeros_like(dk_ref)
            dv_ref[...] = jnp.zeros_like(dv_ref)

        q = q_ref[...]
        d_o = do_ref[...]
        masks = _mla_masks(p, bq)
        qs = [jnp.where(mk, q, jnp.zeros_like(q)) for mk in masks]
        lo = _lane((bq, LANES)) < 64
        dob = d_o.astype(BF16)
        zb = jnp.zeros_like(dob)
        dos = (jnp.where(lo, dob, zb), jnp.where(lo, zb, dob))
        od = o_ref[...] * d_o
        delta = (jnp.sum(jnp.where(lo, od, 0.0), axis=1, keepdims=True),
                 jnp.sum(jnp.where(lo, 0.0, od), axis=1, keepdims=True))
        lse_blk = lse_ref[...]
        lses = (lse_blk[:, 0:1], lse_blk[:, 64:65])
        ri = lax.broadcasted_iota(jnp.int32, (bq, bq), 0)
        ci = lax.broadcasted_iota(jnp.int32, (bq, bq), 1)

        def step(kb, carry, masked):
            off = pl.multiple_of(kb * bq, bq)
            k = k_ref[pl.ds(off, bq), :]
            vv = v_ref[pl.ds(off, bq), :]
            out = []
            dk_acc = jnp.zeros((bq, 2 * LANES), F32)
            dv_acc = jnp.zeros((bq, LANES), F32)
            for j in range(2):
                s = _dot(qs[j], k, "nt") * MLA_SCALE
                if masked:
                    s = jnp.where(ci <= ri, s, NEG)
                pr = jnp.exp(s - lses[j])
                dp = _dot(dos[j], vv, "nt")
                ds = (pr * (dp - delta[j]) * MLA_SCALE).astype(BF16)
                dk_acc = dk_acc + _dot(ds, qs[j], "tn")
                dv_acc = dv_acc + _dot(pr.astype(BF16), dos[j], "tn")
                out.append(carry[j] + _dot(ds, k))
            dk_ref[pl.ds(off, bq), :] += dk_acc
            dv_ref[pl.ds(off, bq), :] += dv_acc
            return tuple(out)

        zero = jnp.zeros((bq, 2 * LANES), F32)
        carry = step(qi, (zero, zero), True)
        carry = lax.fori_loop(0, qi, lambda t, c: step(qi - 1 - t, c, False), carry)
        dq_ref[...] = jnp.where(masks[0], carry[0], 0.0) + jnp.where(masks[1], carry[1], 0.0)

    return pl.pallas_call(
        body, name=name, grid=(2, nq),
        in_specs=[pl.BlockSpec((bq, 2 * LANES), lambda p, i: (i, p)),
                  pl.BlockSpec((lp, 2 * LANES), lambda p, i: (0, p)),
                  pl.BlockSpec((lp, LANES), lambda p, i: (0, p)),
                  pl.BlockSpec((bq, LANES), lambda p, i: (i, p)),
                  pl.BlockSpec((bq, LANES), lambda p, i: (i, p)),
                  pl.BlockSpec((bq, LANES), lambda p, i: (i, p))],
        out_specs=[pl.BlockSpec((bq, 2 * LANES), lambda p, i: (i, p)),
                   pl.BlockSpec((lp, 2 * LANES), lambda p, i: (0, p)),
                   pl.BlockSpec((lp, LANES), lambda p, i: (0, p))],
        out_shape=[jax.ShapeDtypeStruct((lp, 4 * LANES), F32), jax.ShapeDtypeStruct((lp, 4 * LANES), F32),
                   jax.ShapeDtypeStruct((lp, 2 * LANES), F32)],
        compiler_params=_cp("arbitrary", "arbitrary"),
    )(qc, kc, v, o, lse, d_o)


def _mix_out(y_pre, z, o_sb, o_mla, g_ssd, g_sb, g_mla):
    return jnp.concatenate([_rms(y_pre * _silu(z), g_ssd), _rms(o_sb, g_sb), _rms(o_mla, g_mla)], axis=1)


def _ffn_act(up_a, up_b, halo_a, halo_b, w_a, w_b, b_a, b_b, i):
    ca = conv_fwd(up_a, halo_a, w_a, i) + b_a
    cb_ = conv_fwd(up_b, halo_b, w_b, i) + b_b
    return ca, cb_


def layer_fwd(h, w, cs, sn, l):
    lp = h.shape[0]
    nm = f"l{l}_"
    hn = rowwise(lambda i, n, x, g: _rms(x, g), [RI(h), PA(w["norm_mix_g"])], [RO(D_MODEL, BF16)], nm + "rms_mix", lp)[0]
    u = mm(hn, w["w_main"], "nn", F32, nm + "in_main", tn=768)
    qkv = mm(hn, w["w_sb"], "nn", BF16, nm + "in_sb")
    xbc_c = rowwise(lambda i, n, x, hl, cw, cb_: _silu(conv_fwd(x, hl, cw, i) + cb_),
                    [RI(u, SSD_XBC, 0), HP(u, SSD_XBC, 0), PA(w["ssd_conv_w"]), PA(w["ssd_conv_b"])],
                    [RO(SSD_XBC, F32)], nm + "ssd_conv", lp)[0]
    y_pre, hprev = ssd_fwd(xbc_c, u, w["dt_bias"], w["a_log"], w["d_skip"], nm + "ssd_fwd")
    o_sb, rs_sb = sb_attn_fwd(qkv, nm + "sb_fwd")
    qn, kvn = rowwise(lambda i, n, qa, ckv, gq, gkv: (_rms(qa, gq, MLA_Q_RANK), _rms(ckv, gkv)),
                      [RI(u, 256, U_QA // 256), RI(u, LANES, U_CKV // LANES), PA(w["q_norm_g"]), PA(w["kv_norm_g"])],
                      [RO(256, BF16), RO(LANES, BF16)], nm + "mla_rms", lp)
    qf = mm(qn, w["w_uq"], "nn", F32, nm + "mla_uq")
    kvf = mm(kvn, w["w_ukv"], "nn", F32, nm + "mla_ukv")

    def pack(i, n, qf_, kvf_, kr4, cos, sin):
        qr = qf_[:, 256:384]
        qr = qr * cos + rope_rot(qr) * sin
        kr = kr4 * cos + rope_rot(kr4) * sin
        qc = jnp.concatenate([qf_[:, 0:128], qr, qf_[:, 128:256], qr], axis=1)
        kc = jnp.concatenate([kvf_[:, 0:128], kr, kvf_[:, 128:256], kr], axis=1)
        return qc, kc, kvf_[:, 256:512]

    qc, kc, vv = rowwise(pack, [RI(qf), RI(kvf), RI(u, LANES, U_KR4 // LANES), RI(cs), RI(sn)],
                         [RO(512, BF16), RO(512, BF16), RO(256, BF16)], nm + "mla_pack", lp)
    o_mla, lse = mla_attn_fwd(qc, kc, vv, nm + "mla_fwd")
    cat = rowwise(lambda i, n, *a: _mix_out(*a),
                  [RI(y_pre), RI(u, SSD_WIDTH, U_Z // SSD_WIDTH), RI(o_sb), RI(o_mla),
                   PA(w["ssd_norm_g"]), PA(w["sb_norm_g"]), PA(w["mla_norm_g"])],
                  [RO(D_MODEL, BF16)], nm + "mix_out", lp)[0]
    h_mid = mm(cat, w["w_out"], "nn", F32, nm + "out_proj", add=h)
    hn2 = rowwise(lambda i, n, x, g: _rms(x, g), [RI(h_mid), PA(w["norm_ffn_g"])], [RO(D_MODEL, BF16)], nm + "rms_ffn", lp)[0]
    up_a = mm(hn2, w["w_up_a"], "nn", F32, nm + "up_a", tn=1408)
    up_b = mm(hn2, w["w_up_b"], "nn", F32, nm + "up_b", tn=1408)
    wc = 1408

    def act(i, n, ua, ub, ha, hb_, wa, wb, ba, bb_):
        ca, cb_ = _ffn_act(ua, ub, ha, hb_, wa, wb, ba, bb_, i)
        return _silu(ca) * cb_

    a_t = rowwise(act, [RI(up_a, wc, 0, True), RI(up_b, wc, 0, True), HP(up_a, wc, 0, True), HP(up_b, wc, 0, True),
                        PA(w["ffn_conv_w_a"], wc, 0, True), PA(w["ffn_conv_w_b"], wc, 0, True),
                        PA(w["ffn_conv_b_a"], wc, 0, True), PA(w["ffn_conv_b_b"], wc, 0, True)],
                  [RO(D_FF, BF16, wc, True)], nm + "ffn_act", lp, ncol=D_FF // wc)[0]
    h_out = mm(a_t, w["w_down"], "nn", F32, nm + "down", add=h_mid)
    saved = dict(h=h, hn=hn, u=u, qkv=qkv, xbc_c=xbc_c, y_pre=y_pre, hprev=hprev, o_sb=o_sb, rs_sb=rs_sb, qn=qn, kvn=kvn,
                 qc=qc, kc=kc, vv=vv, o_mla=o_mla, lse=lse, cat=cat, h_mid=h_mid, hn2=hn2, up_a=up_a, up_b=up_b, a_t=a_t)
    return h_out, saved


def layer_bwd(dh_out, w, s, cs, sn, l):
    lp = dh_out.shape[0]
    nm = f"l{l}b_"
    g = {}
    wc = 1408
    ncolf = D_FF // wc
    g["w_down"] = mm(s["a_t"], dh_out, "tn", F32, nm + "dw_down")
    d_act = mm(dh_out, w["w_down"], "nt", F32, nm + "d_act", tn=1408)

    def act_bwd(i, n, ua, ub, ha, hb_, wa, wb, ba, bb_, da_):
        ca, cb_ = _ffn_act(ua, ub, ha, hb_, wa, wb, ba, bb_, i)
        sg = jax.nn.sigmoid(ca)
        dca = da_ * cb_ * (sg * (1.0 + ca * (1.0 - sg)))
        dcb = da_ * (ca * sg)
        return (dca, dcb, conv_bwd_w(dca, ua, ha, i, FFN_CONV), conv_bwd_w(dcb, ub, hb_, i, FFN_CONV),
                jnp.sum(dca, axis=0, keepdims=True), jnp.sum(dcb, axis=0, keepdims=True))

    dca, dcb, g["ffn_conv_w_a"], g["ffn_conv_w_b"], g["ffn_conv_b_a"], g["ffn_conv_b_b"] = rowwise(
        act_bwd, [RI(s["up_a"], wc, 0, True), RI(s["up_b"], wc, 0, True), HP(s["up_a"], wc, 0, True),
                  HP(s["up_b"], wc, 0, True), PA(w["ffn_conv_w_a"], wc, 0, True), PA(w["ffn_conv_w_b"], wc, 0, True),
                  PA(w["ffn_conv_b_a"], wc, 0, True), PA(w["ffn_conv_b_b"], wc, 0, True), RI(d_act, wc, 0, True)],
        [RO(D_FF, F32, wc, True), RO(D_FF, F32, wc, True), AO(FFN_CONV, D_FF, wc, True), AO(FFN_CONV, D_FF, wc, True),
         AO(1, D_FF, wc, True), AO(1, D_FF, wc, True)], nm + "ffn_act_bwd", lp, ncol=ncolf)

    def conv_t(i, n, da_, db_, ha, hb_, wa, wb):
        return conv_bwd_data(da_, ha, wa, i, n), conv_bwd_data(db_, hb_, wb, i, n)

    dup_a, dup_b = rowwise(conv_t, [RI(dca, wc, 0, True), RI(dcb, wc, 0, True), HN(dca, wc, 0, True), HN(dcb, wc, 0, True),
                                    PA(w["ffn_conv_w_a"], wc, 0, True), PA(w["ffn_conv_w_b"], wc, 0, True)],
                           [RO(D_FF, BF16, wc, True), RO(D_FF, BF16, wc, True)], nm + "ffn_conv_t", lp, ncol=ncolf)
    g["w_up_a"] = mm(s["hn2"], dup_a, "tn", F32, nm + "dw_up_a", tn=1408)
    g["w_up_b"] = mm(s["hn2"], dup_b, "tn", F32, nm + "dw_up_b", tn=1408)
    dhn2 = mm(dup_a, w["w_up_a"], "nt", F32, nm + "dhn2_a", tk=1408)
    dhn2 = mm(dup_b, w["w_up_b"], "nt", F32, nm + "dhn2_b", add=dhn2, tk=1408)

    def rms_bwd(i, n, x, gg, dy, dres):
        _, vjp = jax.vjp(_rms, x, gg)
        dx, dg = vjp(dy)
        return dres + dx, dg

    dh_mid, g["norm_ffn_g"] = rowwise(rms_bwd, [RI(s["h_mid"]), PA(w["norm_ffn_g"]), RI(dhn2), RI(dh_out)],
                                      [RO(D_MODEL, F32), AO(1, D_MODEL)], nm + "rms_ffn_bwd", lp)
    g["w_out"] = mm(s["cat"], dh_mid, "tn", F32, nm + "dw_out")
    d_cat = mm(dh_mid, w["w_out"], "nt", F32, nm + "d_cat")
    u = s["u"]

    def mix_bwd(i, n, y_pre, z, o_sb, o_mla, g1, g2, g3, dcat):
        _, vjp = jax.vjp(_mix_out, y_pre, z, o_sb, o_mla, g1, g2, g3)
        return vjp(dcat)

    dy_pre, dz, do_sb, do_mla, g["ssd_norm_g"], g["sb_norm_g"], g["mla_norm_g"] = rowwise(
        mix_bwd, [RI(s["y_pre"]), RI(u, SSD_WIDTH, U_Z // SSD_WIDTH), RI(s["o_sb"]), RI(s["o_mla"]),
                  PA(w["ssd_norm_g"]), PA(w["sb_norm_g"]), PA(w["mla_norm_g"]), RI(d_cat)],
        [RO(SSD_WIDTH, F32), RO(SSD_WIDTH, BF16), RO(SB_WIDTH, F32), RO(256, F32),
         AO(1, SSD_WIDTH), AO(1, SB_WIDTH), AO(1, 256)], nm + "mix_out_bwd", lp)
    dxs, dbp, dcp, ddtp, pg = ssd_bwd(s["xbc_c"], u, w["dt_bias"], w["a_log"], w["d_skip"], s["hprev"], dy_pre, nm + "ssd_bwd")
    pg = pg.reshape(4, SUBLANES, LANES).sum(axis=0)
    g["dt_bias"], g["a_log"], g["d_skip"] = pg[0:1], pg[1:2], pg[2:3]

    def conv4_bwd(i, n, x, hl, cw, cb_, dxs_, dbp_, dcp_, ddtp_):
        pre = conv_fwd(x, hl, cw, i) + cb_
        d_b = jnp.concatenate([dbp_[:, 0:128] + dbp_[:, 128:256], dbp_[:, 256:384] + dbp_[:, 384:512]], axis=1)
        d_c = jnp.concatenate([dcp_[:, 0:128] + dcp_[:, 128:256], dcp_[:, 256:384] + dcp_[:, 384:512]], axis=1)
        d_out = jnp.concatenate([dxs_, d_b, d_c], axis=1)
        sg = jax.nn.sigmoid(pre)
        d_pre = d_out * (sg * (1.0 + pre * (1.0 - sg)))
        ddt = ddtp_[:, 0:128] + ddtp_[:, 128:256] + ddtp_[:, 256:384] + ddtp_[:, 384:512]
        return d_pre, ddt, conv_bwd_w(d_pre, x, hl, i, SSD_CONV), jnp.sum(d_pre, axis=0, keepdims=True)

    d_pre, ddt, g["ssd_conv_w"], g["ssd_conv_b"] = rowwise(
        conv4_bwd, [RI(u, SSD_XBC, 0), HP(u, SSD_XBC, 0), PA(w["ssd_conv_w"]), PA(w["ssd_conv_b"]),
                    RI(dxs), RI(dbp), RI(dcp), RI(ddtp)],
        [RO(SSD_XBC, F32), RO(LANES, BF16), AO(SSD_CONV, SSD_XBC), AO(1, SSD_XBC)], nm + "ssd_conv_bwd", lp)
    d_xbc = rowwise(lambda i, n, d, hn_, cw: conv_bwd_data(d, hn_, cw, i, n),
                    [RI(d_pre), HN(d_pre), PA(w["ssd_conv_w"])], [RO(SSD_XBC, BF16)], nm + "ssd_conv_t", lp)[0]
    dq_sb, dk_sb, dv_sb = sb_attn_bwd(s["qkv"], s["rs_sb"], do_sb, nm + "sb_bwd")
    dqkv = jnp.concatenate([dq_sb, dk_sb.astype(BF16), dv_sb.astype(BF16)], axis=1)
    dqc, dkc, dvv = mla_attn_bwd(s["qc"], s["kc"], s["vv"], s["o_mla"], s["lse"], do_mla, nm + "mla_bwd")

    def unpack(i, n, dqc_, dkc_, dvv_, cos, sin):
        dqr = dqc_[:, 128:256] + dqc_[:, 384:512]
        dqr = dqr * cos + rope_rot_t(dqr * sin)
        dkr = dkc_[:, 128:256] + dkc_[:, 384:512]
        dkr = dkr * cos + rope_rot_t(dkr * sin)
        dq = jnp.concatenate([dqc_[:, 0:128], dqc_[:, 256:384], dqr], axis=1)
        dkv = jnp.concatenate([dkc_[:, 0:128], dkc_[:, 256:384], dvv_], axis=1)
        return dq, dkv, dkr

    dq, dkv, dkr4 = rowwise(unpack, [RI(dqc), RI(dkc), RI(dvv), RI(cs), RI(sn)],
                            [RO(384, BF16), RO(512, BF16), RO(LANES, BF16)], nm + "mla_unpack", lp)
    g["w_uq"] = mm(s["qn"], dq, "tn", F32, nm + "dw_uq")
    g["w_ukv"] = mm(s["kvn"], dkv, "tn", F32, nm + "dw_ukv")
    dqn = mm(dq, w["w_uq"], "nt", F32, nm + "dqn")
    dkvn = mm(dkv, w["w_ukv"], "nt", F32, nm + "dkvn")

    def mla_rms_bwd(i, n, qa, ckv, gq, gkv, dqn_, dkvn_):
        _, vjp = jax.vjp(lambda a, b, c, d: (_rms(a, c, MLA_Q_RANK), _rms(b, d)), qa, ckv, gq, gkv)
        return vjp((dqn_, dkvn_))

    dqa, dckv, g["q_norm_g"], g["kv_norm_g"] = rowwise(
        mla_rms_bwd, [RI(u, 256, U_QA // 256), RI(u, LANES, U_CKV // LANES), PA(w["q_norm_g"]), PA(w["kv_norm_g"]),
                      RI(dqn), RI(dkvn)],
        [RO(256, BF16), RO(LANES, BF16), AO(1, 256), AO(1, LANES)], nm + "mla_rms_bwd", lp)
    du = jnp.concatenate([d_xbc, dz, dqa, dckv, dkr4, ddt, jnp.zeros((lp, LANES), BF16)], axis=1)
    g["w_main"] = mm(s["hn"], du, "tn", F32, nm + "dw_main", tn=768)
    g["w_sb"] = mm(s["hn"], dqkv, "tn", F32, nm + "dw_sb")
    dhn = mm(du, w["w_main"], "nt", F32, nm + "dhn_main", tk=768)
    dhn = mm(dqkv, w["w_sb"], "nt", F32, nm + "dhn_sb", add=dhn)
    dh_in, g["norm_mix_g"] = rowwise(rms_bwd, [RI(s["h"]), PA(w["norm_mix_g"]), RI(dhn), RI(dh_mid)],
                                     [RO(D_MODEL, F32), AO(1, D_MODEL)], nm + "rms_mix_bwd", lp)
    return dh_in, g


_IN_CUTS = np.cumsum((512, 1024, 8, 256, 256, 256, 192, 128, 32))


def _pad_cols(a, n):
    return jnp.pad(a, ((0, 0), (0, n - a.shape[1])))


def prep_layer_weights(full, l):
    w_in = full["w_in"][l]
    c = _IN_CUTS
    z, xbc, dtr = w_in[:, :c[0]], w_in[:, c[0]:c[1]], w_in[:, c[1]:c[2]]
    q_sb, k_sb, v_sb = w_in[:, c[2]:c[3]], w_in[:, c[3]:c[4]], w_in[:, c[4]:c[5]]
    q_a, c_kv, k_r = w_in[:, c[5]:c[6]], w_in[:, c[6]:c[7]], w_in[:, c[7]:c[8]]
    w_main = jnp.concatenate([xbc, z, _pad_cols(q_a, 256), c_kv, k_r, k_r, k_r, k_r, _pad_cols(dtr, 256)], axis=1)
    assert w_main.shape[1] == U_MAIN
    row = lambda v, n=None: _pad_cols(v.reshape(1, -1).astype(F32), v.size if n is None else n)
    uq = full["mla_w_uq"][l].reshape(MLA_Q_RANK, 4, 96)
    w_uq = jnp.concatenate([uq[:, :, :64].reshape(MLA_Q_RANK, 256), uq[:, :, 64:].reshape(MLA_Q_RANK, 128)], axis=1)
    w_uq = jnp.pad(w_uq, ((0, 256 - MLA_Q_RANK), (0, 0)))
    ukv = full["mla_w_ukv"][l].reshape(MLA_KV_RANK, 4, 128)
    w_ukv = jnp.concatenate([ukv[:, :, :64].reshape(MLA_KV_RANK, 256), ukv[:, :, 64:].reshape(MLA_KV_RANK, 256)], axis=1)
    return dict(
        norm_mix_g=row(full["norm_mix_g"][l]), w_main=w_main, w_sb=jnp.concatenate([q_sb, k_sb, v_sb], axis=1),
        ssd_conv_w=full["ssd_conv_w"][l], ssd_conv_b=row(full["ssd_conv_b"][l]),
        dt_bias=row(full["ssd_dt_bias"][l], LANES), a_log=row(full["ssd_a_log"][l], LANES), d_skip=row(full["ssd_d"][l], LANES),
        ssd_norm_g=row(full["ssd_norm_g"][l]), sb_norm_g=row(full["sb_norm_g"][l]),
        q_norm_g=row(full["mla_q_norm_g"][l], 256), kv_norm_g=row(full["mla_kv_norm_g"][l]),
        w_uq=w_uq, w_ukv=w_ukv, mla_norm_g=row(full["mla_norm_g"][l]),
        w_out=full["w_out"][l], norm_ffn_g=row(full["norm_ffn_g"][l]),
        w_up_a=full["ffn_w_up"][l][:, :D_FF], w_up_b=full["ffn_w_up"][l][:, D_FF:],
        ffn_conv_w_a=full["ffn_conv_w"][l][:, :D_FF], ffn_conv_w_b=full["ffn_conv_w"][l][:, D_FF:],
        ffn_conv_b_a=row(full["ffn_conv_b"][l][:D_FF]), ffn_conv_b_b=row(full["ffn_conv_b"][l][D_FF:]),
        w_down=full["ffn_w_down"][l],
    )


def unprep_layer_grads(g):
    wm = g["w_main"]
    xbc, z = wm[:, U_XBC:U_XBC + 1024], wm[:, U_Z:U_Z + 512]
    q_a, c_kv = wm[:, U_QA:U_QA + MLA_Q_RANK], wm[:, U_CKV:U_CKV + 128]
    k_r = (wm[:, U_KR4:U_KR4 + 32] + wm[:, U_KR4 + 32:U_KR4 + 64] + wm[:, U_KR4 + 64:U_KR4 + 96] + wm[:, U_KR4 + 96:U_KR4 + 128])
    dtr = wm[:, U_DT:U_DT + SSD_HEADS]
    w_in = jnp.concatenate([z, xbc, dtr, g["w_sb"], q_a, c_kv, k_r], axis=1)
    guq = g["w_uq"][:MLA_Q_RANK]
    guq = jnp.concatenate([guq[:, :256].reshape(MLA_Q_RANK, 4, 64), guq[:, 256:].reshape(MLA_Q_RANK, 4, 32)], axis=2)
    gukv = g["w_ukv"]
    gukv = jnp.concatenate([gukv[:, :256].reshape(MLA_KV_RANK, 4, 64), gukv[:, 256:].reshape(MLA_KV_RANK, 4, 64)], axis=2)
    return dict(
        norm_mix_g=g["norm_mix_g"][0], w_in=w_in, ssd_conv_w=g["ssd_conv_w"], ssd_conv_b=g["ssd_conv_b"][0],
        ssd_dt_bias=g["dt_bias"][0, :SSD_HEADS], ssd_a_log=g["a_log"][0, :SSD_HEADS], ssd_d=g["d_skip"][0, :SSD_HEADS],
        ssd_norm_g=g["ssd_norm_g"][0], sb_norm_g=g["sb_norm_g"][0], mla_q_norm_g=g["q_norm_g"][0, :MLA_Q_RANK],
        mla_kv_norm_g=g["kv_norm_g"][0], mla_w_uq=guq.reshape(MLA_Q_RANK, 384), mla_w_ukv=gukv.reshape(MLA_KV_RANK, 512),
        mla_norm_g=g["mla_norm_g"][0], w_out=g["w_out"], norm_ffn_g=g["norm_ffn_g"][0],
        ffn_w_up=jnp.concatenate([g["w_up_a"], g["w_up_b"]], axis=1),
        ffn_conv_w=jnp.concatenate([g["ffn_conv_w_a"], g["ffn_conv_w_b"]], axis=1),
        ffn_conv_b=jnp.concatenate([g["ffn_conv_b_a"][0], g["ffn_conv_b_b"][0]], axis=0),
        ffn_w_down=g["w_down"],
    )


def rope_tables(lp):
    pos = jnp.arange(lp, dtype=F32)
    inv = 1.0 / (ROPE_BASE ** (jnp.arange(0, MLA_ROPE, 2, dtype=F32) / MLA_ROPE))
    ang = pos[:, None] * inv[None, :]
    ang = jnp.concatenate([ang, ang] * 4, axis=-1)
    return jnp.cos(ang), jnp.sin(ang)


def local_step(x_seq, target, full):
    seq = x_seq.shape[0]
    length = seq + N_META
    lp = -(-length // ROW_TILE) * ROW_TILE
    cs, sn = rope_tables(lp)
    h = jnp.concatenate([full["meta_tokens"].astype(F32), x_seq, jnp.zeros((lp - length, D_MODEL), F32)], axis=0)
    tgt = jnp.pad(target, ((N_META, lp - length), (0, 0)))
    ws = [prep_layer_weights(full, l) for l in range(DEPTH)]
    saved = []
    for l in range(DEPTH):
        h, s = layer_fwd(h, ws[l], cs, sn, l)
        saved.append(s)
    fg = full["final_norm_g"].reshape(1, D_MODEL).astype(F32)
    tm = min(ROW_TILE, lp)

    def loss_fn(i, n, x, g, t):
        rows = _rows_iota(x) + i * tm
        valid = jnp.logical_and(rows >= N_META, rows < length)

        def f(x_, g_):
            err = jnp.where(valid, _rms(x_, g_) - t, 0.0)
            return 0.5 * jnp.sum(err * err) * (1.0 / D_MODEL)

        val, (dx, dg) = jax.value_and_grad(f, argnums=(0, 1))(x, g)
        return dx, jnp.full((1, LANES), val, F32), dg

    dh, loss_row, g_final = rowwise(loss_fn, [RI(h), PA(fg), RI(tgt)], [RO(D_MODEL, F32), AO(1, LANES), AO(1, D_MODEL)],
                                    "loss_head", lp)
    grads = {}
    per_layer = [None] * DEPTH
    for l in reversed(range(DEPTH)):
        dh, g = layer_bwd(dh, ws[l], saved[l], cs, sn, l)
        per_layer[l] = unprep_layer_grads(g)
    for k in per_layer[0]:
        grads[k] = jnp.stack([per_layer[l][k] for l in range(DEPTH)], axis=0)
    grads["final_norm_g"] = g_final[0]
    grads["meta_tokens"] = dh[:N_META]
    return loss_row[0, 0], dh[N_META:length], grads


_ANY = pl.BlockSpec(memory_space=pl.ANY)


def chip_exchange(srcs, broadcast, name):
    n = len(srcs)
    flips = ((1, 0), (0, 1), (1, 1))

    def body(*refs):
        ins, outs = refs[:n], refs[n:2 * n]
        send_sems, recv_sems, loc_sems = refs[2 * n:]
        x, y, c = lax.axis_index("x"), lax.axis_index("y"), lax.axis_index("c")
        me = 2 * x + y
        copies = []
        for a in range(n):
            src_me = ins[a] if broadcast[a] else ins[a].at[me]
            cp = pltpu.make_async_copy(src_me, outs[a].at[me], loc_sems.at[a])
            cp.start()
            copies.append(cp)
            for k, (fx, fy) in enumerate(flips):
                px = 1 - x if fx else x
                py = 1 - y if fy else y
                src = ins[a] if broadcast[a] else ins[a].at[2 * px + py]
                rc = pltpu.make_async_remote_copy(src_ref=src, dst_ref=outs[a].at[me], send_sem=send_sems.at[a, k],
                                                  recv_sem=recv_sems.at[a, k], device_id=(px, py, c), device_id_type=MESH_ID)
                rc.start()
                copies.append(rc)
        for cp in copies:
            cp.wait()

    out_shape = [jax.ShapeDtypeStruct((N_CHIPS,) + (s.shape if b else s.shape[1:]), s.dtype) for s, b in zip(srcs, broadcast)]
    return pl.pallas_call(
        body, name=name, in_specs=[_ANY] * n, out_specs=[_ANY] * n, out_shape=out_shape,
        scratch_shapes=[pltpu.SemaphoreType.DMA((n, 3)), pltpu.SemaphoreType.DMA((n, 3)), pltpu.SemaphoreType.DMA((n,))],    )(*srcs)


def sibling_swap(srcs, name):
    n = len(srcs)

    def body(*refs):
        ins, outs = refs[:n], refs[n:2 * n]
        send_sems, recv_sems = refs[2 * n:]
        x, y, c = lax.axis_index("x"), lax.axis_index("y"), lax.axis_index("c")
        copies = []
        for a in range(n):
            rc = pltpu.make_async_remote_copy(src_ref=ins[a], dst_ref=outs[a], send_sem=send_sems.at[a],
                                              recv_sem=recv_sems.at[a], device_id=(x, y, 1 - c), device_id_type=MESH_ID)
            rc.start()
            copies.append(rc)
        for cp in copies:
            cp.wait()

    return pl.pallas_call(
        body, name=name, in_specs=[_ANY] * n, out_specs=[_ANY] * n,
        out_shape=[jax.ShapeDtypeStruct(s.shape, s.dtype) for s in srcs],
        scratch_shapes=[pltpu.SemaphoreType.DMA((n,)), pltpu.SemaphoreType.DMA((n,))],    )(*srcs)


WEIGHT_NAMES = ("meta_tokens", "norm_mix_g", "w_in", "ssd_conv_w", "ssd_conv_b", "ssd_dt_bias", "ssd_a_log", "ssd_d",
                "ssd_norm_g", "sb_norm_g", "mla_q_norm_g", "mla_kv_norm_g", "mla_w_uq", "mla_w_ukv", "mla_norm_g",
                "w_out", "norm_ffn_g", "ffn_w_up", "ffn_conv_w", "ffn_conv_b", "ffn_w_down", "final_norm_g")
SHARD_AXIS = {"meta_tokens": 1, "w_in": 2, "ssd_conv_w": 2, "mla_w_uq": 2, "mla_w_ukv": 2, "w_out": 1, "ffn_w_up": 2,
              "ffn_conv_w": 2, "ffn_w_down": 1}
SHARDED = tuple(n for n in WEIGHT_NAMES if n in SHARD_AXIS)
REPLICATED = tuple(n for n in WEIGHT_NAMES if n not in SHARD_AXIS)
GATHER_BF16 = ("w_in", "mla_w_uq", "mla_w_ukv", "w_out", "ffn_w_up", "ffn_w_down")
GATHER_F32 = ("meta_tokens", "ssd_conv_w", "ffn_conv_w")
PACK_ROWS = ROW_TILE


def pack(arrs, dtype):
    flat = jnp.concatenate([a.reshape(-1).astype(dtype) for a in arrs])
    per = PACK_ROWS * PACK_W
    total = -(-flat.size // per) * per
    return jnp.pad(flat, (0, total - flat.size)).reshape(total // PACK_W, PACK_W)


def unpack(buf, shapes):
    flat = buf.reshape(-1)
    out, off = [], 0
    for shp in shapes:
        size = int(np.prod(shp))
        out.append(flat[off:off + size].reshape(shp))
        off += size
    return out


def gather_weights(a):
    full = {n: a[n] for n in REPLICATED}
    bufs = [pack([a[n] for n in GATHER_BF16], BF16), pack([a[n] for n in GATHER_F32], F32)]
    got = chip_exchange(bufs, (True, True), "gather_weights")
    for names, g in ((GATHER_BF16, got[0]), (GATHER_F32, got[1])):
        pieces = [unpack(g[k], [a[n].shape for n in names]) for k in range(N_CHIPS)]
        for idx, n in enumerate(names):
            full[n] = jnp.concatenate([pieces[k][idx] for k in range(N_CHIPS)], axis=SHARD_AXIS[n])
    return full


def _adamw(i, n, p, q, w, m, v):
    g = p + q
    m = ADAM_B1 * m + (1.0 - ADAM_B1) * g
    v = ADAM_B2 * v + (1.0 - ADAM_B2) * jnp.square(g)
    m_hat = m / (1.0 - ADAM_B1 ** ADAM_STEP)
    v_hat = v / (1.0 - ADAM_B2 ** ADAM_STEP)
    delta = -ADAM_LR * (m_hat / (jnp.sqrt(v_hat) + ADAM_EPS) + ADAM_WD * w)
    return g, delta, m, v


def reduce_and_update(a, grads):
    sh_shapes = [a[n].shape for n in SHARDED]
    rep_shapes = [a[n].shape for n in REPLICATED]
    slabs = []
    for k in range(N_CHIPS):
        parts = []
        for n in SHARDED:
            ax = SHARD_AXIS[n]
            size = a[n].shape[ax]
            parts.append(lax.slice_in_dim(grads[n], k * size, (k + 1) * size, axis=ax))
        slabs.append(pack(parts, BF16))
    g_sh = jnp.stack(slabs, axis=0)
    g_rep = pack([grads[n] for n in REPLICATED], F32)
    got_sh, got_rep = chip_exchange([g_sh, g_rep], (False, True), "exchange_grads")
    sums = []
    for tag, got in (("sh", got_sh), ("rep", got_rep)):
        rows = got.shape[1]
        blk = rows // PACK_ROWS
        flat = got.reshape(N_CHIPS * rows, PACK_W)
        sums.append(rowwise(lambda i, n, g0, g1, g2, g3: ((g0.astype(F32) + g1.astype(F32)) + g2.astype(F32)) + g3.astype(F32),
                            [RI(flat, rblk=k * blk) for k in range(N_CHIPS)], [RO(PACK_W, F32)], "sum_chips_" + tag, rows,
                            tm=PACK_ROWS)[0])
    others = sibling_swap(sums, "swap_cores")
    outs = {}
    for tag, names, shapes, p, q in (("sh", SHARDED, sh_shapes, sums[0], others[0]),
                                     ("rep", REPLICATED, rep_shapes, sums[1], others[1])):
        packed = [pack([a[pre + n] for n in names], F32) for pre in ("", "m_", "v_")]
        res = rowwise(_adamw, [RI(p), RI(q)] + [RI(t) for t in packed], [RO(PACK_W, F32)] * 4, "adamw_" + tag,
                      p.shape[0], tm=PACK_ROWS)
        for kind, buf in zip(("grad", "delta", "new_m", "new_v"), res):
            for n, val in zip(names, unpack(buf, shapes)):
                outs[(kind, n)] = val
    return outs


INPUT_NAMES = ("x",) + WEIGHT_NAMES + ("loss_target",) + tuple("m_" + n for n in WEIGHT_NAMES) + tuple("v_" + n for n in WEIGHT_NAMES)


def kernel(x, meta_tokens, norm_mix_g, w_in, ssd_conv_w, ssd_conv_b, ssd_dt_bias, ssd_a_log, ssd_d, ssd_norm_g, sb_norm_g, mla_q_norm_g, mla_kv_norm_g, mla_w_uq, mla_w_ukv, mla_norm_g, w_out, norm_ffn_g, ffn_w_up, ffn_conv_w, ffn_conv_b, ffn_w_down, final_norm_g, loss_target, m_meta_tokens, m_norm_mix_g, m_w_in, m_ssd_conv_w, m_ssd_conv_b, m_ssd_dt_bias, m_ssd_a_log, m_ssd_d, m_ssd_norm_g, m_sb_norm_g, m_mla_q_norm_g, m_mla_kv_norm_g, m_mla_w_uq, m_mla_w_ukv, m_mla_norm_g, m_w_out, m_norm_ffn_g, m_ffn_w_up, m_ffn_conv_w, m_ffn_conv_b, m_ffn_w_down, m_final_norm_g, v_meta_tokens, v_norm_mix_g, v_w_in, v_ssd_conv_w, v_ssd_conv_b, v_ssd_dt_bias, v_ssd_a_log, v_ssd_d, v_ssd_norm_g, v_sb_norm_g, v_mla_q_norm_g, v_mla_kv_norm_g, v_mla_w_uq, v_mla_w_ukv, v_mla_norm_g, v_w_out, v_norm_ffn_g, v_ffn_w_up, v_ffn_conv_w, v_ffn_conv_b, v_ffn_w_down, v_final_norm_g):
    args = (x, meta_tokens, norm_mix_g, w_in, ssd_conv_w, ssd_conv_b, ssd_dt_bias, ssd_a_log, ssd_d, ssd_norm_g, sb_norm_g, mla_q_norm_g, mla_kv_norm_g, mla_w_uq, mla_w_ukv, mla_norm_g, w_out, norm_ffn_g, ffn_w_up, ffn_conv_w, ffn_conv_b, ffn_w_down, final_norm_g, loss_target, m_meta_tokens, m_norm_mix_g, m_w_in, m_ssd_conv_w, m_ssd_conv_b, m_ssd_dt_bias, m_ssd_a_log, m_ssd_d, m_ssd_norm_g, m_sb_norm_g, m_mla_q_norm_g, m_mla_kv_norm_g, m_mla_w_uq, m_mla_w_ukv, m_mla_norm_g, m_w_out, m_norm_ffn_g, m_ffn_w_up, m_ffn_conv_w, m_ffn_conv_b, m_ffn_w_down, m_final_norm_g, v_meta_tokens, v_norm_mix_g, v_w_in, v_ssd_conv_w, v_ssd_conv_b, v_ssd_dt_bias, v_ssd_a_log, v_ssd_d, v_ssd_norm_g, v_sb_norm_g, v_mla_q_norm_g, v_mla_kv_norm_g, v_mla_w_uq, v_mla_w_ukv, v_mla_norm_g, v_w_out, v_norm_ffn_g, v_ffn_w_up, v_ffn_conv_w, v_ffn_conv_b, v_ffn_w_down, v_final_norm_g)
    a = dict(zip(INPUT_NAMES, args, strict=True))
    full = gather_weights(a)
    loss, grad_x, grads = local_step(a["x"][0], a["loss_target"][0], full)
    loss = lax.psum(loss, ("x", "y", "c"))
    outs = reduce_and_update(a, grads)
    result = [loss, grad_x[None]]
    for kind in ("grad", "delta", "new_m", "new_v"):
        result += [outs[(kind, n)] for n in WEIGHT_NAMES]
    return tuple(result)
```

```python
import functools
import math

import numpy as np
import jax
import jax.numpy as jnp
from jax import lax
from jax.experimental import pallas as pl
from jax.experimental.pallas import tpu as pltpu

F32 = jnp.float32
BF16 = jnp.bfloat16
HIGHEST = lax.Precision.HIGHEST
MESH_ID = pl.DeviceIdType.MESH

D_MODEL = 1024
DEPTH = 2
N_META = 16
EPS = 1e-6
SSD_HEADS = 8
SSD_WIDTH = 512
SSD_XBC = 1024
SSD_CONV = 4
SB_WIDTH = 256
SB_SCALE = 64 ** -0.5
MLA_Q_RANK = 192
MLA_KV_RANK = 128
MLA_ROPE = 32
MLA_SCALE = 96 ** -0.5
ROPE_BASE = 10000.0
D_FF = 2816
FFN_CONV = 3
IN_COLS = 2664
N_CHIPS = 4

ADAM_LR = 0.001
ADAM_B1 = 0.9
ADAM_B2 = 0.999
ADAM_EPS = 1e-08
ADAM_WD = 0.01
ADAM_STEP = 10

LANES = 128
SUBLANES = 8
ROW_TILE = 256
VMEM_LIMIT = 56 * 1024 * 1024
PACK_W = 1024

U_XBC, U_Z, U_QA, U_CKV, U_KR4, U_DT, U_MAIN = 0, 1024, 1536, 1792, 1920, 2048, 2304
NEG = -1e30


def _cp(*sem):
    return pltpu.CompilerParams(dimension_semantics=sem if sem else None, vmem_limit_bytes=VMEM_LIMIT)


def _pick(dim, pref):
    if dim <= pref:
        return dim
    best = None
    for t in range(LANES, pref + 1, LANES):
        if dim % t == 0:
            best = t
    assert best is not None, (dim, pref)
    return best


def _dot(a, b, dims="nn", precision=None):
    dn = {"nn": (((1,), (0,)), ((), ())), "nt": (((1,), (1,)), ((), ())), "tn": (((0,), (0,)), ((), ()))}[dims]
    return lax.dot_general(a, b, dn, preferred_element_type=F32, precision=precision)


def _softplus(x):
    return jnp.maximum(x, 0.0) + jnp.log1p(jnp.exp(-jnp.abs(x)))


def _silu(x):
    return x * jax.nn.sigmoid(x)


def _rms(x, g, n=None):
    n = x.shape[-1] if n is None else n
    ms = jnp.sum(x * x, axis=-1, keepdims=True) * (1.0 / n)
    return x * lax.rsqrt(ms + EPS) * g


def mm(a, b, dims, out_dtype, name, add=None, tm=512, tn=1024, tk=1536):
    if dims == "nn":
        (m, k), (k2, n) = a.shape, b.shape
    elif dims == "nt":
        (m, k), (n, k2) = a.shape, b.shape
    else:
        (k, m), (k2, n) = a.shape, b.shape
    assert k == k2, (a.shape, b.shape, dims)
    tm, tn, tk = _pick(m, tm), _pick(n, tn), _pick(k, tk)
    nk = k // tk
    if dims == "tn":
        a_spec = pl.BlockSpec((tk, tm), lambda i, j, kk: (kk, i))
    else:
        a_spec = pl.BlockSpec((tm, tk), lambda i, j, kk: (i, kk))
    if dims == "nt":
        b_spec = pl.BlockSpec((tn, tk), lambda i, j, kk: (j, kk))
    else:
        b_spec = pl.BlockSpec((tk, tn), lambda i, j, kk: (kk, j))
    o_spec = pl.BlockSpec((tm, tn), lambda i, j, kk: (i, j))
    has_add = add is not None

    def body(*refs):
        if has_add:
            a_ref, b_ref, add_ref, o_ref, acc_ref = refs
        else:
            a_ref, b_ref, o_ref, acc_ref = refs
        kk = pl.program_id(2)
        part = _dot(a_ref[...].astype(BF16), b_ref[...].astype(BF16), dims)

        @pl.when(kk == 0)
        def _():
            acc_ref[...] = part

        @pl.when(kk > 0)
        def _():
            acc_ref[...] += part

        @pl.when(kk == nk - 1)
        def _():
            r = acc_ref[...]
            if has_add:
                r = r + add_ref[...].astype(F32)
            o_ref[...] = r.astype(o_ref.dtype)

    in_specs = [a_spec, b_spec] + ([o_spec] if has_add else [])
    args = (a, b) + ((add,) if has_add else ())
    return pl.pallas_call(
        body, name=name, grid=(m // tm, n // tn, nk),
        in_specs=in_specs, out_specs=o_spec,
        out_shape=jax.ShapeDtypeStruct((m, n), out_dtype),
        scratch_shapes=[pltpu.VMEM((tm, tn), F32)],
        compiler_params=_cp("parallel", "parallel", "arbitrary"),
    )(*args)


def RI(arr, width=None, cidx=0, cv=False, rblk=0):
    return ("row" if rblk == 0 else ("row", rblk), arr, arr.shape[1] if width is None else width, cidx, cv)


def HP(arr, width=None, cidx=0, cv=False):
    return ("prev", arr, arr.shape[1] if width is None else width, cidx, cv)


def HN(arr, width=None, cidx=0, cv=False):
    return ("next", arr, arr.shape[1] if width is None else width, cidx, cv)


def PA(arr, width=None, cidx=0, cv=False):
    return ("par", arr, arr.shape[1] if width is None else width, cidx, cv)


def RO(ncols, dtype, width=None, cv=False):
    return ("row", ncols, dtype, ncols if width is None else width, cv)


def AO(nrows, ncols, width=None, cv=False):
    return ("acc", (nrows, ncols), F32, ncols if width is None else width, cv)


def rowwise(fn, ins, outs, name, rows, tm=ROW_TILE, ncol=1):
    tm = min(tm, rows)
    assert rows % tm == 0
    nrow = rows // tm
    hb = tm // SUBLANES
    last_hb = rows // SUBLANES - 1
    in_specs, args = [], []
    for kind, arr, width, cidx, cv in ins:
        def cmap(j, cidx=cidx, cv=cv):
            return cidx + j if cv else cidx
        if kind == "row":
            spec = pl.BlockSpec((tm, width), lambda j, i, cmap=cmap: (i, cmap(j)))
        elif isinstance(kind, tuple):
            spec = pl.BlockSpec((tm, width), lambda j, i, cmap=cmap, rblk=kind[1]: (i + rblk, cmap(j)))
        elif kind == "prev":
            spec = pl.BlockSpec((SUBLANES, width), lambda j, i, cmap=cmap: (jnp.maximum(i * hb - 1, 0), cmap(j)))
        elif kind == "next":
            spec = pl.BlockSpec((SUBLANES, width), lambda j, i, cmap=cmap: (jnp.minimum((i + 1) * hb, last_hb), cmap(j)))
        else:
            spec = pl.BlockSpec((arr.shape[0], width), lambda j, i, cmap=cmap: (0, cmap(j)))
        in_specs.append(spec)
        args.append(arr)
    out_specs, out_shapes, acc_cv = [], [], []
    for kind, shp, dtype, width, cv in outs:
        if kind == "row":
            out_specs.append(pl.BlockSpec((tm, width), lambda j, i, cv=cv: (i, j if cv else 0)))
            out_shapes.append(jax.ShapeDtypeStruct((rows, shp), dtype))
            acc_cv.append(None)
        else:
            out_specs.append(pl.BlockSpec((shp[0], width), lambda j, i, cv=cv: (0, j if cv else 0)))
            out_shapes.append(jax.ShapeDtypeStruct(shp, dtype))
            acc_cv.append(cv)
    n_in = len(ins)

    def body(*refs):
        j = pl.program_id(0)
        i = pl.program_id(1)
        vals = fn(i, nrow, *[r[...] for r in refs[:n_in]])
        if not isinstance(vals, (tuple, list)):
            vals = (vals,)
        for o_ref, v, cv in zip(refs[n_in:], vals, acc_cv):
            if cv is None:
                o_ref[...] = v.astype(o_ref.dtype)
            else:
                first = (i == 0) if cv else jnp.logical_and(i == 0, j == 0)

                @pl.when(first)
                def _(o_ref=o_ref, v=v):
                    o_ref[...] = v.astype(o_ref.dtype)

                @pl.when(jnp.logical_not(first))
                def _(o_ref=o_ref, v=v):
                    o_ref[...] += v.astype(o_ref.dtype)

    res = pl.pallas_call(
        body, name=name, grid=(ncol, nrow), in_specs=in_specs, out_specs=out_specs, out_shape=out_shapes,
        compiler_params=_cp("arbitrary", "arbitrary"),
    )(*args)
    return res


def _rows_iota(x):
    return lax.broadcasted_iota(jnp.int32, x.shape, 0)


def shift_down(x, halo, s):
    if s == 0:
        return x
    tm = x.shape[0]
    top = pltpu.roll(halo, s, 0)
    if tm > SUBLANES:
        top = jnp.concatenate([top, jnp.zeros((tm - SUBLANES, x.shape[1]), x.dtype)], axis=0)
    return jnp.where(_rows_iota(x) < s, top, pltpu.roll(x, s, 0))


def shift_up(x, halo, s):
    if s == 0:
        return x
    tm = x.shape[0]
    bot = pltpu.roll(halo, SUBLANES - s, 0)
    if tm > SUBLANES:
        bot = jnp.concatenate([jnp.zeros((tm - SUBLANES, x.shape[1]), x.dtype), bot], axis=0)
    return jnp.where(_rows_iota(x) >= tm - s, bot, pltpu.roll(x, tm - s, 0))


def conv_fwd(x, halo, w, i):
    kw = w.shape[0]
    halo = jnp.where(i == 0, 0.0, halo)
    out = None
    for k in range(kw):
        term = w[k:k + 1, :] * shift_down(x, halo, kw - 1 - k)
        out = term if out is None else out + term
    return out


def conv_bwd_data(dy, halo_next, w, i, n):
    kw = w.shape[0]
    halo_next = jnp.where(i == n - 1, 0.0, halo_next)
    out = None
    for k in range(kw):
        term = w[k:k + 1, :] * shift_up(dy, halo_next, kw - 1 - k)
        out = term if out is None else out + term
    return out


def conv_bwd_w(dy, x, halo, i, kw):
    halo = jnp.where(i == 0, 0.0, halo)
    rows = [jnp.sum(dy * shift_down(x, halo, kw - 1 - k), axis=0, keepdims=True) for k in range(kw)]
    return jnp.concatenate(rows, axis=0)


def _lane(shape):
    return lax.broadcasted_iota(jnp.int32, shape, 1)


def rope_rot(x):
    lane = _lane(x.shape) % MLA_ROPE
    return jnp.where(lane < MLA_ROPE // 2, -pltpu.roll(x, LANES - MLA_ROPE // 2, 1), pltpu.roll(x, MLA_ROPE // 2, 1))


def rope_rot_t(g):
    lane = _lane(g.shape) % MLA_ROPE
    return jnp.where(lane < MLA_ROPE // 2, pltpu.roll(g, LANES - MLA_ROPE // 2, 1), -pltpu.roll(g, MLA_ROPE // 2, 1))


def _ssd_common(p, xs, dt_raw, bias, alog, q):
    lane = _lane((q, LANES))
    pre = dt_raw + bias
    dt = jnp.where(lane < SSD_HEADS, _softplus(pre), 0.0)
    a_row = -jnp.exp(alog)
    d_a = dt * a_row
    ri = lax.broadcasted_iota(jnp.int32, (q, q), 0)
    ci = lax.broadcasted_iota(jnp.int32, (q, q), 1)
    causal = ri >= ci
    acs = _dot(causal.astype(F32), d_a, "nn", HIGHEST)
    acs_t = acs.T
    subl = lax.broadcasted_iota(jnp.int32, (LANES, q), 0)
    h0, h1 = 2 * p, 2 * p + 1

    def col(arr, h):
        return jnp.sum(jnp.where(lane == h, arr, 0.0), axis=1, keepdims=True)

    def row(arr_t, h):
        return jnp.sum(jnp.where(subl == h, arr_t, 0.0), axis=0, keepdims=True)

    lo = lane < 64
    cols = (col(acs, h0), col(acs, h1))
    rows = (row(acs_t, h0), row(acs_t, h1))
    acs_p = jnp.where(lo, cols[0], cols[1])
    dt_p = jnp.where(lo, col(dt, h0), col(dt, h1))
    tots = (cols[0][q - 1:q, :], cols[1][q - 1:q, :])
    tot_p = jnp.where(lo[0:1, :], tots[0], tots[1])
    lms = tuple(jnp.exp(jnp.where(causal, cols[j] - rows[j], NEG)) for j in range(2))
    return dict(lane=lane, lo=lo, pre=pre, dt=dt, a_row=a_row, h=(h0, h1), acs_p=acs_p, dt_p=dt_p, tots=tots,
                tot_p=tot_p, lms=lms, ri=ri, ci=ci, eacs=jnp.exp(acs_p), dte=jnp.exp(tot_p - acs_p), x=xs * dt_p)


def _pick_lane(row_arr, h):
    return jnp.sum(jnp.where(_lane(row_arr.shape) == h, row_arr, 0.0), axis=1, keepdims=True)


def ssd_fwd(xbc_c, u_main, bias_row, alog_row, d_row, name):
    lp = xbc_c.shape[0]
    q = min(ROW_TILE, lp)
    nc = lp // q
    dt_blk = U_DT // LANES

    def body(xs_ref, b_ref, c_ref, dt_ref, bias_ref, alog_ref, d_ref, y_ref, hp_ref, h_scr):
        p = pl.program_id(0)
        c = pl.program_id(1)

        @pl.when(c == 0)
        def _():
            h_scr[...] = jnp.zeros_like(h_scr)

        xs = xs_ref[...]
        bb = b_ref[...].astype(BF16)
        cb_ = c_ref[...].astype(BF16)
        s = _ssd_common(p, xs, dt_ref[...], bias_ref[...], alog_ref[...], q)
        lo = s["lo"]
        g = _dot(cb_, bb, "nt")
        y = jnp.zeros((q, LANES), F32)
        for j in range(2):
            m = (g * s["lms"][j]).astype(BF16)
            xj = jnp.where(lo if j == 0 else jnp.logical_not(lo), s["x"], 0.0).astype(BF16)
            y = y + _dot(m, xj)
        hp = h_scr[...]
        hp_ref[...] = hp
        y = y + _dot(cb_, hp.astype(BF16), "nt") * s["eacs"]
        d_p = jnp.where(lo[0:1, :], _pick_lane(d_ref[...], s["h"][0]), _pick_lane(d_ref[...], s["h"][1]))
        y_ref[...] = y + d_p * xs
        sub_lo = lax.broadcasted_iota(jnp.int32, (LANES, LANES), 0) < 64
        etot = jnp.where(sub_lo, jnp.exp(s["tots"][0]), jnp.exp(s["tots"][1]))
        h_scr[...] = hp * etot + _dot((s["x"] * s["dte"]).astype(BF16), bb, "tn")

    grp = lambda p: p // 2
    in_specs = [
        pl.BlockSpec((q, LANES), lambda p, c: (c, p)),
        pl.BlockSpec((q, LANES), lambda p, c: (c, 4 + grp(p))),
        pl.BlockSpec((q, LANES), lambda p, c: (c, 6 + grp(p))),
        pl.BlockSpec((q, LANES), lambda p, c: (c, dt_blk)),
        pl.BlockSpec((1, LANES), lambda p, c: (0, 0)),
        pl.BlockSpec((1, LANES), lambda p, c: (0, 0)),
        pl.BlockSpec((1, LANES), lambda p, c: (0, 0)),
    ]
    out_specs = [
        pl.BlockSpec((q, LANES), lambda p, c: (c, p)),
        pl.BlockSpec((None, None, LANES, LANES), lambda p, c: (p, c, 0, 0)),
    ]
    return pl.pallas_call(
        body, name=name, grid=(4, nc), in_specs=in_specs, out_specs=out_specs,
        out_shape=[jax.ShapeDtypeStruct((lp, SSD_WIDTH), F32), jax.ShapeDtypeStruct((4, nc, LANES, LANES), F32)],
        scratch_shapes=[pltpu.VMEM((LANES, LANES), F32)],
        compiler_params=_cp("arbitrary", "arbitrary"),
    )(xbc_c, xbc_c, xbc_c, u_main, bias_row, alog_row, d_row)


def ssd_bwd(xbc_c, u_main, bias_row, alog_row, d_row, hprev, dy, name):
    lp = xbc_c.shape[0]
    q = min(ROW_TILE, lp)
    nc = lp // q
    dt_blk = U_DT // LANES

    def body(xs_ref, b_ref, c_ref, dt_ref, bias_ref, alog_ref, d_ref, hp_ref, dy_ref,
             dxs_ref, db_ref, dc_ref, ddt_ref, pg_ref, dh_scr):
        p = pl.program_id(0)
        cc = pl.program_id(1)

        @pl.when(cc == 0)
        def _():
            dh_scr[...] = jnp.zeros_like(dh_scr)
            pg_ref[...] = jnp.zeros_like(pg_ref)

        xs = xs_ref[...]
        bb = b_ref[...].astype(BF16)
        cb_ = c_ref[...].astype(BF16)
        s = _ssd_common(p, xs, dt_ref[...], bias_ref[...], alog_ref[...], q)
        lane, lo, x = s["lane"], s["lo"], s["x"]
        h0, h1 = s["h"]
        d_y = dy_ref[...]
        hp = hp_ref[...]
        hpb = hp.astype(BF16)
        dhn = dh_scr[...]
        dhnb = dhn.astype(BF16)
        xd = x * s["dte"]
        g = _dot(cb_, bb, "nt")
        dxdiag = jnp.zeros((q, LANES), F32)
        dg = jnp.zeros((q, q), F32)
        row_part, col_part = [], []
        for j in range(2):
            mj = lo if j == 0 else jnp.logical_not(lo)
            lm = s["lms"][j]
            m32 = g * lm
            xj = jnp.where(mj, x, 0.0).astype(BF16)
            dyj = jnp.where(mj, d_y, 0.0).astype(BF16)
            dxdiag = dxdiag + _dot(m32.astype(BF16), dyj, "tn")
            dm = _dot(dyj, xj, "nt")
            dg = dg + dm * lm
            wm = dm * m32
            row_part.append(jnp.sum(wm, axis=1, keepdims=True))
            col_part.append(jnp.sum(wm, axis=0, keepdims=True))
        dgb = dg.astype(BF16)
        d_c = _dot(dgb, bb)
        d_b = _dot(dgb, cb_, "tn")
        yoff = _dot(cb_, hpb, "nt") * s["eacs"]
        d_t = (d_y * s["eacs"]).astype(BF16)
        d_c = d_c + _dot(d_t, hpb)
        d_hp = _dot(d_t, cb_, "tn")
        dxd = _dot(bb, dhnb, "nt")
        d_b = d_b + _dot(xd.astype(BF16), dhnb)
        d_x = dxdiag + dxd * s["dte"]
        r = dxd * xd
        a_terms = d_y * yoff - r

        def gsum(arr):
            return (jnp.sum(jnp.where(lo, arr, 0.0), axis=1, keepdims=True),
                    jnp.sum(jnp.where(lo, 0.0, arr), axis=1, keepdims=True))

        dacs = gsum(a_terms)
        rs = gsum(r)
        hh = dhn * hp
        sub_lo = lax.broadcasted_iota(jnp.int32, (LANES, LANES), 0) < 64
        hsum = (jnp.sum(jnp.where(sub_lo, hh, 0.0), keepdims=True), jnp.sum(jnp.where(sub_lo, 0.0, hh), keepdims=True))
        last = lax.broadcasted_iota(jnp.int32, (q, 1), 0) == q - 1
        etots = (jnp.exp(s["tots"][0]), jnp.exp(s["tots"][1]))
        ddacs = jnp.zeros((q, LANES), F32)
        for j, h in enumerate((h0, h1)):
            dtot = jnp.sum(rs[j], keepdims=True) + hsum[j] * etots[j]
            dj = dacs[j] + row_part[j] + jnp.where(last, dtot, 0.0)
            ddacs = ddacs + jnp.where(lane == h, dj, 0.0)
        subl = lax.broadcasted_iota(jnp.int32, (LANES, q), 0)
        cols_t = jnp.where(subl == h0, col_part[0], 0.0) + jnp.where(subl == h1, col_part[1], 0.0)
        ddacs = ddacs - cols_t.T
        anti = (s["ri"] <= s["ci"]).astype(F32)
        da = _dot(anti, ddacs, "nn", HIGHEST)
        ddt_own = gsum(d_x * xs)
        ddt = jnp.where(lane == h0, ddt_own[0], 0.0) + jnp.where(lane == h1, ddt_own[1], 0.0) + da * s["a_row"]
        draw = ddt * jax.nn.sigmoid(s["pre"])
        ddt_ref[...] = draw
        d_p = jnp.where(lo[0:1, :], _pick_lane(d_ref[...], h0), _pick_lane(d_ref[...], h1))
        dxs_ref[...] = d_p * d_y + d_x * s["dt_p"]
        db_ref[...] = d_b
        dc_ref[...] = d_c
        dds = gsum(d_y * xs)
        lane1 = lane[0:1, :]
        dd_row = (jnp.where(lane1 == h0, jnp.sum(dds[0], keepdims=True), 0.0)
                  + jnp.where(lane1 == h1, jnp.sum(dds[1], keepdims=True), 0.0))
        dbias_row = jnp.sum(draw, axis=0, keepdims=True)
        dalog_row = jnp.sum(da * s["dt"], axis=0, keepdims=True) * s["a_row"]
        sub8 = lax.broadcasted_iota(jnp.int32, (SUBLANES, LANES), 0)
        pg_ref[...] += (jnp.where(sub8 == 0, dbias_row, 0.0) + jnp.where(sub8 == 1, dalog_row, 0.0)
                        + jnp.where(sub8 == 2, dd_row, 0.0))
        etot = jnp.where(sub_lo, etots[0], etots[1])
        dh_scr[...] = d_hp + etot * dhn

    grp = lambda p: p // 2
    rc = lambda c: nc - 1 - c
    in_specs = [
        pl.BlockSpec((q, LANES), lambda p, c: (rc(c), p)),
        pl.BlockSpec((q, LANES), lambda p, c: (rc(c), 4 + grp(p))),
        pl.BlockSpec((q, LANES), lambda p, c: (rc(c), 6 + grp(p))),
        pl.BlockSpec((q, LANES), lambda p, c: (rc(c), dt_blk)),
        pl.BlockSpec((1, LANES), lambda p, c: (0, 0)),
        pl.BlockSpec((1, LANES), lambda p, c: (0, 0)),
        pl.BlockSpec((1, LANES), lambda p, c: (0, 0)),
        pl.BlockSpec((None, None, LANES, LANES), lambda p, c: (p, rc(c), 0, 0)),
        pl.BlockSpec((q, LANES), lambda p, c: (rc(c), p)),
    ]
    out_specs = [
        pl.BlockSpec((q, LANES), lambda p, c: (rc(c), p)),
        pl.BlockSpec((q, LANES), lambda p, c: (rc(c), p)),
        pl.BlockSpec((q, LANES), lambda p, c: (rc(c), p)),
        pl.BlockSpec((q, LANES), lambda p, c: (rc(c), p)),
        pl.BlockSpec((SUBLANES, LANES), lambda p, c: (p, 0)),
    ]
    wide = jax.ShapeDtypeStruct((lp, 4 * LANES), F32)
    return pl.pallas_call(
        body, name=name, grid=(4, nc), in_specs=in_specs, out_specs=out_specs,
        out_shape=[wide, wide, wide, wide, jax.ShapeDtypeStruct((4 * SUBLANES, LANES), F32)],
        scratch_shapes=[pltpu.VMEM((LANES, LANES), F32)],
        compiler_params=_cp("arbitrary", "arbitrary"),
    )(xbc_c, xbc_c, xbc_c, u_main, bias_row, alog_row, d_row, hprev, dy)


def _split_bf16(x):
    hi = x.astype(BF16)
    return hi, (x - hi.astype(F32)).astype(BF16)


def _tri_sum(x, tri):
    hi, lo = _split_bf16(x)
    return _dot(hi, tri) + _dot(lo, tri)


def _sb_blocks(qs, k, r_runs, masked, bq):
    ri = lax.broadcasted_iota(jnp.int32, (bq, bq), 0)
    ci = lax.broadcasted_iota(jnp.int32, (bq, bq), 1)
    tri_after = (ri > ci).astype(BF16)
    zs = [_dot(qj, k, "nt") for qj in qs]
    us, sigs, parts = [], [], []
    for z in zs:
        u = -(jnp.maximum(z, 0.0) + jnp.log(1.0 + jnp.exp(-jnp.abs(z))))
        sigs.append(jnp.exp(z + u))
        if masked:
            u = jnp.where(ci < ri, u, 0.0)
        us.append(u)
        parts.append(_split_bf16(u))
    afters = [_dot(hi, tri_after) + _dot(lo, tri_after) for hi, lo in parts]
    ws = []
    for sig, after, r_run in zip(sigs, afters, r_runs):
        w = sig * jnp.exp(after + r_run)
        if masked:
            w = jnp.where(ci < ri, w, 0.0)
        ws.append(w)
    return us, sigs, ws


def sb_attn_fwd(qkv, name):
    lp = qkv.shape[0]
    bq = min(ROW_TILE, lp)
    nq = lp // bq
    assert nq <= 64

    def body(q_ref, k_ref, v_ref, o_ref, rs_ref):
        qi = pl.program_id(1)
        q = q_ref[...]
        lane = _lane((bq, LANES))
        lo = lane < 64
        qs = (jnp.where(lo, q, jnp.zeros_like(q)), jnp.where(lo, jnp.zeros_like(q), q))

        def step(kb, carry, masked):
            off = pl.multiple_of(kb * bq, bq)
            k = k_ref[pl.ds(off, bq), :]
            v = v_ref[pl.ds(off, bq), :]
            heads, rs = carry
            r_runs = [heads[j][1] for j in range(2)]
            for j in range(2):
                rs = jnp.where(lane == 64 * j + kb, r_runs[j], rs)
            us, _, ws = _sb_blocks(qs, k, r_runs, masked, bq)
            pvs = [_dot(w.astype(BF16), v) for w in ws]
            out = tuple((heads[j][0] + pvs[j], r_runs[j] + jnp.sum(us[j], axis=1, keepdims=True)) for j in range(2))
            return out, rs

        zero = (jnp.zeros((bq, LANES), F32), jnp.zeros((bq, 1), F32))
        carry = step(qi, ((zero, zero), jnp.zeros((bq, LANES), F32)), True)
        heads, rs = lax.fori_loop(0, qi, lambda t, c: step(qi - 1 - t, c, False), carry)
        o_ref[...] = jnp.where(lo, heads[0][0], heads[1][0])
        rs_ref[...] = rs

    blk = pl.BlockSpec((bq, LANES), lambda p, i: (i, p))
    return pl.pallas_call(
        body, name=name, grid=(2, nq),
        in_specs=[blk,
                  pl.BlockSpec((lp, LANES), lambda p, i: (0, 2 + p)),
                  pl.BlockSpec((lp, LANES), lambda p, i: (0, 4 + p))],
        out_specs=[blk, blk],
        out_shape=[jax.ShapeDtypeStruct((lp, SB_WIDTH), F32), jax.ShapeDtypeStruct((lp, SB_WIDTH), F32)],
        compiler_params=_cp("arbitrary", "arbitrary"),
    )(qkv, qkv, qkv)


def sb_attn_bwd(qkv, rs, d_o, name):
    lp = qkv.shape[0]
    bq = min(ROW_TILE, lp)
    nq = lp // bq

    def body(q_ref, k_ref, v_ref, rs_ref, do_ref, dq_ref, dk_ref, dv_ref):
        qi = pl.program_id(1)

        @pl.when(qi == 0)
        def _():
            dk_ref[...] = jnp.zeros_like(dk_ref)
            dv_ref[...] = jnp.zeros_like(dv_ref)

        q = q_ref[...]
        lane = _lane((bq, LANES))
        lo = lane < 64
        zq = jnp.zeros_like(q)
        qs = (jnp.where(lo, q, zq), jnp.where(lo, zq, q))
        dob = do_ref[...].astype(BF16)
        dos = (jnp.where(lo, dob, zq), jnp.where(lo, zq, dob))
        rs_blk = rs_ref[...]
        ri = lax.broadcasted_iota(jnp.int32, (bq, bq), 0)
        ci = lax.broadcasted_iota(jnp.int32, (bq, bq), 1)
        tbefore = (ri < ci).astype(BF16)

        def step(kb, carry, masked):
            off = pl.multiple_of(kb * bq, bq)
            k = k_ref[pl.ds(off, bq), :]
            v = v_ref[pl.ds(off, bq), :]
            r_rights = [jnp.sum(jnp.where(lane == 64 * j + kb, rs_blk, 0.0), axis=1, keepdims=True) for j in range(2)]
            dws = [_dot(dos[j], v, "nt") for j in range(2)]
            _, sigs, ws = _sb_blocks(qs, k, r_rights, masked, bq)
            gs = [ws[j] * dws[j] for j in range(2)]
            gparts = [_split_bf16(g) for g in gs]
            wbs = [w.astype(BF16) for w in ws]
            gbefores = [_dot(hi, tbefore) + _dot(lo, tbefore) for hi, lo in gparts]
            dv_acc = _dot(wbs[0], dos[0], "tn") + _dot(wbs[1], dos[1], "tn")
            dzbs = []
            for j in range(2):
                dz = gs[j] - sigs[j] * (gs[j] + gbefores[j] + carry[j][1])
                if masked:
                    dz = jnp.where(ci < ri, dz, 0.0)
                dzbs.append(dz.astype(BF16))
            dqs = [_dot(dzbs[j], k) for j in range(2)]
            dk_acc = _dot(dzbs[0], qs[0], "tn") + _dot(dzbs[1], qs[1], "tn")
            dk_ref[pl.ds(off, bq), :] += dk_acc
            dv_ref[pl.ds(off, bq), :] += dv_acc
            return tuple((carry[j][0] + dqs[j], carry[j][1] + jnp.sum(gs[j], axis=1, keepdims=True)) for j in range(2))

        zero = (jnp.zeros((bq, LANES), F32), jnp.zeros((bq, 1), F32))
        carry = lax.fori_loop(0, qi, lambda t, c: step(t, c, False), (zero, zero))
        carry = step(qi, carry, True)
        dq_ref[...] = jnp.where(lo, carry[0][0], carry[1][0]).astype(dq_ref.dtype)

    blk = pl.BlockSpec((bq, LANES), lambda p, i: (i, p))
    return pl.pallas_call(
        body, name=name, grid=(2, nq),
        in_specs=[blk,
                  pl.BlockSpec((lp, LANES), lambda p, i: (0, 2 + p)),
                  pl.BlockSpec((lp, LANES), lambda p, i: (0, 4 + p)),
                  blk, blk],
        out_specs=[blk,
                   pl.BlockSpec((lp, LANES), lambda p, i: (0, p)),
                   pl.BlockSpec((lp, LANES), lambda p, i: (0, p))],
        out_shape=[jax.ShapeDtypeStruct((lp, SB_WIDTH), BF16), jax.ShapeDtypeStruct((lp, SB_WIDTH), F32),
                   jax.ShapeDtypeStruct((lp, SB_WIDTH), F32)],
        compiler_params=_cp("arbitrary", "arbitrary"),
    )(qkv, qkv, qkv, rs, d_o)


def _mla_masks(p, bq):
    lane = _lane((bq, 2 * LANES))
    out = []
    for j in range(2):
        h = 2 * p + j
        nope = jnp.logical_and(lane >= 64 * j, lane < 64 * (j + 1))
        rope = jnp.logical_and(lane >= LANES + MLA_ROPE * h, lane < LANES + MLA_ROPE * (h + 1))
        out.append(jnp.logical_or(nope, rope))
    return out


def mla_attn_fwd(qc, kc, v, name):
    lp = qc.shape[0]
    bq = min(ROW_TILE, lp)
    nq = lp // bq

    def body(q_ref, k_ref, v_ref, o_ref, lse_ref):
        p = pl.program_id(0)
        qi = pl.program_id(1)
        q = q_ref[...]
        masks = _mla_masks(p, bq)
        qs = [jnp.where(mk, q, jnp.zeros_like(q)) for mk in masks]
        lo = _lane((bq, LANES)) < 64
        ri = lax.broadcasted_iota(jnp.int32, (bq, bq), 0)
        ci = lax.broadcasted_iota(jnp.int32, (bq, bq), 1)

        def step(kb, carry, masked):
            off = pl.multiple_of(kb * bq, bq)
            k = k_ref[pl.ds(off, bq), :]
            vv = v_ref[pl.ds(off, bq), :]
            ss = [_dot(qs[j], k, "nt") for j in range(2)]
            prs, alphas, stats = [], [], []
            for j in range(2):
                _, m_run, l_run = carry[j]
                s = ss[j]
                if masked:
                    s = jnp.where(ci <= ri, s, NEG)
                m_new = jnp.maximum(m_run, jnp.max(s, axis=1, keepdims=True))
                alpha = jnp.exp(m_run - m_new)
                pr = jnp.exp(s - m_new)
                prs.append(pr.astype(BF16))
                alphas.append(alpha)
                stats.append((m_new, l_run * alpha + jnp.sum(pr, axis=1, keepdims=True)))
            pvs = [_dot(prs[j], vv) for j in range(2)]
            return tuple((carry[j][0] * alphas[j] + pvs[j],) + stats[j] for j in range(2))

        zero = (jnp.zeros((bq, LANES), F32), jnp.full((bq, 1), NEG, F32), jnp.zeros((bq, 1), F32))
        carry = step(qi, (zero, zero), True)
        carry = lax.fori_loop(0, qi, lambda t, c: step(qi - 1 - t, c, False), carry)
        (a0, m0, l0), (a1, m1, l1) = carry
        o_ref[...] = jnp.where(lo, a0 / l0, a1 / l1)
        lse_ref[...] = jnp.where(lo, m0 + jnp.log(l0), m1 + jnp.log(l1))

    return pl.pallas_call(
        body, name=name, grid=(2, nq),
        in_specs=[pl.BlockSpec((bq, 2 * LANES), lambda p, i: (i, p)),
                  pl.BlockSpec((lp, 2 * LANES), lambda p, i: (0, p)),
                  pl.BlockSpec((lp, LANES), lambda p, i: (0, p))],
        out_specs=[pl.BlockSpec((bq, LANES), lambda p, i: (i, p)),
                   pl.BlockSpec((bq, LANES), lambda p, i: (i, p))],
        out_shape=[jax.ShapeDtypeStruct((lp, 2 * LANES), F32), jax.ShapeDtypeStruct((lp, 2 * LANES), F32)],
        compiler_params=_cp("arbitrary", "arbitrary"),
    )(qc, kc, v)


def mla_attn_bwd(qc, kc, v, o, lse, d_o, name):
    lp = qc.shape[0]
    bq = min(ROW_TILE, lp)
    nq = lp // bq

    def body(q_ref, k_ref, v_ref, o_ref, lse_ref, do_ref, dq_ref, dk_ref, dv_ref):
        p = pl.program_id(0)
        qi = pl.program_id(1)

        @pl.when(qi == 0)
        def _():
            dk_ref[...] = jnp.zeros_like(dk_ref)
            dv_ref[...] = jnp.zeros_like(dv_ref)

        q = q_ref[...]
        d_o = do_ref[...]
        masks = _mla_masks(p, bq)
        qs = [jnp.where(mk, q, jnp.zeros_like(q)) for mk in masks]
        lo = _lane((bq, LANES)) < 64
        dob = d_o.astype(BF16)
        zb = jnp.zeros_like(dob)
        dos = (jnp.where(lo, dob, zb), jnp.where(lo, zb, dob))
        od = o_ref[...] * d_o
        delta = (jnp.sum(jnp.where(lo, od, 0.0), axis=1, keepdims=True),
                 jnp.sum(jnp.where(lo, 0.0, od), axis=1, keepdims=True))
        lse_blk = lse_ref[...]
        lses = (lse_blk[:, 0:1], lse_blk[:, 64:65])
        ri = lax.broadcasted_iota(jnp.int32, (bq, bq), 0)
        ci = lax.broadcasted_iota(jnp.int32, (bq, bq), 1)

        def step(kb, carry, masked):
            off = pl.multiple_of(kb * bq, bq)
            k = k_ref[pl.ds(off, bq), :]
            vv = v_ref[pl.ds(off, bq), :]
            ss = [_dot(qs[j], k, "nt") for j in range(2)]
            dps = [_dot(dos[j], vv, "nt") for j in range(2)]
            prbs, dss = [], []
            for j in range(2):
                s = ss[j]
                if masked:
                    s = jnp.where(ci <= ri, s, NEG)
                pr = jnp.exp(s - lses[j])
                prbs.append(pr.astype(BF16))
                dss.append((pr * (dps[j] - delta[j])).astype(BF16))
            dv_acc = _dot(prbs[0], dos[0], "tn") + _dot(prbs[1], dos[1], "tn")
            dqs = [_dot(dss[j], k) for j in range(2)]
            dk_acc = _dot(dss[0], qs[0], "tn") + _dot(dss[1], qs[1], "tn")
            dk_ref[pl.ds(off, bq), :] += dk_acc
            dv_ref[pl.ds(off, bq), :] += dv_acc
            return tuple(carry[j] + dqs[j] for j in range(2))

        zero = jnp.zeros((bq, 2 * LANES), F32)
        carry = step(qi, (zero, zero), True)
        carry = lax.fori_loop(0, qi, lambda t, c: step(qi - 1 - t, c, False), carry)
        dq_ref[...] = jnp.where(masks[0], carry[0], 0.0) + jnp.where(masks[1], carry[1], 0.0)

    return pl.pallas_call(
        body, name=name, grid=(2, nq),
        in_specs=[pl.BlockSpec((bq, 2 * LANES), lambda p, i: (i, p)),
                  pl.BlockSpec((lp, 2 * LANES), lambda p, i: (0, p)),
                  pl.BlockSpec((lp, LANES), lambda p, i: (0, p)),
                  pl.BlockSpec((bq, LANES), lambda p, i: (i, p)),
                  pl.BlockSpec((bq, LANES), lambda p, i: (i, p)),
                  pl.BlockSpec((bq, LANES), lambda p, i: (i, p))],
        out_specs=[pl.BlockSpec((bq, 2 * LANES), lambda p, i: (i, p)),
                   pl.BlockSpec((lp, 2 * LANES), lambda p, i: (0, p)),
                   pl.BlockSpec((lp, LANES), lambda p, i: (0, p))],
        out_shape=[jax.ShapeDtypeStruct((lp, 4 * LANES), F32), jax.ShapeDtypeStruct((lp, 4 * LANES), F32),
                   jax.ShapeDtypeStruct((lp, 2 * LANES), F32)],
        compiler_params=_cp("arbitrary", "arbitrary"),
    )(qc, kc, v, o, lse, d_o)


def _mix_out(y_pre, z, o_sb, o_mla, g_ssd, g_sb, g_mla):
    return jnp.concatenate([_rms(y_pre * _silu(z), g_ssd), _rms(o_sb, g_sb), _rms(o_mla, g_mla)], axis=1)


def _ffn_act(up_a, up_b, halo_a, halo_b, w_a, w_b, b_a, b_b, i):
    ca = conv_fwd(up_a, halo_a, w_a, i) + b_a
    cb_ = conv_fwd(up_b, halo_b, w_b, i) + b_b
    return ca, cb_


def layer_fwd(h, w, cs, sn, l):
    lp = h.shape[0]
    nm = f"l{l}_"
    hn = rowwise(lambda i, n, x, g: _rms(x, g), [RI(h), PA(w["norm_mix_g"])], [RO(D_MODEL, BF16)], nm + "rms_mix", lp)[0]
    u = mm(hn, w["w_main"], "nn", F32, nm + "in_main", tn=768)
    qkv = mm(hn, w["w_sb"], "nn", BF16, nm + "in_sb")
    xbc_c = rowwise(lambda i, n, x, hl, cw, cb_: _silu(conv_fwd(x, hl, cw, i) + cb_),
                    [RI(u, SSD_XBC, 0), HP(u, SSD_XBC, 0), PA(w["ssd_conv_w"]), PA(w["ssd_conv_b"])],
                    [RO(SSD_XBC, F32)], nm + "ssd_conv", lp)[0]
    y_pre, hprev = ssd_fwd(xbc_c, u, w["dt_bias"], w["a_log"], w["d_skip"], nm + "ssd_fwd")
    o_sb, rs_sb = sb_attn_fwd(qkv, nm + "sb_fwd")
    qn, kvn = rowwise(lambda i, n, qa, ckv, gq, gkv: (_rms(qa, gq, MLA_Q_RANK), _rms(ckv, gkv)),
                      [RI(u, 256, U_QA // 256), RI(u, LANES, U_CKV // LANES), PA(w["q_norm_g"]), PA(w["kv_norm_g"])],
                      [RO(256, BF16), RO(LANES, BF16)], nm + "mla_rms", lp)
    qf = mm(qn, w["w_uq"], "nn", F32, nm + "mla_uq")
    kvf = mm(kvn, w["w_ukv"], "nn", F32, nm + "mla_ukv")

    def pack(i, n, qf_, kvf_, kr4, cos, sin):
        qf_ = qf_ * MLA_SCALE
        qr = qf_[:, 256:384]
        qr = qr * cos + rope_rot(qr) * sin
        kr = kr4 * cos + rope_rot(kr4) * sin
        qc = jnp.concatenate([qf_[:, 0:128], qr, qf_[:, 128:256], qr], axis=1)
        kc = jnp.concatenate([kvf_[:, 0:128], kr, kvf_[:, 128:256], kr], axis=1)
        return qc, kc, kvf_[:, 256:512]

    qc, kc, vv = rowwise(pack, [RI(qf), RI(kvf), RI(u, LANES, U_KR4 // LANES), RI(cs), RI(sn)],
                         [RO(512, BF16), RO(512, BF16), RO(256, BF16)], nm + "mla_pack", lp)
    o_mla, lse = mla_attn_fwd(qc, kc, vv, nm + "mla_fwd")
    cat = rowwise(lambda i, n, *a: _mix_out(*a),
                  [RI(y_pre), RI(u, SSD_WIDTH, U_Z // SSD_WIDTH), RI(o_sb), RI(o_mla),
                   PA(w["ssd_norm_g"]), PA(w["sb_norm_g"]), PA(w["mla_norm_g"])],
                  [RO(D_MODEL, BF16)], nm + "mix_out", lp)[0]
    h_mid = mm(cat, w["w_out"], "nn", F32, nm + "out_proj", add=h)
    hn2 = rowwise(lambda i, n, x, g: _rms(x, g), [RI(h_mid), PA(w["norm_ffn_g"])], [RO(D_MODEL, BF16)], nm + "rms_ffn", lp)[0]
    up_a = mm(hn2, w["w_up_a"], "nn", F32, nm + "up_a", tn=1408)
    up_b = mm(hn2, w["w_up_b"], "nn", F32, nm + "up_b", tn=1408)
    wc = 1408

    def act(i, n, ua, ub, ha, hb_, wa, wb, ba, bb_):
        ca, cb_ = _ffn_act(ua, ub, ha, hb_, wa, wb, ba, bb_, i)
        return _silu(ca) * cb_

    a_t = rowwise(act, [RI(up_a, wc, 0, True), RI(up_b, wc, 0, True), HP(up_a, wc, 0, True), HP(up_b, wc, 0, True),
                        PA(w["ffn_conv_w_a"], wc, 0, True), PA(w["ffn_conv_w_b"], wc, 0, True),
                        PA(w["ffn_conv_b_a"], wc, 0, True), PA(w["ffn_conv_b_b"], wc, 0, True)],
                  [RO(D_FF, BF16, wc, True)], nm + "ffn_act", lp, ncol=D_FF // wc)[0]
    h_out = mm(a_t, w["w_down"], "nn", F32, nm + "down", add=h_mid)
    saved = dict(h=h, hn=hn, u=u, qkv=qkv, xbc_c=xbc_c, y_pre=y_pre, hprev=hprev, o_sb=o_sb, rs_sb=rs_sb, qn=qn, kvn=kvn,
                 qc=qc, kc=kc, vv=vv, o_mla=o_mla, lse=lse, cat=cat, h_mid=h_mid, hn2=hn2, up_a=up_a, up_b=up_b, a_t=a_t)
    return h_out, saved


def layer_bwd(dh_out, w, s, cs, sn, l):
    lp = dh_out.shape[0]
    nm = f"l{l}b_"
    g = {}
    wc = 1408
    ncolf = D_FF // wc
    g["w_down"] = mm(s["a_t"], dh_out, "tn", F32, nm + "dw_down")
    d_act = mm(dh_out, w["w_down"], "nt", F32, nm + "d_act", tn=1408)

    def act_bwd(i, n, ua, ub, ha, hb_, wa, wb, ba, bb_, da_):
        ca, cb_ = _ffn_act(ua, ub, ha, hb_, wa, wb, ba, bb_, i)
        sg = jax.nn.sigmoid(ca)
        dca = da_ * cb_ * (sg * (1.0 + ca * (1.0 - sg)))
        dcb = da_ * (ca * sg)
        return (dca, dcb, conv_bwd_w(dca, ua, ha, i, FFN_CONV), conv_bwd_w(dcb, ub, hb_, i, FFN_CONV),
                jnp.sum(dca, axis=0, keepdims=True), jnp.sum(dcb, axis=0, keepdims=True))

    dca, dcb, g["ffn_conv_w_a"], g["ffn_conv_w_b"], g["ffn_conv_b_a"], g["ffn_conv_b_b"] = rowwise(
        act_bwd, [RI(s["up_a"], wc, 0, True), RI(s["up_b"], wc, 0, True), HP(s["up_a"], wc, 0, True),
                  HP(s["up_b"], wc, 0, True), PA(w["ffn_conv_w_a"], wc, 0, True), PA(w["ffn_conv_w_b"], wc, 0, True),
                  PA(w["ffn_conv_b_a"], wc, 0, True), PA(w["ffn_conv_b_b"], wc, 0, True), RI(d_act, wc, 0, True)],
        [RO(D_FF, F32, wc, True), RO(D_FF, F32, wc, True), AO(FFN_CONV, D_FF, wc, True), AO(FFN_CONV, D_FF, wc, True),
         AO(1, D_FF, wc, True), AO(1, D_FF, wc, True)], nm + "ffn_act_bwd", lp, ncol=ncolf)

    def conv_t(i, n, da_, db_, ha, hb_, wa, wb):
        return conv_bwd_data(da_, ha, wa, i, n), conv_bwd_data(db_, hb_, wb, i, n)

    dup_a, dup_b = rowwise(conv_t, [RI(dca, wc, 0, True), RI(dcb, wc, 0, True), HN(dca, wc, 0, True), HN(dcb, wc, 0, True),
                                    PA(w["ffn_conv_w_a"], wc, 0, True), PA(w["ffn_conv_w_b"], wc, 0, True)],
                           [RO(D_FF, BF16, wc, True), RO(D_FF, BF16, wc, True)], nm + "ffn_conv_t", lp, ncol=ncolf)
    g["w_up_a"] = mm(s["hn2"], dup_a, "tn", F32, nm + "dw_up_a", tn=1408)
    g["w_up_b"] = mm(s["hn2"], dup_b, "tn", F32, nm + "dw_up_b", tn=1408)
    dhn2 = mm(dup_a, w["w_up_a"], "nt", F32, nm + "dhn2_a", tk=1408)
    dhn2 = mm(dup_b, w["w_up_b"], "nt", F32, nm + "dhn2_b", add=dhn2, tk=1408)

    def rms_bwd(i, n, x, gg, dy, dres):
        _, vjp = jax.vjp(_rms, x, gg)
        dx, dg = vjp(dy)
        return dres + dx, dg

    dh_mid, g["norm_ffn_g"] = rowwise(rms_bwd, [RI(s["h_mid"]), PA(w["norm_ffn_g"]), RI(dhn2), RI(dh_out)],
                                      [RO(D_MODEL, F32), AO(1, D_MODEL)], nm + "rms_ffn_bwd", lp)
    g["w_out"] = mm(s["cat"], dh_mid, "tn", F32, nm + "dw_out")
    d_cat = mm(dh_mid, w["w_out"], "nt", F32, nm + "d_cat")
    u = s["u"]

    def mix_bwd(i, n, y_pre, z, o_sb, o_mla, g1, g2, g3, dcat):
        _, vjp = jax.vjp(_mix_out, y_pre, z, o_sb, o_mla, g1, g2, g3)
        return vjp(dcat)

    dy_pre, dz, do_sb, do_mla, g["ssd_norm_g"], g["sb_norm_g"], g["mla_norm_g"] = rowwise(
        mix_bwd, [RI(s["y_pre"]), RI(u, SSD_WIDTH, U_Z // SSD_WIDTH), RI(s["o_sb"]), RI(s["o_mla"]),
                  PA(w["ssd_norm_g"]), PA(w["sb_norm_g"]), PA(w["mla_norm_g"]), RI(d_cat)],
        [RO(SSD_WIDTH, F32), RO(SSD_WIDTH, BF16), RO(SB_WIDTH, F32), RO(256, F32),
         AO(1, SSD_WIDTH), AO(1, SB_WIDTH), AO(1, 256)], nm + "mix_out_bwd", lp)
    dxs, dbp, dcp, ddtp, pg = ssd_bwd(s["xbc_c"], u, w["dt_bias"], w["a_log"], w["d_skip"], s["hprev"], dy_pre, nm + "ssd_bwd")
    pg = pg.reshape(4, SUBLANES, LANES).sum(axis=0)
    g["dt_bias"], g["a_log"], g["d_skip"] = pg[0:1], pg[1:2], pg[2:3]

    def conv4_bwd(i, n, x, hl, cw, cb_, dxs_, dbp_, dcp_, ddtp_):
        pre = conv_fwd(x, hl, cw, i) + cb_
        d_b = jnp.concatenate([dbp_[:, 0:128] + dbp_[:, 128:256], dbp_[:, 256:384] + dbp_[:, 384:512]], axis=1)
        d_c = jnp.concatenate([dcp_[:, 0:128] + dcp_[:, 128:256], dcp_[:, 256:384] + dcp_[:, 384:512]], axis=1)
        d_out = jnp.concatenate([dxs_, d_b, d_c], axis=1)
        sg = jax.nn.sigmoid(pre)
        d_pre = d_out * (sg * (1.0 + pre * (1.0 - sg)))
        ddt = ddtp_[:, 0:128] + ddtp_[:, 128:256] + ddtp_[:, 256:384] + ddtp_[:, 384:512]
        return d_pre, ddt, conv_bwd_w(d_pre, x, hl, i, SSD_CONV), jnp.sum(d_pre, axis=0, keepdims=True)

    d_pre, ddt, g["ssd_conv_w"], g["ssd_conv_b"] = rowwise(
        conv4_bwd, [RI(u, SSD_XBC, 0), HP(u, SSD_XBC, 0), PA(w["ssd_conv_w"]), PA(w["ssd_conv_b"]),
                    RI(dxs), RI(dbp), RI(dcp), RI(ddtp)],
        [RO(SSD_XBC, F32), RO(LANES, BF16), AO(SSD_CONV, SSD_XBC), AO(1, SSD_XBC)], nm + "ssd_conv_bwd", lp)
    d_xbc = rowwise(lambda i, n, d, hn_, cw: conv_bwd_data(d, hn_, cw, i, n),
                    [RI(d_pre), HN(d_pre), PA(w["ssd_conv_w"])], [RO(SSD_XBC, BF16)], nm + "ssd_conv_t", lp)[0]
    dq_sb, dk_sb, dv_sb = sb_attn_bwd(s["qkv"], s["rs_sb"], do_sb, nm + "sb_bwd")
    dqkv = jnp.concatenate([dq_sb, dk_sb.astype(BF16), dv_sb.astype(BF16)], axis=1)
    dqc, dkc, dvv = mla_attn_bwd(s["qc"], s["kc"], s["vv"], s["o_mla"], s["lse"], do_mla, nm + "mla_bwd")

    def unpack(i, n, dqc_, dkc_, dvv_, cos, sin):
        dqr = dqc_[:, 128:256] + dqc_[:, 384:512]
        dqr = dqr * cos + rope_rot_t(dqr * sin)
        dkr = dkc_[:, 128:256] + dkc_[:, 384:512]
        dkr = dkr * cos + rope_rot_t(dkr * sin)
        dq = jnp.concatenate([dqc_[:, 0:128], dqc_[:, 256:384], dqr], axis=1) * MLA_SCALE
        dkv = jnp.concatenate([dkc_[:, 0:128], dkc_[:, 256:384], dvv_], axis=1)
        return dq, dkv, dkr

    dq, dkv, dkr4 = rowwise(unpack, [RI(dqc), RI(dkc), RI(dvv), RI(cs), RI(sn)],
                            [RO(384, BF16), RO(512, BF16), RO(LANES, BF16)], nm + "mla_unpack", lp)
    g["w_uq"] = mm(s["qn"], dq, "tn", F32, nm + "dw_uq")
    g["w_ukv"] = mm(s["kvn"], dkv, "tn", F32, nm + "dw_ukv")
    dqn = mm(dq, w["w_uq"], "nt", F32, nm + "dqn")
    dkvn = mm(dkv, w["w_ukv"], "nt", F32, nm + "dkvn")

    def mla_rms_bwd(i, n, qa, ckv, gq, gkv, dqn_, dkvn_):
        _, vjp = jax.vjp(lambda a, b, c, d: (_rms(a, c, MLA_Q_RANK), _rms(b, d)), qa, ckv, gq, gkv)
        return vjp((dqn_, dkvn_))

    dqa, dckv, g["q_norm_g"], g["kv_norm_g"] = rowwise(
        mla_rms_bwd, [RI(u, 256, U_QA // 256), RI(u, LANES, U_CKV // LANES), PA(w["q_norm_g"]), PA(w["kv_norm_g"]),
                      RI(dqn), RI(dkvn)],
        [RO(256, BF16), RO(LANES, BF16), AO(1, 256), AO(1, LANES)], nm + "mla_rms_bwd", lp)
    du = jnp.concatenate([d_xbc, dz, dqa, dckv, dkr4, ddt, jnp.zeros((lp, LANES), BF16)], axis=1)
    g["w_main"] = mm(s["hn"], du, "tn", F32, nm + "dw_main", tn=768)
    g["w_sb"] = mm(s["hn"], dqkv, "tn", F32, nm + "dw_sb")
    dhn = mm(du, w["w_main"], "nt", F32, nm + "dhn_main", tk=768)
    dhn = mm(dqkv, w["w_sb"], "nt", F32, nm + "dhn_sb", add=dhn)
    dh_in, g["norm_mix_g"] = rowwise(rms_bwd, [RI(s["h"]), PA(w["norm_mix_g"]), RI(dhn), RI(dh_mid)],
                                     [RO(D_MODEL, F32), AO(1, D_MODEL)], nm + "rms_mix_bwd", lp)
    return dh_in, g


_IN_CUTS = np.cumsum((512, 1024, 8, 256, 256, 256, 192, 128, 32))


def _pad_cols(a, n):
    return jnp.pad(a, ((0, 0), (0, n - a.shape[1])))


def prep_layer_weights(full, l):
    w_in = full["w_in"][l]
    c = _IN_CUTS
    z, xbc, dtr = w_in[:, :c[0]], w_in[:, c[0]:c[1]], w_in[:, c[1]:c[2]]
    q_sb, k_sb, v_sb = w_in[:, c[2]:c[3]], w_in[:, c[3]:c[4]], w_in[:, c[4]:c[5]]
    q_a, c_kv, k_r = w_in[:, c[5]:c[6]], w_in[:, c[6]:c[7]], w_in[:, c[7]:c[8]]
    w_main = jnp.concatenate([xbc, z, _pad_cols(q_a, 256), c_kv, k_r, k_r, k_r, k_r, _pad_cols(dtr, 256)], axis=1)
    assert w_main.shape[1] == U_MAIN
    row = lambda v, n=None: _pad_cols(v.reshape(1, -1).astype(F32), v.size if n is None else n)
    uq = full["mla_w_uq"][l].reshape(MLA_Q_RANK, 4, 96)
    w_uq = jnp.concatenate([uq[:, :, :64].reshape(MLA_Q_RANK, 256), uq[:, :, 64:].reshape(MLA_Q_RANK, 128)], axis=1)
    w_uq = jnp.pad(w_uq, ((0, 256 - MLA_Q_RANK), (0, 0)))
    ukv = full["mla_w_ukv"][l].reshape(MLA_KV_RANK, 4, 128)
    w_ukv = jnp.concatenate([ukv[:, :, :64].reshape(MLA_KV_RANK, 256), ukv[:, :, 64:].reshape(MLA_KV_RANK, 256)], axis=1)
    return dict(
        norm_mix_g=row(full["norm_mix_g"][l]), w_main=w_main, w_sb=jnp.concatenate([q_sb * SB_SCALE, k_sb, v_sb], axis=1),
        ssd_conv_w=full["ssd_conv_w"][l], ssd_conv_b=row(full["ssd_conv_b"][l]),
        dt_bias=row(full["ssd_dt_bias"][l], LANES), a_log=row(full["ssd_a_log"][l], LANES), d_skip=row(full["ssd_d"][l], LANES),
        ssd_norm_g=row(full["ssd_norm_g"][l]), sb_norm_g=row(full["sb_norm_g"][l]),
        q_norm_g=row(full["mla_q_norm_g"][l], 256), kv_norm_g=row(full["mla_kv_norm_g"][l]),
        w_uq=w_uq, w_ukv=w_ukv, mla_norm_g=row(full["mla_norm_g"][l]),
        w_out=full["w_out"][l], norm_ffn_g=row(full["norm_ffn_g"][l]),
        w_up_a=full["ffn_w_up"][l][:, :D_FF], w_up_b=full["ffn_w_up"][l][:, D_FF:],
        ffn_conv_w_a=full["ffn_conv_w"][l][:, :D_FF], ffn_conv_w_b=full["ffn_conv_w"][l][:, D_FF:],
        ffn_conv_b_a=row(full["ffn_conv_b"][l][:D_FF]), ffn_conv_b_b=row(full["ffn_conv_b"][l][D_FF:]),
        w_down=full["ffn_w_down"][l],
    )


def unprep_layer_grads(g):
    wm = g["w_main"]
    xbc, z = wm[:, U_XBC:U_XBC + 1024], wm[:, U_Z:U_Z + 512]
    q_a, c_kv = wm[:, U_QA:U_QA + MLA_Q_RANK], wm[:, U_CKV:U_CKV + 128]
    k_r = (wm[:, U_KR4:U_KR4 + 32] + wm[:, U_KR4 + 32:U_KR4 + 64] + wm[:, U_KR4 + 64:U_KR4 + 96] + wm[:, U_KR4 + 96:U_KR4 + 128])
    dtr = wm[:, U_DT:U_DT + SSD_HEADS]
    w_sb = g["w_sb"]
    w_in = jnp.concatenate([z, xbc, dtr, w_sb[:, :SB_WIDTH] * SB_SCALE, w_sb[:, SB_WIDTH:], q_a, c_kv, k_r], axis=1)
    guq = g["w_uq"][:MLA_Q_RANK]
    guq = jnp.concatenate([guq[:, :256].reshape(MLA_Q_RANK, 4, 64), guq[:, 256:].reshape(MLA_Q_RANK, 4, 32)], axis=2)
    gukv = g["w_ukv"]
    gukv = jnp.concatenate([gukv[:, :256].reshape(MLA_KV_RANK, 4, 64), gukv[:, 256:].reshape(MLA_KV_RANK, 4, 64)], axis=2)
    return dict(
        norm_mix_g=g["norm_mix_g"][0], w_in=w_in, ssd_conv_w=g["ssd_conv_w"], ssd_conv_b=g["ssd_conv_b"][0],
        ssd_dt_bias=g["dt_bias"][0, :SSD_HEADS], ssd_a_log=g["a_log"][0, :SSD_HEADS], ssd_d=g["d_skip"][0, :SSD_HEADS],
        ssd_norm_g=g["ssd_norm_g"][0], sb_norm_g=g["sb_norm_g"][0], mla_q_norm_g=g["q_norm_g"][0, :MLA_Q_RANK],
        mla_kv_norm_g=g["kv_norm_g"][0], mla_w_uq=guq.reshape(MLA_Q_RANK, 384), mla_w_ukv=gukv.reshape(MLA_KV_RANK, 512),
        mla_norm_g=g["mla_norm_g"][0], w_out=g["w_out"], norm_ffn_g=g["norm_ffn_g"][0],
        ffn_w_up=jnp.concatenate([g["w_up_a"], g["w_up_b"]], axis=1),
        ffn_conv_w=jnp.concatenate([g["ffn_conv_w_a"], g["ffn_conv_w_b"]], axis=1),
        ffn_conv_b=jnp.concatenate([g["ffn_conv_b_a"][0], g["ffn_conv_b_b"][0]], axis=0),
        ffn_w_down=g["w_down"],
    )


def rope_tables(lp):
    pos = jnp.arange(lp, dtype=F32)
    inv = 1.0 / (ROPE_BASE ** (jnp.arange(0, MLA_ROPE, 2, dtype=F32) / MLA_ROPE))
    ang = pos[:, None] * inv[None, :]
    ang = jnp.concatenate([ang, ang] * 4, axis=-1)
    return jnp.cos(ang), jnp.sin(ang)


def local_step(x_seq, target, full):
    seq = x_seq.shape[0]
    length = seq + N_META
    lp = -(-length // ROW_TILE) * ROW_TILE
    cs, sn = rope_tables(lp)
    h = jnp.concatenate([full["meta_tokens"].astype(F32), x_seq, jnp.zeros((lp - length, D_MODEL), F32)], axis=0)
    tgt = jnp.pad(target, ((N_META, lp - length), (0, 0)))
    ws = [prep_layer_weights(full, l) for l in range(DEPTH)]
    saved = []
    for l in range(DEPTH):
        h, s = layer_fwd(h, ws[l], cs, sn, l)
        saved.append(s)
    fg = full["final_norm_g"].reshape(1, D_MODEL).astype(F32)
    tm = min(ROW_TILE, lp)

    def loss_fn(i, n, x, g, t):
        rows = _rows_iota(x) + i * tm
        valid = jnp.logical_and(rows >= N_META, rows < length)

        def f(x_, g_):
            err = jnp.where(valid, _rms(x_, g_) - t, 0.0)
            return 0.5 * jnp.sum(err * err) * (1.0 / D_MODEL)

        val, (dx, dg) = jax.value_and_grad(f, argnums=(0, 1))(x, g)
        return dx, jnp.full((1, LANES), val, F32), dg

    dh, loss_row, g_final = rowwise(loss_fn, [RI(h), PA(fg), RI(tgt)], [RO(D_MODEL, F32), AO(1, LANES), AO(1, D_MODEL)],
                                    "loss_head", lp)
    grads = {}
    per_layer = [None] * DEPTH
    for l in reversed(range(DEPTH)):
        dh, g = layer_bwd(dh, ws[l], saved[l], cs, sn, l)
        per_layer[l] = unprep_layer_grads(g)
    for k in per_layer[0]:
        grads[k] = jnp.stack([per_layer[l][k] for l in range(DEPTH)], axis=0)
    grads["final_norm_g"] = g_final[0]
    grads["meta_tokens"] = dh[:N_META]
    return loss_row[0, 0], dh[N_META:length], grads


_ANY = pl.BlockSpec(memory_space=pl.ANY)


def chip_exchange(srcs, broadcast, name):
    n = len(srcs)
    flips = ((1, 0), (0, 1), (1, 1))

    def body(*refs):
        ins, outs = refs[:n], refs[n:2 * n]
        send_sems, recv_sems, loc_sems = refs[2 * n:]
        x, y, c = lax.axis_index("x"), lax.axis_index("y"), lax.axis_index("c")
        me = 2 * x + y
        copies = []
        for a in range(n):
            src_me = ins[a] if broadcast[a] else ins[a].at[me]
            cp = pltpu.make_async_copy(src_me, outs[a].at[me], loc_sems.at[a])
            cp.start()
            copies.append(cp)
            for k, (fx, fy) in enumerate(flips):
                px = 1 - x if fx else x
                py = 1 - y if fy else y
                src = ins[a] if broadcast[a] else ins[a].at[2 * px + py]
                rc = pltpu.make_async_remote_copy(src_ref=src, dst_ref=outs[a].at[me], send_sem=send_sems.at[a, k],
                                                  recv_sem=recv_sems.at[a, k], device_id=(px, py, c), device_id_type=MESH_ID)
                rc.start()
                copies.append(rc)
        for cp in copies:
            cp.wait()

    out_shape = [jax.ShapeDtypeStruct((N_CHIPS,) + (s.shape if b else s.shape[1:]), s.dtype) for s, b in zip(srcs, broadcast)]
    return pl.pallas_call(
        body, name=name, in_specs=[_ANY] * n, out_specs=[_ANY] * n, out_shape=out_shape,
        scratch_shapes=[pltpu.SemaphoreType.DMA((n, 3)), pltpu.SemaphoreType.DMA((n, 3)), pltpu.SemaphoreType.DMA((n,))],    )(*srcs)


def sibling_swap(srcs, name):
    n = len(srcs)

    def body(*refs):
        ins, outs = refs[:n], refs[n:2 * n]
        send_sems, recv_sems = refs[2 * n:]
        x, y, c = lax.axis_index("x"), lax.axis_index("y"), lax.axis_index("c")
        copies = []
        for a in range(n):
            rc = pltpu.make_async_remote_copy(src_ref=ins[a], dst_ref=outs[a], send_sem=send_sems.at[a],
                                              recv_sem=recv_sems.at[a], device_id=(x, y, 1 - c), device_id_type=MESH_ID)
            rc.start()
            copies.append(rc)
        for cp in copies:
            cp.wait()

    return pl.pallas_call(
        body, name=name, in_specs=[_ANY] * n, out_specs=[_ANY] * n,
        out_shape=[jax.ShapeDtypeStruct(s.shape, s.dtype) for s in srcs],
        scratch_shapes=[pltpu.SemaphoreType.DMA((n,)), pltpu.SemaphoreType.DMA((n,))],    )(*srcs)


WEIGHT_NAMES = ("meta_tokens", "norm_mix_g", "w_in", "ssd_conv_w", "ssd_conv_b", "ssd_dt_bias", "ssd_a_log", "ssd_d",
                "ssd_norm_g", "sb_norm_g", "mla_q_norm_g", "mla_kv_norm_g", "mla_w_uq", "mla_w_ukv", "mla_norm_g",
                "w_out", "norm_ffn_g", "ffn_w_up", "ffn_conv_w", "ffn_conv_b", "ffn_w_down", "final_norm_g")
SHARD_AXIS = {"meta_tokens": 1, "w_in": 2, "ssd_conv_w": 2, "mla_w_uq": 2, "mla_w_ukv": 2, "w_out": 1, "ffn_w_up": 2,
              "ffn_conv_w": 2, "ffn_w_down": 1}
SHARDED = tuple(n for n in WEIGHT_NAMES if n in SHARD_AXIS)
REPLICATED = tuple(n for n in WEIGHT_NAMES if n not in SHARD_AXIS)
GATHER_BF16 = ("w_in", "mla_w_uq", "mla_w_ukv", "w_out", "ffn_w_up", "ffn_w_down")
GATHER_F32 = ("meta_tokens", "ssd_conv_w", "ffn_conv_w")
PACK_ROWS = ROW_TILE


def pack(arrs, dtype):
    flat = jnp.concatenate([a.reshape(-1).astype(dtype) for a in arrs])
    per = PACK_ROWS * PACK_W
    total = -(-flat.size // per) * per
    return jnp.pad(flat, (0, total - flat.size)).reshape(total // PACK_W, PACK_W)


def unpack(buf, shapes):
    flat = buf.reshape(-1)
    out, off = [], 0
    for shp in shapes:
        size = int(np.prod(shp))
        out.append(flat[off:off + size].reshape(shp))
        off += size
    return out


def gather_weights(a):
    full = {n: a[n] for n in REPLICATED}
    bufs = [pack([a[n] for n in GATHER_BF16], BF16), pack([a[n] for n in GATHER_F32], F32)]
    got = chip_exchange(bufs, (True, True), "gather_weights")
    for names, g in ((GATHER_BF16, got[0]), (GATHER_F32, got[1])):
        pieces = [unpack(g[k], [a[n].shape for n in names]) for k in range(N_CHIPS)]
        for idx, n in enumerate(names):
            full[n] = jnp.concatenate([pieces[k][idx] for k in range(N_CHIPS)], axis=SHARD_AXIS[n])
    return full


def _adamw(i, n, p, q, w, m, v):
    g = p + q
    m = ADAM_B1 * m + (1.0 - ADAM_B1) * g
    v = ADAM_B2 * v + (1.0 - ADAM_B2) * jnp.square(g)
    m_hat = m / (1.0 - ADAM_B1 ** ADAM_STEP)
    v_hat = v / (1.0 - ADAM_B2 ** ADAM_STEP)
    delta = -ADAM_LR * (m_hat / (jnp.sqrt(v_hat) + ADAM_EPS) + ADAM_WD * w)
    return g, delta, m, v


def reduce_and_update(a, grads):
    sh_shapes = [a[n].shape for n in SHARDED]
    rep_shapes = [a[n].shape for n in REPLICATED]
    slabs = []
    for k in range(N_CHIPS):
        parts = []
        for n in SHARDED:
            ax = SHARD_AXIS[n]
            size = a[n].shape[ax]
            parts.append(lax.slice_in_dim(grads[n], k * size, (k + 1) * size, axis=ax))
        slabs.append(pack(parts, BF16))
    g_sh = jnp.stack(slabs, axis=0)
    g_rep = pack([grads[n] for n in REPLICATED], F32)
    got_sh, got_rep = chip_exchange([g_sh, g_rep], (False, True), "exchange_grads")
    sums = []
    for tag, got in (("sh", got_sh), ("rep", got_rep)):
        rows = got.shape[1]
        blk = rows // PACK_ROWS
        flat = got.reshape(N_CHIPS * rows, PACK_W)
        sums.append(rowwise(lambda i, n, g0, g1, g2, g3: ((g0.astype(F32) + g1.astype(F32)) + g2.astype(F32)) + g3.astype(F32),
                            [RI(flat, rblk=k * blk) for k in range(N_CHIPS)], [RO(PACK_W, F32)], "sum_chips_" + tag, rows,
                            tm=PACK_ROWS)[0])
    others = sibling_swap(sums, "swap_cores")
    outs = {}
    for tag, names, shapes, p, q in (("sh", SHARDED, sh_shapes, sums[0], others[0]),
                                     ("rep", REPLICATED, rep_shapes, sums[1], others[1])):
        packed = [pack([a[pre + n] for n in names], F32) for pre in ("", "m_", "v_")]
        res = rowwise(_adamw, [RI(p), RI(q)] + [RI(t) for t in packed], [RO(PACK_W, F32)] * 4, "adamw_" + tag,
                      p.shape[0], tm=PACK_ROWS)
        for kind, buf in zip(("grad", "delta", "new_m", "new_v"), res):
            for n, val in zip(names, unpack(buf, shapes)):
                outs[(kind, n)] = val
    return outs


INPUT_NAMES = ("x",) + WEIGHT_NAMES + ("loss_target",) + tuple("m_" + n for n in WEIGHT_NAMES) + tuple("v_" + n for n in WEIGHT_NAMES)


def kernel(x, meta_tokens, norm_mix_g, w_in, ssd_conv_w, ssd_conv_b, ssd_dt_bias, ssd_a_log, ssd_d, ssd_norm_g, sb_norm_g, mla_q_norm_g, mla_kv_norm_g, mla_w_uq, mla_w_ukv, mla_norm_g, w_out, norm_ffn_g, ffn_w_up, ffn_conv_w, ffn_conv_b, ffn_w_down, final_norm_g, loss_target, m_meta_tokens, m_norm_mix_g, m_w_in, m_ssd_conv_w, m_ssd_conv_b, m_ssd_dt_bias, m_ssd_a_log, m_ssd_d, m_ssd_norm_g, m_sb_norm_g, m_mla_q_norm_g, m_mla_kv_norm_g, m_mla_w_uq, m_mla_w_ukv, m_mla_norm_g, m_w_out, m_norm_ffn_g, m_ffn_w_up, m_ffn_conv_w, m_ffn_conv_b, m_ffn_w_down, m_final_norm_g, v_meta_tokens, v_norm_mix_g, v_w_in, v_ssd_conv_w, v_ssd_conv_b, v_ssd_dt_bias, v_ssd_a_log, v_ssd_d, v_ssd_norm_g, v_sb_norm_g, v_mla_q_norm_g, v_mla_kv_norm_g, v_mla_w_uq, v_mla_w_ukv, v_mla_norm_g, v_w_out, v_norm_ffn_g, v_ffn_w_up, v_ffn_conv_w, v_ffn_conv_b, v_ffn_w_down, v_final_norm_g):
    args = (x, meta_tokens, norm_mix_g, w_in, ssd_conv_w, ssd_conv_b, ssd_dt_bias, ssd_a_log, ssd_d, ssd_norm_g, sb_norm_g, mla_q_norm_g, mla_kv_norm_g, mla_w_uq, mla_w_ukv, mla_norm_g, w_out, norm_ffn_g, ffn_w_up, ffn_conv_w, ffn_conv_b, ffn_w_down, final_norm_g, loss_target, m_meta_tokens, m_norm_mix_g, m_w_in, m_ssd_conv_w, m_ssd_conv_b, m_ssd_dt_bias, m_ssd_a_log, m_ssd_d, m_ssd_norm_g, m_sb_norm_g, m_mla_q_norm_g, m_mla_kv_norm_g, m_mla_w_uq, m_mla_w_ukv, m_mla_norm_g, m_w_out, m_norm_ffn_g, m_ffn_w_up, m_ffn_conv_w, m_ffn_conv_b, m_ffn_w_down, m_final_norm_g, v_meta_tokens, v_norm_mix_g, v_w_in, v_ssd_conv_w, v_ssd_conv_b, v_ssd_dt_bias, v_ssd_a_log, v_ssd_d, v_ssd_norm_g, v_sb_norm_g, v_mla_q_norm_g, v_mla_kv_norm_g, v_mla_w_uq, v_mla_w_ukv, v_mla_norm_g, v_w_out, v_norm_ffn_g, v_ffn_w_up, v_ffn_conv_w, v_ffn_conv_b, v_ffn_w_down, v_final_norm_g)
    a = dict(zip(INPUT_NAMES, args, strict=True))
    full = gather_weights(a)
    loss, grad_x, grads = local_step(a["x"][0], a["loss_target"][0], full)
    loss = lax.psum(loss, ("x", "y", "c"))
    outs = reduce_and_update(a, grads)
    result = [loss, grad_x[None]]
    for kind in ("grad", "delta", "new_m", "new_v"):
        result += [outs[(kind, n)] for n in WEIGHT_NAMES]
    return tuple(result)
```

```python
import functools
import math

import numpy as np
import jax
import jax.numpy as jnp
from jax import lax
from jax.experimental import pallas as pl
from jax.experimental.pallas import tpu as pltpu

F32 = jnp.float32
BF16 = jnp.bfloat16
HIGHEST = lax.Precision.HIGHEST
MESH_ID = pl.DeviceIdType.MESH

D_MODEL = 1024
DEPTH = 2
N_META = 16
EPS = 1e-6
SSD_HEADS = 8
SSD_WIDTH = 512
SSD_XBC = 1024
SSD_CONV = 4
SB_WIDTH = 256
SB_SCALE = 64 ** -0.5
MLA_Q_RANK = 192
MLA_KV_RANK = 128
MLA_ROPE = 32
MLA_SCALE = 96 ** -0.5
ROPE_BASE = 10000.0
D_FF = 2816
FFN_CONV = 3
IN_COLS = 2664
N_CHIPS = 4

ADAM_LR = 0.001
ADAM_B1 = 0.9
ADAM_B2 = 0.999
ADAM_EPS = 1e-08
ADAM_WD = 0.01
ADAM_STEP = 10

LANES = 128
SUBLANES = 8
ROW_TILE = 256
VMEM_LIMIT = 56 * 1024 * 1024
PACK_W = 1024

U_XBC, U_Z, U_QA, U_CKV, U_KR4, U_DT, U_MAIN = 0, 1024, 1536, 1792, 1920, 2048, 2304
NEG = -1e30


def _cp(*sem):
    return pltpu.CompilerParams(dimension_semantics=sem if sem else None, vmem_limit_bytes=VMEM_LIMIT)


def _pick(dim, pref):
    if dim <= pref:
        return dim
    best = None
    for t in range(LANES, pref + 1, LANES):
        if dim % t == 0:
            best = t
    assert best is not None, (dim, pref)
    return best


def _dot(a, b, dims="nn", precision=None):
    dn = {"nn": (((1,), (0,)), ((), ())), "nt": (((1,), (1,)), ((), ())), "tn": (((0,), (0,)), ((), ()))}[dims]
    return lax.dot_general(a, b, dn, preferred_element_type=F32, precision=precision)


def _softplus(x):
    return jnp.maximum(x, 0.0) + jnp.log1p(jnp.exp(-jnp.abs(x)))


def _silu(x):
    return x * jax.nn.sigmoid(x)


def _rms(x, g, n=None):
    n = x.shape[-1] if n is None else n
    ms = jnp.sum(x * x, axis=-1, keepdims=True) * (1.0 / n)
    return x * lax.rsqrt(ms + EPS) * g


def mm(a, b, dims, out_dtype, name, add=None, tm=512, tn=1024, tk=1536):
    if dims == "nn":
        (m, k), (k2, n) = a.shape, b.shape
    elif dims == "nt":
        (m, k), (n, k2) = a.shape, b.shape
    else:
        (k, m), (k2, n) = a.shape, b.shape
    assert k == k2, (a.shape, b.shape, dims)
    tm, tn, tk = _pick(m, tm), _pick(n, tn), _pick(k, tk)
    nk = k // tk
    if dims == "tn":
        a_spec = pl.BlockSpec((tk, tm), lambda i, j, kk: (kk, i))
    else:
        a_spec = pl.BlockSpec((tm, tk), lambda i, j, kk: (i, kk))
    if dims == "nt":
        b_spec = pl.BlockSpec((tn, tk), lambda i, j, kk: (j, kk))
    else:
        b_spec = pl.BlockSpec((tk, tn), lambda i, j, kk: (kk, j))
    o_spec = pl.BlockSpec((tm, tn), lambda i, j, kk: (i, j))
    has_add = add is not None

    def body(*refs):
        if has_add:
            a_ref, b_ref, add_ref, o_ref, acc_ref = refs
        else:
            a_ref, b_ref, o_ref, acc_ref = refs
        kk = pl.program_id(2)
        part = _dot(a_ref[...].astype(BF16), b_ref[...].astype(BF16), dims)

        @pl.when(kk == 0)
        def _():
            acc_ref[...] = part

        @pl.when(kk > 0)
        def _():
            acc_ref[...] += part

        @pl.when(kk == nk - 1)
        def _():
            r = acc_ref[...]
            if has_add:
                r = r + add_ref[...].astype(F32)
            o_ref[...] = r.astype(o_ref.dtype)

    in_specs = [a_spec, b_spec] + ([o_spec] if has_add else [])
    args = (a, b) + ((add,) if has_add else ())
    return pl.pallas_call(
        body, name=name, grid=(m // tm, n // tn, nk),
        in_specs=in_specs, out_specs=o_spec,
        out_shape=jax.ShapeDtypeStruct((m, n), out_dtype),
        scratch_shapes=[pltpu.VMEM((tm, tn), F32)],
        compiler_params=_cp("parallel", "parallel", "arbitrary"),
    )(*args)


def RI(arr, width=None, cidx=0, cv=False, rblk=0):
    return ("row" if rblk == 0 else ("row", rblk), arr, arr.shape[1] if width is None else width, cidx, cv)


def HP(arr, width=None, cidx=0, cv=False):
    return ("prev", arr, arr.shape[1] if width is None else width, cidx, cv)


def HN(arr, width=None, cidx=0, cv=False):
    return ("next", arr, arr.shape[1] if width is None else width, cidx, cv)


def PA(arr, width=None, cidx=0, cv=False):
    return ("par", arr, arr.shape[1] if width is None else width, cidx, cv)


def RO(ncols, dtype, width=None, cv=False):
    return ("row", ncols, dtype, ncols if width is None else width, cv)


def AO(nrows, ncols, width=None, cv=False):
    return ("acc", (nrows, ncols), F32, ncols if width is None else width, cv)


def rowwise(fn, ins, outs, name, rows, tm=ROW_TILE, ncol=1):
    tm = min(tm, rows)
    assert rows % tm == 0
    nrow = rows // tm
    hb = tm // SUBLANES
    last_hb = rows // SUBLANES - 1
    in_specs, args = [], []
    for kind, arr, width, cidx, cv in ins:
        def cmap(j, cidx=cidx, cv=cv):
            return cidx + j if cv else cidx
        if kind == "row":
            spec = pl.BlockSpec((tm, width), lambda j, i, cmap=cmap: (i, cmap(j)))
        elif isinstance(kind, tuple):
            spec = pl.BlockSpec((tm, width), lambda j, i, cmap=cmap, rblk=kind[1]: (i + rblk, cmap(j)))
        elif kind == "prev":
            spec = pl.BlockSpec((SUBLANES, width), lambda j, i, cmap=cmap: (jnp.maximum(i * hb - 1, 0), cmap(j)))
        elif kind == "next":
            spec = pl.BlockSpec((SUBLANES, width), lambda j, i, cmap=cmap: (jnp.minimum((i + 1) * hb, last_hb), cmap(j)))
        else:
            spec = pl.BlockSpec((arr.shape[0], width), lambda j, i, cmap=cmap: (0, cmap(j)))
        in_specs.append(spec)
        args.append(arr)
    out_specs, out_shapes, acc_cv = [], [], []
    for kind, shp, dtype, width, cv in outs:
        if kind == "row":
            out_specs.append(pl.BlockSpec((tm, width), lambda j, i, cv=cv: (i, j if cv else 0)))
            out_shapes.append(jax.ShapeDtypeStruct((rows, shp), dtype))
            acc_cv.append(None)
        else:
            out_specs.append(pl.BlockSpec((shp[0], width), lambda j, i, cv=cv: (0, j if cv else 0)))
            out_shapes.append(jax.ShapeDtypeStruct(shp, dtype))
            acc_cv.append(cv)
    n_in = len(ins)

    def body(*refs):
        j = pl.program_id(0)
        i = pl.program_id(1)
        vals = fn(i, nrow, *[r[...] for r in refs[:n_in]])
        if not isinstance(vals, (tuple, list)):
            vals = (vals,)
        for o_ref, v, cv in zip(refs[n_in:], vals, acc_cv):
            if cv is None:
                o_ref[...] = v.astype(o_ref.dtype)
            else:
                first = (i == 0) if cv else jnp.logical_and(i == 0, j == 0)

                @pl.when(first)
                def _(o_ref=o_ref, v=v):
                    o_ref[...] = v.astype(o_ref.dtype)

                @pl.when(jnp.logical_not(first))
                def _(o_ref=o_ref, v=v):
                    o_ref[...] += v.astype(o_ref.dtype)

    res = pl.pallas_call(
        body, name=name, grid=(ncol, nrow), in_specs=in_specs, out_specs=out_specs, out_shape=out_shapes,
        compiler_params=_cp("arbitrary", "arbitrary"),
    )(*args)
    return res


def _rows_iota(x):
    return lax.broadcasted_iota(jnp.int32, x.shape, 0)


def shift_down(x, halo, s):
    if s == 0:
        return x
    tm = x.shape[0]
    top = pltpu.roll(halo, s, 0)
    if tm > SUBLANES:
        top = jnp.concatenate([top, jnp.zeros((tm - SUBLANES, x.shape[1]), x.dtype)], axis=0)
    return jnp.where(_rows_iota(x) < s, top, pltpu.roll(x, s, 0))


def shift_up(x, halo, s):
    if s == 0:
        return x
    tm = x.shape[0]
    bot = pltpu.roll(halo, SUBLANES - s, 0)
    if tm > SUBLANES:
        bot = jnp.concatenate([jnp.zeros((tm - SUBLANES, x.shape[1]), x.dtype), bot], axis=0)
    return jnp.where(_rows_iota(x) >= tm - s, bot, pltpu.roll(x, tm - s, 0))


def conv_fwd(x, halo, w, i):
    kw = w.shape[0]
    halo = jnp.where(i == 0, 0.0, halo)
    out = None
    for k in range(kw):
        term = w[k:k + 1, :] * shift_down(x, halo, kw - 1 - k)
        out = term if out is None else out + term
    return out


def conv_bwd_data(dy, halo_next, w, i, n):
    kw = w.shape[0]
    halo_next = jnp.where(i == n - 1, 0.0, halo_next)
    out = None
    for k in range(kw):
        term = w[k:k + 1, :] * shift_up(dy, halo_next, kw - 1 - k)
        out = term if out is None else out + term
    return out


def conv_bwd_w(dy, x, halo, i, kw):
    halo = jnp.where(i == 0, 0.0, halo)
    rows = [jnp.sum(dy * shift_down(x, halo, kw - 1 - k), axis=0, keepdims=True) for k in range(kw)]
    return jnp.concatenate(rows, axis=0)


def _lane(shape):
    return lax.broadcasted_iota(jnp.int32, shape, 1)


def rope_rot(x):
    lane = _lane(x.shape) % MLA_ROPE
    return jnp.where(lane < MLA_ROPE // 2, -pltpu.roll(x, LANES - MLA_ROPE // 2, 1), pltpu.roll(x, MLA_ROPE // 2, 1))


def rope_rot_t(g):
    lane = _lane(g.shape) % MLA_ROPE
    return jnp.where(lane < MLA_ROPE // 2, pltpu.roll(g, LANES - MLA_ROPE // 2, 1), -pltpu.roll(g, MLA_ROPE // 2, 1))


def _ssd_common(p, xs, dt_raw, bias, alog, q):
    lane = _lane((q, LANES))
    pre = dt_raw + bias
    dt = jnp.where(lane < SSD_HEADS, _softplus(pre), 0.0)
    a_row = -jnp.exp(alog)
    d_a = dt * a_row
    ri = lax.broadcasted_iota(jnp.int32, (q, q), 0)
    ci = lax.broadcasted_iota(jnp.int32, (q, q), 1)
    causal = ri >= ci
    acs = _dot(causal.astype(F32), d_a, "nn", HIGHEST)
    acs_t = acs.T
    subl = lax.broadcasted_iota(jnp.int32, (LANES, q), 0)
    h0, h1 = 2 * p, 2 * p + 1

    def col(arr, h):
        return jnp.sum(jnp.where(lane == h, arr, 0.0), axis=1, keepdims=True)

    def row(arr_t, h):
        return jnp.sum(jnp.where(subl == h, arr_t, 0.0), axis=0, keepdims=True)

    lo = lane < 64
    cols = (col(acs, h0), col(acs, h1))
    rows = (row(acs_t, h0), row(acs_t, h1))
    acs_p = jnp.where(lo, cols[0], cols[1])
    dt_p = jnp.where(lo, col(dt, h0), col(dt, h1))
    tots = (cols[0][q - 1:q, :], cols[1][q - 1:q, :])
    tot_p = jnp.where(lo[0:1, :], tots[0], tots[1])
    lms = tuple(jnp.exp(jnp.where(causal, cols[j] - rows[j], NEG)) for j in range(2))
    return dict(lane=lane, lo=lo, pre=pre, dt=dt, a_row=a_row, h=(h0, h1), acs_p=acs_p, dt_p=dt_p, tots=tots,
                tot_p=tot_p, lms=lms, ri=ri, ci=ci, eacs=jnp.exp(acs_p), dte=jnp.exp(tot_p - acs_p), x=xs * dt_p)


def _pick_lane(row_arr, h):
    return jnp.sum(jnp.where(_lane(row_arr.shape) == h, row_arr, 0.0), axis=1, keepdims=True)


def ssd_fwd(xbc_c, u_main, bias_row, alog_row, d_row, name):
    lp = xbc_c.shape[0]
    q = min(ROW_TILE, lp)
    nc = lp // q
    dt_blk = U_DT // LANES

    def body(xs_ref, b_ref, c_ref, dt_ref, bias_ref, alog_ref, d_ref, y_ref, hp_ref, h_scr):
        p = pl.program_id(0)
        c = pl.program_id(1)

        @pl.when(c == 0)
        def _():
            h_scr[...] = jnp.zeros_like(h_scr)

        xs = xs_ref[...]
        bb = b_ref[...].astype(BF16)
        cb_ = c_ref[...].astype(BF16)
        s = _ssd_common(p, xs, dt_ref[...], bias_ref[...], alog_ref[...], q)
        lo = s["lo"]
        g = _dot(cb_, bb, "nt")
        y = jnp.zeros((q, LANES), F32)
        for j in range(2):
            m = (g * s["lms"][j]).astype(BF16)
            xj = jnp.where(lo if j == 0 else jnp.logical_not(lo), s["x"], 0.0).astype(BF16)
            y = y + _dot(m, xj)
        hp = h_scr[...]
        hp_ref[...] = hp
        y = y + _dot(cb_, hp.astype(BF16), "nt") * s["eacs"]
        d_p = jnp.where(lo[0:1, :], _pick_lane(d_ref[...], s["h"][0]), _pick_lane(d_ref[...], s["h"][1]))
        y_ref[...] = y + d_p * xs
        sub_lo = lax.broadcasted_iota(jnp.int32, (LANES, LANES), 0) < 64
        etot = jnp.where(sub_lo, jnp.exp(s["tots"][0]), jnp.exp(s["tots"][1]))
        h_scr[...] = hp * etot + _dot((s["x"] * s["dte"]).astype(BF16), bb, "tn")

    grp = lambda p: p // 2
    in_specs = [
        pl.BlockSpec((q, LANES), lambda p, c: (c, p)),
        pl.BlockSpec((q, LANES), lambda p, c: (c, 4 + grp(p))),
        pl.BlockSpec((q, LANES), lambda p, c: (c, 6 + grp(p))),
        pl.BlockSpec((q, LANES), lambda p, c: (c, dt_blk)),
        pl.BlockSpec((1, LANES), lambda p, c: (0, 0)),
        pl.BlockSpec((1, LANES), lambda p, c: (0, 0)),
        pl.BlockSpec((1, LANES), lambda p, c: (0, 0)),
    ]
    out_specs = [
        pl.BlockSpec((q, LANES), lambda p, c: (c, p)),
        pl.BlockSpec((None, None, LANES, LANES), lambda p, c: (p, c, 0, 0)),
    ]
    return pl.pallas_call(
        body, name=name, grid=(4, nc), in_specs=in_specs, out_specs=out_specs,
        out_shape=[jax.ShapeDtypeStruct((lp, SSD_WIDTH), F32), jax.ShapeDtypeStruct((4, nc, LANES, LANES), F32)],
        scratch_shapes=[pltpu.VMEM((LANES, LANES), F32)],
        compiler_params=_cp("arbitrary", "arbitrary"),
    )(xbc_c, xbc_c, xbc_c, u_main, bias_row, alog_row, d_row)


def ssd_bwd(xbc_c, u_main, bias_row, alog_row, d_row, hprev, dy, name):
    lp = xbc_c.shape[0]
    q = min(ROW_TILE, lp)
    nc = lp // q
    dt_blk = U_DT // LANES

    def body(xs_ref, b_ref, c_ref, dt_ref, bias_ref, alog_ref, d_ref, hp_ref, dy_ref,
             dxs_ref, db_ref, dc_ref, ddt_ref, pg_ref, dh_scr):
        p = pl.program_id(0)
        cc = pl.program_id(1)

        @pl.when(cc == 0)
        def _():
            dh_scr[...] = jnp.zeros_like(dh_scr)
            pg_ref[...] = jnp.zeros_like(pg_ref)

        xs = xs_ref[...]
        bb = b_ref[...].astype(BF16)
        cb_ = c_ref[...].astype(BF16)
        s = _ssd_common(p, xs, dt_ref[...], bias_ref[...], alog_ref[...], q)
        lane, lo, x = s["lane"], s["lo"], s["x"]
        h0, h1 = s["h"]
        d_y = dy_ref[...]
        hp = hp_ref[...]
        hpb = hp.astype(BF16)
        dhn = dh_scr[...]
        dhnb = dhn.astype(BF16)
        xd = x * s["dte"]
        g = _dot(cb_, bb, "nt")
        dxdiag = jnp.zeros((q, LANES), F32)
        dg = jnp.zeros((q, q), F32)
        row_part, col_part = [], []
        for j in range(2):
            mj = lo if j == 0 else jnp.logical_not(lo)
            lm = s["lms"][j]
            m32 = g * lm
            xj = jnp.where(mj, x, 0.0).astype(BF16)
            dyj = jnp.where(mj, d_y, 0.0).astype(BF16)
            dxdiag = dxdiag + _dot(m32.astype(BF16), dyj, "tn")
            dm = _dot(dyj, xj, "nt")
            dg = dg + dm * lm
            wm = dm * m32
            row_part.append(jnp.sum(wm, axis=1, keepdims=True))
            col_part.append(jnp.sum(wm, axis=0, keepdims=True))
        dgb = dg.astype(BF16)
        d_c = _dot(dgb, bb)
        d_b = _dot(dgb, cb_, "tn")
        yoff = _dot(cb_, hpb, "nt") * s["eacs"]
        d_t = (d_y * s["eacs"]).astype(BF16)
        d_c = d_c + _dot(d_t, hpb)
        d_hp = _dot(d_t, cb_, "tn")
        dxd = _dot(bb, dhnb, "nt")
        d_b = d_b + _dot(xd.astype(BF16), dhnb)
        d_x = dxdiag + dxd * s["dte"]
        r = dxd * xd
        a_terms = d_y * yoff - r

        def gsum(arr):
            return (jnp.sum(jnp.where(lo, arr, 0.0), axis=1, keepdims=True),
                    jnp.sum(jnp.where(lo, 0.0, arr), axis=1, keepdims=True))

        dacs = gsum(a_terms)
        rs = gsum(r)
        hh = dhn * hp
        sub_lo = lax.broadcasted_iota(jnp.int32, (LANES, LANES), 0) < 64
        hsum = (jnp.sum(jnp.where(sub_lo, hh, 0.0), keepdims=True), jnp.sum(jnp.where(sub_lo, 0.0, hh), keepdims=True))
        last = lax.broadcasted_iota(jnp.int32, (q, 1), 0) == q - 1
        etots = (jnp.exp(s["tots"][0]), jnp.exp(s["tots"][1]))
        ddacs = jnp.zeros((q, LANES), F32)
        for j, h in enumerate((h0, h1)):
            dtot = jnp.sum(rs[j], keepdims=True) + hsum[j] * etots[j]
            dj = dacs[j] + row_part[j] + jnp.where(last, dtot, 0.0)
            ddacs = ddacs + jnp.where(lane == h, dj, 0.0)
        subl = lax.broadcasted_iota(jnp.int32, (LANES, q), 0)
        cols_t = jnp.where(subl == h0, col_part[0], 0.0) + jnp.where(subl == h1, col_part[1], 0.0)
        ddacs = ddacs - cols_t.T
        anti = (s["ri"] <= s["ci"]).astype(F32)
        da = _dot(anti, ddacs, "nn", HIGHEST)
        ddt_own = gsum(d_x * xs)
        ddt = jnp.where(lane == h0, ddt_own[0], 0.0) + jnp.where(lane == h1, ddt_own[1], 0.0) + da * s["a_row"]
        draw = ddt * jax.nn.sigmoid(s["pre"])
        ddt_ref[...] = draw
        d_p = jnp.where(lo[0:1, :], _pick_lane(d_ref[...], h0), _pick_lane(d_ref[...], h1))
        dxs_ref[...] = d_p * d_y + d_x * s["dt_p"]
        db_ref[...] = d_b
        dc_ref[...] = d_c
        dds = gsum(d_y * xs)
        lane1 = lane[0:1, :]
        dd_row = (jnp.where(lane1 == h0, jnp.sum(dds[0], keepdims=True), 0.0)
                  + jnp.where(lane1 == h1, jnp.sum(dds[1], keepdims=True), 0.0))
        dbias_row = jnp.sum(draw, axis=0, keepdims=True)
        dalog_row = jnp.sum(da * s["dt"], axis=0, keepdims=True) * s["a_row"]
        sub8 = lax.broadcasted_iota(jnp.int32, (SUBLANES, LANES), 0)
        pg_ref[...] += (jnp.where(sub8 == 0, dbias_row, 0.0) + jnp.where(sub8 == 1, dalog_row, 0.0)
                        + jnp.where(sub8 == 2, dd_row, 0.0))
        etot = jnp.where(sub_lo, etots[0], etots[1])
        dh_scr[...] = d_hp + etot * dhn

    grp = lambda p: p // 2
    rc = lambda c: nc - 1 - c
    in_specs = [
        pl.BlockSpec((q, LANES), lambda p, c: (rc(c), p)),
        pl.BlockSpec((q, LANES), lambda p, c: (rc(c), 4 + grp(p))),
        pl.BlockSpec((q, LANES), lambda p, c: (rc(c), 6 + grp(p))),
        pl.BlockSpec((q, LANES), lambda p, c: (rc(c), dt_blk)),
        pl.BlockSpec((1, LANES), lambda p, c: (0, 0)),
        pl.BlockSpec((1, LANES), lambda p, c: (0, 0)),
        pl.BlockSpec((1, LANES), lambda p, c: (0, 0)),
        pl.BlockSpec((None, None, LANES, LANES), lambda p, c: (p, rc(c), 0, 0)),
        pl.BlockSpec((q, LANES), lambda p, c: (rc(c), p)),
    ]
    out_specs = [
        pl.BlockSpec((q, LANES), lambda p, c: (rc(c), p)),
        pl.BlockSpec((q, LANES), lambda p, c: (rc(c), p)),
        pl.BlockSpec((q, LANES), lambda p, c: (rc(c), p)),
        pl.BlockSpec((q, LANES), lambda p, c: (rc(c), p)),
        pl.BlockSpec((SUBLANES, LANES), lambda p, c: (p, 0)),
    ]
    wide = jax.ShapeDtypeStruct((lp, 4 * LANES), F32)
    return pl.pallas_call(
        body, name=name, grid=(4, nc), in_specs=in_specs, out_specs=out_specs,
        out_shape=[wide, wide, wide, wide, jax.ShapeDtypeStruct((4 * SUBLANES, LANES), F32)],
        scratch_shapes=[pltpu.VMEM((LANES, LANES), F32)],
        compiler_params=_cp("arbitrary", "arbitrary"),
    )(xbc_c, xbc_c, xbc_c, u_main, bias_row, alog_row, d_row, hprev, dy)


def _sb_blocks(qs, ks, r_runs, masked, bq):
    ri = lax.broadcasted_iota(jnp.int32, (bq, bq), 0)
    ci = lax.broadcasted_iota(jnp.int32, (bq, bq), 1)
    tri_after = (ri > ci).astype(BF16)
    zs = [_dot(qj, kj, "nt") for qj, kj in zip(qs, ks)]
    us, sigs, ubs = [], [], []
    for z in zs:
        u = -(jnp.maximum(z, 0.0) + jnp.log(1.0 + jnp.exp(-jnp.abs(z))))
        sigs.append(jnp.exp(z + u))
        if masked:
            u = jnp.where(ci < ri, u, 0.0)
        us.append(u)
        ubs.append(u.astype(BF16))
    afters = [_dot(ub, tri_after) for ub in ubs]
    ws = []
    for sig, after, r_run in zip(sigs, afters, r_runs):
        w = sig * jnp.exp(after + r_run)
        if masked:
            w = jnp.where(ci < ri, w, 0.0)
        ws.append(w)
    return us, sigs, ws


def _split_heads(x, lo):
    out = []
    zero = jnp.zeros((x.shape[0], LANES), x.dtype)
    for p in range(2):
        xp = x[:, LANES * p:LANES * (p + 1)]
        out += [jnp.where(lo, xp, zero), jnp.where(lo, zero, xp)]
    return out


def _per_head(x):
    return [x[:, :LANES], x[:, :LANES], x[:, LANES:], x[:, LANES:]]


def _resident(shape, col):
    return pl.BlockSpec(shape, lambda i: (0, col), pipeline_mode=pl.Buffered(1))


def sb_attn_fwd(qkv, name):
    lp = qkv.shape[0]
    bq = min(ROW_TILE, lp)
    nq = lp // bq
    assert nq <= 64

    def body(q_ref, k_ref, v_ref, o_ref, rs_ref):
        qi = pl.program_id(0)
        lane = _lane((bq, LANES))
        lo = lane < 64
        qs = _split_heads(q_ref[...], lo)

        def step(kb, carry, masked):
            off = pl.multiple_of(kb * bq, bq)
            ks = _per_head(k_ref[pl.ds(off, bq), :])
            vs = _per_head(v_ref[pl.ds(off, bq), :])
            heads, rss = carry
            r_runs = [heads[h][1] for h in range(4)]
            rss = list(rss)
            for h in range(4):
                rss[h // 2] = jnp.where(lane == 64 * (h % 2) + kb, r_runs[h], rss[h // 2])
            us, _, ws = _sb_blocks(qs, ks, r_runs, masked, bq)
            pvs = [_dot(ws[h].astype(BF16), vs[h]) for h in range(4)]
            out = tuple((heads[h][0] + pvs[h], r_runs[h] + jnp.sum(us[h], axis=1, keepdims=True)) for h in range(4))
            return out, tuple(rss)

        zero = (jnp.zeros((bq, LANES), F32), jnp.zeros((bq, 1), F32))
        zr = jnp.zeros((bq, LANES), F32)
        carry = step(qi, ((zero,) * 4, (zr, zr)), True)
        heads, rss = lax.fori_loop(0, qi, lambda t, c: step(qi - 1 - t, c, False), carry)
        o_ref[...] = jnp.concatenate([jnp.where(lo, heads[0][0], heads[1][0]), jnp.where(lo, heads[2][0], heads[3][0])], axis=1)
        rs_ref[...] = jnp.concatenate(list(rss), axis=1)

    blk = pl.BlockSpec((bq, 2 * LANES), lambda i: (i, 0))
    return pl.pallas_call(
        body, name=name, grid=(nq,),
        in_specs=[blk, _resident((lp, 2 * LANES), 1), _resident((lp, 2 * LANES), 2)],
        out_specs=[blk, blk],
        out_shape=[jax.ShapeDtypeStruct((lp, SB_WIDTH), F32), jax.ShapeDtypeStruct((lp, SB_WIDTH), F32)],
        compiler_params=_cp("arbitrary"),
    )(qkv, qkv, qkv)


def sb_attn_bwd(qkv, rs, d_o, name):
    lp = qkv.shape[0]
    bq = min(ROW_TILE, lp)
    nq = lp // bq

    def body(q_ref, k_ref, v_ref, rs_ref, do_ref, dq_ref, dk_ref, dv_ref):
        qi = pl.program_id(0)

        @pl.when(qi == 0)
        def _():
            dk_ref[...] = jnp.zeros_like(dk_ref)
            dv_ref[...] = jnp.zeros_like(dv_ref)

        lane = _lane((bq, LANES))
        lo = lane < 64
        qs = _split_heads(q_ref[...], lo)
        dos = _split_heads(do_ref[...].astype(BF16), lo)
        rs_blk = rs_ref[...]
        ri = lax.broadcasted_iota(jnp.int32, (bq, bq), 0)
        ci = lax.broadcasted_iota(jnp.int32, (bq, bq), 1)
        tbefore = (ri < ci).astype(BF16)

        def step(kb, carry, masked):
            off = pl.multiple_of(kb * bq, bq)
            ks = _per_head(k_ref[pl.ds(off, bq), :])
            vs = _per_head(v_ref[pl.ds(off, bq), :])
            r_rights = [jnp.sum(jnp.where(lane == 64 * (h % 2) + kb, rs_blk[:, LANES * (h // 2):LANES * (h // 2 + 1)], 0.0),
                                axis=1, keepdims=True) for h in range(4)]
            dws = [_dot(dos[h], vs[h], "nt") for h in range(4)]
            _, sigs, ws = _sb_blocks(qs, ks, r_rights, masked, bq)
            gs = [ws[h] * dws[h] for h in range(4)]
            gbs = [g.astype(BF16) for g in gs]
            wbs = [w.astype(BF16) for w in ws]
            gbefores = [_dot(gb, tbefore) for gb in gbs]
            dv_acc = [_dot(wbs[2 * p], dos[2 * p], "tn") + _dot(wbs[2 * p + 1], dos[2 * p + 1], "tn") for p in range(2)]
            dzbs = []
            for h in range(4):
                dz = gs[h] - sigs[h] * (gs[h] + gbefores[h] + carry[h][1])
                if masked:
                    dz = jnp.where(ci < ri, dz, 0.0)
                dzbs.append(dz.astype(BF16))
            dqs = [_dot(dzbs[h], ks[h]) for h in range(4)]
            dk_acc = [_dot(dzbs[2 * p], qs[2 * p], "tn") + _dot(dzbs[2 * p + 1], qs[2 * p + 1], "tn") for p in range(2)]
            dk_ref[pl.ds(off, bq), :] += jnp.concatenate(dk_acc, axis=1)
            dv_ref[pl.ds(off, bq), :] += jnp.concatenate(dv_acc, axis=1)
            return tuple((carry[h][0] + dqs[h], carry[h][1] + jnp.sum(gs[h], axis=1, keepdims=True)) for h in range(4))

        zero = (jnp.zeros((bq, LANES), F32), jnp.zeros((bq, 1), F32))
        carry = lax.fori_loop(0, qi, lambda t, c: step(t, c, False), (zero,) * 4)
        carry = step(qi, carry, True)
        dq_ref[...] = jnp.concatenate([jnp.where(lo, carry[0][0], carry[1][0]), jnp.where(lo, carry[2][0], carry[3][0])],
                                      axis=1).astype(dq_ref.dtype)

    blk = pl.BlockSpec((bq, 2 * LANES), lambda i: (i, 0))
    return pl.pallas_call(
        body, name=name, grid=(nq,),
        in_specs=[blk, _resident((lp, 2 * LANES), 1), _resident((lp, 2 * LANES), 2), blk, blk],
        out_specs=[blk, _resident((lp, 2 * LANES), 0), _resident((lp, 2 * LANES), 0)],
        out_shape=[jax.ShapeDtypeStruct((lp, SB_WIDTH), BF16), jax.ShapeDtypeStruct((lp, SB_WIDTH), F32),
                   jax.ShapeDtypeStruct((lp, SB_WIDTH), F32)],
        compiler_params=_cp("arbitrary"),
    )(qkv, qkv, qkv, rs, d_o)


def _mla_masks(bq):
    lane = _lane((bq, 2 * LANES))
    out = []
    for h in range(4):
        j = h % 2
        nope = jnp.logical_and(lane >= 64 * j, lane < 64 * (j + 1))
        rope = jnp.logical_and(lane >= LANES + MLA_ROPE * h, lane < LANES + MLA_ROPE * (h + 1))
        out.append(jnp.logical_or(nope, rope))
    return out


def _mla_split_q(q, masks):
    zero = jnp.zeros((q.shape[0], 2 * LANES), q.dtype)
    return [jnp.where(masks[h], q[:, 2 * LANES * (h // 2):2 * LANES * (h // 2 + 1)], zero) for h in range(4)]


def _mla_per_head_k(k):
    return [k[:, :2 * LANES], k[:, :2 * LANES], k[:, 2 * LANES:], k[:, 2 * LANES:]]


def mla_attn_fwd(qc, kc, v, name):
    lp = qc.shape[0]
    bq = min(ROW_TILE, lp)
    nq = lp // bq

    def body(q_ref, k_ref, v_ref, o_ref, lse_ref):
        qi = pl.program_id(0)
        qs = _mla_split_q(q_ref[...], _mla_masks(bq))
        lo = _lane((bq, LANES)) < 64
        ri = lax.broadcasted_iota(jnp.int32, (bq, bq), 0)
        ci = lax.broadcasted_iota(jnp.int32, (bq, bq), 1)

        def step(kb, carry, masked):
            off = pl.multiple_of(kb * bq, bq)
            ks = _mla_per_head_k(k_ref[pl.ds(off, bq), :])
            vs = _per_head(v_ref[pl.ds(off, bq), :])
            ss = [_dot(qs[h], ks[h], "nt") for h in range(4)]
            prs, alphas, stats = [], [], []
            for h in range(4):
                _, m_run, l_run = carry[h]
                s = ss[h]
                if masked:
                    s = jnp.where(ci <= ri, s, NEG)
                m_new = jnp.maximum(m_run, jnp.max(s, axis=1, keepdims=True))
                alpha = jnp.exp(m_run - m_new)
                pr = jnp.exp(s - m_new)
                prs.append(pr.astype(BF16))
                alphas.append(alpha)
                stats.append((m_new, l_run * alpha + jnp.sum(pr, axis=1, keepdims=True)))
            pvs = [_dot(prs[h], vs[h]) for h in range(4)]
            return tuple((carry[h][0] * alphas[h] + pvs[h],) + stats[h] for h in range(4))

        zero = (jnp.zeros((bq, LANES), F32), jnp.full((bq, 1), NEG, F32), jnp.zeros((bq, 1), F32))
        carry = step(qi, (zero,) * 4, True)
        carry = lax.fori_loop(0, qi, lambda t, c: step(qi - 1 - t, c, False), carry)
        outs = [a / l for a, _, l in carry]
        lses = [m + jnp.log(l) for _, m, l in carry]
        o_ref[...] = jnp.concatenate([jnp.where(lo, outs[0], outs[1]), jnp.where(lo, outs[2], outs[3])], axis=1)
        lse_ref[...] = jnp.concatenate([jnp.where(lo, lses[0], lses[1]), jnp.where(lo, lses[2], lses[3])], axis=1)

    blk = pl.BlockSpec((bq, 2 * LANES), lambda i: (i, 0))
    return pl.pallas_call(
        body, name=name, grid=(nq,),
        in_specs=[pl.BlockSpec((bq, 4 * LANES), lambda i: (i, 0)), _resident((lp, 4 * LANES), 0), _resident((lp, 2 * LANES), 0)],
        out_specs=[blk, blk],
        out_shape=[jax.ShapeDtypeStruct((lp, 2 * LANES), F32), jax.ShapeDtypeStruct((lp, 2 * LANES), F32)],
        compiler_params=_cp("arbitrary"),
    )(qc, kc, v)


def mla_attn_bwd(qc, kc, v, o, lse, d_o, name):
    lp = qc.shape[0]
    bq = min(ROW_TILE, lp)
    nq = lp // bq

    def body(q_ref, k_ref, v_ref, o_ref, lse_ref, do_ref, dq_ref, dk_ref, dv_ref):
        qi = pl.program_id(0)

        @pl.when(qi == 0)
        def _():
            dk_ref[...] = jnp.zeros_like(dk_ref)
            dv_ref[...] = jnp.zeros_like(dv_ref)

        d_o = do_ref[...]
        masks = _mla_masks(bq)
        qs = _mla_split_q(q_ref[...], masks)
        lo = _lane((bq, LANES)) < 64
        dos = _split_heads(d_o.astype(BF16), lo)
        od = o_ref[...] * d_o
        lse_blk = lse_ref[...]
        delta, lses = [], []
        for h in range(4):
            odp = od[:, LANES * (h // 2):LANES * (h // 2 + 1)]
            delta.append(jnp.sum(jnp.where(lo, odp, 0.0) if h % 2 == 0 else jnp.where(lo, 0.0, odp), axis=1, keepdims=True))
            c0 = LANES * (h // 2) + 64 * (h % 2)
            lses.append(lse_blk[:, c0:c0 + 1])
        ri = lax.broadcasted_iota(jnp.int32, (bq, bq), 0)
        ci = lax.broadcasted_iota(jnp.int32, (bq, bq), 1)

        def step(kb, carry, masked):
            off = pl.multiple_of(kb * bq, bq)
            ks = _mla_per_head_k(k_ref[pl.ds(off, bq), :])
            vs = _per_head(v_ref[pl.ds(off, bq), :])
            ss = [_dot(qs[h], ks[h], "nt") for h in range(4)]
            dps = [_dot(dos[h], vs[h], "nt") for h in range(4)]
            prbs, dss = [], []
            for h in range(4):
                s = ss[h]
                if masked:
                    s = jnp.where(ci <= ri, s, NEG)
                pr = jnp.exp(s - lses[h])
                prbs.append(pr.astype(BF16))
                dss.append((pr * (dps[h] - delta[h])).astype(BF16))
            dv_acc = [_dot(prbs[2 * p], dos[2 * p], "tn") + _dot(prbs[2 * p + 1], dos[2 * p + 1], "tn") for p in range(2)]
            dqs = [_dot(dss[h], ks[h]) for h in range(4)]
            dk_acc = [_dot(dss[2 * p], qs[2 * p], "tn") + _dot(dss[2 * p + 1], qs[2 * p + 1], "tn") for p in range(2)]
            dk_ref[pl.ds(off, bq), :] += jnp.concatenate(dk_acc, axis=1)
            dv_ref[pl.ds(off, bq), :] += jnp.concatenate(dv_acc, axis=1)
            return tuple(carry[h] + dqs[h] for h in range(4))

        zero = jnp.zeros((bq, 2 * LANES), F32)
        carry = step(qi, (zero,) * 4, True)
        carry = lax.fori_loop(0, qi, lambda t, c: step(qi - 1 - t, c, False), carry)
        dq_ref[...] = jnp.concatenate([jnp.where(masks[0], carry[0], 0.0) + jnp.where(masks[1], carry[1], 0.0),
                                       jnp.where(masks[2], carry[2], 0.0) + jnp.where(masks[3], carry[3], 0.0)], axis=1)

    blk = pl.BlockSpec((bq, 2 * LANES), lambda i: (i, 0))
    wide = pl.BlockSpec((bq, 4 * LANES), lambda i: (i, 0))
    return pl.pallas_call(
        body, name=name, grid=(nq,),
        in_specs=[wide, _resident((lp, 4 * LANES), 0), _resident((lp, 2 * LANES), 0), blk, blk, blk],
        out_specs=[wide, _resident((lp, 4 * LANES), 0), _resident((lp, 2 * LANES), 0)],
        out_shape=[jax.ShapeDtypeStruct((lp, 4 * LANES), F32), jax.ShapeDtypeStruct((lp, 4 * LANES), F32),
                   jax.ShapeDtypeStruct((lp, 2 * LANES), F32)],
        compiler_params=_cp("arbitrary"),
    )(qc, kc, v, o, lse, d_o)


def _mix_out(y_pre, z, o_sb, o_mla, g_ssd, g_sb, g_mla):
    return jnp.concatenate([_rms(y_pre * _silu(z), g_ssd), _rms(o_sb, g_sb), _rms(o_mla, g_mla)], axis=1)


def _ffn_act(up_a, up_b, halo_a, halo_b, w_a, w_b, b_a, b_b, i):
    ca = conv_fwd(up_a, halo_a, w_a, i) + b_a
    cb_ = conv_fwd(up_b, halo_b, w_b, i) + b_b
    return ca, cb_


def layer_fwd(h, w, cs, sn, l):
    lp = h.shape[0]
    nm = f"l{l}_"
    hn = rowwise(lambda i, n, x, g: _rms(x, g), [RI(h), PA(w["norm_mix_g"])], [RO(D_MODEL, BF16)], nm + "rms_mix", lp)[0]
    u = mm(hn, w["w_main"], "nn", F32, nm + "in_main", tn=768)
    qkv = mm(hn, w["w_sb"], "nn", BF16, nm + "in_sb")
    xbc_c = rowwise(lambda i, n, x, hl, cw, cb_: _silu(conv_fwd(x, hl, cw, i) + cb_),
                    [RI(u, SSD_XBC, 0), HP(u, SSD_XBC, 0), PA(w["ssd_conv_w"]), PA(w["ssd_conv_b"])],
                    [RO(SSD_XBC, F32)], nm + "ssd_conv", lp)[0]
    y_pre, hprev = ssd_fwd(xbc_c, u, w["dt_bias"], w["a_log"], w["d_skip"], nm + "ssd_fwd")
    o_sb, rs_sb = sb_attn_fwd(qkv, nm + "sb_fwd")
    qn, kvn = rowwise(lambda i, n, qa, ckv, gq, gkv: (_rms(qa, gq, MLA_Q_RANK), _rms(ckv, gkv)),
                      [RI(u, 256, U_QA // 256), RI(u, LANES, U_CKV // LANES), PA(w["q_norm_g"]), PA(w["kv_norm_g"])],
                      [RO(256, BF16), RO(LANES, BF16)], nm + "mla_rms", lp)
    qf = mm(qn, w["w_uq"], "nn", F32, nm + "mla_uq")
    kvf = mm(kvn, w["w_ukv"], "nn", F32, nm + "mla_ukv")

    def pack(i, n, qf_, kvf_, kr4, cos, sin):
        qf_ = qf_ * MLA_SCALE
        qr = qf_[:, 256:384]
        qr = qr * cos + rope_rot(qr) * sin
        kr = kr4 * cos + rope_rot(kr4) * sin
        qc = jnp.concatenate([qf_[:, 0:128], qr, qf_[:, 128:256], qr], axis=1)
        kc = jnp.concatenate([kvf_[:, 0:128], kr, kvf_[:, 128:256], kr], axis=1)
        return qc, kc, kvf_[:, 256:512]

    qc, kc, vv = rowwise(pack, [RI(qf), RI(kvf), RI(u, LANES, U_KR4 // LANES), RI(cs), RI(sn)],
                         [RO(512, BF16), RO(512, BF16), RO(256, BF16)], nm + "mla_pack", lp)
    o_mla, lse = mla_attn_fwd(qc, kc, vv, nm + "mla_fwd")
    cat = rowwise(lambda i, n, *a: _mix_out(*a),
                  [RI(y_pre), RI(u, SSD_WIDTH, U_Z // SSD_WIDTH), RI(o_sb), RI(o_mla),
                   PA(w["ssd_norm_g"]), PA(w["sb_norm_g"]), PA(w["mla_norm_g"])],
                  [RO(D_MODEL, BF16)], nm + "mix_out", lp)[0]
    h_mid = mm(cat, w["w_out"], "nn", F32, nm + "out_proj", add=h)
    hn2 = rowwise(lambda i, n, x, g: _rms(x, g), [RI(h_mid), PA(w["norm_ffn_g"])], [RO(D_MODEL, BF16)], nm + "rms_ffn", lp)[0]
    up_a = mm(hn2, w["w_up_a"], "nn", F32, nm + "up_a", tn=1408)
    up_b = mm(hn2, w["w_up_b"], "nn", F32, nm + "up_b", tn=1408)
    wc = 1408

    def act(i, n, ua, ub, ha, hb_, wa, wb, ba, bb_):
        ca, cb_ = _ffn_act(ua, ub, ha, hb_, wa, wb, ba, bb_, i)
        return _silu(ca) * cb_

    a_t = rowwise(act, [RI(up_a, wc, 0, True), RI(up_b, wc, 0, True), HP(up_a, wc, 0, True), HP(up_b, wc, 0, True),
                        PA(w["ffn_conv_w_a"], wc, 0, True), PA(w["ffn_conv_w_b"], wc, 0, True),
                        PA(w["ffn_conv_b_a"], wc, 0, True), PA(w["ffn_conv_b_b"], wc, 0, True)],
                  [RO(D_FF, BF16, wc, True)], nm + "ffn_act", lp, ncol=D_FF // wc)[0]
    h_out = mm(a_t, w["w_down"], "nn", F32, nm + "down", add=h_mid)
    saved = dict(h=h, hn=hn, u=u, qkv=qkv, xbc_c=xbc_c, y_pre=y_pre, hprev=hprev, o_sb=o_sb, rs_sb=rs_sb, qn=qn, kvn=kvn,
                 qc=qc, kc=kc, vv=vv, o_mla=o_mla, lse=lse, cat=cat, h_mid=h_mid, hn2=hn2, up_a=up_a, up_b=up_b, a_t=a_t)
    return h_out, saved


def layer_bwd(dh_out, w, s, cs, sn, l):
    lp = dh_out.shape[0]
    nm = f"l{l}b_"
    g = {}
    wc = 1408
    ncolf = D_FF // wc
    g["w_down"] = mm(s["a_t"], dh_out, "tn", F32, nm + "dw_down")
    d_act = mm(dh_out, w["w_down"], "nt", F32, nm + "d_act", tn=1408)

    def act_bwd(i, n, ua, ub, ha, hb_, wa, wb, ba, bb_, da_):
        ca, cb_ = _ffn_act(ua, ub, ha, hb_, wa, wb, ba, bb_, i)
        sg = jax.nn.sigmoid(ca)
        dca = da_ * cb_ * (sg * (1.0 + ca * (1.0 - sg)))
        dcb = da_ * (ca * sg)
        return (dca, dcb, conv_bwd_w(dca, ua, ha, i, FFN_CONV), conv_bwd_w(dcb, ub, hb_, i, FFN_CONV),
                jnp.sum(dca, axis=0, keepdims=True), jnp.sum(dcb, axis=0, keepdims=True))

    dca, dcb, g["ffn_conv_w_a"], g["ffn_conv_w_b"], g["ffn_conv_b_a"], g["ffn_conv_b_b"] = rowwise(
        act_bwd, [RI(s["up_a"], wc, 0, True), RI(s["up_b"], wc, 0, True), HP(s["up_a"], wc, 0, True),
                  HP(s["up_b"], wc, 0, True), PA(w["ffn_conv_w_a"], wc, 0, True), PA(w["ffn_conv_w_b"], wc, 0, True),
                  PA(w["ffn_conv_b_a"], wc, 0, True), PA(w["ffn_conv_b_b"], wc, 0, True), RI(d_act, wc, 0, True)],
        [RO(D_FF, F32, wc, True), RO(D_FF, F32, wc, True), AO(FFN_CONV, D_FF, wc, True), AO(FFN_CONV, D_FF, wc, True),
         AO(1, D_FF, wc, True), AO(1, D_FF, wc, True)], nm + "ffn_act_bwd", lp, ncol=ncolf)

    def conv_t(i, n, da_, db_, ha, hb_, wa, wb):
        return conv_bwd_data(da_, ha, wa, i, n), conv_bwd_data(db_, hb_, wb, i, n)

    dup_a, dup_b = rowwise(conv_t, [RI(dca, wc, 0, True), RI(dcb, wc, 0, True), HN(dca, wc, 0, True), HN(dcb, wc, 0, True),
                                    PA(w["ffn_conv_w_a"], wc, 0, True), PA(w["ffn_conv_w_b"], wc, 0, True)],
                           [RO(D_FF, BF16, wc, True), RO(D_FF, BF16, wc, True)], nm + "ffn_conv_t", lp, ncol=ncolf)
    g["w_up_a"] = mm(s["hn2"], dup_a, "tn", F32, nm + "dw_up_a", tn=1408)
    g["w_up_b"] = mm(s["hn2"], dup_b, "tn", F32, nm + "dw_up_b", tn=1408)
    dhn2 = mm(dup_a, w["w_up_a"], "nt", F32, nm + "dhn2_a", tk=1408)
    dhn2 = mm(dup_b, w["w_up_b"], "nt", F32, nm + "dhn2_b", add=dhn2, tk=1408)

    def rms_bwd(i, n, x, gg, dy, dres):
        _, vjp = jax.vjp(_rms, x, gg)
        dx, dg = vjp(dy)
        return dres + dx, dg

    dh_mid, g["norm_ffn_g"] = rowwise(rms_bwd, [RI(s["h_mid"]), PA(w["norm_ffn_g"]), RI(dhn2), RI(dh_out)],
                                      [RO(D_MODEL, F32), AO(1, D_MODEL)], nm + "rms_ffn_bwd", lp)
    g["w_out"] = mm(s["cat"], dh_mid, "tn", F32, nm + "dw_out")
    d_cat = mm(dh_mid, w["w_out"], "nt", F32, nm + "d_cat")
    u = s["u"]

    def mix_bwd(i, n, y_pre, z, o_sb, o_mla, g1, g2, g3, dcat):
        _, vjp = jax.vjp(_mix_out, y_pre, z, o_sb, o_mla, g1, g2, g3)
        return vjp(dcat)

    dy_pre, dz, do_sb, do_mla, g["ssd_norm_g"], g["sb_norm_g"], g["mla_norm_g"] = rowwise(
        mix_bwd, [RI(s["y_pre"]), RI(u, SSD_WIDTH, U_Z // SSD_WIDTH), RI(s["o_sb"]), RI(s["o_mla"]),
                  PA(w["ssd_norm_g"]), PA(w["sb_norm_g"]), PA(w["mla_norm_g"]), RI(d_cat)],
        [RO(SSD_WIDTH, F32), RO(SSD_WIDTH, BF16), RO(SB_WIDTH, F32), RO(256, F32),
         AO(1, SSD_WIDTH), AO(1, SB_WIDTH), AO(1, 256)], nm + "mix_out_bwd", lp)
    dxs, dbp, dcp, ddtp, pg = ssd_bwd(s["xbc_c"], u, w["dt_bias"], w["a_log"], w["d_skip"], s["hprev"], dy_pre, nm + "ssd_bwd")
    pg = pg.reshape(4, SUBLANES, LANES).sum(axis=0)
    g["dt_bias"], g["a_log"], g["d_skip"] = pg[0:1], pg[1:2], pg[2:3]

    def conv4_bwd(i, n, x, hl, cw, cb_, dxs_, dbp_, dcp_, ddtp_):
        pre = conv_fwd(x, hl, cw, i) + cb_
        d_b = jnp.concatenate([dbp_[:, 0:128] + dbp_[:, 128:256], dbp_[:, 256:384] + dbp_[:, 384:512]], axis=1)
        d_c = jnp.concatenate([dcp_[:, 0:128] + dcp_[:, 128:256], dcp_[:, 256:384] + dcp_[:, 384:512]], axis=1)
        d_out = jnp.concatenate([dxs_, d_b, d_c], axis=1)
        sg = jax.nn.sigmoid(pre)
        d_pre = d_out * (sg * (1.0 + pre * (1.0 - sg)))
        ddt = ddtp_[:, 0:128] + ddtp_[:, 128:256] + ddtp_[:, 256:384] + ddtp_[:, 384:512]
        return d_pre, ddt, conv_bwd_w(d_pre, x, hl, i, SSD_CONV), jnp.sum(d_pre, axis=0, keepdims=True)

    d_pre, ddt, g["ssd_conv_w"], g["ssd_conv_b"] = rowwise(
        conv4_bwd, [RI(u, SSD_XBC, 0), HP(u, SSD_XBC, 0), PA(w["ssd_conv_w"]), PA(w["ssd_conv_b"]),
                    RI(dxs), RI(dbp), RI(dcp), RI(ddtp)],
        [RO(SSD_XBC, F32), RO(LANES, BF16), AO(SSD_CONV, SSD_XBC), AO(1, SSD_XBC)], nm + "ssd_conv_bwd", lp)
    d_xbc = rowwise(lambda i, n, d, hn_, cw: conv_bwd_data(d, hn_, cw, i, n),
                    [RI(d_pre), HN(d_pre), PA(w["ssd_conv_w"])], [RO(SSD_XBC, BF16)], nm + "ssd_conv_t", lp)[0]
    dq_sb, dk_sb, dv_sb = sb_attn_bwd(s["qkv"], s["rs_sb"], do_sb, nm + "sb_bwd")
    dqkv = jnp.concatenate([dq_sb, dk_sb.astype(BF16), dv_sb.astype(BF16)], axis=1)
    dqc, dkc, dvv = mla_attn_bwd(s["qc"], s["kc"], s["vv"], s["o_mla"], s["lse"], do_mla, nm + "mla_bwd")

    def unpack(i, n, dqc_, dkc_, dvv_, cos, sin):
        dqr = dqc_[:, 128:256] + dqc_[:, 384:512]
        dqr = dqr * cos + rope_rot_t(dqr * sin)
        dkr = dkc_[:, 128:256] + dkc_[:, 384:512]
        dkr = dkr * cos + rope_rot_t(dkr * sin)
        dq = jnp.concatenate([dqc_[:, 0:128], dqc_[:, 256:384], dqr], axis=1) * MLA_SCALE
        dkv = jnp.concatenate([dkc_[:, 0:128], dkc_[:, 256:384], dvv_], axis=1)
        return dq, dkv, dkr

    dq, dkv, dkr4 = rowwise(unpack, [RI(dqc), RI(dkc), RI(dvv), RI(cs), RI(sn)],
                            [RO(384, BF16), RO(512, BF16), RO(LANES, BF16)], nm + "mla_unpack", lp)
    g["w_uq"] = mm(s["qn"], dq, "tn", F32, nm + "dw_uq")
    g["w_ukv"] = mm(s["kvn"], dkv, "tn", F32, nm + "dw_ukv")
    dqn = mm(dq, w["w_uq"], "nt", F32, nm + "dqn")
    dkvn = mm(dkv, w["w_ukv"], "nt", F32, nm + "dkvn")

    def mla_rms_bwd(i, n, qa, ckv, gq, gkv, dqn_, dkvn_):
        _, vjp = jax.vjp(lambda a, b, c, d: (_rms(a, c, MLA_Q_RANK), _rms(b, d)), qa, ckv, gq, gkv)
        return vjp((dqn_, dkvn_))

    dqa, dckv, g["q_norm_g"], g["kv_norm_g"] = rowwise(
        mla_rms_bwd, [RI(u, 256, U_QA // 256), RI(u, LANES, U_CKV // LANES), PA(w["q_norm_g"]), PA(w["kv_norm_g"]),
                      RI(dqn), RI(dkvn)],
        [RO(256, BF16), RO(LANES, BF16), AO(1, 256), AO(1, LANES)], nm + "mla_rms_bwd", lp)
    du = jnp.concatenate([d_xbc, dz, dqa, dckv, dkr4, ddt, jnp.zeros((lp, LANES), BF16)], axis=1)
    g["w_main"] = mm(s["hn"], du, "tn", F32, nm + "dw_main", tn=768)
    g["w_sb"] = mm(s["hn"], dqkv, "tn", F32, nm + "dw_sb")
    dhn = mm(du, w["w_main"], "nt", F32, nm + "dhn_main", tk=768)
    dhn = mm(dqkv, w["w_sb"], "nt", F32, nm + "dhn_sb", add=dhn)
    dh_in, g["norm_mix_g"] = rowwise(rms_bwd, [RI(s["h"]), PA(w["norm_mix_g"]), RI(dhn), RI(dh_mid)],
                                     [RO(D_MODEL, F32), AO(1, D_MODEL)], nm + "rms_mix_bwd", lp)
    return dh_in, g


_IN_CUTS = np.cumsum((512, 1024, 8, 256, 256, 256, 192, 128, 32))


def _pad_cols(a, n):
    return jnp.pad(a, ((0, 0), (0, n - a.shape[1])))


def prep_layer_weights(full, l):
    w_in = full["w_in"][l]
    c = _IN_CUTS
    z, xbc, dtr = w_in[:, :c[0]], w_in[:, c[0]:c[1]], w_in[:, c[1]:c[2]]
    q_sb, k_sb, v_sb = w_in[:, c[2]:c[3]], w_in[:, c[3]:c[4]], w_in[:, c[4]:c[5]]
    q_a, c_kv, k_r = w_in[:, c[5]:c[6]], w_in[:, c[6]:c[7]], w_in[:, c[7]:c[8]]
    w_main = jnp.concatenate([xbc, z, _pad_cols(q_a, 256), c_kv, k_r, k_r, k_r, k_r, _pad_cols(dtr, 256)], axis=1)
    assert w_main.shape[1] == U_MAIN
    row = lambda v, n=None: _pad_cols(v.reshape(1, -1).astype(F32), v.size if n is None else n)
    uq = full["mla_w_uq"][l].reshape(MLA_Q_RANK, 4, 96)
    w_uq = jnp.concatenate([uq[:, :, :64].reshape(MLA_Q_RANK, 256), uq[:, :, 64:].reshape(MLA_Q_RANK, 128)], axis=1)
    w_uq = jnp.pad(w_uq, ((0, 256 - MLA_Q_RANK), (0, 0)))
    ukv = full["mla_w_ukv"][l].reshape(MLA_KV_RANK, 4, 128)
    w_ukv = jnp.concatenate([ukv[:, :, :64].reshape(MLA_KV_RANK, 256), ukv[:, :, 64:].reshape(MLA_KV_RANK, 256)], axis=1)
    return dict(
        norm_mix_g=row(full["norm_mix_g"][l]), w_main=w_main, w_sb=jnp.concatenate([q_sb * SB_SCALE, k_sb, v_sb], axis=1),
        ssd_conv_w=full["ssd_conv_w"][l], ssd_conv_b=row(full["ssd_conv_b"][l]),
        dt_bias=row(full["ssd_dt_bias"][l], LANES), a_log=row(full["ssd_a_log"][l], LANES), d_skip=row(full["ssd_d"][l], LANES),
        ssd_norm_g=row(full["ssd_norm_g"][l]), sb_norm_g=row(full["sb_norm_g"][l]),
        q_norm_g=row(full["mla_q_norm_g"][l], 256), kv_norm_g=row(full["mla_kv_norm_g"][l]),
        w_uq=w_uq, w_ukv=w_ukv, mla_norm_g=row(full["mla_norm_g"][l]),
        w_out=full["w_out"][l], norm_ffn_g=row(full["norm_ffn_g"][l]),
        w_up_a=full["ffn_w_up"][l][:, :D_FF], w_up_b=full["ffn_w_up"][l][:, D_FF:],
        ffn_conv_w_a=full["ffn_conv_w"][l][:, :D_FF], ffn_conv_w_b=full["ffn_conv_w"][l][:, D_FF:],
        ffn_conv_b_a=row(full["ffn_conv_b"][l][:D_FF]), ffn_conv_b_b=row(full["ffn_conv_b"][l][D_FF:]),
        w_down=full["ffn_w_down"][l],
    )


def unprep_layer_grads(g):
    wm = g["w_main"]
    xbc, z = wm[:, U_XBC:U_XBC + 1024], wm[:, U_Z:U_Z + 512]
    q_a, c_kv = wm[:, U_QA:U_QA + MLA_Q_RANK], wm[:, U_CKV:U_CKV + 128]
    k_r = (wm[:, U_KR4:U_KR4 + 32] + wm[:, U_KR4 + 32:U_KR4 + 64] + wm[:, U_KR4 + 64:U_KR4 + 96] + wm[:, U_KR4 + 96:U_KR4 + 128])
    dtr = wm[:, U_DT:U_DT + SSD_HEADS]
    w_sb = g["w_sb"]
    w_in = jnp.concatenate([z, xbc, dtr, w_sb[:, :SB_WIDTH] * SB_SCALE, w_sb[:, SB_WIDTH:], q_a, c_kv, k_r], axis=1)
    guq = g["w_uq"][:MLA_Q_RANK]
    guq = jnp.concatenate([guq[:, :256].reshape(MLA_Q_RANK, 4, 64), guq[:, 256:].reshape(MLA_Q_RANK, 4, 32)], axis=2)
    gukv = g["w_ukv"]
    gukv = jnp.concatenate([gukv[:, :256].reshape(MLA_KV_RANK, 4, 64), gukv[:, 256:].reshape(MLA_KV_RANK, 4, 64)], axis=2)
    return dict(
        norm_mix_g=g["norm_mix_g"][0], w_in=w_in, ssd_conv_w=g["ssd_conv_w"], ssd_conv_b=g["ssd_conv_b"][0],
        ssd_dt_bias=g["dt_bias"][0, :SSD_HEADS], ssd_a_log=g["a_log"][0, :SSD_HEADS], ssd_d=g["d_skip"][0, :SSD_HEADS],
        ssd_norm_g=g["ssd_norm_g"][0], sb_norm_g=g["sb_norm_g"][0], mla_q_norm_g=g["q_norm_g"][0, :MLA_Q_RANK],
        mla_kv_norm_g=g["kv_norm_g"][0], mla_w_uq=guq.reshape(MLA_Q_RANK, 384), mla_w_ukv=gukv.reshape(MLA_KV_RANK, 512),
        mla_norm_g=g["mla_norm_g"][0], w_out=g["w_out"], norm_ffn_g=g["norm_ffn_g"][0],
        ffn_w_up=jnp.concatenate([g["w_up_a"], g["w_up_b"]], axis=1),
        ffn_conv_w=jnp.concatenate([g["ffn_conv_w_a"], g["ffn_conv_w_b"]], axis=1),
        ffn_conv_b=jnp.concatenate([g["ffn_conv_b_a"][0], g["ffn_conv_b_b"][0]], axis=0),
        ffn_w_down=g["w_down"],
    )


def rope_tables(lp):
    pos = jnp.arange(lp, dtype=F32)
    inv = 1.0 / (ROPE_BASE ** (jnp.arange(0, MLA_ROPE, 2, dtype=F32) / MLA_ROPE))
    ang = pos[:, None] * inv[None, :]
    ang = jnp.concatenate([ang, ang] * 4, axis=-1)
    return jnp.cos(ang), jnp.sin(ang)


def local_step(x_seq, target, full):
    seq = x_seq.shape[0]
    length = seq + N_META
    lp = -(-length // ROW_TILE) * ROW_TILE
    cs, sn = rope_tables(lp)
    h = jnp.concatenate([full["meta_tokens"].astype(F32), x_seq, jnp.zeros((lp - length, D_MODEL), F32)], axis=0)
    tgt = jnp.pad(target, ((N_META, lp - length), (0, 0)))
    ws = [prep_layer_weights(full, l) for l in range(DEPTH)]
    saved = []
    for l in range(DEPTH):
        h, s = layer_fwd(h, ws[l], cs, sn, l)
        saved.append(s)
    fg = full["final_norm_g"].reshape(1, D_MODEL).astype(F32)
    tm = min(ROW_TILE, lp)

    def loss_fn(i, n, x, g, t):
        rows = _rows_iota(x) + i * tm
        valid = jnp.logical_and(rows >= N_META, rows < length)

        def f(x_, g_):
            err = jnp.where(valid, _rms(x_, g_) - t, 0.0)
            return 0.5 * jnp.sum(err * err) * (1.0 / D_MODEL)

        val, (dx, dg) = jax.value_and_grad(f, argnums=(0, 1))(x, g)
        return dx, jnp.full((1, LANES), val, F32), dg

    dh, loss_row, g_final = rowwise(loss_fn, [RI(h), PA(fg), RI(tgt)], [RO(D_MODEL, F32), AO(1, LANES), AO(1, D_MODEL)],
                                    "loss_head", lp)
    grads = {}
    per_layer = [None] * DEPTH
    for l in reversed(range(DEPTH)):
        dh, g = layer_bwd(dh, ws[l], saved[l], cs, sn, l)
        per_layer[l] = unprep_layer_grads(g)
    for k in per_layer[0]:
        grads[k] = jnp.stack([per_layer[l][k] for l in range(DEPTH)], axis=0)
    grads["final_norm_g"] = g_final[0]
    grads["meta_tokens"] = dh[:N_META]
    return loss_row[0, 0], dh[N_META:length], grads


_ANY = pl.BlockSpec(memory_space=pl.ANY)


def chip_exchange(srcs, broadcast, name):
    n = len(srcs)
    flips = ((1, 0), (0, 1), (1, 1))

    def body(*refs):
        ins, outs = refs[:n], refs[n:2 * n]
        send_sems, recv_sems, loc_sems = refs[2 * n:]
        x, y, c = lax.axis_index("x"), lax.axis_index("y"), lax.axis_index("c")
        me = 2 * x + y
        copies = []
        for a in range(n):
            src_me = ins[a] if broadcast[a] else ins[a].at[me]
            cp = pltpu.make_async_copy(src_me, outs[a].at[me], loc_sems.at[a])
            cp.start()
            copies.append(cp)
            for k, (fx, fy) in enumerate(flips):
                px = 1 - x if fx else x
                py = 1 - y if fy else y
                src = ins[a] if broadcast[a] else ins[a].at[2 * px + py]
                rc = pltpu.make_async_remote_copy(src_ref=src, dst_ref=outs[a].at[me], send_sem=send_sems.at[a, k],
                                                  recv_sem=recv_sems.at[a, k], device_id=(px, py, c), device_id_type=MESH_ID)
                rc.start()
                copies.append(rc)
        for cp in copies:
            cp.wait()

    out_shape = [jax.ShapeDtypeStruct((N_CHIPS,) + (s.shape if b else s.shape[1:]), s.dtype) for s, b in zip(srcs, broadcast)]
    return pl.pallas_call(
        body, name=name, in_specs=[_ANY] * n, out_specs=[_ANY] * n, out_shape=out_shape,
        scratch_shapes=[pltpu.SemaphoreType.DMA((n, 3)), pltpu.SemaphoreType.DMA((n, 3)), pltpu.SemaphoreType.DMA((n,))],    )(*srcs)


def sibling_swap(srcs, name):
    n = len(srcs)

    def body(*refs):
        ins, outs = refs[:n], refs[n:2 * n]
        send_sems, recv_sems = refs[2 * n:]
        x, y, c = lax.axis_index("x"), lax.axis_index("y"), lax.axis_index("c")
        copies = []
        for a in range(n):
            rc = pltpu.make_async_remote_copy(src_ref=ins[a], dst_ref=outs[a], send_sem=send_sems.at[a],
                                              recv_sem=recv_sems.at[a], device_id=(x, y, 1 - c), device_id_type=MESH_ID)
            rc.start()
            copies.append(rc)
        for cp in copies:
            cp.wait()

    return pl.pallas_call(
        body, name=name, in_specs=[_ANY] * n, out_specs=[_ANY] * n,
        out_shape=[jax.ShapeDtypeStruct(s.shape, s.dtype) for s in srcs],
        scratch_shapes=[pltpu.SemaphoreType.DMA((n,)), pltpu.SemaphoreType.DMA((n,))],    )(*srcs)


WEIGHT_NAMES = ("meta_tokens", "norm_mix_g", "w_in", "ssd_conv_w", "ssd_conv_b", "ssd_dt_bias", "ssd_a_log", "ssd_d",
                "ssd_norm_g", "sb_norm_g", "mla_q_norm_g", "mla_kv_norm_g", "mla_w_uq", "mla_w_ukv", "mla_norm_g",
                "w_out", "norm_ffn_g", "ffn_w_up", "ffn_conv_w", "ffn_conv_b", "ffn_w_down", "final_norm_g")
SHARD_AXIS = {"meta_tokens": 1, "w_in": 2, "ssd_conv_w": 2, "mla_w_uq": 2, "mla_w_ukv": 2, "w_out": 1, "ffn_w_up": 2,
              "ffn_conv_w": 2, "ffn_w_down": 1}
SHARDED = tuple(n for n in WEIGHT_NAMES if n in SHARD_AXIS)
REPLICATED = tuple(n for n in WEIGHT_NAMES if n not in SHARD_AXIS)
GATHER_BF16 = ("w_in", "mla_w_uq", "mla_w_ukv", "w_out", "ffn_w_up", "ffn_w_down")
GATHER_F32 = ("meta_tokens", "ssd_conv_w", "ffn_conv_w")
PACK_ROWS = ROW_TILE


def pack(arrs, dtype):
    flat = jnp.concatenate([a.reshape(-1).astype(dtype) for a in arrs])
    per = PACK_ROWS * PACK_W
    total = -(-flat.size // per) * per
    return jnp.pad(flat, (0, total - flat.size)).reshape(total // PACK_W, PACK_W)


def unpack(buf, shapes):
    flat = buf.reshape(-1)
    out, off = [], 0
    for shp in shapes:
        size = int(np.prod(shp))
        out.append(flat[off:off + size].reshape(shp))
        off += size
    return out


def gather_weights(a):
    full = {n: a[n] for n in REPLICATED}
    bufs = [pack([a[n] for n in GATHER_BF16], BF16), pack([a[n] for n in GATHER_F32], F32)]
    got = chip_exchange(bufs, (True, True), "gather_weights")
    for names, g in ((GATHER_BF16, got[0]), (GATHER_F32, got[1])):
        pieces = [unpack(g[k], [a[n].shape for n in names]) for k in range(N_CHIPS)]
        for idx, n in enumerate(names):
            full[n] = jnp.concatenate([pieces[k][idx] for k in range(N_CHIPS)], axis=SHARD_AXIS[n])
    return full


def _adamw(i, n, p, q, w, m, v):
    g = p + q
    m = ADAM_B1 * m + (1.0 - ADAM_B1) * g
    v = ADAM_B2 * v + (1.0 - ADAM_B2) * jnp.square(g)
    m_hat = m / (1.0 - ADAM_B1 ** ADAM_STEP)
    v_hat = v / (1.0 - ADAM_B2 ** ADAM_STEP)
    delta = -ADAM_LR * (m_hat / (jnp.sqrt(v_hat) + ADAM_EPS) + ADAM_WD * w)
    return g, delta, m, v


def reduce_and_update(a, grads):
    sh_shapes = [a[n].shape for n in SHARDED]
    rep_shapes = [a[n].shape for n in REPLICATED]
    slabs = []
    for k in range(N_CHIPS):
        parts = []
        for n in SHARDED:
            ax = SHARD_AXIS[n]
            size = a[n].shape[ax]
            parts.append(lax.slice_in_dim(grads[n], k * size, (k + 1) * size, axis=ax))
        slabs.append(pack(parts, BF16))
    g_sh = jnp.stack(slabs, axis=0)
    g_rep = pack([grads[n] for n in REPLICATED], F32)
    got_sh, got_rep = chip_exchange([g_sh, g_rep], (False, True), "exchange_grads")
    sums = []
    for tag, got in (("sh", got_sh), ("rep", got_rep)):
        rows = got.shape[1]
        blk = rows // PACK_ROWS
        flat = got.reshape(N_CHIPS * rows, PACK_W)
        sums.append(rowwise(lambda i, n, g0, g1, g2, g3: ((g0.astype(F32) + g1.astype(F32)) + g2.astype(F32)) + g3.astype(F32),
                            [RI(flat, rblk=k * blk) for k in range(N_CHIPS)], [RO(PACK_W, F32)], "sum_chips_" + tag, rows,
                            tm=PACK_ROWS)[0])
    others = sibling_swap(sums, "swap_cores")
    outs = {}
    for tag, names, shapes, p, q in (("sh", SHARDED, sh_shapes, sums[0], others[0]),
                                     ("rep", REPLICATED, rep_shapes, sums[1], others[1])):
        packed = [pack([a[pre + n] for n in names], F32) for pre in ("", "m_", "v_")]
        res = rowwise(_adamw, [RI(p), RI(q)] + [RI(t) for t in packed], [RO(PACK_W, F32)] * 4, "adamw_" + tag,
                      p.shape[0], tm=PACK_ROWS)
        for kind, buf in zip(("grad", "delta", "new_m", "new_v"), res):
            for n, val in zip(names, unpack(buf, shapes)):
                outs[(kind, n)] = val
    return outs


INPUT_NAMES = ("x",) + WEIGHT_NAMES + ("loss_target",) + tuple("m_" + n for n in WEIGHT_NAMES) + tuple("v_" + n for n in WEIGHT_NAMES)


def kernel(x, meta_tokens, norm_mix_g, w_in, ssd_conv_w, ssd_conv_b, ssd_dt_bias, ssd_a_log, ssd_d, ssd_norm_g, sb_norm_g, mla_q_norm_g, mla_kv_norm_g, mla_w_uq, mla_w_ukv, mla_norm_g, w_out, norm_ffn_g, ffn_w_up, ffn_conv_w, ffn_conv_b, ffn_w_down, final_norm_g, loss_target, m_meta_tokens, m_norm_mix_g, m_w_in, m_ssd_conv_w, m_ssd_conv_b, m_ssd_dt_bias, m_ssd_a_log, m_ssd_d, m_ssd_norm_g, m_sb_norm_g, m_mla_q_norm_g, m_mla_kv_norm_g, m_mla_w_uq, m_mla_w_ukv, m_mla_norm_g, m_w_out, m_norm_ffn_g, m_ffn_w_up, m_ffn_conv_w, m_ffn_conv_b, m_ffn_w_down, m_final_norm_g, v_meta_tokens, v_norm_mix_g, v_w_in, v_ssd_conv_w, v_ssd_conv_b, v_ssd_dt_bias, v_ssd_a_log, v_ssd_d, v_ssd_norm_g, v_sb_norm_g, v_mla_q_norm_g, v_mla_kv_norm_g, v_mla_w_uq, v_mla_w_ukv, v_mla_norm_g, v_w_out, v_norm_ffn_g, v_ffn_w_up, v_ffn_conv_w, v_ffn_conv_b, v_ffn_w_down, v_final_norm_g):
    args = (x, meta_tokens, norm_mix_g, w_in, ssd_conv_w, ssd_conv_b, ssd_dt_bias, ssd_a_log, ssd_d, ssd_norm_g, sb_norm_g, mla_q_norm_g, mla_kv_norm_g, mla_w_uq, mla_w_ukv, mla_norm_g, w_out, norm_ffn_g, ffn_w_up, ffn_conv_w, ffn_conv_b, ffn_w_down, final_norm_g, loss_target, m_meta_tokens, m_norm_mix_g, m_w_in, m_ssd_conv_w, m_ssd_conv_b, m_ssd_dt_bias, m_ssd_a_log, m_ssd_d, m_ssd_norm_g, m_sb_norm_g, m_mla_q_norm_g, m_mla_kv_norm_g, m_mla_w_uq, m_mla_w_ukv, m_mla_norm_g, m_w_out, m_norm_ffn_g, m_ffn_w_up, m_ffn_conv_w, m_ffn_conv_b, m_ffn_w_down, m_final_norm_g, v_meta_tokens, v_norm_mix_g, v_w_in, v_ssd_conv_w, v_ssd_conv_b, v_ssd_dt_bias, v_ssd_a_log, v_ssd_d, v_ssd_norm_g, v_sb_norm_g, v_mla_q_norm_g, v_mla_kv_norm_g, v_mla_w_uq, v_mla_w_ukv, v_mla_norm_g, v_w_out, v_norm_ffn_g, v_ffn_w_up, v_ffn_conv_w, v_ffn_conv_b, v_ffn_w_down, v_final_norm_g)
    a = dict(zip(INPUT_NAMES, args, strict=True))
    full = gather_weights(a)
    loss, grad_x, grads = local_step(a["x"][0], a["loss_target"][0], full)
    loss = lax.psum(loss, ("x", "y", "c"))
    outs = reduce_and_update(a, grads)
    result = [loss, grad_x[None]]
    for kind in ("grad", "delta", "new_m", "new_v"):
        result += [outs[(kind, n)] for n in WEIGHT_NAMES]
    return tuple(result)
```

```python
import functools
import math

import numpy as np
import jax
import jax.numpy as jnp
from jax import lax
from jax.experimental import pallas as pl
from jax.experimental.pallas import tpu as pltpu

F32 = jnp.float32
BF16 = jnp.bfloat16
HIGHEST = lax.Precision.HIGHEST
MESH_ID = pl.DeviceIdType.MESH

D_MODEL = 1024
DEPTH = 2
N_META = 16
EPS = 1e-6
SSD_HEADS = 8
SSD_WIDTH = 512
SSD_XBC = 1024
SSD_CONV = 4
SB_WIDTH = 256
SB_SCALE = 64 ** -0.5
MLA_Q_RANK = 192
MLA_KV_RANK = 128
MLA_ROPE = 32
MLA_SCALE = 96 ** -0.5
ROPE_BASE = 10000.0
D_FF = 2816
FFN_CONV = 3
IN_COLS = 2664
N_CHIPS = 4

ADAM_LR = 0.001
ADAM_B1 = 0.9
ADAM_B2 = 0.999
ADAM_EPS = 1e-08
ADAM_WD = 0.01
ADAM_STEP = 10

LANES = 128
SUBLANES = 8
ROW_TILE = 256
VMEM_LIMIT = 56 * 1024 * 1024
PACK_W = 1024

U_XBC, U_Z, U_QA, U_CKV, U_KR4, U_DT, U_MAIN = 0, 1024, 1536, 1792, 1920, 2048, 2304
NEG = -1e30


def _cp(*sem):
    return pltpu.CompilerParams(dimension_semantics=sem if sem else None, vmem_limit_bytes=VMEM_LIMIT)


def _pick(dim, pref):
    if dim <= pref:
        return dim
    best = None
    for t in range(LANES, pref + 1, LANES):
        if dim % t == 0:
            best = t
    assert best is not None, (dim, pref)
    return best


def _dot(a, b, dims="nn", precision=None):
    dn = {"nn": (((1,), (0,)), ((), ())), "nt": (((1,), (1,)), ((), ())), "tn": (((0,), (0,)), ((), ()))}[dims]
    return lax.dot_general(a, b, dn, preferred_element_type=F32, precision=precision)


def _softplus(x):
    return jnp.maximum(x, 0.0) + jnp.log1p(jnp.exp(-jnp.abs(x)))


def _silu(x):
    return x * jax.nn.sigmoid(x)


def _rms(x, g, n=None):
    n = x.shape[-1] if n is None else n
    ms = jnp.sum(x * x, axis=-1, keepdims=True) * (1.0 / n)
    return x * lax.rsqrt(ms + EPS) * g


def mm(a, b, dims, out_dtype, name, add=None, tm=None, tn=None, tk=None):
    if dims == "nn":
        (m, k), (k2, n) = a.shape, b.shape
    elif dims == "nt":
        (m, k), (n, k2) = a.shape, b.shape
    else:
        (k, m), (k2, n) = a.shape, b.shape
    assert k == k2, (a.shape, b.shape, dims)
    if dims == "tn":
        tm, tn, tk = _pick(m, tm or 1408), _pick(n, tn or 1408), _pick(k, tk or 1408)
    else:
        tm, tn, tk = _pick(m, tm or 768), _pick(n, tn or 1408), _pick(k, tk or 2816)
    nk = k // tk
    if dims == "tn":
        a_spec = pl.BlockSpec((tk, tm), lambda j, i, kk: (kk, i))
    else:
        a_spec = pl.BlockSpec((tm, tk), lambda j, i, kk: (i, kk))
    if dims == "nt":
        b_spec = pl.BlockSpec((tn, tk), lambda j, i, kk: (j, kk))
    else:
        b_spec = pl.BlockSpec((tk, tn), lambda j, i, kk: (kk, j))
    o_spec = pl.BlockSpec((tm, tn), lambda j, i, kk: (i, j))
    has_add = add is not None

    def body(*refs):
        a_ref, b_ref = refs[0], refs[1]
        add_ref = refs[2] if has_add else None
        o_ref = refs[3] if has_add else refs[2]
        part = _dot(a_ref[...].astype(BF16), b_ref[...].astype(BF16), dims)

        def finish(r):
            if has_add:
                r = r + add_ref[...].astype(F32)
            o_ref[...] = r.astype(o_ref.dtype)

        if nk == 1:
            finish(part)
            return
        acc_ref = refs[-1]
        kk = pl.program_id(2)

        @pl.when(kk == 0)
        def _():
            acc_ref[...] = part

        @pl.when(jnp.logical_and(kk > 0, kk < nk - 1))
        def _():
            acc_ref[...] += part

        @pl.when(kk == nk - 1)
        def _():
            finish(acc_ref[...] + part)

    in_specs = [a_spec, b_spec] + ([o_spec] if has_add else [])
    args = (a, b) + ((add,) if has_add else ())
    return pl.pallas_call(
        body, name=name, grid=(n // tn, m // tm, nk),
        in_specs=in_specs, out_specs=o_spec,
        out_shape=jax.ShapeDtypeStruct((m, n), out_dtype),
        scratch_shapes=[pltpu.VMEM((tm, tn), F32)] if nk > 1 else [],
        compiler_params=_cp("parallel", "parallel", "arbitrary"),
    )(*args)


def RI(arr, width=None, cidx=0, cv=False, rblk=0):
    return ("row" if rblk == 0 else ("row", rblk), arr, arr.shape[1] if width is None else width, cidx, cv)


def HP(arr, width=None, cidx=0, cv=False):
    return ("prev", arr, arr.shape[1] if width is None else width, cidx, cv)


def HN(arr, width=None, cidx=0, cv=False):
    return ("next", arr, arr.shape[1] if width is None else width, cidx, cv)


def PA(arr, width=None, cidx=0, cv=False):
    return ("par", arr, arr.shape[1] if width is None else width, cidx, cv)


def RO(ncols, dtype, width=None, cv=False):
    return ("row", ncols, dtype, ncols if width is None else width, cv)


def AO(nrows, ncols, width=None, cv=False):
    return ("acc", (nrows, ncols), F32, ncols if width is None else width, cv)


def rowwise(fn, ins, outs, name, rows, tm=ROW_TILE, ncol=1):
    tm = min(tm, rows)
    assert rows % tm == 0
    nrow = rows // tm
    hb = tm // SUBLANES
    last_hb = rows // SUBLANES - 1
    in_specs, args = [], []
    for kind, arr, width, cidx, cv in ins:
        def cmap(j, cidx=cidx, cv=cv):
            return cidx + j if cv else cidx
        if kind == "row":
            spec = pl.BlockSpec((tm, width), lambda j, i, cmap=cmap: (i, cmap(j)))
        elif isinstance(kind, tuple):
            spec = pl.BlockSpec((tm, width), lambda j, i, cmap=cmap, rblk=kind[1]: (i + rblk, cmap(j)))
        elif kind == "prev":
            spec = pl.BlockSpec((SUBLANES, width), lambda j, i, cmap=cmap: (jnp.maximum(i * hb - 1, 0), cmap(j)))
        elif kind == "next":
            spec = pl.BlockSpec((SUBLANES, width), lambda j, i, cmap=cmap: (jnp.minimum((i + 1) * hb, last_hb), cmap(j)))
        else:
            spec = pl.BlockSpec((arr.shape[0], width), lambda j, i, cmap=cmap: (0, cmap(j)))
        in_specs.append(spec)
        args.append(arr)
    out_specs, out_shapes, acc_cv = [], [], []
    for kind, shp, dtype, width, cv in outs:
        if kind == "row":
            out_specs.append(pl.BlockSpec((tm, width), lambda j, i, cv=cv: (i, j if cv else 0)))
            out_shapes.append(jax.ShapeDtypeStruct((rows, shp), dtype))
            acc_cv.append(None)
        else:
            out_specs.append(pl.BlockSpec((shp[0], width), lambda j, i, cv=cv: (0, j if cv else 0)))
            out_shapes.append(jax.ShapeDtypeStruct(shp, dtype))
            acc_cv.append(cv)
    n_in = len(ins)

    def body(*refs):
        j = pl.program_id(0)
        i = pl.program_id(1)
        vals = fn(i, nrow, *[r[...] for r in refs[:n_in]])
        if not isinstance(vals, (tuple, list)):
            vals = (vals,)
        for o_ref, v, cv in zip(refs[n_in:], vals, acc_cv):
            if cv is None:
                o_ref[...] = v.astype(o_ref.dtype)
            else:
                first = (i == 0) if cv else jnp.logical_and(i == 0, j == 0)

                @pl.when(first)
                def _(o_ref=o_ref, v=v):
                    o_ref[...] = v.astype(o_ref.dtype)

                @pl.when(jnp.logical_not(first))
                def _(o_ref=o_ref, v=v):
                    o_ref[...] += v.astype(o_ref.dtype)

    res = pl.pallas_call(
        body, name=name, grid=(ncol, nrow), in_specs=in_specs, out_specs=out_specs, out_shape=out_shapes,
        compiler_params=_cp("arbitrary", "arbitrary"),
    )(*args)
    return res


def _rows_iota(x):
    return lax.broadcasted_iota(jnp.int32, x.shape, 0)


def shift_down(x, halo, s):
    if s == 0:
        return x
    tm = x.shape[0]
    top = pltpu.roll(halo, s, 0)
    if tm > SUBLANES:
        top = jnp.concatenate([top, jnp.zeros((tm - SUBLANES, x.shape[1]), x.dtype)], axis=0)
    return jnp.where(_rows_iota(x) < s, top, pltpu.roll(x, s, 0))


def shift_up(x, halo, s):
    if s == 0:
        return x
    tm = x.shape[0]
    bot = pltpu.roll(halo, SUBLANES - s, 0)
    if tm > SUBLANES:
        bot = jnp.concatenate([jnp.zeros((tm - SUBLANES, x.shape[1]), x.dtype), bot], axis=0)
    return jnp.where(_rows_iota(x) >= tm - s, bot, pltpu.roll(x, tm - s, 0))


def conv_fwd(x, halo, w, i):
    kw = w.shape[0]
    halo = jnp.where(i == 0, 0.0, halo)
    out = None
    for k in range(kw):
        term = w[k:k + 1, :] * shift_down(x, halo, kw - 1 - k)
        out = term if out is None else out + term
    return out


def conv_bwd_data(dy, halo_next, w, i, n):
    kw = w.shape[0]
    halo_next = jnp.where(i == n - 1, 0.0, halo_next)
    out = None
    for k in range(kw):
        term = w[k:k + 1, :] * shift_up(dy, halo_next, kw - 1 - k)
        out = term if out is None else out + term
    return out


def conv_bwd_w(dy, x, halo, i, kw):
    halo = jnp.where(i == 0, 0.0, halo)
    rows = [jnp.sum(dy * shift_down(x, halo, kw - 1 - k), axis=0, keepdims=True) for k in range(kw)]
    return jnp.concatenate(rows, axis=0)


def _lane(shape):
    return lax.broadcasted_iota(jnp.int32, shape, 1)


def rope_rot(x):
    lane = _lane(x.shape) % MLA_ROPE
    return jnp.where(lane < MLA_ROPE // 2, -pltpu.roll(x, LANES - MLA_ROPE // 2, 1), pltpu.roll(x, MLA_ROPE // 2, 1))


def rope_rot_t(g):
    lane = _lane(g.shape) % MLA_ROPE
    return jnp.where(lane < MLA_ROPE // 2, pltpu.roll(g, LANES - MLA_ROPE // 2, 1), -pltpu.roll(g, MLA_ROPE // 2, 1))


def _ssd_common(p, xs, dt_raw, bias, alog, q):
    lane = _lane((q, LANES))
    pre = dt_raw + bias
    dt = jnp.where(lane < SSD_HEADS, _softplus(pre), 0.0)
    a_row = -jnp.exp(alog)
    d_a = dt * a_row
    ri = lax.broadcasted_iota(jnp.int32, (q, q), 0)
    ci = lax.broadcasted_iota(jnp.int32, (q, q), 1)
    causal = ri >= ci
    acs = _dot(causal.astype(F32), d_a, "nn", HIGHEST)
    acs_t = acs.T
    subl = lax.broadcasted_iota(jnp.int32, (LANES, q), 0)
    h0, h1 = 2 * p, 2 * p + 1

    def col(arr, h):
        return jnp.sum(jnp.where(lane == h, arr, 0.0), axis=1, keepdims=True)

    def row(arr_t, h):
        return jnp.sum(jnp.where(subl == h, arr_t, 0.0), axis=0, keepdims=True)

    lo = lane < 64
    cols = (col(acs, h0), col(acs, h1))
    rows = (row(acs_t, h0), row(acs_t, h1))
    acs_p = jnp.where(lo, cols[0], cols[1])
    dt_p = jnp.where(lo, col(dt, h0), col(dt, h1))
    tots = (cols[0][q - 1:q, :], cols[1][q - 1:q, :])
    tot_p = jnp.where(lo[0:1, :], tots[0], tots[1])
    lms = tuple(jnp.exp(jnp.where(causal, cols[j] - rows[j], NEG)) for j in range(2))
    return dict(lane=lane, lo=lo, pre=pre, dt=dt, a_row=a_row, h=(h0, h1), acs_p=acs_p, dt_p=dt_p, tots=tots,
                tot_p=tot_p, lms=lms, ri=ri, ci=ci, eacs=jnp.exp(acs_p), dte=jnp.exp(tot_p - acs_p), x=xs * dt_p)


def _pick_lane(row_arr, h):
    return jnp.sum(jnp.where(_lane(row_arr.shape) == h, row_arr, 0.0), axis=1, keepdims=True)


def ssd_fwd(xbc_c, u_main, bias_row, alog_row, d_row, name):
    lp = xbc_c.shape[0]
    q = min(ROW_TILE, lp)
    nc = lp // q
    dt_blk = U_DT // LANES

    def body(xs_ref, b_ref, c_ref, dt_ref, bias_ref, alog_ref, d_ref, y_ref, hp_ref, h_scr):
        p = pl.program_id(0)
        c = pl.program_id(1)

        @pl.when(c == 0)
        def _():
            h_scr[...] = jnp.zeros_like(h_scr)

        xs = xs_ref[...]
        bb = b_ref[...].astype(BF16)
        cb_ = c_ref[...].astype(BF16)
        s = _ssd_common(p, xs, dt_ref[...], bias_ref[...], alog_ref[...], q)
        lo = s["lo"]
        g = _dot(cb_, bb, "nt")
        y = jnp.zeros((q, LANES), F32)
        for j in range(2):
            m = (g * s["lms"][j]).astype(BF16)
            xj = jnp.where(lo if j == 0 else jnp.logical_not(lo), s["x"], 0.0).astype(BF16)
            y = y + _dot(m, xj)
        hp = h_scr[...]
        hp_ref[...] = hp
        y = y + _dot(cb_, hp.astype(BF16), "nt") * s["eacs"]
        d_p = jnp.where(lo[0:1, :], _pick_lane(d_ref[...], s["h"][0]), _pick_lane(d_ref[...], s["h"][1]))
        y_ref[...] = y + d_p * xs
        sub_lo = lax.broadcasted_iota(jnp.int32, (LANES, LANES), 0) < 64
        etot = jnp.where(sub_lo, jnp.exp(s["tots"][0]), jnp.exp(s["tots"][1]))
        h_scr[...] = hp * etot + _dot((s["x"] * s["dte"]).astype(BF16), bb, "tn")

    grp = lambda p: p // 2
    in_specs = [
        pl.BlockSpec((q, LANES), lambda p, c: (c, p)),
        pl.BlockSpec((q, LANES), lambda p, c: (c, 4 + grp(p))),
        pl.BlockSpec((q, LANES), lambda p, c: (c, 6 + grp(p))),
        pl.BlockSpec((q, LANES), lambda p, c: (c, dt_blk)),
        pl.BlockSpec((1, LANES), lambda p, c: (0, 0)),
        pl.BlockSpec((1, LANES), lambda p, c: (0, 0)),
        pl.BlockSpec((1, LANES), lambda p, c: (0, 0)),
    ]
    out_specs = [
        pl.BlockSpec((q, LANES), lambda p, c: (c, p)),
        pl.BlockSpec((None, None, LANES, LANES), lambda p, c: (p, c, 0, 0)),
    ]
    return pl.pallas_call(
        body, name=name, grid=(4, nc), in_specs=in_specs, out_specs=out_specs,
        out_shape=[jax.ShapeDtypeStruct((lp, SSD_WIDTH), F32), jax.ShapeDtypeStruct((4, nc, LANES, LANES), F32)],
        scratch_shapes=[pltpu.VMEM((LANES, LANES), F32)],
        compiler_params=_cp("arbitrary", "arbitrary"),
    )(xbc_c, xbc_c, xbc_c, u_main, bias_row, alog_row, d_row)


def ssd_bwd(xbc_c, u_main, bias_row, alog_row, d_row, hprev, dy, name):
    lp = xbc_c.shape[0]
    q = min(ROW_TILE, lp)
    nc = lp // q
    dt_blk = U_DT // LANES

    def body(xs_ref, b_ref, c_ref, dt_ref, bias_ref, alog_ref, d_ref, hp_ref, dy_ref,
             dxs_ref, db_ref, dc_ref, ddt_ref, pg_ref, dh_scr):
        p = pl.program_id(0)
        cc = pl.program_id(1)

        @pl.when(cc == 0)
        def _():
            dh_scr[...] = jnp.zeros_like(dh_scr)
            pg_ref[...] = jnp.zeros_like(pg_ref)

        xs = xs_ref[...]
        bb = b_ref[...].astype(BF16)
        cb_ = c_ref[...].astype(BF16)
        s = _ssd_common(p, xs, dt_ref[...], bias_ref[...], alog_ref[...], q)
        lane, lo, x = s["lane"], s["lo"], s["x"]
        h0, h1 = s["h"]
        d_y = dy_ref[...]
        hp = hp_ref[...]
        hpb = hp.astype(BF16)
        dhn = dh_scr[...]
        dhnb = dhn.astype(BF16)
        xd = x * s["dte"]
        g = _dot(cb_, bb, "nt")
        dxdiag = jnp.zeros((q, LANES), F32)
        dg = jnp.zeros((q, q), F32)
        row_part, col_part = [], []
        for j in range(2):
            mj = lo if j == 0 else jnp.logical_not(lo)
            lm = s["lms"][j]
            m32 = g * lm
            xj = jnp.where(mj, x, 0.0).astype(BF16)
            dyj = jnp.where(mj, d_y, 0.0).astype(BF16)
            dxdiag = dxdiag + _dot(m32.astype(BF16), dyj, "tn")
            dm = _dot(dyj, xj, "nt")
            dg = dg + dm * lm
            wm = dm * m32
            row_part.append(jnp.sum(wm, axis=1, keepdims=True))
            col_part.append(jnp.sum(wm, axis=0, keepdims=True))
        dgb = dg.astype(BF16)
        d_c = _dot(dgb, bb)
        d_b = _dot(dgb, cb_, "tn")
        yoff = _dot(cb_, hpb, "nt") * s["eacs"]
        d_t = (d_y * s["eacs"]).astype(BF16)
        d_c = d_c + _dot(d_t, hpb)
        d_hp = _dot(d_t, cb_, "tn")
        dxd = _dot(bb, dhnb, "nt")
        d_b = d_b + _dot(xd.astype(BF16), dhnb)
        d_x = dxdiag + dxd * s["dte"]
        r = dxd * xd
        a_terms = d_y * yoff - r

        def gsum(arr):
            return (jnp.sum(jnp.where(lo, arr, 0.0), axis=1, keepdims=True),
                    jnp.sum(jnp.where(lo, 0.0, arr), axis=1, keepdims=True))

        dacs = gsum(a_terms)
        rs = gsum(r)
        hh = dhn * hp
        sub_lo = lax.broadcasted_iota(jnp.int32, (LANES, LANES), 0) < 64
        hsum = (jnp.sum(jnp.where(sub_lo, hh, 0.0), keepdims=True), jnp.sum(jnp.where(sub_lo, 0.0, hh), keepdims=True))
        last = lax.broadcasted_iota(jnp.int32, (q, 1), 0) == q - 1
        etots = (jnp.exp(s["tots"][0]), jnp.exp(s["tots"][1]))
        ddacs = jnp.zeros((q, LANES), F32)
        for j, h in enumerate((h0, h1)):
            dtot = jnp.sum(rs[j], keepdims=True) + hsum[j] * etots[j]
            dj = dacs[j] + row_part[j] + jnp.where(last, dtot, 0.0)
            ddacs = ddacs + jnp.where(lane == h, dj, 0.0)
        subl = lax.broadcasted_iota(jnp.int32, (LANES, q), 0)
        cols_t = jnp.where(subl == h0, col_part[0], 0.0) + jnp.where(subl == h1, col_part[1], 0.0)
        ddacs = ddacs - cols_t.T
        anti = (s["ri"] <= s["ci"]).astype(F32)
        da = _dot(anti, ddacs, "nn", HIGHEST)
        ddt_own = gsum(d_x * xs)
        ddt = jnp.where(lane == h0, ddt_own[0], 0.0) + jnp.where(lane == h1, ddt_own[1], 0.0) + da * s["a_row"]
        draw = ddt * jax.nn.sigmoid(s["pre"])
        ddt_ref[...] = draw
        d_p = jnp.where(lo[0:1, :], _pick_lane(d_ref[...], h0), _pick_lane(d_ref[...], h1))
        dxs_ref[...] = d_p * d_y + d_x * s["dt_p"]
        db_ref[...] = d_b
        dc_ref[...] = d_c
        dds = gsum(d_y * xs)
        lane1 = lane[0:1, :]
        dd_row = (jnp.where(lane1 == h0, jnp.sum(dds[0], keepdims=True), 0.0)
                  + jnp.where(lane1 == h1, jnp.sum(dds[1], keepdims=True), 0.0))
        dbias_row = jnp.sum(draw, axis=0, keepdims=True)
        dalog_row = jnp.sum(da * s["dt"], axis=0, keepdims=True) * s["a_row"]
        sub8 = lax.broadcasted_iota(jnp.int32, (SUBLANES, LANES), 0)
        pg_ref[...] += (jnp.where(sub8 == 0, dbias_row, 0.0) + jnp.where(sub8 == 1, dalog_row, 0.0)
                        + jnp.where(sub8 == 2, dd_row, 0.0))
        etot = jnp.where(sub_lo, etots[0], etots[1])
        dh_scr[...] = d_hp + etot * dhn

    grp = lambda p: p // 2
    rc = lambda c: nc - 1 - c
    in_specs = [
        pl.BlockSpec((q, LANES), lambda p, c: (rc(c), p)),
        pl.BlockSpec((q, LANES), lambda p, c: (rc(c), 4 + grp(p))),
        pl.BlockSpec((q, LANES), lambda p, c: (rc(c), 6 + grp(p))),
        pl.BlockSpec((q, LANES), lambda p, c: (rc(c), dt_blk)),
        pl.BlockSpec((1, LANES), lambda p, c: (0, 0)),
        pl.BlockSpec((1, LANES), lambda p, c: (0, 0)),
        pl.BlockSpec((1, LANES), lambda p, c: (0, 0)),
        pl.BlockSpec((None, None, LANES, LANES), lambda p, c: (p, rc(c), 0, 0)),
        pl.BlockSpec((q, LANES), lambda p, c: (rc(c), p)),
    ]
    out_specs = [
        pl.BlockSpec((q, LANES), lambda p, c: (rc(c), p)),
        pl.BlockSpec((q, LANES), lambda p, c: (rc(c), p)),
        pl.BlockSpec((q, LANES), lambda p, c: (rc(c), p)),
        pl.BlockSpec((q, LANES), lambda p, c: (rc(c), p)),
        pl.BlockSpec((SUBLANES, LANES), lambda p, c: (p, 0)),
    ]
    wide = jax.ShapeDtypeStruct((lp, 4 * LANES), F32)
    return pl.pallas_call(
        body, name=name, grid=(4, nc), in_specs=in_specs, out_specs=out_specs,
        out_shape=[wide, wide, wide, wide, jax.ShapeDtypeStruct((4 * SUBLANES, LANES), F32)],
        scratch_shapes=[pltpu.VMEM((LANES, LANES), F32)],
        compiler_params=_cp("arbitrary", "arbitrary"),
    )(xbc_c, xbc_c, xbc_c, u_main, bias_row, alog_row, d_row, hprev, dy)


def _sb_blocks(qs, ks, r_runs, masked, bq):
    ri = lax.broadcasted_iota(jnp.int32, (bq, bq), 0)
    ci = lax.broadcasted_iota(jnp.int32, (bq, bq), 1)
    tri_after = (ri > ci).astype(BF16)
    zs = [_dot(qj, kj, "nt") for qj, kj in zip(qs, ks)]
    us, sigs, ubs = [], [], []
    for z in zs:
        u = -(jnp.maximum(z, 0.0) + jnp.log(1.0 + jnp.exp(-jnp.abs(z))))
        sigs.append(jnp.exp(z + u))
        if masked:
            u = jnp.where(ci < ri, u, 0.0)
        us.append(u)
        ubs.append(u.astype(BF16))
    afters = [_dot(ub, tri_after) for ub in ubs]
    ws = []
    for sig, after, r_run in zip(sigs, afters, r_runs):
        w = sig * jnp.exp(after + r_run)
        if masked:
            w = jnp.where(ci < ri, w, 0.0)
        ws.append(w)
    return us, sigs, ws


def _split_heads(x, lo):
    out = []
    zero = jnp.zeros((x.shape[0], LANES), x.dtype)
    for p in range(2):
        xp = x[:, LANES * p:LANES * (p + 1)]
        out += [jnp.where(lo, xp, zero), jnp.where(lo, zero, xp)]
    return out


def _per_head(x):
    return [x[:, :LANES], x[:, :LANES], x[:, LANES:], x[:, LANES:]]


def _resident(shape, col):
    return pl.BlockSpec(shape, lambda i: (0, col), pipeline_mode=pl.Buffered(1))


def sb_attn_fwd(qkv, name):
    lp = qkv.shape[0]
    bq = min(ROW_TILE, lp)
    nq = lp // bq
    assert nq <= 64

    def body(q_ref, k_ref, v_ref, o_ref, rs_ref):
        qi = pl.program_id(0)
        lane = _lane((bq, LANES))
        lo = lane < 64
        qs = _split_heads(q_ref[...], lo)

        def step(kb, carry, masked):
            off = pl.multiple_of(kb * bq, bq)
            ks = _per_head(k_ref[pl.ds(off, bq), :])
            vs = _per_head(v_ref[pl.ds(off, bq), :])
            heads, rss = carry
            r_runs = [heads[h][1] for h in range(4)]
            rss = list(rss)
            for h in range(4):
                rss[h // 2] = jnp.where(lane == 64 * (h % 2) + kb, r_runs[h], rss[h // 2])
            us, _, ws = _sb_blocks(qs, ks, r_runs, masked, bq)
            pvs = [_dot(ws[h].astype(BF16), vs[h]) for h in range(4)]
            out = tuple((heads[h][0] + pvs[h], r_runs[h] + jnp.sum(us[h], axis=1, keepdims=True)) for h in range(4))
            return out, tuple(rss)

        zero = (jnp.zeros((bq, LANES), F32), jnp.zeros((bq, 1), F32))
        zr = jnp.zeros((bq, LANES), F32)
        carry = step(qi, ((zero,) * 4, (zr, zr)), True)
        heads, rss = lax.fori_loop(0, qi, lambda t, c: step(qi - 1 - t, c, False), carry)
        o_ref[...] = jnp.concatenate([jnp.where(lo, heads[0][0], heads[1][0]), jnp.where(lo, heads[2][0], heads[3][0])], axis=1)
        rs_ref[...] = jnp.concatenate(list(rss), axis=1)

    blk = pl.BlockSpec((bq, 2 * LANES), lambda i: (i, 0))
    return pl.pallas_call(
        body, name=name, grid=(nq,),
        in_specs=[blk, _resident((lp, 2 * LANES), 1), _resident((lp, 2 * LANES), 2)],
        out_specs=[blk, blk],
        out_shape=[jax.ShapeDtypeStruct((lp, SB_WIDTH), F32), jax.ShapeDtypeStruct((lp, SB_WIDTH), F32)],
        compiler_params=_cp("arbitrary"),
    )(qkv, qkv, qkv)


def sb_attn_bwd(qkv, rs, d_o, name):
    lp = qkv.shape[0]
    bq = min(ROW_TILE, lp)
    nq = lp // bq

    def body(q_ref, k_ref, v_ref, rs_ref, do_ref, dq_ref, dk_ref, dv_ref):
        qi = pl.program_id(0)

        @pl.when(qi == 0)
        def _():
            dk_ref[...] = jnp.zeros_like(dk_ref)
            dv_ref[...] = jnp.zeros_like(dv_ref)

        lane = _lane((bq, LANES))
        lo = lane < 64
        qs = _split_heads(q_ref[...], lo)
        dos = _split_heads(do_ref[...].astype(BF16), lo)
        rs_blk = rs_ref[...]
        ri = lax.broadcasted_iota(jnp.int32, (bq, bq), 0)
        ci = lax.broadcasted_iota(jnp.int32, (bq, bq), 1)
        tbefore = (ri < ci).astype(BF16)

        def step(kb, carry, masked):
            off = pl.multiple_of(kb * bq, bq)
            ks = _per_head(k_ref[pl.ds(off, bq), :])
            vs = _per_head(v_ref[pl.ds(off, bq), :])
            r_rights = [jnp.sum(jnp.where(lane == 64 * (h % 2) + kb, rs_blk[:, LANES * (h // 2):LANES * (h // 2 + 1)], 0.0),
                                axis=1, keepdims=True) for h in range(4)]
            dws = [_dot(dos[h], vs[h], "nt") for h in range(4)]
            _, sigs, ws = _sb_blocks(qs, ks, r_rights, masked, bq)
            gs = [ws[h] * dws[h] for h in range(4)]
            gbs = [g.astype(BF16) for g in gs]
            wbs = [w.astype(BF16) for w in ws]
            gbefores = [_dot(gb, tbefore) for gb in gbs]
            dv_acc = [_dot(wbs[2 * p], dos[2 * p], "tn") + _dot(wbs[2 * p + 1], dos[2 * p + 1], "tn") for p in range(2)]
            dzbs = []
            for h in range(4):
                dz = gs[h] - sigs[h] * (gs[h] + gbefores[h] + carry[h][1])
                if masked:
                    dz = jnp.where(ci < ri, dz, 0.0)
                dzbs.append(dz.astype(BF16))
            dqs = [_dot(dzbs[h], ks[h]) for h in range(4)]
            dk_acc = [_dot(dzbs[2 * p], qs[2 * p], "tn") + _dot(dzbs[2 * p + 1], qs[2 * p + 1], "tn") for p in range(2)]
            dk_ref[pl.ds(off, bq), :] += jnp.concatenate(dk_acc, axis=1)
            dv_ref[pl.ds(off, bq), :] += jnp.concatenate(dv_acc, axis=1)
            return tuple((carry[h][0] + dqs[h], carry[h][1] + jnp.sum(gs[h], axis=1, keepdims=True)) for h in range(4))

        zero = (jnp.zeros((bq, LANES), F32), jnp.zeros((bq, 1), F32))
        carry = lax.fori_loop(0, qi, lambda t, c: step(t, c, False), (zero,) * 4)
        carry = step(qi, carry, True)
        dq_ref[...] = jnp.concatenate([jnp.where(lo, carry[0][0], carry[1][0]), jnp.where(lo, carry[2][0], carry[3][0])],
                                      axis=1).astype(dq_ref.dtype)

    blk = pl.BlockSpec((bq, 2 * LANES), lambda i: (i, 0))
    return pl.pallas_call(
        body, name=name, grid=(nq,),
        in_specs=[blk, _resident((lp, 2 * LANES), 1), _resident((lp, 2 * LANES), 2), blk, blk],
        out_specs=[blk, _resident((lp, 2 * LANES), 0), _resident((lp, 2 * LANES), 0)],
        out_shape=[jax.ShapeDtypeStruct((lp, SB_WIDTH), BF16), jax.ShapeDtypeStruct((lp, SB_WIDTH), F32),
                   jax.ShapeDtypeStruct((lp, SB_WIDTH), F32)],
        compiler_params=_cp("arbitrary"),
    )(qkv, qkv, qkv, rs, d_o)


def _mla_masks(bq):
    lane = _lane((bq, 2 * LANES))
    out = []
    for h in range(4):
        j = h % 2
        nope = jnp.logical_and(lane >= 64 * j, lane < 64 * (j + 1))
        rope = jnp.logical_and(lane >= LANES + MLA_ROPE * h, lane < LANES + MLA_ROPE * (h + 1))
        out.append(jnp.logical_or(nope, rope))
    return out


def _mla_split_q(q, masks):
    zero = jnp.zeros((q.shape[0], 2 * LANES), q.dtype)
    return [jnp.where(masks[h], q[:, 2 * LANES * (h // 2):2 * LANES * (h // 2 + 1)], zero) for h in range(4)]


def _mla_per_head_k(k):
    return [k[:, :2 * LANES], k[:, :2 * LANES], k[:, 2 * LANES:], k[:, 2 * LANES:]]


def mla_attn_fwd(qc, kc, v, name):
    lp = qc.shape[0]
    bq = min(ROW_TILE, lp)
    nq = lp // bq

    def body(q_ref, k_ref, v_ref, o_ref, lse_ref):
        qi = pl.program_id(0)
        qs = _mla_split_q(q_ref[...], _mla_masks(bq))
        lo = _lane((bq, LANES)) < 64
        ri = lax.broadcasted_iota(jnp.int32, (bq, bq), 0)
        ci = lax.broadcasted_iota(jnp.int32, (bq, bq), 1)

        def step(kb, carry, masked):
            off = pl.multiple_of(kb * bq, bq)
            ks = _mla_per_head_k(k_ref[pl.ds(off, bq), :])
            vs = _per_head(v_ref[pl.ds(off, bq), :])
            ss = [_dot(qs[h], ks[h], "nt") for h in range(4)]
            prs, alphas, stats = [], [], []
            for h in range(4):
                _, m_run, l_run = carry[h]
                s = ss[h]
                if masked:
                    s = jnp.where(ci <= ri, s, NEG)
                m_new = jnp.maximum(m_run, jnp.max(s, axis=1, keepdims=True))
                alpha = jnp.exp(m_run - m_new)
                pr = jnp.exp(s - m_new)
                prs.append(pr.astype(BF16))
                alphas.append(alpha)
                stats.append((m_new, l_run * alpha + jnp.sum(pr, axis=1, keepdims=True)))
            pvs = [_dot(prs[h], vs[h]) for h in range(4)]
            return tuple((carry[h][0] * alphas[h] + pvs[h],) + stats[h] for h in range(4))

        zero = (jnp.zeros((bq, LANES), F32), jnp.full((bq, 1), NEG, F32), jnp.zeros((bq, 1), F32))
        carry = step(qi, (zero,) * 4, True)
        carry = lax.fori_loop(0, qi, lambda t, c: step(qi - 1 - t, c, False), carry)
        outs = [a / l for a, _, l in carry]
        lses = [m + jnp.log(l) for _, m, l in carry]
        o_ref[...] = jnp.concatenate([jnp.where(lo, outs[0], outs[1]), jnp.where(lo, outs[2], outs[3])], axis=1)
        lse_ref[...] = jnp.concatenate([jnp.where(lo, lses[0], lses[1]), jnp.where(lo, lses[2], lses[3])], axis=1)

    blk = pl.BlockSpec((bq, 2 * LANES), lambda i: (i, 0))
    return pl.pallas_call(
        body, name=name, grid=(nq,),
        in_specs=[pl.BlockSpec((bq, 4 * LANES), lambda i: (i, 0)), _resident((lp, 4 * LANES), 0), _resident((lp, 2 * LANES), 0)],
        out_specs=[blk, blk],
        out_shape=[jax.ShapeDtypeStruct((lp, 2 * LANES), F32), jax.ShapeDtypeStruct((lp, 2 * LANES), F32)],
        compiler_params=_cp("arbitrary"),
    )(qc, kc, v)


def mla_attn_bwd(qc, kc, v, o, lse, d_o, name):
    lp = qc.shape[0]
    bq = min(ROW_TILE, lp)
    nq = lp // bq

    def body(q_ref, k_ref, v_ref, o_ref, lse_ref, do_ref, dq_ref, dk_ref, dv_ref):
        qi = pl.program_id(0)

        @pl.when(qi == 0)
        def _():
            dk_ref[...] = jnp.zeros_like(dk_ref)
            dv_ref[...] = jnp.zeros_like(dv_ref)

        d_o = do_ref[...]
        masks = _mla_masks(bq)
        qs = _mla_split_q(q_ref[...], masks)
        lo = _lane((bq, LANES)) < 64
        dos = _split_heads(d_o.astype(BF16), lo)
        od = o_ref[...] * d_o
        lse_blk = lse_ref[...]
        delta, lses = [], []
        for h in range(4):
            odp = od[:, LANES * (h // 2):LANES * (h // 2 + 1)]
            delta.append(jnp.sum(jnp.where(lo, odp, 0.0) if h % 2 == 0 else jnp.where(lo, 0.0, odp), axis=1, keepdims=True))
            c0 = LANES * (h // 2) + 64 * (h % 2)
            lses.append(lse_blk[:, c0:c0 + 1])
        ri = lax.broadcasted_iota(jnp.int32, (bq, bq), 0)
        ci = lax.broadcasted_iota(jnp.int32, (bq, bq), 1)

        def step(kb, carry, masked):
            off = pl.multiple_of(kb * bq, bq)
            ks = _mla_per_head_k(k_ref[pl.ds(off, bq), :])
            vs = _per_head(v_ref[pl.ds(off, bq), :])
            ss = [_dot(qs[h], ks[h], "nt") for h in range(4)]
            dps = [_dot(dos[h], vs[h], "nt") for h in range(4)]
            prbs, dss = [], []
            for h in range(4):
                s = ss[h]
                if masked:
                    s = jnp.where(ci <= ri, s, NEG)
                pr = jnp.exp(s - lses[h])
                prbs.append(pr.astype(BF16))
                dss.append((pr * (dps[h] - delta[h])).astype(BF16))
            dv_acc = [_dot(prbs[2 * p], dos[2 * p], "tn") + _dot(prbs[2 * p + 1], dos[2 * p + 1], "tn") for p in range(2)]
            dqs = [_dot(dss[h], ks[h]) for h in range(4)]
            dk_acc = [_dot(dss[2 * p], qs[2 * p], "tn") + _dot(dss[2 * p + 1], qs[2 * p + 1], "tn") for p in range(2)]
            dk_ref[pl.ds(off, bq), :] += jnp.concatenate(dk_acc, axis=1)
            dv_ref[pl.ds(off, bq), :] += jnp.concatenate(dv_acc, axis=1)
            return tuple(carry[h] + dqs[h] for h in range(4))

        zero = jnp.zeros((bq, 2 * LANES), F32)
        carry = step(qi, (zero,) * 4, True)
        carry = lax.fori_loop(0, qi, lambda t, c: step(qi - 1 - t, c, False), carry)
        dq_ref[...] = jnp.concatenate([jnp.where(masks[0], carry[0], 0.0) + jnp.where(masks[1], carry[1], 0.0),
                                       jnp.where(masks[2], carry[2], 0.0) + jnp.where(masks[3], carry[3], 0.0)], axis=1)

    blk = pl.BlockSpec((bq, 2 * LANES), lambda i: (i, 0))
    wide = pl.BlockSpec((bq, 4 * LANES), lambda i: (i, 0))
    return pl.pallas_call(
        body, name=name, grid=(nq,),
        in_specs=[wide, _resident((lp, 4 * LANES), 0), _resident((lp, 2 * LANES), 0), blk, blk, blk],
        out_specs=[wide, _resident((lp, 4 * LANES), 0), _resident((lp, 2 * LANES), 0)],
        out_shape=[jax.ShapeDtypeStruct((lp, 4 * LANES), F32), jax.ShapeDtypeStruct((lp, 4 * LANES), F32),
                   jax.ShapeDtypeStruct((lp, 2 * LANES), F32)],
        compiler_params=_cp("arbitrary"),
    )(qc, kc, v, o, lse, d_o)


def _mix_out(y_pre, z, o_sb, o_mla, g_ssd, g_sb, g_mla):
    return jnp.concatenate([_rms(y_pre * _silu(z), g_ssd), _rms(o_sb, g_sb), _rms(o_mla, g_mla)], axis=1)


def _ffn_act(up_a, up_b, halo_a, halo_b, w_a, w_b, b_a, b_b, i):
    ca = conv_fwd(up_a, halo_a, w_a, i) + b_a
    cb_ = conv_fwd(up_b, halo_b, w_b, i) + b_b
    return ca, cb_


def layer_fwd(h, w, cs, sn, l):
    lp = h.shape[0]
    nm = f"l{l}_"
    hn = rowwise(lambda i, n, x, g: _rms(x, g), [RI(h), PA(w["norm_mix_g"])], [RO(D_MODEL, BF16)], nm + "rms_mix", lp)[0]
    u = mm(hn, w["w_main"], "nn", F32, nm + "in_main")
    qkv = mm(hn, w["w_sb"], "nn", BF16, nm + "in_sb")
    xbc_c = rowwise(lambda i, n, x, hl, cw, cb_: _silu(conv_fwd(x, hl, cw, i) + cb_),
                    [RI(u, SSD_XBC, 0), HP(u, SSD_XBC, 0), PA(w["ssd_conv_w"]), PA(w["ssd_conv_b"])],
                    [RO(SSD_XBC, F32)], nm + "ssd_conv", lp)[0]
    y_pre, hprev = ssd_fwd(xbc_c, u, w["dt_bias"], w["a_log"], w["d_skip"], nm + "ssd_fwd")
    o_sb, rs_sb = sb_attn_fwd(qkv, nm + "sb_fwd")
    qn, kvn = rowwise(lambda i, n, qa, ckv, gq, gkv: (_rms(qa, gq, MLA_Q_RANK), _rms(ckv, gkv)),
                      [RI(u, 256, U_QA // 256), RI(u, LANES, U_CKV // LANES), PA(w["q_norm_g"]), PA(w["kv_norm_g"])],
                      [RO(256, BF16), RO(LANES, BF16)], nm + "mla_rms", lp)
    qf = mm(qn, w["w_uq"], "nn", F32, nm + "mla_uq")
    kvf = mm(kvn, w["w_ukv"], "nn", F32, nm + "mla_ukv")

    def pack(i, n, qf_, kvf_, kr4, cos, sin):
        qf_ = qf_ * MLA_SCALE
        qr = qf_[:, 256:384]
        qr = qr * cos + rope_rot(qr) * sin
        kr = kr4 * cos + rope_rot(kr4) * sin
        qc = jnp.concatenate([qf_[:, 0:128], qr, qf_[:, 128:256], qr], axis=1)
        kc = jnp.concatenate([kvf_[:, 0:128], kr, kvf_[:, 128:256], kr], axis=1)
        return qc, kc, kvf_[:, 256:512]

    qc, kc, vv = rowwise(pack, [RI(qf), RI(kvf), RI(u, LANES, U_KR4 // LANES), RI(cs), RI(sn)],
                         [RO(512, BF16), RO(512, BF16), RO(256, BF16)], nm + "mla_pack", lp)
    o_mla, lse = mla_attn_fwd(qc, kc, vv, nm + "mla_fwd")
    cat = rowwise(lambda i, n, *a: _mix_out(*a),
                  [RI(y_pre), RI(u, SSD_WIDTH, U_Z // SSD_WIDTH), RI(o_sb), RI(o_mla),
                   PA(w["ssd_norm_g"]), PA(w["sb_norm_g"]), PA(w["mla_norm_g"])],
                  [RO(D_MODEL, BF16)], nm + "mix_out", lp)[0]
    h_mid = mm(cat, w["w_out"], "nn", F32, nm + "out_proj", add=h)
    hn2 = rowwise(lambda i, n, x, g: _rms(x, g), [RI(h_mid), PA(w["norm_ffn_g"])], [RO(D_MODEL, BF16)], nm + "rms_ffn", lp)[0]
    up_a = mm(hn2, w["w_up_a"], "nn", F32, nm + "up_a")
    up_b = mm(hn2, w["w_up_b"], "nn", F32, nm + "up_b")
    wc = 1408

    def act(i, n, ua, ub, ha, hb_, wa, wb, ba, bb_):
        ca, cb_ = _ffn_act(ua, ub, ha, hb_, wa, wb, ba, bb_, i)
        return _silu(ca) * cb_

    a_t = rowwise(act, [RI(up_a, wc, 0, True), RI(up_b, wc, 0, True), HP(up_a, wc, 0, True), HP(up_b, wc, 0, True),
                        PA(w["ffn_conv_w_a"], wc, 0, True), PA(w["ffn_conv_w_b"], wc, 0, True),
                        PA(w["ffn_conv_b_a"], wc, 0, True), PA(w["ffn_conv_b_b"], wc, 0, True)],
                  [RO(D_FF, BF16, wc, True)], nm + "ffn_act", lp, ncol=D_FF // wc)[0]
    h_out = mm(a_t, w["w_down"], "nn", F32, nm + "down", add=h_mid)
    saved = dict(h=h, hn=hn, u=u, qkv=qkv, xbc_c=xbc_c, y_pre=y_pre, hprev=hprev, o_sb=o_sb, rs_sb=rs_sb, qn=qn, kvn=kvn,
                 qc=qc, kc=kc, vv=vv, o_mla=o_mla, lse=lse, cat=cat, h_mid=h_mid, hn2=hn2, up_a=up_a, up_b=up_b, a_t=a_t)
    return h_out, saved


def layer_bwd(dh_out, w, s, cs, sn, l):
    lp = dh_out.shape[0]
    nm = f"l{l}b_"
    g = {}
    wc = 1408
    ncolf = D_FF // wc
    g["w_down"] = mm(s["a_t"], dh_out, "tn", F32, nm + "dw_down")
    d_act = mm(dh_out, w["w_down"], "nt", F32, nm + "d_act")

    def act_bwd(i, n, ua, ub, ha, hb_, wa, wb, ba, bb_, da_):
        ca, cb_ = _ffn_act(ua, ub, ha, hb_, wa, wb, ba, bb_, i)
        sg = jax.nn.sigmoid(ca)
        dca = da_ * cb_ * (sg * (1.0 + ca * (1.0 - sg)))
        dcb = da_ * (ca * sg)
        return (dca, dcb, conv_bwd_w(dca, ua, ha, i, FFN_CONV), conv_bwd_w(dcb, ub, hb_, i, FFN_CONV),
                jnp.sum(dca, axis=0, keepdims=True), jnp.sum(dcb, axis=0, keepdims=True))

    dca, dcb, g["ffn_conv_w_a"], g["ffn_conv_w_b"], g["ffn_conv_b_a"], g["ffn_conv_b_b"] = rowwise(
        act_bwd, [RI(s["up_a"], wc, 0, True), RI(s["up_b"], wc, 0, True), HP(s["up_a"], wc, 0, True),
                  HP(s["up_b"], wc, 0, True), PA(w["ffn_conv_w_a"], wc, 0, True), PA(w["ffn_conv_w_b"], wc, 0, True),
                  PA(w["ffn_conv_b_a"], wc, 0, True), PA(w["ffn_conv_b_b"], wc, 0, True), RI(d_act, wc, 0, True)],
        [RO(D_FF, F32, wc, True), RO(D_FF, F32, wc, True), AO(FFN_CONV, D_FF, wc, True), AO(FFN_CONV, D_FF, wc, True),
         AO(1, D_FF, wc, True), AO(1, D_FF, wc, True)], nm + "ffn_act_bwd", lp, ncol=ncolf)

    def conv_t(i, n, da_, db_, ha, hb_, wa, wb):
        return conv_bwd_data(da_, ha, wa, i, n), conv_bwd_data(db_, hb_, wb, i, n)

    dup_a, dup_b = rowwise(conv_t, [RI(dca, wc, 0, True), RI(dcb, wc, 0, True), HN(dca, wc, 0, True), HN(dcb, wc, 0, True),
                                    PA(w["ffn_conv_w_a"], wc, 0, True), PA(w["ffn_conv_w_b"], wc, 0, True)],
                           [RO(D_FF, BF16, wc, True), RO(D_FF, BF16, wc, True)], nm + "ffn_conv_t", lp, ncol=ncolf)
    g["w_up_a"] = mm(s["hn2"], dup_a, "tn", F32, nm + "dw_up_a")
    g["w_up_b"] = mm(s["hn2"], dup_b, "tn", F32, nm + "dw_up_b")
    dhn2 = mm(dup_a, w["w_up_a"], "nt", F32, nm + "dhn2_a")
    dhn2 = mm(dup_b, w["w_up_b"], "nt", F32, nm + "dhn2_b", add=dhn2)

    def rms_bwd(i, n, x, gg, dy, dres):
        _, vjp = jax.vjp(_rms, x, gg)
        dx, dg = vjp(dy)
        return dres + dx, dg

    dh_mid, g["norm_ffn_g"] = rowwise(rms_bwd, [RI(s["h_mid"]), PA(w["norm_ffn_g"]), RI(dhn2), RI(dh_out)],
                                      [RO(D_MODEL, F32), AO(1, D_MODEL)], nm + "rms_ffn_bwd", lp)
    g["w_out"] = mm(s["cat"], dh_mid, "tn", F32, nm + "dw_out")
    d_cat = mm(dh_mid, w["w_out"], "nt", F32, nm + "d_cat")
    u = s["u"]

    def mix_bwd(i, n, y_pre, z, o_sb, o_mla, g1, g2, g3, dcat):
        _, vjp = jax.vjp(_mix_out, y_pre, z, o_sb, o_mla, g1, g2, g3)
        return vjp(dcat)

    dy_pre, dz, do_sb, do_mla, g["ssd_norm_g"], g["sb_norm_g"], g["mla_norm_g"] = rowwise(
        mix_bwd, [RI(s["y_pre"]), RI(u, SSD_WIDTH, U_Z // SSD_WIDTH), RI(s["o_sb"]), RI(s["o_mla"]),
                  PA(w["ssd_norm_g"]), PA(w["sb_norm_g"]), PA(w["mla_norm_g"]), RI(d_cat)],
        [RO(SSD_WIDTH, F32), RO(SSD_WIDTH, BF16), RO(SB_WIDTH, F32), RO(256, F32),
         AO(1, SSD_WIDTH), AO(1, SB_WIDTH), AO(1, 256)], nm + "mix_out_bwd", lp)
    dxs, dbp, dcp, ddtp, pg = ssd_bwd(s["xbc_c"], u, w["dt_bias"], w["a_log"], w["d_skip"], s["hprev"], dy_pre, nm + "ssd_bwd")
    pg = pg.reshape(4, SUBLANES, LANES).sum(axis=0)
    g["dt_bias"], g["a_log"], g["d_skip"] = pg[0:1], pg[1:2], pg[2:3]

    def conv4_bwd(i, n, x, hl, cw, cb_, dxs_, dbp_, dcp_, ddtp_):
        pre = conv_fwd(x, hl, cw, i) + cb_
        d_b = jnp.concatenate([dbp_[:, 0:128] + dbp_[:, 128:256], dbp_[:, 256:384] + dbp_[:, 384:512]], axis=1)
        d_c = jnp.concatenate([dcp_[:, 0:128] + dcp_[:, 128:256], dcp_[:, 256:384] + dcp_[:, 384:512]], axis=1)
        d_out = jnp.concatenate([dxs_, d_b, d_c], axis=1)
        sg = jax.nn.sigmoid(pre)
        d_pre = d_out * (sg * (1.0 + pre * (1.0 - sg)))
        ddt = ddtp_[:, 0:128] + ddtp_[:, 128:256] + ddtp_[:, 256:384] + ddtp_[:, 384:512]
        return d_pre, ddt, conv_bwd_w(d_pre, x, hl, i, SSD_CONV), jnp.sum(d_pre, axis=0, keepdims=True)

    d_pre, ddt, g["ssd_conv_w"], g["ssd_conv_b"] = rowwise(
        conv4_bwd, [RI(u, SSD_XBC, 0), HP(u, SSD_XBC, 0), PA(w["ssd_conv_w"]), PA(w["ssd_conv_b"]),
                    RI(dxs), RI(dbp), RI(dcp), RI(ddtp)],
        [RO(SSD_XBC, F32), RO(LANES, BF16), AO(SSD_CONV, SSD_XBC), AO(1, SSD_XBC)], nm + "ssd_conv_bwd", lp)
    d_xbc = rowwise(lambda i, n, d, hn_, cw: conv_bwd_data(d, hn_, cw, i, n),
                    [RI(d_pre), HN(d_pre), PA(w["ssd_conv_w"])], [RO(SSD_XBC, BF16)], nm + "ssd_conv_t", lp)[0]
    dq_sb, dk_sb, dv_sb = sb_attn_bwd(s["qkv"], s["rs_sb"], do_sb, nm + "sb_bwd")
    dqkv = jnp.concatenate([dq_sb, dk_sb.astype(BF16), dv_sb.astype(BF16)], axis=1)
    dqc, dkc, dvv = mla_attn_bwd(s["qc"], s["kc"], s["vv"], s["o_mla"], s["lse"], do_mla, nm + "mla_bwd")

    def unpack(i, n, dqc_, dkc_, dvv_, cos, sin):
        dqr = dqc_[:, 128:256] + dqc_[:, 384:512]
        dqr = dqr * cos + rope_rot_t(dqr * sin)
        dkr = dkc_[:, 128:256] + dkc_[:, 384:512]
        dkr = dkr * cos + rope_rot_t(dkr * sin)
        dq = jnp.concatenate([dqc_[:, 0:128], dqc_[:, 256:384], dqr], axis=1) * MLA_SCALE
        dkv = jnp.concatenate([dkc_[:, 0:128], dkc_[:, 256:384], dvv_], axis=1)
        return dq, dkv, dkr

    dq, dkv, dkr4 = rowwise(unpack, [RI(dqc), RI(dkc), RI(dvv), RI(cs), RI(sn)],
                            [RO(384, BF16), RO(512, BF16), RO(LANES, BF16)], nm + "mla_unpack", lp)
    g["w_uq"] = mm(s["qn"], dq, "tn", F32, nm + "dw_uq")
    g["w_ukv"] = mm(s["kvn"], dkv, "tn", F32, nm + "dw_ukv")
    dqn = mm(dq, w["w_uq"], "nt", F32, nm + "dqn")
    dkvn = mm(dkv, w["w_ukv"], "nt", F32, nm + "dkvn")

    def mla_rms_bwd(i, n, qa, ckv, gq, gkv, dqn_, dkvn_):
        _, vjp = jax.vjp(lambda a, b, c, d: (_rms(a, c, MLA_Q_RANK), _rms(b, d)), qa, ckv, gq, gkv)
        return vjp((dqn_, dkvn_))

    dqa, dckv, g["q_norm_g"], g["kv_norm_g"] = rowwise(
        mla_rms_bwd, [RI(u, 256, U_QA // 256), RI(u, LANES, U_CKV // LANES), PA(w["q_norm_g"]), PA(w["kv_norm_g"]),
                      RI(dqn), RI(dkvn)],
        [RO(256, BF16), RO(LANES, BF16), AO(1, 256), AO(1, LANES)], nm + "mla_rms_bwd", lp)
    du = jnp.concatenate([d_xbc, dz, dqa, dckv, dkr4, ddt, jnp.zeros((lp, LANES), BF16)], axis=1)
    g["w_main"] = mm(s["hn"], du, "tn", F32, nm + "dw_main")
    g["w_sb"] = mm(s["hn"], dqkv, "tn", F32, nm + "dw_sb")
    dhn = mm(du, w["w_main"], "nt", F32, nm + "dhn_main")
    dhn = mm(dqkv, w["w_sb"], "nt", F32, nm + "dhn_sb", add=dhn)
    dh_in, g["norm_mix_g"] = rowwise(rms_bwd, [RI(s["h"]), PA(w["norm_mix_g"]), RI(dhn), RI(dh_mid)],
                                     [RO(D_MODEL, F32), AO(1, D_MODEL)], nm + "rms_mix_bwd", lp)
    return dh_in, g


_IN_CUTS = np.cumsum((512, 1024, 8, 256, 256, 256, 192, 128, 32))


def _pad_cols(a, n):
    return jnp.pad(a, ((0, 0), (0, n - a.shape[1])))


def prep_layer_weights(full, l):
    w_in = full["w_in"][l]
    c = _IN_CUTS
    z, xbc, dtr = w_in[:, :c[0]], w_in[:, c[0]:c[1]], w_in[:, c[1]:c[2]]
    q_sb, k_sb, v_sb = w_in[:, c[2]:c[3]], w_in[:, c[3]:c[4]], w_in[:, c[4]:c[5]]
    q_a, c_kv, k_r = w_in[:, c[5]:c[6]], w_in[:, c[6]:c[7]], w_in[:, c[7]:c[8]]
    w_main = jnp.concatenate([xbc, z, _pad_cols(q_a, 256), c_kv, k_r, k_r, k_r, k_r, _pad_cols(dtr, 256)], axis=1)
    assert w_main.shape[1] == U_MAIN
    row = lambda v, n=None: _pad_cols(v.reshape(1, -1).astype(F32), v.size if n is None else n)
    uq = full["mla_w_uq"][l].reshape(MLA_Q_RANK, 4, 96)
    w_uq = jnp.concatenate([uq[:, :, :64].reshape(MLA_Q_RANK, 256), uq[:, :, 64:].reshape(MLA_Q_RANK, 128)], axis=1)
    w_uq = jnp.pad(w_uq, ((0, 256 - MLA_Q_RANK), (0, 0)))
    ukv = full["mla_w_ukv"][l].reshape(MLA_KV_RANK, 4, 128)
    w_ukv = jnp.concatenate([ukv[:, :, :64].reshape(MLA_KV_RANK, 256), ukv[:, :, 64:].reshape(MLA_KV_RANK, 256)], axis=1)
    return dict(
        norm_mix_g=row(full["norm_mix_g"][l]), w_main=w_main, w_sb=jnp.concatenate([q_sb * SB_SCALE, k_sb, v_sb], axis=1),
        ssd_conv_w=full["ssd_conv_w"][l], ssd_conv_b=row(full["ssd_conv_b"][l]),
        dt_bias=row(full["ssd_dt_bias"][l], LANES), a_log=row(full["ssd_a_log"][l], LANES), d_skip=row(full["ssd_d"][l], LANES),
        ssd_norm_g=row(full["ssd_norm_g"][l]), sb_norm_g=row(full["sb_norm_g"][l]),
        q_norm_g=row(full["mla_q_norm_g"][l], 256), kv_norm_g=row(full["mla_kv_norm_g"][l]),
        w_uq=w_uq, w_ukv=w_ukv, mla_norm_g=row(full["mla_norm_g"][l]),
        w_out=full["w_out"][l], norm_ffn_g=row(full["norm_ffn_g"][l]),
        w_up_a=full["ffn_w_up"][l][:, :D_FF], w_up_b=full["ffn_w_up"][l][:, D_FF:],
        ffn_conv_w_a=full["ffn_conv_w"][l][:, :D_FF], ffn_conv_w_b=full["ffn_conv_w"][l][:, D_FF:],
        ffn_conv_b_a=row(full["ffn_conv_b"][l][:D_FF]), ffn_conv_b_b=row(full["ffn_conv_b"][l][D_FF:]),
        w_down=full["ffn_w_down"][l],
    )


def unprep_layer_grads(g):
    wm = g["w_main"]
    xbc, z = wm[:, U_XBC:U_XBC + 1024], wm[:, U_Z:U_Z + 512]
    q_a, c_kv = wm[:, U_QA:U_QA + MLA_Q_RANK], wm[:, U_CKV:U_CKV + 128]
    k_r = (wm[:, U_KR4:U_KR4 + 32] + wm[:, U_KR4 + 32:U_KR4 + 64] + wm[:, U_KR4 + 64:U_KR4 + 96] + wm[:, U_KR4 + 96:U_KR4 + 128])
    dtr = wm[:, U_DT:U_DT + SSD_HEADS]
    w_sb = g["w_sb"]
    w_in = jnp.concatenate([z, xbc, dtr, w_sb[:, :SB_WIDTH] * SB_SCALE, w_sb[:, SB_WIDTH:], q_a, c_kv, k_r], axis=1)
    guq = g["w_uq"][:MLA_Q_RANK]
    guq = jnp.concatenate([guq[:, :256].reshape(MLA_Q_RANK, 4, 64), guq[:, 256:].reshape(MLA_Q_RANK, 4, 32)], axis=2)
    gukv = g["w_ukv"]
    gukv = jnp.concatenate([gukv[:, :256].reshape(MLA_KV_RANK, 4, 64), gukv[:, 256:].reshape(MLA_KV_RANK, 4, 64)], axis=2)
    return dict(
        norm_mix_g=g["norm_mix_g"][0], w_in=w_in, ssd_conv_w=g["ssd_conv_w"], ssd_conv_b=g["ssd_conv_b"][0],
        ssd_dt_bias=g["dt_bias"][0, :SSD_HEADS], ssd_a_log=g["a_log"][0, :SSD_HEADS], ssd_d=g["d_skip"][0, :SSD_HEADS],
        ssd_norm_g=g["ssd_norm_g"][0], sb_norm_g=g["sb_norm_g"][0], mla_q_norm_g=g["q_norm_g"][0, :MLA_Q_RANK],
        mla_kv_norm_g=g["kv_norm_g"][0], mla_w_uq=guq.reshape(MLA_Q_RANK, 384), mla_w_ukv=gukv.reshape(MLA_KV_RANK, 512),
        mla_norm_g=g["mla_norm_g"][0], w_out=g["w_out"], norm_ffn_g=g["norm_ffn_g"][0],
        ffn_w_up=jnp.concatenate([g["w_up_a"], g["w_up_b"]], axis=1),
        ffn_conv_w=jnp.concatenate([g["ffn_conv_w_a"], g["ffn_conv_w_b"]], axis=1),
        ffn_conv_b=jnp.concatenate([g["ffn_conv_b_a"][0], g["ffn_conv_b_b"][0]], axis=0),
        ffn_w_down=g["w_down"],
    )


def rope_tables(lp):
    pos = jnp.arange(lp, dtype=F32)
    inv = 1.0 / (ROPE_BASE ** (jnp.arange(0, MLA_ROPE, 2, dtype=F32) / MLA_ROPE))
    ang = pos[:, None] * inv[None, :]
    ang = jnp.concatenate([ang, ang] * 4, axis=-1)
    return jnp.cos(ang), jnp.sin(ang)


def local_step(x_seq, target, full):
    seq = x_seq.shape[0]
    length = seq + N_META
    lp = -(-length // ROW_TILE) * ROW_TILE
    cs, sn = rope_tables(lp)
    h = jnp.concatenate([full["meta_tokens"].astype(F32), x_seq, jnp.zeros((lp - length, D_MODEL), F32)], axis=0)
    tgt = jnp.pad(target, ((N_META, lp - length), (0, 0)))
    ws = [prep_layer_weights(full, l) for l in range(DEPTH)]
    saved = []
    for l in range(DEPTH):
        h, s = layer_fwd(h, ws[l], cs, sn, l)
        saved.append(s)
    fg = full["final_norm_g"].reshape(1, D_MODEL).astype(F32)
    tm = min(ROW_TILE, lp)

    def loss_fn(i, n, x, g, t):
        rows = _rows_iota(x) + i * tm
        valid = jnp.logical_and(rows >= N_META, rows < length)

        def f(x_, g_):
            err = jnp.where(valid, _rms(x_, g_) - t, 0.0)
            return 0.5 * jnp.sum(err * err) * (1.0 / D_MODEL)

        val, (dx, dg) = jax.value_and_grad(f, argnums=(0, 1))(x, g)
        return dx, jnp.full((1, LANES), val, F32), dg

    dh, loss_row, g_final = rowwise(loss_fn, [RI(h), PA(fg), RI(tgt)], [RO(D_MODEL, F32), AO(1, LANES), AO(1, D_MODEL)],
                                    "loss_head", lp)
    grads = {}
    per_layer = [None] * DEPTH
    for l in reversed(range(DEPTH)):
        dh, g = layer_bwd(dh, ws[l], saved[l], cs, sn, l)
        per_layer[l] = unprep_layer_grads(g)
    for k in per_layer[0]:
        grads[k] = jnp.stack([per_layer[l][k] for l in range(DEPTH)], axis=0)
    grads["final_norm_g"] = g_final[0]
    grads["meta_tokens"] = dh[:N_META]
    return loss_row[0, 0], dh[N_META:length], grads


_ANY = pl.BlockSpec(memory_space=pl.ANY)


def chip_exchange(srcs, modes, name):
    n = len(srcs)
    flips = ((1, 0), (0, 1), (1, 1))

    def body(*refs):
        ins, outs = refs[:n], refs[n:2 * n]
        send_sems, recv_sems, fwd_send_sems, fwd_recv_sems, loc_sems = refs[2 * n:]
        x, y, c = lax.axis_index("x"), lax.axis_index("y"), lax.axis_index("c")
        me = 2 * x + y
        waits, forwards = [], []
        for a in range(n):
            whole = modes[a] != "slab"
            cp = pltpu.make_async_copy(ins[a] if whole else ins[a].at[me], outs[a].at[me], loc_sems.at[a])
            cp.start()
            waits.append(cp.wait)
            half = ins[a].shape[0] // 2 if modes[a] == "bcast_split" else None
            for k, (fx, fy) in enumerate(flips):
                px = 1 - x if fx else x
                py = 1 - y if fy else y
                peer = 2 * px + py
                if half is None:
                    src = ins[a] if whole else ins[a].at[peer]
                    dst = outs[a].at[me]
                else:
                    src = ins[a].at[pl.ds(c * half, half)]
                    dst = outs[a].at[me, pl.ds(c * half, half)]
                rc = pltpu.make_async_remote_copy(src_ref=src, dst_ref=dst, send_sem=send_sems.at[a, k],
                                                  recv_sem=recv_sems.at[a, k], device_id=(px, py, c), device_id_type=MESH_ID)
                rc.start()
                if half is None:
                    waits.append(rc.wait)
                else:
                    waits.append(rc.wait_send)
                    landed = outs[a].at[peer, pl.ds(c * half, half)]
                    fw = pltpu.make_async_remote_copy(src_ref=landed, dst_ref=landed, send_sem=fwd_send_sems.at[a, k],
                                                      recv_sem=fwd_recv_sems.at[a, k], device_id=(x, y, 1 - c),
                                                      device_id_type=MESH_ID)
                    forwards.append((rc, fw))
        for rc, fw in forwards:
            rc.wait_recv()
            fw.start()
        for rc, fw in forwards:
            fw.wait()
        for w in waits:
            w()

    out_shape = [jax.ShapeDtypeStruct((N_CHIPS,) + (s.shape if m != "slab" else s.shape[1:]), s.dtype) for s, m in zip(srcs, modes)]
    return pl.pallas_call(
        body, name=name, in_specs=[_ANY] * n, out_specs=[_ANY] * n, out_shape=out_shape,
        scratch_shapes=[pltpu.SemaphoreType.DMA((n, 3)), pltpu.SemaphoreType.DMA((n, 3)), pltpu.SemaphoreType.DMA((n, 3)),
                        pltpu.SemaphoreType.DMA((n, 3)), pltpu.SemaphoreType.DMA((n,))],
    )(*srcs)


def sibling_swap(srcs, name):
    n = len(srcs)

    def body(*refs):
        ins, outs = refs[:n], refs[n:2 * n]
        send_sems, recv_sems = refs[2 * n:]
        x, y, c = lax.axis_index("x"), lax.axis_index("y"), lax.axis_index("c")
        copies = []
        for a in range(n):
            rc = pltpu.make_async_remote_copy(src_ref=ins[a], dst_ref=outs[a], send_sem=send_sems.at[a],
                                              recv_sem=recv_sems.at[a], device_id=(x, y, 1 - c), device_id_type=MESH_ID)
            rc.start()
            copies.append(rc)
        for cp in copies:
            cp.wait()

    return pl.pallas_call(
        body, name=name, in_specs=[_ANY] * n, out_specs=[_ANY] * n,
        out_shape=[jax.ShapeDtypeStruct(s.shape, s.dtype) for s in srcs],
        scratch_shapes=[pltpu.SemaphoreType.DMA((n,)), pltpu.SemaphoreType.DMA((n,))],    )(*srcs)


WEIGHT_NAMES = ("meta_tokens", "norm_mix_g", "w_in", "ssd_conv_w", "ssd_conv_b", "ssd_dt_bias", "ssd_a_log", "ssd_d",
                "ssd_norm_g", "sb_norm_g", "mla_q_norm_g", "mla_kv_norm_g", "mla_w_uq", "mla_w_ukv", "mla_norm_g",
                "w_out", "norm_ffn_g", "ffn_w_up", "ffn_conv_w", "ffn_conv_b", "ffn_w_down", "final_norm_g")
SHARD_AXIS = {"meta_tokens": 1, "w_in": 2, "ssd_conv_w": 2, "mla_w_uq": 2, "mla_w_ukv": 2, "w_out": 1, "ffn_w_up": 2,
              "ffn_conv_w": 2, "ffn_w_down": 1}
SHARDED = tuple(n for n in WEIGHT_NAMES if n in SHARD_AXIS)
REPLICATED = tuple(n for n in WEIGHT_NAMES if n not in SHARD_AXIS)
GATHER_BF16 = ("w_in", "mla_w_uq", "mla_w_ukv", "w_out", "ffn_w_up", "ffn_w_down")
GATHER_F32 = ("meta_tokens", "ssd_conv_w", "ffn_conv_w")
PACK_ROWS = ROW_TILE


def pack(arrs, dtype):
    flat = jnp.concatenate([a.reshape(-1).astype(dtype) for a in arrs])
    per = PACK_ROWS * PACK_W
    total = -(-flat.size // per) * per
    return jnp.pad(flat, (0, total - flat.size)).reshape(total // PACK_W, PACK_W)


def unpack(buf, shapes):
    flat = buf.reshape(-1)
    out, off = [], 0
    for shp in shapes:
        size = int(np.prod(shp))
        out.append(flat[off:off + size].reshape(shp))
        off += size
    return out


def gather_weights(a):
    full = {n: a[n] for n in REPLICATED}
    bufs = [pack([a[n] for n in GATHER_BF16], BF16), pack([a[n] for n in GATHER_F32], F32)]
    got = chip_exchange(bufs, ("bcast_split", "bcast"), "gather_weights")
    for names, g in ((GATHER_BF16, got[0]), (GATHER_F32, got[1])):
        pieces = [unpack(g[k], [a[n].shape for n in names]) for k in range(N_CHIPS)]
        for idx, n in enumerate(names):
            full[n] = jnp.concatenate([pieces[k][idx] for k in range(N_CHIPS)], axis=SHARD_AXIS[n])
    return full


def _adamw(i, n, p, q, w, m, v):
    g = p + q
    m = ADAM_B1 * m + (1.0 - ADAM_B1) * g
    v = ADAM_B2 * v + (1.0 - ADAM_B2) * jnp.square(g)
    m_hat = m / (1.0 - ADAM_B1 ** ADAM_STEP)
    v_hat = v / (1.0 - ADAM_B2 ** ADAM_STEP)
    delta = -ADAM_LR * (m_hat / (jnp.sqrt(v_hat) + ADAM_EPS) + ADAM_WD * w)
    return g, delta, m, v


def reduce_and_update(a, grads):
    sh_shapes = [a[n].shape for n in SHARDED]
    rep_shapes = [a[n].shape for n in REPLICATED]
    slabs = []
    for k in range(N_CHIPS):
        parts = []
        for n in SHARDED:
            ax = SHARD_AXIS[n]
            size = a[n].shape[ax]
            parts.append(lax.slice_in_dim(grads[n], k * size, (k + 1) * size, axis=ax))
        slabs.append(pack(parts, BF16))
    g_sh = jnp.stack(slabs, axis=0)
    g_rep = pack([grads[n] for n in REPLICATED], F32)
    got_sh, got_rep = chip_exchange([g_sh, g_rep], ("slab", "bcast"), "exchange_grads")
    sums = []
    for tag, got in (("sh", got_sh), ("rep", got_rep)):
        rows = got.shape[1]
        blk = rows // PACK_ROWS
        flat = got.reshape(N_CHIPS * rows, PACK_W)
        sums.append(rowwise(lambda i, n, g0, g1, g2, g3: ((g0.astype(F32) + g1.astype(F32)) + g2.astype(F32)) + g3.astype(F32),
                            [RI(flat, rblk=k * blk) for k in range(N_CHIPS)], [RO(PACK_W, F32)], "sum_chips_" + tag, rows,
                            tm=PACK_ROWS)[0])
    others = sibling_swap(sums, "swap_cores")
    outs = {}
    for tag, names, shapes, p, q in (("sh", SHARDED, sh_shapes, sums[0], others[0]),
                                     ("rep", REPLICATED, rep_shapes, sums[1], others[1])):
        packed = [pack([a[pre + n] for n in names], F32) for pre in ("", "m_", "v_")]
        res = rowwise(_adamw, [RI(p), RI(q)] + [RI(t) for t in packed], [RO(PACK_W, F32)] * 4, "adamw_" + tag,
                      p.shape[0], tm=PACK_ROWS)
        for kind, buf in zip(("grad", "delta", "new_m", "new_v"), res):
            for n, val in zip(names, unpack(buf, shapes)):
                outs[(kind, n)] = val
    return outs


INPUT_NAMES = ("x",) + WEIGHT_NAMES + ("loss_target",) + tuple("m_" + n for n in WEIGHT_NAMES) + tuple("v_" + n for n in WEIGHT_NAMES)


def kernel(x, meta_tokens, norm_mix_g, w_in, ssd_conv_w, ssd_conv_b, ssd_dt_bias, ssd_a_log, ssd_d, ssd_norm_g, sb_norm_g, mla_q_norm_g, mla_kv_norm_g, mla_w_uq, mla_w_ukv, mla_norm_g, w_out, norm_ffn_g, ffn_w_up, ffn_conv_w, ffn_conv_b, ffn_w_down, final_norm_g, loss_target, m_meta_tokens, m_norm_mix_g, m_w_in, m_ssd_conv_w, m_ssd_conv_b, m_ssd_dt_bias, m_ssd_a_log, m_ssd_d, m_ssd_norm_g, m_sb_norm_g, m_mla_q_norm_g, m_mla_kv_norm_g, m_mla_w_uq, m_mla_w_ukv, m_mla_norm_g, m_w_out, m_norm_ffn_g, m_ffn_w_up, m_ffn_conv_w, m_ffn_conv_b, m_ffn_w_down, m_final_norm_g, v_meta_tokens, v_norm_mix_g, v_w_in, v_ssd_conv_w, v_ssd_conv_b, v_ssd_dt_bias, v_ssd_a_log, v_ssd_d, v_ssd_norm_g, v_sb_norm_g, v_mla_q_norm_g, v_mla_kv_norm_g, v_mla_w_uq, v_mla_w_ukv, v_mla_norm_g, v_w_out, v_norm_ffn_g, v_ffn_w_up, v_ffn_conv_w, v_ffn_conv_b, v_ffn_w_down, v_final_norm_g):
    args = (x, meta_tokens, norm_mix_g, w_in, ssd_conv_w, ssd_conv_b, ssd_dt_bias, ssd_a_log, ssd_d, ssd_norm_g, sb_norm_g, mla_q_norm_g, mla_kv_norm_g, mla_w_uq, mla_w_ukv, mla_norm_g, w_out, norm_ffn_g, ffn_w_up, ffn_conv_w, ffn_conv_b, ffn_w_down, final_norm_g, loss_target, m_meta_tokens, m_norm_mix_g, m_w_in, m_ssd_conv_w, m_ssd_conv_b, m_ssd_dt_bias, m_ssd_a_log, m_ssd_d, m_ssd_norm_g, m_sb_norm_g, m_mla_q_norm_g, m_mla_kv_norm_g, m_mla_w_uq, m_mla_w_ukv, m_mla_norm_g, m_w_out, m_norm_ffn_g, m_ffn_w_up, m_ffn_conv_w, m_ffn_conv_b, m_ffn_w_down, m_final_norm_g, v_meta_tokens, v_norm_mix_g, v_w_in, v_ssd_conv_w, v_ssd_conv_b, v_ssd_dt_bias, v_ssd_a_log, v_ssd_d, v_ssd_norm_g, v_sb_norm_g, v_mla_q_norm_g, v_mla_kv_norm_g, v_mla_w_uq, v_mla_w_ukv, v_mla_norm_g, v_w_out, v_norm_ffn_g, v_ffn_w_up, v_ffn_conv_w, v_ffn_conv_b, v_ffn_w_down, v_final_norm_g)
    a = dict(zip(INPUT_NAMES, args, strict=True))
    full = gather_weights(a)
    loss, grad_x, grads = local_step(a["x"][0], a["loss_target"][0], full)
    loss = lax.psum(loss, ("x", "y", "c"))
    outs = reduce_and_update(a, grads)
    result = [loss, grad_x[None]]
    for kind in ("grad", "delta", "new_m", "new_v"):
        result += [outs[(kind, n)] for n in WEIGHT_NAMES]
    return tuple(result)
```

```python
import functools
import math

import numpy as np
import jax
import jax.numpy as jnp
from jax import lax
from jax.experimental import pallas as pl
from jax.experimental.pallas import tpu as pltpu

F32 = jnp.float32
BF16 = jnp.bfloat16
HIGHEST = lax.Precision.HIGHEST
MESH_ID = pl.DeviceIdType.MESH

D_MODEL = 1024
DEPTH = 2
N_META = 16
EPS = 1e-6
SSD_HEADS = 8
SSD_WIDTH = 512
SSD_XBC = 1024
SSD_CONV = 4
SB_WIDTH = 256
SB_SCALE = 64 ** -0.5
MLA_Q_RANK = 192
MLA_KV_RANK = 128
MLA_ROPE = 32
MLA_SCALE = 96 ** -0.5
ROPE_BASE = 10000.0
D_FF = 2816
FFN_CONV = 3
IN_COLS = 2664
N_CHIPS = 4

ADAM_LR = 0.001
ADAM_B1 = 0.9
ADAM_B2 = 0.999
ADAM_EPS = 1e-08
ADAM_WD = 0.01
ADAM_STEP = 10

LANES = 128
SUBLANES = 8
ROW_TILE = 256
VMEM_LIMIT = 56 * 1024 * 1024
PACK_W = 1024

U_XBC, U_Z, U_QA, U_CKV, U_KR4, U_DT, U_MAIN = 0, 1024, 1536, 1792, 1920, 2048, 2304
NEG = -1e30


def _cp(*sem):
    return pltpu.CompilerParams(dimension_semantics=sem if sem else None, vmem_limit_bytes=VMEM_LIMIT)


def _pick(dim, pref):
    if dim <= pref:
        return dim
    best = None
    for t in range(LANES, pref + 1, LANES):
        if dim % t == 0:
            best = t
    assert best is not None, (dim, pref)
    return best


def _dot(a, b, dims="nn", precision=None):
    dn = {"nn": (((1,), (0,)), ((), ())), "nt": (((1,), (1,)), ((), ())), "tn": (((0,), (0,)), ((), ()))}[dims]
    return lax.dot_general(a, b, dn, preferred_element_type=F32, precision=precision)


def _softplus(x):
    return jnp.maximum(x, 0.0) + jnp.log1p(jnp.exp(-jnp.abs(x)))


def _silu(x):
    return x * jax.nn.sigmoid(x)


def _rms(x, g, n=None):
    n = x.shape[-1] if n is None else n
    ms = jnp.sum(x * x, axis=-1, keepdims=True) * (1.0 / n)
    return x * lax.rsqrt(ms + EPS) * g


def mm(a, b, dims, out_dtype, name, add=None, tm=None, tn=None, tk=None):
    if dims == "nn":
        (m, k), (k2, n) = a.shape, b.shape
    elif dims == "nt":
        (m, k), (n, k2) = a.shape, b.shape
    else:
        (k, m), (k2, n) = a.shape, b.shape
    assert k == k2, (a.shape, b.shape, dims)
    if dims == "tn":
        tm, tn, tk = _pick(m, tm or 1408), _pick(n, tn or 1408), _pick(k, tk or 1408)
    else:
        tm, tn, tk = _pick(m, tm or 768), _pick(n, tn or 1408), _pick(k, tk or 2816)
    nk = k // tk
    if dims == "tn":
        a_spec = pl.BlockSpec((tk, tm), lambda j, i, kk: (kk, i))
    else:
        a_spec = pl.BlockSpec((tm, tk), lambda j, i, kk: (i, kk))
    if dims == "nt":
        b_spec = pl.BlockSpec((tn, tk), lambda j, i, kk: (j, kk))
    else:
        b_spec = pl.BlockSpec((tk, tn), lambda j, i, kk: (kk, j))
    o_spec = pl.BlockSpec((tm, tn), lambda j, i, kk: (i, j))
    has_add = add is not None

    def body(*refs):
        a_ref, b_ref = refs[0], refs[1]
        add_ref = refs[2] if has_add else None
        o_ref = refs[3] if has_add else refs[2]
        part = _dot(a_ref[...].astype(BF16), b_ref[...].astype(BF16), dims)

        def finish(r):
            if has_add:
                r = r + add_ref[...].astype(F32)
            o_ref[...] = r.astype(o_ref.dtype)

        if nk == 1:
            finish(part)
            return
        acc_ref = refs[-1]
        kk = pl.program_id(2)

        @pl.when(kk == 0)
        def _():
            acc_ref[...] = part

        @pl.when(jnp.logical_and(kk > 0, kk < nk - 1))
        def _():
            acc_ref[...] += part

        @pl.when(kk == nk - 1)
        def _():
            finish(acc_ref[...] + part)

    in_specs = [a_spec, b_spec] + ([o_spec] if has_add else [])
    args = (a, b) + ((add,) if has_add else ())
    return pl.pallas_call(
        body, name=name, grid=(n // tn, m // tm, nk),
        in_specs=in_specs, out_specs=o_spec,
        out_shape=jax.ShapeDtypeStruct((m, n), out_dtype),
        scratch_shapes=[pltpu.VMEM((tm, tn), F32)] if nk > 1 else [],
        compiler_params=_cp("parallel", "parallel", "arbitrary"),
    )(*args)


def RI(arr, width=None, cidx=0, cv=False, rblk=0):
    return ("row" if rblk == 0 else ("row", rblk), arr, arr.shape[1] if width is None else width, cidx, cv)


def HP(arr, width=None, cidx=0, cv=False):
    return ("prev", arr, arr.shape[1] if width is None else width, cidx, cv)


def HN(arr, width=None, cidx=0, cv=False):
    return ("next", arr, arr.shape[1] if width is None else width, cidx, cv)


def PA(arr, width=None, cidx=0, cv=False):
    return ("par", arr, arr.shape[1] if width is None else width, cidx, cv)


def RO(ncols, dtype, width=None, cv=False):
    return ("row", ncols, dtype, ncols if width is None else width, cv)


def AO(nrows, ncols, width=None, cv=False):
    return ("acc", (nrows, ncols), F32, ncols if width is None else width, cv)


def rowwise(fn, ins, outs, name, rows, tm=ROW_TILE, ncol=1):
    tm = min(tm, rows)
    assert rows % tm == 0
    nrow = rows // tm
    hb = tm // SUBLANES
    last_hb = rows // SUBLANES - 1
    in_specs, args = [], []
    for kind, arr, width, cidx, cv in ins:
        def cmap(j, cidx=cidx, cv=cv):
            return cidx + j if cv else cidx
        if kind == "row":
            spec = pl.BlockSpec((tm, width), lambda j, i, cmap=cmap: (i, cmap(j)))
        elif isinstance(kind, tuple):
            spec = pl.BlockSpec((tm, width), lambda j, i, cmap=cmap, rblk=kind[1]: (i + rblk, cmap(j)))
        elif kind == "prev":
            spec = pl.BlockSpec((SUBLANES, width), lambda j, i, cmap=cmap: (jnp.maximum(i * hb - 1, 0), cmap(j)))
        elif kind == "next":
            spec = pl.BlockSpec((SUBLANES, width), lambda j, i, cmap=cmap: (jnp.minimum((i + 1) * hb, last_hb), cmap(j)))
        else:
            spec = pl.BlockSpec((arr.shape[0], width), lambda j, i, cmap=cmap: (0, cmap(j)))
        in_specs.append(spec)
        args.append(arr)
    out_specs, out_shapes, acc_cv = [], [], []
    for kind, shp, dtype, width, cv in outs:
        if kind == "row":
            out_specs.append(pl.BlockSpec((tm, width), lambda j, i, cv=cv: (i, j if cv else 0)))
            out_shapes.append(jax.ShapeDtypeStruct((rows, shp), dtype))
            acc_cv.append(None)
        else:
            out_specs.append(pl.BlockSpec((shp[0], width), lambda j, i, cv=cv: (0, j if cv else 0)))
            out_shapes.append(jax.ShapeDtypeStruct(shp, dtype))
            acc_cv.append(cv)
    n_in = len(ins)

    def body(*refs):
        j = pl.program_id(0)
        i = pl.program_id(1)
        vals = fn(i, nrow, *[r[...] for r in refs[:n_in]])
        if not isinstance(vals, (tuple, list)):
            vals = (vals,)
        for o_ref, v, cv in zip(refs[n_in:], vals, acc_cv):
            if cv is None:
                o_ref[...] = v.astype(o_ref.dtype)
            else:
                first = (i == 0) if cv else jnp.logical_and(i == 0, j == 0)

                @pl.when(first)
                def _(o_ref=o_ref, v=v):
                    o_ref[...] = v.astype(o_ref.dtype)

                @pl.when(jnp.logical_not(first))
                def _(o_ref=o_ref, v=v):
                    o_ref[...] += v.astype(o_ref.dtype)

    res = pl.pallas_call(
        body, name=name, grid=(ncol, nrow), in_specs=in_specs, out_specs=out_specs, out_shape=out_shapes,
        compiler_params=_cp("arbitrary", "arbitrary"),
    )(*args)
    return res


def _rows_iota(x):
    return lax.broadcasted_iota(jnp.int32, x.shape, 0)


def shift_down(x, halo, s):
    if s == 0:
        return x
    tm = x.shape[0]
    top = pltpu.roll(halo, s, 0)
    if tm > SUBLANES:
        top = jnp.concatenate([top, jnp.zeros((tm - SUBLANES, x.shape[1]), x.dtype)], axis=0)
    return jnp.where(_rows_iota(x) < s, top, pltpu.roll(x, s, 0))


def shift_up(x, halo, s):
    if s == 0:
        return x
    tm = x.shape[0]
    bot = pltpu.roll(halo, SUBLANES - s, 0)
    if tm > SUBLANES:
        bot = jnp.concatenate([jnp.zeros((tm - SUBLANES, x.shape[1]), x.dtype), bot], axis=0)
    return jnp.where(_rows_iota(x) >= tm - s, bot, pltpu.roll(x, tm - s, 0))


def conv_fwd(x, halo, w, i):
    kw = w.shape[0]
    halo = jnp.where(i == 0, 0.0, halo)
    out = None
    for k in range(kw):
        term = w[k:k + 1, :] * shift_down(x, halo, kw - 1 - k)
        out = term if out is None else out + term
    return out


def conv_bwd_data(dy, halo_next, w, i, n):
    kw = w.shape[0]
    halo_next = jnp.where(i == n - 1, 0.0, halo_next)
    out = None
    for k in range(kw):
        term = w[k:k + 1, :] * shift_up(dy, halo_next, kw - 1 - k)
        out = term if out is None else out + term
    return out


def conv_bwd_w(dy, x, halo, i, kw):
    halo = jnp.where(i == 0, 0.0, halo)
    rows = [jnp.sum(dy * shift_down(x, halo, kw - 1 - k), axis=0, keepdims=True) for k in range(kw)]
    return jnp.concatenate(rows, axis=0)


def _lane(shape):
    return lax.broadcasted_iota(jnp.int32, shape, 1)


def rope_rot(x):
    lane = _lane(x.shape) % MLA_ROPE
    return jnp.where(lane < MLA_ROPE // 2, -pltpu.roll(x, LANES - MLA_ROPE // 2, 1), pltpu.roll(x, MLA_ROPE // 2, 1))


def rope_rot_t(g):
    lane = _lane(g.shape) % MLA_ROPE
    return jnp.where(lane < MLA_ROPE // 2, pltpu.roll(g, LANES - MLA_ROPE // 2, 1), -pltpu.roll(g, MLA_ROPE // 2, 1))


def _ssd_common(p, xs, dt_raw, bias, alog, q):
    lane = _lane((q, LANES))
    pre = dt_raw + bias
    dt = jnp.where(lane < SSD_HEADS, _softplus(pre), 0.0)
    a_row = -jnp.exp(alog)
    d_a = dt * a_row
    ri = lax.broadcasted_iota(jnp.int32, (q, q), 0)
    ci = lax.broadcasted_iota(jnp.int32, (q, q), 1)
    causal = ri >= ci
    acs = _dot(causal.astype(F32), d_a, "nn", HIGHEST)
    acs_t = acs.T
    subl = lax.broadcasted_iota(jnp.int32, (LANES, q), 0)
    h0, h1 = 2 * p, 2 * p + 1

    def col(arr, h):
        return jnp.sum(jnp.where(lane == h, arr, 0.0), axis=1, keepdims=True)

    def row(arr_t, h):
        return jnp.sum(jnp.where(subl == h, arr_t, 0.0), axis=0, keepdims=True)

    lo = lane < 64
    cols = (col(acs, h0), col(acs, h1))
    rows = (row(acs_t, h0), row(acs_t, h1))
    acs_p = jnp.where(lo, cols[0], cols[1])
    dt_p = jnp.where(lo, col(dt, h0), col(dt, h1))
    tots = (cols[0][q - 1:q, :], cols[1][q - 1:q, :])
    tot_p = jnp.where(lo[0:1, :], tots[0], tots[1])
    lms = tuple(jnp.exp(jnp.where(causal, cols[j] - rows[j], NEG)) for j in range(2))
    return dict(lane=lane, lo=lo, pre=pre, dt=dt, a_row=a_row, h=(h0, h1), acs_p=acs_p, dt_p=dt_p, tots=tots,
                tot_p=tot_p, lms=lms, ri=ri, ci=ci, eacs=jnp.exp(acs_p), dte=jnp.exp(tot_p - acs_p), x=xs * dt_p)


def _pick_lane(row_arr, h):
    return jnp.sum(jnp.where(_lane(row_arr.shape) == h, row_arr, 0.0), axis=1, keepdims=True)


def ssd_fwd(xbc_c, u_main, bias_row, alog_row, d_row, name):
    lp = xbc_c.shape[0]
    q = min(ROW_TILE, lp)
    nc = lp // q
    dt_blk = U_DT // LANES

    def body(xs_ref, b_ref, c_ref, dt_ref, bias_ref, alog_ref, d_ref, y_ref, hp_ref, h_scr):
        p = pl.program_id(0)
        c = pl.program_id(1)

        @pl.when(c == 0)
        def _():
            h_scr[...] = jnp.zeros_like(h_scr)

        xs = xs_ref[...]
        bb = b_ref[...].astype(BF16)
        cb_ = c_ref[...].astype(BF16)
        s = _ssd_common(p, xs, dt_ref[...], bias_ref[...], alog_ref[...], q)
        lo = s["lo"]
        g = _dot(cb_, bb, "nt")
        y = jnp.zeros((q, LANES), F32)
        for j in range(2):
            m = (g * s["lms"][j]).astype(BF16)
            xj = jnp.where(lo if j == 0 else jnp.logical_not(lo), s["x"], 0.0).astype(BF16)
            y = y + _dot(m, xj)
        hp = h_scr[...]
        hp_ref[...] = hp
        y = y + _dot(cb_, hp.astype(BF16), "nt") * s["eacs"]
        d_p = jnp.where(lo[0:1, :], _pick_lane(d_ref[...], s["h"][0]), _pick_lane(d_ref[...], s["h"][1]))
        y_ref[...] = y + d_p * xs
        sub_lo = lax.broadcasted_iota(jnp.int32, (LANES, LANES), 0) < 64
        etot = jnp.where(sub_lo, jnp.exp(s["tots"][0]), jnp.exp(s["tots"][1]))
        h_scr[...] = hp * etot + _dot((s["x"] * s["dte"]).astype(BF16), bb, "tn")

    grp = lambda p: p // 2
    in_specs = [
        pl.BlockSpec((q, LANES), lambda p, c: (c, p)),
        pl.BlockSpec((q, LANES), lambda p, c: (c, 4 + grp(p))),
        pl.BlockSpec((q, LANES), lambda p, c: (c, 6 + grp(p))),
        pl.BlockSpec((q, LANES), lambda p, c: (c, dt_blk)),
        pl.BlockSpec((1, LANES), lambda p, c: (0, 0)),
        pl.BlockSpec((1, LANES), lambda p, c: (0, 0)),
        pl.BlockSpec((1, LANES), lambda p, c: (0, 0)),
    ]
    out_specs = [
        pl.BlockSpec((q, LANES), lambda p, c: (c, p)),
        pl.BlockSpec((None, None, LANES, LANES), lambda p, c: (p, c, 0, 0)),
    ]
    return pl.pallas_call(
        body, name=name, grid=(4, nc), in_specs=in_specs, out_specs=out_specs,
        out_shape=[jax.ShapeDtypeStruct((lp, SSD_WIDTH), F32), jax.ShapeDtypeStruct((4, nc, LANES, LANES), F32)],
        scratch_shapes=[pltpu.VMEM((LANES, LANES), F32)],
        compiler_params=_cp("arbitrary", "arbitrary"),
    )(xbc_c, xbc_c, xbc_c, u_main, bias_row, alog_row, d_row)


def ssd_bwd(xbc_c, u_main, bias_row, alog_row, d_row, hprev, dy, name):
    lp = xbc_c.shape[0]
    q = min(ROW_TILE, lp)
    nc = lp // q
    dt_blk = U_DT // LANES

    def body(xs_ref, b_ref, c_ref, dt_ref, bias_ref, alog_ref, d_ref, hp_ref, dy_ref,
             dxs_ref, db_ref, dc_ref, ddt_ref, pg_ref, dh_scr):
        p = pl.program_id(0)
        cc = pl.program_id(1)

        @pl.when(cc == 0)
        def _():
            dh_scr[...] = jnp.zeros_like(dh_scr)
            pg_ref[...] = jnp.zeros_like(pg_ref)

        xs = xs_ref[...]
        bb = b_ref[...].astype(BF16)
        cb_ = c_ref[...].astype(BF16)
        s = _ssd_common(p, xs, dt_ref[...], bias_ref[...], alog_ref[...], q)
        lane, lo, x = s["lane"], s["lo"], s["x"]
        h0, h1 = s["h"]
        d_y = dy_ref[...]
        hp = hp_ref[...]
        hpb = hp.astype(BF16)
        dhn = dh_scr[...]
        dhnb = dhn.astype(BF16)
        xd = x * s["dte"]
        g = _dot(cb_, bb, "nt")
        dxdiag = jnp.zeros((q, LANES), F32)
        dg = jnp.zeros((q, q), F32)
        row_part, col_part = [], []
        for j in range(2):
            mj = lo if j == 0 else jnp.logical_not(lo)
            lm = s["lms"][j]
            m32 = g * lm
            xj = jnp.where(mj, x, 0.0).astype(BF16)
            dyj = jnp.where(mj, d_y, 0.0).astype(BF16)
            dxdiag = dxdiag + _dot(m32.astype(BF16), dyj, "tn")
            dm = _dot(dyj, xj, "nt")
            dg = dg + dm * lm
            wm = dm * m32
            row_part.append(jnp.sum(wm, axis=1, keepdims=True))
            col_part.append(jnp.sum(wm, axis=0, keepdims=True))
        dgb = dg.astype(BF16)
        d_c = _dot(dgb, bb)
        d_b = _dot(dgb, cb_, "tn")
        yoff = _dot(cb_, hpb, "nt") * s["eacs"]
        d_t = (d_y * s["eacs"]).astype(BF16)
        d_c = d_c + _dot(d_t, hpb)
        d_hp = _dot(d_t, cb_, "tn")
        dxd = _dot(bb, dhnb, "nt")
        d_b = d_b + _dot(xd.astype(BF16), dhnb)
        d_x = dxdiag + dxd * s["dte"]
        r = dxd * xd
        a_terms = d_y * yoff - r

        def gsum(arr):
            return (jnp.sum(jnp.where(lo, arr, 0.0), axis=1, keepdims=True),
                    jnp.sum(jnp.where(lo, 0.0, arr), axis=1, keepdims=True))

        dacs = gsum(a_terms)
        rs = gsum(r)
        hh = dhn * hp
        sub_lo = lax.broadcasted_iota(jnp.int32, (LANES, LANES), 0) < 64
        hsum = (jnp.sum(jnp.where(sub_lo, hh, 0.0), keepdims=True), jnp.sum(jnp.where(sub_lo, 0.0, hh), keepdims=True))
        last = lax.broadcasted_iota(jnp.int32, (q, 1), 0) == q - 1
        etots = (jnp.exp(s["tots"][0]), jnp.exp(s["tots"][1]))
        ddacs = jnp.zeros((q, LANES), F32)
        for j, h in enumerate((h0, h1)):
            dtot = jnp.sum(rs[j], keepdims=True) + hsum[j] * etots[j]
            dj = dacs[j] + row_part[j] + jnp.where(last, dtot, 0.0)
            ddacs = ddacs + jnp.where(lane == h, dj, 0.0)
        subl = lax.broadcasted_iota(jnp.int32, (LANES, q), 0)
        cols_t = jnp.where(subl == h0, col_part[0], 0.0) + jnp.where(subl == h1, col_part[1], 0.0)
        ddacs = ddacs - cols_t.T
        anti = (s["ri"] <= s["ci"]).astype(F32)
        da = _dot(anti, ddacs, "nn", HIGHEST)
        ddt_own = gsum(d_x * xs)
        ddt = jnp.where(lane == h0, ddt_own[0], 0.0) + jnp.where(lane == h1, ddt_own[1], 0.0) + da * s["a_row"]
        draw = ddt * jax.nn.sigmoid(s["pre"])
        ddt_ref[...] = draw
        d_p = jnp.where(lo[0:1, :], _pick_lane(d_ref[...], h0), _pick_lane(d_ref[...], h1))
        dxs_ref[...] = d_p * d_y + d_x * s["dt_p"]
        db_ref[...] = d_b
        dc_ref[...] = d_c
        dds = gsum(d_y * xs)
        lane1 = lane[0:1, :]
        dd_row = (jnp.where(lane1 == h0, jnp.sum(dds[0], keepdims=True), 0.0)
                  + jnp.where(lane1 == h1, jnp.sum(dds[1], keepdims=True), 0.0))
        dbias_row = jnp.sum(draw, axis=0, keepdims=True)
        dalog_row = jnp.sum(da * s["dt"], axis=0, keepdims=True) * s["a_row"]
        sub8 = lax.broadcasted_iota(jnp.int32, (SUBLANES, LANES), 0)
        pg_ref[...] += (jnp.where(sub8 == 0, dbias_row, 0.0) + jnp.where(sub8 == 1, dalog_row, 0.0)
                        + jnp.where(sub8 == 2, dd_row, 0.0))
        etot = jnp.where(sub_lo, etots[0], etots[1])
        dh_scr[...] = d_hp + etot * dhn

    grp = lambda p: p // 2
    rc = lambda c: nc - 1 - c
    in_specs = [
        pl.BlockSpec((q, LANES), lambda p, c: (rc(c), p)),
        pl.BlockSpec((q, LANES), lambda p, c: (rc(c), 4 + grp(p))),
        pl.BlockSpec((q, LANES), lambda p, c: (rc(c), 6 + grp(p))),
        pl.BlockSpec((q, LANES), lambda p, c: (rc(c), dt_blk)),
        pl.BlockSpec((1, LANES), lambda p, c: (0, 0)),
        pl.BlockSpec((1, LANES), lambda p, c: (0, 0)),
        pl.BlockSpec((1, LANES), lambda p, c: (0, 0)),
        pl.BlockSpec((None, None, LANES, LANES), lambda p, c: (p, rc(c), 0, 0)),
        pl.BlockSpec((q, LANES), lambda p, c: (rc(c), p)),
    ]
    out_specs = [
        pl.BlockSpec((q, LANES), lambda p, c: (rc(c), p)),
        pl.BlockSpec((q, LANES), lambda p, c: (rc(c), p)),
        pl.BlockSpec((q, LANES), lambda p, c: (rc(c), p)),
        pl.BlockSpec((q, LANES), lambda p, c: (rc(c), p)),
        pl.BlockSpec((SUBLANES, LANES), lambda p, c: (p, 0)),
    ]
    wide = jax.ShapeDtypeStruct((lp, 4 * LANES), F32)
    return pl.pallas_call(
        body, name=name, grid=(4, nc), in_specs=in_specs, out_specs=out_specs,
        out_shape=[wide, wide, wide, wide, jax.ShapeDtypeStruct((4 * SUBLANES, LANES), F32)],
        scratch_shapes=[pltpu.VMEM((LANES, LANES), F32)],
        compiler_params=_cp("arbitrary", "arbitrary"),
    )(xbc_c, xbc_c, xbc_c, u_main, bias_row, alog_row, d_row, hprev, dy)


def _sb_blocks(qs, ks, r_runs, masked, bq):
    ri = lax.broadcasted_iota(jnp.int32, (bq, bq), 0)
    ci = lax.broadcasted_iota(jnp.int32, (bq, bq), 1)
    tri_after = (ri > ci).astype(BF16)
    zs = [_dot(qj, kj, "nt") for qj, kj in zip(qs, ks)]
    us, sigs, ubs = [], [], []
    for z in zs:
        u = -(jnp.maximum(z, 0.0) + jnp.log(1.0 + jnp.exp(-jnp.abs(z))))
        sigs.append(jnp.exp(z + u))
        if masked:
            u = jnp.where(ci < ri, u, 0.0)
        us.append(u)
        ubs.append(u.astype(BF16))
    afters = [_dot(ub, tri_after) for ub in ubs]
    ws = []
    for sig, after, r_run in zip(sigs, afters, r_runs):
        w = sig * jnp.exp(after + r_run)
        if masked:
            w = jnp.where(ci < ri, w, 0.0)
        ws.append(w)
    return us, sigs, ws


def _split_heads(x, lo):
    out = []
    zero = jnp.zeros((x.shape[0], LANES), x.dtype)
    for p in range(2):
        xp = x[:, LANES * p:LANES * (p + 1)]
        out += [jnp.where(lo, xp, zero), jnp.where(lo, zero, xp)]
    return out


def _per_head(x):
    return [x[:, :LANES], x[:, :LANES], x[:, LANES:], x[:, LANES:]]


def _resident(shape, col):
    return pl.BlockSpec(shape, lambda i: (0, col), pipeline_mode=pl.Buffered(1))


def sb_attn_fwd(qkv, name):
    lp = qkv.shape[0]
    bq = min(ROW_TILE, lp)
    nq = lp // bq
    assert nq <= 64

    def body(q_ref, k_ref, v_ref, o_ref, rs_ref):
        qi = pl.program_id(0)
        lane = _lane((bq, LANES))
        lo = lane < 64
        qs = _split_heads(q_ref[...], lo)

        def step(kb, carry, masked):
            off = pl.multiple_of(kb * bq, bq)
            ks = _per_head(k_ref[pl.ds(off, bq), :])
            vs = _per_head(v_ref[pl.ds(off, bq), :])
            heads, rss = carry
            r_runs = [heads[h][1] for h in range(4)]
            rss = list(rss)
            for h in range(4):
                rss[h // 2] = jnp.where(lane == 64 * (h % 2) + kb, r_runs[h], rss[h // 2])
            us, _, ws = _sb_blocks(qs, ks, r_runs, masked, bq)
            pvs = [_dot(ws[h].astype(BF16), vs[h]) for h in range(4)]
            out = tuple((heads[h][0] + pvs[h], r_runs[h] + jnp.sum(us[h], axis=1, keepdims=True)) for h in range(4))
            return out, tuple(rss)

        zero = (jnp.zeros((bq, LANES), F32), jnp.zeros((bq, 1), F32))
        zr = jnp.zeros((bq, LANES), F32)
        carry = step(qi, ((zero,) * 4, (zr, zr)), True)
        heads, rss = lax.fori_loop(0, qi, lambda t, c: step(qi - 1 - t, c, False), carry)
        o_ref[...] = jnp.concatenate([jnp.where(lo, heads[0][0], heads[1][0]), jnp.where(lo, heads[2][0], heads[3][0])], axis=1)
        rs_ref[...] = jnp.concatenate(list(rss), axis=1)

    blk = pl.BlockSpec((bq, 2 * LANES), lambda i: (i, 0))
    return pl.pallas_call(
        body, name=name, grid=(nq,),
        in_specs=[blk, _resident((lp, 2 * LANES), 1), _resident((lp, 2 * LANES), 2)],
        out_specs=[blk, blk],
        out_shape=[jax.ShapeDtypeStruct((lp, SB_WIDTH), F32), jax.ShapeDtypeStruct((lp, SB_WIDTH), F32)],
        compiler_params=_cp("arbitrary"),
    )(qkv, qkv, qkv)


def sb_attn_bwd(qkv, rs, d_o, name):
    lp = qkv.shape[0]
    bq = min(ROW_TILE, lp)
    nq = lp // bq

    def body(q_ref, k_ref, v_ref, rs_ref, do_ref, dq_ref, dk_ref, dv_ref):
        qi = pl.program_id(0)

        @pl.when(qi == 0)
        def _():
            dk_ref[...] = jnp.zeros_like(dk_ref)
            dv_ref[...] = jnp.zeros_like(dv_ref)

        lane = _lane((bq, LANES))
        lo = lane < 64
        qs = _split_heads(q_ref[...], lo)
        dos = _split_heads(do_ref[...].astype(BF16), lo)
        rs_blk = rs_ref[...]
        ri = lax.broadcasted_iota(jnp.int32, (bq, bq), 0)
        ci = lax.broadcasted_iota(jnp.int32, (bq, bq), 1)
        tbefore = (ri < ci).astype(BF16)

        def step(kb, carry, masked):
            off = pl.multiple_of(kb * bq, bq)
            ks = _per_head(k_ref[pl.ds(off, bq), :])
            vs = _per_head(v_ref[pl.ds(off, bq), :])
            r_rights = [jnp.sum(jnp.where(lane == 64 * (h % 2) + kb, rs_blk[:, LANES * (h // 2):LANES * (h // 2 + 1)], 0.0),
                                axis=1, keepdims=True) for h in range(4)]
            dws = [_dot(dos[h], vs[h], "nt") for h in range(4)]
            _, sigs, ws = _sb_blocks(qs, ks, r_rights, masked, bq)
            gs = [ws[h] * dws[h] for h in range(4)]
            gbs = [g.astype(BF16) for g in gs]
            wbs = [w.astype(BF16) for w in ws]
            gbefores = [_dot(gb, tbefore) for gb in gbs]
            dv_acc = [_dot(wbs[2 * p], dos[2 * p], "tn") + _dot(wbs[2 * p + 1], dos[2 * p + 1], "tn") for p in range(2)]
            dzbs = []
            for h in range(4):
                dz = gs[h] - sigs[h] * (gs[h] + gbefores[h] + carry[h][1])
                if masked:
                    dz = jnp.where(ci < ri, dz, 0.0)
                dzbs.append(dz.astype(BF16))
            dqs = [_dot(dzbs[h], ks[h]) for h in range(4)]
            dk_acc = [_dot(dzbs[2 * p], qs[2 * p], "tn") + _dot(dzbs[2 * p + 1], qs[2 * p + 1], "tn") for p in range(2)]
            dk_ref[pl.ds(off, bq), :] += jnp.concatenate(dk_acc, axis=1)
            dv_ref[pl.ds(off, bq), :] += jnp.concatenate(dv_acc, axis=1)
            return tuple((carry[h][0] + dqs[h], carry[h][1] + jnp.sum(gs[h], axis=1, keepdims=True)) for h in range(4))

        zero = (jnp.zeros((bq, LANES), F32), jnp.zeros((bq, 1), F32))
        carry = lax.fori_loop(0, qi, lambda t, c: step(t, c, False), (zero,) * 4)
        carry = step(qi, carry, True)
        dq_ref[...] = jnp.concatenate([jnp.where(lo, carry[0][0], carry[1][0]), jnp.where(lo, carry[2][0], carry[3][0])],
                                      axis=1).astype(dq_ref.dtype)

    blk = pl.BlockSpec((bq, 2 * LANES), lambda i: (i, 0))
    return pl.pallas_call(
        body, name=name, grid=(nq,),
        in_specs=[blk, _resident((lp, 2 * LANES), 1), _resident((lp, 2 * LANES), 2), blk, blk],
        out_specs=[blk, _resident((lp, 2 * LANES), 0), _resident((lp, 2 * LANES), 0)],
        out_shape=[jax.ShapeDtypeStruct((lp, SB_WIDTH), BF16), jax.ShapeDtypeStruct((lp, SB_WIDTH), F32),
                   jax.ShapeDtypeStruct((lp, SB_WIDTH), F32)],
        compiler_params=_cp("arbitrary"),
    )(qkv, qkv, qkv, rs, d_o)


def _mla_masks(bq):
    lane = _lane((bq, 2 * LANES))
    out = []
    for h in range(4):
        j = h % 2
        nope = jnp.logical_and(lane >= 64 * j, lane < 64 * (j + 1))
        rope = jnp.logical_and(lane >= LANES + MLA_ROPE * h, lane < LANES + MLA_ROPE * (h + 1))
        out.append(jnp.logical_or(nope, rope))
    return out


def _mla_split_q(q, masks):
    zero = jnp.zeros((q.shape[0], 2 * LANES), q.dtype)
    return [jnp.where(masks[h], q[:, 2 * LANES * (h // 2):2 * LANES * (h // 2 + 1)], zero) for h in range(4)]


def _mla_per_head_k(k):
    return [k[:, :2 * LANES], k[:, :2 * LANES], k[:, 2 * LANES:], k[:, 2 * LANES:]]


def mla_attn_fwd(qc, kc, v, name):
    lp = qc.shape[0]
    bq = min(ROW_TILE, lp)
    nq = lp // bq

    def body(q_ref, k_ref, v_ref, o_ref, lse_ref):
        qi = pl.program_id(0)
        qs = _mla_split_q(q_ref[...], _mla_masks(bq))
        lo = _lane((bq, LANES)) < 64
        ri = lax.broadcasted_iota(jnp.int32, (bq, bq), 0)
        ci = lax.broadcasted_iota(jnp.int32, (bq, bq), 1)

        def step(kb, carry, masked):
            off = pl.multiple_of(kb * bq, bq)
            ks = _mla_per_head_k(k_ref[pl.ds(off, bq), :])
            vs = _per_head(v_ref[pl.ds(off, bq), :])
            ss = [_dot(qs[h], ks[h], "nt") for h in range(4)]
            prs, alphas, stats = [], [], []
            for h in range(4):
                _, m_run, l_run = carry[h]
                s = ss[h]
                if masked:
                    s = jnp.where(ci <= ri, s, NEG)
                m_new = jnp.maximum(m_run, jnp.max(s, axis=1, keepdims=True))
                alpha = jnp.exp(m_run - m_new)
                pr = jnp.exp(s - m_new)
                prs.append(pr.astype(BF16))
                alphas.append(alpha)
                stats.append((m_new, l_run * alpha + jnp.sum(pr, axis=1, keepdims=True)))
            pvs = [_dot(prs[h], vs[h]) for h in range(4)]
            return tuple((carry[h][0] * alphas[h] + pvs[h],) + stats[h] for h in range(4))

        zero = (jnp.zeros((bq, LANES), F32), jnp.full((bq, 1), NEG, F32), jnp.zeros((bq, 1), F32))
        carry = step(qi, (zero,) * 4, True)
        carry = lax.fori_loop(0, qi, lambda t, c: step(qi - 1 - t, c, False), carry)
        outs = [a / l for a, _, l in carry]
        lses = [m + jnp.log(l) for _, m, l in carry]
        o_ref[...] = jnp.concatenate([jnp.where(lo, outs[0], outs[1]), jnp.where(lo, outs[2], outs[3])], axis=1)
        lse_ref[...] = jnp.concatenate([jnp.where(lo, lses[0], lses[1]), jnp.where(lo, lses[2], lses[3])], axis=1)

    blk = pl.BlockSpec((bq, 2 * LANES), lambda i: (i, 0))
    return pl.pallas_call(
        body, name=name, grid=(nq,),
        in_specs=[pl.BlockSpec((bq, 4 * LANES), lambda i: (i, 0)), _resident((lp, 4 * LANES), 0), _resident((lp, 2 * LANES), 0)],
        out_specs=[blk, blk],
        out_shape=[jax.ShapeDtypeStruct((lp, 2 * LANES), F32), jax.ShapeDtypeStruct((lp, 2 * LANES), F32)],
        compiler_params=_cp("arbitrary"),
    )(qc, kc, v)


def mla_attn_bwd(qc, kc, v, o, lse, d_o, name):
    lp = qc.shape[0]
    bq = min(ROW_TILE, lp)
    nq = lp // bq

    def body(q_ref, k_ref, v_ref, o_ref, lse_ref, do_ref, dq_ref, dk_ref, dv_ref):
        qi = pl.program_id(0)

        @pl.when(qi == 0)
        def _():
            dk_ref[...] = jnp.zeros_like(dk_ref)
            dv_ref[...] = jnp.zeros_like(dv_ref)

        d_o = do_ref[...]
        masks = _mla_masks(bq)
        qs = _mla_split_q(q_ref[...], masks)
        lo = _lane((bq, LANES)) < 64
        dos = _split_heads(d_o.astype(BF16), lo)
        od = o_ref[...] * d_o
        lse_blk = lse_ref[...]
        delta, lses = [], []
        for h in range(4):
            odp = od[:, LANES * (h // 2):LANES * (h // 2 + 1)]
            delta.append(jnp.sum(jnp.where(lo, odp, 0.0) if h % 2 == 0 else jnp.where(lo, 0.0, odp), axis=1, keepdims=True))
            c0 = LANES * (h // 2) + 64 * (h % 2)
            lses.append(lse_blk[:, c0:c0 + 1])
        ri = lax.broadcasted_iota(jnp.int32, (bq, bq), 0)
        ci = lax.broadcasted_iota(jnp.int32, (bq, bq), 1)

        def step(kb, carry, masked):
            off = pl.multiple_of(kb * bq, bq)
            ks = _mla_per_head_k(k_ref[pl.ds(off, bq), :])
            vs = _per_head(v_ref[pl.ds(off, bq), :])
            ss = [_dot(qs[h], ks[h], "nt") for h in range(4)]
            dps = [_dot(dos[h], vs[h], "nt") for h in range(4)]
            prbs, dss = [], []
            for h in range(4):
                s = ss[h]
                if masked:
                    s = jnp.where(ci <= ri, s, NEG)
                pr = jnp.exp(s - lses[h])
                prbs.append(pr.astype(BF16))
                dss.append((pr * (dps[h] - delta[h])).astype(BF16))
            dv_acc = [_dot(prbs[2 * p], dos[2 * p], "tn") + _dot(prbs[2 * p + 1], dos[2 * p + 1], "tn") for p in range(2)]
            dqs = [_dot(dss[h], ks[h]) for h in range(4)]
            dk_acc = [_dot(dss[2 * p], qs[2 * p], "tn") + _dot(dss[2 * p + 1], qs[2 * p + 1], "tn") for p in range(2)]
            dk_ref[pl.ds(off, bq), :] += jnp.concatenate(dk_acc, axis=1)
            dv_ref[pl.ds(off, bq), :] += jnp.concatenate(dv_acc, axis=1)
            return tuple(carry[h] + dqs[h] for h in range(4))

        zero = jnp.zeros((bq, 2 * LANES), F32)
        carry = step(qi, (zero,) * 4, True)
        carry = lax.fori_loop(0, qi, lambda t, c: step(qi - 1 - t, c, False), carry)
        dq_ref[...] = jnp.concatenate([jnp.where(masks[0], carry[0], 0.0) + jnp.where(masks[1], carry[1], 0.0),
                                       jnp.where(masks[2], carry[2], 0.0) + jnp.where(masks[3], carry[3], 0.0)], axis=1)

    blk = pl.BlockSpec((bq, 2 * LANES), lambda i: (i, 0))
    wide = pl.BlockSpec((bq, 4 * LANES), lambda i: (i, 0))
    return pl.pallas_call(
        body, name=name, grid=(nq,),
        in_specs=[wide, _resident((lp, 4 * LANES), 0), _resident((lp, 2 * LANES), 0), blk, blk, blk],
        out_specs=[wide, _resident((lp, 4 * LANES), 0), _resident((lp, 2 * LANES), 0)],
        out_shape=[jax.ShapeDtypeStruct((lp, 4 * LANES), F32), jax.ShapeDtypeStruct((lp, 4 * LANES), F32),
                   jax.ShapeDtypeStruct((lp, 2 * LANES), F32)],
        compiler_params=_cp("arbitrary"),
    )(qc, kc, v, o, lse, d_o)


def _mix_out(y_pre, z, o_sb, o_mla, g_ssd, g_sb, g_mla):
    return jnp.concatenate([_rms(y_pre * _silu(z), g_ssd), _rms(o_sb, g_sb), _rms(o_mla, g_mla)], axis=1)


def _ffn_act(up_a, up_b, halo_a, halo_b, w_a, w_b, b_a, b_b, i):
    ca = conv_fwd(up_a, halo_a, w_a, i) + b_a
    cb_ = conv_fwd(up_b, halo_b, w_b, i) + b_b
    return ca, cb_


def layer_fwd(h, w, cs, sn, l):
    lp = h.shape[0]
    nm = f"l{l}_"
    hn = rowwise(lambda i, n, x, g: _rms(x, g), [RI(h), PA(w["norm_mix_g"])], [RO(D_MODEL, BF16)], nm + "rms_mix", lp)[0]
    u = mm(hn, w["w_main"], "nn", F32, nm + "in_main")
    qkv = mm(hn, w["w_sb"], "nn", BF16, nm + "in_sb")
    xbc_c = rowwise(lambda i, n, x, hl, cw, cb_: _silu(conv_fwd(x, hl, cw, i) + cb_),
                    [RI(u, SSD_XBC, 0), HP(u, SSD_XBC, 0), PA(w["ssd_conv_w"]), PA(w["ssd_conv_b"])],
                    [RO(SSD_XBC, F32)], nm + "ssd_conv", lp)[0]
    y_pre, hprev = ssd_fwd(xbc_c, u, w["dt_bias"], w["a_log"], w["d_skip"], nm + "ssd_fwd")
    o_sb, rs_sb = sb_attn_fwd(qkv, nm + "sb_fwd")
    qn, kvn = rowwise(lambda i, n, qa, ckv, gq, gkv: (_rms(qa, gq, MLA_Q_RANK), _rms(ckv, gkv)),
                      [RI(u, 256, U_QA // 256), RI(u, LANES, U_CKV // LANES), PA(w["q_norm_g"]), PA(w["kv_norm_g"])],
                      [RO(256, BF16), RO(LANES, BF16)], nm + "mla_rms", lp)
    qf = mm(qn, w["w_uq"], "nn", F32, nm + "mla_uq")
    kvf = mm(kvn, w["w_ukv"], "nn", F32, nm + "mla_ukv")

    def pack(i, n, qf_, kvf_, kr4, cos, sin):
        qf_ = qf_ * MLA_SCALE
        qr = qf_[:, 256:384]
        qr = qr * cos + rope_rot(qr) * sin
        kr = kr4 * cos + rope_rot(kr4) * sin
        qc = jnp.concatenate([qf_[:, 0:128], qr, qf_[:, 128:256], qr], axis=1)
        kc = jnp.concatenate([kvf_[:, 0:128], kr, kvf_[:, 128:256], kr], axis=1)
        return qc, kc, kvf_[:, 256:512]

    qc, kc, vv = rowwise(pack, [RI(qf), RI(kvf), RI(u, LANES, U_KR4 // LANES), RI(cs), RI(sn)],
                         [RO(512, BF16), RO(512, BF16), RO(256, BF16)], nm + "mla_pack", lp)
    o_mla, lse = mla_attn_fwd(qc, kc, vv, nm + "mla_fwd")
    cat = rowwise(lambda i, n, *a: _mix_out(*a),
                  [RI(y_pre), RI(u, SSD_WIDTH, U_Z // SSD_WIDTH), RI(o_sb), RI(o_mla),
                   PA(w["ssd_norm_g"]), PA(w["sb_norm_g"]), PA(w["mla_norm_g"])],
                  [RO(D_MODEL, BF16)], nm + "mix_out", lp)[0]
    h_mid = mm(cat, w["w_out"], "nn", F32, nm + "out_proj", add=h)
    hn2 = rowwise(lambda i, n, x, g: _rms(x, g), [RI(h_mid), PA(w["norm_ffn_g"])], [RO(D_MODEL, BF16)], nm + "rms_ffn", lp)[0]
    up_a = mm(hn2, w["w_up_a"], "nn", F32, nm + "up_a")
    up_b = mm(hn2, w["w_up_b"], "nn", F32, nm + "up_b")
    wc = 1408

    def act(i, n, ua, ub, ha, hb_, wa, wb, ba, bb_):
        ca, cb_ = _ffn_act(ua, ub, ha, hb_, wa, wb, ba, bb_, i)
        return _silu(ca) * cb_

    a_t = rowwise(act, [RI(up_a, wc, 0, True), RI(up_b, wc, 0, True), HP(up_a, wc, 0, True), HP(up_b, wc, 0, True),
                        PA(w["ffn_conv_w_a"], wc, 0, True), PA(w["ffn_conv_w_b"], wc, 0, True),
                        PA(w["ffn_conv_b_a"], wc, 0, True), PA(w["ffn_conv_b_b"], wc, 0, True)],
                  [RO(D_FF, BF16, wc, True)], nm + "ffn_act", lp, ncol=D_FF // wc)[0]
    h_out = mm(a_t, w["w_down"], "nn", F32, nm + "down", add=h_mid)
    saved = dict(h=h, hn=hn, u=u, qkv=qkv, xbc_c=xbc_c, y_pre=y_pre, hprev=hprev, o_sb=o_sb, rs_sb=rs_sb, qn=qn, kvn=kvn,
                 qc=qc, kc=kc, vv=vv, o_mla=o_mla, lse=lse, cat=cat, h_mid=h_mid, hn2=hn2, up_a=up_a, up_b=up_b, a_t=a_t)
    return h_out, saved


def layer_bwd(dh_out, w, s, cs, sn, l):
    lp = dh_out.shape[0]
    nm = f"l{l}b_"
    g = {}
    wc = 1408
    ncolf = D_FF // wc
    g["w_down"] = mm(s["a_t"], dh_out, "tn", BF16, nm + "dw_down")
    d_act = mm(dh_out, w["w_down"], "nt", F32, nm + "d_act")

    def act_bwd(i, n, ua, ub, ha, hb_, wa, wb, ba, bb_, da_):
        ca, cb_ = _ffn_act(ua, ub, ha, hb_, wa, wb, ba, bb_, i)
        sg = jax.nn.sigmoid(ca)
        dca = da_ * cb_ * (sg * (1.0 + ca * (1.0 - sg)))
        dcb = da_ * (ca * sg)
        return (dca, dcb, conv_bwd_w(dca, ua, ha, i, FFN_CONV), conv_bwd_w(dcb, ub, hb_, i, FFN_CONV),
                jnp.sum(dca, axis=0, keepdims=True), jnp.sum(dcb, axis=0, keepdims=True))

    dca, dcb, g["ffn_conv_w_a"], g["ffn_conv_w_b"], g["ffn_conv_b_a"], g["ffn_conv_b_b"] = rowwise(
        act_bwd, [RI(s["up_a"], wc, 0, True), RI(s["up_b"], wc, 0, True), HP(s["up_a"], wc, 0, True),
                  HP(s["up_b"], wc, 0, True), PA(w["ffn_conv_w_a"], wc, 0, True), PA(w["ffn_conv_w_b"], wc, 0, True),
                  PA(w["ffn_conv_b_a"], wc, 0, True), PA(w["ffn_conv_b_b"], wc, 0, True), RI(d_act, wc, 0, True)],
        [RO(D_FF, F32, wc, True), RO(D_FF, F32, wc, True), AO(FFN_CONV, D_FF, wc, True), AO(FFN_CONV, D_FF, wc, True),
         AO(1, D_FF, wc, True), AO(1, D_FF, wc, True)], nm + "ffn_act_bwd", lp, ncol=ncolf)

    def conv_t(i, n, da_, db_, ha, hb_, wa, wb):
        return conv_bwd_data(da_, ha, wa, i, n), conv_bwd_data(db_, hb_, wb, i, n)

    dup_a, dup_b = rowwise(conv_t, [RI(dca, wc, 0, True), RI(dcb, wc, 0, True), HN(dca, wc, 0, True), HN(dcb, wc, 0, True),
                                    PA(w["ffn_conv_w_a"], wc, 0, True), PA(w["ffn_conv_w_b"], wc, 0, True)],
                           [RO(D_FF, BF16, wc, True), RO(D_FF, BF16, wc, True)], nm + "ffn_conv_t", lp, ncol=ncolf)
    g["w_up_a"] = mm(s["hn2"], dup_a, "tn", BF16, nm + "dw_up_a")
    g["w_up_b"] = mm(s["hn2"], dup_b, "tn", BF16, nm + "dw_up_b")
    dhn2 = mm(dup_a, w["w_up_a"], "nt", F32, nm + "dhn2_a")
    dhn2 = mm(dup_b, w["w_up_b"], "nt", F32, nm + "dhn2_b", add=dhn2)

    def rms_bwd(i, n, x, gg, dy, dres):
        _, vjp = jax.vjp(_rms, x, gg)
        dx, dg = vjp(dy)
        return dres + dx, dg

    dh_mid, g["norm_ffn_g"] = rowwise(rms_bwd, [RI(s["h_mid"]), PA(w["norm_ffn_g"]), RI(dhn2), RI(dh_out)],
                                      [RO(D_MODEL, F32), AO(1, D_MODEL)], nm + "rms_ffn_bwd", lp)
    g["w_out"] = mm(s["cat"], dh_mid, "tn", BF16, nm + "dw_out")
    d_cat = mm(dh_mid, w["w_out"], "nt", F32, nm + "d_cat")
    u = s["u"]

    def mix_bwd(i, n, y_pre, z, o_sb, o_mla, g1, g2, g3, dcat):
        _, vjp = jax.vjp(_mix_out, y_pre, z, o_sb, o_mla, g1, g2, g3)
        return vjp(dcat)

    dy_pre, dz, do_sb, do_mla, g["ssd_norm_g"], g["sb_norm_g"], g["mla_norm_g"] = rowwise(
        mix_bwd, [RI(s["y_pre"]), RI(u, SSD_WIDTH, U_Z // SSD_WIDTH), RI(s["o_sb"]), RI(s["o_mla"]),
                  PA(w["ssd_norm_g"]), PA(w["sb_norm_g"]), PA(w["mla_norm_g"]), RI(d_cat)],
        [RO(SSD_WIDTH, F32), RO(SSD_WIDTH, BF16), RO(SB_WIDTH, F32), RO(256, F32),
         AO(1, SSD_WIDTH), AO(1, SB_WIDTH), AO(1, 256)], nm + "mix_out_bwd", lp)
    dxs, dbp, dcp, ddtp, pg = ssd_bwd(s["xbc_c"], u, w["dt_bias"], w["a_log"], w["d_skip"], s["hprev"], dy_pre, nm + "ssd_bwd")
    pg = pg.reshape(4, SUBLANES, LANES).sum(axis=0)
    g["dt_bias"], g["a_log"], g["d_skip"] = pg[0:1], pg[1:2], pg[2:3]

    def conv4_bwd(i, n, x, hl, cw, cb_, dxs_, dbp_, dcp_, ddtp_):
        pre = conv_fwd(x, hl, cw, i) + cb_
        d_b = jnp.concatenate([dbp_[:, 0:128] + dbp_[:, 128:256], dbp_[:, 256:384] + dbp_[:, 384:512]], axis=1)
        d_c = jnp.concatenate([dcp_[:, 0:128] + dcp_[:, 128:256], dcp_[:, 256:384] + dcp_[:, 384:512]], axis=1)
        d_out = jnp.concatenate([dxs_, d_b, d_c], axis=1)
        sg = jax.nn.sigmoid(pre)
        d_pre = d_out * (sg * (1.0 + pre * (1.0 - sg)))
        ddt = ddtp_[:, 0:128] + ddtp_[:, 128:256] + ddtp_[:, 256:384] + ddtp_[:, 384:512]
        return d_pre, ddt, conv_bwd_w(d_pre, x, hl, i, SSD_CONV), jnp.sum(d_pre, axis=0, keepdims=True)

    d_pre, ddt, g["ssd_conv_w"], g["ssd_conv_b"] = rowwise(
        conv4_bwd, [RI(u, SSD_XBC, 0), HP(u, SSD_XBC, 0), PA(w["ssd_conv_w"]), PA(w["ssd_conv_b"]),
                    RI(dxs), RI(dbp), RI(dcp), RI(ddtp)],
        [RO(SSD_XBC, F32), RO(LANES, BF16), AO(SSD_CONV, SSD_XBC), AO(1, SSD_XBC)], nm + "ssd_conv_bwd", lp)
    d_xbc = rowwise(lambda i, n, d, hn_, cw: conv_bwd_data(d, hn_, cw, i, n),
                    [RI(d_pre), HN(d_pre), PA(w["ssd_conv_w"])], [RO(SSD_XBC, BF16)], nm + "ssd_conv_t", lp)[0]
    dq_sb, dk_sb, dv_sb = sb_attn_bwd(s["qkv"], s["rs_sb"], do_sb, nm + "sb_bwd")
    dqkv = jnp.concatenate([dq_sb, dk_sb.astype(BF16), dv_sb.astype(BF16)], axis=1)
    dqc, dkc, dvv = mla_attn_bwd(s["qc"], s["kc"], s["vv"], s["o_mla"], s["lse"], do_mla, nm + "mla_bwd")

    def unpack(i, n, dqc_, dkc_, dvv_, cos, sin):
        dqr = dqc_[:, 128:256] + dqc_[:, 384:512]
        dqr = dqr * cos + rope_rot_t(dqr * sin)
        dkr = dkc_[:, 128:256] + dkc_[:, 384:512]
        dkr = dkr * cos + rope_rot_t(dkr * sin)
        dq = jnp.concatenate([dqc_[:, 0:128], dqc_[:, 256:384], dqr], axis=1) * MLA_SCALE
        dkv = jnp.concatenate([dkc_[:, 0:128], dkc_[:, 256:384], dvv_], axis=1)
        return dq, dkv, dkr

    dq, dkv, dkr4 = rowwise(unpack, [RI(dqc), RI(dkc), RI(dvv), RI(cs), RI(sn)],
                            [RO(384, BF16), RO(512, BF16), RO(LANES, BF16)], nm + "mla_unpack", lp)
    g["w_uq"] = mm(s["qn"], dq, "tn", F32, nm + "dw_uq")
    g["w_ukv"] = mm(s["kvn"], dkv, "tn", F32, nm + "dw_ukv")
    dqn = mm(dq, w["w_uq"], "nt", F32, nm + "dqn")
    dkvn = mm(dkv, w["w_ukv"], "nt", F32, nm + "dkvn")

    def mla_rms_bwd(i, n, qa, ckv, gq, gkv, dqn_, dkvn_):
        _, vjp = jax.vjp(lambda a, b, c, d: (_rms(a, c, MLA_Q_RANK), _rms(b, d)), qa, ckv, gq, gkv)
        return vjp((dqn_, dkvn_))

    dqa, dckv, g["q_norm_g"], g["kv_norm_g"] = rowwise(
        mla_rms_bwd, [RI(u, 256, U_QA // 256), RI(u, LANES, U_CKV // LANES), PA(w["q_norm_g"]), PA(w["kv_norm_g"]),
                      RI(dqn), RI(dkvn)],
        [RO(256, BF16), RO(LANES, BF16), AO(1, 256), AO(1, LANES)], nm + "mla_rms_bwd", lp)
    du = jnp.concatenate([d_xbc, dz, dqa, dckv, dkr4, ddt, jnp.zeros((lp, LANES), BF16)], axis=1)
    g["w_main"] = mm(s["hn"], du, "tn", F32, nm + "dw_main")
    g["w_sb"] = mm(s["hn"], dqkv, "tn", F32, nm + "dw_sb")
    dhn = mm(du, w["w_main"], "nt", F32, nm + "dhn_main")
    dhn = mm(dqkv, w["w_sb"], "nt", F32, nm + "dhn_sb", add=dhn)
    dh_in, g["norm_mix_g"] = rowwise(rms_bwd, [RI(s["h"]), PA(w["norm_mix_g"]), RI(dhn), RI(dh_mid)],
                                     [RO(D_MODEL, F32), AO(1, D_MODEL)], nm + "rms_mix_bwd", lp)
    return dh_in, g


_IN_CUTS = np.cumsum((512, 1024, 8, 256, 256, 256, 192, 128, 32))


def _pad_cols(a, n):
    return jnp.pad(a, ((0, 0), (0, n - a.shape[1])))


def prep_layer_weights(full, l):
    w_in = full["w_in"][l]
    c = _IN_CUTS
    z, xbc, dtr = w_in[:, :c[0]], w_in[:, c[0]:c[1]], w_in[:, c[1]:c[2]]
    q_sb, k_sb, v_sb = w_in[:, c[2]:c[3]], w_in[:, c[3]:c[4]], w_in[:, c[4]:c[5]]
    q_a, c_kv, k_r = w_in[:, c[5]:c[6]], w_in[:, c[6]:c[7]], w_in[:, c[7]:c[8]]
    w_main = jnp.concatenate([xbc, z, _pad_cols(q_a, 256), c_kv, k_r, k_r, k_r, k_r, _pad_cols(dtr, 256)], axis=1)
    assert w_main.shape[1] == U_MAIN
    row = lambda v, n=None: _pad_cols(v.reshape(1, -1).astype(F32), v.size if n is None else n)
    uq = full["mla_w_uq"][l].reshape(MLA_Q_RANK, 4, 96)
    w_uq = jnp.concatenate([uq[:, :, :64].reshape(MLA_Q_RANK, 256), uq[:, :, 64:].reshape(MLA_Q_RANK, 128)], axis=1)
    w_uq = jnp.pad(w_uq, ((0, 256 - MLA_Q_RANK), (0, 0)))
    ukv = full["mla_w_ukv"][l].reshape(MLA_KV_RANK, 4, 128)
    w_ukv = jnp.concatenate([ukv[:, :, :64].reshape(MLA_KV_RANK, 256), ukv[:, :, 64:].reshape(MLA_KV_RANK, 256)], axis=1)
    return dict(
        norm_mix_g=row(full["norm_mix_g"][l]), w_main=w_main, w_sb=jnp.concatenate([q_sb * SB_SCALE, k_sb, v_sb], axis=1),
        ssd_conv_w=full["ssd_conv_w"][l], ssd_conv_b=row(full["ssd_conv_b"][l]),
        dt_bias=row(full["ssd_dt_bias"][l], LANES), a_log=row(full["ssd_a_log"][l], LANES), d_skip=row(full["ssd_d"][l], LANES),
        ssd_norm_g=row(full["ssd_norm_g"][l]), sb_norm_g=row(full["sb_norm_g"][l]),
        q_norm_g=row(full["mla_q_norm_g"][l], 256), kv_norm_g=row(full["mla_kv_norm_g"][l]),
        w_uq=w_uq, w_ukv=w_ukv, mla_norm_g=row(full["mla_norm_g"][l]),
        w_out=full["w_out"][l], norm_ffn_g=row(full["norm_ffn_g"][l]),
        w_up_a=full["ffn_w_up"][l][:, :D_FF], w_up_b=full["ffn_w_up"][l][:, D_FF:],
        ffn_conv_w_a=full["ffn_conv_w"][l][:, :D_FF], ffn_conv_w_b=full["ffn_conv_w"][l][:, D_FF:],
        ffn_conv_b_a=row(full["ffn_conv_b"][l][:D_FF]), ffn_conv_b_b=row(full["ffn_conv_b"][l][D_FF:]),
        w_down=full["ffn_w_down"][l],
    )


def unprep_layer_grads(g):
    wm = g["w_main"]
    xbc, z = wm[:, U_XBC:U_XBC + 1024], wm[:, U_Z:U_Z + 512]
    q_a, c_kv = wm[:, U_QA:U_QA + MLA_Q_RANK], wm[:, U_CKV:U_CKV + 128]
    k_r = (wm[:, U_KR4:U_KR4 + 32] + wm[:, U_KR4 + 32:U_KR4 + 64] + wm[:, U_KR4 + 64:U_KR4 + 96] + wm[:, U_KR4 + 96:U_KR4 + 128])
    dtr = wm[:, U_DT:U_DT + SSD_HEADS]
    w_sb = g["w_sb"]
    w_in = jnp.concatenate([z, xbc, dtr, w_sb[:, :SB_WIDTH] * SB_SCALE, w_sb[:, SB_WIDTH:], q_a, c_kv, k_r], axis=1)
    guq = g["w_uq"][:MLA_Q_RANK]
    guq = jnp.concatenate([guq[:, :256].reshape(MLA_Q_RANK, 4, 64), guq[:, 256:].reshape(MLA_Q_RANK, 4, 32)], axis=2)
    gukv = g["w_ukv"]
    gukv = jnp.concatenate([gukv[:, :256].reshape(MLA_KV_RANK, 4, 64), gukv[:, 256:].reshape(MLA_KV_RANK, 4, 64)], axis=2)
    return dict(
        norm_mix_g=g["norm_mix_g"][0], w_in=w_in, ssd_conv_w=g["ssd_conv_w"], ssd_conv_b=g["ssd_conv_b"][0],
        ssd_dt_bias=g["dt_bias"][0, :SSD_HEADS], ssd_a_log=g["a_log"][0, :SSD_HEADS], ssd_d=g["d_skip"][0, :SSD_HEADS],
        ssd_norm_g=g["ssd_norm_g"][0], sb_norm_g=g["sb_norm_g"][0], mla_q_norm_g=g["q_norm_g"][0, :MLA_Q_RANK],
        mla_kv_norm_g=g["kv_norm_g"][0], mla_w_uq=guq.reshape(MLA_Q_RANK, 384), mla_w_ukv=gukv.reshape(MLA_KV_RANK, 512),
        mla_norm_g=g["mla_norm_g"][0], w_out=g["w_out"], norm_ffn_g=g["norm_ffn_g"][0],
        ffn_w_up=jnp.concatenate([g["w_up_a"], g["w_up_b"]], axis=1),
        ffn_conv_w=jnp.concatenate([g["ffn_conv_w_a"], g["ffn_conv_w_b"]], axis=1),
        ffn_conv_b=jnp.concatenate([g["ffn_conv_b_a"][0], g["ffn_conv_b_b"][0]], axis=0),
        ffn_w_down=g["w_down"],
    )


def rope_tables(lp):
    pos = jnp.arange(lp, dtype=F32)
    inv = 1.0 / (ROPE_BASE ** (jnp.arange(0, MLA_ROPE, 2, dtype=F32) / MLA_ROPE))
    ang = pos[:, None] * inv[None, :]
    ang = jnp.concatenate([ang, ang] * 4, axis=-1)
    return jnp.cos(ang), jnp.sin(ang)


def local_step(x_seq, target, full):
    seq = x_seq.shape[0]
    length = seq + N_META
    lp = -(-length // ROW_TILE) * ROW_TILE
    cs, sn = rope_tables(lp)
    h = jnp.concatenate([full["meta_tokens"].astype(F32), x_seq, jnp.zeros((lp - length, D_MODEL), F32)], axis=0)
    tgt = jnp.pad(target, ((N_META, lp - length), (0, 0)))
    ws = [prep_layer_weights(full, l) for l in range(DEPTH)]
    saved = []
    for l in range(DEPTH):
        h, s = layer_fwd(h, ws[l], cs, sn, l)
        saved.append(s)
    fg = full["final_norm_g"].reshape(1, D_MODEL).astype(F32)
    tm = min(ROW_TILE, lp)

    def loss_fn(i, n, x, g, t):
        rows = _rows_iota(x) + i * tm
        valid = jnp.logical_and(rows >= N_META, rows < length)

        def f(x_, g_):
            err = jnp.where(valid, _rms(x_, g_) - t, 0.0)
            return 0.5 * jnp.sum(err * err) * (1.0 / D_MODEL)

        val, (dx, dg) = jax.value_and_grad(f, argnums=(0, 1))(x, g)
        return dx, jnp.full((1, LANES), val, F32), dg

    dh, loss_row, g_final = rowwise(loss_fn, [RI(h), PA(fg), RI(tgt)], [RO(D_MODEL, F32), AO(1, LANES), AO(1, D_MODEL)],
                                    "loss_head", lp)
    grads = {}
    per_layer = [None] * DEPTH
    for l in reversed(range(DEPTH)):
        dh, g = layer_bwd(dh, ws[l], saved[l], cs, sn, l)
        per_layer[l] = unprep_layer_grads(g)
    for k in per_layer[0]:
        grads[k] = jnp.stack([per_layer[l][k] for l in range(DEPTH)], axis=0)
    grads["final_norm_g"] = g_final[0]
    grads["meta_tokens"] = dh[:N_META]
    return loss_row[0, 0], dh[N_META:length], grads


_ANY = pl.BlockSpec(memory_space=pl.ANY)


def chip_exchange(srcs, modes, name):
    n = len(srcs)
    flips = ((1, 0), (0, 1), (1, 1))

    def body(*refs):
        ins, outs = refs[:n], refs[n:2 * n]
        send_sems, recv_sems, fwd_send_sems, fwd_recv_sems, loc_sems = refs[2 * n:]
        x, y, c = lax.axis_index("x"), lax.axis_index("y"), lax.axis_index("c")
        me = 2 * x + y
        waits, forwards = [], []
        for a in range(n):
            whole = modes[a] != "slab"
            cp = pltpu.make_async_copy(ins[a] if whole else ins[a].at[me], outs[a].at[me], loc_sems.at[a])
            cp.start()
            waits.append(cp.wait)
            half = ins[a].shape[0] // 2 if modes[a] == "bcast_split" else None
            for k, (fx, fy) in enumerate(flips):
                px = 1 - x if fx else x
                py = 1 - y if fy else y
                peer = 2 * px + py
                if half is None:
                    src = ins[a] if whole else ins[a].at[peer]
                    dst = outs[a].at[me]
                else:
                    src = ins[a].at[pl.ds(c * half, half)]
                    dst = outs[a].at[me, pl.ds(c * half, half)]
                rc = pltpu.make_async_remote_copy(src_ref=src, dst_ref=dst, send_sem=send_sems.at[a, k],
                                                  recv_sem=recv_sems.at[a, k], device_id=(px, py, c), device_id_type=MESH_ID)
                rc.start()
                if half is None:
                    waits.append(rc.wait)
                else:
                    waits.append(rc.wait_send)
                    landed = outs[a].at[peer, pl.ds(c * half, half)]
                    fw = pltpu.make_async_remote_copy(src_ref=landed, dst_ref=landed, send_sem=fwd_send_sems.at[a, k],
                                                      recv_sem=fwd_recv_sems.at[a, k], device_id=(x, y, 1 - c),
                                                      device_id_type=MESH_ID)
                    forwards.append((rc, fw))
        for rc, fw in forwards:
            rc.wait_recv()
            fw.start()
        for rc, fw in forwards:
            fw.wait()
        for w in waits:
            w()

    out_shape = [jax.ShapeDtypeStruct((N_CHIPS,) + (s.shape if m != "slab" else s.shape[1:]), s.dtype) for s, m in zip(srcs, modes)]
    return pl.pallas_call(
        body, name=name, in_specs=[_ANY] * n, out_specs=[_ANY] * n, out_shape=out_shape,
        scratch_shapes=[pltpu.SemaphoreType.DMA((n, 3)), pltpu.SemaphoreType.DMA((n, 3)), pltpu.SemaphoreType.DMA((n, 3)),
                        pltpu.SemaphoreType.DMA((n, 3)), pltpu.SemaphoreType.DMA((n,))],
    )(*srcs)


def _piece(ref, mode, k):
    if mode == "slab":
        return ref.at[k]
    if mode == "rows":
        rs = ref.shape[1] // N_CHIPS
        return ref.at[:, pl.ds(pl.multiple_of(k * rs, 16), rs), :]
    if mode == "cols":
        cs = ref.shape[2] // N_CHIPS
        return ref.at[:, :, pl.ds(pl.multiple_of(k * cs, LANES), cs)]
    return ref


def _piece_shape(shape, mode):
    if mode == "slab":
        return shape[1:]
    if mode == "rows":
        return (shape[0], shape[1] // N_CHIPS, shape[2])
    if mode == "cols":
        return (shape[0], shape[1], shape[2] // N_CHIPS)
    return shape


def grad_exchange(srcs, modes, name):
    n = len(srcs)
    flips = ((1, 0), (0, 1), (1, 1))

    def body(*refs):
        ins, outs = refs[:n], refs[n:2 * n]
        send_sems, recv_sems, fwd_send_sems, fwd_recv_sems, sib_send_sems, sib_recv_sems, loc_sems = refs[2 * n:]
        x, y, c = lax.axis_index("x"), lax.axis_index("y"), lax.axis_index("c")
        me = 2 * x + y
        sibling = (x, y, 1 - c)
        waits, forwards = [], []
        for a in range(n):
            mine = _piece(ins[a], modes[a], me)
            slot = outs[a].at[4 * c + me]
            cp = pltpu.make_async_copy(mine, slot, loc_sems.at[a])
            cp.start()
            sb = pltpu.make_async_remote_copy(src_ref=mine, dst_ref=slot, send_sem=sib_send_sems.at[a],
                                              recv_sem=sib_recv_sems.at[a], device_id=sibling, device_id_type=MESH_ID)
            sb.start()
            waits += [cp.wait, sb.wait]
            for k, (fx, fy) in enumerate(flips):
                px = 1 - x if fx else x
                py = 1 - y if fy else y
                peer = 2 * px + py
                rc = pltpu.make_async_remote_copy(src_ref=_piece(ins[a], modes[a], peer), dst_ref=slot,
                                                  send_sem=send_sems.at[a, k], recv_sem=recv_sems.at[a, k],
                                                  device_id=(px, py, c), device_id_type=MESH_ID)
                rc.start()
                landed = outs[a].at[4 * c + peer]
                fw = pltpu.make_async_remote_copy(src_ref=landed, dst_ref=landed, send_sem=fwd_send_sems.at[a, k],
                                                  recv_sem=fwd_recv_sems.at[a, k], device_id=sibling, device_id_type=MESH_ID)
                waits.append(rc.wait_send)
                forwards.append((rc, fw))
        for rc, fw in forwards:
            rc.wait_recv()
            fw.start()
        for rc, fw in forwards:
            fw.wait()
        for w in waits:
            w()

    out_shape = [jax.ShapeDtypeStruct((2 * N_CHIPS,) + tuple(_piece_shape(s.shape, m)), s.dtype) for s, m in zip(srcs, modes)]
    dma = pltpu.SemaphoreType.DMA
    return pl.pallas_call(
        body, name=name, in_specs=[_ANY] * n, out_specs=[_ANY] * n, out_shape=out_shape,
        scratch_shapes=[dma((n, 3)), dma((n, 3)), dma((n, 3)), dma((n, 3)), dma((n,)), dma((n,)), dma((n,))],
    )(*srcs)


WEIGHT_NAMES = ("meta_tokens", "norm_mix_g", "w_in", "ssd_conv_w", "ssd_conv_b", "ssd_dt_bias", "ssd_a_log", "ssd_d",
                "ssd_norm_g", "sb_norm_g", "mla_q_norm_g", "mla_kv_norm_g", "mla_w_uq", "mla_w_ukv", "mla_norm_g",
                "w_out", "norm_ffn_g", "ffn_w_up", "ffn_conv_w", "ffn_conv_b", "ffn_w_down", "final_norm_g")
SHARD_AXIS = {"meta_tokens": 1, "w_in": 2, "ssd_conv_w": 2, "mla_w_uq": 2, "mla_w_ukv": 2, "w_out": 1, "ffn_w_up": 2,
              "ffn_conv_w": 2, "ffn_w_down": 1}
SHARDED = tuple(n for n in WEIGHT_NAMES if n in SHARD_AXIS)
REPLICATED = tuple(n for n in WEIGHT_NAMES if n not in SHARD_AXIS)
GATHER_BF16 = ("w_in", "mla_w_uq", "mla_w_ukv", "w_out", "ffn_w_up", "ffn_w_down")
GATHER_F32 = ("meta_tokens", "ssd_conv_w", "ffn_conv_w")
PACK_ROWS = ROW_TILE


def pack(arrs, dtype):
    flat = jnp.concatenate([a.reshape(-1).astype(dtype) for a in arrs])
    per = PACK_ROWS * PACK_W
    total = -(-flat.size // per) * per
    return jnp.pad(flat, (0, total - flat.size)).reshape(total // PACK_W, PACK_W)


def unpack(buf, shapes):
    flat = buf.reshape(-1)
    out, off = [], 0
    for shp in shapes:
        size = int(np.prod(shp))
        out.append(flat[off:off + size].reshape(shp))
        off += size
    return out


def gather_weights(a):
    full = {n: a[n] for n in REPLICATED}
    bufs = [pack([a[n] for n in GATHER_BF16], BF16), pack([a[n] for n in GATHER_F32], F32)]
    got = chip_exchange(bufs, ("bcast_split", "bcast"), "gather_weights")
    for names, g in ((GATHER_BF16, got[0]), (GATHER_F32, got[1])):
        pieces = [unpack(g[k], [a[n].shape for n in names]) for k in range(N_CHIPS)]
        for idx, n in enumerate(names):
            full[n] = jnp.concatenate([pieces[k][idx] for k in range(N_CHIPS)], axis=SHARD_AXIS[n])
    return full


BIG = ("w_in", "w_out", "ffn_w_up", "ffn_w_down")
BIG_MODE = {"w_in": "slab", "w_out": "rows", "ffn_w_up": "cols", "ffn_w_down": "rows"}
SMALL_SHARDED = tuple(n for n in SHARDED if n not in BIG)
ADAM_TILE = 128


def _adamw(i, n, *vals):
    parts, (w, m, v) = vals[:2 * N_CHIPS], vals[2 * N_CHIPS:]
    g = parts[0].astype(F32)
    for p in parts[1:]:
        g = g + p.astype(F32)
    m = ADAM_B1 * m + (1.0 - ADAM_B1) * g
    v = ADAM_B2 * v + (1.0 - ADAM_B2) * jnp.square(g)
    m_hat = m / (1.0 - ADAM_B1 ** ADAM_STEP)
    v_hat = v / (1.0 - ADAM_B2 ** ADAM_STEP)
    delta = -ADAM_LR * (m_hat / (jnp.sqrt(v_hat) + ADAM_EPS) + ADAM_WD * w)
    return g, delta, m, v


def _adamw_call(got, w, m, v, name):
    rows, width = w.shape
    flat = got.reshape(2 * N_CHIPS * rows, width)
    blk = rows // ADAM_TILE
    ins = [RI(flat, rblk=k * blk) for k in range(2 * N_CHIPS)] + [RI(w), RI(m), RI(v)]
    return rowwise(_adamw, ins, [RO(width, F32)] * 4, name, rows, tm=ADAM_TILE)


def reduce_and_update(a, grads):
    srcs, modes = [], []
    for n in BIG:
        g = grads[n].astype(BF16)
        if n == "w_in":
            cs = a[n].shape[2]
            g = g.reshape(DEPTH, D_MODEL, N_CHIPS, cs).transpose(2, 0, 1, 3)
        srcs.append(g)
        modes.append(BIG_MODE[n])
    slabs = []
    for k in range(N_CHIPS):
        parts = []
        for n in SMALL_SHARDED:
            ax = SHARD_AXIS[n]
            size = a[n].shape[ax]
            parts.append(lax.slice_in_dim(grads[n], k * size, (k + 1) * size, axis=ax))
        slabs.append(pack(parts, BF16))
    srcs += [jnp.stack(slabs, axis=0), pack([grads[n] for n in REPLICATED], F32)]
    modes += ["slab", "bcast"]
    got = grad_exchange(srcs, modes, "exchange_grads")
    outs = {}
    kinds = ("grad", "delta", "new_m", "new_v")
    for n, g8 in zip(BIG, got):
        shp = a[n].shape
        rows = shp[0] * shp[1]
        flat = lambda t: t.reshape(rows, shp[2])
        res = _adamw_call(g8.reshape(2 * N_CHIPS, rows, shp[2]), flat(a[n]), flat(a["m_" + n]), flat(a["v_" + n]), "adamw_" + n)
        for kind, val in zip(kinds, res):
            outs[(kind, n)] = val.reshape(shp)
    for tag, names, g8 in (("small", SMALL_SHARDED, got[len(BIG)]), ("rep", REPLICATED, got[len(BIG) + 1])):
        shapes = [a[n].shape for n in names]
        packed = [pack([a[pre + n] for n in names], F32) for pre in ("", "m_", "v_")]
        res = _adamw_call(g8, *packed, "adamw_" + tag)
        for kind, buf in zip(kinds, res):
            for n, val in zip(names, unpack(buf, shapes)):
                outs[(kind, n)] = val
    return outs


INPUT_NAMES = ("x",) + WEIGHT_NAMES + ("loss_target",) + tuple("m_" + n for n in WEIGHT_NAMES) + tuple("v_" + n for n in WEIGHT_NAMES)


def kernel(x, meta_tokens, norm_mix_g, w_in, ssd_conv_w, ssd_conv_b, ssd_dt_bias, ssd_a_log, ssd_d, ssd_norm_g, sb_norm_g, mla_q_norm_g, mla_kv_norm_g, mla_w_uq, mla_w_ukv, mla_norm_g, w_out, norm_ffn_g, ffn_w_up, ffn_conv_w, ffn_conv_b, ffn_w_down, final_norm_g, loss_target, m_meta_tokens, m_norm_mix_g, m_w_in, m_ssd_conv_w, m_ssd_conv_b, m_ssd_dt_bias, m_ssd_a_log, m_ssd_d, m_ssd_norm_g, m_sb_norm_g, m_mla_q_norm_g, m_mla_kv_norm_g, m_mla_w_uq, m_mla_w_ukv, m_mla_norm_g, m_w_out, m_norm_ffn_g, m_ffn_w_up, m_ffn_conv_w, m_ffn_conv_b, m_ffn_w_down, m_final_norm_g, v_meta_tokens, v_norm_mix_g, v_w_in, v_ssd_conv_w, v_ssd_conv_b, v_ssd_dt_bias, v_ssd_a_log, v_ssd_d, v_ssd_norm_g, v_sb_norm_g, v_mla_q_norm_g, v_mla_kv_norm_g, v_mla_w_uq, v_mla_w_ukv, v_mla_norm_g, v_w_out, v_norm_ffn_g, v_ffn_w_up, v_ffn_conv_w, v_ffn_conv_b, v_ffn_w_down, v_final_norm_g):
    args = (x, meta_tokens, norm_mix_g, w_in, ssd_conv_w, ssd_conv_b, ssd_dt_bias, ssd_a_log, ssd_d, ssd_norm_g, sb_norm_g, mla_q_norm_g, mla_kv_norm_g, mla_w_uq, mla_w_ukv, mla_norm_g, w_out, norm_ffn_g, ffn_w_up, ffn_conv_w, ffn_conv_b, ffn_w_down, final_norm_g, loss_target, m_meta_tokens, m_norm_mix_g, m_w_in, m_ssd_conv_w, m_ssd_conv_b, m_ssd_dt_bias, m_ssd_a_log, m_ssd_d, m_ssd_norm_g, m_sb_norm_g, m_mla_q_norm_g, m_mla_kv_norm_g, m_mla_w_uq, m_mla_w_ukv, m_mla_norm_g, m_w_out, m_norm_ffn_g, m_ffn_w_up, m_ffn_conv_w, m_ffn_conv_b, m_ffn_w_down, m_final_norm_g, v_meta_tokens, v_norm_mix_g, v_w_in, v_ssd_conv_w, v_ssd_conv_b, v_ssd_dt_bias, v_ssd_a_log, v_ssd_d, v_ssd_norm_g, v_sb_norm_g, v_mla_q_norm_g, v_mla_kv_norm_g, v_mla_w_uq, v_mla_w_ukv, v_mla_norm_g, v_w_out, v_norm_ffn_g, v_ffn_w_up, v_ffn_conv_w, v_ffn_conv_b, v_ffn_w_down, v_final_norm_g)
    a = dict(zip(INPUT_NAMES, args, strict=True))
    full = gather_weights(a)
    loss, grad_x, grads = local_step(a["x"][0], a["loss_target"][0], full)
    loss = lax.psum(loss, ("x", "y", "c"))
    outs = reduce_and_update(a, grads)
    result = [loss, grad_x[None]]
    for kind in ("grad", "delta", "new_m", "new_v"):
        result += [outs[(kind, n)] for n in WEIGHT_NAMES]
    return tuple(result)
```

```python
import functools
import math

import numpy as np
import jax
import jax.numpy as jnp
from jax import lax
from jax.experimental import pallas as pl
from jax.experimental.pallas import tpu as pltpu

F32 = jnp.float32
BF16 = jnp.bfloat16
HIGHEST = lax.Precision.HIGHEST
MESH_ID = pl.DeviceIdType.MESH

D_MODEL = 1024
DEPTH = 2
N_META = 16
EPS = 1e-6
SSD_HEADS = 8
SSD_WIDTH = 512
SSD_XBC = 1024
SSD_CONV = 4
SB_WIDTH = 256
SB_SCALE = 64 ** -0.5
MLA_Q_RANK = 192
MLA_KV_RANK = 128
MLA_ROPE = 32
MLA_SCALE = 96 ** -0.5
ROPE_BASE = 10000.0
D_FF = 2816
FFN_CONV = 3
IN_COLS = 2664
N_CHIPS = 4

ADAM_LR = 0.001
ADAM_B1 = 0.9
ADAM_B2 = 0.999
ADAM_EPS = 1e-08
ADAM_WD = 0.01
ADAM_STEP = 10

LANES = 128
SUBLANES = 8
ROW_TILE = 256
VMEM_LIMIT = 56 * 1024 * 1024
PACK_W = 1024

U_XBC, U_Z, U_QA, U_CKV, U_KR4, U_DT, U_MAIN = 0, 1024, 1536, 1792, 1920, 2048, 2304
NEG = -1e30


def _cp(*sem):
    return pltpu.CompilerParams(dimension_semantics=sem if sem else None, vmem_limit_bytes=VMEM_LIMIT)


def _pick(dim, pref):
    if dim <= pref:
        return dim
    best = None
    for t in range(LANES, pref + 1, LANES):
        if dim % t == 0:
            best = t
    assert best is not None, (dim, pref)
    return best


def _dot(a, b, dims="nn", precision=None):
    dn = {"nn": (((1,), (0,)), ((), ())), "nt": (((1,), (1,)), ((), ())), "tn": (((0,), (0,)), ((), ()))}[dims]
    return lax.dot_general(a, b, dn, preferred_element_type=F32, precision=precision)


def _softplus(x):
    return jnp.maximum(x, 0.0) + jnp.log1p(jnp.exp(-jnp.abs(x)))


def _silu(x):
    return x * jax.nn.sigmoid(x)


def _rms(x, g, n=None):
    n = x.shape[-1] if n is None else n
    ms = jnp.sum(x * x, axis=-1, keepdims=True) * (1.0 / n)
    return x * lax.rsqrt(ms + EPS) * g


def mm(a, b, dims, out_dtype, name, add=None, tm=None, tn=None, tk=None):
    if dims == "nn":
        (m, k), (k2, n) = a.shape, b.shape
    elif dims == "nt":
        (m, k), (n, k2) = a.shape, b.shape
    else:
        (k, m), (k2, n) = a.shape, b.shape
    assert k == k2, (a.shape, b.shape, dims)
    if dims == "tn":
        tm, tn, tk = _pick(m, tm or 1408), _pick(n, tn or 1408), _pick(k, tk or 1408)
    else:
        tm, tn, tk = _pick(m, tm or 768), _pick(n, tn or 1408), _pick(k, tk or 2816)
    nk = k // tk
    if dims == "tn":
        a_spec = pl.BlockSpec((tk, tm), lambda j, i, kk: (kk, i))
    else:
        a_spec = pl.BlockSpec((tm, tk), lambda j, i, kk: (i, kk))
    if dims == "nt":
        b_spec = pl.BlockSpec((tn, tk), lambda j, i, kk: (j, kk))
    else:
        b_spec = pl.BlockSpec((tk, tn), lambda j, i, kk: (kk, j))
    o_spec = pl.BlockSpec((tm, tn), lambda j, i, kk: (i, j))
    has_add = add is not None

    def body(*refs):
        a_ref, b_ref = refs[0], refs[1]
        add_ref = refs[2] if has_add else None
        o_ref = refs[3] if has_add else refs[2]
        part = _dot(a_ref[...].astype(BF16), b_ref[...].astype(BF16), dims)

        def finish(r):
            if has_add:
                r = r + add_ref[...].astype(F32)
            o_ref[...] = r.astype(o_ref.dtype)

        if nk == 1:
            finish(part)
            return
        acc_ref = refs[-1]
        kk = pl.program_id(2)

        @pl.when(kk == 0)
        def _():
            acc_ref[...] = part

        @pl.when(jnp.logical_and(kk > 0, kk < nk - 1))
        def _():
            acc_ref[...] += part

        @pl.when(kk == nk - 1)
        def _():
            finish(acc_ref[...] + part)

    in_specs = [a_spec, b_spec] + ([o_spec] if has_add else [])
    args = (a, b) + ((add,) if has_add else ())
    return pl.pallas_call(
        body, name=name, grid=(n // tn, m // tm, nk),
        in_specs=in_specs, out_specs=o_spec,
        out_shape=jax.ShapeDtypeStruct((m, n), out_dtype),
        scratch_shapes=[pltpu.VMEM((tm, tn), F32)] if nk > 1 else [],
        compiler_params=_cp("parallel", "parallel", "arbitrary"),
    )(*args)


def RI(arr, width=None, cidx=0, cv=False, rblk=0):
    return ("row" if rblk == 0 else ("row", rblk), arr, arr.shape[1] if width is None else width, cidx, cv)


def HP(arr, width=None, cidx=0, cv=False):
    return ("prev", arr, arr.shape[1] if width is None else width, cidx, cv)


def HN(arr, width=None, cidx=0, cv=False):
    return ("next", arr, arr.shape[1] if width is None else width, cidx, cv)


def PA(arr, width=None, cidx=0, cv=False):
    return ("par", arr, arr.shape[1] if width is None else width, cidx, cv)


def RO(ncols, dtype, width=None, cv=False):
    return ("row", ncols, dtype, ncols if width is None else width, cv)


def AO(nrows, ncols, width=None, cv=False):
    return ("acc", (nrows, ncols), F32, ncols if width is None else width, cv)


def rowwise(fn, ins, outs, name, rows, tm=ROW_TILE, ncol=1):
    tm = min(tm, rows)
    assert rows % tm == 0
    nrow = rows // tm
    hb = tm // SUBLANES
    last_hb = rows // SUBLANES - 1
    in_specs, args = [], []
    for kind, arr, width, cidx, cv in ins:
        def cmap(j, cidx=cidx, cv=cv):
            return cidx + j if cv else cidx
        if kind == "row":
            spec = pl.BlockSpec((tm, width), lambda j, i, cmap=cmap: (i, cmap(j)))
        elif isinstance(kind, tuple):
            spec = pl.BlockSpec((tm, width), lambda j, i, cmap=cmap, rblk=kind[1]: (i + rblk, cmap(j)))
        elif kind == "prev":
            spec = pl.BlockSpec((SUBLANES, width), lambda j, i, cmap=cmap: (jnp.maximum(i * hb - 1, 0), cmap(j)))
        elif kind == "next":
            spec = pl.BlockSpec((SUBLANES, width), lambda j, i, cmap=cmap: (jnp.minimum((i + 1) * hb, last_hb), cmap(j)))
        else:
            spec = pl.BlockSpec((arr.shape[0], width), lambda j, i, cmap=cmap: (0, cmap(j)))
        in_specs.append(spec)
        args.append(arr)
    out_specs, out_shapes, acc_cv = [], [], []
    for kind, shp, dtype, width, cv in outs:
        if kind == "row":
            out_specs.append(pl.BlockSpec((tm, width), lambda j, i, cv=cv: (i, j if cv else 0)))
            out_shapes.append(jax.ShapeDtypeStruct((rows, shp), dtype))
            acc_cv.append(None)
        else:
            out_specs.append(pl.BlockSpec((shp[0], width), lambda j, i, cv=cv: (0, j if cv else 0)))
            out_shapes.append(jax.ShapeDtypeStruct(shp, dtype))
            acc_cv.append(cv)
    n_in = len(ins)

    def body(*refs):
        j = pl.program_id(0)
        i = pl.program_id(1)
        vals = fn(i, nrow, *[r[...] for r in refs[:n_in]])
        if not isinstance(vals, (tuple, list)):
            vals = (vals,)
        for o_ref, v, cv in zip(refs[n_in:], vals, acc_cv):
            if cv is None:
                o_ref[...] = v.astype(o_ref.dtype)
            else:
                first = (i == 0) if cv else jnp.logical_and(i == 0, j == 0)

                @pl.when(first)
                def _(o_ref=o_ref, v=v):
                    o_ref[...] = v.astype(o_ref.dtype)

                @pl.when(jnp.logical_not(first))
                def _(o_ref=o_ref, v=v):
                    o_ref[...] += v.astype(o_ref.dtype)

    res = pl.pallas_call(
        body, name=name, grid=(ncol, nrow), in_specs=in_specs, out_specs=out_specs, out_shape=out_shapes,
        compiler_params=_cp("arbitrary", "arbitrary"),
    )(*args)
    return res


def _rows_iota(x):
    return lax.broadcasted_iota(jnp.int32, x.shape, 0)


def shift_down(x, halo, s):
    if s == 0:
        return x
    tm = x.shape[0]
    top = pltpu.roll(halo, s, 0)
    if tm > SUBLANES:
        top = jnp.concatenate([top, jnp.zeros((tm - SUBLANES, x.shape[1]), x.dtype)], axis=0)
    return jnp.where(_rows_iota(x) < s, top, pltpu.roll(x, s, 0))


def shift_up(x, halo, s):
    if s == 0:
        return x
    tm = x.shape[0]
    bot = pltpu.roll(halo, SUBLANES - s, 0)
    if tm > SUBLANES:
        bot = jnp.concatenate([jnp.zeros((tm - SUBLANES, x.shape[1]), x.dtype), bot], axis=0)
    return jnp.where(_rows_iota(x) >= tm - s, bot, pltpu.roll(x, tm - s, 0))


def conv_fwd(x, halo, w, i):
    kw = w.shape[0]
    halo = jnp.where(i == 0, 0.0, halo)
    out = None
    for k in range(kw):
        term = w[k:k + 1, :] * shift_down(x, halo, kw - 1 - k)
        out = term if out is None else out + term
    return out


def conv_bwd_data(dy, halo_next, w, i, n):
    kw = w.shape[0]
    halo_next = jnp.where(i == n - 1, 0.0, halo_next)
    out = None
    for k in range(kw):
        term = w[k:k + 1, :] * shift_up(dy, halo_next, kw - 1 - k)
        out = term if out is None else out + term
    return out


def conv_bwd_w(dy, x, halo, i, kw):
    halo = jnp.where(i == 0, 0.0, halo)
    rows = [jnp.sum(dy * shift_down(x, halo, kw - 1 - k), axis=0, keepdims=True) for k in range(kw)]
    return jnp.concatenate(rows, axis=0)


def _lane(shape):
    return lax.broadcasted_iota(jnp.int32, shape, 1)


def rope_rot(x):
    lane = _lane(x.shape) % MLA_ROPE
    return jnp.where(lane < MLA_ROPE // 2, -pltpu.roll(x, LANES - MLA_ROPE // 2, 1), pltpu.roll(x, MLA_ROPE // 2, 1))


def rope_rot_t(g):
    lane = _lane(g.shape) % MLA_ROPE
    return jnp.where(lane < MLA_ROPE // 2, pltpu.roll(g, LANES - MLA_ROPE // 2, 1), -pltpu.roll(g, MLA_ROPE // 2, 1))


def _ssd_common(p, xs, dt_raw, bias, alog, q):
    lane = _lane((q, LANES))
    pre = dt_raw + bias
    dt = jnp.where(lane < SSD_HEADS, _softplus(pre), 0.0)
    a_row = -jnp.exp(alog)
    d_a = dt * a_row
    ri = lax.broadcasted_iota(jnp.int32, (q, q), 0)
    ci = lax.broadcasted_iota(jnp.int32, (q, q), 1)
    causal = ri >= ci
    acs = _dot(causal.astype(F32), d_a, "nn", HIGHEST)
    acs_t = acs.T
    subl = lax.broadcasted_iota(jnp.int32, (LANES, q), 0)
    h0, h1 = 2 * p, 2 * p + 1

    def col(arr, h):
        return jnp.sum(jnp.where(lane == h, arr, 0.0), axis=1, keepdims=True)

    def row(arr_t, h):
        return jnp.sum(jnp.where(subl == h, arr_t, 0.0), axis=0, keepdims=True)

    lo = lane < 64
    cols = (col(acs, h0), col(acs, h1))
    rows = (row(acs_t, h0), row(acs_t, h1))
    acs_p = jnp.where(lo, cols[0], cols[1])
    dt_p = jnp.where(lo, col(dt, h0), col(dt, h1))
    tots = (cols[0][q - 1:q, :], cols[1][q - 1:q, :])
    tot_p = jnp.where(lo[0:1, :], tots[0], tots[1])
    lms = tuple(jnp.exp(jnp.where(causal, cols[j] - rows[j], NEG)) for j in range(2))
    return dict(lane=lane, lo=lo, pre=pre, dt=dt, a_row=a_row, h=(h0, h1), acs_p=acs_p, dt_p=dt_p, tots=tots,
                tot_p=tot_p, lms=lms, ri=ri, ci=ci, eacs=jnp.exp(acs_p), dte=jnp.exp(tot_p - acs_p), x=xs * dt_p)


def _pick_lane(row_arr, h):
    return jnp.sum(jnp.where(_lane(row_arr.shape) == h, row_arr, 0.0), axis=1, keepdims=True)


def ssd_fwd(xbc_c, u_main, bias_row, alog_row, d_row, name):
    lp = xbc_c.shape[0]
    q = min(ROW_TILE, lp)
    nc = lp // q
    dt_blk = U_DT // LANES

    def body(xs_ref, b_ref, c_ref, dt_ref, bias_ref, alog_ref, d_ref, y_ref, hp_ref, h_scr):
        p = pl.program_id(0)
        c = pl.program_id(1)

        @pl.when(c == 0)
        def _():
            h_scr[...] = jnp.zeros_like(h_scr)

        xs = xs_ref[...]
        bb = b_ref[...].astype(BF16)
        cb_ = c_ref[...].astype(BF16)
        s = _ssd_common(p, xs, dt_ref[...], bias_ref[...], alog_ref[...], q)
        lo = s["lo"]
        g = _dot(cb_, bb, "nt")
        y = jnp.zeros((q, LANES), F32)
        for j in range(2):
            m = (g * s["lms"][j]).astype(BF16)
            xj = jnp.where(lo if j == 0 else jnp.logical_not(lo), s["x"], 0.0).astype(BF16)
            y = y + _dot(m, xj)
        hp = h_scr[...]
        hp_ref[...] = hp
        y = y + _dot(cb_, hp.astype(BF16), "nt") * s["eacs"]
        d_p = jnp.where(lo[0:1, :], _pick_lane(d_ref[...], s["h"][0]), _pick_lane(d_ref[...], s["h"][1]))
        y_ref[...] = y + d_p * xs
        sub_lo = lax.broadcasted_iota(jnp.int32, (LANES, LANES), 0) < 64
        etot = jnp.where(sub_lo, jnp.exp(s["tots"][0]), jnp.exp(s["tots"][1]))
        h_scr[...] = hp * etot + _dot((s["x"] * s["dte"]).astype(BF16), bb, "tn")

    grp = lambda p: p // 2
    in_specs = [
        pl.BlockSpec((q, LANES), lambda p, c: (c, p)),
        pl.BlockSpec((q, LANES), lambda p, c: (c, 4 + grp(p))),
        pl.BlockSpec((q, LANES), lambda p, c: (c, 6 + grp(p))),
        pl.BlockSpec((q, LANES), lambda p, c: (c, dt_blk)),
        pl.BlockSpec((1, LANES), lambda p, c: (0, 0)),
        pl.BlockSpec((1, LANES), lambda p, c: (0, 0)),
        pl.BlockSpec((1, LANES), lambda p, c: (0, 0)),
    ]
    out_specs = [
        pl.BlockSpec((q, LANES), lambda p, c: (c, p)),
        pl.BlockSpec((None, None, LANES, LANES), lambda p, c: (p, c, 0, 0)),
    ]
    return pl.pallas_call(
        body, name=name, grid=(4, nc), in_specs=in_specs, out_specs=out_specs,
        out_shape=[jax.ShapeDtypeStruct((lp, SSD_WIDTH), F32), jax.ShapeDtypeStruct((4, nc, LANES, LANES), F32)],
        scratch_shapes=[pltpu.VMEM((LANES, LANES), F32)],
        compiler_params=_cp("arbitrary", "arbitrary"),
    )(xbc_c, xbc_c, xbc_c, u_main, bias_row, alog_row, d_row)


def ssd_bwd(xbc_c, u_main, bias_row, alog_row, d_row, hprev, dy, name):
    lp = xbc_c.shape[0]
    q = min(ROW_TILE, lp)
    nc = lp // q
    dt_blk = U_DT // LANES

    def body(xs_ref, b_ref, c_ref, dt_ref, bias_ref, alog_ref, d_ref, hp_ref, dy_ref,
             dxs_ref, db_ref, dc_ref, ddt_ref, pg_ref, dh_scr):
        p = pl.program_id(0)
        cc = pl.program_id(1)

        @pl.when(cc == 0)
        def _():
            dh_scr[...] = jnp.zeros_like(dh_scr)
            pg_ref[...] = jnp.zeros_like(pg_ref)

        xs = xs_ref[...]
        bb = b_ref[...].astype(BF16)
        cb_ = c_ref[...].astype(BF16)
        s = _ssd_common(p, xs, dt_ref[...], bias_ref[...], alog_ref[...], q)
        lane, lo, x = s["lane"], s["lo"], s["x"]
        h0, h1 = s["h"]
        d_y = dy_ref[...]
        hp = hp_ref[...]
        hpb = hp.astype(BF16)
        dhn = dh_scr[...]
        dhnb = dhn.astype(BF16)
        xd = x * s["dte"]
        g = _dot(cb_, bb, "nt")
        dxdiag = jnp.zeros((q, LANES), F32)
        dg = jnp.zeros((q, q), F32)
        row_part, col_part = [], []
        for j in range(2):
            mj = lo if j == 0 else jnp.logical_not(lo)
            lm = s["lms"][j]
            m32 = g * lm
            xj = jnp.where(mj, x, 0.0).astype(BF16)
            dyj = jnp.where(mj, d_y, 0.0).astype(BF16)
            dxdiag = dxdiag + _dot(m32.astype(BF16), dyj, "tn")
            dm = _dot(dyj, xj, "nt")
            dg = dg + dm * lm
            wm = dm * m32
            row_part.append(jnp.sum(wm, axis=1, keepdims=True))
            col_part.append(jnp.sum(wm, axis=0, keepdims=True))
        dgb = dg.astype(BF16)
        d_c = _dot(dgb, bb)
        d_b = _dot(dgb, cb_, "tn")
        yoff = _dot(cb_, hpb, "nt") * s["eacs"]
        d_t = (d_y * s["eacs"]).astype(BF16)
        d_c = d_c + _dot(d_t, hpb)
        d_hp = _dot(d_t, cb_, "tn")
        dxd = _dot(bb, dhnb, "nt")
        d_b = d_b + _dot(xd.astype(BF16), dhnb)
        d_x = dxdiag + dxd * s["dte"]
        r = dxd * xd
        a_terms = d_y * yoff - r

        def gsum(arr):
            return (jnp.sum(jnp.where(lo, arr, 0.0), axis=1, keepdims=True),
                    jnp.sum(jnp.where(lo, 0.0, arr), axis=1, keepdims=True))

        dacs = gsum(a_terms)
        rs = gsum(r)
        hh = dhn * hp
        sub_lo = lax.broadcasted_iota(jnp.int32, (LANES, LANES), 0) < 64
        hsum = (jnp.sum(jnp.where(sub_lo, hh, 0.0), keepdims=True), jnp.sum(jnp.where(sub_lo, 0.0, hh), keepdims=True))
        last = lax.broadcasted_iota(jnp.int32, (q, 1), 0) == q - 1
        etots = (jnp.exp(s["tots"][0]), jnp.exp(s["tots"][1]))
        ddacs = jnp.zeros((q, LANES), F32)
        for j, h in enumerate((h0, h1)):
            dtot = jnp.sum(rs[j], keepdims=True) + hsum[j] * etots[j]
            dj = dacs[j] + row_part[j] + jnp.where(last, dtot, 0.0)
            ddacs = ddacs + jnp.where(lane == h, dj, 0.0)
        subl = lax.broadcasted_iota(jnp.int32, (LANES, q), 0)
        cols_t = jnp.where(subl == h0, col_part[0], 0.0) + jnp.where(subl == h1, col_part[1], 0.0)
        ddacs = ddacs - cols_t.T
        anti = (s["ri"] <= s["ci"]).astype(F32)
        da = _dot(anti, ddacs, "nn", HIGHEST)
        ddt_own = gsum(d_x * xs)
        ddt = jnp.where(lane == h0, ddt_own[0], 0.0) + jnp.where(lane == h1, ddt_own[1], 0.0) + da * s["a_row"]
        draw = ddt * jax.nn.sigmoid(s["pre"])
        ddt_ref[...] = draw
        d_p = jnp.where(lo[0:1, :], _pick_lane(d_ref[...], h0), _pick_lane(d_ref[...], h1))
        dxs_ref[...] = d_p * d_y + d_x * s["dt_p"]
        db_ref[...] = d_b
        dc_ref[...] = d_c
        dds = gsum(d_y * xs)
        lane1 = lane[0:1, :]
        dd_row = (jnp.where(lane1 == h0, jnp.sum(dds[0], keepdims=True), 0.0)
                  + jnp.where(lane1 == h1, jnp.sum(dds[1], keepdims=True), 0.0))
        dbias_row = jnp.sum(draw, axis=0, keepdims=True)
        dalog_row = jnp.sum(da * s["dt"], axis=0, keepdims=True) * s["a_row"]
        sub8 = lax.broadcasted_iota(jnp.int32, (SUBLANES, LANES), 0)
        pg_ref[...] += (jnp.where(sub8 == 0, dbias_row, 0.0) + jnp.where(sub8 == 1, dalog_row, 0.0)
                        + jnp.where(sub8 == 2, dd_row, 0.0))
        etot = jnp.where(sub_lo, etots[0], etots[1])
        dh_scr[...] = d_hp + etot * dhn

    grp = lambda p: p // 2
    rc = lambda c: nc - 1 - c
    in_specs = [
        pl.BlockSpec((q, LANES), lambda p, c: (rc(c), p)),
        pl.BlockSpec((q, LANES), lambda p, c: (rc(c), 4 + grp(p))),
        pl.BlockSpec((q, LANES), lambda p, c: (rc(c), 6 + grp(p))),
        pl.BlockSpec((q, LANES), lambda p, c: (rc(c), dt_blk)),
        pl.BlockSpec((1, LANES), lambda p, c: (0, 0)),
        pl.BlockSpec((1, LANES), lambda p, c: (0, 0)),
        pl.BlockSpec((1, LANES), lambda p, c: (0, 0)),
        pl.BlockSpec((None, None, LANES, LANES), lambda p, c: (p, rc(c), 0, 0)),
        pl.BlockSpec((q, LANES), lambda p, c: (rc(c), p)),
    ]
    out_specs = [
        pl.BlockSpec((q, LANES), lambda p, c: (rc(c), p)),
        pl.BlockSpec((q, LANES), lambda p, c: (rc(c), p)),
        pl.BlockSpec((q, LANES), lambda p, c: (rc(c), p)),
        pl.BlockSpec((q, LANES), lambda p, c: (rc(c), p)),
        pl.BlockSpec((SUBLANES, LANES), lambda p, c: (p, 0)),
    ]
    wide = jax.ShapeDtypeStruct((lp, 4 * LANES), F32)
    return pl.pallas_call(
        body, name=name, grid=(4, nc), in_specs=in_specs, out_specs=out_specs,
        out_shape=[wide, wide, wide, wide, jax.ShapeDtypeStruct((4 * SUBLANES, LANES), F32)],
        scratch_shapes=[pltpu.VMEM((LANES, LANES), F32)],
        compiler_params=_cp("arbitrary", "arbitrary"),
    )(xbc_c, xbc_c, xbc_c, u_main, bias_row, alog_row, d_row, hprev, dy)


def _sb_blocks(qs, ks, r_runs, masked, bq):
    ri = lax.broadcasted_iota(jnp.int32, (bq, bq), 0)
    ci = lax.broadcasted_iota(jnp.int32, (bq, bq), 1)
    tri_after = (ri > ci).astype(BF16)
    zs = [_dot(qj, kj, "nt") for qj, kj in zip(qs, ks)]
    us, sigs, ubs = [], [], []
    for z in zs:
        u = -(jnp.maximum(z, 0.0) + jnp.log(1.0 + jnp.exp(-jnp.abs(z))))
        sigs.append(jnp.exp(z + u))
        if masked:
            u = jnp.where(ci < ri, u, 0.0)
        us.append(u)
        ubs.append(u.astype(BF16))
    afters = [_dot(ub, tri_after) for ub in ubs]
    ws = []
    for sig, after, r_run in zip(sigs, afters, r_runs):
        w = sig * jnp.exp(after + r_run)
        if masked:
            w = jnp.where(ci < ri, w, 0.0)
        ws.append(w)
    return us, sigs, ws


def _split_heads(x, lo):
    out = []
    zero = jnp.zeros((x.shape[0], LANES), x.dtype)
    for p in range(2):
        xp = x[:, LANES * p:LANES * (p + 1)]
        out += [jnp.where(lo, xp, zero), jnp.where(lo, zero, xp)]
    return out


def _per_head(x):
    return [x[:, :LANES], x[:, :LANES], x[:, LANES:], x[:, LANES:]]


def _resident(shape, col):
    return pl.BlockSpec(shape, lambda i: (0, col), pipeline_mode=pl.Buffered(1))


def sb_attn_fwd(qkv, name):
    lp = qkv.shape[0]
    bq = min(ROW_TILE, lp)
    nq = lp // bq
    assert nq <= 64

    def body(q_ref, k_ref, v_ref, o_ref, rs_ref):
        qi = pl.program_id(0)
        lane = _lane((bq, LANES))
        lo = lane < 64
        qs = _split_heads(q_ref[...], lo)

        def step(kb, carry, masked):
            off = pl.multiple_of(kb * bq, bq)
            ks = _per_head(k_ref[pl.ds(off, bq), :])
            vs = _per_head(v_ref[pl.ds(off, bq), :])
            heads, rss = carry
            r_runs = [heads[h][1] for h in range(4)]
            rss = list(rss)
            for h in range(4):
                rss[h // 2] = jnp.where(lane == 64 * (h % 2) + kb, r_runs[h], rss[h // 2])
            us, _, ws = _sb_blocks(qs, ks, r_runs, masked, bq)
            pvs = [_dot(ws[h].astype(BF16), vs[h]) for h in range(4)]
            out = tuple((heads[h][0] + pvs[h], r_runs[h] + jnp.sum(us[h], axis=1, keepdims=True)) for h in range(4))
            return out, tuple(rss)

        zero = (jnp.zeros((bq, LANES), F32), jnp.zeros((bq, 1), F32))
        zr = jnp.zeros((bq, LANES), F32)
        carry = step(qi, ((zero,) * 4, (zr, zr)), True)
        def two(t, c):
            return step(qi - 2 - 2 * t, step(qi - 1 - 2 * t, c, False), False)

        carry = lax.fori_loop(0, qi // 2, two, carry)
        heads, rss = lax.cond(qi % 2 == 1, lambda c: step(0, c, False), lambda c: c, carry)
        o_ref[...] = jnp.concatenate([jnp.where(lo, heads[0][0], heads[1][0]), jnp.where(lo, heads[2][0], heads[3][0])], axis=1)
        rs_ref[...] = jnp.concatenate(list(rss), axis=1)

    blk = pl.BlockSpec((bq, 2 * LANES), lambda i: (i, 0))
    return pl.pallas_call(
        body, name=name, grid=(nq,),
        in_specs=[blk, _resident((lp, 2 * LANES), 1), _resident((lp, 2 * LANES), 2)],
        out_specs=[blk, blk],
        out_shape=[jax.ShapeDtypeStruct((lp, SB_WIDTH), F32), jax.ShapeDtypeStruct((lp, SB_WIDTH), F32)],
        compiler_params=_cp("arbitrary"),
    )(qkv, qkv, qkv)


def sb_attn_bwd(qkv, rs, d_o, name):
    lp = qkv.shape[0]
    bq = min(ROW_TILE, lp)
    nq = lp // bq

    def body(q_ref, k_ref, v_ref, rs_ref, do_ref, dq_ref, dk_ref, dv_ref):
        qi = pl.program_id(0)

        @pl.when(qi == 0)
        def _():
            dk_ref[...] = jnp.zeros_like(dk_ref)
            dv_ref[...] = jnp.zeros_like(dv_ref)

        lane = _lane((bq, LANES))
        lo = lane < 64
        qs = _split_heads(q_ref[...], lo)
        dos = _split_heads(do_ref[...].astype(BF16), lo)
        rs_blk = rs_ref[...]
        ri = lax.broadcasted_iota(jnp.int32, (bq, bq), 0)
        ci = lax.broadcasted_iota(jnp.int32, (bq, bq), 1)
        tbefore = (ri < ci).astype(BF16)

        def step(kb, carry, masked):
            off = pl.multiple_of(kb * bq, bq)
            ks = _per_head(k_ref[pl.ds(off, bq), :])
            vs = _per_head(v_ref[pl.ds(off, bq), :])
            r_rights = [jnp.sum(jnp.where(lane == 64 * (h % 2) + kb, rs_blk[:, LANES * (h // 2):LANES * (h // 2 + 1)], 0.0),
                                axis=1, keepdims=True) for h in range(4)]
            dws = [_dot(dos[h], vs[h], "nt") for h in range(4)]
            _, sigs, ws = _sb_blocks(qs, ks, r_rights, masked, bq)
            gs = [ws[h] * dws[h] for h in range(4)]
            gbs = [g.astype(BF16) for g in gs]
            wbs = [w.astype(BF16) for w in ws]
            gbefores = [_dot(gb, tbefore) for gb in gbs]
            dv_acc = [_dot(wbs[2 * p], dos[2 * p], "tn") + _dot(wbs[2 * p + 1], dos[2 * p + 1], "tn") for p in range(2)]
            dzbs = []
            for h in range(4):
                dz = gs[h] - sigs[h] * (gs[h] + gbefores[h] + carry[h][1])
                if masked:
                    dz = jnp.where(ci < ri, dz, 0.0)
                dzbs.append(dz.astype(BF16))
            dqs = [_dot(dzbs[h], ks[h]) for h in range(4)]
            dk_acc = [_dot(dzbs[2 * p], qs[2 * p], "tn") + _dot(dzbs[2 * p + 1], qs[2 * p + 1], "tn") for p in range(2)]
            dk_ref[pl.ds(off, bq), :] += jnp.concatenate(dk_acc, axis=1)
            dv_ref[pl.ds(off, bq), :] += jnp.concatenate(dv_acc, axis=1)
            return tuple((carry[h][0] + dqs[h], carry[h][1] + jnp.sum(gs[h], axis=1, keepdims=True)) for h in range(4))

        zero = (jnp.zeros((bq, LANES), F32), jnp.zeros((bq, 1), F32))
        def two(t, c):
            return step(2 * t + 1, step(2 * t, c, False), False)

        carry = lax.fori_loop(0, qi // 2, two, (zero,) * 4)
        carry = lax.cond(qi % 2 == 1, lambda c: step(qi - 1, c, False), lambda c: c, carry)
        carry = step(qi, carry, True)
        dq_ref[...] = jnp.concatenate([jnp.where(lo, carry[0][0], carry[1][0]), jnp.where(lo, carry[2][0], carry[3][0])],
                                      axis=1).astype(dq_ref.dtype)

    blk = pl.BlockSpec((bq, 2 * LANES), lambda i: (i, 0))
    return pl.pallas_call(
        body, name=name, grid=(nq,),
        in_specs=[blk, _resident((lp, 2 * LANES), 1), _resident((lp, 2 * LANES), 2), blk, blk],
        out_specs=[blk, _resident((lp, 2 * LANES), 0), _resident((lp, 2 * LANES), 0)],
        out_shape=[jax.ShapeDtypeStruct((lp, SB_WIDTH), BF16), jax.ShapeDtypeStruct((lp, SB_WIDTH), F32),
                   jax.ShapeDtypeStruct((lp, SB_WIDTH), F32)],
        compiler_params=_cp("arbitrary"),
    )(qkv, qkv, qkv, rs, d_o)


def _mla_masks(bq):
    lane = _lane((bq, 2 * LANES))
    out = []
    for h in range(4):
        j = h % 2
        nope = jnp.logical_and(lane >= 64 * j, lane < 64 * (j + 1))
        rope = jnp.logical_and(lane >= LANES + MLA_ROPE * h, lane < LANES + MLA_ROPE * (h + 1))
        out.append(jnp.logical_or(nope, rope))
    return out


def _mla_split_q(q, masks):
    zero = jnp.zeros((q.shape[0], 2 * LANES), q.dtype)
    return [jnp.where(masks[h], q[:, 2 * LANES * (h // 2):2 * LANES * (h // 2 + 1)], zero) for h in range(4)]


def _mla_per_head_k(k):
    return [k[:, :2 * LANES], k[:, :2 * LANES], k[:, 2 * LANES:], k[:, 2 * LANES:]]


def mla_attn_fwd(qc, kc, v, name):
    lp = qc.shape[0]
    bq = min(ROW_TILE, lp)
    nq = lp // bq

    def body(q_ref, k_ref, v_ref, o_ref, lse_ref):
        qi = pl.program_id(0)
        qs = _mla_split_q(q_ref[...], _mla_masks(bq))
        lo = _lane((bq, LANES)) < 64
        ri = lax.broadcasted_iota(jnp.int32, (bq, bq), 0)
        ci = lax.broadcasted_iota(jnp.int32, (bq, bq), 1)

        def step(kb, carry, masked):
            off = pl.multiple_of(kb * bq, bq)
            ks = _mla_per_head_k(k_ref[pl.ds(off, bq), :])
            vs = _per_head(v_ref[pl.ds(off, bq), :])
            ss = [_dot(qs[h], ks[h], "nt") for h in range(4)]
            prs, alphas, stats = [], [], []
            for h in range(4):
                _, m_run, l_run = carry[h]
                s = ss[h]
                if masked:
                    s = jnp.where(ci <= ri, s, NEG)
                m_new = jnp.maximum(m_run, jnp.max(s, axis=1, keepdims=True))
                alpha = jnp.exp(m_run - m_new)
                pr = jnp.exp(s - m_new)
                prs.append(pr.astype(BF16))
                alphas.append(alpha)
                stats.append((m_new, l_run * alpha + jnp.sum(pr, axis=1, keepdims=True)))
            pvs = [_dot(prs[h], vs[h]) for h in range(4)]
            return tuple((carry[h][0] * alphas[h] + pvs[h],) + stats[h] for h in range(4))

        zero = (jnp.zeros((bq, LANES), F32), jnp.full((bq, 1), NEG, F32), jnp.zeros((bq, 1), F32))
        carry = step(qi, (zero,) * 4, True)
        def two(t, c):
            return step(qi - 2 - 2 * t, step(qi - 1 - 2 * t, c, False), False)

        carry = lax.fori_loop(0, qi // 2, two, carry)
        carry = lax.cond(qi % 2 == 1, lambda c: step(0, c, False), lambda c: c, carry)
        outs = [a / l for a, _, l in carry]
        lses = [m + jnp.log(l) for _, m, l in carry]
        o_ref[...] = jnp.concatenate([jnp.where(lo, outs[0], outs[1]), jnp.where(lo, outs[2], outs[3])], axis=1)
        lse_ref[...] = jnp.concatenate([jnp.where(lo, lses[0], lses[1]), jnp.where(lo, lses[2], lses[3])], axis=1)

    blk = pl.BlockSpec((bq, 2 * LANES), lambda i: (i, 0))
    return pl.pallas_call(
        body, name=name, grid=(nq,),
        in_specs=[pl.BlockSpec((bq, 4 * LANES), lambda i: (i, 0)), _resident((lp, 4 * LANES), 0), _resident((lp, 2 * LANES), 0)],
        out_specs=[blk, blk],
        out_shape=[jax.ShapeDtypeStruct((lp, 2 * LANES), F32), jax.ShapeDtypeStruct((lp, 2 * LANES), F32)],
        compiler_params=_cp("arbitrary"),
    )(qc, kc, v)


def mla_attn_bwd(qc, kc, v, o, lse, d_o, name):
    lp = qc.shape[0]
    bq = min(ROW_TILE, lp)
    nq = lp // bq

    def body(q_ref, k_ref, v_ref, o_ref, lse_ref, do_ref, dq_ref, dk_ref, dv_ref):
        qi = pl.program_id(0)

        @pl.when(qi == 0)
        def _():
            dk_ref[...] = jnp.zeros_like(dk_ref)
            dv_ref[...] = jnp.zeros_like(dv_ref)

        d_o = do_ref[...]
        masks = _mla_masks(bq)
        qs = _mla_split_q(q_ref[...], masks)
        lo = _lane((bq, LANES)) < 64
        dos = _split_heads(d_o.astype(BF16), lo)
        od = o_ref[...] * d_o
        lse_blk = lse_ref[...]
        delta, lses = [], []
        for h in range(4):
            odp = od[:, LANES * (h // 2):LANES * (h // 2 + 1)]
            delta.append(jnp.sum(jnp.where(lo, odp, 0.0) if h % 2 == 0 else jnp.where(lo, 0.0, odp), axis=1, keepdims=True))
            c0 = LANES * (h // 2) + 64 * (h % 2)
            lses.append(lse_blk[:, c0:c0 + 1])
        ri = lax.broadcasted_iota(jnp.int32, (bq, bq), 0)
        ci = lax.broadcasted_iota(jnp.int32, (bq, bq), 1)

        def step(kb, carry, masked):
            off = pl.multiple_of(kb * bq, bq)
            ks = _mla_per_head_k(k_ref[pl.ds(off, bq), :])
            vs = _per_head(v_ref[pl.ds(off, bq), :])
            ss = [_dot(qs[h], ks[h], "nt") for h in range(4)]
            dps = [_dot(dos[h], vs[h], "nt") for h in range(4)]
            prbs, dss = [], []
            for h in range(4):
                s = ss[h]
                if masked:
                    s = jnp.where(ci <= ri, s, NEG)
                pr = jnp.exp(s - lses[h])
                prbs.append(pr.astype(BF16))
                dss.append((pr * (dps[h] - delta[h])).astype(BF16))
            dv_acc = [_dot(prbs[2 * p], dos[2 * p], "tn") + _dot(prbs[2 * p + 1], dos[2 * p + 1], "tn") for p in range(2)]
            dqs = [_dot(dss[h], ks[h]) for h in range(4)]
            dk_acc = [_dot(dss[2 * p], qs[2 * p], "tn") + _dot(dss[2 * p + 1], qs[2 * p + 1], "tn") for p in range(2)]
            dk_ref[pl.ds(off, bq), :] += jnp.concatenate(dk_acc, axis=1)
            dv_ref[pl.ds(off, bq), :] += jnp.concatenate(dv_acc, axis=1)
            return tuple(carry[h] + dqs[h] for h in range(4))

        zero = jnp.zeros((bq, 2 * LANES), F32)
        carry = step(qi, (zero,) * 4, True)
        def two(t, c):
            return step(qi - 2 - 2 * t, step(qi - 1 - 2 * t, c, False), False)

        carry = lax.fori_loop(0, qi // 2, two, carry)
        carry = lax.cond(qi % 2 == 1, lambda c: step(0, c, False), lambda c: c, carry)
        dq_ref[...] = jnp.concatenate([jnp.where(masks[0], carry[0], 0.0) + jnp.where(masks[1], carry[1], 0.0),
                                       jnp.where(masks[2], carry[2], 0.0) + jnp.where(masks[3], carry[3], 0.0)], axis=1)

    blk = pl.BlockSpec((bq, 2 * LANES), lambda i: (i, 0))
    wide = pl.BlockSpec((bq, 4 * LANES), lambda i: (i, 0))
    return pl.pallas_call(
        body, name=name, grid=(nq,),
        in_specs=[wide, _resident((lp, 4 * LANES), 0), _resident((lp, 2 * LANES), 0), blk, blk, blk],
        out_specs=[wide, _resident((lp, 4 * LANES), 0), _resident((lp, 2 * LANES), 0)],
        out_shape=[jax.ShapeDtypeStruct((lp, 4 * LANES), F32), jax.ShapeDtypeStruct((lp, 4 * LANES), F32),
                   jax.ShapeDtypeStruct((lp, 2 * LANES), F32)],
        compiler_params=_cp("arbitrary"),
    )(qc, kc, v, o, lse, d_o)


def _mix_out(y_pre, z, o_sb, o_mla, g_ssd, g_sb, g_mla):
    return jnp.concatenate([_rms(y_pre * _silu(z), g_ssd), _rms(o_sb, g_sb), _rms(o_mla, g_mla)], axis=1)


def _ffn_act(up_a, up_b, halo_a, halo_b, w_a, w_b, b_a, b_b, i):
    ca = conv_fwd(up_a, halo_a, w_a, i) + b_a
    cb_ = conv_fwd(up_b, halo_b, w_b, i) + b_b
    return ca, cb_


def layer_fwd(h, w, cs, sn, l):
    lp = h.shape[0]
    nm = f"l{l}_"
    hn = rowwise(lambda i, n, x, g: _rms(x, g), [RI(h), PA(w["norm_mix_g"])], [RO(D_MODEL, BF16)], nm + "rms_mix", lp)[0]
    u = mm(hn, w["w_main"], "nn", F32, nm + "in_main")
    qkv = mm(hn, w["w_sb"], "nn", BF16, nm + "in_sb")
    xbc_c = rowwise(lambda i, n, x, hl, cw, cb_: _silu(conv_fwd(x, hl, cw, i) + cb_),
                    [RI(u, SSD_XBC, 0), HP(u, SSD_XBC, 0), PA(w["ssd_conv_w"]), PA(w["ssd_conv_b"])],
                    [RO(SSD_XBC, F32)], nm + "ssd_conv", lp)[0]
    y_pre, hprev = ssd_fwd(xbc_c, u, w["dt_bias"], w["a_log"], w["d_skip"], nm + "ssd_fwd")
    o_sb, rs_sb = sb_attn_fwd(qkv, nm + "sb_fwd")
    qn, kvn = rowwise(lambda i, n, qa, ckv, gq, gkv: (_rms(qa, gq, MLA_Q_RANK), _rms(ckv, gkv)),
                      [RI(u, 256, U_QA // 256), RI(u, LANES, U_CKV // LANES), PA(w["q_norm_g"]), PA(w["kv_norm_g"])],
                      [RO(256, BF16), RO(LANES, BF16)], nm + "mla_rms", lp)
    qf = mm(qn, w["w_uq"], "nn", F32, nm + "mla_uq")
    kvf = mm(kvn, w["w_ukv"], "nn", F32, nm + "mla_ukv")

    def pack(i, n, qf_, kvf_, kr4, cos, sin):
        qf_ = qf_ * MLA_SCALE
        qr = qf_[:, 256:384]
        qr = qr * cos + rope_rot(qr) * sin
        kr = kr4 * cos + rope_rot(kr4) * sin
        qc = jnp.concatenate([qf_[:, 0:128], qr, qf_[:, 128:256], qr], axis=1)
        kc = jnp.concatenate([kvf_[:, 0:128], kr, kvf_[:, 128:256], kr], axis=1)
        return qc, kc, kvf_[:, 256:512]

    qc, kc, vv = rowwise(pack, [RI(qf), RI(kvf), RI(u, LANES, U_KR4 // LANES), RI(cs), RI(sn)],
                         [RO(512, BF16), RO(512, BF16), RO(256, BF16)], nm + "mla_pack", lp)
    o_mla, lse = mla_attn_fwd(qc, kc, vv, nm + "mla_fwd")
    cat = rowwise(lambda i, n, *a: _mix_out(*a),
                  [RI(y_pre), RI(u, SSD_WIDTH, U_Z // SSD_WIDTH), RI(o_sb), RI(o_mla),
                   PA(w["ssd_norm_g"]), PA(w["sb_norm_g"]), PA(w["mla_norm_g"])],
                  [RO(D_MODEL, BF16)], nm + "mix_out", lp)[0]
    h_mid = mm(cat, w["w_out"], "nn", F32, nm + "out_proj", add=h)
    hn2 = rowwise(lambda i, n, x, g: _rms(x, g), [RI(h_mid), PA(w["norm_ffn_g"])], [RO(D_MODEL, BF16)], nm + "rms_ffn", lp)[0]
    up_a = mm(hn2, w["w_up_a"], "nn", F32, nm + "up_a")
    up_b = mm(hn2, w["w_up_b"], "nn", F32, nm + "up_b")
    wc = 1408

    def act(i, n, ua, ub, ha, hb_, wa, wb, ba, bb_):
        ca, cb_ = _ffn_act(ua, ub, ha, hb_, wa, wb, ba, bb_, i)
        return _silu(ca) * cb_

    a_t = rowwise(act, [RI(up_a, wc, 0, True), RI(up_b, wc, 0, True), HP(up_a, wc, 0, True), HP(up_b, wc, 0, True),
                        PA(w["ffn_conv_w_a"], wc, 0, True), PA(w["ffn_conv_w_b"], wc, 0, True),
                        PA(w["ffn_conv_b_a"], wc, 0, True), PA(w["ffn_conv_b_b"], wc, 0, True)],
                  [RO(D_FF, BF16, wc, True)], nm + "ffn_act", lp, ncol=D_FF // wc)[0]
    h_out = mm(a_t, w["w_down"], "nn", F32, nm + "down", add=h_mid)
    saved = dict(h=h, hn=hn, u=u, qkv=qkv, xbc_c=xbc_c, y_pre=y_pre, hprev=hprev, o_sb=o_sb, rs_sb=rs_sb, qn=qn, kvn=kvn,
                 qc=qc, kc=kc, vv=vv, o_mla=o_mla, lse=lse, cat=cat, h_mid=h_mid, hn2=hn2, up_a=up_a, up_b=up_b, a_t=a_t)
    return h_out, saved


def layer_bwd(dh_out, w, s, cs, sn, l):
    lp = dh_out.shape[0]
    nm = f"l{l}b_"
    g = {}
    wc = 1408
    ncolf = D_FF // wc
    g["w_down"] = mm(s["a_t"], dh_out, "tn", BF16, nm + "dw_down")
    d_act = mm(dh_out, w["w_down"], "nt", F32, nm + "d_act")

    def act_bwd(i, n, ua, ub, ha, hb_, wa, wb, ba, bb_, da_):
        ca, cb_ = _ffn_act(ua, ub, ha, hb_, wa, wb, ba, bb_, i)
        sg = jax.nn.sigmoid(ca)
        dca = da_ * cb_ * (sg * (1.0 + ca * (1.0 - sg)))
        dcb = da_ * (ca * sg)
        return (dca, dcb, conv_bwd_w(dca, ua, ha, i, FFN_CONV), conv_bwd_w(dcb, ub, hb_, i, FFN_CONV),
                jnp.sum(dca, axis=0, keepdims=True), jnp.sum(dcb, axis=0, keepdims=True))

    dca, dcb, g["ffn_conv_w_a"], g["ffn_conv_w_b"], g["ffn_conv_b_a"], g["ffn_conv_b_b"] = rowwise(
        act_bwd, [RI(s["up_a"], wc, 0, True), RI(s["up_b"], wc, 0, True), HP(s["up_a"], wc, 0, True),
                  HP(s["up_b"], wc, 0, True), PA(w["ffn_conv_w_a"], wc, 0, True), PA(w["ffn_conv_w_b"], wc, 0, True),
                  PA(w["ffn_conv_b_a"], wc, 0, True), PA(w["ffn_conv_b_b"], wc, 0, True), RI(d_act, wc, 0, True)],
        [RO(D_FF, F32, wc, True), RO(D_FF, F32, wc, True), AO(FFN_CONV, D_FF, wc, True), AO(FFN_CONV, D_FF, wc, True),
         AO(1, D_FF, wc, True), AO(1, D_FF, wc, True)], nm + "ffn_act_bwd", lp, ncol=ncolf)

    def conv_t(i, n, da_, db_, ha, hb_, wa, wb):
        return conv_bwd_data(da_, ha, wa, i, n), conv_bwd_data(db_, hb_, wb, i, n)

    dup_a, dup_b = rowwise(conv_t, [RI(dca, wc, 0, True), RI(dcb, wc, 0, True), HN(dca, wc, 0, True), HN(dcb, wc, 0, True),
                                    PA(w["ffn_conv_w_a"], wc, 0, True), PA(w["ffn_conv_w_b"], wc, 0, True)],
                           [RO(D_FF, BF16, wc, True), RO(D_FF, BF16, wc, True)], nm + "ffn_conv_t", lp, ncol=ncolf)
    g["w_up_a"] = mm(s["hn2"], dup_a, "tn", BF16, nm + "dw_up_a")
    g["w_up_b"] = mm(s["hn2"], dup_b, "tn", BF16, nm + "dw_up_b")
    dhn2 = mm(dup_a, w["w_up_a"], "nt", F32, nm + "dhn2_a")
    dhn2 = mm(dup_b, w["w_up_b"], "nt", F32, nm + "dhn2_b", add=dhn2)

    def rms_bwd(i, n, x, gg, dy, dres):
        _, vjp = jax.vjp(_rms, x, gg)
        dx, dg = vjp(dy)
        return dres + dx, dg

    dh_mid, g["norm_ffn_g"] = rowwise(rms_bwd, [RI(s["h_mid"]), PA(w["norm_ffn_g"]), RI(dhn2), RI(dh_out)],
                                      [RO(D_MODEL, F32), AO(1, D_MODEL)], nm + "rms_ffn_bwd", lp)
    g["w_out"] = mm(s["cat"], dh_mid, "tn", BF16, nm + "dw_out")
    d_cat = mm(dh_mid, w["w_out"], "nt", F32, nm + "d_cat")
    u = s["u"]

    def mix_bwd(i, n, y_pre, z, o_sb, o_mla, g1, g2, g3, dcat):
        _, vjp = jax.vjp(_mix_out, y_pre, z, o_sb, o_mla, g1, g2, g3)
        return vjp(dcat)

    dy_pre, dz, do_sb, do_mla, g["ssd_norm_g"], g["sb_norm_g"], g["mla_norm_g"] = rowwise(
        mix_bwd, [RI(s["y_pre"]), RI(u, SSD_WIDTH, U_Z // SSD_WIDTH), RI(s["o_sb"]), RI(s["o_mla"]),
                  PA(w["ssd_norm_g"]), PA(w["sb_norm_g"]), PA(w["mla_norm_g"]), RI(d_cat)],
        [RO(SSD_WIDTH, F32), RO(SSD_WIDTH, BF16), RO(SB_WIDTH, F32), RO(256, F32),
         AO(1, SSD_WIDTH), AO(1, SB_WIDTH), AO(1, 256)], nm + "mix_out_bwd", lp)
    dxs, dbp, dcp, ddtp, pg = ssd_bwd(s["xbc_c"], u, w["dt_bias"], w["a_log"], w["d_skip"], s["hprev"], dy_pre, nm + "ssd_bwd")
    pg = pg.reshape(4, SUBLANES, LANES).sum(axis=0)
    g["dt_bias"], g["a_log"], g["d_skip"] = pg[0:1], pg[1:2], pg[2:3]

    def conv4_bwd(i, n, x, hl, cw, cb_, dxs_, dbp_, dcp_, ddtp_):
        pre = conv_fwd(x, hl, cw, i) + cb_
        d_b = jnp.concatenate([dbp_[:, 0:128] + dbp_[:, 128:256], dbp_[:, 256:384] + dbp_[:, 384:512]], axis=1)
        d_c = jnp.concatenate([dcp_[:, 0:128] + dcp_[:, 128:256], dcp_[:, 256:384] + dcp_[:, 384:512]], axis=1)
        d_out = jnp.concatenate([dxs_, d_b, d_c], axis=1)
        sg = jax.nn.sigmoid(pre)
        d_pre = d_out * (sg * (1.0 + pre * (1.0 - sg)))
        ddt = ddtp_[:, 0:128] + ddtp_[:, 128:256] + ddtp_[:, 256:384] + ddtp_[:, 384:512]
        return d_pre, ddt, conv_bwd_w(d_pre, x, hl, i, SSD_CONV), jnp.sum(d_pre, axis=0, keepdims=True)

    d_pre, ddt, g["ssd_conv_w"], g["ssd_conv_b"] = rowwise(
        conv4_bwd, [RI(u, SSD_XBC, 0), HP(u, SSD_XBC, 0), PA(w["ssd_conv_w"]), PA(w["ssd_conv_b"]),
                    RI(dxs), RI(dbp), RI(dcp), RI(ddtp)],
        [RO(SSD_XBC, F32), RO(LANES, BF16), AO(SSD_CONV, SSD_XBC), AO(1, SSD_XBC)], nm + "ssd_conv_bwd", lp)
    d_xbc = rowwise(lambda i, n, d, hn_, cw: conv_bwd_data(d, hn_, cw, i, n),
                    [RI(d_pre), HN(d_pre), PA(w["ssd_conv_w"])], [RO(SSD_XBC, BF16)], nm + "ssd_conv_t", lp)[0]
    dq_sb, dk_sb, dv_sb = sb_attn_bwd(s["qkv"], s["rs_sb"], do_sb, nm + "sb_bwd")
    dqkv = jnp.concatenate([dq_sb, dk_sb.astype(BF16), dv_sb.astype(BF16)], axis=1)
    dqc, dkc, dvv = mla_attn_bwd(s["qc"], s["kc"], s["vv"], s["o_mla"], s["lse"], do_mla, nm + "mla_bwd")

    def unpack(i, n, dqc_, dkc_, dvv_, cos, sin):
        dqr = dqc_[:, 128:256] + dqc_[:, 384:512]
        dqr = dqr * cos + rope_rot_t(dqr * sin)
        dkr = dkc_[:, 128:256] + dkc_[:, 384:512]
        dkr = dkr * cos + rope_rot_t(dkr * sin)
        dq = jnp.concatenate([dqc_[:, 0:128], dqc_[:, 256:384], dqr], axis=1) * MLA_SCALE
        dkv = jnp.concatenate([dkc_[:, 0:128], dkc_[:, 256:384], dvv_], axis=1)
        return dq, dkv, dkr

    dq, dkv, dkr4 = rowwise(unpack, [RI(dqc), RI(dkc), RI(dvv), RI(cs), RI(sn)],
                            [RO(384, BF16), RO(512, BF16), RO(LANES, BF16)], nm + "mla_unpack", lp)
    g["w_uq"] = mm(s["qn"], dq, "tn", F32, nm + "dw_uq")
    g["w_ukv"] = mm(s["kvn"], dkv, "tn", F32, nm + "dw_ukv")
    dqn = mm(dq, w["w_uq"], "nt", F32, nm + "dqn")
    dkvn = mm(dkv, w["w_ukv"], "nt", F32, nm + "dkvn")

    def mla_rms_bwd(i, n, qa, ckv, gq, gkv, dqn_, dkvn_):
        _, vjp = jax.vjp(lambda a, b, c, d: (_rms(a, c, MLA_Q_RANK), _rms(b, d)), qa, ckv, gq, gkv)
        return vjp((dqn_, dkvn_))

    dqa, dckv, g["q_norm_g"], g["kv_norm_g"] = rowwise(
        mla_rms_bwd, [RI(u, 256, U_QA // 256), RI(u, LANES, U_CKV // LANES), PA(w["q_norm_g"]), PA(w["kv_norm_g"]),
                      RI(dqn), RI(dkvn)],
        [RO(256, BF16), RO(LANES, BF16), AO(1, 256), AO(1, LANES)], nm + "mla_rms_bwd", lp)
    du = jnp.concatenate([d_xbc, dz, dqa, dckv, dkr4, ddt, jnp.zeros((lp, LANES), BF16)], axis=1)
    g["w_main"] = mm(s["hn"], du, "tn", F32, nm + "dw_main")
    g["w_sb"] = mm(s["hn"], dqkv, "tn", F32, nm + "dw_sb")
    dhn = mm(du, w["w_main"], "nt", F32, nm + "dhn_main")
    dhn = mm(dqkv, w["w_sb"], "nt", F32, nm + "dhn_sb", add=dhn)
    dh_in, g["norm_mix_g"] = rowwise(rms_bwd, [RI(s["h"]), PA(w["norm_mix_g"]), RI(dhn), RI(dh_mid)],
                                     [RO(D_MODEL, F32), AO(1, D_MODEL)], nm + "rms_mix_bwd", lp)
    return dh_in, g


_IN_CUTS = np.cumsum((512, 1024, 8, 256, 256, 256, 192, 128, 32))


def _pad_cols(a, n):
    return jnp.pad(a, ((0, 0), (0, n - a.shape[1])))


def prep_layer_weights(full, l):
    w_in = full["w_in"][l]
    c = _IN_CUTS
    z, xbc, dtr = w_in[:, :c[0]], w_in[:, c[0]:c[1]], w_in[:, c[1]:c[2]]
    q_sb, k_sb, v_sb = w_in[:, c[2]:c[3]], w_in[:, c[3]:c[4]], w_in[:, c[4]:c[5]]
    q_a, c_kv, k_r = w_in[:, c[5]:c[6]], w_in[:, c[6]:c[7]], w_in[:, c[7]:c[8]]
    w_main = jnp.concatenate([xbc, z, _pad_cols(q_a, 256), c_kv, k_r, k_r, k_r, k_r, _pad_cols(dtr, 256)], axis=1)
    assert w_main.shape[1] == U_MAIN
    row = lambda v, n=None: _pad_cols(v.reshape(1, -1).astype(F32), v.size if n is None else n)
    uq = full["mla_w_uq"][l].reshape(MLA_Q_RANK, 4, 96)
    w_uq = jnp.concatenate([uq[:, :, :64].reshape(MLA_Q_RANK, 256), uq[:, :, 64:].reshape(MLA_Q_RANK, 128)], axis=1)
    w_uq = jnp.pad(w_uq, ((0, 256 - MLA_Q_RANK), (0, 0)))
    ukv = full["mla_w_ukv"][l].reshape(MLA_KV_RANK, 4, 128)
    w_ukv = jnp.concatenate([ukv[:, :, :64].reshape(MLA_KV_RANK, 256), ukv[:, :, 64:].reshape(MLA_KV_RANK, 256)], axis=1)
    return dict(
        norm_mix_g=row(full["norm_mix_g"][l]), w_main=w_main, w_sb=jnp.concatenate([q_sb * SB_SCALE, k_sb, v_sb], axis=1),
        ssd_conv_w=full["ssd_conv_w"][l], ssd_conv_b=row(full["ssd_conv_b"][l]),
        dt_bias=row(full["ssd_dt_bias"][l], LANES), a_log=row(full["ssd_a_log"][l], LANES), d_skip=row(full["ssd_d"][l], LANES),
        ssd_norm_g=row(full["ssd_norm_g"][l]), sb_norm_g=row(full["sb_norm_g"][l]),
        q_norm_g=row(full["mla_q_norm_g"][l], 256), kv_norm_g=row(full["mla_kv_norm_g"][l]),
        w_uq=w_uq, w_ukv=w_ukv, mla_norm_g=row(full["mla_norm_g"][l]),
        w_out=full["w_out"][l], norm_ffn_g=row(full["norm_ffn_g"][l]),
        w_up_a=full["ffn_w_up"][l][:, :D_FF], w_up_b=full["ffn_w_up"][l][:, D_FF:],
        ffn_conv_w_a=full["ffn_conv_w"][l][:, :D_FF], ffn_conv_w_b=full["ffn_conv_w"][l][:, D_FF:],
        ffn_conv_b_a=row(full["ffn_conv_b"][l][:D_FF]), ffn_conv_b_b=row(full["ffn_conv_b"][l][D_FF:]),
        w_down=full["ffn_w_down"][l],
    )


def unprep_layer_grads(g):
    wm = g["w_main"]
    xbc, z = wm[:, U_XBC:U_XBC + 1024], wm[:, U_Z:U_Z + 512]
    q_a, c_kv = wm[:, U_QA:U_QA + MLA_Q_RANK], wm[:, U_CKV:U_CKV + 128]
    k_r = (wm[:, U_KR4:U_KR4 + 32] + wm[:, U_KR4 + 32:U_KR4 + 64] + wm[:, U_KR4 + 64:U_KR4 + 96] + wm[:, U_KR4 + 96:U_KR4 + 128])
    dtr = wm[:, U_DT:U_DT + SSD_HEADS]
    w_sb = g["w_sb"]
    w_in = jnp.concatenate([z, xbc, dtr, w_sb[:, :SB_WIDTH] * SB_SCALE, w_sb[:, SB_WIDTH:], q_a, c_kv, k_r], axis=1)
    guq = g["w_uq"][:MLA_Q_RANK]
    guq = jnp.concatenate([guq[:, :256].reshape(MLA_Q_RANK, 4, 64), guq[:, 256:].reshape(MLA_Q_RANK, 4, 32)], axis=2)
    gukv = g["w_ukv"]
    gukv = jnp.concatenate([gukv[:, :256].reshape(MLA_KV_RANK, 4, 64), gukv[:, 256:].reshape(MLA_KV_RANK, 4, 64)], axis=2)
    return dict(
        norm_mix_g=g["norm_mix_g"][0], w_in=w_in, ssd_conv_w=g["ssd_conv_w"], ssd_conv_b=g["ssd_conv_b"][0],
        ssd_dt_bias=g["dt_bias"][0, :SSD_HEADS], ssd_a_log=g["a_log"][0, :SSD_HEADS], ssd_d=g["d_skip"][0, :SSD_HEADS],
        ssd_norm_g=g["ssd_norm_g"][0], sb_norm_g=g["sb_norm_g"][0], mla_q_norm_g=g["q_norm_g"][0, :MLA_Q_RANK],
        mla_kv_norm_g=g["kv_norm_g"][0], mla_w_uq=guq.reshape(MLA_Q_RANK, 384), mla_w_ukv=gukv.reshape(MLA_KV_RANK, 512),
        mla_norm_g=g["mla_norm_g"][0], w_out=g["w_out"], norm_ffn_g=g["norm_ffn_g"][0],
        ffn_w_up=jnp.concatenate([g["w_up_a"], g["w_up_b"]], axis=1),
        ffn_conv_w=jnp.concatenate([g["ffn_conv_w_a"], g["ffn_conv_w_b"]], axis=1),
        ffn_conv_b=jnp.concatenate([g["ffn_conv_b_a"][0], g["ffn_conv_b_b"][0]], axis=0),
        ffn_w_down=g["w_down"],
    )


def rope_tables(lp):
    pos = jnp.arange(lp, dtype=F32)
    inv = 1.0 / (ROPE_BASE ** (jnp.arange(0, MLA_ROPE, 2, dtype=F32) / MLA_ROPE))
    ang = pos[:, None] * inv[None, :]
    ang = jnp.concatenate([ang, ang] * 4, axis=-1)
    return jnp.cos(ang), jnp.sin(ang)


def local_step(x_seq, target, full):
    seq = x_seq.shape[0]
    length = seq + N_META
    lp = -(-length // ROW_TILE) * ROW_TILE
    cs, sn = rope_tables(lp)
    h = jnp.concatenate([full["meta_tokens"].astype(F32), x_seq, jnp.zeros((lp - length, D_MODEL), F32)], axis=0)
    tgt = jnp.pad(target, ((N_META, lp - length), (0, 0)))
    ws = [prep_layer_weights(full, l) for l in range(DEPTH)]
    saved = []
    for l in range(DEPTH):
        h, s = layer_fwd(h, ws[l], cs, sn, l)
        saved.append(s)
    fg = full["final_norm_g"].reshape(1, D_MODEL).astype(F32)
    tm = min(ROW_TILE, lp)

    def loss_fn(i, n, x, g, t):
        rows = _rows_iota(x) + i * tm
        valid = jnp.logical_and(rows >= N_META, rows < length)

        def f(x_, g_):
            err = jnp.where(valid, _rms(x_, g_) - t, 0.0)
            return 0.5 * jnp.sum(err * err) * (1.0 / D_MODEL)

        val, (dx, dg) = jax.value_and_grad(f, argnums=(0, 1))(x, g)
        return dx, jnp.full((1, LANES), val, F32), dg

    dh, loss_row, g_final = rowwise(loss_fn, [RI(h), PA(fg), RI(tgt)], [RO(D_MODEL, F32), AO(1, LANES), AO(1, D_MODEL)],
                                    "loss_head", lp)
    grads = {}
    per_layer = [None] * DEPTH
    for l in reversed(range(DEPTH)):
        dh, g = layer_bwd(dh, ws[l], saved[l], cs, sn, l)
        per_layer[l] = unprep_layer_grads(g)
    for k in per_layer[0]:
        grads[k] = jnp.stack([per_layer[l][k] for l in range(DEPTH)], axis=0)
    grads["final_norm_g"] = g_final[0]
    grads["meta_tokens"] = dh[:N_META]
    return loss_row[0, 0], dh[N_META:length], grads


_ANY = pl.BlockSpec(memory_space=pl.ANY)


def chip_exchange(srcs, modes, name):
    n = len(srcs)
    flips = ((1, 0), (0, 1), (1, 1))

    def body(*refs):
        ins, outs = refs[:n], refs[n:2 * n]
        send_sems, recv_sems, fwd_send_sems, fwd_recv_sems, loc_sems = refs[2 * n:]
        x, y, c = lax.axis_index("x"), lax.axis_index("y"), lax.axis_index("c")
        me = 2 * x + y
        waits, forwards = [], []
        for a in range(n):
            whole = modes[a] != "slab"
            cp = pltpu.make_async_copy(ins[a] if whole else ins[a].at[me], outs[a].at[me], loc_sems.at[a])
            cp.start()
            waits.append(cp.wait)
            half = ins[a].shape[0] // 2 if modes[a] == "bcast_split" else None
            for k, (fx, fy) in enumerate(flips):
                px = 1 - x if fx else x
                py = 1 - y if fy else y
                peer = 2 * px + py
                if half is None:
                    src = ins[a] if whole else ins[a].at[peer]
                    dst = outs[a].at[me]
                else:
                    src = ins[a].at[pl.ds(c * half, half)]
                    dst = outs[a].at[me, pl.ds(c * half, half)]
                rc = pltpu.make_async_remote_copy(src_ref=src, dst_ref=dst, send_sem=send_sems.at[a, k],
                                                  recv_sem=recv_sems.at[a, k], device_id=(px, py, c), device_id_type=MESH_ID)
                rc.start()
                if half is None:
                    waits.append(rc.wait)
                else:
                    waits.append(rc.wait_send)
                    landed = outs[a].at[peer, pl.ds(c * half, half)]
                    fw = pltpu.make_async_remote_copy(src_ref=landed, dst_ref=landed, send_sem=fwd_send_sems.at[a, k],
                                                      recv_sem=fwd_recv_sems.at[a, k], device_id=(x, y, 1 - c),
                                                      device_id_type=MESH_ID)
                    forwards.append((rc, fw))
        for rc, fw in forwards:
            rc.wait_recv()
            fw.start()
        for rc, fw in forwards:
            fw.wait()
        for w in waits:
            w()

    out_shape = [jax.ShapeDtypeStruct((N_CHIPS,) + (s.shape if m != "slab" else s.shape[1:]), s.dtype) for s, m in zip(srcs, modes)]
    return pl.pallas_call(
        body, name=name, in_specs=[_ANY] * n, out_specs=[_ANY] * n, out_shape=out_shape,
        scratch_shapes=[pltpu.SemaphoreType.DMA((n, 3)), pltpu.SemaphoreType.DMA((n, 3)), pltpu.SemaphoreType.DMA((n, 3)),
                        pltpu.SemaphoreType.DMA((n, 3)), pltpu.SemaphoreType.DMA((n,))],
    )(*srcs)


def _piece(ref, mode, k):
    if mode == "slab":
        return ref.at[k]
    if mode == "rows":
        rs = ref.shape[1] // N_CHIPS
        return ref.at[:, pl.ds(pl.multiple_of(k * rs, 16), rs), :]
    if mode == "cols":
        cs = ref.shape[2] // N_CHIPS
        return ref.at[:, :, pl.ds(pl.multiple_of(k * cs, LANES), cs)]
    return ref


def _piece_shape(shape, mode):
    if mode == "slab":
        return shape[1:]
    if mode == "rows":
        return (shape[0], shape[1] // N_CHIPS, shape[2])
    if mode == "cols":
        return (shape[0], shape[1], shape[2] // N_CHIPS)
    return shape


def grad_exchange(srcs, modes, name):
    n = len(srcs)
    flips = ((1, 0), (0, 1), (1, 1))

    def body(*refs):
        ins, outs = refs[:n], refs[n:2 * n]
        send_sems, recv_sems, fwd_send_sems, fwd_recv_sems, sib_send_sems, sib_recv_sems, loc_sems = refs[2 * n:]
        x, y, c = lax.axis_index("x"), lax.axis_index("y"), lax.axis_index("c")
        me = 2 * x + y
        sibling = (x, y, 1 - c)
        waits, forwards = [], []
        for a in range(n):
            mine = _piece(ins[a], modes[a], me)
            slot = outs[a].at[4 * c + me]
            cp = pltpu.make_async_copy(mine, slot, loc_sems.at[a])
            cp.start()
            sb = pltpu.make_async_remote_copy(src_ref=mine, dst_ref=slot, send_sem=sib_send_sems.at[a],
                                              recv_sem=sib_recv_sems.at[a], device_id=sibling, device_id_type=MESH_ID)
            sb.start()
            waits += [cp.wait, sb.wait]
            for k, (fx, fy) in enumerate(flips):
                px = 1 - x if fx else x
                py = 1 - y if fy else y
                peer = 2 * px + py
                rc = pltpu.make_async_remote_copy(src_ref=_piece(ins[a], modes[a], peer), dst_ref=slot,
                                                  send_sem=send_sems.at[a, k], recv_sem=recv_sems.at[a, k],
                                                  device_id=(px, py, c), device_id_type=MESH_ID)
                rc.start()
                landed = outs[a].at[4 * c + peer]
                fw = pltpu.make_async_remote_copy(src_ref=landed, dst_ref=landed, send_sem=fwd_send_sems.at[a, k],
                                                  recv_sem=fwd_recv_sems.at[a, k], device_id=sibling, device_id_type=MESH_ID)
                waits.append(rc.wait_send)
                forwards.append((rc, fw))
        for rc, fw in forwards:
            rc.wait_recv()
            fw.start()
        for rc, fw in forwards:
            fw.wait()
        for w in waits:
            w()

    out_shape = [jax.ShapeDtypeStruct((2 * N_CHIPS,) + tuple(_piece_shape(s.shape, m)), s.dtype) for s, m in zip(srcs, modes)]
    dma = pltpu.SemaphoreType.DMA
    return pl.pallas_call(
        body, name=name, in_specs=[_ANY] * n, out_specs=[_ANY] * n, out_shape=out_shape,
        scratch_shapes=[dma((n, 3)), dma((n, 3)), dma((n, 3)), dma((n, 3)), dma((n,)), dma((n,)), dma((n,))],
    )(*srcs)


WEIGHT_NAMES = ("meta_tokens", "norm_mix_g", "w_in", "ssd_conv_w", "ssd_conv_b", "ssd_dt_bias", "ssd_a_log", "ssd_d",
                "ssd_norm_g", "sb_norm_g", "mla_q_norm_g", "mla_kv_norm_g", "mla_w_uq", "mla_w_ukv", "mla_norm_g",
                "w_out", "norm_ffn_g", "ffn_w_up", "ffn_conv_w", "ffn_conv_b", "ffn_w_down", "final_norm_g")
SHARD_AXIS = {"meta_tokens": 1, "w_in": 2, "ssd_conv_w": 2, "mla_w_uq": 2, "mla_w_ukv": 2, "w_out": 1, "ffn_w_up": 2,
              "ffn_conv_w": 2, "ffn_w_down": 1}
SHARDED = tuple(n for n in WEIGHT_NAMES if n in SHARD_AXIS)
REPLICATED = tuple(n for n in WEIGHT_NAMES if n not in SHARD_AXIS)
GATHER_BF16 = ("w_in", "mla_w_uq", "mla_w_ukv", "w_out", "ffn_w_up", "ffn_w_down")
GATHER_F32 = ("meta_tokens", "ssd_conv_w", "ffn_conv_w")
PACK_ROWS = ROW_TILE


def pack(arrs, dtype):
    flat = jnp.concatenate([a.reshape(-1).astype(dtype) for a in arrs])
    per = PACK_ROWS * PACK_W
    total = -(-flat.size // per) * per
    return jnp.pad(flat, (0, total - flat.size)).reshape(total // PACK_W, PACK_W)


def unpack(buf, shapes):
    flat = buf.reshape(-1)
    out, off = [], 0
    for shp in shapes:
        size = int(np.prod(shp))
        out.append(flat[off:off + size].reshape(shp))
        off += size
    return out


def gather_weights(a):
    full = {n: a[n] for n in REPLICATED}
    bufs = [pack([a[n] for n in GATHER_BF16], BF16), pack([a[n] for n in GATHER_F32], F32)]
    got = chip_exchange(bufs, ("bcast_split", "bcast"), "gather_weights")
    for names, g in ((GATHER_BF16, got[0]), (GATHER_F32, got[1])):
        pieces = [unpack(g[k], [a[n].shape for n in names]) for k in range(N_CHIPS)]
        for idx, n in enumerate(names):
            full[n] = jnp.concatenate([pieces[k][idx] for k in range(N_CHIPS)], axis=SHARD_AXIS[n])
    return full


BIG = ("w_in", "w_out", "ffn_w_up", "ffn_w_down")
BIG_MODE = {"w_in": "slab", "w_out": "rows", "ffn_w_up": "cols", "ffn_w_down": "rows"}
SMALL_SHARDED = tuple(n for n in SHARDED if n not in BIG)
ADAM_TILE = 128


def _adamw(i, n, *vals):
    parts, (w, m, v) = vals[:2 * N_CHIPS], vals[2 * N_CHIPS:]
    g = parts[0].astype(F32)
    for p in parts[1:]:
        g = g + p.astype(F32)
    m = ADAM_B1 * m + (1.0 - ADAM_B1) * g
    v = ADAM_B2 * v + (1.0 - ADAM_B2) * jnp.square(g)
    m_hat = m / (1.0 - ADAM_B1 ** ADAM_STEP)
    v_hat = v / (1.0 - ADAM_B2 ** ADAM_STEP)
    delta = -ADAM_LR * (m_hat / (jnp.sqrt(v_hat) + ADAM_EPS) + ADAM_WD * w)
    return g, delta, m, v


def _adamw_call(got, w, m, v, name):
    rows, width = w.shape
    flat = got.reshape(2 * N_CHIPS * rows, width)
    blk = rows // ADAM_TILE
    ins = [RI(flat, rblk=k * blk) for k in range(2 * N_CHIPS)] + [RI(w), RI(m), RI(v)]
    return rowwise(_adamw, ins, [RO(width, F32)] * 4, name, rows, tm=ADAM_TILE)


def reduce_and_update(a, grads):
    srcs, modes = [], []
    for n in BIG:
        g = grads[n].astype(BF16)
        if n == "w_in":
            cs = a[n].shape[2]
            g = g.reshape(DEPTH, D_MODEL, N_CHIPS, cs).transpose(2, 0, 1, 3)
        srcs.append(g)
        modes.append(BIG_MODE[n])
    slabs = []
    for k in range(N_CHIPS):
        parts = []
        for n in SMALL_SHARDED:
            ax = SHARD_AXIS[n]
            size = a[n].shape[ax]
            parts.append(lax.slice_in_dim(grads[n], k * size, (k + 1) * size, axis=ax))
        slabs.append(pack(parts, BF16))
    srcs += [jnp.stack(slabs, axis=0), pack([grads[n] for n in REPLICATED], F32)]
    modes += ["slab", "bcast"]
    got = grad_exchange(srcs, modes, "exchange_grads")
    outs = {}
    kinds = ("grad", "delta", "new_m", "new_v")
    for n, g8 in zip(BIG, got):
        shp = a[n].shape
        rows = shp[0] * shp[1]
        flat = lambda t: t.reshape(rows, shp[2])
        res = _adamw_call(g8.reshape(2 * N_CHIPS, rows, shp[2]), flat(a[n]), flat(a["m_" + n]), flat(a["v_" + n]), "adamw_" + n)
        for kind, val in zip(kinds, res):
            outs[(kind, n)] = val.reshape(shp)
    for tag, names, g8 in (("small", SMALL_SHARDED, got[len(BIG)]), ("rep", REPLICATED, got[len(BIG) + 1])):
        shapes = [a[n].shape for n in names]
        packed = [pack([a[pre + n] for n in names], F32) for pre in ("", "m_", "v_")]
        res = _adamw_call(g8, *packed, "adamw_" + tag)
        for kind, buf in zip(kinds, res):
            for n, val in zip(names, unpack(buf, shapes)):
                outs[(kind, n)] = val
    return outs


INPUT_NAMES = ("x",) + WEIGHT_NAMES + ("loss_target",) + tuple("m_" + n for n in WEIGHT_NAMES) + tuple("v_" + n for n in WEIGHT_NAMES)


def kernel(x, meta_tokens, norm_mix_g, w_in, ssd_conv_w, ssd_conv_b, ssd_dt_bias, ssd_a_log, ssd_d, ssd_norm_g, sb_norm_g, mla_q_norm_g, mla_kv_norm_g, mla_w_uq, mla_w_ukv, mla_norm_g, w_out, norm_ffn_g, ffn_w_up, ffn_conv_w, ffn_conv_b, ffn_w_down, final_norm_g, loss_target, m_meta_tokens, m_norm_mix_g, m_w_in, m_ssd_conv_w, m_ssd_conv_b, m_ssd_dt_bias, m_ssd_a_log, m_ssd_d, m_ssd_norm_g, m_sb_norm_g, m_mla_q_norm_g, m_mla_kv_norm_g, m_mla_w_uq, m_mla_w_ukv, m_mla_norm_g, m_w_out, m_norm_ffn_g, m_ffn_w_up, m_ffn_conv_w, m_ffn_conv_b, m_ffn_w_down, m_final_norm_g, v_meta_tokens, v_norm_mix_g, v_w_in, v_ssd_conv_w, v_ssd_conv_b, v_ssd_dt_bias, v_ssd_a_log, v_ssd_d, v_ssd_norm_g, v_sb_norm_g, v_mla_q_norm_g, v_mla_kv_norm_g, v_mla_w_uq, v_mla_w_ukv, v_mla_norm_g, v_w_out, v_norm_ffn_g, v_ffn_w_up, v_ffn_conv_w, v_ffn_conv_b, v_ffn_w_down, v_final_norm_g):
    args = (x, meta_tokens, norm_mix_g, w_in, ssd_conv_w, ssd_conv_b, ssd_dt_bias, ssd_a_log, ssd_d, ssd_norm_g, sb_norm_g, mla_q_norm_g, mla_kv_norm_g, mla_w_uq, mla_w_ukv, mla_norm_g, w_out, norm_ffn_g, ffn_w_up, ffn_conv_w, ffn_conv_b, ffn_w_down, final_norm_g, loss_target, m_meta_tokens, m_norm_mix_g, m_w_in, m_ssd_conv_w, m_ssd_conv_b, m_ssd_dt_bias, m_ssd_a_log, m_ssd_d, m_ssd_norm_g, m_sb_norm_g, m_mla_q_norm_g, m_mla_kv_norm_g, m_mla_w_uq, m_mla_w_ukv, m_mla_norm_g, m_w_out, m_norm_ffn_g, m_ffn_w_up, m_ffn_conv_w, m_ffn_conv_b, m_ffn_w_down, m_final_norm_g, v_meta_tokens, v_norm_mix_g, v_w_in, v_ssd_conv_w, v_ssd_conv_b, v_ssd_dt_bias, v_ssd_a_log, v_ssd_d, v_ssd_norm_g, v_sb_norm_g, v_mla_q_norm_g, v_mla_kv_norm_g, v_mla_w_uq, v_mla_w_ukv, v_mla_norm_g, v_w_out, v_norm_ffn_g, v_ffn_w_up, v_ffn_conv_w, v_ffn_conv_b, v_ffn_w_down, v_final_norm_g)
    a = dict(zip(INPUT_NAMES, args, strict=True))
    full = gather_weights(a)
    loss, grad_x, grads = local_step(a["x"][0], a["loss_target"][0], full)
    loss = lax.psum(loss, ("x", "y", "c"))
    outs = reduce_and_update(a, grads)
    result = [loss, grad_x[None]]
    for kind in ("grad", "delta", "new_m", "new_v"):
        result += [outs[(kind, n)] for n in WEIGHT_NAMES]
    return tuple(result)
```

```python
import functools
import math

import numpy as np
import jax
import jax.numpy as jnp
from jax import lax
from jax.experimental import pallas as pl
from jax.experimental.pallas import tpu as pltpu

F32 = jnp.float32
BF16 = jnp.bfloat16
HIGHEST = lax.Precision.HIGHEST
MESH_ID = pl.DeviceIdType.MESH

D_MODEL = 1024
DEPTH = 2
N_META = 16
EPS = 1e-6
SSD_HEADS = 8
SSD_WIDTH = 512
SSD_XBC = 1024
SSD_CONV = 4
SB_WIDTH = 256
SB_SCALE = 64 ** -0.5
MLA_Q_RANK = 192
MLA_KV_RANK = 128
MLA_ROPE = 32
MLA_SCALE = 96 ** -0.5
ROPE_BASE = 10000.0
D_FF = 2816
FFN_CONV = 3
IN_COLS = 2664
N_CHIPS = 4

ADAM_LR = 0.001
ADAM_B1 = 0.9
ADAM_B2 = 0.999
ADAM_EPS = 1e-08
ADAM_WD = 0.01
ADAM_STEP = 10

LANES = 128
SUBLANES = 8
ROW_TILE = 256
KEY_UNROLL = 4
VMEM_LIMIT = 56 * 1024 * 1024
PACK_W = 1024

U_XBC, U_Z, U_QA, U_CKV, U_KR4, U_DT, U_MAIN = 0, 1024, 1536, 1792, 1920, 2048, 2304
NEG = -1e30


def _cp(*sem):
    return pltpu.CompilerParams(dimension_semantics=sem if sem else None, vmem_limit_bytes=VMEM_LIMIT)


def _pick(dim, pref):
    if dim <= pref:
        return dim
    best = None
    for t in range(LANES, pref + 1, LANES):
        if dim % t == 0:
            best = t
    assert best is not None, (dim, pref)
    return best


def _dot(a, b, dims="nn", precision=None):
    dn = {"nn": (((1,), (0,)), ((), ())), "nt": (((1,), (1,)), ((), ())), "tn": (((0,), (0,)), ((), ()))}[dims]
    return lax.dot_general(a, b, dn, preferred_element_type=F32, precision=precision)


def _softplus(x):
    return jnp.maximum(x, 0.0) + jnp.log1p(jnp.exp(-jnp.abs(x)))


def _silu(x):
    return x * jax.nn.sigmoid(x)


def _rms(x, g, n=None):
    n = x.shape[-1] if n is None else n
    ms = jnp.sum(x * x, axis=-1, keepdims=True) * (1.0 / n)
    return x * lax.rsqrt(ms + EPS) * g


def mm(a, b, dims, out_dtype, name, add=None, tm=None, tn=None, tk=None):
    if dims == "nn":
        (m, k), (k2, n) = a.shape, b.shape
    elif dims == "nt":
        (m, k), (n, k2) = a.shape, b.shape
    else:
        (k, m), (k2, n) = a.shape, b.shape
    assert k == k2, (a.shape, b.shape, dims)
    if dims == "tn":
        tm, tn, tk = _pick(m, tm or 1408), _pick(n, tn or 1408), _pick(k, tk or 1408)
    else:
        tm, tn, tk = _pick(m, tm or 768), _pick(n, tn or 1408), _pick(k, tk or 2816)
    nk = k // tk
    if dims == "tn":
        a_spec = pl.BlockSpec((tk, tm), lambda j, i, kk: (kk, i))
    else:
        a_spec = pl.BlockSpec((tm, tk), lambda j, i, kk: (i, kk))
    if dims == "nt":
        b_spec = pl.BlockSpec((tn, tk), lambda j, i, kk: (j, kk))
    else:
        b_spec = pl.BlockSpec((tk, tn), lambda j, i, kk: (kk, j))
    o_spec = pl.BlockSpec((tm, tn), lambda j, i, kk: (i, j))
    has_add = add is not None

    def body(*refs):
        a_ref, b_ref = refs[0], refs[1]
        add_ref = refs[2] if has_add else None
        o_ref = refs[3] if has_add else refs[2]
        part = _dot(a_ref[...].astype(BF16), b_ref[...].astype(BF16), dims)

        def finish(r):
            if has_add:
                r = r + add_ref[...].astype(F32)
            o_ref[...] = r.astype(o_ref.dtype)

        if nk == 1:
            finish(part)
            return
        acc_ref = refs[-1]
        kk = pl.program_id(2)

        @pl.when(kk == 0)
        def _():
            acc_ref[...] = part

        @pl.when(jnp.logical_and(kk > 0, kk < nk - 1))
        def _():
            acc_ref[...] += part

        @pl.when(kk == nk - 1)
        def _():
            finish(acc_ref[...] + part)

    in_specs = [a_spec, b_spec] + ([o_spec] if has_add else [])
    args = (a, b) + ((add,) if has_add else ())
    return pl.pallas_call(
        body, name=name, grid=(n // tn, m // tm, nk),
        in_specs=in_specs, out_specs=o_spec,
        out_shape=jax.ShapeDtypeStruct((m, n), out_dtype),
        scratch_shapes=[pltpu.VMEM((tm, tn), F32)] if nk > 1 else [],
        compiler_params=_cp("parallel", "parallel", "arbitrary"),
    )(*args)


def RI(arr, width=None, cidx=0, cv=False, rblk=0):
    return ("row" if rblk == 0 else ("row", rblk), arr, arr.shape[1] if width is None else width, cidx, cv)


def HP(arr, width=None, cidx=0, cv=False):
    return ("prev", arr, arr.shape[1] if width is None else width, cidx, cv)


def HN(arr, width=None, cidx=0, cv=False):
    return ("next", arr, arr.shape[1] if width is None else width, cidx, cv)


def PA(arr, width=None, cidx=0, cv=False):
    return ("par", arr, arr.shape[1] if width is None else width, cidx, cv)


def RO(ncols, dtype, width=None, cv=False):
    return ("row", ncols, dtype, ncols if width is None else width, cv)


def AO(nrows, ncols, width=None, cv=False):
    return ("acc", (nrows, ncols), F32, ncols if width is None else width, cv)


def rowwise(fn, ins, outs, name, rows, tm=ROW_TILE, ncol=1):
    tm = min(tm, rows)
    assert rows % tm == 0
    nrow = rows // tm
    hb = tm // SUBLANES
    last_hb = rows // SUBLANES - 1
    in_specs, args = [], []
    for kind, arr, width, cidx, cv in ins:
        def cmap(j, cidx=cidx, cv=cv):
            return cidx + j if cv else cidx
        if kind == "row":
            spec = pl.BlockSpec((tm, width), lambda j, i, cmap=cmap: (i, cmap(j)))
        elif isinstance(kind, tuple):
            spec = pl.BlockSpec((tm, width), lambda j, i, cmap=cmap, rblk=kind[1]: (i + rblk, cmap(j)))
        elif kind == "prev":
            spec = pl.BlockSpec((SUBLANES, width), lambda j, i, cmap=cmap: (jnp.maximum(i * hb - 1, 0), cmap(j)))
        elif kind == "next":
            spec = pl.BlockSpec((SUBLANES, width), lambda j, i, cmap=cmap: (jnp.minimum((i + 1) * hb, last_hb), cmap(j)))
        else:
            spec = pl.BlockSpec((arr.shape[0], width), lambda j, i, cmap=cmap: (0, cmap(j)))
        in_specs.append(spec)
        args.append(arr)
    out_specs, out_shapes, acc_cv = [], [], []
    for kind, shp, dtype, width, cv in outs:
        if kind == "row":
            out_specs.append(pl.BlockSpec((tm, width), lambda j, i, cv=cv: (i, j if cv else 0)))
            out_shapes.append(jax.ShapeDtypeStruct((rows, shp), dtype))
            acc_cv.append(None)
        else:
            out_specs.append(pl.BlockSpec((shp[0], width), lambda j, i, cv=cv: (0, j if cv else 0)))
            out_shapes.append(jax.ShapeDtypeStruct(shp, dtype))
            acc_cv.append(cv)
    n_in = len(ins)

    def body(*refs):
        j = pl.program_id(0)
        i = pl.program_id(1)
        vals = fn(i, nrow, *[r[...] for r in refs[:n_in]])
        if not isinstance(vals, (tuple, list)):
            vals = (vals,)
        for o_ref, v, cv in zip(refs[n_in:], vals, acc_cv):
            if cv is None:
                o_ref[...] = v.astype(o_ref.dtype)
            else:
                first = (i == 0) if cv else jnp.logical_and(i == 0, j == 0)

                @pl.when(first)
                def _(o_ref=o_ref, v=v):
                    o_ref[...] = v.astype(o_ref.dtype)

                @pl.when(jnp.logical_not(first))
                def _(o_ref=o_ref, v=v):
                    o_ref[...] += v.astype(o_ref.dtype)

    res = pl.pallas_call(
        body, name=name, grid=(ncol, nrow), in_specs=in_specs, out_specs=out_specs, out_shape=out_shapes,
        compiler_params=_cp("arbitrary", "arbitrary"),
    )(*args)
    return res


def _rows_iota(x):
    return lax.broadcasted_iota(jnp.int32, x.shape, 0)


def shift_down(x, halo, s):
    if s == 0:
        return x
    tm = x.shape[0]
    top = pltpu.roll(halo, s, 0)
    if tm > SUBLANES:
        top = jnp.concatenate([top, jnp.zeros((tm - SUBLANES, x.shape[1]), x.dtype)], axis=0)
    return jnp.where(_rows_iota(x) < s, top, pltpu.roll(x, s, 0))


def shift_up(x, halo, s):
    if s == 0:
        return x
    tm = x.shape[0]
    bot = pltpu.roll(halo, SUBLANES - s, 0)
    if tm > SUBLANES:
        bot = jnp.concatenate([jnp.zeros((tm - SUBLANES, x.shape[1]), x.dtype), bot], axis=0)
    return jnp.where(_rows_iota(x) >= tm - s, bot, pltpu.roll(x, tm - s, 0))


def conv_fwd(x, halo, w, i):
    kw = w.shape[0]
    halo = jnp.where(i == 0, 0.0, halo)
    out = None
    for k in range(kw):
        term = w[k:k + 1, :] * shift_down(x, halo, kw - 1 - k)
        out = term if out is None else out + term
    return out


def conv_bwd_data(dy, halo_next, w, i, n):
    kw = w.shape[0]
    halo_next = jnp.where(i == n - 1, 0.0, halo_next)
    out = None
    for k in range(kw):
        term = w[k:k + 1, :] * shift_up(dy, halo_next, kw - 1 - k)
        out = term if out is None else out + term
    return out


def conv_bwd_w(dy, x, halo, i, kw):
    halo = jnp.where(i == 0, 0.0, halo)
    rows = [jnp.sum(dy * shift_down(x, halo, kw - 1 - k), axis=0, keepdims=True) for k in range(kw)]
    return jnp.concatenate(rows, axis=0)


def _lane(shape):
    return lax.broadcasted_iota(jnp.int32, shape, 1)


def rope_rot(x):
    lane = _lane(x.shape) % MLA_ROPE
    return jnp.where(lane < MLA_ROPE // 2, -pltpu.roll(x, LANES - MLA_ROPE // 2, 1), pltpu.roll(x, MLA_ROPE // 2, 1))


def rope_rot_t(g):
    lane = _lane(g.shape) % MLA_ROPE
    return jnp.where(lane < MLA_ROPE // 2, pltpu.roll(g, LANES - MLA_ROPE // 2, 1), -pltpu.roll(g, MLA_ROPE // 2, 1))


def _ssd_common(g, xs, dt_raw, bias, alog, q):
    lane = _lane((q, LANES))
    pre = dt_raw + bias
    dt = jnp.where(lane < SSD_HEADS, _softplus(pre), 0.0)
    a_row = -jnp.exp(alog)
    d_a = dt * a_row
    ri = lax.broadcasted_iota(jnp.int32, (q, q), 0)
    ci = lax.broadcasted_iota(jnp.int32, (q, q), 1)
    causal = ri >= ci
    acs = _dot(causal.astype(F32), d_a, "nn", HIGHEST)
    acs_t = acs.T
    subl = lax.broadcasted_iota(jnp.int32, (LANES, q), 0)
    heads = [4 * g + i for i in range(4)]
    lo = lane < 64

    def col(arr, h):
        return jnp.sum(jnp.where(lane == h, arr, 0.0), axis=1, keepdims=True)

    def lanes4(v):
        m = lo if v[0].shape[0] == q else lo[0:1, :]
        return jnp.concatenate([jnp.where(m, v[0], v[1]), jnp.where(m, v[2], v[3])], axis=1)

    cols = [col(acs, h) for h in heads]
    rows = [jnp.sum(jnp.where(subl == h, acs_t, 0.0), axis=0, keepdims=True) for h in heads]
    tots = [c_[q - 1:q, :] for c_ in cols]
    acs4 = lanes4(cols)
    dt4 = lanes4([col(dt, h) for h in heads])
    lms = [jnp.exp(jnp.where(causal, cols[i] - rows[i], NEG)) for i in range(4)]
    lane4 = _lane((q, 2 * LANES))
    hm = [jnp.logical_and(lane4 >= 64 * i, lane4 < 64 * (i + 1)) for i in range(4)]
    return dict(lane=lane, lo=lo, pre=pre, dt=dt, a_row=a_row, heads=heads, tots=tots, lms=lms, ri=ri, ci=ci, hm=hm,
                lanes4=lanes4, eacs=jnp.exp(acs4), dte=jnp.exp(lanes4(tots) - acs4), dt4=dt4, x=xs * dt4)


def _pick_lane(row_arr, h):
    return jnp.sum(jnp.where(_lane(row_arr.shape) == h, row_arr, 0.0), axis=1, keepdims=True)


def _etot(tots):
    sub = lax.broadcasted_iota(jnp.int32, (2 * LANES, LANES), 0)
    e = [jnp.exp(t) for t in tots]
    return jnp.where(sub < 64, e[0], jnp.where(sub < 128, e[1], jnp.where(sub < 192, e[2], e[3]))), e


def _half(arr, i, lo):
    slab = arr[:, LANES * (i // 2):LANES * (i // 2 + 1)]
    return jnp.where(lo, slab, 0.0) if i % 2 == 0 else jnp.where(lo, 0.0, slab)


def ssd_fwd(xbc_c, u_main, bias_row, alog_row, d_row, name):
    lp = xbc_c.shape[0]
    q = min(ROW_TILE, lp)
    nc = lp // q
    dt_blk = U_DT // LANES

    def body(xs_ref, b_ref, c_ref, dt_ref, bias_ref, alog_ref, d_ref, y_ref, hp_ref, h_scr):
        g = pl.program_id(0)
        c = pl.program_id(1)

        @pl.when(c == 0)
        def _():
            h_scr[...] = jnp.zeros_like(h_scr)

        xs = xs_ref[...]
        bb = b_ref[...].astype(BF16)
        cb_ = c_ref[...].astype(BF16)
        s = _ssd_common(g, xs, dt_ref[...], bias_ref[...], alog_ref[...], q)
        gmat = _dot(cb_, bb, "nt")
        ms = [(gmat * s["lms"][i]).astype(BF16) for i in range(4)]
        xjs = [_half(s["x"], i, s["lo"]).astype(BF16) for i in range(4)]
        ys = [_dot(ms[i], xjs[i]) for i in range(4)]
        hp = h_scr[...]
        hp_ref[...] = hp
        yoff = _dot(cb_, hp.astype(BF16), "nt") * s["eacs"]
        d4 = s["lanes4"]([_pick_lane(d_ref[...], h) for h in s["heads"]])
        y_ref[...] = jnp.concatenate([ys[0] + ys[1], ys[2] + ys[3]], axis=1) + yoff + d4 * xs
        etot, _ = _etot(s["tots"])
        h_scr[...] = hp * etot + _dot((s["x"] * s["dte"]).astype(BF16), bb, "tn")

    in_specs = [
        pl.BlockSpec((q, 2 * LANES), lambda g, c: (c, g)),
        pl.BlockSpec((q, LANES), lambda g, c: (c, 4 + g)),
        pl.BlockSpec((q, LANES), lambda g, c: (c, 6 + g)),
        pl.BlockSpec((q, LANES), lambda g, c: (c, dt_blk)),
        pl.BlockSpec((1, LANES), lambda g, c: (0, 0)),
        pl.BlockSpec((1, LANES), lambda g, c: (0, 0)),
        pl.BlockSpec((1, LANES), lambda g, c: (0, 0)),
    ]
    out_specs = [
        pl.BlockSpec((q, 2 * LANES), lambda g, c: (c, g)),
        pl.BlockSpec((None, None, 2 * LANES, LANES), lambda g, c: (g, c, 0, 0)),
    ]
    return pl.pallas_call(
        body, name=name, grid=(2, nc), in_specs=in_specs, out_specs=out_specs,
        out_shape=[jax.ShapeDtypeStruct((lp, SSD_WIDTH), F32), jax.ShapeDtypeStruct((2, nc, 2 * LANES, LANES), F32)],
        scratch_shapes=[pltpu.VMEM((2 * LANES, LANES), F32)],
        compiler_params=_cp("arbitrary", "arbitrary"),
    )(xbc_c, xbc_c, xbc_c, u_main, bias_row, alog_row, d_row)


def ssd_bwd(xbc_c, u_main, bias_row, alog_row, d_row, hprev, dy, name):
    lp = xbc_c.shape[0]
    q = min(ROW_TILE, lp)
    nc = lp // q
    dt_blk = U_DT // LANES

    def body(xs_ref, b_ref, c_ref, dt_ref, bias_ref, alog_ref, d_ref, hp_ref, dy_ref,
             dxs_ref, db_ref, dc_ref, ddt_ref, pg_ref, dh_scr):
        g = pl.program_id(0)
        cc = pl.program_id(1)

        @pl.when(cc == 0)
        def _():
            dh_scr[...] = jnp.zeros_like(dh_scr)
            pg_ref[...] = jnp.zeros_like(pg_ref)

        xs = xs_ref[...]
        bb = b_ref[...].astype(BF16)
        cb_ = c_ref[...].astype(BF16)
        s = _ssd_common(g, xs, dt_ref[...], bias_ref[...], alog_ref[...], q)
        lane, lo, x, hm, heads = s["lane"], s["lo"], s["x"], s["hm"], s["heads"]
        d_y = dy_ref[...]
        hp = hp_ref[...]
        hpb = hp.astype(BF16)
        dhn = dh_scr[...]
        dhnb = dhn.astype(BF16)
        xd = x * s["dte"]
        gmat = _dot(cb_, bb, "nt")
        m32s = [gmat * s["lms"][i] for i in range(4)]
        xjs = [_half(x, i, lo).astype(BF16) for i in range(4)]
        dyjs = [_half(d_y, i, lo).astype(BF16) for i in range(4)]
        dxparts = [_dot(m32s[i].astype(BF16), dyjs[i], "tn") for i in range(4)]
        dms = [_dot(dyjs[i], xjs[i], "nt") for i in range(4)]
        dg = dms[0] * s["lms"][0] + dms[1] * s["lms"][1] + dms[2] * s["lms"][2] + dms[3] * s["lms"][3]
        wms = [dms[i] * m32s[i] for i in range(4)]
        row_part = [jnp.sum(wm, axis=1, keepdims=True) for wm in wms]
        col_part = [jnp.sum(wm, axis=0, keepdims=True) for wm in wms]
        dgb = dg.astype(BF16)
        yoff = _dot(cb_, hpb, "nt") * s["eacs"]
        d_t = (d_y * s["eacs"]).astype(BF16)
        d_c = _dot(dgb, bb) + _dot(d_t, hpb)
        d_hp = _dot(d_t, cb_, "tn")
        dxd = _dot(bb, dhnb, "nt")
        d_b = _dot(dgb, cb_, "tn") + _dot(xd.astype(BF16), dhnb)
        d_x = jnp.concatenate([dxparts[0] + dxparts[1], dxparts[2] + dxparts[3]], axis=1) + dxd * s["dte"]
        r = dxd * xd
        a_terms = d_y * yoff - r

        def hsum(arr):
            return [jnp.sum(jnp.where(hm[i], arr, 0.0), axis=1, keepdims=True) for i in range(4)]

        dacs, rs = hsum(a_terms), hsum(r)
        hh = dhn * hp
        sub = lax.broadcasted_iota(jnp.int32, (2 * LANES, LANES), 0)
        hsums = [jnp.sum(jnp.where(jnp.logical_and(sub >= 64 * i, sub < 64 * (i + 1)), hh, 0.0), keepdims=True) for i in range(4)]
        last = lax.broadcasted_iota(jnp.int32, (q, 1), 0) == q - 1
        etot, etots = _etot(s["tots"])
        ddacs = jnp.zeros((q, LANES), F32)
        for i, h in enumerate(heads):
            dtot = jnp.sum(rs[i], keepdims=True) + hsums[i] * etots[i]
            ddacs = ddacs + jnp.where(lane == h, dacs[i] + row_part[i] + jnp.where(last, dtot, 0.0), 0.0)
        subl = lax.broadcasted_iota(jnp.int32, (LANES, q), 0)
        cols_t = jnp.zeros((LANES, q), F32)
        for i, h in enumerate(heads):
            cols_t = cols_t + jnp.where(subl == h, col_part[i], 0.0)
        ddacs = ddacs - cols_t.T
        anti = (s["ri"] <= s["ci"]).astype(F32)
        da = _dot(anti, ddacs, "nn", HIGHEST)
        ddt_own = hsum(d_x * xs)
        ddt = da * s["a_row"]
        for i, h in enumerate(heads):
            ddt = ddt + jnp.where(lane == h, ddt_own[i], 0.0)
        draw = ddt * jax.nn.sigmoid(s["pre"])
        ddt_ref[...] = draw
        d4 = s["lanes4"]([_pick_lane(d_ref[...], h) for h in heads])
        dxs_ref[...] = d4 * d_y + d_x * s["dt4"]
        db_ref[...] = d_b
        dc_ref[...] = d_c
        dds = hsum(d_y * xs)
        lane1 = lane[0:1, :]
        dd_row = jnp.zeros((1, LANES), F32)
        for i, h in enumerate(heads):
            dd_row = dd_row + jnp.where(lane1 == h, jnp.sum(dds[i], keepdims=True), 0.0)
        dbias_row = jnp.sum(draw, axis=0, keepdims=True)
        dalog_row = jnp.sum(da * s["dt"], axis=0, keepdims=True) * s["a_row"]
        sub8 = lax.broadcasted_iota(jnp.int32, (SUBLANES, LANES), 0)
        pg_ref[...] += (jnp.where(sub8 == 0, dbias_row, 0.0) + jnp.where(sub8 == 1, dalog_row, 0.0)
                        + jnp.where(sub8 == 2, dd_row, 0.0))
        dh_scr[...] = d_hp + etot * dhn

    rc = lambda c: nc - 1 - c
    in_specs = [
        pl.BlockSpec((q, 2 * LANES), lambda g, c: (rc(c), g)),
        pl.BlockSpec((q, LANES), lambda g, c: (rc(c), 4 + g)),
        pl.BlockSpec((q, LANES), lambda g, c: (rc(c), 6 + g)),
        pl.BlockSpec((q, LANES), lambda g, c: (rc(c), dt_blk)),
        pl.BlockSpec((1, LANES), lambda g, c: (0, 0)),
        pl.BlockSpec((1, LANES), lambda g, c: (0, 0)),
        pl.BlockSpec((1, LANES), lambda g, c: (0, 0)),
        pl.BlockSpec((None, None, 2 * LANES, LANES), lambda g, c: (g, rc(c), 0, 0)),
        pl.BlockSpec((q, 2 * LANES), lambda g, c: (rc(c), g)),
    ]
    out_specs = [
        pl.BlockSpec((q, 2 * LANES), lambda g, c: (rc(c), g)),
        pl.BlockSpec((q, LANES), lambda g, c: (rc(c), g)),
        pl.BlockSpec((q, LANES), lambda g, c: (rc(c), g)),
        pl.BlockSpec((q, LANES), lambda g, c: (rc(c), g)),
        pl.BlockSpec((SUBLANES, LANES), lambda g, c: (g, 0)),
    ]
    per_group = jax.ShapeDtypeStruct((lp, 2 * LANES), F32)
    return pl.pallas_call(
        body, name=name, grid=(2, nc), in_specs=in_specs, out_specs=out_specs,
        out_shape=[jax.ShapeDtypeStruct((lp, SSD_WIDTH), F32), per_group, per_group, per_group,
                   jax.ShapeDtypeStruct((2 * SUBLANES, LANES), F32)],
        scratch_shapes=[pltpu.VMEM((2 * LANES, LANES), F32)],
        compiler_params=_cp("arbitrary", "arbitrary"),
    )(xbc_c, xbc_c, xbc_c, u_main, bias_row, alog_row, d_row, hprev, dy)


def _sb_blocks(qs, ks, r_runs, masked, bq):
    ri = lax.broadcasted_iota(jnp.int32, (bq, bq), 0)
    ci = lax.broadcasted_iota(jnp.int32, (bq, bq), 1)
    tri_after = (ri > ci).astype(BF16)
    zs = [_dot(qj, kj, "nt") for qj, kj in zip(qs, ks)]
    us, sigs, ubs = [], [], []
    for z in zs:
        u = -(jnp.maximum(z, 0.0) + jnp.log(1.0 + jnp.exp(-jnp.abs(z))))
        sigs.append(jnp.exp(z + u))
        if masked:
            u = jnp.where(ci < ri, u, 0.0)
        us.append(u)
        ubs.append(u.astype(BF16))
    afters = [_dot(ub, tri_after) for ub in ubs]
    ws = []
    for sig, after, r_run in zip(sigs, afters, r_runs):
        w = sig * jnp.exp(after + r_run)
        if masked:
            w = jnp.where(ci < ri, w, 0.0)
        ws.append(w)
    return us, sigs, ws


def _split_heads(x, lo):
    out = []
    zero = jnp.zeros((x.shape[0], LANES), x.dtype)
    for p in range(2):
        xp = x[:, LANES * p:LANES * (p + 1)]
        out += [jnp.where(lo, xp, zero), jnp.where(lo, zero, xp)]
    return out


def _per_head(x):
    return [x[:, :LANES], x[:, :LANES], x[:, LANES:], x[:, LANES:]]


def _resident(shape, col):
    return pl.BlockSpec(shape, lambda i: (0, col), pipeline_mode=pl.Buffered(1))


def sb_attn_fwd(qkv, name):
    lp = qkv.shape[0]
    bq = min(ROW_TILE, lp)
    nq = lp // bq
    assert nq <= 64

    def body(q_ref, k_ref, v_ref, o_ref, rs_ref):
        qi = pl.program_id(0)
        lane = _lane((bq, LANES))
        lo = lane < 64
        qs = _split_heads(q_ref[...], lo)

        def step(kb, carry, masked):
            off = pl.multiple_of(kb * bq, bq)
            ks = _per_head(k_ref[pl.ds(off, bq), :])
            vs = _per_head(v_ref[pl.ds(off, bq), :])
            heads, rss = carry
            r_runs = [heads[h][1] for h in range(4)]
            rss = list(rss)
            for h in range(4):
                rss[h // 2] = jnp.where(lane == 64 * (h % 2) + kb, r_runs[h], rss[h // 2])
            us, _, ws = _sb_blocks(qs, ks, r_runs, masked, bq)
            pvs = [_dot(ws[h].astype(BF16), vs[h]) for h in range(4)]
            out = tuple((heads[h][0] + pvs[h], r_runs[h] + jnp.sum(us[h], axis=1, keepdims=True)) for h in range(4))
            return out, tuple(rss)

        zero = (jnp.zeros((bq, LANES), F32), jnp.zeros((bq, 1), F32))
        zr = jnp.zeros((bq, LANES), F32)
        carry = step(qi, ((zero,) * 4, (zr, zr)), True)
        def several(t, c):
            for r in range(KEY_UNROLL):
                c = step(qi - 1 - r - KEY_UNROLL * t, c, False)
            return c

        carry = lax.fori_loop(0, qi // KEY_UNROLL, several, carry)
        rem = qi % KEY_UNROLL
        heads, rss = lax.fori_loop(0, rem, lambda t, c: step(rem - 1 - t, c, False), carry)
        o_ref[...] = jnp.concatenate([jnp.where(lo, heads[0][0], heads[1][0]), jnp.where(lo, heads[2][0], heads[3][0])], axis=1)
        rs_ref[...] = jnp.concatenate(list(rss), axis=1)

    blk = pl.BlockSpec((bq, 2 * LANES), lambda i: (i, 0))
    return pl.pallas_call(
        body, name=name, grid=(nq,),
        in_specs=[blk, _resident((lp, 2 * LANES), 1), _resident((lp, 2 * LANES), 2)],
        out_specs=[blk, blk],
        out_shape=[jax.ShapeDtypeStruct((lp, SB_WIDTH), F32), jax.ShapeDtypeStruct((lp, SB_WIDTH), F32)],
        compiler_params=_cp("arbitrary"),
    )(qkv, qkv, qkv)


def sb_attn_bwd(qkv, rs, d_o, name):
    lp = qkv.shape[0]
    bq = min(ROW_TILE, lp)
    nq = lp // bq

    def body(q_ref, k_ref, v_ref, rs_ref, do_ref, dq_ref, dk_ref, dv_ref):
        qi = pl.program_id(0)

        @pl.when(qi == 0)
        def _():
            dk_ref[...] = jnp.zeros_like(dk_ref)
            dv_ref[...] = jnp.zeros_like(dv_ref)

        lane = _lane((bq, LANES))
        lo = lane < 64
        qs = _split_heads(q_ref[...], lo)
        dos = _split_heads(do_ref[...].astype(BF16), lo)
        rs_blk = rs_ref[...]
        ri = lax.broadcasted_iota(jnp.int32, (bq, bq), 0)
        ci = lax.broadcasted_iota(jnp.int32, (bq, bq), 1)
        tbefore = (ri < ci).astype(BF16)

        def step(kb, carry, masked):
            off = pl.multiple_of(kb * bq, bq)
            ks = _per_head(k_ref[pl.ds(off, bq), :])
            vs = _per_head(v_ref[pl.ds(off, bq), :])
            r_rights = [jnp.sum(jnp.where(lane == 64 * (h % 2) + kb, rs_blk[:, LANES * (h // 2):LANES * (h // 2 + 1)], 0.0),
                                axis=1, keepdims=True) for h in range(4)]
            dws = [_dot(dos[h], vs[h], "nt") for h in range(4)]
            _, sigs, ws = _sb_blocks(qs, ks, r_rights, masked, bq)
            gs = [ws[h] * dws[h] for h in range(4)]
            gbs = [g.astype(BF16) for g in gs]
            wbs = [w.astype(BF16) for w in ws]
            gbefores = [_dot(gb, tbefore) for gb in gbs]
            dv_acc = [_dot(wbs[2 * p], dos[2 * p], "tn") + _dot(wbs[2 * p + 1], dos[2 * p + 1], "tn") for p in range(2)]
            dzbs = []
            for h in range(4):
                dz = gs[h] - sigs[h] * (gs[h] + gbefores[h] + carry[h][1])
                if masked:
                    dz = jnp.where(ci < ri, dz, 0.0)
                dzbs.append(dz.astype(BF16))
            dqs = [_dot(dzbs[h], ks[h]) for h in range(4)]
            dk_acc = [_dot(dzbs[2 * p], qs[2 * p], "tn") + _dot(dzbs[2 * p + 1], qs[2 * p + 1], "tn") for p in range(2)]
            dk_ref[pl.ds(off, bq), :] += jnp.concatenate(dk_acc, axis=1)
            dv_ref[pl.ds(off, bq), :] += jnp.concatenate(dv_acc, axis=1)
            return tuple((carry[h][0] + dqs[h], carry[h][1] + jnp.sum(gs[h], axis=1, keepdims=True)) for h in range(4))

        zero = (jnp.zeros((bq, LANES), F32), jnp.zeros((bq, 1), F32))
        def several(t, c):
            for r in range(KEY_UNROLL):
                c = step(KEY_UNROLL * t + r, c, False)
            return c

        carry = lax.fori_loop(0, qi // KEY_UNROLL, several, (zero,) * 4)
        carry = lax.fori_loop(qi - qi % KEY_UNROLL, qi, lambda t, c: step(t, c, False), carry)
        carry = step(qi, carry, True)
        dq_ref[...] = jnp.concatenate([jnp.where(lo, carry[0][0], carry[1][0]), jnp.where(lo, carry[2][0], carry[3][0])],
                                      axis=1).astype(dq_ref.dtype)

    blk = pl.BlockSpec((bq, 2 * LANES), lambda i: (i, 0))
    return pl.pallas_call(
        body, name=name, grid=(nq,),
        in_specs=[blk, _resident((lp, 2 * LANES), 1), _resident((lp, 2 * LANES), 2), blk, blk],
        out_specs=[blk, _resident((lp, 2 * LANES), 0), _resident((lp, 2 * LANES), 0)],
        out_shape=[jax.ShapeDtypeStruct((lp, SB_WIDTH), BF16), jax.ShapeDtypeStruct((lp, SB_WIDTH), F32),
                   jax.ShapeDtypeStruct((lp, SB_WIDTH), F32)],
        compiler_params=_cp("arbitrary"),
    )(qkv, qkv, qkv, rs, d_o)


def _mla_masks(bq):
    lane = _lane((bq, 2 * LANES))
    out = []
    for h in range(4):
        j = h % 2
        nope = jnp.logical_and(lane >= 64 * j, lane < 64 * (j + 1))
        rope = jnp.logical_and(lane >= LANES + MLA_ROPE * h, lane < LANES + MLA_ROPE * (h + 1))
        out.append(jnp.logical_or(nope, rope))
    return out


def _mla_split_q(q, masks):
    zero = jnp.zeros((q.shape[0], 2 * LANES), q.dtype)
    return [jnp.where(masks[h], q[:, 2 * LANES * (h // 2):2 * LANES * (h // 2 + 1)], zero) for h in range(4)]


def _mla_per_head_k(k):
    return [k[:, :2 * LANES], k[:, :2 * LANES], k[:, 2 * LANES:], k[:, 2 * LANES:]]


def mla_attn_fwd(qc, kc, v, name):
    lp = qc.shape[0]
    bq = min(ROW_TILE, lp)
    nq = lp // bq

    def body(q_ref, k_ref, v_ref, o_ref, lse_ref):
        qi = pl.program_id(0)
        qs = _mla_split_q(q_ref[...], _mla_masks(bq))
        lo = _lane((bq, LANES)) < 64
        ri = lax.broadcasted_iota(jnp.int32, (bq, bq), 0)
        ci = lax.broadcasted_iota(jnp.int32, (bq, bq), 1)

        def step(kb, carry, masked):
            off = pl.multiple_of(kb * bq, bq)
            ks = _mla_per_head_k(k_ref[pl.ds(off, bq), :])
            vs = _per_head(v_ref[pl.ds(off, bq), :])
            ss = [_dot(qs[h], ks[h], "nt") for h in range(4)]
            prs, alphas, stats = [], [], []
            for h in range(4):
                _, m_run, l_run = carry[h]
                s = ss[h]
                if masked:
                    s = jnp.where(ci <= ri, s, NEG)
                m_new = jnp.maximum(m_run, jnp.max(s, axis=1, keepdims=True))
                alpha = jnp.exp(m_run - m_new)
                pr = jnp.exp(s - m_new)
                prs.append(pr.astype(BF16))
                alphas.append(alpha)
                stats.append((m_new, l_run * alpha + jnp.sum(pr, axis=1, keepdims=True)))
            pvs = [_dot(prs[h], vs[h]) for h in range(4)]
            return tuple((carry[h][0] * alphas[h] + pvs[h],) + stats[h] for h in range(4))

        zero = (jnp.zeros((bq, LANES), F32), jnp.full((bq, 1), NEG, F32), jnp.zeros((bq, 1), F32))
        carry = step(qi, (zero,) * 4, True)
        def several(t, c):
            for r in range(KEY_UNROLL):
                c = step(qi - 1 - r - KEY_UNROLL * t, c, False)
            return c

        carry = lax.fori_loop(0, qi // KEY_UNROLL, several, carry)
        rem = qi % KEY_UNROLL
        carry = lax.fori_loop(0, rem, lambda t, c: step(rem - 1 - t, c, False), carry)
        outs = [a / l for a, _, l in carry]
        lses = [m + jnp.log(l) for _, m, l in carry]
        o_ref[...] = jnp.concatenate([jnp.where(lo, outs[0], outs[1]), jnp.where(lo, outs[2], outs[3])], axis=1)
        lse_ref[...] = jnp.concatenate([jnp.where(lo, lses[0], lses[1]), jnp.where(lo, lses[2], lses[3])], axis=1)

    blk = pl.BlockSpec((bq, 2 * LANES), lambda i: (i, 0))
    return pl.pallas_call(
        body, name=name, grid=(nq,),
        in_specs=[pl.BlockSpec((bq, 4 * LANES), lambda i: (i, 0)), _resident((lp, 4 * LANES), 0), _resident((lp, 2 * LANES), 0)],
        out_specs=[blk, blk],
        out_shape=[jax.ShapeDtypeStruct((lp, 2 * LANES), F32), jax.ShapeDtypeStruct((lp, 2 * LANES), F32)],
        compiler_params=_cp("arbitrary"),
    )(qc, kc, v)


def mla_attn_bwd(qc, kc, v, o, lse, d_o, name):
    lp = qc.shape[0]
    bq = min(ROW_TILE, lp)
    nq = lp // bq

    def body(q_ref, k_ref, v_ref, o_ref, lse_ref, do_ref, dq_ref, dk_ref, dv_ref):
        qi = pl.program_id(0)

        @pl.when(qi == 0)
        def _():
            dk_ref[...] = jnp.zeros_like(dk_ref)
            dv_ref[...] = jnp.zeros_like(dv_ref)

        d_o = do_ref[...]
        masks = _mla_masks(bq)
        qs = _mla_split_q(q_ref[...], masks)
        lo = _lane((bq, LANES)) < 64
        dos = _split_heads(d_o.astype(BF16), lo)
        od = o_ref[...] * d_o
        lse_blk = lse_ref[...]
        delta, lses = [], []
        for h in range(4):
            odp = od[:, LANES * (h // 2):LANES * (h // 2 + 1)]
            delta.append(jnp.sum(jnp.where(lo, odp, 0.0) if h % 2 == 0 else jnp.where(lo, 0.0, odp), axis=1, keepdims=True))
            c0 = LANES * (h // 2) + 64 * (h % 2)
            lses.append(lse_blk[:, c0:c0 + 1])
        ri = lax.broadcasted_iota(jnp.int32, (bq, bq), 0)
        ci = lax.broadcasted_iota(jnp.int32, (bq, bq), 1)

        def step(kb, carry, masked):
            off = pl.multiple_of(kb * bq, bq)
            ks = _mla_per_head_k(k_ref[pl.ds(off, bq), :])
            vs = _per_head(v_ref[pl.ds(off, bq), :])
            ss = [_dot(qs[h], ks[h], "nt") for h in range(4)]
            dps = [_dot(dos[h], vs[h], "nt") for h in range(4)]
            prbs, dss = [], []
            for h in range(4):
                s = ss[h]
                if masked:
                    s = jnp.where(ci <= ri, s, NEG)
                pr = jnp.exp(s - lses[h])
                prbs.append(pr.astype(BF16))
                dss.append((pr * (dps[h] - delta[h])).astype(BF16))
            dv_acc = [_dot(prbs[2 * p], dos[2 * p], "tn") + _dot(prbs[2 * p + 1], dos[2 * p + 1], "tn") for p in range(2)]
            dqs = [_dot(dss[h], ks[h]) for h in range(4)]
            dk_acc = [_dot(dss[2 * p], qs[2 * p], "tn") + _dot(dss[2 * p + 1], qs[2 * p + 1], "tn") for p in range(2)]
            dk_ref[pl.ds(off, bq), :] += jnp.concatenate(dk_acc, axis=1)
            dv_ref[pl.ds(off, bq), :] += jnp.concatenate(dv_acc, axis=1)
            return tuple(carry[h] + dqs[h] for h in range(4))

        zero = jnp.zeros((bq, 2 * LANES), F32)
        carry = step(qi, (zero,) * 4, True)
        def several(t, c):
            for r in range(KEY_UNROLL):
                c = step(qi - 1 - r - KEY_UNROLL * t, c, False)
            return c

        carry = lax.fori_loop(0, qi // KEY_UNROLL, several, carry)
        rem = qi % KEY_UNROLL
        carry = lax.fori_loop(0, rem, lambda t, c: step(rem - 1 - t, c, False), carry)
        dq_ref[...] = jnp.concatenate([jnp.where(masks[0], carry[0], 0.0) + jnp.where(masks[1], carry[1], 0.0),
                                       jnp.where(masks[2], carry[2], 0.0) + jnp.where(masks[3], carry[3], 0.0)], axis=1)

    blk = pl.BlockSpec((bq, 2 * LANES), lambda i: (i, 0))
    wide = pl.BlockSpec((bq, 4 * LANES), lambda i: (i, 0))
    return pl.pallas_call(
        body, name=name, grid=(nq,),
        in_specs=[wide, _resident((lp, 4 * LANES), 0), _resident((lp, 2 * LANES), 0), blk, blk, blk],
        out_specs=[wide, _resident((lp, 4 * LANES), 0), _resident((lp, 2 * LANES), 0)],
        out_shape=[jax.ShapeDtypeStruct((lp, 4 * LANES), F32), jax.ShapeDtypeStruct((lp, 4 * LANES), F32),
                   jax.ShapeDtypeStruct((lp, 2 * LANES), F32)],
        compiler_params=_cp("arbitrary"),
    )(qc, kc, v, o, lse, d_o)


def _mix_out(y_pre, z, o_sb, o_mla, g_ssd, g_sb, g_mla):
    return jnp.concatenate([_rms(y_pre * _silu(z), g_ssd), _rms(o_sb, g_sb), _rms(o_mla, g_mla)], axis=1)


def _ffn_act(up_a, up_b, halo_a, halo_b, w_a, w_b, b_a, b_b, i):
    ca = conv_fwd(up_a, halo_a, w_a, i) + b_a
    cb_ = conv_fwd(up_b, halo_b, w_b, i) + b_b
    return ca, cb_


def layer_fwd(h, w, cs, sn, l):
    lp = h.shape[0]
    nm = f"l{l}_"
    hn = rowwise(lambda i, n, x, g: _rms(x, g), [RI(h), PA(w["norm_mix_g"])], [RO(D_MODEL, BF16)], nm + "rms_mix", lp)[0]
    u = mm(hn, w["w_main"], "nn", F32, nm + "in_main")
    qkv = mm(hn, w["w_sb"], "nn", BF16, nm + "in_sb")
    xbc_c = rowwise(lambda i, n, x, hl, cw, cb_: _silu(conv_fwd(x, hl, cw, i) + cb_),
                    [RI(u, SSD_XBC, 0), HP(u, SSD_XBC, 0), PA(w["ssd_conv_w"]), PA(w["ssd_conv_b"])],
                    [RO(SSD_XBC, F32)], nm + "ssd_conv", lp)[0]
    y_pre, hprev = ssd_fwd(xbc_c, u, w["dt_bias"], w["a_log"], w["d_skip"], nm + "ssd_fwd")
    o_sb, rs_sb = sb_attn_fwd(qkv, nm + "sb_fwd")
    qn, kvn = rowwise(lambda i, n, qa, ckv, gq, gkv: (_rms(qa, gq, MLA_Q_RANK), _rms(ckv, gkv)),
                      [RI(u, 256, U_QA // 256), RI(u, LANES, U_CKV // LANES), PA(w["q_norm_g"]), PA(w["kv_norm_g"])],
                      [RO(256, BF16), RO(LANES, BF16)], nm + "mla_rms", lp)
    qf = mm(qn, w["w_uq"], "nn", F32, nm + "mla_uq")
    kvf = mm(kvn, w["w_ukv"], "nn", F32, nm + "mla_ukv")

    def pack(i, n, qf_, kvf_, kr4, cos, sin):
        qf_ = qf_ * MLA_SCALE
        qr = qf_[:, 256:384]
        qr = qr * cos + rope_rot(qr) * sin
        kr = kr4 * cos + rope_rot(kr4) * sin
        qc = jnp.concatenate([qf_[:, 0:128], qr, qf_[:, 128:256], qr], axis=1)
        kc = jnp.concatenate([kvf_[:, 0:128], kr, kvf_[:, 128:256], kr], axis=1)
        return qc, kc, kvf_[:, 256:512]

    qc, kc, vv = rowwise(pack, [RI(qf), RI(kvf), RI(u, LANES, U_KR4 // LANES), RI(cs), RI(sn)],
                         [RO(512, BF16), RO(512, BF16), RO(256, BF16)], nm + "mla_pack", lp)
    o_mla, lse = mla_attn_fwd(qc, kc, vv, nm + "mla_fwd")
    cat = rowwise(lambda i, n, *a: _mix_out(*a),
                  [RI(y_pre), RI(u, SSD_WIDTH, U_Z // SSD_WIDTH), RI(o_sb), RI(o_mla),
                   PA(w["ssd_norm_g"]), PA(w["sb_norm_g"]), PA(w["mla_norm_g"])],
                  [RO(D_MODEL, BF16)], nm + "mix_out", lp)[0]
    h_mid = mm(cat, w["w_out"], "nn", F32, nm + "out_proj", add=h)
    hn2 = rowwise(lambda i, n, x, g: _rms(x, g), [RI(h_mid), PA(w["norm_ffn_g"])], [RO(D_MODEL, BF16)], nm + "rms_ffn", lp)[0]
    up_a = mm(hn2, w["w_up_a"], "nn", F32, nm + "up_a")
    up_b = mm(hn2, w["w_up_b"], "nn", F32, nm + "up_b")
    wc = 1408

    def act(i, n, ua, ub, ha, hb_, wa, wb, ba, bb_):
        ca, cb_ = _ffn_act(ua, ub, ha, hb_, wa, wb, ba, bb_, i)
        return _silu(ca) * cb_

    a_t = rowwise(act, [RI(up_a, wc, 0, True), RI(up_b, wc, 0, True), HP(up_a, wc, 0, True), HP(up_b, wc, 0, True),
                        PA(w["ffn_conv_w_a"], wc, 0, True), PA(w["ffn_conv_w_b"], wc, 0, True),
                        PA(w["ffn_conv_b_a"], wc, 0, True), PA(w["ffn_conv_b_b"], wc, 0, True)],
                  [RO(D_FF, BF16, wc, True)], nm + "ffn_act", lp, ncol=D_FF // wc)[0]
    h_out = mm(a_t, w["w_down"], "nn", F32, nm + "down", add=h_mid)
    saved = dict(h=h, hn=hn, u=u, qkv=qkv, xbc_c=xbc_c, y_pre=y_pre, hprev=hprev, o_sb=o_sb, rs_sb=rs_sb, qn=qn, kvn=kvn,
                 qc=qc, kc=kc, vv=vv, o_mla=o_mla, lse=lse, cat=cat, h_mid=h_mid, hn2=hn2, up_a=up_a, up_b=up_b, a_t=a_t)
    return h_out, saved


def layer_bwd(dh_out, w, s, cs, sn, l):
    lp = dh_out.shape[0]
    nm = f"l{l}b_"
    g = {}
    wc = 1408
    ncolf = D_FF // wc
    g["w_down"] = mm(s["a_t"], dh_out, "tn", BF16, nm + "dw_down")
    d_act = mm(dh_out, w["w_down"], "nt", F32, nm + "d_act")

    def act_bwd(i, n, ua, ub, ha, hb_, wa, wb, ba, bb_, da_):
        ca, cb_ = _ffn_act(ua, ub, ha, hb_, wa, wb, ba, bb_, i)
        sg = jax.nn.sigmoid(ca)
        dca = da_ * cb_ * (sg * (1.0 + ca * (1.0 - sg)))
        dcb = da_ * (ca * sg)
        return (dca, dcb, conv_bwd_w(dca, ua, ha, i, FFN_CONV), conv_bwd_w(dcb, ub, hb_, i, FFN_CONV),
                jnp.sum(dca, axis=0, keepdims=True), jnp.sum(dcb, axis=0, keepdims=True))

    dca, dcb, g["ffn_conv_w_a"], g["ffn_conv_w_b"], g["ffn_conv_b_a"], g["ffn_conv_b_b"] = rowwise(
        act_bwd, [RI(s["up_a"], wc, 0, True), RI(s["up_b"], wc, 0, True), HP(s["up_a"], wc, 0, True),
                  HP(s["up_b"], wc, 0, True), PA(w["ffn_conv_w_a"], wc, 0, True), PA(w["ffn_conv_w_b"], wc, 0, True),
                  PA(w["ffn_conv_b_a"], wc, 0, True), PA(w["ffn_conv_b_b"], wc, 0, True), RI(d_act, wc, 0, True)],
        [RO(D_FF, F32, wc, True), RO(D_FF, F32, wc, True), AO(FFN_CONV, D_FF, wc, True), AO(FFN_CONV, D_FF, wc, True),
         AO(1, D_FF, wc, True), AO(1, D_FF, wc, True)], nm + "ffn_act_bwd", lp, ncol=ncolf)

    def conv_t(i, n, da_, db_, ha, hb_, wa, wb):
        return conv_bwd_data(da_, ha, wa, i, n), conv_bwd_data(db_, hb_, wb, i, n)

    dup_a, dup_b = rowwise(conv_t, [RI(dca, wc, 0, True), RI(dcb, wc, 0, True), HN(dca, wc, 0, True), HN(dcb, wc, 0, True),
                                    PA(w["ffn_conv_w_a"], wc, 0, True), PA(w["ffn_conv_w_b"], wc, 0, True)],
                           [RO(D_FF, BF16, wc, True), RO(D_FF, BF16, wc, True)], nm + "ffn_conv_t", lp, ncol=ncolf)
    g["w_up_a"] = mm(s["hn2"], dup_a, "tn", BF16, nm + "dw_up_a")
    g["w_up_b"] = mm(s["hn2"], dup_b, "tn", BF16, nm + "dw_up_b")
    dhn2 = mm(dup_a, w["w_up_a"], "nt", F32, nm + "dhn2_a")
    dhn2 = mm(dup_b, w["w_up_b"], "nt", F32, nm + "dhn2_b", add=dhn2)

    def rms_bwd(i, n, x, gg, dy, dres):
        _, vjp = jax.vjp(_rms, x, gg)
        dx, dg = vjp(dy)
        return dres + dx, dg

    dh_mid, g["norm_ffn_g"] = rowwise(rms_bwd, [RI(s["h_mid"]), PA(w["norm_ffn_g"]), RI(dhn2), RI(dh_out)],
                                      [RO(D_MODEL, F32), AO(1, D_MODEL)], nm + "rms_ffn_bwd", lp)
    g["w_out"] = mm(s["cat"], dh_mid, "tn", BF16, nm + "dw_out")
    d_cat = mm(dh_mid, w["w_out"], "nt", F32, nm + "d_cat")
    u = s["u"]

    def mix_bwd(i, n, y_pre, z, o_sb, o_mla, g1, g2, g3, dcat):
        _, vjp = jax.vjp(_mix_out, y_pre, z, o_sb, o_mla, g1, g2, g3)
        return vjp(dcat)

    dy_pre, dz, do_sb, do_mla, g["ssd_norm_g"], g["sb_norm_g"], g["mla_norm_g"] = rowwise(
        mix_bwd, [RI(s["y_pre"]), RI(u, SSD_WIDTH, U_Z // SSD_WIDTH), RI(s["o_sb"]), RI(s["o_mla"]),
                  PA(w["ssd_norm_g"]), PA(w["sb_norm_g"]), PA(w["mla_norm_g"]), RI(d_cat)],
        [RO(SSD_WIDTH, F32), RO(SSD_WIDTH, BF16), RO(SB_WIDTH, F32), RO(256, F32),
         AO(1, SSD_WIDTH), AO(1, SB_WIDTH), AO(1, 256)], nm + "mix_out_bwd", lp)
    dxs, dbp, dcp, ddtp, pg = ssd_bwd(s["xbc_c"], u, w["dt_bias"], w["a_log"], w["d_skip"], s["hprev"], dy_pre, nm + "ssd_bwd")
    pg = pg.reshape(2, SUBLANES, LANES).sum(axis=0)
    g["dt_bias"], g["a_log"], g["d_skip"] = pg[0:1], pg[1:2], pg[2:3]

    def conv4_bwd(i, n, x, hl, cw, cb_, dxs_, dbp_, dcp_, ddtp_):
        pre = conv_fwd(x, hl, cw, i) + cb_
        d_out = jnp.concatenate([dxs_, dbp_, dcp_], axis=1)
        sg = jax.nn.sigmoid(pre)
        d_pre = d_out * (sg * (1.0 + pre * (1.0 - sg)))
        ddt = ddtp_[:, 0:128] + ddtp_[:, 128:256]
        return d_pre, ddt, conv_bwd_w(d_pre, x, hl, i, SSD_CONV), jnp.sum(d_pre, axis=0, keepdims=True)

    d_pre, ddt, g["ssd_conv_w"], g["ssd_conv_b"] = rowwise(
        conv4_bwd, [RI(u, SSD_XBC, 0), HP(u, SSD_XBC, 0), PA(w["ssd_conv_w"]), PA(w["ssd_conv_b"]),
                    RI(dxs), RI(dbp), RI(dcp), RI(ddtp)],
        [RO(SSD_XBC, F32), RO(LANES, BF16), AO(SSD_CONV, SSD_XBC), AO(1, SSD_XBC)], nm + "ssd_conv_bwd", lp)
    d_xbc = rowwise(lambda i, n, d, hn_, cw: conv_bwd_data(d, hn_, cw, i, n),
                    [RI(d_pre), HN(d_pre), PA(w["ssd_conv_w"])], [RO(SSD_XBC, BF16)], nm + "ssd_conv_t", lp)[0]
    dq_sb, dk_sb, dv_sb = sb_attn_bwd(s["qkv"], s["rs_sb"], do_sb, nm + "sb_bwd")
    dqkv = jnp.concatenate([dq_sb, dk_sb.astype(BF16), dv_sb.astype(BF16)], axis=1)
    dqc, dkc, dvv = mla_attn_bwd(s["qc"], s["kc"], s["vv"], s["o_mla"], s["lse"], do_mla, nm + "mla_bwd")

    def unpack(i, n, dqc_, dkc_, dvv_, cos, sin):
        dqr = dqc_[:, 128:256] + dqc_[:, 384:512]
        dqr = dqr * cos + rope_rot_t(dqr * sin)
        dkr = dkc_[:, 128:256] + dkc_[:, 384:512]
        dkr = dkr * cos + rope_rot_t(dkr * sin)
        dq = jnp.concatenate([dqc_[:, 0:128], dqc_[:, 256:384], dqr], axis=1) * MLA_SCALE
        dkv = jnp.concatenate([dkc_[:, 0:128], dkc_[:, 256:384], dvv_], axis=1)
        return dq, dkv, dkr

    dq, dkv, dkr4 = rowwise(unpack, [RI(dqc), RI(dkc), RI(dvv), RI(cs), RI(sn)],
                            [RO(384, BF16), RO(512, BF16), RO(LANES, BF16)], nm + "mla_unpack", lp)
    g["w_uq"] = mm(s["qn"], dq, "tn", F32, nm + "dw_uq")
    g["w_ukv"] = mm(s["kvn"], dkv, "tn", F32, nm + "dw_ukv")
    dqn = mm(dq, w["w_uq"], "nt", F32, nm + "dqn")
    dkvn = mm(dkv, w["w_ukv"], "nt", F32, nm + "dkvn")

    def mla_rms_bwd(i, n, qa, ckv, gq, gkv, dqn_, dkvn_):
        _, vjp = jax.vjp(lambda a, b, c, d: (_rms(a, c, MLA_Q_RANK), _rms(b, d)), qa, ckv, gq, gkv)
        return vjp((dqn_, dkvn_))

    dqa, dckv, g["q_norm_g"], g["kv_norm_g"] = rowwise(
        mla_rms_bwd, [RI(u, 256, U_QA // 256), RI(u, LANES, U_CKV // LANES), PA(w["q_norm_g"]), PA(w["kv_norm_g"]),
                      RI(dqn), RI(dkvn)],
        [RO(256, BF16), RO(LANES, BF16), AO(1, 256), AO(1, LANES)], nm + "mla_rms_bwd", lp)
    du = jnp.concatenate([d_xbc, dz, dqa, dckv, dkr4, ddt, jnp.zeros((lp, LANES), BF16)], axis=1)
    g["w_main"] = mm(s["hn"], du, "tn", F32, nm + "dw_main")
    g["w_sb"] = mm(s["hn"], dqkv, "tn", F32, nm + "dw_sb")
    dhn = mm(du, w["w_main"], "nt", F32, nm + "dhn_main")
    dhn = mm(dqkv, w["w_sb"], "nt", F32, nm + "dhn_sb", add=dhn)
    dh_in, g["norm_mix_g"] = rowwise(rms_bwd, [RI(s["h"]), PA(w["norm_mix_g"]), RI(dhn), RI(dh_mid)],
                                     [RO(D_MODEL, F32), AO(1, D_MODEL)], nm + "rms_mix_bwd", lp)
    return dh_in, g


_IN_CUTS = np.cumsum((512, 1024, 8, 256, 256, 256, 192, 128, 32))


def _pad_cols(a, n):
    return jnp.pad(a, ((0, 0), (0, n - a.shape[1])))


def prep_layer_weights(full, l):
    w_in = full["w_in"][l]
    c = _IN_CUTS
    z, xbc, dtr = w_in[:, :c[0]], w_in[:, c[0]:c[1]], w_in[:, c[1]:c[2]]
    q_sb, k_sb, v_sb = w_in[:, c[2]:c[3]], w_in[:, c[3]:c[4]], w_in[:, c[4]:c[5]]
    q_a, c_kv, k_r = w_in[:, c[5]:c[6]], w_in[:, c[6]:c[7]], w_in[:, c[7]:c[8]]
    w_main = jnp.concatenate([xbc, z, _pad_cols(q_a, 256), c_kv, k_r, k_r, k_r, k_r, _pad_cols(dtr, 256)], axis=1)
    assert w_main.shape[1] == U_MAIN
    row = lambda v, n=None: _pad_cols(v.reshape(1, -1).astype(F32), v.size if n is None else n)
    uq = full["mla_w_uq"][l].reshape(MLA_Q_RANK, 4, 96)
    w_uq = jnp.concatenate([uq[:, :, :64].reshape(MLA_Q_RANK, 256), uq[:, :, 64:].reshape(MLA_Q_RANK, 128)], axis=1)
    w_uq = jnp.pad(w_uq, ((0, 256 - MLA_Q_RANK), (0, 0)))
    ukv = full["mla_w_ukv"][l].reshape(MLA_KV_RANK, 4, 128)
    w_ukv = jnp.concatenate([ukv[:, :, :64].reshape(MLA_KV_RANK, 256), ukv[:, :, 64:].reshape(MLA_KV_RANK, 256)], axis=1)
    return dict(
        norm_mix_g=row(full["norm_mix_g"][l]), w_main=w_main, w_sb=jnp.concatenate([q_sb * SB_SCALE, k_sb, v_sb], axis=1),
        ssd_conv_w=full["ssd_conv_w"][l], ssd_conv_b=row(full["ssd_conv_b"][l]),
        dt_bias=row(full["ssd_dt_bias"][l], LANES), a_log=row(full["ssd_a_log"][l], LANES), d_skip=row(full["ssd_d"][l], LANES),
        ssd_norm_g=row(full["ssd_norm_g"][l]), sb_norm_g=row(full["sb_norm_g"][l]),
        q_norm_g=row(full["mla_q_norm_g"][l], 256), kv_norm_g=row(full["mla_kv_norm_g"][l]),
        w_uq=w_uq, w_ukv=w_ukv, mla_norm_g=row(full["mla_norm_g"][l]),
        w_out=full["w_out"][l], norm_ffn_g=row(full["norm_ffn_g"][l]),
        w_up_a=full["ffn_w_up"][l][:, :D_FF], w_up_b=full["ffn_w_up"][l][:, D_FF:],
        ffn_conv_w_a=full["ffn_conv_w"][l][:, :D_FF], ffn_conv_w_b=full["ffn_conv_w"][l][:, D_FF:],
        ffn_conv_b_a=row(full["ffn_conv_b"][l][:D_FF]), ffn_conv_b_b=row(full["ffn_conv_b"][l][D_FF:]),
        w_down=full["ffn_w_down"][l],
    )


def unprep_layer_grads(g):
    wm = g["w_main"]
    xbc, z = wm[:, U_XBC:U_XBC + 1024], wm[:, U_Z:U_Z + 512]
    q_a, c_kv = wm[:, U_QA:U_QA + MLA_Q_RANK], wm[:, U_CKV:U_CKV + 128]
    k_r = (wm[:, U_KR4:U_KR4 + 32] + wm[:, U_KR4 + 32:U_KR4 + 64] + wm[:, U_KR4 + 64:U_KR4 + 96] + wm[:, U_KR4 + 96:U_KR4 + 128])
    dtr = wm[:, U_DT:U_DT + SSD_HEADS]
    w_sb = g["w_sb"]
    w_in = jnp.concatenate([z, xbc, dtr, w_sb[:, :SB_WIDTH] * SB_SCALE, w_sb[:, SB_WIDTH:], q_a, c_kv, k_r], axis=1)
    guq = g["w_uq"][:MLA_Q_RANK]
    guq = jnp.concatenate([guq[:, :256].reshape(MLA_Q_RANK, 4, 64), guq[:, 256:].reshape(MLA_Q_RANK, 4, 32)], axis=2)
    gukv = g["w_ukv"]
    gukv = jnp.concatenate([gukv[:, :256].reshape(MLA_KV_RANK, 4, 64), gukv[:, 256:].reshape(MLA_KV_RANK, 4, 64)], axis=2)
    return dict(
        norm_mix_g=g["norm_mix_g"][0], w_in=w_in, ssd_conv_w=g["ssd_conv_w"], ssd_conv_b=g["ssd_conv_b"][0],
        ssd_dt_bias=g["dt_bias"][0, :SSD_HEADS], ssd_a_log=g["a_log"][0, :SSD_HEADS], ssd_d=g["d_skip"][0, :SSD_HEADS],
        ssd_norm_g=g["ssd_norm_g"][0], sb_norm_g=g["sb_norm_g"][0], mla_q_norm_g=g["q_norm_g"][0, :MLA_Q_RANK],
        mla_kv_norm_g=g["kv_norm_g"][0], mla_w_uq=guq.reshape(MLA_Q_RANK, 384), mla_w_ukv=gukv.reshape(MLA_KV_RANK, 512),
        mla_norm_g=g["mla_norm_g"][0], w_out=g["w_out"], norm_ffn_g=g["norm_ffn_g"][0],
        ffn_w_up=jnp.concatenate([g["w_up_a"], g["w_up_b"]], axis=1),
        ffn_conv_w=jnp.concatenate([g["ffn_conv_w_a"], g["ffn_conv_w_b"]], axis=1),
        ffn_conv_b=jnp.concatenate([g["ffn_conv_b_a"][0], g["ffn_conv_b_b"][0]], axis=0),
        ffn_w_down=g["w_down"],
    )


def rope_tables(lp):
    pos = jnp.arange(lp, dtype=F32)
    inv = 1.0 / (ROPE_BASE ** (jnp.arange(0, MLA_ROPE, 2, dtype=F32) / MLA_ROPE))
    ang = pos[:, None] * inv[None, :]
    ang = jnp.concatenate([ang, ang] * 4, axis=-1)
    return jnp.cos(ang), jnp.sin(ang)


def local_step(x_seq, target, full):
    seq = x_seq.shape[0]
    length = seq + N_META
    lp = -(-length // ROW_TILE) * ROW_TILE
    cs, sn = rope_tables(lp)
    h = jnp.concatenate([full["meta_tokens"].astype(F32), x_seq, jnp.zeros((lp - length, D_MODEL), F32)], axis=0)
    tgt = jnp.pad(target, ((N_META, lp - length), (0, 0)))
    ws = [prep_layer_weights(full, l) for l in range(DEPTH)]
    saved = []
    for l in range(DEPTH):
        h, s = layer_fwd(h, ws[l], cs, sn, l)
        saved.append(s)
    fg = full["final_norm_g"].reshape(1, D_MODEL).astype(F32)
    tm = min(ROW_TILE, lp)

    def loss_fn(i, n, x, g, t):
        rows = _rows_iota(x) + i * tm
        valid = jnp.logical_and(rows >= N_META, rows < length)

        def f(x_, g_):
            err = jnp.where(valid, _rms(x_, g_) - t, 0.0)
            return 0.5 * jnp.sum(err * err) * (1.0 / D_MODEL)

        val, (dx, dg) = jax.value_and_grad(f, argnums=(0, 1))(x, g)
        return dx, jnp.full((1, LANES), val, F32), dg

    dh, loss_row, g_final = rowwise(loss_fn, [RI(h), PA(fg), RI(tgt)], [RO(D_MODEL, F32), AO(1, LANES), AO(1, D_MODEL)],
                                    "loss_head", lp)
    grads = {}
    per_layer = [None] * DEPTH
    for l in reversed(range(DEPTH)):
        dh, g = layer_bwd(dh, ws[l], saved[l], cs, sn, l)
        per_layer[l] = unprep_layer_grads(g)
    for k in per_layer[0]:
        grads[k] = jnp.stack([per_layer[l][k] for l in range(DEPTH)], axis=0)
    grads["final_norm_g"] = g_final[0]
    grads["meta_tokens"] = dh[:N_META]
    return loss_row[0, 0], dh[N_META:length], grads


_ANY = pl.BlockSpec(memory_space=pl.ANY)


def chip_exchange(srcs, modes, name):
    n = len(srcs)
    flips = ((1, 0), (0, 1), (1, 1))

    def body(*refs):
        ins, outs = refs[:n], refs[n:2 * n]
        send_sems, recv_sems, fwd_send_sems, fwd_recv_sems, loc_sems = refs[2 * n:]
        x, y, c = lax.axis_index("x"), lax.axis_index("y"), lax.axis_index("c")
        me = 2 * x + y
        waits, forwards = [], []
        for a in range(n):
            whole = modes[a] != "slab"
            cp = pltpu.make_async_copy(ins[a] if whole else ins[a].at[me], outs[a].at[me], loc_sems.at[a])
            cp.start()
            waits.append(cp.wait)
            half = ins[a].shape[0] // 2 if modes[a] == "bcast_split" else None
            for k, (fx, fy) in enumerate(flips):
                px = 1 - x if fx else x
                py = 1 - y if fy else y
                peer = 2 * px + py
                if half is None:
                    src = ins[a] if whole else ins[a].at[peer]
                    dst = outs[a].at[me]
                else:
                    src = ins[a].at[pl.ds(c * half, half)]
                    dst = outs[a].at[me, pl.ds(c * half, half)]
                rc = pltpu.make_async_remote_copy(src_ref=src, dst_ref=dst, send_sem=send_sems.at[a, k],
                                                  recv_sem=recv_sems.at[a, k], device_id=(px, py, c), device_id_type=MESH_ID)
                rc.start()
                if half is None:
                    waits.append(rc.wait)
                else:
                    waits.append(rc.wait_send)
                    landed = outs[a].at[peer, pl.ds(c * half, half)]
                    fw = pltpu.make_async_remote_copy(src_ref=landed, dst_ref=landed, send_sem=fwd_send_sems.at[a, k],
                                                      recv_sem=fwd_recv_sems.at[a, k], device_id=(x, y, 1 - c),
                                                      device_id_type=MESH_ID)
                    forwards.append((rc, fw))
        for rc, fw in forwards:
            rc.wait_recv()
            fw.start()
        for rc, fw in forwards:
            fw.wait()
        for w in waits:
            w()

    out_shape = [jax.ShapeDtypeStruct((N_CHIPS,) + (s.shape if m != "slab" else s.shape[1:]), s.dtype) for s, m in zip(srcs, modes)]
    return pl.pallas_call(
        body, name=name, in_specs=[_ANY] * n, out_specs=[_ANY] * n, out_shape=out_shape,
        scratch_shapes=[pltpu.SemaphoreType.DMA((n, 3)), pltpu.SemaphoreType.DMA((n, 3)), pltpu.SemaphoreType.DMA((n, 3)),
                        pltpu.SemaphoreType.DMA((n, 3)), pltpu.SemaphoreType.DMA((n,))],
    )(*srcs)


def _piece(ref, mode, k):
    if mode == "slab":
        return ref.at[k]
    if mode == "rows":
        rs = ref.shape[1] // N_CHIPS
        return ref.at[:, pl.ds(pl.multiple_of(k * rs, 16), rs), :]
    if mode == "cols":
        cs = ref.shape[2] // N_CHIPS
        return ref.at[:, :, pl.ds(pl.multiple_of(k * cs, LANES), cs)]
    return ref


def _piece_shape(shape, mode):
    if mode == "slab":
        return shape[1:]
    if mode == "rows":
        return (shape[0], shape[1] // N_CHIPS, shape[2])
    if mode == "cols":
        return (shape[0], shape[1], shape[2] // N_CHIPS)
    return shape


def grad_exchange(srcs, modes, name):
    n = len(srcs)
    flips = ((1, 0), (0, 1), (1, 1))

    def body(*refs):
        ins, outs = refs[:n], refs[n:2 * n]
        send_sems, recv_sems, fwd_send_sems, fwd_recv_sems, sib_send_sems, sib_recv_sems, loc_sems = refs[2 * n:]
        x, y, c = lax.axis_index("x"), lax.axis_index("y"), lax.axis_index("c")
        me = 2 * x + y
        sibling = (x, y, 1 - c)
        waits, forwards = [], []
        for a in range(n):
            mine = _piece(ins[a], modes[a], me)
            slot = outs[a].at[4 * c + me]
            cp = pltpu.make_async_copy(mine, slot, loc_sems.at[a])
            cp.start()
            sb = pltpu.make_async_remote_copy(src_ref=mine, dst_ref=slot, send_sem=sib_send_sems.at[a],
                                              recv_sem=sib_recv_sems.at[a], device_id=sibling, device_id_type=MESH_ID)
            sb.start()
            waits += [cp.wait, sb.wait]
            for k, (fx, fy) in enumerate(flips):
                px = 1 - x if fx else x
                py = 1 - y if fy else y
                peer = 2 * px + py
                rc = pltpu.make_async_remote_copy(src_ref=_piece(ins[a], modes[a], peer), dst_ref=slot,
                                                  send_sem=send_sems.at[a, k], recv_sem=recv_sems.at[a, k],
                                                  device_id=(px, py, c), device_id_type=MESH_ID)
                rc.start()
                landed = outs[a].at[4 * c + peer]
                fw = pltpu.make_async_remote_copy(src_ref=landed, dst_ref=landed, send_sem=fwd_send_sems.at[a, k],
                                                  recv_sem=fwd_recv_sems.at[a, k], device_id=sibling, device_id_type=MESH_ID)
                waits.append(rc.wait_send)
                forwards.append((rc, fw))
        for rc, fw in forwards:
            rc.wait_recv()
            fw.start()
        for rc, fw in forwards:
            fw.wait()
        for w in waits:
            w()

    out_shape = [jax.ShapeDtypeStruct((2 * N_CHIPS,) + tuple(_piece_shape(s.shape, m)), s.dtype) for s, m in zip(srcs, modes)]
    dma = pltpu.SemaphoreType.DMA
    return pl.pallas_call(
        body, name=name, in_specs=[_ANY] * n, out_specs=[_ANY] * n, out_shape=out_shape,
        scratch_shapes=[dma((n, 3)), dma((n, 3)), dma((n, 3)), dma((n, 3)), dma((n,)), dma((n,)), dma((n,))],
    )(*srcs)


WEIGHT_NAMES = ("meta_tokens", "norm_mix_g", "w_in", "ssd_conv_w", "ssd_conv_b", "ssd_dt_bias", "ssd_a_log", "ssd_d",
                "ssd_norm_g", "sb_norm_g", "mla_q_norm_g", "mla_kv_norm_g", "mla_w_uq", "mla_w_ukv", "mla_norm_g",
                "w_out", "norm_ffn_g", "ffn_w_up", "ffn_conv_w", "ffn_conv_b", "ffn_w_down", "final_norm_g")
SHARD_AXIS = {"meta_tokens": 1, "w_in": 2, "ssd_conv_w": 2, "mla_w_uq": 2, "mla_w_ukv": 2, "w_out": 1, "ffn_w_up": 2,
              "ffn_conv_w": 2, "ffn_w_down": 1}
SHARDED = tuple(n for n in WEIGHT_NAMES if n in SHARD_AXIS)
REPLICATED = tuple(n for n in WEIGHT_NAMES if n not in SHARD_AXIS)
GATHER_BF16 = ("w_in", "mla_w_uq", "mla_w_ukv", "w_out", "ffn_w_up", "ffn_w_down")
GATHER_F32 = ("meta_tokens", "ssd_conv_w", "ffn_conv_w")
PACK_ROWS = ROW_TILE


def pack(arrs, dtype):
    flat = jnp.concatenate([a.reshape(-1).astype(dtype) for a in arrs])
    per = PACK_ROWS * PACK_W
    total = -(-flat.size // per) * per
    return jnp.pad(flat, (0, total - flat.size)).reshape(total // PACK_W, PACK_W)


def unpack(buf, shapes):
    flat = buf.reshape(-1)
    out, off = [], 0
    for shp in shapes:
        size = int(np.prod(shp))
        out.append(flat[off:off + size].reshape(shp))
        off += size
    return out


def gather_weights(a):
    full = {n: a[n] for n in REPLICATED}
    bufs = [pack([a[n] for n in GATHER_BF16], BF16), pack([a[n] for n in GATHER_F32], F32)]
    got = chip_exchange(bufs, ("bcast_split", "bcast"), "gather_weights")
    for names, g in ((GATHER_BF16, got[0]), (GATHER_F32, got[1])):
        pieces = [unpack(g[k], [a[n].shape for n in names]) for k in range(N_CHIPS)]
        for idx, n in enumerate(names):
            full[n] = jnp.concatenate([pieces[k][idx] for k in range(N_CHIPS)], axis=SHARD_AXIS[n])
    return full


BIG = ("w_in", "w_out", "ffn_w_up", "ffn_w_down")
BIG_MODE = {"w_in": "slab", "w_out": "rows", "ffn_w_up": "cols", "ffn_w_down": "rows"}
SMALL_SHARDED = tuple(n for n in SHARDED if n not in BIG)
ADAM_TILE = 128


def _adamw(i, n, *vals):
    parts, (w, m, v) = vals[:2 * N_CHIPS], vals[2 * N_CHIPS:]
    g = parts[0].astype(F32)
    for p in parts[1:]:
        g = g + p.astype(F32)
    m = ADAM_B1 * m + (1.0 - ADAM_B1) * g
    v = ADAM_B2 * v + (1.0 - ADAM_B2) * jnp.square(g)
    m_hat = m / (1.0 - ADAM_B1 ** ADAM_STEP)
    v_hat = v / (1.0 - ADAM_B2 ** ADAM_STEP)
    delta = -ADAM_LR * (m_hat / (jnp.sqrt(v_hat) + ADAM_EPS) + ADAM_WD * w)
    return g, delta, m, v


def _adamw_call(got, w, m, v, name):
    rows, width = w.shape
    flat = got.reshape(2 * N_CHIPS * rows, width)
    blk = rows // ADAM_TILE
    ins = [RI(flat, rblk=k * blk) for k in range(2 * N_CHIPS)] + [RI(w), RI(m), RI(v)]
    return rowwise(_adamw, ins, [RO(width, F32)] * 4, name, rows, tm=ADAM_TILE)


def reduce_and_update(a, grads):
    srcs, modes = [], []
    for n in BIG:
        g = grads[n].astype(BF16)
        if n == "w_in":
            cs = a[n].shape[2]
            g = g.reshape(DEPTH, D_MODEL, N_CHIPS, cs).transpose(2, 0, 1, 3)
        srcs.append(g)
        modes.append(BIG_MODE[n])
    slabs = []
    for k in range(N_CHIPS):
        parts = []
        for n in SMALL_SHARDED:
            ax = SHARD_AXIS[n]
            size = a[n].shape[ax]
            parts.append(lax.slice_in_dim(grads[n], k * size, (k + 1) * size, axis=ax))
        slabs.append(pack(parts, BF16))
    srcs += [jnp.stack(slabs, axis=0), pack([grads[n] for n in REPLICATED], F32)]
    modes += ["slab", "bcast"]
    got = grad_exchange(srcs, modes, "exchange_grads")
    outs = {}
    kinds = ("grad", "delta", "new_m", "new_v")
    for n, g8 in zip(BIG, got):
        shp = a[n].shape
        rows = shp[0] * shp[1]
        flat = lambda t: t.reshape(rows, shp[2])
        res = _adamw_call(g8.reshape(2 * N_CHIPS, rows, shp[2]), flat(a[n]), flat(a["m_" + n]), flat(a["v_" + n]), "adamw_" + n)
        for kind, val in zip(kinds, res):
            outs[(kind, n)] = val.reshape(shp)
    for tag, names, g8 in (("small", SMALL_SHARDED, got[len(BIG)]), ("rep", REPLICATED, got[len(BIG) + 1])):
        shapes = [a[n].shape for n in names]
        packed = [pack([a[pre + n] for n in names], F32) for pre in ("", "m_", "v_")]
        res = _adamw_call(g8, *packed, "adamw_" + tag)
        for kind, buf in zip(kinds, res):
            for n, val in zip(names, unpack(buf, shapes)):
                outs[(kind, n)] = val
    return outs


INPUT_NAMES = ("x",) + WEIGHT_NAMES + ("loss_target",) + tuple("m_" + n for n in WEIGHT_NAMES) + tuple("v_" + n for n in WEIGHT_NAMES)


def kernel(x, meta_tokens, norm_mix_g, w_in, ssd_conv_w, ssd_conv_b, ssd_dt_bias, ssd_a_log, ssd_d, ssd_norm_g, sb_norm_g, mla_q_norm_g, mla_kv_norm_g, mla_w_uq, mla_w_ukv, mla_norm_g, w_out, norm_ffn_g, ffn_w_up, ffn_conv_w, ffn_conv_b, ffn_w_down, final_norm_g, loss_target, m_meta_tokens, m_norm_mix_g, m_w_in, m_ssd_conv_w, m_ssd_conv_b, m_ssd_dt_bias, m_ssd_a_log, m_ssd_d, m_ssd_norm_g, m_sb_norm_g, m_mla_q_norm_g, m_mla_kv_norm_g, m_mla_w_uq, m_mla_w_ukv, m_mla_norm_g, m_w_out, m_norm_ffn_g, m_ffn_w_up, m_ffn_conv_w, m_ffn_conv_b, m_ffn_w_down, m_final_norm_g, v_meta_tokens, v_norm_mix_g, v_w_in, v_ssd_conv_w, v_ssd_conv_b, v_ssd_dt_bias, v_ssd_a_log, v_ssd_d, v_ssd_norm_g, v_sb_norm_g, v_mla_q_norm_g, v_mla_kv_norm_g, v_mla_w_uq, v_mla_w_ukv, v_mla_norm_g, v_w_out, v_norm_ffn_g, v_ffn_w_up, v_ffn_conv_w, v_ffn_conv_b, v_ffn_w_down, v_final_norm_g):
    args = (x, meta_tokens, norm_mix_g, w_in, ssd_conv_w, ssd_conv_b, ssd_dt_bias, ssd_a_log, ssd_d, ssd_norm_g, sb_norm_g, mla_q_norm_g, mla_kv_norm_g, mla_w_uq, mla_w_ukv, mla_norm_g, w_out, norm_ffn_g, ffn_w_up, ffn_conv_w, ffn_conv_b, ffn_w_down, final_norm_g, loss_target, m_meta_tokens, m_norm_mix_g, m_w_in, m_ssd_conv_w, m_ssd_conv_b, m_ssd_dt_bias, m_ssd_a_log, m_ssd_d, m_ssd_norm_g, m_sb_norm_g, m_mla_q_norm_g, m_mla_kv_norm_g, m_mla_w_uq, m_mla_w_ukv, m_mla_norm_g, m_w_out, m_norm_ffn_g, m_ffn_w_up, m_ffn_conv_w, m_ffn_conv_b, m_ffn_w_down, m_final_norm_g, v_meta_tokens, v_norm_mix_g, v_w_in, v_ssd_conv_w, v_ssd_conv_b, v_ssd_dt_bias, v_ssd_a_log, v_ssd_d, v_ssd_norm_g, v_sb_norm_g, v_mla_q_norm_g, v_mla_kv_norm_g, v_mla_w_uq, v_mla_w_ukv, v_mla_norm_g, v_w_out, v_norm_ffn_g, v_ffn_w_up, v_ffn_conv_w, v_ffn_conv_b, v_ffn_w_down, v_final_norm_g)
    a = dict(zip(INPUT_NAMES, args, strict=True))
    full = gather_weights(a)
    loss, grad_x, grads = local_step(a["x"][0], a["loss_target"][0], full)
    loss = lax.psum(loss, ("x", "y", "c"))
    outs = reduce_and_update(a, grads)
    result = [loss, grad_x[None]]
    for kind in ("grad", "delta", "new_m", "new_v"):
        result += [outs[(kind, n)] for n in WEIGHT_NAMES]
    return tuple(result)
```

```python
import functools
import math

import numpy as np
import jax
import jax.numpy as jnp
from jax import lax
from jax.experimental import pallas as pl
from jax.experimental.pallas import tpu as pltpu

F32 = jnp.float32
BF16 = jnp.bfloat16
HIGHEST = lax.Precision.HIGHEST
MESH_ID = pl.DeviceIdType.MESH

D_MODEL = 1024
DEPTH = 2
N_META = 16
EPS = 1e-6
SSD_HEADS = 8
SSD_WIDTH = 512
SSD_XBC = 1024
SSD_CONV = 4
SB_WIDTH = 256
SB_SCALE = 64 ** -0.5
MLA_Q_RANK = 192
MLA_KV_RANK = 128
MLA_ROPE = 32
MLA_SCALE = 96 ** -0.5
ROPE_BASE = 10000.0
D_FF = 2816
FFN_CONV = 3
IN_COLS = 2664
N_CHIPS = 4

ADAM_LR = 0.001
ADAM_B1 = 0.9
ADAM_B2 = 0.999
ADAM_EPS = 1e-08
ADAM_WD = 0.01
ADAM_STEP = 10

LANES = 128
SUBLANES = 8
ROW_TILE = 256
KEY_UNROLL = 4
VMEM_LIMIT = 56 * 1024 * 1024
PACK_W = 1024

U_XBC, U_Z, U_QA, U_CKV, U_KR4, U_DT, U_MAIN = 0, 1024, 1536, 1792, 1920, 2048, 2304
NEG = -1e30


def _cp(*sem):
    return pltpu.CompilerParams(dimension_semantics=sem if sem else None, vmem_limit_bytes=VMEM_LIMIT)


def _pick(dim, pref):
    if dim <= pref:
        return dim
    best = None
    for t in range(LANES, pref + 1, LANES):
        if dim % t == 0:
            best = t
    assert best is not None, (dim, pref)
    return best


def _dot(a, b, dims="nn", precision=None):
    dn = {"nn": (((1,), (0,)), ((), ())), "nt": (((1,), (1,)), ((), ())), "tn": (((0,), (0,)), ((), ()))}[dims]
    return lax.dot_general(a, b, dn, preferred_element_type=F32, precision=precision)


def _softplus(x):
    return jnp.maximum(x, 0.0) + jnp.log1p(jnp.exp(-jnp.abs(x)))


def _silu(x):
    return x * jax.nn.sigmoid(x)


def _rms(x, g, n=None):
    n = x.shape[-1] if n is None else n
    ms = jnp.sum(x * x, axis=-1, keepdims=True) * (1.0 / n)
    return x * lax.rsqrt(ms + EPS) * g


def mm(a, b, dims, out_dtype, name, add=None, tm=None, tn=None, tk=None):
    if dims == "nn":
        (m, k), (k2, n) = a.shape, b.shape
    elif dims == "nt":
        (m, k), (n, k2) = a.shape, b.shape
    else:
        (k, m), (k2, n) = a.shape, b.shape
    assert k == k2, (a.shape, b.shape, dims)
    if dims == "tn":
        tm, tn, tk = _pick(m, tm or 1408), _pick(n, tn or 1408), _pick(k, tk or 1408)
    else:
        tm, tn, tk = _pick(m, tm or 768), _pick(n, tn or 1408), _pick(k, tk or 2816)
    nk = k // tk
    if dims == "tn":
        a_spec = pl.BlockSpec((tk, tm), lambda j, i, kk: (kk, i))
    else:
        a_spec = pl.BlockSpec((tm, tk), lambda j, i, kk: (i, kk))
    if dims == "nt":
        b_spec = pl.BlockSpec((tn, tk), lambda j, i, kk: (j, kk))
    else:
        b_spec = pl.BlockSpec((tk, tn), lambda j, i, kk: (kk, j))
    o_spec = pl.BlockSpec((tm, tn), lambda j, i, kk: (i, j))
    has_add = add is not None

    def body(*refs):
        a_ref, b_ref = refs[0], refs[1]
        add_ref = refs[2] if has_add else None
        o_ref = refs[3] if has_add else refs[2]
        part = _dot(a_ref[...].astype(BF16), b_ref[...].astype(BF16), dims)

        def finish(r):
            if has_add:
                r = r + add_ref[...].astype(F32)
            o_ref[...] = r.astype(o_ref.dtype)

        if nk == 1:
            finish(part)
            return
        acc_ref = refs[-1]
        kk = pl.program_id(2)

        @pl.when(kk == 0)
        def _():
            acc_ref[...] = part

        @pl.when(jnp.logical_and(kk > 0, kk < nk - 1))
        def _():
            acc_ref[...] += part

        @pl.when(kk == nk - 1)
        def _():
            finish(acc_ref[...] + part)

    in_specs = [a_spec, b_spec] + ([o_spec] if has_add else [])
    args = (a, b) + ((add,) if has_add else ())
    return pl.pallas_call(
        body, name=name, grid=(n // tn, m // tm, nk),
        in_specs=in_specs, out_specs=o_spec,
        out_shape=jax.ShapeDtypeStruct((m, n), out_dtype),
        scratch_shapes=[pltpu.VMEM((tm, tn), F32)] if nk > 1 else [],
        compiler_params=_cp("parallel", "parallel", "arbitrary"),
    )(*args)


def RI(arr, width=None, cidx=0, cv=False, rblk=0):
    return ("row" if rblk == 0 else ("row", rblk), arr, arr.shape[1] if width is None else width, cidx, cv)


def HP(arr, width=None, cidx=0, cv=False):
    return ("prev", arr, arr.shape[1] if width is None else width, cidx, cv)


def HN(arr, width=None, cidx=0, cv=False):
    return ("next", arr, arr.shape[1] if width is None else width, cidx, cv)


def PA(arr, width=None, cidx=0, cv=False):
    return ("par", arr, arr.shape[1] if width is None else width, cidx, cv)


def RO(ncols, dtype, width=None, cv=False):
    return ("row", ncols, dtype, ncols if width is None else width, cv)


def AO(nrows, ncols, width=None, cv=False):
    return ("acc", (nrows, ncols), F32, ncols if width is None else width, cv)


def rowwise(fn, ins, outs, name, rows, tm=ROW_TILE, ncol=1):
    tm = min(tm, rows)
    assert rows % tm == 0
    nrow = rows // tm
    hb = tm // SUBLANES
    last_hb = rows // SUBLANES - 1
    in_specs, args = [], []
    for kind, arr, width, cidx, cv in ins:
        def cmap(j, cidx=cidx, cv=cv):
            return cidx + j if cv else cidx
        if kind == "row":
            spec = pl.BlockSpec((tm, width), lambda j, i, cmap=cmap: (i, cmap(j)))
        elif isinstance(kind, tuple):
            spec = pl.BlockSpec((tm, width), lambda j, i, cmap=cmap, rblk=kind[1]: (i + rblk, cmap(j)))
        elif kind == "prev":
            spec = pl.BlockSpec((SUBLANES, width), lambda j, i, cmap=cmap: (jnp.maximum(i * hb - 1, 0), cmap(j)))
        elif kind == "next":
            spec = pl.BlockSpec((SUBLANES, width), lambda j, i, cmap=cmap: (jnp.minimum((i + 1) * hb, last_hb), cmap(j)))
        else:
            spec = pl.BlockSpec((arr.shape[0], width), lambda j, i, cmap=cmap: (0, cmap(j)))
        in_specs.append(spec)
        args.append(arr)
    out_specs, out_shapes, acc_cv = [], [], []
    for kind, shp, dtype, width, cv in outs:
        if kind == "row":
            out_specs.append(pl.BlockSpec((tm, width), lambda j, i, cv=cv: (i, j if cv else 0)))
            out_shapes.append(jax.ShapeDtypeStruct((rows, shp), dtype))
            acc_cv.append(None)
        else:
            out_specs.append(pl.BlockSpec((shp[0], width), lambda j, i, cv=cv: (0, j if cv else 0)))
            out_shapes.append(jax.ShapeDtypeStruct(shp, dtype))
            acc_cv.append(cv)
    n_in = len(ins)

    def body(*refs):
        j = pl.program_id(0)
        i = pl.program_id(1)
        vals = fn(i, nrow, *[r[...] for r in refs[:n_in]])
        if not isinstance(vals, (tuple, list)):
            vals = (vals,)
        for o_ref, v, cv in zip(refs[n_in:], vals, acc_cv):
            if cv is None:
                o_ref[...] = v.astype(o_ref.dtype)
            else:
                first = (i == 0) if cv else jnp.logical_and(i == 0, j == 0)

                @pl.when(first)
                def _(o_ref=o_ref, v=v):
                    o_ref[...] = v.astype(o_ref.dtype)

                @pl.when(jnp.logical_not(first))
                def _(o_ref=o_ref, v=v):
                    o_ref[...] += v.astype(o_ref.dtype)

    res = pl.pallas_call(
        body, name=name, grid=(ncol, nrow), in_specs=in_specs, out_specs=out_specs, out_shape=out_shapes,
        compiler_params=_cp("arbitrary", "arbitrary"),
    )(*args)
    return res


def _rows_iota(x):
    return lax.broadcasted_iota(jnp.int32, x.shape, 0)


def shift_down(x, halo, s):
    if s == 0:
        return x
    tm = x.shape[0]
    top = pltpu.roll(halo, s, 0)
    if tm > SUBLANES:
        top = jnp.concatenate([top, jnp.zeros((tm - SUBLANES, x.shape[1]), x.dtype)], axis=0)
    return jnp.where(_rows_iota(x) < s, top, pltpu.roll(x, s, 0))


def shift_up(x, halo, s):
    if s == 0:
        return x
    tm = x.shape[0]
    bot = pltpu.roll(halo, SUBLANES - s, 0)
    if tm > SUBLANES:
        bot = jnp.concatenate([jnp.zeros((tm - SUBLANES, x.shape[1]), x.dtype), bot], axis=0)
    return jnp.where(_rows_iota(x) >= tm - s, bot, pltpu.roll(x, tm - s, 0))


def conv_fwd(x, halo, w, i):
    kw = w.shape[0]
    halo = jnp.where(i == 0, 0.0, halo)
    out = None
    for k in range(kw):
        term = w[k:k + 1, :] * shift_down(x, halo, kw - 1 - k)
        out = term if out is None else out + term
    return out


def conv_bwd_data(dy, halo_next, w, i, n):
    kw = w.shape[0]
    halo_next = jnp.where(i == n - 1, 0.0, halo_next)
    out = None
    for k in range(kw):
        term = w[k:k + 1, :] * shift_up(dy, halo_next, kw - 1 - k)
        out = term if out is None else out + term
    return out


def conv_bwd_w(dy, x, halo, i, kw):
    halo = jnp.where(i == 0, 0.0, halo)
    rows = [jnp.sum(dy * shift_down(x, halo, kw - 1 - k), axis=0, keepdims=True) for k in range(kw)]
    return jnp.concatenate(rows, axis=0)


def _lane(shape):
    return lax.broadcasted_iota(jnp.int32, shape, 1)


def rope_rot(x):
    lane = _lane(x.shape) % MLA_ROPE
    return jnp.where(lane < MLA_ROPE // 2, -pltpu.roll(x, LANES - MLA_ROPE // 2, 1), pltpu.roll(x, MLA_ROPE // 2, 1))


def rope_rot_t(g):
    lane = _lane(g.shape) % MLA_ROPE
    return jnp.where(lane < MLA_ROPE // 2, pltpu.roll(g, LANES - MLA_ROPE // 2, 1), -pltpu.roll(g, MLA_ROPE // 2, 1))


def _ssd_common(g, xs, dt_raw, bias, alog, q):
    lane = _lane((q, LANES))
    pre = dt_raw + bias
    dt = jnp.where(lane < SSD_HEADS, _softplus(pre), 0.0)
    a_row = -jnp.exp(alog)
    d_a = dt * a_row
    ri = lax.broadcasted_iota(jnp.int32, (q, q), 0)
    ci = lax.broadcasted_iota(jnp.int32, (q, q), 1)
    causal = ri >= ci
    acs = _dot(causal.astype(F32), d_a, "nn", HIGHEST)
    acs_t = acs.T
    subl = lax.broadcasted_iota(jnp.int32, (LANES, q), 0)
    heads = [4 * g + i for i in range(4)]
    lo = lane < 64

    def col(arr, h):
        return jnp.sum(jnp.where(lane == h, arr, 0.0), axis=1, keepdims=True)

    def lanes4(v):
        m = lo if v[0].shape[0] == q else lo[0:1, :]
        return jnp.concatenate([jnp.where(m, v[0], v[1]), jnp.where(m, v[2], v[3])], axis=1)

    cols = [col(acs, h) for h in heads]
    rows = [jnp.sum(jnp.where(subl == h, acs_t, 0.0), axis=0, keepdims=True) for h in heads]
    tots = [c_[q - 1:q, :] for c_ in cols]
    acs4 = lanes4(cols)
    dt4 = lanes4([col(dt, h) for h in heads])
    lms = [jnp.exp(jnp.where(causal, cols[i] - rows[i], NEG)) for i in range(4)]
    lane4 = _lane((q, 2 * LANES))
    hm = [jnp.logical_and(lane4 >= 64 * i, lane4 < 64 * (i + 1)) for i in range(4)]
    return dict(lane=lane, lo=lo, pre=pre, dt=dt, a_row=a_row, heads=heads, tots=tots, lms=lms, ri=ri, ci=ci, hm=hm,
                lanes4=lanes4, eacs=jnp.exp(acs4), dte=jnp.exp(lanes4(tots) - acs4), dt4=dt4, x=xs * dt4)


def _pick_lane(row_arr, h):
    return jnp.sum(jnp.where(_lane(row_arr.shape) == h, row_arr, 0.0), axis=1, keepdims=True)


def _etot(tots):
    sub = lax.broadcasted_iota(jnp.int32, (2 * LANES, LANES), 0)
    e = [jnp.exp(t) for t in tots]
    return jnp.where(sub < 64, e[0], jnp.where(sub < 128, e[1], jnp.where(sub < 192, e[2], e[3]))), e


def _half(arr, i, lo):
    slab = arr[:, LANES * (i // 2):LANES * (i // 2 + 1)]
    return jnp.where(lo, slab, 0.0) if i % 2 == 0 else jnp.where(lo, 0.0, slab)


def ssd_fwd(xbc_c, u_main, bias_row, alog_row, d_row, name):
    lp = xbc_c.shape[0]
    q = min(ROW_TILE, lp)
    nc = lp // q
    dt_blk = U_DT // LANES

    def body(xs_ref, b_ref, c_ref, dt_ref, bias_ref, alog_ref, d_ref, y_ref, hp_ref, h_scr):
        g = pl.program_id(0)
        c = pl.program_id(1)

        @pl.when(c == 0)
        def _():
            h_scr[...] = jnp.zeros_like(h_scr)

        xs = xs_ref[...]
        bb = b_ref[...].astype(BF16)
        cb_ = c_ref[...].astype(BF16)
        s = _ssd_common(g, xs, dt_ref[...], bias_ref[...], alog_ref[...], q)
        gmat = _dot(cb_, bb, "nt")
        ms = [(gmat * s["lms"][i]).astype(BF16) for i in range(4)]
        xjs = [_half(s["x"], i, s["lo"]).astype(BF16) for i in range(4)]
        ys = [_dot(ms[i], xjs[i]) for i in range(4)]
        hp = h_scr[...]
        hp_ref[...] = hp
        yoff = _dot(cb_, hp.astype(BF16), "nt") * s["eacs"]
        d4 = s["lanes4"]([_pick_lane(d_ref[...], h) for h in s["heads"]])
        y_ref[...] = jnp.concatenate([ys[0] + ys[1], ys[2] + ys[3]], axis=1) + yoff + d4 * xs
        etot, _ = _etot(s["tots"])
        h_scr[...] = hp * etot + _dot((s["x"] * s["dte"]).astype(BF16), bb, "tn")

    in_specs = [
        pl.BlockSpec((q, 2 * LANES), lambda g, c: (c, g)),
        pl.BlockSpec((q, LANES), lambda g, c: (c, 4 + g)),
        pl.BlockSpec((q, LANES), lambda g, c: (c, 6 + g)),
        pl.BlockSpec((q, LANES), lambda g, c: (c, dt_blk)),
        pl.BlockSpec((1, LANES), lambda g, c: (0, 0)),
        pl.BlockSpec((1, LANES), lambda g, c: (0, 0)),
        pl.BlockSpec((1, LANES), lambda g, c: (0, 0)),
    ]
    out_specs = [
        pl.BlockSpec((q, 2 * LANES), lambda g, c: (c, g)),
        pl.BlockSpec((None, None, 2 * LANES, LANES), lambda g, c: (g, c, 0, 0)),
    ]
    return pl.pallas_call(
        body, name=name, grid=(2, nc), in_specs=in_specs, out_specs=out_specs,
        out_shape=[jax.ShapeDtypeStruct((lp, SSD_WIDTH), F32), jax.ShapeDtypeStruct((2, nc, 2 * LANES, LANES), F32)],
        scratch_shapes=[pltpu.VMEM((2 * LANES, LANES), F32)],
        compiler_params=_cp("arbitrary", "arbitrary"),
    )(xbc_c, xbc_c, xbc_c, u_main, bias_row, alog_row, d_row)


def ssd_bwd(xbc_c, u_main, bias_row, alog_row, d_row, hprev, dy, name):
    lp = xbc_c.shape[0]
    q = min(ROW_TILE, lp)
    nc = lp // q
    dt_blk = U_DT // LANES

    def body(xs_ref, b_ref, c_ref, dt_ref, bias_ref, alog_ref, d_ref, hp_ref, dy_ref,
             dxs_ref, db_ref, dc_ref, ddt_ref, pg_ref, dh_scr):
        g = pl.program_id(0)
        cc = pl.program_id(1)

        @pl.when(cc == 0)
        def _():
            dh_scr[...] = jnp.zeros_like(dh_scr)
            pg_ref[...] = jnp.zeros_like(pg_ref)

        xs = xs_ref[...]
        bb = b_ref[...].astype(BF16)
        cb_ = c_ref[...].astype(BF16)
        s = _ssd_common(g, xs, dt_ref[...], bias_ref[...], alog_ref[...], q)
        lane, lo, x, hm, heads = s["lane"], s["lo"], s["x"], s["hm"], s["heads"]
        d_y = dy_ref[...]
        hp = hp_ref[...]
        hpb = hp.astype(BF16)
        dhn = dh_scr[...]
        dhnb = dhn.astype(BF16)
        xd = x * s["dte"]
        gmat = _dot(cb_, bb, "nt")
        m32s = [gmat * s["lms"][i] for i in range(4)]
        xjs = [_half(x, i, lo).astype(BF16) for i in range(4)]
        dyjs = [_half(d_y, i, lo).astype(BF16) for i in range(4)]
        dxparts = [_dot(m32s[i].astype(BF16), dyjs[i], "tn") for i in range(4)]
        dms = [_dot(dyjs[i], xjs[i], "nt") for i in range(4)]
        dg = dms[0] * s["lms"][0] + dms[1] * s["lms"][1] + dms[2] * s["lms"][2] + dms[3] * s["lms"][3]
        wms = [dms[i] * m32s[i] for i in range(4)]
        row_part = [jnp.sum(wm, axis=1, keepdims=True) for wm in wms]
        col_part = [jnp.sum(wm, axis=0, keepdims=True) for wm in wms]
        dgb = dg.astype(BF16)
        yoff = _dot(cb_, hpb, "nt") * s["eacs"]
        d_t = (d_y * s["eacs"]).astype(BF16)
        d_c = _dot(dgb, bb) + _dot(d_t, hpb)
        d_hp = _dot(d_t, cb_, "tn")
        dxd = _dot(bb, dhnb, "nt")
        d_b = _dot(dgb, cb_, "tn") + _dot(xd.astype(BF16), dhnb)
        d_x = jnp.concatenate([dxparts[0] + dxparts[1], dxparts[2] + dxparts[3]], axis=1) + dxd * s["dte"]
        r = dxd * xd
        a_terms = d_y * yoff - r

        def hsum(arr):
            return [jnp.sum(jnp.where(hm[i], arr, 0.0), axis=1, keepdims=True) for i in range(4)]

        dacs, rs = hsum(a_terms), hsum(r)
        hh = dhn * hp
        sub = lax.broadcasted_iota(jnp.int32, (2 * LANES, LANES), 0)
        hsums = [jnp.sum(jnp.where(jnp.logical_and(sub >= 64 * i, sub < 64 * (i + 1)), hh, 0.0), keepdims=True) for i in range(4)]
        last = lax.broadcasted_iota(jnp.int32, (q, 1), 0) == q - 1
        etot, etots = _etot(s["tots"])
        ddacs = jnp.zeros((q, LANES), F32)
        for i, h in enumerate(heads):
            dtot = jnp.sum(rs[i], keepdims=True) + hsums[i] * etots[i]
            ddacs = ddacs + jnp.where(lane == h, dacs[i] + row_part[i] + jnp.where(last, dtot, 0.0), 0.0)
        subl = lax.broadcasted_iota(jnp.int32, (LANES, q), 0)
        cols_t = jnp.zeros((LANES, q), F32)
        for i, h in enumerate(heads):
            cols_t = cols_t + jnp.where(subl == h, col_part[i], 0.0)
        ddacs = ddacs - cols_t.T
        anti = (s["ri"] <= s["ci"]).astype(F32)
        da = _dot(anti, ddacs, "nn", HIGHEST)
        ddt_own = hsum(d_x * xs)
        ddt = da * s["a_row"]
        for i, h in enumerate(heads):
            ddt = ddt + jnp.where(lane == h, ddt_own[i], 0.0)
        draw = ddt * jax.nn.sigmoid(s["pre"])
        ddt_ref[...] = draw
        d4 = s["lanes4"]([_pick_lane(d_ref[...], h) for h in heads])
        dxs_ref[...] = d4 * d_y + d_x * s["dt4"]
        db_ref[...] = d_b
        dc_ref[...] = d_c
        dds = hsum(d_y * xs)
        lane1 = lane[0:1, :]
        dd_row = jnp.zeros((1, LANES), F32)
        for i, h in enumerate(heads):
            dd_row = dd_row + jnp.where(lane1 == h, jnp.sum(dds[i], keepdims=True), 0.0)
        dbias_row = jnp.sum(draw, axis=0, keepdims=True)
        dalog_row = jnp.sum(da * s["dt"], axis=0, keepdims=True) * s["a_row"]
        sub8 = lax.broadcasted_iota(jnp.int32, (SUBLANES, LANES), 0)
        pg_ref[...] += (jnp.where(sub8 == 0, dbias_row, 0.0) + jnp.where(sub8 == 1, dalog_row, 0.0)
                        + jnp.where(sub8 == 2, dd_row, 0.0))
        dh_scr[...] = d_hp + etot * dhn

    rc = lambda c: nc - 1 - c
    in_specs = [
        pl.BlockSpec((q, 2 * LANES), lambda g, c: (rc(c), g)),
        pl.BlockSpec((q, LANES), lambda g, c: (rc(c), 4 + g)),
        pl.BlockSpec((q, LANES), lambda g, c: (rc(c), 6 + g)),
        pl.BlockSpec((q, LANES), lambda g, c: (rc(c), dt_blk)),
        pl.BlockSpec((1, LANES), lambda g, c: (0, 0)),
        pl.BlockSpec((1, LANES), lambda g, c: (0, 0)),
        pl.BlockSpec((1, LANES), lambda g, c: (0, 0)),
        pl.BlockSpec((None, None, 2 * LANES, LANES), lambda g, c: (g, rc(c), 0, 0)),
        pl.BlockSpec((q, 2 * LANES), lambda g, c: (rc(c), g)),
    ]
    out_specs = [
        pl.BlockSpec((q, 2 * LANES), lambda g, c: (rc(c), g)),
        pl.BlockSpec((q, LANES), lambda g, c: (rc(c), g)),
        pl.BlockSpec((q, LANES), lambda g, c: (rc(c), g)),
        pl.BlockSpec((q, LANES), lambda g, c: (rc(c), g)),
        pl.BlockSpec((SUBLANES, LANES), lambda g, c: (g, 0)),
    ]
    per_group = jax.ShapeDtypeStruct((lp, 2 * LANES), F32)
    return pl.pallas_call(
        body, name=name, grid=(2, nc), in_specs=in_specs, out_specs=out_specs,
        out_shape=[jax.ShapeDtypeStruct((lp, SSD_WIDTH), F32), per_group, per_group, per_group,
                   jax.ShapeDtypeStruct((2 * SUBLANES, LANES), F32)],
        scratch_shapes=[pltpu.VMEM((2 * LANES, LANES), F32)],
        compiler_params=_cp("arbitrary", "arbitrary"),
    )(xbc_c, xbc_c, xbc_c, u_main, bias_row, alog_row, d_row, hprev, dy)


def _sb_blocks(qs, ks, r_runs, masked, bq, after_scores=None):
    ri = lax.broadcasted_iota(jnp.int32, (bq, bq), 0)
    ci = lax.broadcasted_iota(jnp.int32, (bq, bq), 1)
    tri_after = (ri > ci).astype(BF16)
    zs = [_dot(qj, kj, "nt") for qj, kj in zip(qs, ks)]
    extra = after_scores() if after_scores is not None else None
    sigs, ubs = [], []
    for z in zs:
        zb = z.astype(BF16)
        u = -(jnp.maximum(zb, 0) + jnp.log(1 + jnp.exp(-jnp.abs(zb))))
        sigs.append(jnp.exp(zb + u))
        if masked:
            u = jnp.where(ci < ri, u, jnp.zeros_like(u))
        ubs.append(u)
    afters = [_dot(ub, tri_after) for ub in ubs]
    usums = [after[:, 0:1] + ub[:, 0:1].astype(F32) for after, ub in zip(afters, ubs)]
    ws = []
    for sig, after, r_run in zip(sigs, afters, r_runs):
        w = sig * jnp.exp(after + r_run).astype(BF16)
        if masked:
            w = jnp.where(ci < ri, w, jnp.zeros_like(w))
        ws.append(w)
    return usums, sigs, ws, extra


def _split_heads(x, lo):
    out = []
    zero = jnp.zeros((x.shape[0], LANES), x.dtype)
    for p in range(2):
        xp = x[:, LANES * p:LANES * (p + 1)]
        out += [jnp.where(lo, xp, zero), jnp.where(lo, zero, xp)]
    return out


def _per_head(x):
    return [x[:, :LANES], x[:, :LANES], x[:, LANES:], x[:, LANES:]]


def _resident(shape, col):
    return pl.BlockSpec(shape, lambda i: (0, col), pipeline_mode=pl.Buffered(1))


def sb_attn_fwd(qkv, name):
    lp = qkv.shape[0]
    bq = min(ROW_TILE, lp)
    nq = lp // bq
    assert nq <= 64

    def body(q_ref, k_ref, v_ref, o_ref, rs_ref):
        qi = pl.program_id(0)
        lane = _lane((bq, LANES))
        lo = lane < 64
        qs = _split_heads(q_ref[...], lo)

        def step(kb, carry, masked):
            off = pl.multiple_of(kb * bq, bq)
            ks = _per_head(k_ref[pl.ds(off, bq), :])
            vs = _per_head(v_ref[pl.ds(off, bq), :])
            heads, rss = carry
            r_runs = [heads[h][1] for h in range(4)]
            rss = list(rss)
            for h in range(4):
                rss[h // 2] = jnp.where(lane == 64 * (h % 2) + kb, r_runs[h], rss[h // 2])
            usums, _, ws, _ = _sb_blocks(qs, ks, r_runs, masked, bq)
            pvs = [_dot(ws[h], vs[h]) for h in range(4)]
            out = tuple((heads[h][0] + pvs[h], r_runs[h] + usums[h]) for h in range(4))
            return out, tuple(rss)

        zero = (jnp.zeros((bq, LANES), F32), jnp.zeros((bq, 1), F32))
        zr = jnp.zeros((bq, LANES), F32)
        carry = step(qi, ((zero,) * 4, (zr, zr)), True)
        def several(t, c):
            for r in range(KEY_UNROLL):
                c = step(qi - 1 - r - KEY_UNROLL * t, c, False)
            return c

        carry = lax.fori_loop(0, qi // KEY_UNROLL, several, carry)
        rem = qi % KEY_UNROLL
        heads, rss = lax.fori_loop(0, rem, lambda t, c: step(rem - 1 - t, c, False), carry)
        o_ref[...] = jnp.concatenate([jnp.where(lo, heads[0][0], heads[1][0]), jnp.where(lo, heads[2][0], heads[3][0])], axis=1)
        rs_ref[...] = jnp.concatenate(list(rss), axis=1)

    blk = pl.BlockSpec((bq, 2 * LANES), lambda i: (i, 0))
    return pl.pallas_call(
        body, name=name, grid=(nq,),
        in_specs=[blk, _resident((lp, 2 * LANES), 1), _resident((lp, 2 * LANES), 2)],
        out_specs=[blk, blk],
        out_shape=[jax.ShapeDtypeStruct((lp, SB_WIDTH), F32), jax.ShapeDtypeStruct((lp, SB_WIDTH), F32)],
        compiler_params=_cp("arbitrary"),
    )(qkv, qkv, qkv)


def sb_attn_bwd(qkv, rs, d_o, name):
    lp = qkv.shape[0]
    bq = min(ROW_TILE, lp)
    nq = lp // bq

    def body(q_ref, k_ref, v_ref, rs_ref, do_ref, dq_ref, dk_ref, dv_ref):
        qi = pl.program_id(0)

        @pl.when(qi == 0)
        def _():
            dk_ref[...] = jnp.zeros_like(dk_ref)
            dv_ref[...] = jnp.zeros_like(dv_ref)

        lane = _lane((bq, LANES))
        lo = lane < 64
        qs = _split_heads(q_ref[...], lo)
        dos = _split_heads(do_ref[...].astype(BF16), lo)
        rs_blk = rs_ref[...]
        ri = lax.broadcasted_iota(jnp.int32, (bq, bq), 0)
        ci = lax.broadcasted_iota(jnp.int32, (bq, bq), 1)
        tbefore = (ri < ci).astype(BF16)

        def step(kb, carry, masked):
            off = pl.multiple_of(kb * bq, bq)
            ks = _per_head(k_ref[pl.ds(off, bq), :])
            vs = _per_head(v_ref[pl.ds(off, bq), :])
            r_rights = [jnp.sum(jnp.where(lane == 64 * (h % 2) + kb, rs_blk[:, LANES * (h // 2):LANES * (h // 2 + 1)], 0.0),
                                axis=1, keepdims=True) for h in range(4)]
            _, sigs, wbs, dws = _sb_blocks(qs, ks, r_rights, masked, bq,
                                           after_scores=lambda: [_dot(dos[h], vs[h], "nt") for h in range(4)])
            gs = [wbs[h].astype(F32) * dws[h] for h in range(4)]
            gbs = [g.astype(BF16) for g in gs]
            gbefores = [_dot(gb, tbefore) for gb in gbs]
            dv_acc = [_dot(wbs[2 * p], dos[2 * p], "tn") + _dot(wbs[2 * p + 1], dos[2 * p + 1], "tn") for p in range(2)]
            dzbs = []
            for h in range(4):
                dz = gs[h] - sigs[h].astype(F32) * (gs[h] + gbefores[h] + carry[h][1])
                if masked:
                    dz = jnp.where(ci < ri, dz, 0.0)
                dzbs.append(dz.astype(BF16))
            dqs = [_dot(dzbs[h], ks[h]) for h in range(4)]
            dk_acc = [_dot(dzbs[2 * p], qs[2 * p], "tn") + _dot(dzbs[2 * p + 1], qs[2 * p + 1], "tn") for p in range(2)]
            dk_ref[pl.ds(off, bq), :] += jnp.concatenate(dk_acc, axis=1)
            dv_ref[pl.ds(off, bq), :] += jnp.concatenate(dv_acc, axis=1)
            return tuple((carry[h][0] + dqs[h], carry[h][1] + jnp.sum(gs[h], axis=1, keepdims=True)) for h in range(4))

        zero = (jnp.zeros((bq, LANES), F32), jnp.zeros((bq, 1), F32))
        def several(t, c):
            for r in range(KEY_UNROLL):
                c = step(KEY_UNROLL * t + r, c, False)
            return c

        carry = lax.fori_loop(0, qi // KEY_UNROLL, several, (zero,) * 4)
        carry = lax.fori_loop(qi - qi % KEY_UNROLL, qi, lambda t, c: step(t, c, False), carry)
        carry = step(qi, carry, True)
        dq_ref[...] = jnp.concatenate([jnp.where(lo, carry[0][0], carry[1][0]), jnp.where(lo, carry[2][0], carry[3][0])],
                                      axis=1).astype(dq_ref.dtype)

    blk = pl.BlockSpec((bq, 2 * LANES), lambda i: (i, 0))
    return pl.pallas_call(
        body, name=name, grid=(nq,),
        in_specs=[blk, _resident((lp, 2 * LANES), 1), _resident((lp, 2 * LANES), 2), blk, blk],
        out_specs=[blk, _resident((lp, 2 * LANES), 0), _resident((lp, 2 * LANES), 0)],
        out_shape=[jax.ShapeDtypeStruct((lp, SB_WIDTH), BF16), jax.ShapeDtypeStruct((lp, SB_WIDTH), F32),
                   jax.ShapeDtypeStruct((lp, SB_WIDTH), F32)],
        compiler_params=_cp("arbitrary"),
    )(qkv, qkv, qkv, rs, d_o)


def _mla_masks(bq):
    lane = _lane((bq, 2 * LANES))
    out = []
    for h in range(4):
        j = h % 2
        nope = jnp.logical_and(lane >= 64 * j, lane < 64 * (j + 1))
        rope = jnp.logical_and(lane >= LANES + MLA_ROPE * h, lane < LANES + MLA_ROPE * (h + 1))
        out.append(jnp.logical_or(nope, rope))
    return out


def _mla_split_q(q, masks):
    zero = jnp.zeros((q.shape[0], 2 * LANES), q.dtype)
    return [jnp.where(masks[h], q[:, 2 * LANES * (h // 2):2 * LANES * (h // 2 + 1)], zero) for h in range(4)]


def _mla_per_head_k(k):
    return [k[:, :2 * LANES], k[:, :2 * LANES], k[:, 2 * LANES:], k[:, 2 * LANES:]]


def mla_attn_fwd(qc, kc, v, name):
    lp = qc.shape[0]
    bq = min(ROW_TILE, lp)
    nq = lp // bq

    def body(q_ref, k_ref, v_ref, o_ref, lse_ref):
        qi = pl.program_id(0)
        qs = _mla_split_q(q_ref[...], _mla_masks(bq))
        lo = _lane((bq, LANES)) < 64
        ri = lax.broadcasted_iota(jnp.int32, (bq, bq), 0)
        ci = lax.broadcasted_iota(jnp.int32, (bq, bq), 1)

        def step(kb, carry, masked):
            off = pl.multiple_of(kb * bq, bq)
            ks = _mla_per_head_k(k_ref[pl.ds(off, bq), :])
            vs = _per_head(v_ref[pl.ds(off, bq), :])
            ss = [_dot(qs[h], ks[h], "nt") for h in range(4)]
            prs, alphas, stats = [], [], []
            for h in range(4):
                _, m_run, l_run = carry[h]
                s = ss[h]
                if masked:
                    s = jnp.where(ci <= ri, s, NEG)
                m_new = jnp.maximum(m_run, jnp.max(s, axis=1, keepdims=True))
                alpha = jnp.exp(m_run - m_new)
                pr = jnp.exp(s - m_new)
                prs.append(pr.astype(BF16))
                alphas.append(alpha)
                stats.append((m_new, l_run * alpha + jnp.sum(pr, axis=1, keepdims=True)))
            pvs = [_dot(prs[h], vs[h]) for h in range(4)]
            return tuple((carry[h][0] * alphas[h] + pvs[h],) + stats[h] for h in range(4))

        zero = (jnp.zeros((bq, LANES), F32), jnp.full((bq, 1), NEG, F32), jnp.zeros((bq, 1), F32))
        carry = step(qi, (zero,) * 4, True)
        def several(t, c):
            for r in range(KEY_UNROLL):
                c = step(qi - 1 - r - KEY_UNROLL * t, c, False)
            return c

        carry = lax.fori_loop(0, qi // KEY_UNROLL, several, carry)
        rem = qi % KEY_UNROLL
        carry = lax.fori_loop(0, rem, lambda t, c: step(rem - 1 - t, c, False), carry)
        outs = [a / l for a, _, l in carry]
        lses = [m + jnp.log(l) for _, m, l in carry]
        o_ref[...] = jnp.concatenate([jnp.where(lo, outs[0], outs[1]), jnp.where(lo, outs[2], outs[3])], axis=1)
        lse_ref[...] = jnp.concatenate([jnp.where(lo, lses[0], lses[1]), jnp.where(lo, lses[2], lses[3])], axis=1)

    blk = pl.BlockSpec((bq, 2 * LANES), lambda i: (i, 0))
    return pl.pallas_call(
        body, name=name, grid=(nq,),
        in_specs=[pl.BlockSpec((bq, 4 * LANES), lambda i: (i, 0)), _resident((lp, 4 * LANES), 0), _resident((lp, 2 * LANES), 0)],
        out_specs=[blk, blk],
        out_shape=[jax.ShapeDtypeStruct((lp, 2 * LANES), F32), jax.ShapeDtypeStruct((lp, 2 * LANES), F32)],
        compiler_params=_cp("arbitrary"),
    )(qc, kc, v)


def mla_attn_bwd(qc, kc, v, o, lse, d_o, name):
    lp = qc.shape[0]
    bq = min(ROW_TILE, lp)
    nq = lp // bq

    def body(q_ref, k_ref, v_ref, o_ref, lse_ref, do_ref, dq_ref, dk_ref, dv_ref):
        qi = pl.program_id(0)

        @pl.when(qi == 0)
        def _():
            dk_ref[...] = jnp.zeros_like(dk_ref)
            dv_ref[...] = jnp.zeros_like(dv_ref)

        d_o = do_ref[...]
        masks = _mla_masks(bq)
        qs = _mla_split_q(q_ref[...], masks)
        lo = _lane((bq, LANES)) < 64
        dos = _split_heads(d_o.astype(BF16), lo)
        od = o_ref[...] * d_o
        lse_blk = lse_ref[...]
        delta, lses = [], []
        for h in range(4):
            odp = od[:, LANES * (h // 2):LANES * (h // 2 + 1)]
            delta.append(jnp.sum(jnp.where(lo, odp, 0.0) if h % 2 == 0 else jnp.where(lo, 0.0, odp), axis=1, keepdims=True))
            c0 = LANES * (h // 2) + 64 * (h % 2)
            lses.append(lse_blk[:, c0:c0 + 1])
        ri = lax.broadcasted_iota(jnp.int32, (bq, bq), 0)
        ci = lax.broadcasted_iota(jnp.int32, (bq, bq), 1)

        def step(kb, carry, masked):
            off = pl.multiple_of(kb * bq, bq)
            ks = _mla_per_head_k(k_ref[pl.ds(off, bq), :])
            vs = _per_head(v_ref[pl.ds(off, bq), :])
            ss = [_dot(qs[h], ks[h], "nt") for h in range(4)]
            dps = [_dot(dos[h], vs[h], "nt") for h in range(4)]
            prbs, dss = [], []
            for h in range(4):
                s = ss[h]
                if masked:
                    s = jnp.where(ci <= ri, s, NEG)
                pr = jnp.exp(s - lses[h])
                prbs.append(pr.astype(BF16))
                dss.append((pr * (dps[h] - delta[h])).astype(BF16))
            dv_acc = [_dot(prbs[2 * p], dos[2 * p], "tn") + _dot(prbs[2 * p + 1], dos[2 * p + 1], "tn") for p in range(2)]
            dqs = [_dot(dss[h], ks[h]) for h in range(4)]
            dk_acc = [_dot(dss[2 * p], qs[2 * p], "tn") + _dot(dss[2 * p + 1], qs[2 * p + 1], "tn") for p in range(2)]
            dk_ref[pl.ds(off, bq), :] += jnp.concatenate(dk_acc, axis=1)
            dv_ref[pl.ds(off, bq), :] += jnp.concatenate(dv_acc, axis=1)
            return tuple(carry[h] + dqs[h] for h in range(4))

        zero = jnp.zeros((bq, 2 * LANES), F32)
        carry = step(qi, (zero,) * 4, True)
        def several(t, c):
            for r in range(KEY_UNROLL):
                c = step(qi - 1 - r - KEY_UNROLL * t, c, False)
            return c

        carry = lax.fori_loop(0, qi // KEY_UNROLL, several, carry)
        rem = qi % KEY_UNROLL
        carry = lax.fori_loop(0, rem, lambda t, c: step(rem - 1 - t, c, False), carry)
        dq_ref[...] = jnp.concatenate([jnp.where(masks[0], carry[0], 0.0) + jnp.where(masks[1], carry[1], 0.0),
                                       jnp.where(masks[2], carry[2], 0.0) + jnp.where(masks[3], carry[3], 0.0)], axis=1)

    blk = pl.BlockSpec((bq, 2 * LANES), lambda i: (i, 0))
    wide = pl.BlockSpec((bq, 4 * LANES), lambda i: (i, 0))
    return pl.pallas_call(
        body, name=name, grid=(nq,),
        in_specs=[wide, _resident((lp, 4 * LANES), 0), _resident((lp, 2 * LANES), 0), blk, blk, blk],
        out_specs=[wide, _resident((lp, 4 * LANES), 0), _resident((lp, 2 * LANES), 0)],
        out_shape=[jax.ShapeDtypeStruct((lp, 4 * LANES), F32), jax.ShapeDtypeStruct((lp, 4 * LANES), F32),
                   jax.ShapeDtypeStruct((lp, 2 * LANES), F32)],
        compiler_params=_cp("arbitrary"),
    )(qc, kc, v, o, lse, d_o)


def _mix_out(y_pre, z, o_sb, o_mla, g_ssd, g_sb, g_mla):
    return jnp.concatenate([_rms(y_pre * _silu(z), g_ssd), _rms(o_sb, g_sb), _rms(o_mla, g_mla)], axis=1)


def _ffn_act(up_a, up_b, halo_a, halo_b, w_a, w_b, b_a, b_b, i):
    ca = conv_fwd(up_a, halo_a, w_a, i) + b_a
    cb_ = conv_fwd(up_b, halo_b, w_b, i) + b_b
    return ca, cb_


def layer_fwd(h, w, cs, sn, l):
    lp = h.shape[0]
    nm = f"l{l}_"
    hn = rowwise(lambda i, n, x, g: _rms(x, g), [RI(h), PA(w["norm_mix_g"])], [RO(D_MODEL, BF16)], nm + "rms_mix", lp)[0]
    u = mm(hn, w["w_main"], "nn", F32, nm + "in_main")
    qkv = mm(hn, w["w_sb"], "nn", BF16, nm + "in_sb")
    xbc_c = rowwise(lambda i, n, x, hl, cw, cb_: _silu(conv_fwd(x, hl, cw, i) + cb_),
                    [RI(u, SSD_XBC, 0), HP(u, SSD_XBC, 0), PA(w["ssd_conv_w"]), PA(w["ssd_conv_b"])],
                    [RO(SSD_XBC, F32)], nm + "ssd_conv", lp)[0]
    y_pre, hprev = ssd_fwd(xbc_c, u, w["dt_bias"], w["a_log"], w["d_skip"], nm + "ssd_fwd")
    o_sb, rs_sb = sb_attn_fwd(qkv, nm + "sb_fwd")
    qn, kvn = rowwise(lambda i, n, qa, ckv, gq, gkv: (_rms(qa, gq, MLA_Q_RANK), _rms(ckv, gkv)),
                      [RI(u, 256, U_QA // 256), RI(u, LANES, U_CKV // LANES), PA(w["q_norm_g"]), PA(w["kv_norm_g"])],
                      [RO(256, BF16), RO(LANES, BF16)], nm + "mla_rms", lp)
    qf = mm(qn, w["w_uq"], "nn", F32, nm + "mla_uq")
    kvf = mm(kvn, w["w_ukv"], "nn", F32, nm + "mla_ukv")

    def pack(i, n, qf_, kvf_, kr4, cos, sin):
        qf_ = qf_ * MLA_SCALE
        qr = qf_[:, 256:384]
        qr = qr * cos + rope_rot(qr) * sin
        kr = kr4 * cos + rope_rot(kr4) * sin
        qc = jnp.concatenate([qf_[:, 0:128], qr, qf_[:, 128:256], qr], axis=1)
        kc = jnp.concatenate([kvf_[:, 0:128], kr, kvf_[:, 128:256], kr], axis=1)
        return qc, kc, kvf_[:, 256:512]

    qc, kc, vv = rowwise(pack, [RI(qf), RI(kvf), RI(u, LANES, U_KR4 // LANES), RI(cs), RI(sn)],
                         [RO(512, BF16), RO(512, BF16), RO(256, BF16)], nm + "mla_pack", lp)
    o_mla, lse = mla_attn_fwd(qc, kc, vv, nm + "mla_fwd")
    cat = rowwise(lambda i, n, *a: _mix_out(*a),
                  [RI(y_pre), RI(u, SSD_WIDTH, U_Z // SSD_WIDTH), RI(o_sb), RI(o_mla),
                   PA(w["ssd_norm_g"]), PA(w["sb_norm_g"]), PA(w["mla_norm_g"])],
                  [RO(D_MODEL, BF16)], nm + "mix_out", lp)[0]
    h_mid = mm(cat, w["w_out"], "nn", F32, nm + "out_proj", add=h)
    hn2 = rowwise(lambda i, n, x, g: _rms(x, g), [RI(h_mid), PA(w["norm_ffn_g"])], [RO(D_MODEL, BF16)], nm + "rms_ffn", lp)[0]
    up_a = mm(hn2, w["w_up_a"], "nn", F32, nm + "up_a")
    up_b = mm(hn2, w["w_up_b"], "nn", F32, nm + "up_b")
    wc = 1408

    def act(i, n, ua, ub, ha, hb_, wa, wb, ba, bb_):
        ca, cb_ = _ffn_act(ua, ub, ha, hb_, wa, wb, ba, bb_, i)
        return _silu(ca) * cb_

    a_t = rowwise(act, [RI(up_a, wc, 0, True), RI(up_b, wc, 0, True), HP(up_a, wc, 0, True), HP(up_b, wc, 0, True),
                        PA(w["ffn_conv_w_a"], wc, 0, True), PA(w["ffn_conv_w_b"], wc, 0, True),
                        PA(w["ffn_conv_b_a"], wc, 0, True), PA(w["ffn_conv_b_b"], wc, 0, True)],
                  [RO(D_FF, BF16, wc, True)], nm + "ffn_act", lp, ncol=D_FF // wc)[0]
    h_out = mm(a_t, w["w_down"], "nn", F32, nm + "down", add=h_mid)
    saved = dict(h=h, hn=hn, u=u, qkv=qkv, xbc_c=xbc_c, y_pre=y_pre, hprev=hprev, o_sb=o_sb, rs_sb=rs_sb, qn=qn, kvn=kvn,
                 qc=qc, kc=kc, vv=vv, o_mla=o_mla, lse=lse, cat=cat, h_mid=h_mid, hn2=hn2, up_a=up_a, up_b=up_b, a_t=a_t)
    return h_out, saved


def layer_bwd(dh_out, w, s, cs, sn, l):
    lp = dh_out.shape[0]
    nm = f"l{l}b_"
    g = {}
    wc = 1408
    ncolf = D_FF // wc
    g["w_down"] = mm(s["a_t"], dh_out, "tn", BF16, nm + "dw_down")
    d_act = mm(dh_out, w["w_down"], "nt", F32, nm + "d_act")

    def act_bwd(i, n, ua, ub, ha, hb_, wa, wb, ba, bb_, da_):
        ca, cb_ = _ffn_act(ua, ub, ha, hb_, wa, wb, ba, bb_, i)
        sg = jax.nn.sigmoid(ca)
        dca = da_ * cb_ * (sg * (1.0 + ca * (1.0 - sg)))
        dcb = da_ * (ca * sg)
        return (dca, dcb, conv_bwd_w(dca, ua, ha, i, FFN_CONV), conv_bwd_w(dcb, ub, hb_, i, FFN_CONV),
                jnp.sum(dca, axis=0, keepdims=True), jnp.sum(dcb, axis=0, keepdims=True))

    dca, dcb, g["ffn_conv_w_a"], g["ffn_conv_w_b"], g["ffn_conv_b_a"], g["ffn_conv_b_b"] = rowwise(
        act_bwd, [RI(s["up_a"], wc, 0, True), RI(s["up_b"], wc, 0, True), HP(s["up_a"], wc, 0, True),
                  HP(s["up_b"], wc, 0, True), PA(w["ffn_conv_w_a"], wc, 0, True), PA(w["ffn_conv_w_b"], wc, 0, True),
                  PA(w["ffn_conv_b_a"], wc, 0, True), PA(w["ffn_conv_b_b"], wc, 0, True), RI(d_act, wc, 0, True)],
        [RO(D_FF, F32, wc, True), RO(D_FF, F32, wc, True), AO(FFN_CONV, D_FF, wc, True), AO(FFN_CONV, D_FF, wc, True),
         AO(1, D_FF, wc, True), AO(1, D_FF, wc, True)], nm + "ffn_act_bwd", lp, ncol=ncolf)

    def conv_t(i, n, da_, db_, ha, hb_, wa, wb):
        return conv_bwd_data(da_, ha, wa, i, n), conv_bwd_data(db_, hb_, wb, i, n)

    dup_a, dup_b = rowwise(conv_t, [RI(dca, wc, 0, True), RI(dcb, wc, 0, True), HN(dca, wc, 0, True), HN(dcb, wc, 0, True),
                                    PA(w["ffn_conv_w_a"], wc, 0, True), PA(w["ffn_conv_w_b"], wc, 0, True)],
                           [RO(D_FF, BF16, wc, True), RO(D_FF, BF16, wc, True)], nm + "ffn_conv_t", lp, ncol=ncolf)
    g["w_up_a"] = mm(s["hn2"], dup_a, "tn", BF16, nm + "dw_up_a")
    g["w_up_b"] = mm(s["hn2"], dup_b, "tn", BF16, nm + "dw_up_b")
    dhn2 = mm(dup_a, w["w_up_a"], "nt", F32, nm + "dhn2_a")
    dhn2 = mm(dup_b, w["w_up_b"], "nt", F32, nm + "dhn2_b", add=dhn2)

    def rms_bwd(i, n, x, gg, dy, dres):
        _, vjp = jax.vjp(_rms, x, gg)
        dx, dg = vjp(dy)
        return dres + dx, dg

    dh_mid, g["norm_ffn_g"] = rowwise(rms_bwd, [RI(s["h_mid"]), PA(w["norm_ffn_g"]), RI(dhn2), RI(dh_out)],
                                      [RO(D_MODEL, F32), AO(1, D_MODEL)], nm + "rms_ffn_bwd", lp)
    g["w_out"] = mm(s["cat"], dh_mid, "tn", BF16, nm + "dw_out")
    d_cat = mm(dh_mid, w["w_out"], "nt", F32, nm + "d_cat")
    u = s["u"]

    def mix_bwd(i, n, y_pre, z, o_sb, o_mla, g1, g2, g3, dcat):
        _, vjp = jax.vjp(_mix_out, y_pre, z, o_sb, o_mla, g1, g2, g3)
        return vjp(dcat)

    dy_pre, dz, do_sb, do_mla, g["ssd_norm_g"], g["sb_norm_g"], g["mla_norm_g"] = rowwise(
        mix_bwd, [RI(s["y_pre"]), RI(u, SSD_WIDTH, U_Z // SSD_WIDTH), RI(s["o_sb"]), RI(s["o_mla"]),
                  PA(w["ssd_norm_g"]), PA(w["sb_norm_g"]), PA(w["mla_norm_g"]), RI(d_cat)],
        [RO(SSD_WIDTH, F32), RO(SSD_WIDTH, BF16), RO(SB_WIDTH, F32), RO(256, F32),
         AO(1, SSD_WIDTH), AO(1, SB_WIDTH), AO(1, 256)], nm + "mix_out_bwd", lp)
    dxs, dbp, dcp, ddtp, pg = ssd_bwd(s["xbc_c"], u, w["dt_bias"], w["a_log"], w["d_skip"], s["hprev"], dy_pre, nm + "ssd_bwd")
    pg = pg.reshape(2, SUBLANES, LANES).sum(axis=0)
    g["dt_bias"], g["a_log"], g["d_skip"] = pg[0:1], pg[1:2], pg[2:3]

    def conv4_bwd(i, n, x, hl, cw, cb_, dxs_, dbp_, dcp_, ddtp_):
        pre = conv_fwd(x, hl, cw, i) + cb_
        d_out = jnp.concatenate([dxs_, dbp_, dcp_], axis=1)
        sg = jax.nn.sigmoid(pre)
        d_pre = d_out * (sg * (1.0 + pre * (1.0 - sg)))
        ddt = ddtp_[:, 0:128] + ddtp_[:, 128:256]
        return d_pre, ddt, conv_bwd_w(d_pre, x, hl, i, SSD_CONV), jnp.sum(d_pre, axis=0, keepdims=True)

    d_pre, ddt, g["ssd_conv_w"], g["ssd_conv_b"] = rowwise(
        conv4_bwd, [RI(u, SSD_XBC, 0), HP(u, SSD_XBC, 0), PA(w["ssd_conv_w"]), PA(w["ssd_conv_b"]),
                    RI(dxs), RI(dbp), RI(dcp), RI(ddtp)],
        [RO(SSD_XBC, F32), RO(LANES, BF16), AO(SSD_CONV, SSD_XBC), AO(1, SSD_XBC)], nm + "ssd_conv_bwd", lp)
    d_xbc = rowwise(lambda i, n, d, hn_, cw: conv_bwd_data(d, hn_, cw, i, n),
                    [RI(d_pre), HN(d_pre), PA(w["ssd_conv_w"])], [RO(SSD_XBC, BF16)], nm + "ssd_conv_t", lp)[0]
    dq_sb, dk_sb, dv_sb = sb_attn_bwd(s["qkv"], s["rs_sb"], do_sb, nm + "sb_bwd")
    dqkv = jnp.concatenate([dq_sb, dk_sb.astype(BF16), dv_sb.astype(BF16)], axis=1)
    dqc, dkc, dvv = mla_attn_bwd(s["qc"], s["kc"], s["vv"], s["o_mla"], s["lse"], do_mla, nm + "mla_bwd")

    def unpack(i, n, dqc_, dkc_, dvv_, cos, sin):
        dqr = dqc_[:, 128:256] + dqc_[:, 384:512]
        dqr = dqr * cos + rope_rot_t(dqr * sin)
        dkr = dkc_[:, 128:256] + dkc_[:, 384:512]
        dkr = dkr * cos + rope_rot_t(dkr * sin)
        dq = jnp.concatenate([dqc_[:, 0:128], dqc_[:, 256:384], dqr], axis=1) * MLA_SCALE
        dkv = jnp.concatenate([dkc_[:, 0:128], dkc_[:, 256:384], dvv_], axis=1)
        return dq, dkv, dkr

    dq, dkv, dkr4 = rowwise(unpack, [RI(dqc), RI(dkc), RI(dvv), RI(cs), RI(sn)],
                            [RO(384, BF16), RO(512, BF16), RO(LANES, BF16)], nm + "mla_unpack", lp)
    g["w_uq"] = mm(s["qn"], dq, "tn", F32, nm + "dw_uq")
    g["w_ukv"] = mm(s["kvn"], dkv, "tn", F32, nm + "dw_ukv")
    dqn = mm(dq, w["w_uq"], "nt", F32, nm + "dqn")
    dkvn = mm(dkv, w["w_ukv"], "nt", F32, nm + "dkvn")

    def mla_rms_bwd(i, n, qa, ckv, gq, gkv, dqn_, dkvn_):
        _, vjp = jax.vjp(lambda a, b, c, d: (_rms(a, c, MLA_Q_RANK), _rms(b, d)), qa, ckv, gq, gkv)
        return vjp((dqn_, dkvn_))

    dqa, dckv, g["q_norm_g"], g["kv_norm_g"] = rowwise(
        mla_rms_bwd, [RI(u, 256, U_QA // 256), RI(u, LANES, U_CKV // LANES), PA(w["q_norm_g"]), PA(w["kv_norm_g"]),
                      RI(dqn), RI(dkvn)],
        [RO(256, BF16), RO(LANES, BF16), AO(1, 256), AO(1, LANES)], nm + "mla_rms_bwd", lp)
    du = jnp.concatenate([d_xbc, dz, dqa, dckv, dkr4, ddt, jnp.zeros((lp, LANES), BF16)], axis=1)
    g["w_main"] = mm(s["hn"], du, "tn", F32, nm + "dw_main")
    g["w_sb"] = mm(s["hn"], dqkv, "tn", F32, nm + "dw_sb")
    dhn = mm(du, w["w_main"], "nt", F32, nm + "dhn_main")
    dhn = mm(dqkv, w["w_sb"], "nt", F32, nm + "dhn_sb", add=dhn)
    dh_in, g["norm_mix_g"] = rowwise(rms_bwd, [RI(s["h"]), PA(w["norm_mix_g"]), RI(dhn), RI(dh_mid)],
                                     [RO(D_MODEL, F32), AO(1, D_MODEL)], nm + "rms_mix_bwd", lp)
    return dh_in, g


_IN_CUTS = np.cumsum((512, 1024, 8, 256, 256, 256, 192, 128, 32))


def _pad_cols(a, n):
    return jnp.pad(a, ((0, 0), (0, n - a.shape[1])))


def prep_layer_weights(full, l):
    w_in = full["w_in"][l]
    c = _IN_CUTS
    z, xbc, dtr = w_in[:, :c[0]], w_in[:, c[0]:c[1]], w_in[:, c[1]:c[2]]
    q_sb, k_sb, v_sb = w_in[:, c[2]:c[3]], w_in[:, c[3]:c[4]], w_in[:, c[4]:c[5]]
    q_a, c_kv, k_r = w_in[:, c[5]:c[6]], w_in[:, c[6]:c[7]], w_in[:, c[7]:c[8]]
    w_main = jnp.concatenate([xbc, z, _pad_cols(q_a, 256), c_kv, k_r, k_r, k_r, k_r, _pad_cols(dtr, 256)], axis=1)
    assert w_main.shape[1] == U_MAIN
    row = lambda v, n=None: _pad_cols(v.reshape(1, -1).astype(F32), v.size if n is None else n)
    uq = full["mla_w_uq"][l].reshape(MLA_Q_RANK, 4, 96)
    w_uq = jnp.concatenate([uq[:, :, :64].reshape(MLA_Q_RANK, 256), uq[:, :, 64:].reshape(MLA_Q_RANK, 128)], axis=1)
    w_uq = jnp.pad(w_uq, ((0, 256 - MLA_Q_RANK), (0, 0)))
    ukv = full["mla_w_ukv"][l].reshape(MLA_KV_RANK, 4, 128)
    w_ukv = jnp.concatenate([ukv[:, :, :64].reshape(MLA_KV_RANK, 256), ukv[:, :, 64:].reshape(MLA_KV_RANK, 256)], axis=1)
    return dict(
        norm_mix_g=row(full["norm_mix_g"][l]), w_main=w_main, w_sb=jnp.concatenate([q_sb * SB_SCALE, k_sb, v_sb], axis=1),
        ssd_conv_w=full["ssd_conv_w"][l], ssd_conv_b=row(full["ssd_conv_b"][l]),
        dt_bias=row(full["ssd_dt_bias"][l], LANES), a_log=row(full["ssd_a_log"][l], LANES), d_skip=row(full["ssd_d"][l], LANES),
        ssd_norm_g=row(full["ssd_norm_g"][l]), sb_norm_g=row(full["sb_norm_g"][l]),
        q_norm_g=row(full["mla_q_norm_g"][l], 256), kv_norm_g=row(full["mla_kv_norm_g"][l]),
        w_uq=w_uq, w_ukv=w_ukv, mla_norm_g=row(full["mla_norm_g"][l]),
        w_out=full["w_out"][l], norm_ffn_g=row(full["norm_ffn_g"][l]),
        w_up_a=full["ffn_w_up"][l][:, :D_FF], w_up_b=full["ffn_w_up"][l][:, D_FF:],
        ffn_conv_w_a=full["ffn_conv_w"][l][:, :D_FF], ffn_conv_w_b=full["ffn_conv_w"][l][:, D_FF:],
        ffn_conv_b_a=row(full["ffn_conv_b"][l][:D_FF]), ffn_conv_b_b=row(full["ffn_conv_b"][l][D_FF:]),
        w_down=full["ffn_w_down"][l],
    )


def unprep_layer_grads(g):
    wm = g["w_main"]
    xbc, z = wm[:, U_XBC:U_XBC + 1024], wm[:, U_Z:U_Z + 512]
    q_a, c_kv = wm[:, U_QA:U_QA + MLA_Q_RANK], wm[:, U_CKV:U_CKV + 128]
    k_r = (wm[:, U_KR4:U_KR4 + 32] + wm[:, U_KR4 + 32:U_KR4 + 64] + wm[:, U_KR4 + 64:U_KR4 + 96] + wm[:, U_KR4 + 96:U_KR4 + 128])
    dtr = wm[:, U_DT:U_DT + SSD_HEADS]
    w_sb = g["w_sb"]
    w_in = jnp.concatenate([z, xbc, dtr, w_sb[:, :SB_WIDTH] * SB_SCALE, w_sb[:, SB_WIDTH:], q_a, c_kv, k_r], axis=1)
    guq = g["w_uq"][:MLA_Q_RANK]
    guq = jnp.concatenate([guq[:, :256].reshape(MLA_Q_RANK, 4, 64), guq[:, 256:].reshape(MLA_Q_RANK, 4, 32)], axis=2)
    gukv = g["w_ukv"]
    gukv = jnp.concatenate([gukv[:, :256].reshape(MLA_KV_RANK, 4, 64), gukv[:, 256:].reshape(MLA_KV_RANK, 4, 64)], axis=2)
    return dict(
        norm_mix_g=g["norm_mix_g"][0], w_in=w_in, ssd_conv_w=g["ssd_conv_w"], ssd_conv_b=g["ssd_conv_b"][0],
        ssd_dt_bias=g["dt_bias"][0, :SSD_HEADS], ssd_a_log=g["a_log"][0, :SSD_HEADS], ssd_d=g["d_skip"][0, :SSD_HEADS],
        ssd_norm_g=g["ssd_norm_g"][0], sb_norm_g=g["sb_norm_g"][0], mla_q_norm_g=g["q_norm_g"][0, :MLA_Q_RANK],
        mla_kv_norm_g=g["kv_norm_g"][0], mla_w_uq=guq.reshape(MLA_Q_RANK, 384), mla_w_ukv=gukv.reshape(MLA_KV_RANK, 512),
        mla_norm_g=g["mla_norm_g"][0], w_out=g["w_out"], norm_ffn_g=g["norm_ffn_g"][0],
        ffn_w_up=jnp.concatenate([g["w_up_a"], g["w_up_b"]], axis=1),
        ffn_conv_w=jnp.concatenate([g["ffn_conv_w_a"], g["ffn_conv_w_b"]], axis=1),
        ffn_conv_b=jnp.concatenate([g["ffn_conv_b_a"][0], g["ffn_conv_b_b"][0]], axis=0),
        ffn_w_down=g["w_down"],
    )


def rope_tables(lp):
    pos = jnp.arange(lp, dtype=F32)
    inv = 1.0 / (ROPE_BASE ** (jnp.arange(0, MLA_ROPE, 2, dtype=F32) / MLA_ROPE))
    ang = pos[:, None] * inv[None, :]
    ang = jnp.concatenate([ang, ang] * 4, axis=-1)
    return jnp.cos(ang), jnp.sin(ang)


def local_step(x_seq, target, full):
    seq = x_seq.shape[0]
    length = seq + N_META
    lp = -(-length // ROW_TILE) * ROW_TILE
    cs, sn = rope_tables(lp)
    h = jnp.concatenate([full["meta_tokens"].astype(F32), x_seq, jnp.zeros((lp - length, D_MODEL), F32)], axis=0)
    tgt = jnp.pad(target, ((N_META, lp - length), (0, 0)))
    ws = [prep_layer_weights(full, l) for l in range(DEPTH)]
    saved = []
    for l in range(DEPTH):
        h, s = layer_fwd(h, ws[l], cs, sn, l)
        saved.append(s)
    fg = full["final_norm_g"].reshape(1, D_MODEL).astype(F32)
    tm = min(ROW_TILE, lp)

    def loss_fn(i, n, x, g, t):
        rows = _rows_iota(x) + i * tm
        valid = jnp.logical_and(rows >= N_META, rows < length)

        def f(x_, g_):
            err = jnp.where(valid, _rms(x_, g_) - t, 0.0)
            return 0.5 * jnp.sum(err * err) * (1.0 / D_MODEL)

        val, (dx, dg) = jax.value_and_grad(f, argnums=(0, 1))(x, g)
        return dx, jnp.full((1, LANES), val, F32), dg

    dh, loss_row, g_final = rowwise(loss_fn, [RI(h), PA(fg), RI(tgt)], [RO(D_MODEL, F32), AO(1, LANES), AO(1, D_MODEL)],
                                    "loss_head", lp)
    grads = {}
    per_layer = [None] * DEPTH
    for l in reversed(range(DEPTH)):
        dh, g = layer_bwd(dh, ws[l], saved[l], cs, sn, l)
        per_layer[l] = unprep_layer_grads(g)
    for k in per_layer[0]:
        grads[k] = jnp.stack([per_layer[l][k] for l in range(DEPTH)], axis=0)
    grads["final_norm_g"] = g_final[0]
    grads["meta_tokens"] = dh[:N_META]
    return loss_row[0, 0], dh[N_META:length], grads


_ANY = pl.BlockSpec(memory_space=pl.ANY)


def chip_exchange(srcs, modes, name):
    n = len(srcs)
    flips = ((1, 0), (0, 1), (1, 1))

    def body(*refs):
        ins, outs = refs[:n], refs[n:2 * n]
        send_sems, recv_sems, fwd_send_sems, fwd_recv_sems, loc_sems = refs[2 * n:]
        x, y, c = lax.axis_index("x"), lax.axis_index("y"), lax.axis_index("c")
        me = 2 * x + y
        waits, forwards = [], []
        for a in range(n):
            whole = modes[a] != "slab"
            cp = pltpu.make_async_copy(ins[a] if whole else ins[a].at[me], outs[a].at[me], loc_sems.at[a])
            cp.start()
            waits.append(cp.wait)
            half = ins[a].shape[0] // 2 if modes[a] == "bcast_split" else None
            for k, (fx, fy) in enumerate(flips):
                px = 1 - x if fx else x
                py = 1 - y if fy else y
                peer = 2 * px + py
                if half is None:
                    src = ins[a] if whole else ins[a].at[peer]
                    dst = outs[a].at[me]
                else:
                    src = ins[a].at[pl.ds(c * half, half)]
                    dst = outs[a].at[me, pl.ds(c * half, half)]
                rc = pltpu.make_async_remote_copy(src_ref=src, dst_ref=dst, send_sem=send_sems.at[a, k],
                                                  recv_sem=recv_sems.at[a, k], device_id=(px, py, c), device_id_type=MESH_ID)
                rc.start()
                if half is None:
                    waits.append(rc.wait)
                else:
                    waits.append(rc.wait_send)
                    landed = outs[a].at[peer, pl.ds(c * half, half)]
                    fw = pltpu.make_async_remote_copy(src_ref=landed, dst_ref=landed, send_sem=fwd_send_sems.at[a, k],
                                                      recv_sem=fwd_recv_sems.at[a, k], device_id=(x, y, 1 - c),
                                                      device_id_type=MESH_ID)
                    forwards.append((rc, fw))
        for rc, fw in forwards:
            rc.wait_recv()
            fw.start()
        for rc, fw in forwards:
            fw.wait()
        for w in waits:
            w()

    out_shape = [jax.ShapeDtypeStruct((N_CHIPS,) + (s.shape if m != "slab" else s.shape[1:]), s.dtype) for s, m in zip(srcs, modes)]
    return pl.pallas_call(
        body, name=name, in_specs=[_ANY] * n, out_specs=[_ANY] * n, out_shape=out_shape,
        scratch_shapes=[pltpu.SemaphoreType.DMA((n, 3)), pltpu.SemaphoreType.DMA((n, 3)), pltpu.SemaphoreType.DMA((n, 3)),
                        pltpu.SemaphoreType.DMA((n, 3)), pltpu.SemaphoreType.DMA((n,))],
    )(*srcs)


def _piece(ref, mode, k):
    if mode == "slab":
        return ref.at[k]
    if mode == "rows":
        rs = ref.shape[1] // N_CHIPS
        return ref.at[:, pl.ds(pl.multiple_of(k * rs, 16), rs), :]
    if mode == "cols":
        cs = ref.shape[2] // N_CHIPS
        return ref.at[:, :, pl.ds(pl.multiple_of(k * cs, LANES), cs)]
    return ref


def _piece_shape(shape, mode):
    if mode == "slab":
        return shape[1:]
    if mode == "rows":
        return (shape[0], shape[1] // N_CHIPS, shape[2])
    if mode == "cols":
        return (shape[0], shape[1], shape[2] // N_CHIPS)
    return shape


def grad_exchange(srcs, modes, name):
    n = len(srcs)
    flips = ((1, 0), (0, 1), (1, 1))

    def body(*refs):
        ins, outs = refs[:n], refs[n:2 * n]
        send_sems, recv_sems, fwd_send_sems, fwd_recv_sems, sib_send_sems, sib_recv_sems, loc_sems = refs[2 * n:]
        x, y, c = lax.axis_index("x"), lax.axis_index("y"), lax.axis_index("c")
        me = 2 * x + y
        sibling = (x, y, 1 - c)
        waits, forwards = [], []
        for a in range(n):
            mine = _piece(ins[a], modes[a], me)
            slot = outs[a].at[4 * c + me]
            cp = pltpu.make_async_copy(mine, slot, loc_sems.at[a])
            cp.start()
            sb = pltpu.make_async_remote_copy(src_ref=mine, dst_ref=slot, send_sem=sib_send_sems.at[a],
                                              recv_sem=sib_recv_sems.at[a], device_id=sibling, device_id_type=MESH_ID)
            sb.start()
            waits += [cp.wait, sb.wait]
            for k, (fx, fy) in enumerate(flips):
                px = 1 - x if fx else x
                py = 1 - y if fy else y
                peer = 2 * px + py
                rc = pltpu.make_async_remote_copy(src_ref=_piece(ins[a], modes[a], peer), dst_ref=slot,
                                                  send_sem=send_sems.at[a, k], recv_sem=recv_sems.at[a, k],
                                                  device_id=(px, py, c), device_id_type=MESH_ID)
                rc.start()
                landed = outs[a].at[4 * c + peer]
                fw = pltpu.make_async_remote_copy(src_ref=landed, dst_ref=landed, send_sem=fwd_send_sems.at[a, k],
                                                  recv_sem=fwd_recv_sems.at[a, k], device_id=sibling, device_id_type=MESH_ID)
                waits.append(rc.wait_send)
                forwards.append((rc, fw))
        for rc, fw in forwards:
            rc.wait_recv()
            fw.start()
        for rc, fw in forwards:
            fw.wait()
        for w in waits:
            w()

    out_shape = [jax.ShapeDtypeStruct((2 * N_CHIPS,) + tuple(_piece_shape(s.shape, m)), s.dtype) for s, m in zip(srcs, modes)]
    dma = pltpu.SemaphoreType.DMA
    return pl.pallas_call(
        body, name=name, in_specs=[_ANY] * n, out_specs=[_ANY] * n, out_shape=out_shape,
        scratch_shapes=[dma((n, 3)), dma((n, 3)), dma((n, 3)), dma((n, 3)), dma((n,)), dma((n,)), dma((n,))],
    )(*srcs)


WEIGHT_NAMES = ("meta_tokens", "norm_mix_g", "w_in", "ssd_conv_w", "ssd_conv_b", "ssd_dt_bias", "ssd_a_log", "ssd_d",
                "ssd_norm_g", "sb_norm_g", "mla_q_norm_g", "mla_kv_norm_g", "mla_w_uq", "mla_w_ukv", "mla_norm_g",
                "w_out", "norm_ffn_g", "ffn_w_up", "ffn_conv_w", "ffn_conv_b", "ffn_w_down", "final_norm_g")
SHARD_AXIS = {"meta_tokens": 1, "w_in": 2, "ssd_conv_w": 2, "mla_w_uq": 2, "mla_w_ukv": 2, "w_out": 1, "ffn_w_up": 2,
              "ffn_conv_w": 2, "ffn_w_down": 1}
SHARDED = tuple(n for n in WEIGHT_NAMES if n in SHARD_AXIS)
REPLICATED = tuple(n for n in WEIGHT_NAMES if n not in SHARD_AXIS)
GATHER_BF16 = ("w_in", "mla_w_uq", "mla_w_ukv", "w_out", "ffn_w_up", "ffn_w_down")
GATHER_F32 = ("meta_tokens", "ssd_conv_w", "ffn_conv_w")
PACK_ROWS = ROW_TILE


def pack(arrs, dtype):
    flat = jnp.concatenate([a.reshape(-1).astype(dtype) for a in arrs])
    per = PACK_ROWS * PACK_W
    total = -(-flat.size // per) * per
    return jnp.pad(flat, (0, total - flat.size)).reshape(total // PACK_W, PACK_W)


def unpack(buf, shapes):
    flat = buf.reshape(-1)
    out, off = [], 0
    for shp in shapes:
        size = int(np.prod(shp))
        out.append(flat[off:off + size].reshape(shp))
        off += size
    return out


def gather_weights(a):
    full = {n: a[n] for n in REPLICATED}
    bufs = [pack([a[n] for n in GATHER_BF16], BF16), pack([a[n] for n in GATHER_F32], F32)]
    got = chip_exchange(bufs, ("bcast_split", "bcast"), "gather_weights")
    for names, g in ((GATHER_BF16, got[0]), (GATHER_F32, got[1])):
        pieces = [unpack(g[k], [a[n].shape for n in names]) for k in range(N_CHIPS)]
        for idx, n in enumerate(names):
            full[n] = jnp.concatenate([pieces[k][idx] for k in range(N_CHIPS)], axis=SHARD_AXIS[n])
    return full


BIG = ("w_in", "w_out", "ffn_w_up", "ffn_w_down")
BIG_MODE = {"w_in": "slab", "w_out": "rows", "ffn_w_up": "cols", "ffn_w_down": "rows"}
SMALL_SHARDED = tuple(n for n in SHARDED if n not in BIG)
ADAM_TILE = 128


def _adamw(i, n, *vals):
    parts, (w, m, v) = vals[:2 * N_CHIPS], vals[2 * N_CHIPS:]
    g = parts[0].astype(F32)
    for p in parts[1:]:
        g = g + p.astype(F32)
    m = ADAM_B1 * m + (1.0 - ADAM_B1) * g
    v = ADAM_B2 * v + (1.0 - ADAM_B2) * jnp.square(g)
    m_hat = m / (1.0 - ADAM_B1 ** ADAM_STEP)
    v_hat = v / (1.0 - ADAM_B2 ** ADAM_STEP)
    delta = -ADAM_LR * (m_hat / (jnp.sqrt(v_hat) + ADAM_EPS) + ADAM_WD * w)
    return g, delta, m, v


def _adamw_call(got, w, m, v, name):
    rows, width = w.shape
    flat = got.reshape(2 * N_CHIPS * rows, width)
    blk = rows // ADAM_TILE
    ins = [RI(flat, rblk=k * blk) for k in range(2 * N_CHIPS)] + [RI(w), RI(m), RI(v)]
    return rowwise(_adamw, ins, [RO(width, F32)] * 4, name, rows, tm=ADAM_TILE)


def reduce_and_update(a, grads):
    srcs, modes = [], []
    for n in BIG:
        g = grads[n].astype(BF16)
        if n == "w_in":
            cs = a[n].shape[2]
            g = g.reshape(DEPTH, D_MODEL, N_CHIPS, cs).transpose(2, 0, 1, 3)
        srcs.append(g)
        modes.append(BIG_MODE[n])
    slabs = []
    for k in range(N_CHIPS):
        parts = []
        for n in SMALL_SHARDED:
            ax = SHARD_AXIS[n]
            size = a[n].shape[ax]
            parts.append(lax.slice_in_dim(grads[n], k * size, (k + 1) * size, axis=ax))
        slabs.append(pack(parts, BF16))
    srcs += [jnp.stack(slabs, axis=0), pack([grads[n] for n in REPLICATED], F32)]
    modes += ["slab", "bcast"]
    got = grad_exchange(srcs, modes, "exchange_grads")
    outs = {}
    kinds = ("grad", "delta", "new_m", "new_v")
    for n, g8 in zip(BIG, got):
        shp = a[n].shape
        rows = shp[0] * shp[1]
        flat = lambda t: t.reshape(rows, shp[2])
        res = _adamw_call(g8.reshape(2 * N_CHIPS, rows, shp[2]), flat(a[n]), flat(a["m_" + n]), flat(a["v_" + n]), "adamw_" + n)
        for kind, val in zip(kinds, res):
            outs[(kind, n)] = val.reshape(shp)
    for tag, names, g8 in (("small", SMALL_SHARDED, got[len(BIG)]), ("rep", REPLICATED, got[len(BIG) + 1])):
        shapes = [a[n].shape for n in names]
        packed = [pack([a[pre + n] for n in names], F32) for pre in ("", "m_", "v_")]
        res = _adamw_call(g8, *packed, "adamw_" + tag)
        for kind, buf in zip(kinds, res):
            for n, val in zip(names, unpack(buf, shapes)):
                outs[(kind, n)] = val
    return outs


INPUT_NAMES = ("x",) + WEIGHT_NAMES + ("loss_target",) + tuple("m_" + n for n in WEIGHT_NAMES) + tuple("v_" + n for n in WEIGHT_NAMES)


def kernel(x, meta_tokens, norm_mix_g, w_in, ssd_conv_w, ssd_conv_b, ssd_dt_bias, ssd_a_log, ssd_d, ssd_norm_g, sb_norm_g, mla_q_norm_g, mla_kv_norm_g, mla_w_uq, mla_w_ukv, mla_norm_g, w_out, norm_ffn_g, ffn_w_up, ffn_conv_w, ffn_conv_b, ffn_w_down, final_norm_g, loss_target, m_meta_tokens, m_norm_mix_g, m_w_in, m_ssd_conv_w, m_ssd_conv_b, m_ssd_dt_bias, m_ssd_a_log, m_ssd_d, m_ssd_norm_g, m_sb_norm_g, m_mla_q_norm_g, m_mla_kv_norm_g, m_mla_w_uq, m_mla_w_ukv, m_mla_norm_g, m_w_out, m_norm_ffn_g, m_ffn_w_up, m_ffn_conv_w, m_ffn_conv_b, m_ffn_w_down, m_final_norm_g, v_meta_tokens, v_norm_mix_g, v_w_in, v_ssd_conv_w, v_ssd_conv_b, v_ssd_dt_bias, v_ssd_a_log, v_ssd_d, v_ssd_norm_g, v_sb_norm_g, v_mla_q_norm_g, v_mla_kv_norm_g, v_mla_w_uq, v_mla_w_ukv, v_mla_norm_g, v_w_out, v_norm_ffn_g, v_ffn_w_up, v_ffn_conv_w, v_ffn_conv_b, v_ffn_w_down, v_final_norm_g):
    args = (x, meta_tokens, norm_mix_g, w_in, ssd_conv_w, ssd_conv_b, ssd_dt_bias, ssd_a_log, ssd_d, ssd_norm_g, sb_norm_g, mla_q_norm_g, mla_kv_norm_g, mla_w_uq, mla_w_ukv, mla_norm_g, w_out, norm_ffn_g, ffn_w_up, ffn_conv_w, ffn_conv_b, ffn_w_down, final_norm_g, loss_target, m_meta_tokens, m_norm_mix_g, m_w_in, m_ssd_conv_w, m_ssd_conv_b, m_ssd_dt_bias, m_ssd_a_log, m_ssd_d, m_ssd_norm_g, m_sb_norm_g, m_mla_q_norm_g, m_mla_kv_norm_g, m_mla_w_uq, m_mla_w_ukv, m_mla_norm_g, m_w_out, m_norm_ffn_g, m_ffn_w_up, m_ffn_conv_w, m_ffn_conv_b, m_ffn_w_down, m_final_norm_g, v_meta_tokens, v_norm_mix_g, v_w_in, v_ssd_conv_w, v_ssd_conv_b, v_ssd_dt_bias, v_ssd_a_log, v_ssd_d, v_ssd_norm_g, v_sb_norm_g, v_mla_q_norm_g, v_mla_kv_norm_g, v_mla_w_uq, v_mla_w_ukv, v_mla_norm_g, v_w_out, v_norm_ffn_g, v_ffn_w_up, v_ffn_conv_w, v_ffn_conv_b, v_ffn_w_down, v_final_norm_g)
    a = dict(zip(INPUT_NAMES, args, strict=True))
    full = gather_weights(a)
    loss, grad_x, grads = local_step(a["x"][0], a["loss_target"][0], full)
    loss = lax.psum(loss, ("x", "y", "c"))
    outs = reduce_and_update(a, grads)
    result = [loss, grad_x[None]]
    for kind in ("grad", "delta", "new_m", "new_v"):
        result += [outs[(kind, n)] for n in WEIGHT_NAMES]
    return tuple(result)
```

```python
import functools
import math

import numpy as np
import jax
import jax.numpy as jnp
from jax import lax
from jax.experimental import pallas as pl
from jax.experimental.pallas import tpu as pltpu

F32 = jnp.float32
BF16 = jnp.bfloat16
HIGHEST = lax.Precision.HIGHEST
MESH_ID = pl.DeviceIdType.MESH

D_MODEL = 1024
DEPTH = 2
N_META = 16
EPS = 1e-6
SSD_HEADS = 8
SSD_WIDTH = 512
SSD_XBC = 1024
SSD_CONV = 4
SB_WIDTH = 256
SB_SCALE = 64 ** -0.5
MLA_Q_RANK = 192
MLA_KV_RANK = 128
MLA_ROPE = 32
MLA_SCALE = 96 ** -0.5
ROPE_BASE = 10000.0
D_FF = 2816
FFN_CONV = 3
IN_COLS = 2664
N_CHIPS = 4

ADAM_LR = 0.001
ADAM_B1 = 0.9
ADAM_B2 = 0.999
ADAM_EPS = 1e-08
ADAM_WD = 0.01
ADAM_STEP = 10

LANES = 128
SUBLANES = 8
ROW_TILE = 256
KEY_UNROLL = 4
VMEM_LIMIT = 56 * 1024 * 1024
PACK_W = 1024

U_XBC, U_Z, U_QA, U_CKV, U_KR4, U_DT, U_MAIN = 0, 1024, 1536, 1792, 1920, 2048, 2304
NEG = -1e30


def _cp(*sem):
    return pltpu.CompilerParams(dimension_semantics=sem if sem else None, vmem_limit_bytes=VMEM_LIMIT)


def _pick(dim, pref):
    if dim <= pref:
        return dim
    best = None
    for t in range(LANES, pref + 1, LANES):
        if dim % t == 0:
            best = t
    assert best is not None, (dim, pref)
    return best


def _dot(a, b, dims="nn", precision=None):
    dn = {"nn": (((1,), (0,)), ((), ())), "nt": (((1,), (1,)), ((), ())), "tn": (((0,), (0,)), ((), ()))}[dims]
    return lax.dot_general(a, b, dn, preferred_element_type=F32, precision=precision)


def _softplus(x):
    return jnp.maximum(x, 0.0) + jnp.log1p(jnp.exp(-jnp.abs(x)))


def _silu(x):
    return x * jax.nn.sigmoid(x)


def _rms(x, g, n=None):
    n = x.shape[-1] if n is None else n
    ms = jnp.sum(x * x, axis=-1, keepdims=True) * (1.0 / n)
    return x * lax.rsqrt(ms + EPS) * g


def mm(a, b, dims, out_dtype, name, add=None, tm=None, tn=None, tk=None):
    if dims == "nn":
        (m, k), (k2, n) = a.shape, b.shape
    elif dims == "nt":
        (m, k), (n, k2) = a.shape, b.shape
    else:
        (k, m), (k2, n) = a.shape, b.shape
    assert k == k2, (a.shape, b.shape, dims)
    if dims == "tn":
        tm, tn, tk = _pick(m, tm or 1408), _pick(n, tn or 1408), _pick(k, tk or 1408)
    else:
        tm, tn, tk = _pick(m, tm or 768), _pick(n, tn or 1408), _pick(k, tk or 2816)
    nk = k // tk
    if dims == "tn":
        a_spec = pl.BlockSpec((tk, tm), lambda j, i, kk: (kk, i))
    else:
        a_spec = pl.BlockSpec((tm, tk), lambda j, i, kk: (i, kk))
    if dims == "nt":
        b_spec = pl.BlockSpec((tn, tk), lambda j, i, kk: (j, kk))
    else:
        b_spec = pl.BlockSpec((tk, tn), lambda j, i, kk: (kk, j))
    o_spec = pl.BlockSpec((tm, tn), lambda j, i, kk: (i, j))
    has_add = add is not None

    def body(*refs):
        a_ref, b_ref = refs[0], refs[1]
        add_ref = refs[2] if has_add else None
        o_ref = refs[3] if has_add else refs[2]
        part = _dot(a_ref[...].astype(BF16), b_ref[...].astype(BF16), dims)

        def finish(r):
            if has_add:
                r = r + add_ref[...].astype(F32)
            o_ref[...] = r.astype(o_ref.dtype)

        if nk == 1:
            finish(part)
            return
        acc_ref = refs[-1]
        kk = pl.program_id(2)

        @pl.when(kk == 0)
        def _():
            acc_ref[...] = part

        @pl.when(jnp.logical_and(kk > 0, kk < nk - 1))
        def _():
            acc_ref[...] += part

        @pl.when(kk == nk - 1)
        def _():
            finish(acc_ref[...] + part)

    in_specs = [a_spec, b_spec] + ([o_spec] if has_add else [])
    args = (a, b) + ((add,) if has_add else ())
    return pl.pallas_call(
        body, name=name, grid=(n // tn, m // tm, nk),
        in_specs=in_specs, out_specs=o_spec,
        out_shape=jax.ShapeDtypeStruct((m, n), out_dtype),
        scratch_shapes=[pltpu.VMEM((tm, tn), F32)] if nk > 1 else [],
        compiler_params=_cp("parallel", "parallel", "arbitrary"),
    )(*args)


def RI(arr, width=None, cidx=0, cv=False, rblk=0):
    return ("row" if rblk == 0 else ("row", rblk), arr, arr.shape[1] if width is None else width, cidx, cv)


def HP(arr, width=None, cidx=0, cv=False):
    return ("prev", arr, arr.shape[1] if width is None else width, cidx, cv)


def HN(arr, width=None, cidx=0, cv=False):
    return ("next", arr, arr.shape[1] if width is None else width, cidx, cv)


def PA(arr, width=None, cidx=0, cv=False):
    return ("par", arr, arr.shape[1] if width is None else width, cidx, cv)


def RO(ncols, dtype, width=None, cv=False):
    return ("row", ncols, dtype, ncols if width is None else width, cv)


def AO(nrows, ncols, width=None, cv=False):
    return ("acc", (nrows, ncols), F32, ncols if width is None else width, cv)


def rowwise(fn, ins, outs, name, rows, tm=ROW_TILE, ncol=1):
    tm = min(tm, rows)
    assert rows % tm == 0
    nrow = rows // tm
    hb = tm // SUBLANES
    last_hb = rows // SUBLANES - 1
    in_specs, args = [], []
    for kind, arr, width, cidx, cv in ins:
        def cmap(j, cidx=cidx, cv=cv):
            return cidx + j if cv else cidx
        if kind == "row":
            spec = pl.BlockSpec((tm, width), lambda j, i, cmap=cmap: (i, cmap(j)))
        elif isinstance(kind, tuple):
            spec = pl.BlockSpec((tm, width), lambda j, i, cmap=cmap, rblk=kind[1]: (i + rblk, cmap(j)))
        elif kind == "prev":
            spec = pl.BlockSpec((SUBLANES, width), lambda j, i, cmap=cmap: (jnp.maximum(i * hb - 1, 0), cmap(j)))
        elif kind == "next":
            spec = pl.BlockSpec((SUBLANES, width), lambda j, i, cmap=cmap: (jnp.minimum((i + 1) * hb, last_hb), cmap(j)))
        else:
            spec = pl.BlockSpec((arr.shape[0], width), lambda j, i, cmap=cmap: (0, cmap(j)))
        in_specs.append(spec)
        args.append(arr)
    out_specs, out_shapes, acc_cv = [], [], []
    for kind, shp, dtype, width, cv in outs:
        if kind == "row":
            out_specs.append(pl.BlockSpec((tm, width), lambda j, i, cv=cv: (i, j if cv else 0)))
            out_shapes.append(jax.ShapeDtypeStruct((rows, shp), dtype))
            acc_cv.append(None)
        else:
            out_specs.append(pl.BlockSpec((shp[0], width), lambda j, i, cv=cv: (0, j if cv else 0)))
            out_shapes.append(jax.ShapeDtypeStruct(shp, dtype))
            acc_cv.append(cv)
    n_in = len(ins)

    def body(*refs):
        j = pl.program_id(0)
        i = pl.program_id(1)
        vals = fn(i, nrow, *[r[...] for r in refs[:n_in]])
        if not isinstance(vals, (tuple, list)):
            vals = (vals,)
        for o_ref, v, cv in zip(refs[n_in:], vals, acc_cv):
            if cv is None:
                o_ref[...] = v.astype(o_ref.dtype)
            else:
                first = (i == 0) if cv else jnp.logical_and(i == 0, j == 0)

                @pl.when(first)
                def _(o_ref=o_ref, v=v):
                    o_ref[...] = v.astype(o_ref.dtype)

                @pl.when(jnp.logical_not(first))
                def _(o_ref=o_ref, v=v):
                    o_ref[...] += v.astype(o_ref.dtype)

    res = pl.pallas_call(
        body, name=name, grid=(ncol, nrow), in_specs=in_specs, out_specs=out_specs, out_shape=out_shapes,
        compiler_params=_cp("arbitrary", "arbitrary"),
    )(*args)
    return res


def _rows_iota(x):
    return lax.broadcasted_iota(jnp.int32, x.shape, 0)


def shift_down(x, halo, s):
    if s == 0:
        return x
    tm = x.shape[0]
    top = pltpu.roll(halo, s, 0)
    if tm > SUBLANES:
        top = jnp.concatenate([top, jnp.zeros((tm - SUBLANES, x.shape[1]), x.dtype)], axis=0)
    return jnp.where(_rows_iota(x) < s, top, pltpu.roll(x, s, 0))


def shift_up(x, halo, s):
    if s == 0:
        return x
    tm = x.shape[0]
    bot = pltpu.roll(halo, SUBLANES - s, 0)
    if tm > SUBLANES:
        bot = jnp.concatenate([jnp.zeros((tm - SUBLANES, x.shape[1]), x.dtype), bot], axis=0)
    return jnp.where(_rows_iota(x) >= tm - s, bot, pltpu.roll(x, tm - s, 0))


def conv_fwd(x, halo, w, i):
    kw = w.shape[0]
    halo = jnp.where(i == 0, 0.0, halo)
    out = None
    for k in range(kw):
        term = w[k:k + 1, :] * shift_down(x, halo, kw - 1 - k)
        out = term if out is None else out + term
    return out


def conv_bwd_data(dy, halo_next, w, i, n):
    kw = w.shape[0]
    halo_next = jnp.where(i == n - 1, 0.0, halo_next)
    out = None
    for k in range(kw):
        term = w[k:k + 1, :] * shift_up(dy, halo_next, kw - 1 - k)
        out = term if out is None else out + term
    return out


def conv_bwd_w(dy, x, halo, i, kw):
    halo = jnp.where(i == 0, 0.0, halo)
    rows = [jnp.sum(dy * shift_down(x, halo, kw - 1 - k), axis=0, keepdims=True) for k in range(kw)]
    return jnp.concatenate(rows, axis=0)


def _lane(shape):
    return lax.broadcasted_iota(jnp.int32, shape, 1)


def rope_rot(x):
    lane = _lane(x.shape) % MLA_ROPE
    return jnp.where(lane < MLA_ROPE // 2, -pltpu.roll(x, LANES - MLA_ROPE // 2, 1), pltpu.roll(x, MLA_ROPE // 2, 1))


def rope_rot_t(g):
    lane = _lane(g.shape) % MLA_ROPE
    return jnp.where(lane < MLA_ROPE // 2, pltpu.roll(g, LANES - MLA_ROPE // 2, 1), -pltpu.roll(g, MLA_ROPE // 2, 1))


def _ssd_common(g, xs, dt_raw, bias, alog, q):
    lane = _lane((q, LANES))
    pre = dt_raw + bias
    dt = jnp.where(lane < SSD_HEADS, _softplus(pre), 0.0)
    a_row = -jnp.exp(alog)
    d_a = dt * a_row
    ri = lax.broadcasted_iota(jnp.int32, (q, q), 0)
    ci = lax.broadcasted_iota(jnp.int32, (q, q), 1)
    causal = ri >= ci
    acs = _dot(causal.astype(F32), d_a, "nn", HIGHEST)
    acs_t = acs.T
    subl = lax.broadcasted_iota(jnp.int32, (LANES, q), 0)
    heads = [4 * g + i for i in range(4)]
    lo = lane < 64

    def col(arr, h):
        return jnp.sum(jnp.where(lane == h, arr, 0.0), axis=1, keepdims=True)

    def lanes4(v):
        m = lo if v[0].shape[0] == q else lo[0:1, :]
        return jnp.concatenate([jnp.where(m, v[0], v[1]), jnp.where(m, v[2], v[3])], axis=1)

    cols = [col(acs, h) for h in heads]
    rows = [jnp.sum(jnp.where(subl == h, acs_t, 0.0), axis=0, keepdims=True) for h in heads]
    tots = [c_[q - 1:q, :] for c_ in cols]
    acs4 = lanes4(cols)
    dt4 = lanes4([col(dt, h) for h in heads])
    lms = [jnp.exp(jnp.where(causal, cols[i] - rows[i], NEG)) for i in range(4)]
    lane4 = _lane((q, 2 * LANES))
    hm = [jnp.logical_and(lane4 >= 64 * i, lane4 < 64 * (i + 1)) for i in range(4)]
    return dict(lane=lane, lo=lo, pre=pre, dt=dt, a_row=a_row, heads=heads, tots=tots, lms=lms, ri=ri, ci=ci, hm=hm,
                lanes4=lanes4, eacs=jnp.exp(acs4), dte=jnp.exp(lanes4(tots) - acs4), dt4=dt4, x=xs * dt4)


def _pick_lane(row_arr, h):
    return jnp.sum(jnp.where(_lane(row_arr.shape) == h, row_arr, 0.0), axis=1, keepdims=True)


def _etot(tots):
    sub = lax.broadcasted_iota(jnp.int32, (2 * LANES, LANES), 0)
    e = [jnp.exp(t) for t in tots]
    return jnp.where(sub < 64, e[0], jnp.where(sub < 128, e[1], jnp.where(sub < 192, e[2], e[3]))), e


def _half(arr, i, lo):
    slab = arr[:, LANES * (i // 2):LANES * (i // 2 + 1)]
    return jnp.where(lo, slab, 0.0) if i % 2 == 0 else jnp.where(lo, 0.0, slab)


def ssd_fwd(xbc_c, u_main, bias_row, alog_row, d_row, name):
    lp = xbc_c.shape[0]
    q = min(ROW_TILE, lp)
    nc = lp // q
    dt_blk = U_DT // LANES

    def body(xs_ref, b_ref, c_ref, dt_ref, bias_ref, alog_ref, d_ref, y_ref, hp_ref, h_scr):
        g = pl.program_id(0)
        c = pl.program_id(1)

        @pl.when(c == 0)
        def _():
            h_scr[...] = jnp.zeros_like(h_scr)

        xs = xs_ref[...]
        bb = b_ref[...].astype(BF16)
        cb_ = c_ref[...].astype(BF16)
        s = _ssd_common(g, xs, dt_ref[...], bias_ref[...], alog_ref[...], q)
        gmat = _dot(cb_, bb, "nt")
        ms = [(gmat * s["lms"][i]).astype(BF16) for i in range(4)]
        xjs = [_half(s["x"], i, s["lo"]).astype(BF16) for i in range(4)]
        ys = [_dot(ms[i], xjs[i]) for i in range(4)]
        hp = h_scr[...]
        hp_ref[...] = hp
        yoff = _dot(cb_, hp.astype(BF16), "nt") * s["eacs"]
        d4 = s["lanes4"]([_pick_lane(d_ref[...], h) for h in s["heads"]])
        y_ref[...] = jnp.concatenate([ys[0] + ys[1], ys[2] + ys[3]], axis=1) + yoff + d4 * xs
        etot, _ = _etot(s["tots"])
        h_scr[...] = hp * etot + _dot((s["x"] * s["dte"]).astype(BF16), bb, "tn")

    in_specs = [
        pl.BlockSpec((q, 2 * LANES), lambda g, c: (c, g)),
        pl.BlockSpec((q, LANES), lambda g, c: (c, 4 + g)),
        pl.BlockSpec((q, LANES), lambda g, c: (c, 6 + g)),
        pl.BlockSpec((q, LANES), lambda g, c: (c, dt_blk)),
        pl.BlockSpec((1, LANES), lambda g, c: (0, 0)),
        pl.BlockSpec((1, LANES), lambda g, c: (0, 0)),
        pl.BlockSpec((1, LANES), lambda g, c: (0, 0)),
    ]
    out_specs = [
        pl.BlockSpec((q, 2 * LANES), lambda g, c: (c, g)),
        pl.BlockSpec((None, None, 2 * LANES, LANES), lambda g, c: (g, c, 0, 0)),
    ]
    return pl.pallas_call(
        body, name=name, grid=(2, nc), in_specs=in_specs, out_specs=out_specs,
        out_shape=[jax.ShapeDtypeStruct((lp, SSD_WIDTH), F32), jax.ShapeDtypeStruct((2, nc, 2 * LANES, LANES), F32)],
        scratch_shapes=[pltpu.VMEM((2 * LANES, LANES), F32)],
        compiler_params=_cp("arbitrary", "arbitrary"),
    )(xbc_c, xbc_c, xbc_c, u_main, bias_row, alog_row, d_row)


def ssd_bwd(xbc_c, u_main, bias_row, alog_row, d_row, hprev, dy, name):
    lp = xbc_c.shape[0]
    q = min(ROW_TILE, lp)
    nc = lp // q
    dt_blk = U_DT // LANES

    def body(xs_ref, b_ref, c_ref, dt_ref, bias_ref, alog_ref, d_ref, hp_ref, dy_ref,
             dxs_ref, db_ref, dc_ref, ddt_ref, pg_ref, dh_scr):
        g = pl.program_id(0)
        cc = pl.program_id(1)

        @pl.when(cc == 0)
        def _():
            dh_scr[...] = jnp.zeros_like(dh_scr)
            pg_ref[...] = jnp.zeros_like(pg_ref)

        xs = xs_ref[...]
        bb = b_ref[...].astype(BF16)
        cb_ = c_ref[...].astype(BF16)
        s = _ssd_common(g, xs, dt_ref[...], bias_ref[...], alog_ref[...], q)
        lane, lo, x, hm, heads = s["lane"], s["lo"], s["x"], s["hm"], s["heads"]
        d_y = dy_ref[...]
        hp = hp_ref[...]
        hpb = hp.astype(BF16)
        dhn = dh_scr[...]
        dhnb = dhn.astype(BF16)
        xd = x * s["dte"]
        gmat = _dot(cb_, bb, "nt")
        m32s = [gmat * s["lms"][i] for i in range(4)]
        xjs = [_half(x, i, lo).astype(BF16) for i in range(4)]
        dyjs = [_half(d_y, i, lo).astype(BF16) for i in range(4)]
        dxparts = [_dot(m32s[i].astype(BF16), dyjs[i], "tn") for i in range(4)]
        dms = [_dot(dyjs[i], xjs[i], "nt") for i in range(4)]
        dg = dms[0] * s["lms"][0] + dms[1] * s["lms"][1] + dms[2] * s["lms"][2] + dms[3] * s["lms"][3]
        wms = [dms[i] * m32s[i] for i in range(4)]
        row_part = [jnp.sum(wm, axis=1, keepdims=True) for wm in wms]
        col_part = [jnp.sum(wm, axis=0, keepdims=True) for wm in wms]
        dgb = dg.astype(BF16)
        yoff = _dot(cb_, hpb, "nt") * s["eacs"]
        d_t = (d_y * s["eacs"]).astype(BF16)
        d_c = _dot(dgb, bb) + _dot(d_t, hpb)
        d_hp = _dot(d_t, cb_, "tn")
        dxd = _dot(bb, dhnb, "nt")
        d_b = _dot(dgb, cb_, "tn") + _dot(xd.astype(BF16), dhnb)
        d_x = jnp.concatenate([dxparts[0] + dxparts[1], dxparts[2] + dxparts[3]], axis=1) + dxd * s["dte"]
        r = dxd * xd
        a_terms = d_y * yoff - r

        def hsum(arr):
            return [jnp.sum(jnp.where(hm[i], arr, 0.0), axis=1, keepdims=True) for i in range(4)]

        dacs, rs = hsum(a_terms), hsum(r)
        hh = dhn * hp
        sub = lax.broadcasted_iota(jnp.int32, (2 * LANES, LANES), 0)
        hsums = [jnp.sum(jnp.where(jnp.logical_and(sub >= 64 * i, sub < 64 * (i + 1)), hh, 0.0), keepdims=True) for i in range(4)]
        last = lax.broadcasted_iota(jnp.int32, (q, 1), 0) == q - 1
        etot, etots = _etot(s["tots"])
        ddacs = jnp.zeros((q, LANES), F32)
        for i, h in enumerate(heads):
            dtot = jnp.sum(rs[i], keepdims=True) + hsums[i] * etots[i]
            ddacs = ddacs + jnp.where(lane == h, dacs[i] + row_part[i] + jnp.where(last, dtot, 0.0), 0.0)
        subl = lax.broadcasted_iota(jnp.int32, (LANES, q), 0)
        cols_t = jnp.zeros((LANES, q), F32)
        for i, h in enumerate(heads):
            cols_t = cols_t + jnp.where(subl == h, col_part[i], 0.0)
        ddacs = ddacs - cols_t.T
        anti = (s["ri"] <= s["ci"]).astype(F32)
        da = _dot(anti, ddacs, "nn", HIGHEST)
        ddt_own = hsum(d_x * xs)
        ddt = da * s["a_row"]
        for i, h in enumerate(heads):
            ddt = ddt + jnp.where(lane == h, ddt_own[i], 0.0)
        draw = ddt * jax.nn.sigmoid(s["pre"])
        ddt_ref[...] = draw
        d4 = s["lanes4"]([_pick_lane(d_ref[...], h) for h in heads])
        dxs_ref[...] = d4 * d_y + d_x * s["dt4"]
        db_ref[...] = d_b
        dc_ref[...] = d_c
        dds = hsum(d_y * xs)
        lane1 = lane[0:1, :]
        dd_row = jnp.zeros((1, LANES), F32)
        for i, h in enumerate(heads):
            dd_row = dd_row + jnp.where(lane1 == h, jnp.sum(dds[i], keepdims=True), 0.0)
        dbias_row = jnp.sum(draw, axis=0, keepdims=True)
        dalog_row = jnp.sum(da * s["dt"], axis=0, keepdims=True) * s["a_row"]
        sub8 = lax.broadcasted_iota(jnp.int32, (SUBLANES, LANES), 0)
        pg_ref[...] += (jnp.where(sub8 == 0, dbias_row, 0.0) + jnp.where(sub8 == 1, dalog_row, 0.0)
                        + jnp.where(sub8 == 2, dd_row, 0.0))
        dh_scr[...] = d_hp + etot * dhn

    rc = lambda c: nc - 1 - c
    in_specs = [
        pl.BlockSpec((q, 2 * LANES), lambda g, c: (rc(c), g)),
        pl.BlockSpec((q, LANES), lambda g, c: (rc(c), 4 + g)),
        pl.BlockSpec((q, LANES), lambda g, c: (rc(c), 6 + g)),
        pl.BlockSpec((q, LANES), lambda g, c: (rc(c), dt_blk)),
        pl.BlockSpec((1, LANES), lambda g, c: (0, 0)),
        pl.BlockSpec((1, LANES), lambda g, c: (0, 0)),
        pl.BlockSpec((1, LANES), lambda g, c: (0, 0)),
        pl.BlockSpec((None, None, 2 * LANES, LANES), lambda g, c: (g, rc(c), 0, 0)),
        pl.BlockSpec((q, 2 * LANES), lambda g, c: (rc(c), g)),
    ]
    out_specs = [
        pl.BlockSpec((q, 2 * LANES), lambda g, c: (rc(c), g)),
        pl.BlockSpec((q, LANES), lambda g, c: (rc(c), g)),
        pl.BlockSpec((q, LANES), lambda g, c: (rc(c), g)),
        pl.BlockSpec((q, LANES), lambda g, c: (rc(c), g)),
        pl.BlockSpec((SUBLANES, LANES), lambda g, c: (g, 0)),
    ]
    per_group = jax.ShapeDtypeStruct((lp, 2 * LANES), F32)
    return pl.pallas_call(
        body, name=name, grid=(2, nc), in_specs=in_specs, out_specs=out_specs,
        out_shape=[jax.ShapeDtypeStruct((lp, SSD_WIDTH), F32), per_group, per_group, per_group,
                   jax.ShapeDtypeStruct((2 * SUBLANES, LANES), F32)],
        scratch_shapes=[pltpu.VMEM((2 * LANES, LANES), F32)],
        compiler_params=_cp("arbitrary", "arbitrary"),
    )(xbc_c, xbc_c, xbc_c, u_main, bias_row, alog_row, d_row, hprev, dy)


def _sb_blocks(qs, ks, r_runs, masked, bq, after_scores=None):
    ri = lax.broadcasted_iota(jnp.int32, (bq, bq), 0)
    ci = lax.broadcasted_iota(jnp.int32, (bq, bq), 1)
    tri_after = (ri > ci).astype(BF16)
    zs = [_dot(qj, kj, "nt") for qj, kj in zip(qs, ks)]
    extra = after_scores() if after_scores is not None else None
    sigs, ubs = [], []
    for z in zs:
        zb = z.astype(BF16)
        u = -(jnp.maximum(zb, 0) + jnp.log(1 + jnp.exp(-jnp.abs(zb))))
        sigs.append(jnp.exp(zb + u))
        if masked:
            u = jnp.where(ci < ri, u, jnp.zeros_like(u))
        ubs.append(u)
    afters = [_dot(ub, tri_after) for ub in ubs]
    usums = [after[:, 0:1] + ub[:, 0:1].astype(F32) for after, ub in zip(afters, ubs)]
    ws = []
    for sig, after, r_run in zip(sigs, afters, r_runs):
        w = sig * jnp.exp(after + r_run).astype(BF16)
        if masked:
            w = jnp.where(ci < ri, w, jnp.zeros_like(w))
        ws.append(w)
    return usums, sigs, ws, extra


def _split_heads(x, lo):
    out = []
    zero = jnp.zeros((x.shape[0], LANES), x.dtype)
    for p in range(2):
        xp = x[:, LANES * p:LANES * (p + 1)]
        out += [jnp.where(lo, xp, zero), jnp.where(lo, zero, xp)]
    return out


def _per_head(x):
    return [x[:, :LANES], x[:, :LANES], x[:, LANES:], x[:, LANES:]]


def _resident(shape, col):
    return pl.BlockSpec(shape, lambda i: (0, col), pipeline_mode=pl.Buffered(1))


def sb_attn_fwd(qkv, name):
    lp = qkv.shape[0]
    bq = min(ROW_TILE, lp)
    nq = lp // bq
    assert nq <= 64

    def body(q_ref, k_ref, v_ref, o_ref, rs_ref):
        qi = pl.program_id(0)
        lane = _lane((bq, LANES))
        lo = lane < 64
        qs = _split_heads(q_ref[...], lo)

        def step(kb, carry, masked):
            off = pl.multiple_of(kb * bq, bq)
            ks = _per_head(k_ref[pl.ds(off, bq), :])
            vs = _per_head(v_ref[pl.ds(off, bq), :])
            heads, rss = carry
            r_runs = [heads[h][1] for h in range(4)]
            rss = list(rss)
            for h in range(4):
                rss[h // 2] = jnp.where(lane == 64 * (h % 2) + kb, r_runs[h], rss[h // 2])
            usums, _, ws, _ = _sb_blocks(qs, ks, r_runs, masked, bq)
            pvs = [_dot(ws[h], vs[h]) for h in range(4)]
            out = tuple((heads[h][0] + pvs[h], r_runs[h] + usums[h]) for h in range(4))
            return out, tuple(rss)

        zero = (jnp.zeros((bq, LANES), F32), jnp.zeros((bq, 1), F32))
        zr = jnp.zeros((bq, LANES), F32)
        carry = step(qi, ((zero,) * 4, (zr, zr)), True)
        def several(t, c):
            for r in range(KEY_UNROLL):
                c = step(qi - 1 - r - KEY_UNROLL * t, c, False)
            return c

        carry = lax.fori_loop(0, qi // KEY_UNROLL, several, carry)
        rem = qi % KEY_UNROLL
        heads, rss = lax.fori_loop(0, rem, lambda t, c: step(rem - 1 - t, c, False), carry)
        o_ref[...] = jnp.concatenate([jnp.where(lo, heads[0][0], heads[1][0]), jnp.where(lo, heads[2][0], heads[3][0])], axis=1)
        rs_ref[...] = jnp.concatenate(list(rss), axis=1)

    blk = pl.BlockSpec((bq, 2 * LANES), lambda i: (i, 0))
    return pl.pallas_call(
        body, name=name, grid=(nq,),
        in_specs=[blk, _resident((lp, 2 * LANES), 1), _resident((lp, 2 * LANES), 2)],
        out_specs=[blk, blk],
        out_shape=[jax.ShapeDtypeStruct((lp, SB_WIDTH), F32), jax.ShapeDtypeStruct((lp, SB_WIDTH), F32)],
        compiler_params=_cp("arbitrary"),
    )(qkv, qkv, qkv)


def sb_attn_bwd(qkv, rs, d_o, name):
    lp = qkv.shape[0]
    bq = min(ROW_TILE, lp)
    nq = lp // bq

    def body(q_ref, k_ref, v_ref, rs_ref, do_ref, dq_ref, dk_ref, dv_ref):
        qi = pl.program_id(0)

        @pl.when(qi == 0)
        def _():
            dk_ref[...] = jnp.zeros_like(dk_ref)
            dv_ref[...] = jnp.zeros_like(dv_ref)

        lane = _lane((bq, LANES))
        lo = lane < 64
        qs = _split_heads(q_ref[...], lo)
        dos = _split_heads(do_ref[...].astype(BF16), lo)
        rs_blk = rs_ref[...]
        ri = lax.broadcasted_iota(jnp.int32, (bq, bq), 0)
        ci = lax.broadcasted_iota(jnp.int32, (bq, bq), 1)
        tbefore = (ri < ci).astype(BF16)

        def step(kb, carry, masked):
            off = pl.multiple_of(kb * bq, bq)
            ks = _per_head(k_ref[pl.ds(off, bq), :])
            vs = _per_head(v_ref[pl.ds(off, bq), :])
            r_rights = [jnp.sum(jnp.where(lane == 64 * (h % 2) + kb, rs_blk[:, LANES * (h // 2):LANES * (h // 2 + 1)], 0.0),
                                axis=1, keepdims=True) for h in range(4)]
            _, sigs, wbs, dws = _sb_blocks(qs, ks, r_rights, masked, bq,
                                           after_scores=lambda: [_dot(dos[h], vs[h], "nt") for h in range(4)])
            gs = [wbs[h].astype(F32) * dws[h] for h in range(4)]
            gbs = [g.astype(BF16) for g in gs]
            gbefores = [_dot(gb, tbefore) for gb in gbs]
            dv_acc = [_dot(wbs[2 * p], dos[2 * p], "tn") + _dot(wbs[2 * p + 1], dos[2 * p + 1], "tn") for p in range(2)]
            dzbs = []
            for h in range(4):
                dz = gs[h] - sigs[h].astype(F32) * (gs[h] + gbefores[h] + carry[h][1])
                if masked:
                    dz = jnp.where(ci < ri, dz, 0.0)
                dzbs.append(dz.astype(BF16))
            dqs = [_dot(dzbs[h], ks[h]) for h in range(4)]
            dk_acc = [_dot(dzbs[2 * p], qs[2 * p], "tn") + _dot(dzbs[2 * p + 1], qs[2 * p + 1], "tn") for p in range(2)]
            dk_ref[pl.ds(off, bq), :] += jnp.concatenate(dk_acc, axis=1)
            dv_ref[pl.ds(off, bq), :] += jnp.concatenate(dv_acc, axis=1)
            return tuple((carry[h][0] + dqs[h], carry[h][1] + jnp.sum(gs[h], axis=1, keepdims=True)) for h in range(4))

        zero = (jnp.zeros((bq, LANES), F32), jnp.zeros((bq, 1), F32))
        def several(t, c):
            for r in range(KEY_UNROLL):
                c = step(KEY_UNROLL * t + r, c, False)
            return c

        carry = lax.fori_loop(0, qi // KEY_UNROLL, several, (zero,) * 4)
        carry = lax.fori_loop(qi - qi % KEY_UNROLL, qi, lambda t, c: step(t, c, False), carry)
        carry = step(qi, carry, True)
        dq_ref[...] = jnp.concatenate([jnp.where(lo, carry[0][0], carry[1][0]), jnp.where(lo, carry[2][0], carry[3][0])],
                                      axis=1).astype(dq_ref.dtype)

    blk = pl.BlockSpec((bq, 2 * LANES), lambda i: (i, 0))
    return pl.pallas_call(
        body, name=name, grid=(nq,),
        in_specs=[blk, _resident((lp, 2 * LANES), 1), _resident((lp, 2 * LANES), 2), blk, blk],
        out_specs=[blk, _resident((lp, 2 * LANES), 0), _resident((lp, 2 * LANES), 0)],
        out_shape=[jax.ShapeDtypeStruct((lp, SB_WIDTH), BF16), jax.ShapeDtypeStruct((lp, SB_WIDTH), F32),
                   jax.ShapeDtypeStruct((lp, SB_WIDTH), F32)],
        compiler_params=_cp("arbitrary"),
    )(qkv, qkv, qkv, rs, d_o)


def _mla_masks(bq):
    lane = _lane((bq, 2 * LANES))
    out = []
    for h in range(4):
        j = h % 2
        nope = jnp.logical_and(lane >= 64 * j, lane < 64 * (j + 1))
        rope = jnp.logical_and(lane >= LANES + MLA_ROPE * h, lane < LANES + MLA_ROPE * (h + 1))
        out.append(jnp.logical_or(nope, rope))
    return out


def _mla_split_q(q, masks):
    zero = jnp.zeros((q.shape[0], 2 * LANES), q.dtype)
    return [jnp.where(masks[h], q[:, 2 * LANES * (h // 2):2 * LANES * (h // 2 + 1)], zero) for h in range(4)]


def _mla_per_head_k(k):
    return [k[:, :2 * LANES], k[:, :2 * LANES], k[:, 2 * LANES:], k[:, 2 * LANES:]]


def mla_attn_fwd(qc, kc, v, name):
    lp = qc.shape[0]
    bq = min(ROW_TILE, lp)
    nq = lp // bq

    def body(q_ref, k_ref, v_ref, o_ref, lse_ref):
        qi = pl.program_id(0)
        qs = _mla_split_q(q_ref[...], _mla_masks(bq))
        lo = _lane((bq, LANES)) < 64
        ri = lax.broadcasted_iota(jnp.int32, (bq, bq), 0)
        ci = lax.broadcasted_iota(jnp.int32, (bq, bq), 1)

        def blocks(kbs, carry, masked):
            offs = [pl.multiple_of(kb * bq, bq) for kb in kbs]
            ks = [_mla_per_head_k(k_ref[pl.ds(o, bq), :]) for o in offs]
            ones = jnp.ones((bq, LANES), BF16)
            vs = []
            for o in offs:
                vp = _per_head(v_ref[pl.ds(o, bq), :])
                vs.append([jnp.where(lo, vp[h], ones) if h % 2 == 0 else jnp.where(lo, ones, vp[h]) for h in range(4)])
            ss = [[_dot(qs[h], ks[b][h], "nt") for h in range(4)] for b in range(len(kbs))]
            if masked:
                ss = [[jnp.where(ci <= ri, s, NEG) for s in row] for row in ss]
            prs, alphas, ms = [], [], []
            for h in range(4):
                top = ss[0][h]
                for b in range(1, len(kbs)):
                    top = jnp.maximum(top, ss[b][h])
                m_new = jnp.maximum(carry[h][1], jnp.max(top, axis=1, keepdims=True))
                alphas.append(jnp.exp(carry[h][1] - m_new))
                ms.append(m_new)
                prs.append([jnp.exp(ss[b][h] - m_new).astype(BF16) for b in range(len(kbs))])
            out = []
            for h in range(4):
                acc = carry[h][0] * alphas[h]
                for b in range(len(kbs)):
                    acc = acc + _dot(prs[h][b], vs[b][h])
                out.append((acc, ms[h]))
            return tuple(out)

        zero = (jnp.zeros((bq, LANES), F32), jnp.full((bq, 1), NEG, F32))
        carry = blocks([qi], (zero,) * 4, True)
        carry = lax.fori_loop(0, qi // KEY_UNROLL,
                              lambda t, c: blocks([qi - 1 - r - KEY_UNROLL * t for r in range(KEY_UNROLL)], c, False), carry)
        rem = qi % KEY_UNROLL
        carry = lax.fori_loop(0, rem, lambda t, c: blocks([rem - 1 - t], c, False), carry)
        outs, lses = [], []
        for h in range(4):
            acc, m = carry[h]
            l = acc[:, 64:65] if h % 2 == 0 else acc[:, 0:1]
            outs.append(acc / l)
            lses.append(m + jnp.log(l))
        o_ref[...] = jnp.concatenate([jnp.where(lo, outs[0], outs[1]), jnp.where(lo, outs[2], outs[3])], axis=1)
        lse_ref[...] = jnp.concatenate([jnp.where(lo, lses[0], lses[1]), jnp.where(lo, lses[2], lses[3])], axis=1)

    blk = pl.BlockSpec((bq, 2 * LANES), lambda i: (i, 0))
    return pl.pallas_call(
        body, name=name, grid=(nq,),
        in_specs=[pl.BlockSpec((bq, 4 * LANES), lambda i: (i, 0)), _resident((lp, 4 * LANES), 0), _resident((lp, 2 * LANES), 0)],
        out_specs=[blk, blk],
        out_shape=[jax.ShapeDtypeStruct((lp, 2 * LANES), F32), jax.ShapeDtypeStruct((lp, 2 * LANES), F32)],
        compiler_params=_cp("arbitrary"),
    )(qc, kc, v)


def mla_attn_bwd(qc, kc, v, o, lse, d_o, name):
    lp = qc.shape[0]
    bq = min(ROW_TILE, lp)
    nq = lp // bq

    def body(q_ref, k_ref, v_ref, o_ref, lse_ref, do_ref, dq_ref, dk_ref, dv_ref):
        qi = pl.program_id(0)

        @pl.when(qi == 0)
        def _():
            dk_ref[...] = jnp.zeros_like(dk_ref)
            dv_ref[...] = jnp.zeros_like(dv_ref)

        d_o = do_ref[...]
        masks = _mla_masks(bq)
        qs = _mla_split_q(q_ref[...], masks)
        lo = _lane((bq, LANES)) < 64
        dos = _split_heads(d_o.astype(BF16), lo)
        od = o_ref[...] * d_o
        lse_blk = lse_ref[...]
        delta, lses = [], []
        for h in range(4):
            odp = od[:, LANES * (h // 2):LANES * (h // 2 + 1)]
            delta.append(jnp.sum(jnp.where(lo, odp, 0.0) if h % 2 == 0 else jnp.where(lo, 0.0, odp), axis=1, keepdims=True))
            c0 = LANES * (h // 2) + 64 * (h % 2)
            lses.append(lse_blk[:, c0:c0 + 1])
        ri = lax.broadcasted_iota(jnp.int32, (bq, bq), 0)
        ci = lax.broadcasted_iota(jnp.int32, (bq, bq), 1)

        def step(kb, carry, masked):
            off = pl.multiple_of(kb * bq, bq)
            ks = _mla_per_head_k(k_ref[pl.ds(off, bq), :])
            vs = _per_head(v_ref[pl.ds(off, bq), :])
            ss = [_dot(qs[h], ks[h], "nt") for h in range(4)]
            dps = [_dot(dos[h], vs[h], "nt") for h in range(4)]
            prbs, dss = [], []
            for h in range(4):
                s = ss[h]
                if masked:
                    s = jnp.where(ci <= ri, s, NEG)
                pr = jnp.exp(s - lses[h])
                prbs.append(pr.astype(BF16))
                dss.append((pr * (dps[h] - delta[h])).astype(BF16))
            dv_acc = [_dot(prbs[2 * p], dos[2 * p], "tn") + _dot(prbs[2 * p + 1], dos[2 * p + 1], "tn") for p in range(2)]
            dqs = [_dot(dss[h], ks[h]) for h in range(4)]
            dk_acc = [_dot(dss[2 * p], qs[2 * p], "tn") + _dot(dss[2 * p + 1], qs[2 * p + 1], "tn") for p in range(2)]
            dk_ref[pl.ds(off, bq), :] += jnp.concatenate(dk_acc, axis=1)
            dv_ref[pl.ds(off, bq), :] += jnp.concatenate(dv_acc, axis=1)
            return tuple(carry[h] + dqs[h] for h in range(4))

        zero = jnp.zeros((bq, 2 * LANES), F32)
        carry = step(qi, (zero,) * 4, True)
        def several(t, c):
            for r in range(KEY_UNROLL):
                c = step(qi - 1 - r - KEY_UNROLL * t, c, False)
            return c

        carry = lax.fori_loop(0, qi // KEY_UNROLL, several, carry)
        rem = qi % KEY_UNROLL
        carry = lax.fori_loop(0, rem, lambda t, c: step(rem - 1 - t, c, False), carry)
        dq_ref[...] = jnp.concatenate([jnp.where(masks[0], carry[0], 0.0) + jnp.where(masks[1], carry[1], 0.0),
                                       jnp.where(masks[2], carry[2], 0.0) + jnp.where(masks[3], carry[3], 0.0)], axis=1)

    blk = pl.BlockSpec((bq, 2 * LANES), lambda i: (i, 0))
    wide = pl.BlockSpec((bq, 4 * LANES), lambda i: (i, 0))
    return pl.pallas_call(
        body, name=name, grid=(nq,),
        in_specs=[wide, _resident((lp, 4 * LANES), 0), _resident((lp, 2 * LANES), 0), blk, blk, blk],
        out_specs=[wide, _resident((lp, 4 * LANES), 0), _resident((lp, 2 * LANES), 0)],
        out_shape=[jax.ShapeDtypeStruct((lp, 4 * LANES), F32), jax.ShapeDtypeStruct((lp, 4 * LANES), F32),
                   jax.ShapeDtypeStruct((lp, 2 * LANES), F32)],
        compiler_params=_cp("arbitrary"),
    )(qc, kc, v, o, lse, d_o)


def _mix_out(y_pre, z, o_sb, o_mla, g_ssd, g_sb, g_mla):
    return jnp.concatenate([_rms(y_pre * _silu(z), g_ssd), _rms(o_sb, g_sb), _rms(o_mla, g_mla)], axis=1)


def _ffn_act(up_a, up_b, halo_a, halo_b, w_a, w_b, b_a, b_b, i):
    ca = conv_fwd(up_a, halo_a, w_a, i) + b_a
    cb_ = conv_fwd(up_b, halo_b, w_b, i) + b_b
    return ca, cb_


def layer_fwd(h, w, cs, sn, l):
    lp = h.shape[0]
    nm = f"l{l}_"
    hn = rowwise(lambda i, n, x, g: _rms(x, g), [RI(h), PA(w["norm_mix_g"])], [RO(D_MODEL, BF16)], nm + "rms_mix", lp)[0]
    u = mm(hn, w["w_main"], "nn", F32, nm + "in_main")
    qkv = mm(hn, w["w_sb"], "nn", BF16, nm + "in_sb")
    xbc_c = rowwise(lambda i, n, x, hl, cw, cb_: _silu(conv_fwd(x, hl, cw, i) + cb_),
                    [RI(u, SSD_XBC, 0), HP(u, SSD_XBC, 0), PA(w["ssd_conv_w"]), PA(w["ssd_conv_b"])],
                    [RO(SSD_XBC, F32)], nm + "ssd_conv", lp)[0]
    y_pre, hprev = ssd_fwd(xbc_c, u, w["dt_bias"], w["a_log"], w["d_skip"], nm + "ssd_fwd")
    o_sb, rs_sb = sb_attn_fwd(qkv, nm + "sb_fwd")
    qn, kvn = rowwise(lambda i, n, qa, ckv, gq, gkv: (_rms(qa, gq, MLA_Q_RANK), _rms(ckv, gkv)),
                      [RI(u, 256, U_QA // 256), RI(u, LANES, U_CKV // LANES), PA(w["q_norm_g"]), PA(w["kv_norm_g"])],
                      [RO(256, BF16), RO(LANES, BF16)], nm + "mla_rms", lp)
    qf = mm(qn, w["w_uq"], "nn", F32, nm + "mla_uq")
    kvf = mm(kvn, w["w_ukv"], "nn", F32, nm + "mla_ukv")

    def pack(i, n, qf_, kvf_, kr4, cos, sin):
        qf_ = qf_ * MLA_SCALE
        qr = qf_[:, 256:384]
        qr = qr * cos + rope_rot(qr) * sin
        kr = kr4 * cos + rope_rot(kr4) * sin
        qc = jnp.concatenate([qf_[:, 0:128], qr, qf_[:, 128:256], qr], axis=1)
        kc = jnp.concatenate([kvf_[:, 0:128], kr, kvf_[:, 128:256], kr], axis=1)
        return qc, kc, kvf_[:, 256:512]

    qc, kc, vv = rowwise(pack, [RI(qf), RI(kvf), RI(u, LANES, U_KR4 // LANES), RI(cs), RI(sn)],
                         [RO(512, BF16), RO(512, BF16), RO(256, BF16)], nm + "mla_pack", lp)
    o_mla, lse = mla_attn_fwd(qc, kc, vv, nm + "mla_fwd")
    cat = rowwise(lambda i, n, *a: _mix_out(*a),
                  [RI(y_pre), RI(u, SSD_WIDTH, U_Z // SSD_WIDTH), RI(o_sb), RI(o_mla),
                   PA(w["ssd_norm_g"]), PA(w["sb_norm_g"]), PA(w["mla_norm_g"])],
                  [RO(D_MODEL, BF16)], nm + "mix_out", lp)[0]
    h_mid = mm(cat, w["w_out"], "nn", F32, nm + "out_proj", add=h)
    hn2 = rowwise(lambda i, n, x, g: _rms(x, g), [RI(h_mid), PA(w["norm_ffn_g"])], [RO(D_MODEL, BF16)], nm + "rms_ffn", lp)[0]
    up_a = mm(hn2, w["w_up_a"], "nn", F32, nm + "up_a")
    up_b = mm(hn2, w["w_up_b"], "nn", F32, nm + "up_b")
    wc = 1408

    def act(i, n, ua, ub, ha, hb_, wa, wb, ba, bb_):
        ca, cb_ = _ffn_act(ua, ub, ha, hb_, wa, wb, ba, bb_, i)
        return _silu(ca) * cb_

    a_t = rowwise(act, [RI(up_a, wc, 0, True), RI(up_b, wc, 0, True), HP(up_a, wc, 0, True), HP(up_b, wc, 0, True),
                        PA(w["ffn_conv_w_a"], wc, 0, True), PA(w["ffn_conv_w_b"], wc, 0, True),
                        PA(w["ffn_conv_b_a"], wc, 0, True), PA(w["ffn_conv_b_b"], wc, 0, True)],
                  [RO(D_FF, BF16, wc, True)], nm + "ffn_act", lp, ncol=D_FF // wc)[0]
    h_out = mm(a_t, w["w_down"], "nn", F32, nm + "down", add=h_mid)
    saved = dict(h=h, hn=hn, u=u, qkv=qkv, xbc_c=xbc_c, y_pre=y_pre, hprev=hprev, o_sb=o_sb, rs_sb=rs_sb, qn=qn, kvn=kvn,
                 qc=qc, kc=kc, vv=vv, o_mla=o_mla, lse=lse, cat=cat, h_mid=h_mid, hn2=hn2, up_a=up_a, up_b=up_b, a_t=a_t)
    return h_out, saved


def layer_bwd(dh_out, w, s, cs, sn, l):
    lp = dh_out.shape[0]
    nm = f"l{l}b_"
    g = {}
    wc = 1408
    ncolf = D_FF // wc
    g["w_down"] = mm(s["a_t"], dh_out, "tn", BF16, nm + "dw_down")
    d_act = mm(dh_out, w["w_down"], "nt", F32, nm + "d_act")

    def act_bwd(i, n, ua, ub, ha, hb_, wa, wb, ba, bb_, da_):
        ca, cb_ = _ffn_act(ua, ub, ha, hb_, wa, wb, ba, bb_, i)
        sg = jax.nn.sigmoid(ca)
        dca = da_ * cb_ * (sg * (1.0 + ca * (1.0 - sg)))
        dcb = da_ * (ca * sg)
        return (dca, dcb, conv_bwd_w(dca, ua, ha, i, FFN_CONV), conv_bwd_w(dcb, ub, hb_, i, FFN_CONV),
                jnp.sum(dca, axis=0, keepdims=True), jnp.sum(dcb, axis=0, keepdims=True))

    dca, dcb, g["ffn_conv_w_a"], g["ffn_conv_w_b"], g["ffn_conv_b_a"], g["ffn_conv_b_b"] = rowwise(
        act_bwd, [RI(s["up_a"], wc, 0, True), RI(s["up_b"], wc, 0, True), HP(s["up_a"], wc, 0, True),
                  HP(s["up_b"], wc, 0, True), PA(w["ffn_conv_w_a"], wc, 0, True), PA(w["ffn_conv_w_b"], wc, 0, True),
                  PA(w["ffn_conv_b_a"], wc, 0, True), PA(w["ffn_conv_b_b"], wc, 0, True), RI(d_act, wc, 0, True)],
        [RO(D_FF, F32, wc, True), RO(D_FF, F32, wc, True), AO(FFN_CONV, D_FF, wc, True), AO(FFN_CONV, D_FF, wc, True),
         AO(1, D_FF, wc, True), AO(1, D_FF, wc, True)], nm + "ffn_act_bwd", lp, ncol=ncolf)

    def conv_t(i, n, da_, db_, ha, hb_, wa, wb):
        return conv_bwd_data(da_, ha, wa, i, n), conv_bwd_data(db_, hb_, wb, i, n)

    dup_a, dup_b = rowwise(conv_t, [RI(dca, wc, 0, True), RI(dcb, wc, 0, True), HN(dca, wc, 0, True), HN(dcb, wc, 0, True),
                                    PA(w["ffn_conv_w_a"], wc, 0, True), PA(w["ffn_conv_w_b"], wc, 0, True)],
                           [RO(D_FF, BF16, wc, True), RO(D_FF, BF16, wc, True)], nm + "ffn_conv_t", lp, ncol=ncolf)
    g["w_up_a"] = mm(s["hn2"], dup_a, "tn", BF16, nm + "dw_up_a")
    g["w_up_b"] = mm(s["hn2"], dup_b, "tn", BF16, nm + "dw_up_b")
    dhn2 = mm(dup_a, w["w_up_a"], "nt", F32, nm + "dhn2_a")
    dhn2 = mm(dup_b, w["w_up_b"], "nt", F32, nm + "dhn2_b", add=dhn2)

    def rms_bwd(i, n, x, gg, dy, dres):
        _, vjp = jax.vjp(_rms, x, gg)
        dx, dg = vjp(dy)
        return dres + dx, dg

    dh_mid, g["norm_ffn_g"] = rowwise(rms_bwd, [RI(s["h_mid"]), PA(w["norm_ffn_g"]), RI(dhn2), RI(dh_out)],
                                      [RO(D_MODEL, F32), AO(1, D_MODEL)], nm + "rms_ffn_bwd", lp)
    g["w_out"] = mm(s["cat"], dh_mid, "tn", BF16, nm + "dw_out")
    d_cat = mm(dh_mid, w["w_out"], "nt", F32, nm + "d_cat")
    u = s["u"]

    def mix_bwd(i, n, y_pre, z, o_sb, o_mla, g1, g2, g3, dcat):
        _, vjp = jax.vjp(_mix_out, y_pre, z, o_sb, o_mla, g1, g2, g3)
        return vjp(dcat)

    dy_pre, dz, do_sb, do_mla, g["ssd_norm_g"], g["sb_norm_g"], g["mla_norm_g"] = rowwise(
        mix_bwd, [RI(s["y_pre"]), RI(u, SSD_WIDTH, U_Z // SSD_WIDTH), RI(s["o_sb"]), RI(s["o_mla"]),
                  PA(w["ssd_norm_g"]), PA(w["sb_norm_g"]), PA(w["mla_norm_g"]), RI(d_cat)],
        [RO(SSD_WIDTH, F32), RO(SSD_WIDTH, BF16), RO(SB_WIDTH, F32), RO(256, F32),
         AO(1, SSD_WIDTH), AO(1, SB_WIDTH), AO(1, 256)], nm + "mix_out_bwd", lp)
    dxs, dbp, dcp, ddtp, pg = ssd_bwd(s["xbc_c"], u, w["dt_bias"], w["a_log"], w["d_skip"], s["hprev"], dy_pre, nm + "ssd_bwd")
    pg = pg.reshape(2, SUBLANES, LANES).sum(axis=0)
    g["dt_bias"], g["a_log"], g["d_skip"] = pg[0:1], pg[1:2], pg[2:3]

    def conv4_bwd(i, n, x, hl, cw, cb_, dxs_, dbp_, dcp_, ddtp_):
        pre = conv_fwd(x, hl, cw, i) + cb_
        d_out = jnp.concatenate([dxs_, dbp_, dcp_], axis=1)
        sg = jax.nn.sigmoid(pre)
        d_pre = d_out * (sg * (1.0 + pre * (1.0 - sg)))
        ddt = ddtp_[:, 0:128] + ddtp_[:, 128:256]
        return d_pre, ddt, conv_bwd_w(d_pre, x, hl, i, SSD_CONV), jnp.sum(d_pre, axis=0, keepdims=True)

    d_pre, ddt, g["ssd_conv_w"], g["ssd_conv_b"] = rowwise(
        conv4_bwd, [RI(u, SSD_XBC, 0), HP(u, SSD_XBC, 0), PA(w["ssd_conv_w"]), PA(w["ssd_conv_b"]),
                    RI(dxs), RI(dbp), RI(dcp), RI(ddtp)],
        [RO(SSD_XBC, F32), RO(LANES, BF16), AO(SSD_CONV, SSD_XBC), AO(1, SSD_XBC)], nm + "ssd_conv_bwd", lp)
    d_xbc = rowwise(lambda i, n, d, hn_, cw: conv_bwd_data(d, hn_, cw, i, n),
                    [RI(d_pre), HN(d_pre), PA(w["ssd_conv_w"])], [RO(SSD_XBC, BF16)], nm + "ssd_conv_t", lp)[0]
    dq_sb, dk_sb, dv_sb = sb_attn_bwd(s["qkv"], s["rs_sb"], do_sb, nm + "sb_bwd")
    dqkv = jnp.concatenate([dq_sb, dk_sb.astype(BF16), dv_sb.astype(BF16)], axis=1)
    dqc, dkc, dvv = mla_attn_bwd(s["qc"], s["kc"], s["vv"], s["o_mla"], s["lse"], do_mla, nm + "mla_bwd")

    def unpack(i, n, dqc_, dkc_, dvv_, cos, sin):
        dqr = dqc_[:, 128:256] + dqc_[:, 384:512]
        dqr = dqr * cos + rope_rot_t(dqr * sin)
        dkr = dkc_[:, 128:256] + dkc_[:, 384:512]
        dkr = dkr * cos + rope_rot_t(dkr * sin)
        dq = jnp.concatenate([dqc_[:, 0:128], dqc_[:, 256:384], dqr], axis=1) * MLA_SCALE
        dkv = jnp.concatenate([dkc_[:, 0:128], dkc_[:, 256:384], dvv_], axis=1)
        return dq, dkv, dkr

    dq, dkv, dkr4 = rowwise(unpack, [RI(dqc), RI(dkc), RI(dvv), RI(cs), RI(sn)],
                            [RO(384, BF16), RO(512, BF16), RO(LANES, BF16)], nm + "mla_unpack", lp)
    g["w_uq"] = mm(s["qn"], dq, "tn", F32, nm + "dw_uq")
    g["w_ukv"] = mm(s["kvn"], dkv, "tn", F32, nm + "dw_ukv")
    dqn = mm(dq, w["w_uq"], "nt", F32, nm + "dqn")
    dkvn = mm(dkv, w["w_ukv"], "nt", F32, nm + "dkvn")

    def mla_rms_bwd(i, n, qa, ckv, gq, gkv, dqn_, dkvn_):
        _, vjp = jax.vjp(lambda a, b, c, d: (_rms(a, c, MLA_Q_RANK), _rms(b, d)), qa, ckv, gq, gkv)
        return vjp((dqn_, dkvn_))

    dqa, dckv, g["q_norm_g"], g["kv_norm_g"] = rowwise(
        mla_rms_bwd, [RI(u, 256, U_QA // 256), RI(u, LANES, U_CKV // LANES), PA(w["q_norm_g"]), PA(w["kv_norm_g"]),
                      RI(dqn), RI(dkvn)],
        [RO(256, BF16), RO(LANES, BF16), AO(1, 256), AO(1, LANES)], nm + "mla_rms_bwd", lp)
    du = jnp.concatenate([d_xbc, dz, dqa, dckv, dkr4, ddt, jnp.zeros((lp, LANES), BF16)], axis=1)
    g["w_main"] = mm(s["hn"], du, "tn", F32, nm + "dw_main")
    g["w_sb"] = mm(s["hn"], dqkv, "tn", F32, nm + "dw_sb")
    dhn = mm(du, w["w_main"], "nt", F32, nm + "dhn_main")
    dhn = mm(dqkv, w["w_sb"], "nt", F32, nm + "dhn_sb", add=dhn)
    dh_in, g["norm_mix_g"] = rowwise(rms_bwd, [RI(s["h"]), PA(w["norm_mix_g"]), RI(dhn), RI(dh_mid)],
                                     [RO(D_MODEL, F32), AO(1, D_MODEL)], nm + "rms_mix_bwd", lp)
    return dh_in, g


_IN_CUTS = np.cumsum((512, 1024, 8, 256, 256, 256, 192, 128, 32))


def _pad_cols(a, n):
    return jnp.pad(a, ((0, 0), (0, n - a.shape[1])))


def prep_layer_weights(full, l):
    w_in = full["w_in"][l]
    c = _IN_CUTS
    z, xbc, dtr = w_in[:, :c[0]], w_in[:, c[0]:c[1]], w_in[:, c[1]:c[2]]
    q_sb, k_sb, v_sb = w_in[:, c[2]:c[3]], w_in[:, c[3]:c[4]], w_in[:, c[4]:c[5]]
    q_a, c_kv, k_r = w_in[:, c[5]:c[6]], w_in[:, c[6]:c[7]], w_in[:, c[7]:c[8]]
    w_main = jnp.concatenate([xbc, z, _pad_cols(q_a, 256), c_kv, k_r, k_r, k_r, k_r, _pad_cols(dtr, 256)], axis=1)
    assert w_main.shape[1] == U_MAIN
    row = lambda v, n=None: _pad_cols(v.reshape(1, -1).astype(F32), v.size if n is None else n)
    uq = full["mla_w_uq"][l].reshape(MLA_Q_RANK, 4, 96)
    w_uq = jnp.concatenate([uq[:, :, :64].reshape(MLA_Q_RANK, 256), uq[:, :, 64:].reshape(MLA_Q_RANK, 128)], axis=1)
    w_uq = jnp.pad(w_uq, ((0, 256 - MLA_Q_RANK), (0, 0)))
    ukv = full["mla_w_ukv"][l].reshape(MLA_KV_RANK, 4, 128)
    w_ukv = jnp.concatenate([ukv[:, :, :64].reshape(MLA_KV_RANK, 256), ukv[:, :, 64:].reshape(MLA_KV_RANK, 256)], axis=1)
    return dict(
        norm_mix_g=row(full["norm_mix_g"][l]), w_main=w_main, w_sb=jnp.concatenate([q_sb * SB_SCALE, k_sb, v_sb], axis=1),
        ssd_conv_w=full["ssd_conv_w"][l], ssd_conv_b=row(full["ssd_conv_b"][l]),
        dt_bias=row(full["ssd_dt_bias"][l], LANES), a_log=row(full["ssd_a_log"][l], LANES), d_skip=row(full["ssd_d"][l], LANES),
        ssd_norm_g=row(full["ssd_norm_g"][l]), sb_norm_g=row(full["sb_norm_g"][l]),
        q_norm_g=row(full["mla_q_norm_g"][l], 256), kv_norm_g=row(full["mla_kv_norm_g"][l]),
        w_uq=w_uq, w_ukv=w_ukv, mla_norm_g=row(full["mla_norm_g"][l]),
        w_out=full["w_out"][l], norm_ffn_g=row(full["norm_ffn_g"][l]),
        w_up_a=full["ffn_w_up"][l][:, :D_FF], w_up_b=full["ffn_w_up"][l][:, D_FF:],
        ffn_conv_w_a=full["ffn_conv_w"][l][:, :D_FF], ffn_conv_w_b=full["ffn_conv_w"][l][:, D_FF:],
        ffn_conv_b_a=row(full["ffn_conv_b"][l][:D_FF]), ffn_conv_b_b=row(full["ffn_conv_b"][l][D_FF:]),
        w_down=full["ffn_w_down"][l],
    )


def unprep_layer_grads(g):
    wm = g["w_main"]
    xbc, z = wm[:, U_XBC:U_XBC + 1024], wm[:, U_Z:U_Z + 512]
    q_a, c_kv = wm[:, U_QA:U_QA + MLA_Q_RANK], wm[:, U_CKV:U_CKV + 128]
    k_r = (wm[:, U_KR4:U_KR4 + 32] + wm[:, U_KR4 + 32:U_KR4 + 64] + wm[:, U_KR4 + 64:U_KR4 + 96] + wm[:, U_KR4 + 96:U_KR4 + 128])
    dtr = wm[:, U_DT:U_DT + SSD_HEADS]
    w_sb = g["w_sb"]
    w_in = jnp.concatenate([z, xbc, dtr, w_sb[:, :SB_WIDTH] * SB_SCALE, w_sb[:, SB_WIDTH:], q_a, c_kv, k_r], axis=1)
    guq = g["w_uq"][:MLA_Q_RANK]
    guq = jnp.concatenate([guq[:, :256].reshape(MLA_Q_RANK, 4, 64), guq[:, 256:].reshape(MLA_Q_RANK, 4, 32)], axis=2)
    gukv = g["w_ukv"]
    gukv = jnp.concatenate([gukv[:, :256].reshape(MLA_KV_RANK, 4, 64), gukv[:, 256:].reshape(MLA_KV_RANK, 4, 64)], axis=2)
    return dict(
        norm_mix_g=g["norm_mix_g"][0], w_in=w_in, ssd_conv_w=g["ssd_conv_w"], ssd_conv_b=g["ssd_conv_b"][0],
        ssd_dt_bias=g["dt_bias"][0, :SSD_HEADS], ssd_a_log=g["a_log"][0, :SSD_HEADS], ssd_d=g["d_skip"][0, :SSD_HEADS],
        ssd_norm_g=g["ssd_norm_g"][0], sb_norm_g=g["sb_norm_g"][0], mla_q_norm_g=g["q_norm_g"][0, :MLA_Q_RANK],
        mla_kv_norm_g=g["kv_norm_g"][0], mla_w_uq=guq.reshape(MLA_Q_RANK, 384), mla_w_ukv=gukv.reshape(MLA_KV_RANK, 512),
        mla_norm_g=g["mla_norm_g"][0], w_out=g["w_out"], norm_ffn_g=g["norm_ffn_g"][0],
        ffn_w_up=jnp.concatenate([g["w_up_a"], g["w_up_b"]], axis=1),
        ffn_conv_w=jnp.concatenate([g["ffn_conv_w_a"], g["ffn_conv_w_b"]], axis=1),
        ffn_conv_b=jnp.concatenate([g["ffn_conv_b_a"][0], g["ffn_conv_b_b"][0]], axis=0),
        ffn_w_down=g["w_down"],
    )


def rope_tables(lp):
    pos = jnp.arange(lp, dtype=F32)
    inv = 1.0 / (ROPE_BASE ** (jnp.arange(0, MLA_ROPE, 2, dtype=F32) / MLA_ROPE))
    ang = pos[:, None] * inv[None, :]
    ang = jnp.concatenate([ang, ang] * 4, axis=-1)
    return jnp.cos(ang), jnp.sin(ang)


def local_step(x_seq, target, full):
    seq = x_seq.shape[0]
    length = seq + N_META
    lp = -(-length // ROW_TILE) * ROW_TILE
    cs, sn = rope_tables(lp)
    h = jnp.concatenate([full["meta_tokens"].astype(F32), x_seq, jnp.zeros((lp - length, D_MODEL), F32)], axis=0)
    tgt = jnp.pad(target, ((N_META, lp - length), (0, 0)))
    ws = [prep_layer_weights(full, l) for l in range(DEPTH)]
    saved = []
    for l in range(DEPTH):
        h, s = layer_fwd(h, ws[l], cs, sn, l)
        saved.append(s)
    fg = full["final_norm_g"].reshape(1, D_MODEL).astype(F32)
    tm = min(ROW_TILE, lp)

    def loss_fn(i, n, x, g, t):
        rows = _rows_iota(x) + i * tm
        valid = jnp.logical_and(rows >= N_META, rows < length)

        def f(x_, g_):
            err = jnp.where(valid, _rms(x_, g_) - t, 0.0)
            return 0.5 * jnp.sum(err * err) * (1.0 / D_MODEL)

        val, (dx, dg) = jax.value_and_grad(f, argnums=(0, 1))(x, g)
        return dx, jnp.full((1, LANES), val, F32), dg

    dh, loss_row, g_final = rowwise(loss_fn, [RI(h), PA(fg), RI(tgt)], [RO(D_MODEL, F32), AO(1, LANES), AO(1, D_MODEL)],
                                    "loss_head", lp)
    grads = {}
    per_layer = [None] * DEPTH
    for l in reversed(range(DEPTH)):
        dh, g = layer_bwd(dh, ws[l], saved[l], cs, sn, l)
        per_layer[l] = unprep_layer_grads(g)
    for k in per_layer[0]:
        grads[k] = jnp.stack([per_layer[l][k] for l in range(DEPTH)], axis=0)
    grads["final_norm_g"] = g_final[0]
    grads["meta_tokens"] = dh[:N_META]
    return loss_row[0, 0], dh[N_META:length], grads


_ANY = pl.BlockSpec(memory_space=pl.ANY)


def chip_exchange(srcs, modes, name):
    n = len(srcs)
    flips = ((1, 0), (0, 1), (1, 1))

    def body(*refs):
        ins, outs = refs[:n], refs[n:2 * n]
        send_sems, recv_sems, fwd_send_sems, fwd_recv_sems, loc_sems = refs[2 * n:]
        x, y, c = lax.axis_index("x"), lax.axis_index("y"), lax.axis_index("c")
        me = 2 * x + y
        waits, forwards = [], []
        for a in range(n):
            whole = modes[a] != "slab"
            cp = pltpu.make_async_copy(ins[a] if whole else ins[a].at[me], outs[a].at[me], loc_sems.at[a])
            cp.start()
            waits.append(cp.wait)
            half = ins[a].shape[0] // 2 if modes[a] == "bcast_split" else None
            for k, (fx, fy) in enumerate(flips):
                px = 1 - x if fx else x
                py = 1 - y if fy else y
                peer = 2 * px + py
                if half is None:
                    src = ins[a] if whole else ins[a].at[peer]
                    dst = outs[a].at[me]
                else:
                    src = ins[a].at[pl.ds(c * half, half)]
                    dst = outs[a].at[me, pl.ds(c * half, half)]
                rc = pltpu.make_async_remote_copy(src_ref=src, dst_ref=dst, send_sem=send_sems.at[a, k],
                                                  recv_sem=recv_sems.at[a, k], device_id=(px, py, c), device_id_type=MESH_ID)
                rc.start()
                if half is None:
                    waits.append(rc.wait)
                else:
                    waits.append(rc.wait_send)
                    landed = outs[a].at[peer, pl.ds(c * half, half)]
                    fw = pltpu.make_async_remote_copy(src_ref=landed, dst_ref=landed, send_sem=fwd_send_sems.at[a, k],
                                                      recv_sem=fwd_recv_sems.at[a, k], device_id=(x, y, 1 - c),
                                                      device_id_type=MESH_ID)
                    forwards.append((rc, fw))
        for rc, fw in forwards:
            rc.wait_recv()
            fw.start()
        for rc, fw in forwards:
            fw.wait()
        for w in waits:
            w()

    out_shape = [jax.ShapeDtypeStruct((N_CHIPS,) + (s.shape if m != "slab" else s.shape[1:]), s.dtype) for s, m in zip(srcs, modes)]
    return pl.pallas_call(
        body, name=name, in_specs=[_ANY] * n, out_specs=[_ANY] * n, out_shape=out_shape,
        scratch_shapes=[pltpu.SemaphoreType.DMA((n, 3)), pltpu.SemaphoreType.DMA((n, 3)), pltpu.SemaphoreType.DMA((n, 3)),
                        pltpu.SemaphoreType.DMA((n, 3)), pltpu.SemaphoreType.DMA((n,))],
    )(*srcs)


def _piece(ref, mode, k):
    if mode == "slab":
        return ref.at[k]
    if mode == "rows":
        rs = ref.shape[1] // N_CHIPS
        return ref.at[:, pl.ds(pl.multiple_of(k * rs, 16), rs), :]
    if mode == "cols":
        cs = ref.shape[2] // N_CHIPS
        return ref.at[:, :, pl.ds(pl.multiple_of(k * cs, LANES), cs)]
    return ref


def _piece_shape(shape, mode):
    if mode == "slab":
        return shape[1:]
    if mode == "rows":
        return (shape[0], shape[1] // N_CHIPS, shape[2])
    if mode == "cols":
        return (shape[0], shape[1], shape[2] // N_CHIPS)
    return shape


def grad_exchange(srcs, modes, name):
    n = len(srcs)
    flips = ((1, 0), (0, 1), (1, 1))

    def body(*refs):
        ins, outs = refs[:n], refs[n:2 * n]
        send_sems, recv_sems, fwd_send_sems, fwd_recv_sems, sib_send_sems, sib_recv_sems, loc_sems = refs[2 * n:]
        x, y, c = lax.axis_index("x"), lax.axis_index("y"), lax.axis_index("c")
        me = 2 * x + y
        sibling = (x, y, 1 - c)
        waits, forwards = [], []
        for a in range(n):
            mine = _piece(ins[a], modes[a], me)
            slot = outs[a].at[4 * c + me]
            cp = pltpu.make_async_copy(mine, slot, loc_sems.at[a])
            cp.start()
            sb = pltpu.make_async_remote_copy(src_ref=mine, dst_ref=slot, send_sem=sib_send_sems.at[a],
                                              recv_sem=sib_recv_sems.at[a], device_id=sibling, device_id_type=MESH_ID)
            sb.start()
            waits += [cp.wait, sb.wait]
            for k, (fx, fy) in enumerate(flips):
                px = 1 - x if fx else x
                py = 1 - y if fy else y
                peer = 2 * px + py
                rc = pltpu.make_async_remote_copy(src_ref=_piece(ins[a], modes[a], peer), dst_ref=slot,
                                                  send_sem=send_sems.at[a, k], recv_sem=recv_sems.at[a, k],
                                                  device_id=(px, py, c), device_id_type=MESH_ID)
                rc.start()
                landed = outs[a].at[4 * c + peer]
                fw = pltpu.make_async_remote_copy(src_ref=landed, dst_ref=landed, send_sem=fwd_send_sems.at[a, k],
                                                  recv_sem=fwd_recv_sems.at[a, k], device_id=sibling, device_id_type=MESH_ID)
                waits.append(rc.wait_send)
                forwards.append((rc, fw))
        for rc, fw in forwards:
            rc.wait_recv()
            fw.start()
        for rc, fw in forwards:
            fw.wait()
        for w in waits:
            w()

    out_shape = [jax.ShapeDtypeStruct((2 * N_CHIPS,) + tuple(_piece_shape(s.shape, m)), s.dtype) for s, m in zip(srcs, modes)]
    dma = pltpu.SemaphoreType.DMA
    return pl.pallas_call(
        body, name=name, in_specs=[_ANY] * n, out_specs=[_ANY] * n, out_shape=out_shape,
        scratch_shapes=[dma((n, 3)), dma((n, 3)), dma((n, 3)), dma((n, 3)), dma((n,)), dma((n,)), dma((n,))],
    )(*srcs)


WEIGHT_NAMES = ("meta_tokens", "norm_mix_g", "w_in", "ssd_conv_w", "ssd_conv_b", "ssd_dt_bias", "ssd_a_log", "ssd_d",
                "ssd_norm_g", "sb_norm_g", "mla_q_norm_g", "mla_kv_norm_g", "mla_w_uq", "mla_w_ukv", "mla_norm_g",
                "w_out", "norm_ffn_g", "ffn_w_up", "ffn_conv_w", "ffn_conv_b", "ffn_w_down", "final_norm_g")
SHARD_AXIS = {"meta_tokens": 1, "w_in": 2, "ssd_conv_w": 2, "mla_w_uq": 2, "mla_w_ukv": 2, "w_out": 1, "ffn_w_up": 2,
              "ffn_conv_w": 2, "ffn_w_down": 1}
SHARDED = tuple(n for n in WEIGHT_NAMES if n in SHARD_AXIS)
REPLICATED = tuple(n for n in WEIGHT_NAMES if n not in SHARD_AXIS)
GATHER_BF16 = ("w_in", "mla_w_uq", "mla_w_ukv", "w_out", "ffn_w_up", "ffn_w_down")
GATHER_F32 = ("meta_tokens", "ssd_conv_w", "ffn_conv_w")
PACK_ROWS = ROW_TILE


def pack(arrs, dtype):
    flat = jnp.concatenate([a.reshape(-1).astype(dtype) for a in arrs])
    per = PACK_ROWS * PACK_W
    total = -(-flat.size // per) * per
    return jnp.pad(flat, (0, total - flat.size)).reshape(total // PACK_W, PACK_W)


def unpack(buf, shapes):
    flat = buf.reshape(-1)
    out, off = [], 0
    for shp in shapes:
        size = int(np.prod(shp))
        out.append(flat[off:off + size].reshape(shp))
        off += size
    return out


def gather_weights(a):
    full = {n: a[n] for n in REPLICATED}
    bufs = [pack([a[n] for n in GATHER_BF16], BF16), pack([a[n] for n in GATHER_F32], F32)]
    got = chip_exchange(bufs, ("bcast_split", "bcast"), "gather_weights")
    for names, g in ((GATHER_BF16, got[0]), (GATHER_F32, got[1])):
        pieces = [unpack(g[k], [a[n].shape for n in names]) for k in range(N_CHIPS)]
        for idx, n in enumerate(names):
            full[n] = jnp.concatenate([pieces[k][idx] for k in range(N_CHIPS)], axis=SHARD_AXIS[n])
    return full


BIG = ("w_in", "w_out", "ffn_w_up", "ffn_w_down")
BIG_MODE = {"w_in": "slab", "w_out": "rows", "ffn_w_up": "cols", "ffn_w_down": "rows"}
SMALL_SHARDED = tuple(n for n in SHARDED if n not in BIG)
ADAM_TILE = 128


def _adamw(i, n, *vals):
    parts, (w, m, v) = vals[:2 * N_CHIPS], vals[2 * N_CHIPS:]
    g = parts[0].astype(F32)
    for p in parts[1:]:
        g = g + p.astype(F32)
    m = ADAM_B1 * m + (1.0 - ADAM_B1) * g
    v = ADAM_B2 * v + (1.0 - ADAM_B2) * jnp.square(g)
    m_hat = m / (1.0 - ADAM_B1 ** ADAM_STEP)
    v_hat = v / (1.0 - ADAM_B2 ** ADAM_STEP)
    delta = -ADAM_LR * (m_hat / (jnp.sqrt(v_hat) + ADAM_EPS) + ADAM_WD * w)
    return g, delta, m, v


def _adamw_call(got, w, m, v, name):
    rows, width = w.shape
    flat = got.reshape(2 * N_CHIPS * rows, width)
    blk = rows // ADAM_TILE
    ins = [RI(flat, rblk=k * blk) for k in range(2 * N_CHIPS)] + [RI(w), RI(m), RI(v)]
    return rowwise(_adamw, ins, [RO(width, F32)] * 4, name, rows, tm=ADAM_TILE)


def reduce_and_update(a, grads):
    srcs, modes = [], []
    for n in BIG:
        g = grads[n].astype(BF16)
        if n == "w_in":
            cs = a[n].shape[2]
            g = g.reshape(DEPTH, D_MODEL, N_CHIPS, cs).transpose(2, 0, 1, 3)
        srcs.append(g)
        modes.append(BIG_MODE[n])
    slabs = []
    for k in range(N_CHIPS):
        parts = []
        for n in SMALL_SHARDED:
            ax = SHARD_AXIS[n]
            size = a[n].shape[ax]
            parts.append(lax.slice_in_dim(grads[n], k * size, (k + 1) * size, axis=ax))
        slabs.append(pack(parts, BF16))
    srcs += [jnp.stack(slabs, axis=0), pack([grads[n] for n in REPLICATED], F32)]
    modes += ["slab", "bcast"]
    got = grad_exchange(srcs, modes, "exchange_grads")
    outs = {}
    kinds = ("grad", "delta", "new_m", "new_v")
    for n, g8 in zip(BIG, got):
        shp = a[n].shape
        rows = shp[0] * shp[1]
        flat = lambda t: t.reshape(rows, shp[2])
        res = _adamw_call(g8.reshape(2 * N_CHIPS, rows, shp[2]), flat(a[n]), flat(a["m_" + n]), flat(a["v_" + n]), "adamw_" + n)
        for kind, val in zip(kinds, res):
            outs[(kind, n)] = val.reshape(shp)
    for tag, names, g8 in (("small", SMALL_SHARDED, got[len(BIG)]), ("rep", REPLICATED, got[len(BIG) + 1])):
        shapes = [a[n].shape for n in names]
        packed = [pack([a[pre + n] for n in names], F32) for pre in ("", "m_", "v_")]
        res = _adamw_call(g8, *packed, "adamw_" + tag)
        for kind, buf in zip(kinds, res):
            for n, val in zip(names, unpack(buf, shapes)):
                outs[(kind, n)] = val
    return outs


INPUT_NAMES = ("x",) + WEIGHT_NAMES + ("loss_target",) + tuple("m_" + n for n in WEIGHT_NAMES) + tuple("v_" + n for n in WEIGHT_NAMES)


def kernel(x, meta_tokens, norm_mix_g, w_in, ssd_conv_w, ssd_conv_b, ssd_dt_bias, ssd_a_log, ssd_d, ssd_norm_g, sb_norm_g, mla_q_norm_g, mla_kv_norm_g, mla_w_uq, mla_w_ukv, mla_norm_g, w_out, norm_ffn_g, ffn_w_up, ffn_conv_w, ffn_conv_b, ffn_w_down, final_norm_g, loss_target, m_meta_tokens, m_norm_mix_g, m_w_in, m_ssd_conv_w, m_ssd_conv_b, m_ssd_dt_bias, m_ssd_a_log, m_ssd_d, m_ssd_norm_g, m_sb_norm_g, m_mla_q_norm_g, m_mla_kv_norm_g, m_mla_w_uq, m_mla_w_ukv, m_mla_norm_g, m_w_out, m_norm_ffn_g, m_ffn_w_up, m_ffn_conv_w, m_ffn_conv_b, m_ffn_w_down, m_final_norm_g, v_meta_tokens, v_norm_mix_g, v_w_in, v_ssd_conv_w, v_ssd_conv_b, v_ssd_dt_bias, v_ssd_a_log, v_ssd_d, v_ssd_norm_g, v_sb_norm_g, v_mla_q_norm_g, v_mla_kv_norm_g, v_mla_w_uq, v_mla_w_ukv, v_mla_norm_g, v_w_out, v_norm_ffn_g, v_ffn_w_up, v_ffn_conv_w, v_ffn_conv_b, v_ffn_w_down, v_final_norm_g):
    args = (x, meta_tokens, norm_mix_g, w_in, ssd_conv_w, ssd_conv_b, ssd_dt_bias, ssd_a_log, ssd_d, ssd_norm_g, sb_norm_g, mla_q_norm_g, mla_kv_norm_g, mla_w_uq, mla_w_ukv, mla_norm_g, w_out, norm_ffn_g, ffn_w_up, ffn_conv_w, ffn_conv_b, ffn_w_down, final_norm_g, loss_target, m_meta_tokens, m_norm_mix_g, m_w_in, m_ssd_conv_w, m_ssd_conv_b, m_ssd_dt_bias, m_ssd_a_log, m_ssd_d, m_ssd_norm_g, m_sb_norm_g, m_mla_q_norm_g, m_mla_kv_norm_g, m_mla_w_uq, m_mla_w_ukv, m_mla_norm_g, m_w_out, m_norm_ffn_g, m_ffn_w_up, m_ffn_conv_w, m_ffn_conv_b, m_ffn_w_down, m_final_norm_g, v_meta_tokens, v_norm_mix_g, v_w_in, v_ssd_conv_w, v_ssd_conv_b, v_ssd_dt_bias, v_ssd_a_log, v_ssd_d, v_ssd_norm_g, v_sb_norm_g, v_mla_q_norm_g, v_mla_kv_norm_g, v_mla_w_uq, v_mla_w_ukv, v_mla_norm_g, v_w_out, v_norm_ffn_g, v_ffn_w_up, v_ffn_conv_w, v_ffn_conv_b, v_ffn_w_down, v_final_norm_g)
    a = dict(zip(INPUT_NAMES, args, strict=True))
    full = gather_weights(a)
    loss, grad_x, grads = local_step(a["x"][0], a["loss_target"][0], full)
    loss = lax.psum(loss, ("x", "y", "c"))
    outs = reduce_and_update(a, grads)
    result = [loss, grad_x[None]]
    for kind in ("grad", "delta", "new_m", "new_v"):
        result += [outs[(kind, n)] for n in WEIGHT_NAMES]
    return tuple(result)
```

```python
import functools
import math

import numpy as np
import jax
import jax.numpy as jnp
from jax import lax
from jax.experimental import pallas as pl
from jax.experimental.pallas import tpu as pltpu

F32 = jnp.float32
BF16 = jnp.bfloat16
HIGHEST = lax.Precision.HIGHEST
MESH_ID = pl.DeviceIdType.MESH

D_MODEL = 1024
DEPTH = 2
N_META = 16
EPS = 1e-6
SSD_HEADS = 8
SSD_WIDTH = 512
SSD_XBC = 1024
SSD_CONV = 4
SB_WIDTH = 256
SB_SCALE = 64 ** -0.5
MLA_Q_RANK = 192
MLA_KV_RANK = 128
MLA_ROPE = 32
MLA_SCALE = 96 ** -0.5
ROPE_BASE = 10000.0
D_FF = 2816
FFN_CONV = 3
IN_COLS = 2664
N_CHIPS = 4

ADAM_LR = 0.001
ADAM_B1 = 0.9
ADAM_B2 = 0.999
ADAM_EPS = 1e-08
ADAM_WD = 0.01
ADAM_STEP = 10

LANES = 128
SUBLANES = 8
ROW_TILE = 256
KEY_UNROLL = 4
WIDE_TILE = 768
VMEM_LIMIT = 56 * 1024 * 1024
PACK_W = 1024

U_XBC, U_Z, U_QA, U_CKV, U_KR4, U_DT, U_MAIN = 0, 1024, 1536, 1792, 1920, 2048, 2304
NEG = -1e30


def _cp(*sem):
    return pltpu.CompilerParams(dimension_semantics=sem if sem else None, vmem_limit_bytes=VMEM_LIMIT)


def _pick(dim, pref):
    if dim <= pref:
        return dim
    best = None
    for t in range(LANES, pref + 1, LANES):
        if dim % t == 0:
            best = t
    assert best is not None, (dim, pref)
    return best


def _dot(a, b, dims="nn", precision=None):
    dn = {"nn": (((1,), (0,)), ((), ())), "nt": (((1,), (1,)), ((), ())), "tn": (((0,), (0,)), ((), ()))}[dims]
    return lax.dot_general(a, b, dn, preferred_element_type=F32, precision=precision)


def _softplus(x):
    return jnp.maximum(x, 0.0) + jnp.log1p(jnp.exp(-jnp.abs(x)))


def _silu(x):
    return x * jax.nn.sigmoid(x)


def _rms(x, g, n=None):
    n = x.shape[-1] if n is None else n
    ms = jnp.sum(x * x, axis=-1, keepdims=True) * (1.0 / n)
    return x * lax.rsqrt(ms + EPS) * g


def mm(a, b, dims, out_dtype, name, add=None, tm=None, tn=None, tk=None):
    if dims == "nn":
        (m, k), (k2, n) = a.shape, b.shape
    elif dims == "nt":
        (m, k), (n, k2) = a.shape, b.shape
    else:
        (k, m), (k2, n) = a.shape, b.shape
    assert k == k2, (a.shape, b.shape, dims)
    if dims == "tn":
        tm, tn, tk = _pick(m, tm or 1408), _pick(n, tn or 1408), _pick(k, tk or 1408)
    else:
        tm, tn, tk = _pick(m, tm or 768), _pick(n, tn or 1408), _pick(k, tk or 2816)
    nk = k // tk
    if dims == "tn":
        a_spec = pl.BlockSpec((tk, tm), lambda j, i, kk: (kk, i))
    else:
        a_spec = pl.BlockSpec((tm, tk), lambda j, i, kk: (i, kk))
    if dims == "nt":
        b_spec = pl.BlockSpec((tn, tk), lambda j, i, kk: (j, kk))
    else:
        b_spec = pl.BlockSpec((tk, tn), lambda j, i, kk: (kk, j))
    o_spec = pl.BlockSpec((tm, tn), lambda j, i, kk: (i, j))
    has_add = add is not None

    def body(*refs):
        a_ref, b_ref = refs[0], refs[1]
        add_ref = refs[2] if has_add else None
        o_ref = refs[3] if has_add else refs[2]
        part = _dot(a_ref[...].astype(BF16), b_ref[...].astype(BF16), dims)

        def finish(r):
            if has_add:
                r = r + add_ref[...].astype(F32)
            o_ref[...] = r.astype(o_ref.dtype)

        if nk == 1:
            finish(part)
            return
        acc_ref = refs[-1]
        kk = pl.program_id(2)

        @pl.when(kk == 0)
        def _():
            acc_ref[...] = part

        @pl.when(jnp.logical_and(kk > 0, kk < nk - 1))
        def _():
            acc_ref[...] += part

        @pl.when(kk == nk - 1)
        def _():
            finish(acc_ref[...] + part)

    in_specs = [a_spec, b_spec] + ([o_spec] if has_add else [])
    args = (a, b) + ((add,) if has_add else ())
    return pl.pallas_call(
        body, name=name, grid=(n // tn, m // tm, nk),
        in_specs=in_specs, out_specs=o_spec,
        out_shape=jax.ShapeDtypeStruct((m, n), out_dtype),
        scratch_shapes=[pltpu.VMEM((tm, tn), F32)] if nk > 1 else [],
        compiler_params=_cp("parallel", "parallel", "arbitrary"),
    )(*args)


def RI(arr, width=None, cidx=0, cv=False, rblk=0):
    return ("row" if rblk == 0 else ("row", rblk), arr, arr.shape[1] if width is None else width, cidx, cv)


def HP(arr, width=None, cidx=0, cv=False):
    return ("prev", arr, arr.shape[1] if width is None else width, cidx, cv)


def HN(arr, width=None, cidx=0, cv=False):
    return ("next", arr, arr.shape[1] if width is None else width, cidx, cv)


def PA(arr, width=None, cidx=0, cv=False):
    return ("par", arr, arr.shape[1] if width is None else width, cidx, cv)


def RO(ncols, dtype, width=None, cv=False):
    return ("row", ncols, dtype, ncols if width is None else width, cv)


def AO(nrows, ncols, width=None, cv=False):
    return ("acc", (nrows, ncols), F32, ncols if width is None else width, cv)


def rowwise(fn, ins, outs, name, rows, tm=ROW_TILE, ncol=1):
    tm = min(tm, rows)
    assert rows % tm == 0
    nrow = rows // tm
    hb = tm // SUBLANES
    last_hb = rows // SUBLANES - 1
    in_specs, args = [], []
    for kind, arr, width, cidx, cv in ins:
        def cmap(j, cidx=cidx, cv=cv):
            return cidx + j if cv else cidx
        if kind == "row":
            spec = pl.BlockSpec((tm, width), lambda j, i, cmap=cmap: (i, cmap(j)))
        elif isinstance(kind, tuple):
            spec = pl.BlockSpec((tm, width), lambda j, i, cmap=cmap, rblk=kind[1]: (i + rblk, cmap(j)))
        elif kind == "prev":
            spec = pl.BlockSpec((SUBLANES, width), lambda j, i, cmap=cmap: (jnp.maximum(i * hb - 1, 0), cmap(j)))
        elif kind == "next":
            spec = pl.BlockSpec((SUBLANES, width), lambda j, i, cmap=cmap: (jnp.minimum((i + 1) * hb, last_hb), cmap(j)))
        else:
            spec = pl.BlockSpec((arr.shape[0], width), lambda j, i, cmap=cmap: (0, cmap(j)))
        in_specs.append(spec)
        args.append(arr)
    out_specs, out_shapes, acc_cv = [], [], []
    for kind, shp, dtype, width, cv in outs:
        if kind == "row":
            out_specs.append(pl.BlockSpec((tm, width), lambda j, i, cv=cv: (i, j if cv else 0)))
            out_shapes.append(jax.ShapeDtypeStruct((rows, shp), dtype))
            acc_cv.append(None)
        else:
            out_specs.append(pl.BlockSpec((shp[0], width), lambda j, i, cv=cv: (0, j if cv else 0)))
            out_shapes.append(jax.ShapeDtypeStruct(shp, dtype))
            acc_cv.append(cv)
    n_in = len(ins)

    def body(*refs):
        j = pl.program_id(0)
        i = pl.program_id(1)
        vals = fn(i, nrow, *[r[...] for r in refs[:n_in]])
        if not isinstance(vals, (tuple, list)):
            vals = (vals,)
        for o_ref, v, cv in zip(refs[n_in:], vals, acc_cv):
            if cv is None:
                o_ref[...] = v.astype(o_ref.dtype)
            else:
                first = (i == 0) if cv else jnp.logical_and(i == 0, j == 0)

                @pl.when(first)
                def _(o_ref=o_ref, v=v):
                    o_ref[...] = v.astype(o_ref.dtype)

                @pl.when(jnp.logical_not(first))
                def _(o_ref=o_ref, v=v):
                    o_ref[...] += v.astype(o_ref.dtype)

    res = pl.pallas_call(
        body, name=name, grid=(ncol, nrow), in_specs=in_specs, out_specs=out_specs, out_shape=out_shapes,
        compiler_params=_cp("arbitrary", "arbitrary"),
    )(*args)
    return res


def _rows_iota(x):
    return lax.broadcasted_iota(jnp.int32, x.shape, 0)


def shift_down(x, halo, s):
    if s == 0:
        return x
    tm = x.shape[0]
    top = pltpu.roll(halo, s, 0)
    if tm > SUBLANES:
        top = jnp.concatenate([top, jnp.zeros((tm - SUBLANES, x.shape[1]), x.dtype)], axis=0)
    return jnp.where(_rows_iota(x) < s, top, pltpu.roll(x, s, 0))


def shift_up(x, halo, s):
    if s == 0:
        return x
    tm = x.shape[0]
    bot = pltpu.roll(halo, SUBLANES - s, 0)
    if tm > SUBLANES:
        bot = jnp.concatenate([jnp.zeros((tm - SUBLANES, x.shape[1]), x.dtype), bot], axis=0)
    return jnp.where(_rows_iota(x) >= tm - s, bot, pltpu.roll(x, tm - s, 0))


def conv_fwd(x, halo, w, i):
    kw = w.shape[0]
    halo = jnp.where(i == 0, 0.0, halo)
    out = None
    for k in range(kw):
        term = w[k:k + 1, :] * shift_down(x, halo, kw - 1 - k)
        out = term if out is None else out + term
    return out


def conv_bwd_data(dy, halo_next, w, i, n):
    kw = w.shape[0]
    halo_next = jnp.where(i == n - 1, 0.0, halo_next)
    out = None
    for k in range(kw):
        term = w[k:k + 1, :] * shift_up(dy, halo_next, kw - 1 - k)
        out = term if out is None else out + term
    return out


def conv_bwd_w(dy, x, halo, i, kw):
    halo = jnp.where(i == 0, 0.0, halo)
    rows = [jnp.sum(dy * shift_down(x, halo, kw - 1 - k), axis=0, keepdims=True) for k in range(kw)]
    return jnp.concatenate(rows, axis=0)


def _lane(shape):
    return lax.broadcasted_iota(jnp.int32, shape, 1)


def rope_rot(x):
    lane = _lane(x.shape) % MLA_ROPE
    return jnp.where(lane < MLA_ROPE // 2, -pltpu.roll(x, LANES - MLA_ROPE // 2, 1), pltpu.roll(x, MLA_ROPE // 2, 1))


def rope_rot_t(g):
    lane = _lane(g.shape) % MLA_ROPE
    return jnp.where(lane < MLA_ROPE // 2, pltpu.roll(g, LANES - MLA_ROPE // 2, 1), -pltpu.roll(g, MLA_ROPE // 2, 1))


def _ssd_common(g, xs, dt_raw, bias, alog, q):
    lane = _lane((q, LANES))
    pre = dt_raw + bias
    dt = jnp.where(lane < SSD_HEADS, _softplus(pre), 0.0)
    a_row = -jnp.exp(alog)
    d_a = dt * a_row
    ri = lax.broadcasted_iota(jnp.int32, (q, q), 0)
    ci = lax.broadcasted_iota(jnp.int32, (q, q), 1)
    causal = ri >= ci
    acs = _dot(causal.astype(F32), d_a, "nn", HIGHEST)
    acs_t = acs.T
    subl = lax.broadcasted_iota(jnp.int32, (LANES, q), 0)
    heads = [4 * g + i for i in range(4)]
    lo = lane < 64

    def col(arr, h):
        return jnp.sum(jnp.where(lane == h, arr, 0.0), axis=1, keepdims=True)

    def lanes4(v):
        m = lo if v[0].shape[0] == q else lo[0:1, :]
        return jnp.concatenate([jnp.where(m, v[0], v[1]), jnp.where(m, v[2], v[3])], axis=1)

    cols = [col(acs, h) for h in heads]
    rows = [jnp.sum(jnp.where(subl == h, acs_t, 0.0), axis=0, keepdims=True) for h in heads]
    tots = [c_[q - 1:q, :] for c_ in cols]
    acs4 = lanes4(cols)
    dt4 = lanes4([col(dt, h) for h in heads])
    lms = [jnp.exp(jnp.where(causal, cols[i] - rows[i], NEG)) for i in range(4)]
    lane4 = _lane((q, 2 * LANES))
    hm = [jnp.logical_and(lane4 >= 64 * i, lane4 < 64 * (i + 1)) for i in range(4)]
    return dict(lane=lane, lo=lo, pre=pre, dt=dt, a_row=a_row, heads=heads, tots=tots, lms=lms, ri=ri, ci=ci, hm=hm,
                lanes4=lanes4, eacs=jnp.exp(acs4), dte=jnp.exp(lanes4(tots) - acs4), dt4=dt4, x=xs * dt4)


def _pick_lane(row_arr, h):
    return jnp.sum(jnp.where(_lane(row_arr.shape) == h, row_arr, 0.0), axis=1, keepdims=True)


def _etot(tots):
    sub = lax.broadcasted_iota(jnp.int32, (2 * LANES, LANES), 0)
    e = [jnp.exp(t) for t in tots]
    return jnp.where(sub < 64, e[0], jnp.where(sub < 128, e[1], jnp.where(sub < 192, e[2], e[3]))), e


def _half(arr, i, lo):
    slab = arr[:, LANES * (i // 2):LANES * (i // 2 + 1)]
    return jnp.where(lo, slab, 0.0) if i % 2 == 0 else jnp.where(lo, 0.0, slab)


def ssd_fwd(xbc_c, u_main, bias_row, alog_row, d_row, name):
    lp = xbc_c.shape[0]
    q = min(ROW_TILE, lp)
    nc = lp // q
    dt_blk = U_DT // LANES

    def body(xs_ref, b_ref, c_ref, dt_ref, bias_ref, alog_ref, d_ref, y_ref, hp_ref, h_scr):
        g = pl.program_id(0)
        c = pl.program_id(1)

        @pl.when(c == 0)
        def _():
            h_scr[...] = jnp.zeros_like(h_scr)

        xs = xs_ref[...]
        bb = b_ref[...].astype(BF16)
        cb_ = c_ref[...].astype(BF16)
        s = _ssd_common(g, xs, dt_ref[...], bias_ref[...], alog_ref[...], q)
        gmat = _dot(cb_, bb, "nt")
        ms = [(gmat * s["lms"][i]).astype(BF16) for i in range(4)]
        xjs = [_half(s["x"], i, s["lo"]).astype(BF16) for i in range(4)]
        ys = [_dot(ms[i], xjs[i]) for i in range(4)]
        hp = h_scr[...]
        hp_ref[...] = hp
        yoff = _dot(cb_, hp.astype(BF16), "nt") * s["eacs"]
        d4 = s["lanes4"]([_pick_lane(d_ref[...], h) for h in s["heads"]])
        y_ref[...] = jnp.concatenate([ys[0] + ys[1], ys[2] + ys[3]], axis=1) + yoff + d4 * xs
        etot, _ = _etot(s["tots"])
        h_scr[...] = hp * etot + _dot((s["x"] * s["dte"]).astype(BF16), bb, "tn")

    in_specs = [
        pl.BlockSpec((q, 2 * LANES), lambda g, c: (c, g)),
        pl.BlockSpec((q, LANES), lambda g, c: (c, 4 + g)),
        pl.BlockSpec((q, LANES), lambda g, c: (c, 6 + g)),
        pl.BlockSpec((q, LANES), lambda g, c: (c, dt_blk)),
        pl.BlockSpec((1, LANES), lambda g, c: (0, 0)),
        pl.BlockSpec((1, LANES), lambda g, c: (0, 0)),
        pl.BlockSpec((1, LANES), lambda g, c: (0, 0)),
    ]
    out_specs = [
        pl.BlockSpec((q, 2 * LANES), lambda g, c: (c, g)),
        pl.BlockSpec((None, None, 2 * LANES, LANES), lambda g, c: (g, c, 0, 0)),
    ]
    return pl.pallas_call(
        body, name=name, grid=(2, nc), in_specs=in_specs, out_specs=out_specs,
        out_shape=[jax.ShapeDtypeStruct((lp, SSD_WIDTH), F32), jax.ShapeDtypeStruct((2, nc, 2 * LANES, LANES), F32)],
        scratch_shapes=[pltpu.VMEM((2 * LANES, LANES), F32)],
        compiler_params=_cp("arbitrary", "arbitrary"),
    )(xbc_c, xbc_c, xbc_c, u_main, bias_row, alog_row, d_row)


def ssd_bwd(xbc_c, u_main, bias_row, alog_row, d_row, hprev, dy, name):
    lp = xbc_c.shape[0]
    q = min(ROW_TILE, lp)
    nc = lp // q
    dt_blk = U_DT // LANES

    def body(xs_ref, b_ref, c_ref, dt_ref, bias_ref, alog_ref, d_ref, hp_ref, dy_ref,
             dxs_ref, db_ref, dc_ref, ddt_ref, pg_ref, dh_scr):
        g = pl.program_id(0)
        cc = pl.program_id(1)

        @pl.when(cc == 0)
        def _():
            dh_scr[...] = jnp.zeros_like(dh_scr)
            pg_ref[...] = jnp.zeros_like(pg_ref)

        xs = xs_ref[...]
        bb = b_ref[...].astype(BF16)
        cb_ = c_ref[...].astype(BF16)
        s = _ssd_common(g, xs, dt_ref[...], bias_ref[...], alog_ref[...], q)
        lane, lo, x, hm, heads = s["lane"], s["lo"], s["x"], s["hm"], s["heads"]
        d_y = dy_ref[...]
        hp = hp_ref[...]
        hpb = hp.astype(BF16)
        dhn = dh_scr[...]
        dhnb = dhn.astype(BF16)
        xd = x * s["dte"]
        gmat = _dot(cb_, bb, "nt")
        m32s = [gmat * s["lms"][i] for i in range(4)]
        xjs = [_half(x, i, lo).astype(BF16) for i in range(4)]
        dyjs = [_half(d_y, i, lo).astype(BF16) for i in range(4)]
        dxparts = [_dot(m32s[i].astype(BF16), dyjs[i], "tn") for i in range(4)]
        dms = [_dot(dyjs[i], xjs[i], "nt") for i in range(4)]
        dg = dms[0] * s["lms"][0] + dms[1] * s["lms"][1] + dms[2] * s["lms"][2] + dms[3] * s["lms"][3]
        wms = [dms[i] * m32s[i] for i in range(4)]
        row_part = [jnp.sum(wm, axis=1, keepdims=True) for wm in wms]
        col_part = [jnp.sum(wm, axis=0, keepdims=True) for wm in wms]
        dgb = dg.astype(BF16)
        yoff = _dot(cb_, hpb, "nt") * s["eacs"]
        d_t = (d_y * s["eacs"]).astype(BF16)
        d_c = _dot(dgb, bb) + _dot(d_t, hpb)
        d_hp = _dot(d_t, cb_, "tn")
        dxd = _dot(bb, dhnb, "nt")
        d_b = _dot(dgb, cb_, "tn") + _dot(xd.astype(BF16), dhnb)
        d_x = jnp.concatenate([dxparts[0] + dxparts[1], dxparts[2] + dxparts[3]], axis=1) + dxd * s["dte"]
        r = dxd * xd
        a_terms = d_y * yoff - r

        def hsum(arr):
            return [jnp.sum(jnp.where(hm[i], arr, 0.0), axis=1, keepdims=True) for i in range(4)]

        dacs, rs = hsum(a_terms), hsum(r)
        hh = dhn * hp
        sub = lax.broadcasted_iota(jnp.int32, (2 * LANES, LANES), 0)
        hsums = [jnp.sum(jnp.where(jnp.logical_and(sub >= 64 * i, sub < 64 * (i + 1)), hh, 0.0), keepdims=True) for i in range(4)]
        last = lax.broadcasted_iota(jnp.int32, (q, 1), 0) == q - 1
        etot, etots = _etot(s["tots"])
        ddacs = jnp.zeros((q, LANES), F32)
        for i, h in enumerate(heads):
            dtot = jnp.sum(rs[i], keepdims=True) + hsums[i] * etots[i]
            ddacs = ddacs + jnp.where(lane == h, dacs[i] + row_part[i] + jnp.where(last, dtot, 0.0), 0.0)
        subl = lax.broadcasted_iota(jnp.int32, (LANES, q), 0)
        cols_t = jnp.zeros((LANES, q), F32)
        for i, h in enumerate(heads):
            cols_t = cols_t + jnp.where(subl == h, col_part[i], 0.0)
        ddacs = ddacs - cols_t.T
        anti = (s["ri"] <= s["ci"]).astype(F32)
        da = _dot(anti, ddacs, "nn", HIGHEST)
        ddt_own = hsum(d_x * xs)
        ddt = da * s["a_row"]
        for i, h in enumerate(heads):
            ddt = ddt + jnp.where(lane == h, ddt_own[i], 0.0)
        draw = ddt * jax.nn.sigmoid(s["pre"])
        ddt_ref[...] = draw
        d4 = s["lanes4"]([_pick_lane(d_ref[...], h) for h in heads])
        dxs_ref[...] = d4 * d_y + d_x * s["dt4"]
        db_ref[...] = d_b
        dc_ref[...] = d_c
        dds = hsum(d_y * xs)
        lane1 = lane[0:1, :]
        dd_row = jnp.zeros((1, LANES), F32)
        for i, h in enumerate(heads):
            dd_row = dd_row + jnp.where(lane1 == h, jnp.sum(dds[i], keepdims=True), 0.0)
        dbias_row = jnp.sum(draw, axis=0, keepdims=True)
        dalog_row = jnp.sum(da * s["dt"], axis=0, keepdims=True) * s["a_row"]
        sub8 = lax.broadcasted_iota(jnp.int32, (SUBLANES, LANES), 0)
        pg_ref[...] += (jnp.where(sub8 == 0, dbias_row, 0.0) + jnp.where(sub8 == 1, dalog_row, 0.0)
                        + jnp.where(sub8 == 2, dd_row, 0.0))
        dh_scr[...] = d_hp + etot * dhn

    rc = lambda c: nc - 1 - c
    in_specs = [
        pl.BlockSpec((q, 2 * LANES), lambda g, c: (rc(c), g)),
        pl.BlockSpec((q, LANES), lambda g, c: (rc(c), 4 + g)),
        pl.BlockSpec((q, LANES), lambda g, c: (rc(c), 6 + g)),
        pl.BlockSpec((q, LANES), lambda g, c: (rc(c), dt_blk)),
        pl.BlockSpec((1, LANES), lambda g, c: (0, 0)),
        pl.BlockSpec((1, LANES), lambda g, c: (0, 0)),
        pl.BlockSpec((1, LANES), lambda g, c: (0, 0)),
        pl.BlockSpec((None, None, 2 * LANES, LANES), lambda g, c: (g, rc(c), 0, 0)),
        pl.BlockSpec((q, 2 * LANES), lambda g, c: (rc(c), g)),
    ]
    out_specs = [
        pl.BlockSpec((q, 2 * LANES), lambda g, c: (rc(c), g)),
        pl.BlockSpec((q, LANES), lambda g, c: (rc(c), g)),
        pl.BlockSpec((q, LANES), lambda g, c: (rc(c), g)),
        pl.BlockSpec((q, LANES), lambda g, c: (rc(c), g)),
        pl.BlockSpec((SUBLANES, LANES), lambda g, c: (g, 0)),
    ]
    per_group = jax.ShapeDtypeStruct((lp, 2 * LANES), F32)
    return pl.pallas_call(
        body, name=name, grid=(2, nc), in_specs=in_specs, out_specs=out_specs,
        out_shape=[jax.ShapeDtypeStruct((lp, SSD_WIDTH), F32), per_group, per_group, per_group,
                   jax.ShapeDtypeStruct((2 * SUBLANES, LANES), F32)],
        scratch_shapes=[pltpu.VMEM((2 * LANES, LANES), F32)],
        compiler_params=_cp("arbitrary", "arbitrary"),
    )(xbc_c, xbc_c, xbc_c, u_main, bias_row, alog_row, d_row, hprev, dy)


def _sb_blocks(qs, ks, r_runs, masked, bq, after_scores=None):
    ri = lax.broadcasted_iota(jnp.int32, (bq, bq), 0)
    ci = lax.broadcasted_iota(jnp.int32, (bq, bq), 1)
    tri_after = (ri > ci).astype(BF16)
    zs = [_dot(qj, kj, "nt") for qj, kj in zip(qs, ks)]
    extra = after_scores() if after_scores is not None else None
    sigs, ubs = [], []
    for z in zs:
        zb = z.astype(BF16)
        u = -(jnp.maximum(zb, 0) + jnp.log(1 + jnp.exp(-jnp.abs(zb))))
        sigs.append(jnp.exp(zb + u))
        if masked:
            u = jnp.where(ci < ri, u, jnp.zeros_like(u))
        ubs.append(u)
    afters = [_dot(ub, tri_after) for ub in ubs]
    usums = [after[:, 0:1] + ub[:, 0:1].astype(F32) for after, ub in zip(afters, ubs)]
    ws = []
    for sig, after, r_run in zip(sigs, afters, r_runs):
        w = sig * jnp.exp(after + r_run).astype(BF16)
        if masked:
            w = jnp.where(ci < ri, w, jnp.zeros_like(w))
        ws.append(w)
    return usums, sigs, ws, extra


def _split_heads(x, lo):
    out = []
    zero = jnp.zeros((x.shape[0], LANES), x.dtype)
    for p in range(2):
        xp = x[:, LANES * p:LANES * (p + 1)]
        out += [jnp.where(lo, xp, zero), jnp.where(lo, zero, xp)]
    return out


def _per_head(x):
    return [x[:, :LANES], x[:, :LANES], x[:, LANES:], x[:, LANES:]]


def _resident(shape, col):
    return pl.BlockSpec(shape, lambda i: (0, col), pipeline_mode=pl.Buffered(1))


def sb_attn_fwd(qkv, name):
    lp = qkv.shape[0]
    bq = min(ROW_TILE, lp)
    nq = lp // bq
    assert nq <= 64

    def body(q_ref, k_ref, v_ref, o_ref, rs_ref):
        qi = pl.program_id(0)
        lane = _lane((bq, LANES))
        lo = lane < 64
        qs = _split_heads(q_ref[...], lo)

        def step(kb, carry, masked):
            off = pl.multiple_of(kb * bq, bq)
            ks = _per_head(k_ref[pl.ds(off, bq), :])
            vs = _per_head(v_ref[pl.ds(off, bq), :])
            heads, rss = carry
            r_runs = [heads[h][1] for h in range(4)]
            rss = list(rss)
            for h in range(4):
                rss[h // 2] = jnp.where(lane == 64 * (h % 2) + kb, r_runs[h], rss[h // 2])
            usums, _, ws, _ = _sb_blocks(qs, ks, r_runs, masked, bq)
            pvs = [_dot(ws[h], vs[h]) for h in range(4)]
            out = tuple((heads[h][0] + pvs[h], r_runs[h] + usums[h]) for h in range(4))
            return out, tuple(rss)

        zero = (jnp.zeros((bq, LANES), F32), jnp.zeros((bq, 1), F32))
        zr = jnp.zeros((bq, LANES), F32)
        carry = step(qi, ((zero,) * 4, (zr, zr)), True)
        def several(t, c):
            for r in range(KEY_UNROLL):
                c = step(qi - 1 - r - KEY_UNROLL * t, c, False)
            return c

        carry = lax.fori_loop(0, qi // KEY_UNROLL, several, carry)
        rem = qi % KEY_UNROLL
        heads, rss = lax.fori_loop(0, rem, lambda t, c: step(rem - 1 - t, c, False), carry)
        o_ref[...] = jnp.concatenate([jnp.where(lo, heads[0][0], heads[1][0]), jnp.where(lo, heads[2][0], heads[3][0])], axis=1)
        rs_ref[...] = jnp.concatenate(list(rss), axis=1)

    blk = pl.BlockSpec((bq, 2 * LANES), lambda i: (i, 0))
    return pl.pallas_call(
        body, name=name, grid=(nq,),
        in_specs=[blk, _resident((lp, 2 * LANES), 1), _resident((lp, 2 * LANES), 2)],
        out_specs=[blk, blk],
        out_shape=[jax.ShapeDtypeStruct((lp, SB_WIDTH), F32), jax.ShapeDtypeStruct((lp, SB_WIDTH), F32)],
        compiler_params=_cp("arbitrary"),
    )(qkv, qkv, qkv)


def sb_attn_bwd(qkv, rs, d_o, name):
    lp = qkv.shape[0]
    bq = min(ROW_TILE, lp)
    nq = lp // bq

    def body(q_ref, k_ref, v_ref, rs_ref, do_ref, dq_ref, dk_ref, dv_ref):
        qi = pl.program_id(0)

        @pl.when(qi == 0)
        def _():
            dk_ref[...] = jnp.zeros_like(dk_ref)
            dv_ref[...] = jnp.zeros_like(dv_ref)

        lane = _lane((bq, LANES))
        lo = lane < 64
        qs = _split_heads(q_ref[...], lo)
        dos = _split_heads(do_ref[...].astype(BF16), lo)
        rs_blk = rs_ref[...]
        ri = lax.broadcasted_iota(jnp.int32, (bq, bq), 0)
        ci = lax.broadcasted_iota(jnp.int32, (bq, bq), 1)
        tbefore = (ri < ci).astype(BF16)

        def step(kb, carry, masked):
            off = pl.multiple_of(kb * bq, bq)
            ks = _per_head(k_ref[pl.ds(off, bq), :])
            vs = _per_head(v_ref[pl.ds(off, bq), :])
            r_rights = [jnp.sum(jnp.where(lane == 64 * (h % 2) + kb, rs_blk[:, LANES * (h // 2):LANES * (h // 2 + 1)], 0.0),
                                axis=1, keepdims=True) for h in range(4)]
            _, sigs, wbs, dws = _sb_blocks(qs, ks, r_rights, masked, bq,
                                           after_scores=lambda: [_dot(dos[h], vs[h], "nt") for h in range(4)])
            gs = [wbs[h].astype(F32) * dws[h] for h in range(4)]
            gbs = [g.astype(BF16) for g in gs]
            gbefores = [_dot(gb, tbefore) for gb in gbs]
            dv_acc = [_dot(wbs[2 * p], dos[2 * p], "tn") + _dot(wbs[2 * p + 1], dos[2 * p + 1], "tn") for p in range(2)]
            dzbs = []
            for h in range(4):
                dz = gs[h] - sigs[h].astype(F32) * (gs[h] + gbefores[h] + carry[h][1])
                if masked:
                    dz = jnp.where(ci < ri, dz, 0.0)
                dzbs.append(dz.astype(BF16))
            dqs = [_dot(dzbs[h], ks[h]) for h in range(4)]
            dk_acc = [_dot(dzbs[2 * p], qs[2 * p], "tn") + _dot(dzbs[2 * p + 1], qs[2 * p + 1], "tn") for p in range(2)]
            dk_ref[pl.ds(off, bq), :] += jnp.concatenate(dk_acc, axis=1)
            dv_ref[pl.ds(off, bq), :] += jnp.concatenate(dv_acc, axis=1)
            return tuple((carry[h][0] + dqs[h], carry[h][1] + jnp.sum(gs[h], axis=1, keepdims=True)) for h in range(4))

        zero = (jnp.zeros((bq, LANES), F32), jnp.zeros((bq, 1), F32))
        def several(t, c):
            for r in range(KEY_UNROLL):
                c = step(KEY_UNROLL * t + r, c, False)
            return c

        carry = lax.fori_loop(0, qi // KEY_UNROLL, several, (zero,) * 4)
        carry = lax.fori_loop(qi - qi % KEY_UNROLL, qi, lambda t, c: step(t, c, False), carry)
        carry = step(qi, carry, True)
        dq_ref[...] = jnp.concatenate([jnp.where(lo, carry[0][0], carry[1][0]), jnp.where(lo, carry[2][0], carry[3][0])],
                                      axis=1).astype(dq_ref.dtype)

    blk = pl.BlockSpec((bq, 2 * LANES), lambda i: (i, 0))
    return pl.pallas_call(
        body, name=name, grid=(nq,),
        in_specs=[blk, _resident((lp, 2 * LANES), 1), _resident((lp, 2 * LANES), 2), blk, blk],
        out_specs=[blk, _resident((lp, 2 * LANES), 0), _resident((lp, 2 * LANES), 0)],
        out_shape=[jax.ShapeDtypeStruct((lp, SB_WIDTH), BF16), jax.ShapeDtypeStruct((lp, SB_WIDTH), F32),
                   jax.ShapeDtypeStruct((lp, SB_WIDTH), F32)],
        compiler_params=_cp("arbitrary"),
    )(qkv, qkv, qkv, rs, d_o)


def _mla_masks(bq):
    lane = _lane((bq, 2 * LANES))
    out = []
    for h in range(4):
        j = h % 2
        nope = jnp.logical_and(lane >= 64 * j, lane < 64 * (j + 1))
        rope = jnp.logical_and(lane >= LANES + MLA_ROPE * h, lane < LANES + MLA_ROPE * (h + 1))
        out.append(jnp.logical_or(nope, rope))
    return out


def _mla_split_q(q, masks):
    zero = jnp.zeros((q.shape[0], 2 * LANES), q.dtype)
    return [jnp.where(masks[h], q[:, 2 * LANES * (h // 2):2 * LANES * (h // 2 + 1)], zero) for h in range(4)]


def _mla_per_head_k(k):
    return [k[:, :2 * LANES], k[:, :2 * LANES], k[:, 2 * LANES:], k[:, 2 * LANES:]]


def mla_attn_fwd(qc, kc, v, name):
    lp = qc.shape[0]
    bq = min(ROW_TILE, lp)
    nq = lp // bq

    def body(q_ref, k_ref, v_ref, o_ref, lse_ref):
        qi = pl.program_id(0)
        qs = _mla_split_q(q_ref[...], _mla_masks(bq))
        lo = _lane((bq, LANES)) < 64
        ri = lax.broadcasted_iota(jnp.int32, (bq, bq), 0)
        ci = lax.broadcasted_iota(jnp.int32, (bq, bq), 1)

        def blocks(kbs, carry, masked):
            offs = [pl.multiple_of(kb * bq, bq) for kb in kbs]
            ks = [_mla_per_head_k(k_ref[pl.ds(o, bq), :]) for o in offs]
            ones = jnp.ones((bq, LANES), BF16)
            vs = []
            for o in offs:
                vp = _per_head(v_ref[pl.ds(o, bq), :])
                vs.append([jnp.where(lo, vp[h], ones) if h % 2 == 0 else jnp.where(lo, ones, vp[h]) for h in range(4)])
            ss = [[_dot(qs[h], ks[b][h], "nt") for h in range(4)] for b in range(len(kbs))]
            if masked:
                ss = [[jnp.where(ci <= ri, s, NEG) for s in row] for row in ss]
            prs, alphas, ms = [], [], []
            for h in range(4):
                top = ss[0][h]
                for b in range(1, len(kbs)):
                    top = jnp.maximum(top, ss[b][h])
                m_new = jnp.maximum(carry[h][1], jnp.max(top, axis=1, keepdims=True))
                alphas.append(jnp.exp(carry[h][1] - m_new))
                ms.append(m_new)
                prs.append([jnp.exp(ss[b][h] - m_new).astype(BF16) for b in range(len(kbs))])
            out = []
            for h in range(4):
                acc = carry[h][0] * alphas[h]
                for b in range(len(kbs)):
                    acc = acc + _dot(prs[h][b], vs[b][h])
                out.append((acc, ms[h]))
            return tuple(out)

        zero = (jnp.zeros((bq, LANES), F32), jnp.full((bq, 1), NEG, F32))
        carry = blocks([qi], (zero,) * 4, True)
        carry = lax.fori_loop(0, qi // KEY_UNROLL,
                              lambda t, c: blocks([qi - 1 - r - KEY_UNROLL * t for r in range(KEY_UNROLL)], c, False), carry)
        rem = qi % KEY_UNROLL
        carry = lax.fori_loop(0, rem, lambda t, c: blocks([rem - 1 - t], c, False), carry)
        outs, lses = [], []
        for h in range(4):
            acc, m = carry[h]
            l = acc[:, 64:65] if h % 2 == 0 else acc[:, 0:1]
            outs.append(acc / l)
            lses.append(m + jnp.log(l))
        o_ref[...] = jnp.concatenate([jnp.where(lo, outs[0], outs[1]), jnp.where(lo, outs[2], outs[3])], axis=1)
        lse_ref[...] = jnp.concatenate([jnp.where(lo, lses[0], lses[1]), jnp.where(lo, lses[2], lses[3])], axis=1)

    blk = pl.BlockSpec((bq, 2 * LANES), lambda i: (i, 0))
    return pl.pallas_call(
        body, name=name, grid=(nq,),
        in_specs=[pl.BlockSpec((bq, 4 * LANES), lambda i: (i, 0)), _resident((lp, 4 * LANES), 0), _resident((lp, 2 * LANES), 0)],
        out_specs=[blk, blk],
        out_shape=[jax.ShapeDtypeStruct((lp, 2 * LANES), F32), jax.ShapeDtypeStruct((lp, 2 * LANES), F32)],
        compiler_params=_cp("arbitrary"),
    )(qc, kc, v)


def mla_attn_bwd(qc, kc, v, o, lse, d_o, name):
    lp = qc.shape[0]
    bq = min(ROW_TILE, lp)
    nq = lp // bq

    def body(q_ref, k_ref, v_ref, o_ref, lse_ref, do_ref, dq_ref, dk_ref, dv_ref):
        qi = pl.program_id(0)

        @pl.when(qi == 0)
        def _():
            dk_ref[...] = jnp.zeros_like(dk_ref)
            dv_ref[...] = jnp.zeros_like(dv_ref)

        d_o = do_ref[...]
        masks = _mla_masks(bq)
        qs = _mla_split_q(q_ref[...], masks)
        lo = _lane((bq, LANES)) < 64
        dos = _split_heads(d_o.astype(BF16), lo)
        od = o_ref[...] * d_o
        lse_blk = lse_ref[...]
        delta, lses = [], []
        for h in range(4):
            odp = od[:, LANES * (h // 2):LANES * (h // 2 + 1)]
            delta.append(jnp.sum(jnp.where(lo, odp, 0.0) if h % 2 == 0 else jnp.where(lo, 0.0, odp), axis=1, keepdims=True))
            c0 = LANES * (h // 2) + 64 * (h % 2)
            lses.append(lse_blk[:, c0:c0 + 1])
        ri = lax.broadcasted_iota(jnp.int32, (bq, bq), 0)
        ci = lax.broadcasted_iota(jnp.int32, (bq, bq), 1)

        def step(kb, carry, masked):
            off = pl.multiple_of(kb * bq, bq)
            ks = _mla_per_head_k(k_ref[pl.ds(off, bq), :])
            vs = _per_head(v_ref[pl.ds(off, bq), :])
            ss = [_dot(qs[h], ks[h], "nt") for h in range(4)]
            dps = [_dot(dos[h], vs[h], "nt") for h in range(4)]
            prbs, dss = [], []
            for h in range(4):
                s = ss[h]
                if masked:
                    s = jnp.where(ci <= ri, s, NEG)
                pr = jnp.exp(s - lses[h])
                prbs.append(pr.astype(BF16))
                dss.append((pr * (dps[h] - delta[h])).astype(BF16))
            dv_acc = [_dot(prbs[2 * p], dos[2 * p], "tn") + _dot(prbs[2 * p + 1], dos[2 * p + 1], "tn") for p in range(2)]
            dqs = [_dot(dss[h], ks[h]) for h in range(4)]
            dk_acc = [_dot(dss[2 * p], qs[2 * p], "tn") + _dot(dss[2 * p + 1], qs[2 * p + 1], "tn") for p in range(2)]
            dk_ref[pl.ds(off, bq), :] += jnp.concatenate(dk_acc, axis=1)
            dv_ref[pl.ds(off, bq), :] += jnp.concatenate(dv_acc, axis=1)
            return tuple(carry[h] + dqs[h] for h in range(4))

        zero = jnp.zeros((bq, 2 * LANES), F32)
        carry = step(qi, (zero,) * 4, True)
        def several(t, c):
            for r in range(KEY_UNROLL):
                c = step(qi - 1 - r - KEY_UNROLL * t, c, False)
            return c

        carry = lax.fori_loop(0, qi // KEY_UNROLL, several, carry)
        rem = qi % KEY_UNROLL
        carry = lax.fori_loop(0, rem, lambda t, c: step(rem - 1 - t, c, False), carry)
        dq_ref[...] = jnp.concatenate([jnp.where(masks[0], carry[0], 0.0) + jnp.where(masks[1], carry[1], 0.0),
                                       jnp.where(masks[2], carry[2], 0.0) + jnp.where(masks[3], carry[3], 0.0)], axis=1)

    blk = pl.BlockSpec((bq, 2 * LANES), lambda i: (i, 0))
    wide = pl.BlockSpec((bq, 4 * LANES), lambda i: (i, 0))
    return pl.pallas_call(
        body, name=name, grid=(nq,),
        in_specs=[wide, _resident((lp, 4 * LANES), 0), _resident((lp, 2 * LANES), 0), blk, blk, blk],
        out_specs=[wide, _resident((lp, 4 * LANES), 0), _resident((lp, 2 * LANES), 0)],
        out_shape=[jax.ShapeDtypeStruct((lp, 4 * LANES), F32), jax.ShapeDtypeStruct((lp, 4 * LANES), F32),
                   jax.ShapeDtypeStruct((lp, 2 * LANES), F32)],
        compiler_params=_cp("arbitrary"),
    )(qc, kc, v, o, lse, d_o)


def _mix_out(y_pre, z, o_sb, o_mla, g_ssd, g_sb, g_mla):
    return jnp.concatenate([_rms(y_pre * _silu(z), g_ssd), _rms(o_sb, g_sb), _rms(o_mla, g_mla)], axis=1)


def _ffn_act(up_a, up_b, halo_a, halo_b, w_a, w_b, b_a, b_b, i):
    ca = conv_fwd(up_a, halo_a, w_a, i) + b_a
    cb_ = conv_fwd(up_b, halo_b, w_b, i) + b_b
    return ca, cb_


def layer_fwd(h, w, cs, sn, l):
    lp = h.shape[0]
    nm = f"l{l}_"
    hn = rowwise(lambda i, n, x, g: _rms(x, g), [RI(h), PA(w["norm_mix_g"])], [RO(D_MODEL, BF16)], nm + "rms_mix", lp, tm=WIDE_TILE)[0]
    u = mm(hn, w["w_main"], "nn", F32, nm + "in_main")
    qkv = mm(hn, w["w_sb"], "nn", BF16, nm + "in_sb")
    xbc_c = rowwise(lambda i, n, x, hl, cw, cb_: _silu(conv_fwd(x, hl, cw, i) + cb_),
                    [RI(u, SSD_XBC, 0), HP(u, SSD_XBC, 0), PA(w["ssd_conv_w"]), PA(w["ssd_conv_b"])],
                    [RO(SSD_XBC, F32)], nm + "ssd_conv", lp)[0]
    y_pre, hprev = ssd_fwd(xbc_c, u, w["dt_bias"], w["a_log"], w["d_skip"], nm + "ssd_fwd")
    o_sb, rs_sb = sb_attn_fwd(qkv, nm + "sb_fwd")
    qn, kvn = rowwise(lambda i, n, qa, ckv, gq, gkv: (_rms(qa, gq, MLA_Q_RANK), _rms(ckv, gkv)),
                      [RI(u, 256, U_QA // 256), RI(u, LANES, U_CKV // LANES), PA(w["q_norm_g"]), PA(w["kv_norm_g"])],
                      [RO(256, BF16), RO(LANES, BF16)], nm + "mla_rms", lp, tm=WIDE_TILE)
    qf = mm(qn, w["w_uq"], "nn", F32, nm + "mla_uq")
    kvf = mm(kvn, w["w_ukv"], "nn", F32, nm + "mla_ukv")

    def pack(i, n, qf_, kvf_, kr4, cos, sin):
        qf_ = qf_ * MLA_SCALE
        qr = qf_[:, 256:384]
        qr = qr * cos + rope_rot(qr) * sin
        kr = kr4 * cos + rope_rot(kr4) * sin
        qc = jnp.concatenate([qf_[:, 0:128], qr, qf_[:, 128:256], qr], axis=1)
        kc = jnp.concatenate([kvf_[:, 0:128], kr, kvf_[:, 128:256], kr], axis=1)
        return qc, kc, kvf_[:, 256:512]

    qc, kc, vv = rowwise(pack, [RI(qf), RI(kvf), RI(u, LANES, U_KR4 // LANES), RI(cs), RI(sn)],
                         [RO(512, BF16), RO(512, BF16), RO(256, BF16)], nm + "mla_pack", lp, tm=WIDE_TILE)
    o_mla, lse = mla_attn_fwd(qc, kc, vv, nm + "mla_fwd")
    cat = rowwise(lambda i, n, *a: _mix_out(*a),
                  [RI(y_pre), RI(u, SSD_WIDTH, U_Z // SSD_WIDTH), RI(o_sb), RI(o_mla),
                   PA(w["ssd_norm_g"]), PA(w["sb_norm_g"]), PA(w["mla_norm_g"])],
                  [RO(D_MODEL, BF16)], nm + "mix_out", lp, tm=WIDE_TILE)[0]
    h_mid = mm(cat, w["w_out"], "nn", F32, nm + "out_proj", add=h)
    hn2 = rowwise(lambda i, n, x, g: _rms(x, g), [RI(h_mid), PA(w["norm_ffn_g"])], [RO(D_MODEL, BF16)], nm + "rms_ffn", lp, tm=WIDE_TILE)[0]
    up_a = mm(hn2, w["w_up_a"], "nn", F32, nm + "up_a")
    up_b = mm(hn2, w["w_up_b"], "nn", F32, nm + "up_b")
    wc = 1408

    def act(i, n, ua, ub, ha, hb_, wa, wb, ba, bb_):
        ca, cb_ = _ffn_act(ua, ub, ha, hb_, wa, wb, ba, bb_, i)
        return _silu(ca) * cb_

    a_t = rowwise(act, [RI(up_a, wc, 0, True), RI(up_b, wc, 0, True), HP(up_a, wc, 0, True), HP(up_b, wc, 0, True),
                        PA(w["ffn_conv_w_a"], wc, 0, True), PA(w["ffn_conv_w_b"], wc, 0, True),
                        PA(w["ffn_conv_b_a"], wc, 0, True), PA(w["ffn_conv_b_b"], wc, 0, True)],
                  [RO(D_FF, BF16, wc, True)], nm + "ffn_act", lp, ncol=D_FF // wc)[0]
    h_out = mm(a_t, w["w_down"], "nn", F32, nm + "down", add=h_mid)
    saved = dict(h=h, hn=hn, u=u, qkv=qkv, xbc_c=xbc_c, y_pre=y_pre, hprev=hprev, o_sb=o_sb, rs_sb=rs_sb, qn=qn, kvn=kvn,
                 qc=qc, kc=kc, vv=vv, o_mla=o_mla, lse=lse, cat=cat, h_mid=h_mid, hn2=hn2, up_a=up_a, up_b=up_b, a_t=a_t)
    return h_out, saved


def layer_bwd(dh_out, w, s, cs, sn, l):
    lp = dh_out.shape[0]
    nm = f"l{l}b_"
    g = {}
    wc = 1408
    ncolf = D_FF // wc
    g["w_down"] = mm(s["a_t"], dh_out, "tn", BF16, nm + "dw_down")
    d_act = mm(dh_out, w["w_down"], "nt", F32, nm + "d_act")

    def act_bwd(i, n, ua, ub, ha, hb_, wa, wb, ba, bb_, da_):
        ca, cb_ = _ffn_act(ua, ub, ha, hb_, wa, wb, ba, bb_, i)
        sg = jax.nn.sigmoid(ca)
        dca = da_ * cb_ * (sg * (1.0 + ca * (1.0 - sg)))
        dcb = da_ * (ca * sg)
        return (dca, dcb, conv_bwd_w(dca, ua, ha, i, FFN_CONV), conv_bwd_w(dcb, ub, hb_, i, FFN_CONV),
                jnp.sum(dca, axis=0, keepdims=True), jnp.sum(dcb, axis=0, keepdims=True))

    dca, dcb, g["ffn_conv_w_a"], g["ffn_conv_w_b"], g["ffn_conv_b_a"], g["ffn_conv_b_b"] = rowwise(
        act_bwd, [RI(s["up_a"], wc, 0, True), RI(s["up_b"], wc, 0, True), HP(s["up_a"], wc, 0, True),
                  HP(s["up_b"], wc, 0, True), PA(w["ffn_conv_w_a"], wc, 0, True), PA(w["ffn_conv_w_b"], wc, 0, True),
                  PA(w["ffn_conv_b_a"], wc, 0, True), PA(w["ffn_conv_b_b"], wc, 0, True), RI(d_act, wc, 0, True)],
        [RO(D_FF, F32, wc, True), RO(D_FF, F32, wc, True), AO(FFN_CONV, D_FF, wc, True), AO(FFN_CONV, D_FF, wc, True),
         AO(1, D_FF, wc, True), AO(1, D_FF, wc, True)], nm + "ffn_act_bwd", lp, ncol=ncolf)

    def conv_t(i, n, da_, db_, ha, hb_, wa, wb):
        return conv_bwd_data(da_, ha, wa, i, n), conv_bwd_data(db_, hb_, wb, i, n)

    dup_a, dup_b = rowwise(conv_t, [RI(dca, wc, 0, True), RI(dcb, wc, 0, True), HN(dca, wc, 0, True), HN(dcb, wc, 0, True),
                                    PA(w["ffn_conv_w_a"], wc, 0, True), PA(w["ffn_conv_w_b"], wc, 0, True)],
                           [RO(D_FF, BF16, wc, True), RO(D_FF, BF16, wc, True)], nm + "ffn_conv_t", lp, ncol=ncolf)
    g["w_up_a"] = mm(s["hn2"], dup_a, "tn", BF16, nm + "dw_up_a")
    g["w_up_b"] = mm(s["hn2"], dup_b, "tn", BF16, nm + "dw_up_b")
    dhn2 = mm(dup_a, w["w_up_a"], "nt", F32, nm + "dhn2_a")
    dhn2 = mm(dup_b, w["w_up_b"], "nt", F32, nm + "dhn2_b", add=dhn2)

    def rms_bwd(i, n, x, gg, dy, dres):
        _, vjp = jax.vjp(_rms, x, gg)
        dx, dg = vjp(dy)
        return dres + dx, dg

    dh_mid, g["norm_ffn_g"] = rowwise(rms_bwd, [RI(s["h_mid"]), PA(w["norm_ffn_g"]), RI(dhn2), RI(dh_out)],
                                      [RO(D_MODEL, F32), AO(1, D_MODEL)], nm + "rms_ffn_bwd", lp, tm=WIDE_TILE)
    g["w_out"] = mm(s["cat"], dh_mid, "tn", BF16, nm + "dw_out")
    d_cat = mm(dh_mid, w["w_out"], "nt", F32, nm + "d_cat")
    u = s["u"]

    def mix_bwd(i, n, y_pre, z, o_sb, o_mla, g1, g2, g3, dcat):
        _, vjp = jax.vjp(_mix_out, y_pre, z, o_sb, o_mla, g1, g2, g3)
        return vjp(dcat)

    dy_pre, dz, do_sb, do_mla, g["ssd_norm_g"], g["sb_norm_g"], g["mla_norm_g"] = rowwise(
        mix_bwd, [RI(s["y_pre"]), RI(u, SSD_WIDTH, U_Z // SSD_WIDTH), RI(s["o_sb"]), RI(s["o_mla"]),
                  PA(w["ssd_norm_g"]), PA(w["sb_norm_g"]), PA(w["mla_norm_g"]), RI(d_cat)],
        [RO(SSD_WIDTH, F32), RO(SSD_WIDTH, BF16), RO(SB_WIDTH, F32), RO(256, F32),
         AO(1, SSD_WIDTH), AO(1, SB_WIDTH), AO(1, 256)], nm + "mix_out_bwd", lp, tm=WIDE_TILE)
    dxs, dbp, dcp, ddtp, pg = ssd_bwd(s["xbc_c"], u, w["dt_bias"], w["a_log"], w["d_skip"], s["hprev"], dy_pre, nm + "ssd_bwd")
    pg = pg.reshape(2, SUBLANES, LANES).sum(axis=0)
    g["dt_bias"], g["a_log"], g["d_skip"] = pg[0:1], pg[1:2], pg[2:3]

    def conv4_bwd(i, n, x, hl, cw, cb_, dxs_, dbp_, dcp_, ddtp_):
        pre = conv_fwd(x, hl, cw, i) + cb_
        d_out = jnp.concatenate([dxs_, dbp_, dcp_], axis=1)
        sg = jax.nn.sigmoid(pre)
        d_pre = d_out * (sg * (1.0 + pre * (1.0 - sg)))
        ddt = ddtp_[:, 0:128] + ddtp_[:, 128:256]
        return d_pre, ddt, conv_bwd_w(d_pre, x, hl, i, SSD_CONV), jnp.sum(d_pre, axis=0, keepdims=True)

    d_pre, ddt, g["ssd_conv_w"], g["ssd_conv_b"] = rowwise(
        conv4_bwd, [RI(u, SSD_XBC, 0), HP(u, SSD_XBC, 0), PA(w["ssd_conv_w"]), PA(w["ssd_conv_b"]),
                    RI(dxs), RI(dbp), RI(dcp), RI(ddtp)],
        [RO(SSD_XBC, F32), RO(LANES, BF16), AO(SSD_CONV, SSD_XBC), AO(1, SSD_XBC)], nm + "ssd_conv_bwd", lp)
    d_xbc = rowwise(lambda i, n, d, hn_, cw: conv_bwd_data(d, hn_, cw, i, n),
                    [RI(d_pre), HN(d_pre), PA(w["ssd_conv_w"])], [RO(SSD_XBC, BF16)], nm + "ssd_conv_t", lp)[0]
    dq_sb, dk_sb, dv_sb = sb_attn_bwd(s["qkv"], s["rs_sb"], do_sb, nm + "sb_bwd")
    dqkv = jnp.concatenate([dq_sb, dk_sb.astype(BF16), dv_sb.astype(BF16)], axis=1)
    dqc, dkc, dvv = mla_attn_bwd(s["qc"], s["kc"], s["vv"], s["o_mla"], s["lse"], do_mla, nm + "mla_bwd")

    def unpack(i, n, dqc_, dkc_, dvv_, cos, sin):
        dqr = dqc_[:, 128:256] + dqc_[:, 384:512]
        dqr = dqr * cos + rope_rot_t(dqr * sin)
        dkr = dkc_[:, 128:256] + dkc_[:, 384:512]
        dkr = dkr * cos + rope_rot_t(dkr * sin)
        dq = jnp.concatenate([dqc_[:, 0:128], dqc_[:, 256:384], dqr], axis=1) * MLA_SCALE
        dkv = jnp.concatenate([dkc_[:, 0:128], dkc_[:, 256:384], dvv_], axis=1)
        return dq, dkv, dkr

    dq, dkv, dkr4 = rowwise(unpack, [RI(dqc), RI(dkc), RI(dvv), RI(cs), RI(sn)],
                            [RO(384, BF16), RO(512, BF16), RO(LANES, BF16)], nm + "mla_unpack", lp, tm=WIDE_TILE)
    g["w_uq"] = mm(s["qn"], dq, "tn", F32, nm + "dw_uq")
    g["w_ukv"] = mm(s["kvn"], dkv, "tn", F32, nm + "dw_ukv")
    dqn = mm(dq, w["w_uq"], "nt", F32, nm + "dqn")
    dkvn = mm(dkv, w["w_ukv"], "nt", F32, nm + "dkvn")

    def mla_rms_bwd(i, n, qa, ckv, gq, gkv, dqn_, dkvn_):
        _, vjp = jax.vjp(lambda a, b, c, d: (_rms(a, c, MLA_Q_RANK), _rms(b, d)), qa, ckv, gq, gkv)
        return vjp((dqn_, dkvn_))

    dqa, dckv, g["q_norm_g"], g["kv_norm_g"] = rowwise(
        mla_rms_bwd, [RI(u, 256, U_QA // 256), RI(u, LANES, U_CKV // LANES), PA(w["q_norm_g"]), PA(w["kv_norm_g"]),
                      RI(dqn), RI(dkvn)],
        [RO(256, BF16), RO(LANES, BF16), AO(1, 256), AO(1, LANES)], nm + "mla_rms_bwd", lp, tm=WIDE_TILE)
    du = jnp.concatenate([d_xbc, dz, dqa, dckv, dkr4, ddt, jnp.zeros((lp, LANES), BF16)], axis=1)
    g["w_main"] = mm(s["hn"], du, "tn", F32, nm + "dw_main")
    g["w_sb"] = mm(s["hn"], dqkv, "tn", F32, nm + "dw_sb")
    dhn = mm(du, w["w_main"], "nt", F32, nm + "dhn_main")
    dhn = mm(dqkv, w["w_sb"], "nt", F32, nm + "dhn_sb", add=dhn)
    dh_in, g["norm_mix_g"] = rowwise(rms_bwd, [RI(s["h"]), PA(w["norm_mix_g"]), RI(dhn), RI(dh_mid)],
                                     [RO(D_MODEL, F32), AO(1, D_MODEL)], nm + "rms_mix_bwd", lp, tm=WIDE_TILE)
    return dh_in, g


_IN_CUTS = np.cumsum((512, 1024, 8, 256, 256, 256, 192, 128, 32))


def _pad_cols(a, n):
    return jnp.pad(a, ((0, 0), (0, n - a.shape[1])))


def prep_layer_weights(full, l):
    w_in = full["w_in"][l]
    c = _IN_CUTS
    z, xbc, dtr = w_in[:, :c[0]], w_in[:, c[0]:c[1]], w_in[:, c[1]:c[2]]
    q_sb, k_sb, v_sb = w_in[:, c[2]:c[3]], w_in[:, c[3]:c[4]], w_in[:, c[4]:c[5]]
    q_a, c_kv, k_r = w_in[:, c[5]:c[6]], w_in[:, c[6]:c[7]], w_in[:, c[7]:c[8]]
    w_main = jnp.concatenate([xbc, z, _pad_cols(q_a, 256), c_kv, k_r, k_r, k_r, k_r, _pad_cols(dtr, 256)], axis=1)
    assert w_main.shape[1] == U_MAIN
    row = lambda v, n=None: _pad_cols(v.reshape(1, -1).astype(F32), v.size if n is None else n)
    uq = full["mla_w_uq"][l].reshape(MLA_Q_RANK, 4, 96)
    w_uq = jnp.concatenate([uq[:, :, :64].reshape(MLA_Q_RANK, 256), uq[:, :, 64:].reshape(MLA_Q_RANK, 128)], axis=1)
    w_uq = jnp.pad(w_uq, ((0, 256 - MLA_Q_RANK), (0, 0)))
    ukv = full["mla_w_ukv"][l].reshape(MLA_KV_RANK, 4, 128)
    w_ukv = jnp.concatenate([ukv[:, :, :64].reshape(MLA_KV_RANK, 256), ukv[:, :, 64:].reshape(MLA_KV_RANK, 256)], axis=1)
    return dict(
        norm_mix_g=row(full["norm_mix_g"][l]), w_main=w_main, w_sb=jnp.concatenate([q_sb * SB_SCALE, k_sb, v_sb], axis=1),
        ssd_conv_w=full["ssd_conv_w"][l], ssd_conv_b=row(full["ssd_conv_b"][l]),
        dt_bias=row(full["ssd_dt_bias"][l], LANES), a_log=row(full["ssd_a_log"][l], LANES), d_skip=row(full["ssd_d"][l], LANES),
        ssd_norm_g=row(full["ssd_norm_g"][l]), sb_norm_g=row(full["sb_norm_g"][l]),
        q_norm_g=row(full["mla_q_norm_g"][l], 256), kv_norm_g=row(full["mla_kv_norm_g"][l]),
        w_uq=w_uq, w_ukv=w_ukv, mla_norm_g=row(full["mla_norm_g"][l]),
        w_out=full["w_out"][l], norm_ffn_g=row(full["norm_ffn_g"][l]),
        w_up_a=full["ffn_w_up"][l][:, :D_FF], w_up_b=full["ffn_w_up"][l][:, D_FF:],
        ffn_conv_w_a=full["ffn_conv_w"][l][:, :D_FF], ffn_conv_w_b=full["ffn_conv_w"][l][:, D_FF:],
        ffn_conv_b_a=row(full["ffn_conv_b"][l][:D_FF]), ffn_conv_b_b=row(full["ffn_conv_b"][l][D_FF:]),
        w_down=full["ffn_w_down"][l],
    )


def unprep_layer_grads(g):
    wm = g["w_main"]
    xbc, z = wm[:, U_XBC:U_XBC + 1024], wm[:, U_Z:U_Z + 512]
    q_a, c_kv = wm[:, U_QA:U_QA + MLA_Q_RANK], wm[:, U_CKV:U_CKV + 128]
    k_r = (wm[:, U_KR4:U_KR4 + 32] + wm[:, U_KR4 + 32:U_KR4 + 64] + wm[:, U_KR4 + 64:U_KR4 + 96] + wm[:, U_KR4 + 96:U_KR4 + 128])
    dtr = wm[:, U_DT:U_DT + SSD_HEADS]
    w_sb = g["w_sb"]
    w_in = jnp.concatenate([z, xbc, dtr, w_sb[:, :SB_WIDTH] * SB_SCALE, w_sb[:, SB_WIDTH:], q_a, c_kv, k_r], axis=1)
    guq = g["w_uq"][:MLA_Q_RANK]
    guq = jnp.concatenate([guq[:, :256].reshape(MLA_Q_RANK, 4, 64), guq[:, 256:].reshape(MLA_Q_RANK, 4, 32)], axis=2)
    gukv = g["w_ukv"]
    gukv = jnp.concatenate([gukv[:, :256].reshape(MLA_KV_RANK, 4, 64), gukv[:, 256:].reshape(MLA_KV_RANK, 4, 64)], axis=2)
    return dict(
        norm_mix_g=g["norm_mix_g"][0], w_in=w_in, ssd_conv_w=g["ssd_conv_w"], ssd_conv_b=g["ssd_conv_b"][0],
        ssd_dt_bias=g["dt_bias"][0, :SSD_HEADS], ssd_a_log=g["a_log"][0, :SSD_HEADS], ssd_d=g["d_skip"][0, :SSD_HEADS],
        ssd_norm_g=g["ssd_norm_g"][0], sb_norm_g=g["sb_norm_g"][0], mla_q_norm_g=g["q_norm_g"][0, :MLA_Q_RANK],
        mla_kv_norm_g=g["kv_norm_g"][0], mla_w_uq=guq.reshape(MLA_Q_RANK, 384), mla_w_ukv=gukv.reshape(MLA_KV_RANK, 512),
        mla_norm_g=g["mla_norm_g"][0], w_out=g["w_out"], norm_ffn_g=g["norm_ffn_g"][0],
        ffn_w_up=jnp.concatenate([g["w_up_a"], g["w_up_b"]], axis=1),
        ffn_conv_w=jnp.concatenate([g["ffn_conv_w_a"], g["ffn_conv_w_b"]], axis=1),
        ffn_conv_b=jnp.concatenate([g["ffn_conv_b_a"][0], g["ffn_conv_b_b"][0]], axis=0),
        ffn_w_down=g["w_down"],
    )


def rope_tables(lp):
    pos = jnp.arange(lp, dtype=F32)
    inv = 1.0 / (ROPE_BASE ** (jnp.arange(0, MLA_ROPE, 2, dtype=F32) / MLA_ROPE))
    ang = pos[:, None] * inv[None, :]
    ang = jnp.concatenate([ang, ang] * 4, axis=-1)
    return jnp.cos(ang), jnp.sin(ang)


def local_step(x_seq, target, full):
    seq = x_seq.shape[0]
    length = seq + N_META
    lp = -(-length // ROW_TILE) * ROW_TILE
    cs, sn = rope_tables(lp)
    h = jnp.concatenate([full["meta_tokens"].astype(F32), x_seq, jnp.zeros((lp - length, D_MODEL), F32)], axis=0)
    tgt = jnp.pad(target, ((N_META, lp - length), (0, 0)))
    ws = [prep_layer_weights(full, l) for l in range(DEPTH)]
    saved = []
    for l in range(DEPTH):
        h, s = layer_fwd(h, ws[l], cs, sn, l)
        saved.append(s)
    fg = full["final_norm_g"].reshape(1, D_MODEL).astype(F32)
    tm = min(ROW_TILE, lp)

    def loss_fn(i, n, x, g, t):
        rows = _rows_iota(x) + i * tm
        valid = jnp.logical_and(rows >= N_META, rows < length)

        def f(x_, g_):
            err = jnp.where(valid, _rms(x_, g_) - t, 0.0)
            return 0.5 * jnp.sum(err * err) * (1.0 / D_MODEL)

        val, (dx, dg) = jax.value_and_grad(f, argnums=(0, 1))(x, g)
        return dx, jnp.full((1, LANES), val, F32), dg

    dh, loss_row, g_final = rowwise(loss_fn, [RI(h), PA(fg), RI(tgt)], [RO(D_MODEL, F32), AO(1, LANES), AO(1, D_MODEL)],
                                    "loss_head", lp)
    grads = {}
    per_layer = [None] * DEPTH
    for l in reversed(range(DEPTH)):
        dh, g = layer_bwd(dh, ws[l], saved[l], cs, sn, l)
        per_layer[l] = unprep_layer_grads(g)
    for k in per_layer[0]:
        grads[k] = jnp.stack([per_layer[l][k] for l in range(DEPTH)], axis=0)
    grads["final_norm_g"] = g_final[0]
    grads["meta_tokens"] = dh[:N_META]
    return loss_row[0, 0], dh[N_META:length], grads


_ANY = pl.BlockSpec(memory_space=pl.ANY)


def chip_exchange(srcs, modes, name):
    n = len(srcs)
    flips = ((1, 0), (0, 1), (1, 1))

    def body(*refs):
        ins, outs = refs[:n], refs[n:2 * n]
        send_sems, recv_sems, fwd_send_sems, fwd_recv_sems, loc_sems = refs[2 * n:]
        x, y, c = lax.axis_index("x"), lax.axis_index("y"), lax.axis_index("c")
        me = 2 * x + y
        waits, forwards = [], []
        for a in range(n):
            whole = modes[a] != "slab"
            cp = pltpu.make_async_copy(ins[a] if whole else ins[a].at[me], outs[a].at[me], loc_sems.at[a])
            cp.start()
            waits.append(cp.wait)
            half = ins[a].shape[0] // 2 if modes[a] == "bcast_split" else None
            for k, (fx, fy) in enumerate(flips):
                px = 1 - x if fx else x
                py = 1 - y if fy else y
                peer = 2 * px + py
                if half is None:
                    src = ins[a] if whole else ins[a].at[peer]
                    dst = outs[a].at[me]
                else:
                    src = ins[a].at[pl.ds(c * half, half)]
                    dst = outs[a].at[me, pl.ds(c * half, half)]
                rc = pltpu.make_async_remote_copy(src_ref=src, dst_ref=dst, send_sem=send_sems.at[a, k],
                                                  recv_sem=recv_sems.at[a, k], device_id=(px, py, c), device_id_type=MESH_ID)
                rc.start()
                if half is None:
                    waits.append(rc.wait)
                else:
                    waits.append(rc.wait_send)
                    landed = outs[a].at[peer, pl.ds(c * half, half)]
                    fw = pltpu.make_async_remote_copy(src_ref=landed, dst_ref=landed, send_sem=fwd_send_sems.at[a, k],
                                                      recv_sem=fwd_recv_sems.at[a, k], device_id=(x, y, 1 - c),
                                                      device_id_type=MESH_ID)
                    forwards.append((rc, fw))
        for rc, fw in forwards:
            rc.wait_recv()
            fw.start()
        for rc, fw in forwards:
            fw.wait()
        for w in waits:
            w()

    out_shape = [jax.ShapeDtypeStruct((N_CHIPS,) + (s.shape if m != "slab" else s.shape[1:]), s.dtype) for s, m in zip(srcs, modes)]
    return pl.pallas_call(
        body, name=name, in_specs=[_ANY] * n, out_specs=[_ANY] * n, out_shape=out_shape,
        scratch_shapes=[pltpu.SemaphoreType.DMA((n, 3)), pltpu.SemaphoreType.DMA((n, 3)), pltpu.SemaphoreType.DMA((n, 3)),
                        pltpu.SemaphoreType.DMA((n, 3)), pltpu.SemaphoreType.DMA((n,))],
    )(*srcs)


def _piece(ref, mode, k):
    if mode == "slab":
        return ref.at[k]
    if mode == "rows":
        rs = ref.shape[1] // N_CHIPS
        return ref.at[:, pl.ds(pl.multiple_of(k * rs, 16), rs), :]
    if mode == "cols":
        cs = ref.shape[2] // N_CHIPS
        return ref.at[:, :, pl.ds(pl.multiple_of(k * cs, LANES), cs)]
    return ref


def _piece_shape(shape, mode):
    if mode == "slab":
        return shape[1:]
    if mode == "rows":
        return (shape[0], shape[1] // N_CHIPS, shape[2])
    if mode == "cols":
        return (shape[0], shape[1], shape[2] // N_CHIPS)
    return shape


def grad_exchange(srcs, modes, name):
    n = len(srcs)
    flips = ((1, 0), (0, 1), (1, 1))

    def body(*refs):
        ins, outs = refs[:n], refs[n:2 * n]
        send_sems, recv_sems, fwd_send_sems, fwd_recv_sems, sib_send_sems, sib_recv_sems, loc_sems = refs[2 * n:]
        x, y, c = lax.axis_index("x"), lax.axis_index("y"), lax.axis_index("c")
        me = 2 * x + y
        sibling = (x, y, 1 - c)
        waits, forwards = [], []
        for a in range(n):
            mine = _piece(ins[a], modes[a], me)
            slot = outs[a].at[4 * c + me]
            cp = pltpu.make_async_copy(mine, slot, loc_sems.at[a])
            cp.start()
            sb = pltpu.make_async_remote_copy(src_ref=mine, dst_ref=slot, send_sem=sib_send_sems.at[a],
                                              recv_sem=sib_recv_sems.at[a], device_id=sibling, device_id_type=MESH_ID)
            sb.start()
            waits += [cp.wait, sb.wait]
            for k, (fx, fy) in enumerate(flips):
                px = 1 - x if fx else x
                py = 1 - y if fy else y
                peer = 2 * px + py
                rc = pltpu.make_async_remote_copy(src_ref=_piece(ins[a], modes[a], peer), dst_ref=slot,
                                                  send_sem=send_sems.at[a, k], recv_sem=recv_sems.at[a, k],
                                                  device_id=(px, py, c), device_id_type=MESH_ID)
                rc.start()
                landed = outs[a].at[4 * c + peer]
                fw = pltpu.make_async_remote_copy(src_ref=landed, dst_ref=landed, send_sem=fwd_send_sems.at[a, k],
                                                  recv_sem=fwd_recv_sems.at[a, k], device_id=sibling, device_id_type=MESH_ID)
                waits.append(rc.wait_send)
                forwards.append((rc, fw))
        for rc, fw in forwards:
            rc.wait_recv()
            fw.start()
        for rc, fw in forwards:
            fw.wait()
        for w in waits:
            w()

    out_shape = [jax.ShapeDtypeStruct((2 * N_CHIPS,) + tuple(_piece_shape(s.shape, m)), s.dtype) for s, m in zip(srcs, modes)]
    dma = pltpu.SemaphoreType.DMA
    return pl.pallas_call(
        body, name=name, in_specs=[_ANY] * n, out_specs=[_ANY] * n, out_shape=out_shape,
        scratch_shapes=[dma((n, 3)), dma((n, 3)), dma((n, 3)), dma((n, 3)), dma((n,)), dma((n,)), dma((n,))],
    )(*srcs)


WEIGHT_NAMES = ("meta_tokens", "norm_mix_g", "w_in", "ssd_conv_w", "ssd_conv_b", "ssd_dt_bias", "ssd_a_log", "ssd_d",
                "ssd_norm_g", "sb_norm_g", "mla_q_norm_g", "mla_kv_norm_g", "mla_w_uq", "mla_w_ukv", "mla_norm_g",
                "w_out", "norm_ffn_g", "ffn_w_up", "ffn_conv_w", "ffn_conv_b", "ffn_w_down", "final_norm_g")
SHARD_AXIS = {"meta_tokens": 1, "w_in": 2, "ssd_conv_w": 2, "mla_w_uq": 2, "mla_w_ukv": 2, "w_out": 1, "ffn_w_up": 2,
              "ffn_conv_w": 2, "ffn_w_down": 1}
SHARDED = tuple(n for n in WEIGHT_NAMES if n in SHARD_AXIS)
REPLICATED = tuple(n for n in WEIGHT_NAMES if n not in SHARD_AXIS)
GATHER_BF16 = ("w_in", "mla_w_uq", "mla_w_ukv", "w_out", "ffn_w_up", "ffn_w_down")
GATHER_F32 = ("meta_tokens", "ssd_conv_w", "ffn_conv_w")
PACK_ROWS = ROW_TILE


def pack(arrs, dtype):
    flat = jnp.concatenate([a.reshape(-1).astype(dtype) for a in arrs])
    per = PACK_ROWS * PACK_W
    total = -(-flat.size // per) * per
    return jnp.pad(flat, (0, total - flat.size)).reshape(total // PACK_W, PACK_W)


def unpack(buf, shapes):
    flat = buf.reshape(-1)
    out, off = [], 0
    for shp in shapes:
        size = int(np.prod(shp))
        out.append(flat[off:off + size].reshape(shp))
        off += size
    return out


def gather_weights(a):
    full = {n: a[n] for n in REPLICATED}
    bufs = [pack([a[n] for n in GATHER_BF16], BF16), pack([a[n] for n in GATHER_F32], F32)]
    got = chip_exchange(bufs, ("bcast_split", "bcast"), "gather_weights")
    for names, g in ((GATHER_BF16, got[0]), (GATHER_F32, got[1])):
        pieces = [unpack(g[k], [a[n].shape for n in names]) for k in range(N_CHIPS)]
        for idx, n in enumerate(names):
            full[n] = jnp.concatenate([pieces[k][idx] for k in range(N_CHIPS)], axis=SHARD_AXIS[n])
    return full


BIG = ("w_in", "w_out", "ffn_w_up", "ffn_w_down")
BIG_MODE = {"w_in": "slab", "w_out": "rows", "ffn_w_up": "cols", "ffn_w_down": "rows"}
SMALL_SHARDED = tuple(n for n in SHARDED if n not in BIG)
ADAM_TILE = 128


def _adamw(i, n, *vals):
    parts, (w, m, v) = vals[:2 * N_CHIPS], vals[2 * N_CHIPS:]
    g = parts[0].astype(F32)
    for p in parts[1:]:
        g = g + p.astype(F32)
    m = ADAM_B1 * m + (1.0 - ADAM_B1) * g
    v = ADAM_B2 * v + (1.0 - ADAM_B2) * jnp.square(g)
    m_hat = m / (1.0 - ADAM_B1 ** ADAM_STEP)
    v_hat = v / (1.0 - ADAM_B2 ** ADAM_STEP)
    delta = -ADAM_LR * (m_hat / (jnp.sqrt(v_hat) + ADAM_EPS) + ADAM_WD * w)
    return g, delta, m, v


def _adamw_call(got, w, m, v, name):
    rows, width = w.shape
    flat = got.reshape(2 * N_CHIPS * rows, width)
    blk = rows // ADAM_TILE
    ins = [RI(flat, rblk=k * blk) for k in range(2 * N_CHIPS)] + [RI(w), RI(m), RI(v)]
    return rowwise(_adamw, ins, [RO(width, F32)] * 4, name, rows, tm=ADAM_TILE)


def reduce_and_update(a, grads):
    srcs, modes = [], []
    for n in BIG:
        g = grads[n].astype(BF16)
        if n == "w_in":
            cs = a[n].shape[2]
            g = g.reshape(DEPTH, D_MODEL, N_CHIPS, cs).transpose(2, 0, 1, 3)
        srcs.append(g)
        modes.append(BIG_MODE[n])
    slabs = []
    for k in range(N_CHIPS):
        parts = []
        for n in SMALL_SHARDED:
            ax = SHARD_AXIS[n]
            size = a[n].shape[ax]
            parts.append(lax.slice_in_dim(grads[n], k * size, (k + 1) * size, axis=ax))
        slabs.append(pack(parts, BF16))
    srcs += [jnp.stack(slabs, axis=0), pack([grads[n] for n in REPLICATED], F32)]
    modes += ["slab", "bcast"]
    got = grad_exchange(srcs, modes, "exchange_grads")
    outs = {}
    kinds = ("grad", "delta", "new_m", "new_v")
    for n, g8 in zip(BIG, got):
        shp = a[n].shape
        rows = shp[0] * shp[1]
        flat = lambda t: t.reshape(rows, shp[2])
        res = _adamw_call(g8.reshape(2 * N_CHIPS, rows, shp[2]), flat(a[n]), flat(a["m_" + n]), flat(a["v_" + n]), "adamw_" + n)
        for kind, val in zip(kinds, res):
            outs[(kind, n)] = val.reshape(shp)
    for tag, names, g8 in (("small", SMALL_SHARDED, got[len(BIG)]), ("rep", REPLICATED, got[len(BIG) + 1])):
        shapes = [a[n].shape for n in names]
        packed = [pack([a[pre + n] for n in names], F32) for pre in ("", "m_", "v_")]
        res = _adamw_call(g8, *packed, "adamw_" + tag)
        for kind, buf in zip(kinds, res):
            for n, val in zip(names, unpack(buf, shapes)):
                outs[(kind, n)] = val
    return outs


INPUT_NAMES = ("x",) + WEIGHT_NAMES + ("loss_target",) + tuple("m_" + n for n in WEIGHT_NAMES) + tuple("v_" + n for n in WEIGHT_NAMES)


def kernel(x, meta_tokens, norm_mix_g, w_in, ssd_conv_w, ssd_conv_b, ssd_dt_bias, ssd_a_log, ssd_d, ssd_norm_g, sb_norm_g, mla_q_norm_g, mla_kv_norm_g, mla_w_uq, mla_w_ukv, mla_norm_g, w_out, norm_ffn_g, ffn_w_up, ffn_conv_w, ffn_conv_b, ffn_w_down, final_norm_g, loss_target, m_meta_tokens, m_norm_mix_g, m_w_in, m_ssd_conv_w, m_ssd_conv_b, m_ssd_dt_bias, m_ssd_a_log, m_ssd_d, m_ssd_norm_g, m_sb_norm_g, m_mla_q_norm_g, m_mla_kv_norm_g, m_mla_w_uq, m_mla_w_ukv, m_mla_norm_g, m_w_out, m_norm_ffn_g, m_ffn_w_up, m_ffn_conv_w, m_ffn_conv_b, m_ffn_w_down, m_final_norm_g, v_meta_tokens, v_norm_mix_g, v_w_in, v_ssd_conv_w, v_ssd_conv_b, v_ssd_dt_bias, v_ssd_a_log, v_ssd_d, v_ssd_norm_g, v_sb_norm_g, v_mla_q_norm_g, v_mla_kv_norm_g, v_mla_w_uq, v_mla_w_ukv, v_mla_norm_g, v_w_out, v_norm_ffn_g, v_ffn_w_up, v_ffn_conv_w, v_ffn_conv_b, v_ffn_w_down, v_final_norm_g):
    args = (x, meta_tokens, norm_mix_g, w_in, ssd_conv_w, ssd_conv_b, ssd_dt_bias, ssd_a_log, ssd_d, ssd_norm_g, sb_norm_g, mla_q_norm_g, mla_kv_norm_g, mla_w_uq, mla_w_ukv, mla_norm_g, w_out, norm_ffn_g, ffn_w_up, ffn_conv_w, ffn_conv_b, ffn_w_down, final_norm_g, loss_target, m_meta_tokens, m_norm_mix_g, m_w_in, m_ssd_conv_w, m_ssd_conv_b, m_ssd_dt_bias, m_ssd_a_log, m_ssd_d, m_ssd_norm_g, m_sb_norm_g, m_mla_q_norm_g, m_mla_kv_norm_g, m_mla_w_uq, m_mla_w_ukv, m_mla_norm_g, m_w_out, m_norm_ffn_g, m_ffn_w_up, m_ffn_conv_w, m_ffn_conv_b, m_ffn_w_down, m_final_norm_g, v_meta_tokens, v_norm_mix_g, v_w_in, v_ssd_conv_w, v_ssd_conv_b, v_ssd_dt_bias, v_ssd_a_log, v_ssd_d, v_ssd_norm_g, v_sb_norm_g, v_mla_q_norm_g, v_mla_kv_norm_g, v_mla_w_uq, v_mla_w_ukv, v_mla_norm_g, v_w_out, v_norm_ffn_g, v_ffn_w_up, v_ffn_conv_w, v_ffn_conv_b, v_ffn_w_down, v_final_norm_g)
    a = dict(zip(INPUT_NAMES, args, strict=True))
    full = gather_weights(a)
    loss, grad_x, grads = local_step(a["x"][0], a["loss_target"][0], full)
    loss = lax.psum(loss, ("x", "y", "c"))
    outs = reduce_and_update(a, grads)
    result = [loss, grad_x[None]]
    for kind in ("grad", "delta", "new_m", "new_v"):
        result += [outs[(kind, n)] for n in WEIGHT_NAMES]
    return tuple(result)
```

```python
import functools
import math

import numpy as np
import jax
import jax.numpy as jnp
from jax import lax
from jax.experimental import pallas as pl
from jax.experimental.pallas import tpu as pltpu

F32 = jnp.float32
BF16 = jnp.bfloat16
HIGHEST = lax.Precision.HIGHEST
MESH_ID = pl.DeviceIdType.MESH

D_MODEL = 1024
DEPTH = 2
N_META = 16
EPS = 1e-6
SSD_HEADS = 8
SSD_WIDTH = 512
SSD_XBC = 1024
SSD_CONV = 4
SB_WIDTH = 256
SB_SCALE = 64 ** -0.5
MLA_Q_RANK = 192
MLA_KV_RANK = 128
MLA_ROPE = 32
MLA_SCALE = 96 ** -0.5
ROPE_BASE = 10000.0
D_FF = 2816
FFN_CONV = 3
IN_COLS = 2664
N_CHIPS = 4

ADAM_LR = 0.001
ADAM_B1 = 0.9
ADAM_B2 = 0.999
ADAM_EPS = 1e-08
ADAM_WD = 0.01
ADAM_STEP = 10

LANES = 128
SUBLANES = 8
ROW_TILE = 256
KEY_UNROLL = 4
WIDE_TILE = 768
VMEM_LIMIT = 56 * 1024 * 1024
PACK_W = 1024

U_XBC, U_Z, U_QA, U_CKV, U_KR4, U_DT, U_MAIN = 0, 1024, 1536, 1792, 1920, 2048, 2304
NEG = -1e30


def _cp(*sem):
    return pltpu.CompilerParams(dimension_semantics=sem if sem else None, vmem_limit_bytes=VMEM_LIMIT)


def _pick(dim, pref):
    if dim <= pref:
        return dim
    best = None
    for t in range(LANES, pref + 1, LANES):
        if dim % t == 0:
            best = t
    assert best is not None, (dim, pref)
    return best


def _dot(a, b, dims="nn", precision=None):
    dn = {"nn": (((1,), (0,)), ((), ())), "nt": (((1,), (1,)), ((), ())), "tn": (((0,), (0,)), ((), ()))}[dims]
    return lax.dot_general(a, b, dn, preferred_element_type=F32, precision=precision)


def _softplus(x):
    return jnp.maximum(x, 0.0) + jnp.log1p(jnp.exp(-jnp.abs(x)))


def _silu(x):
    return x * jax.nn.sigmoid(x)


def _rms(x, g, n=None):
    n = x.shape[-1] if n is None else n
    ms = jnp.sum(x * x, axis=-1, keepdims=True) * (1.0 / n)
    return x * lax.rsqrt(ms + EPS) * g


def mm(a, b, dims, out_dtype, name, add=None, tm=None, tn=None, tk=None):
    if dims == "nn":
        (m, k), (k2, n) = a.shape, b.shape
    elif dims == "nt":
        (m, k), (n, k2) = a.shape, b.shape
    else:
        (k, m), (k2, n) = a.shape, b.shape
    assert k == k2, (a.shape, b.shape, dims)
    if dims == "tn":
        tm, tn, tk = _pick(m, tm or 1408), _pick(n, tn or 1408), _pick(k, tk or 1408)
    else:
        tm, tn, tk = _pick(m, tm or (1408 if k <= 2304 else 768)), _pick(n, tn or 1408), _pick(k, tk or 2816)
    nk = k // tk
    if dims == "tn":
        a_spec = pl.BlockSpec((tk, tm), lambda j, i, kk: (kk, i))
    else:
        a_spec = pl.BlockSpec((tm, tk), lambda j, i, kk: (i, kk))
    if dims == "nt":
        b_spec = pl.BlockSpec((tn, tk), lambda j, i, kk: (j, kk))
    else:
        b_spec = pl.BlockSpec((tk, tn), lambda j, i, kk: (kk, j))
    o_spec = pl.BlockSpec((tm, tn), lambda j, i, kk: (i, j))
    has_add = add is not None

    def body(*refs):
        a_ref, b_ref = refs[0], refs[1]
        add_ref = refs[2] if has_add else None
        o_ref = refs[3] if has_add else refs[2]
        part = _dot(a_ref[...].astype(BF16), b_ref[...].astype(BF16), dims)

        def finish(r):
            if has_add:
                r = r + add_ref[...].astype(F32)
            o_ref[...] = r.astype(o_ref.dtype)

        if nk == 1:
            finish(part)
            return
        acc_ref = refs[-1]
        kk = pl.program_id(2)

        @pl.when(kk == 0)
        def _():
            acc_ref[...] = part

        @pl.when(jnp.logical_and(kk > 0, kk < nk - 1))
        def _():
            acc_ref[...] += part

        @pl.when(kk == nk - 1)
        def _():
            finish(acc_ref[...] + part)

    in_specs = [a_spec, b_spec] + ([o_spec] if has_add else [])
    args = (a, b) + ((add,) if has_add else ())
    return pl.pallas_call(
        body, name=name, grid=(n // tn, m // tm, nk),
        in_specs=in_specs, out_specs=o_spec,
        out_shape=jax.ShapeDtypeStruct((m, n), out_dtype),
        scratch_shapes=[pltpu.VMEM((tm, tn), F32)] if nk > 1 else [],
        compiler_params=_cp("parallel", "parallel", "arbitrary"),
    )(*args)


def RI(arr, width=None, cidx=0, cv=False, rblk=0):
    return ("row" if rblk == 0 else ("row", rblk), arr, arr.shape[1] if width is None else width, cidx, cv)


def HP(arr, width=None, cidx=0, cv=False):
    return ("prev", arr, arr.shape[1] if width is None else width, cidx, cv)


def HN(arr, width=None, cidx=0, cv=False):
    return ("next", arr, arr.shape[1] if width is None else width, cidx, cv)


def PA(arr, width=None, cidx=0, cv=False):
    return ("par", arr, arr.shape[1] if width is None else width, cidx, cv)


def RO(ncols, dtype, width=None, cv=False):
    return ("row", ncols, dtype, ncols if width is None else width, cv)


def AO(nrows, ncols, width=None, cv=False):
    return ("acc", (nrows, ncols), F32, ncols if width is None else width, cv)


def rowwise(fn, ins, outs, name, rows, tm=ROW_TILE, ncol=1):
    tm = min(tm, rows)
    assert rows % tm == 0
    nrow = rows // tm
    hb = tm // SUBLANES
    last_hb = rows // SUBLANES - 1
    in_specs, args = [], []
    for kind, arr, width, cidx, cv in ins:
        def cmap(j, cidx=cidx, cv=cv):
            return cidx + j if cv else cidx
        if kind == "row":
            spec = pl.BlockSpec((tm, width), lambda j, i, cmap=cmap: (i, cmap(j)))
        elif isinstance(kind, tuple):
            spec = pl.BlockSpec((tm, width), lambda j, i, cmap=cmap, rblk=kind[1]: (i + rblk, cmap(j)))
        elif kind == "prev":
            spec = pl.BlockSpec((SUBLANES, width), lambda j, i, cmap=cmap: (jnp.maximum(i * hb - 1, 0), cmap(j)))
        elif kind == "next":
            spec = pl.BlockSpec((SUBLANES, width), lambda j, i, cmap=cmap: (jnp.minimum((i + 1) * hb, last_hb), cmap(j)))
        else:
            spec = pl.BlockSpec((arr.shape[0], width), lambda j, i, cmap=cmap: (0, cmap(j)))
        in_specs.append(spec)
        args.append(arr)
    out_specs, out_shapes, acc_cv = [], [], []
    for kind, shp, dtype, width, cv in outs:
        if kind == "row":
            out_specs.append(pl.BlockSpec((tm, width), lambda j, i, cv=cv: (i, j if cv else 0)))
            out_shapes.append(jax.ShapeDtypeStruct((rows, shp), dtype))
            acc_cv.append(None)
        else:
            out_specs.append(pl.BlockSpec((shp[0], width), lambda j, i, cv=cv: (0, j if cv else 0)))
            out_shapes.append(jax.ShapeDtypeStruct(shp, dtype))
            acc_cv.append(cv)
    n_in = len(ins)

    def body(*refs):
        j = pl.program_id(0)
        i = pl.program_id(1)
        vals = fn(i, nrow, *[r[...] for r in refs[:n_in]])
        if not isinstance(vals, (tuple, list)):
            vals = (vals,)
        for o_ref, v, cv in zip(refs[n_in:], vals, acc_cv):
            if cv is None:
                o_ref[...] = v.astype(o_ref.dtype)
            else:
                first = (i == 0) if cv else jnp.logical_and(i == 0, j == 0)

                @pl.when(first)
                def _(o_ref=o_ref, v=v):
                    o_ref[...] = v.astype(o_ref.dtype)

                @pl.when(jnp.logical_not(first))
                def _(o_ref=o_ref, v=v):
                    o_ref[...] += v.astype(o_ref.dtype)

    res = pl.pallas_call(
        body, name=name, grid=(ncol, nrow), in_specs=in_specs, out_specs=out_specs, out_shape=out_shapes,
        compiler_params=_cp("arbitrary", "arbitrary"),
    )(*args)
    return res


def _rows_iota(x):
    return lax.broadcasted_iota(jnp.int32, x.shape, 0)


def shift_down(x, halo, s):
    if s == 0:
        return x
    tm = x.shape[0]
    top = pltpu.roll(halo, s, 0)
    if tm > SUBLANES:
        top = jnp.concatenate([top, jnp.zeros((tm - SUBLANES, x.shape[1]), x.dtype)], axis=0)
    return jnp.where(_rows_iota(x) < s, top, pltpu.roll(x, s, 0))


def shift_up(x, halo, s):
    if s == 0:
        return x
    tm = x.shape[0]
    bot = pltpu.roll(halo, SUBLANES - s, 0)
    if tm > SUBLANES:
        bot = jnp.concatenate([jnp.zeros((tm - SUBLANES, x.shape[1]), x.dtype), bot], axis=0)
    return jnp.where(_rows_iota(x) >= tm - s, bot, pltpu.roll(x, tm - s, 0))


def conv_fwd(x, halo, w, i):
    kw = w.shape[0]
    halo = jnp.where(i == 0, 0.0, halo)
    out = None
    for k in range(kw):
        term = w[k:k + 1, :] * shift_down(x, halo, kw - 1 - k)
        out = term if out is None else out + term
    return out


def conv_bwd_data(dy, halo_next, w, i, n):
    kw = w.shape[0]
    halo_next = jnp.where(i == n - 1, 0.0, halo_next)
    out = None
    for k in range(kw):
        term = w[k:k + 1, :] * shift_up(dy, halo_next, kw - 1 - k)
        out = term if out is None else out + term
    return out


def conv_bwd_w(dy, x, halo, i, kw):
    halo = jnp.where(i == 0, 0.0, halo)
    rows = [jnp.sum(dy * shift_down(x, halo, kw - 1 - k), axis=0, keepdims=True) for k in range(kw)]
    return jnp.concatenate(rows, axis=0)


def _lane(shape):
    return lax.broadcasted_iota(jnp.int32, shape, 1)


def rope_rot(x):
    lane = _lane(x.shape) % MLA_ROPE
    return jnp.where(lane < MLA_ROPE // 2, -pltpu.roll(x, LANES - MLA_ROPE // 2, 1), pltpu.roll(x, MLA_ROPE // 2, 1))


def rope_rot_t(g):
    lane = _lane(g.shape) % MLA_ROPE
    return jnp.where(lane < MLA_ROPE // 2, pltpu.roll(g, LANES - MLA_ROPE // 2, 1), -pltpu.roll(g, MLA_ROPE // 2, 1))


def _ssd_common(g, xs, dt_raw, bias, alog, q):
    lane = _lane((q, LANES))
    pre = dt_raw + bias
    dt = jnp.where(lane < SSD_HEADS, _softplus(pre), 0.0)
    a_row = -jnp.exp(alog)
    d_a = dt * a_row
    ri = lax.broadcasted_iota(jnp.int32, (q, q), 0)
    ci = lax.broadcasted_iota(jnp.int32, (q, q), 1)
    causal = ri >= ci
    acs = _dot(causal.astype(F32), d_a, "nn", HIGHEST)
    acs_t = acs.T
    subl = lax.broadcasted_iota(jnp.int32, (LANES, q), 0)
    heads = [4 * g + i for i in range(4)]
    lo = lane < 64

    def col(arr, h):
        return jnp.sum(jnp.where(lane == h, arr, 0.0), axis=1, keepdims=True)

    def lanes4(v):
        m = lo if v[0].shape[0] == q else lo[0:1, :]
        return jnp.concatenate([jnp.where(m, v[0], v[1]), jnp.where(m, v[2], v[3])], axis=1)

    cols = [col(acs, h) for h in heads]
    rows = [jnp.sum(jnp.where(subl == h, acs_t, 0.0), axis=0, keepdims=True) for h in heads]
    tots = [c_[q - 1:q, :] for c_ in cols]
    acs4 = lanes4(cols)
    dt4 = lanes4([col(dt, h) for h in heads])
    lms = [jnp.exp(jnp.where(causal, cols[i] - rows[i], NEG)) for i in range(4)]
    lane4 = _lane((q, 2 * LANES))
    hm = [jnp.logical_and(lane4 >= 64 * i, lane4 < 64 * (i + 1)) for i in range(4)]
    return dict(lane=lane, lo=lo, pre=pre, dt=dt, a_row=a_row, heads=heads, tots=tots, lms=lms, ri=ri, ci=ci, hm=hm,
                lanes4=lanes4, eacs=jnp.exp(acs4), dte=jnp.exp(lanes4(tots) - acs4), dt4=dt4, x=xs * dt4)


def _pick_lane(row_arr, h):
    return jnp.sum(jnp.where(_lane(row_arr.shape) == h, row_arr, 0.0), axis=1, keepdims=True)


def _etot(tots):
    sub = lax.broadcasted_iota(jnp.int32, (2 * LANES, LANES), 0)
    e = [jnp.exp(t) for t in tots]
    return jnp.where(sub < 64, e[0], jnp.where(sub < 128, e[1], jnp.where(sub < 192, e[2], e[3]))), e


def _half(arr, i, lo):
    slab = arr[:, LANES * (i // 2):LANES * (i // 2 + 1)]
    return jnp.where(lo, slab, 0.0) if i % 2 == 0 else jnp.where(lo, 0.0, slab)


def ssd_fwd(xbc_c, u_main, bias_row, alog_row, d_row, name):
    lp = xbc_c.shape[0]
    q = min(ROW_TILE, lp)
    nc = lp // q
    dt_blk = U_DT // LANES

    def body(xs_ref, b_ref, c_ref, dt_ref, bias_ref, alog_ref, d_ref, y_ref, hp_ref, h_scr):
        g = pl.program_id(0)
        c = pl.program_id(1)

        @pl.when(c == 0)
        def _():
            h_scr[...] = jnp.zeros_like(h_scr)

        xs = xs_ref[...]
        bb = b_ref[...].astype(BF16)
        cb_ = c_ref[...].astype(BF16)
        s = _ssd_common(g, xs, dt_ref[...], bias_ref[...], alog_ref[...], q)
        gmat = _dot(cb_, bb, "nt")
        ms = [(gmat * s["lms"][i]).astype(BF16) for i in range(4)]
        xjs = [_half(s["x"], i, s["lo"]).astype(BF16) for i in range(4)]
        ys = [_dot(ms[i], xjs[i]) for i in range(4)]
        hp = h_scr[...]
        hp_ref[...] = hp
        yoff = _dot(cb_, hp.astype(BF16), "nt") * s["eacs"]
        d4 = s["lanes4"]([_pick_lane(d_ref[...], h) for h in s["heads"]])
        y_ref[...] = jnp.concatenate([ys[0] + ys[1], ys[2] + ys[3]], axis=1) + yoff + d4 * xs
        etot, _ = _etot(s["tots"])
        h_scr[...] = hp * etot + _dot((s["x"] * s["dte"]).astype(BF16), bb, "tn")

    in_specs = [
        pl.BlockSpec((q, 2 * LANES), lambda g, c: (c, g)),
        pl.BlockSpec((q, LANES), lambda g, c: (c, 4 + g)),
        pl.BlockSpec((q, LANES), lambda g, c: (c, 6 + g)),
        pl.BlockSpec((q, LANES), lambda g, c: (c, dt_blk)),
        pl.BlockSpec((1, LANES), lambda g, c: (0, 0)),
        pl.BlockSpec((1, LANES), lambda g, c: (0, 0)),
        pl.BlockSpec((1, LANES), lambda g, c: (0, 0)),
    ]
    out_specs = [
        pl.BlockSpec((q, 2 * LANES), lambda g, c: (c, g)),
        pl.BlockSpec((None, None, 2 * LANES, LANES), lambda g, c: (g, c, 0, 0)),
    ]
    return pl.pallas_call(
        body, name=name, grid=(2, nc), in_specs=in_specs, out_specs=out_specs,
        out_shape=[jax.ShapeDtypeStruct((lp, SSD_WIDTH), F32), jax.ShapeDtypeStruct((2, nc, 2 * LANES, LANES), F32)],
        scratch_shapes=[pltpu.VMEM((2 * LANES, LANES), F32)],
        compiler_params=_cp("arbitrary", "arbitrary"),
    )(xbc_c, xbc_c, xbc_c, u_main, bias_row, alog_row, d_row)


def ssd_bwd(xbc_c, u_main, bias_row, alog_row, d_row, hprev, dy, name):
    lp = xbc_c.shape[0]
    q = min(ROW_TILE, lp)
    nc = lp // q
    dt_blk = U_DT // LANES

    def body(xs_ref, b_ref, c_ref, dt_ref, bias_ref, alog_ref, d_ref, hp_ref, dy_ref,
             dxs_ref, db_ref, dc_ref, ddt_ref, pg_ref, dh_scr):
        g = pl.program_id(0)
        cc = pl.program_id(1)

        @pl.when(cc == 0)
        def _():
            dh_scr[...] = jnp.zeros_like(dh_scr)
            pg_ref[...] = jnp.zeros_like(pg_ref)

        xs = xs_ref[...]
        bb = b_ref[...].astype(BF16)
        cb_ = c_ref[...].astype(BF16)
        s = _ssd_common(g, xs, dt_ref[...], bias_ref[...], alog_ref[...], q)
        lane, lo, x, hm, heads = s["lane"], s["lo"], s["x"], s["hm"], s["heads"]
        d_y = dy_ref[...]
        hp = hp_ref[...]
        hpb = hp.astype(BF16)
        dhn = dh_scr[...]
        dhnb = dhn.astype(BF16)
        xd = x * s["dte"]
        gmat = _dot(cb_, bb, "nt")
        m32s = [gmat * s["lms"][i] for i in range(4)]
        xjs = [_half(x, i, lo).astype(BF16) for i in range(4)]
        dyjs = [_half(d_y, i, lo).astype(BF16) for i in range(4)]
        dxparts = [_dot(m32s[i].astype(BF16), dyjs[i], "tn") for i in range(4)]
        dms = [_dot(dyjs[i], xjs[i], "nt") for i in range(4)]
        dg = dms[0] * s["lms"][0] + dms[1] * s["lms"][1] + dms[2] * s["lms"][2] + dms[3] * s["lms"][3]
        wms = [dms[i] * m32s[i] for i in range(4)]
        row_part = [jnp.sum(wm, axis=1, keepdims=True) for wm in wms]
        col_part = [jnp.sum(wm, axis=0, keepdims=True) for wm in wms]
        dgb = dg.astype(BF16)
        yoff = _dot(cb_, hpb, "nt") * s["eacs"]
        d_t = (d_y * s["eacs"]).astype(BF16)
        d_c = _dot(dgb, bb) + _dot(d_t, hpb)
        d_hp = _dot(d_t, cb_, "tn")
        dxd = _dot(bb, dhnb, "nt")
        d_b = _dot(dgb, cb_, "tn") + _dot(xd.astype(BF16), dhnb)
        d_x = jnp.concatenate([dxparts[0] + dxparts[1], dxparts[2] + dxparts[3]], axis=1) + dxd * s["dte"]
        r = dxd * xd
        a_terms = d_y * yoff - r

        def hsum(arr):
            return [jnp.sum(jnp.where(hm[i], arr, 0.0), axis=1, keepdims=True) for i in range(4)]

        dacs, rs = hsum(a_terms), hsum(r)
        hh = dhn * hp
        sub = lax.broadcasted_iota(jnp.int32, (2 * LANES, LANES), 0)
        hsums = [jnp.sum(jnp.where(jnp.logical_and(sub >= 64 * i, sub < 64 * (i + 1)), hh, 0.0), keepdims=True) for i in range(4)]
        last = lax.broadcasted_iota(jnp.int32, (q, 1), 0) == q - 1
        etot, etots = _etot(s["tots"])
        ddacs = jnp.zeros((q, LANES), F32)
        for i, h in enumerate(heads):
            dtot = jnp.sum(rs[i], keepdims=True) + hsums[i] * etots[i]
            ddacs = ddacs + jnp.where(lane == h, dacs[i] + row_part[i] + jnp.where(last, dtot, 0.0), 0.0)
        subl = lax.broadcasted_iota(jnp.int32, (LANES, q), 0)
        cols_t = jnp.zeros((LANES, q), F32)
        for i, h in enumerate(heads):
            cols_t = cols_t + jnp.where(subl == h, col_part[i], 0.0)
        ddacs = ddacs - cols_t.T
        anti = (s["ri"] <= s["ci"]).astype(F32)
        da = _dot(anti, ddacs, "nn", HIGHEST)
        ddt_own = hsum(d_x * xs)
        ddt = da * s["a_row"]
        for i, h in enumerate(heads):
            ddt = ddt + jnp.where(lane == h, ddt_own[i], 0.0)
        draw = ddt * jax.nn.sigmoid(s["pre"])
        ddt_ref[...] = draw
        d4 = s["lanes4"]([_pick_lane(d_ref[...], h) for h in heads])
        dxs_ref[...] = d4 * d_y + d_x * s["dt4"]
        db_ref[...] = d_b
        dc_ref[...] = d_c
        dds = hsum(d_y * xs)
        lane1 = lane[0:1, :]
        dd_row = jnp.zeros((1, LANES), F32)
        for i, h in enumerate(heads):
            dd_row = dd_row + jnp.where(lane1 == h, jnp.sum(dds[i], keepdims=True), 0.0)
        dbias_row = jnp.sum(draw, axis=0, keepdims=True)
        dalog_row = jnp.sum(da * s["dt"], axis=0, keepdims=True) * s["a_row"]
        sub8 = lax.broadcasted_iota(jnp.int32, (SUBLANES, LANES), 0)
        pg_ref[...] += (jnp.where(sub8 == 0, dbias_row, 0.0) + jnp.where(sub8 == 1, dalog_row, 0.0)
                        + jnp.where(sub8 == 2, dd_row, 0.0))
        dh_scr[...] = d_hp + etot * dhn

    rc = lambda c: nc - 1 - c
    in_specs = [
        pl.BlockSpec((q, 2 * LANES), lambda g, c: (rc(c), g)),
        pl.BlockSpec((q, LANES), lambda g, c: (rc(c), 4 + g)),
        pl.BlockSpec((q, LANES), lambda g, c: (rc(c), 6 + g)),
        pl.BlockSpec((q, LANES), lambda g, c: (rc(c), dt_blk)),
        pl.BlockSpec((1, LANES), lambda g, c: (0, 0)),
        pl.BlockSpec((1, LANES), lambda g, c: (0, 0)),
        pl.BlockSpec((1, LANES), lambda g, c: (0, 0)),
        pl.BlockSpec((None, None, 2 * LANES, LANES), lambda g, c: (g, rc(c), 0, 0)),
        pl.BlockSpec((q, 2 * LANES), lambda g, c: (rc(c), g)),
    ]
    out_specs = [
        pl.BlockSpec((q, 2 * LANES), lambda g, c: (rc(c), g)),
        pl.BlockSpec((q, LANES), lambda g, c: (rc(c), g)),
        pl.BlockSpec((q, LANES), lambda g, c: (rc(c), g)),
        pl.BlockSpec((q, LANES), lambda g, c: (rc(c), g)),
        pl.BlockSpec((SUBLANES, LANES), lambda g, c: (g, 0)),
    ]
    per_group = jax.ShapeDtypeStruct((lp, 2 * LANES), F32)
    return pl.pallas_call(
        body, name=name, grid=(2, nc), in_specs=in_specs, out_specs=out_specs,
        out_shape=[jax.ShapeDtypeStruct((lp, SSD_WIDTH), F32), per_group, per_group, per_group,
                   jax.ShapeDtypeStruct((2 * SUBLANES, LANES), F32)],
        scratch_shapes=[pltpu.VMEM((2 * LANES, LANES), F32)],
        compiler_params=_cp("arbitrary", "arbitrary"),
    )(xbc_c, xbc_c, xbc_c, u_main, bias_row, alog_row, d_row, hprev, dy)


def _sb_blocks(qs, ks, r_runs, masked, bq, after_scores=None):
    ri = lax.broadcasted_iota(jnp.int32, (bq, bq), 0)
    ci = lax.broadcasted_iota(jnp.int32, (bq, bq), 1)
    tri_after = (ri > ci).astype(BF16)
    zs = [_dot(qj, kj, "nt") for qj, kj in zip(qs, ks)]
    extra = after_scores() if after_scores is not None else None
    sigs, ubs = [], []
    for z in zs:
        zb = z.astype(BF16)
        u = -(jnp.maximum(zb, 0) + jnp.log(1 + jnp.exp(-jnp.abs(zb))))
        sigs.append(jnp.exp(zb + u))
        if masked:
            u = jnp.where(ci < ri, u, jnp.zeros_like(u))
        ubs.append(u)
    afters = [_dot(ub, tri_after) for ub in ubs]
    usums = [after[:, 0:1] + ub[:, 0:1].astype(F32) for after, ub in zip(afters, ubs)]
    ws = []
    for sig, after, r_run in zip(sigs, afters, r_runs):
        w = sig * jnp.exp(after + r_run).astype(BF16)
        if masked:
            w = jnp.where(ci < ri, w, jnp.zeros_like(w))
        ws.append(w)
    return usums, sigs, ws, extra


def _split_heads(x, lo):
    out = []
    zero = jnp.zeros((x.shape[0], LANES), x.dtype)
    for p in range(2):
        xp = x[:, LANES * p:LANES * (p + 1)]
        out += [jnp.where(lo, xp, zero), jnp.where(lo, zero, xp)]
    return out


def _per_head(x):
    return [x[:, :LANES], x[:, :LANES], x[:, LANES:], x[:, LANES:]]


def _resident(shape, col):
    return pl.BlockSpec(shape, lambda i: (0, col), pipeline_mode=pl.Buffered(1))


def sb_attn_fwd(qkv, name):
    lp = qkv.shape[0]
    bq = min(ROW_TILE, lp)
    nq = lp // bq
    assert nq <= 64

    def body(q_ref, k_ref, v_ref, o_ref, rs_ref):
        qi = pl.program_id(0)
        lane = _lane((bq, LANES))
        lo = lane < 64
        qs = _split_heads(q_ref[...], lo)

        def step(kb, carry, masked):
            off = pl.multiple_of(kb * bq, bq)
            ks = _per_head(k_ref[pl.ds(off, bq), :])
            vs = _per_head(v_ref[pl.ds(off, bq), :])
            heads, rss = carry
            r_runs = [heads[h][1] for h in range(4)]
            rss = list(rss)
            for h in range(4):
                rss[h // 2] = jnp.where(lane == 64 * (h % 2) + kb, r_runs[h], rss[h // 2])
            usums, _, ws, _ = _sb_blocks(qs, ks, r_runs, masked, bq)
            pvs = [_dot(ws[h], vs[h]) for h in range(4)]
            out = tuple((heads[h][0] + pvs[h], r_runs[h] + usums[h]) for h in range(4))
            return out, tuple(rss)

        zero = (jnp.zeros((bq, LANES), F32), jnp.zeros((bq, 1), F32))
        zr = jnp.zeros((bq, LANES), F32)
        carry = step(qi, ((zero,) * 4, (zr, zr)), True)
        def several(t, c):
            for r in range(KEY_UNROLL):
                c = step(qi - 1 - r - KEY_UNROLL * t, c, False)
            return c

        carry = lax.fori_loop(0, qi // KEY_UNROLL, several, carry)
        rem = qi % KEY_UNROLL
        heads, rss = lax.fori_loop(0, rem, lambda t, c: step(rem - 1 - t, c, False), carry)
        o_ref[...] = jnp.concatenate([jnp.where(lo, heads[0][0], heads[1][0]), jnp.where(lo, heads[2][0], heads[3][0])], axis=1)
        rs_ref[...] = jnp.concatenate(list(rss), axis=1)

    blk = pl.BlockSpec((bq, 2 * LANES), lambda i: (i, 0))
    return pl.pallas_call(
        body, name=name, grid=(nq,),
        in_specs=[blk, _resident((lp, 2 * LANES), 1), _resident((lp, 2 * LANES), 2)],
        out_specs=[blk, blk],
        out_shape=[jax.ShapeDtypeStruct((lp, SB_WIDTH), F32), jax.ShapeDtypeStruct((lp, SB_WIDTH), F32)],
        compiler_params=_cp("arbitrary"),
    )(qkv, qkv, qkv)


def sb_attn_bwd(qkv, rs, d_o, name):
    lp = qkv.shape[0]
    bq = min(ROW_TILE, lp)
    nq = lp // bq

    def body(q_ref, k_ref, v_ref, rs_ref, do_ref, dq_ref, dk_ref, dv_ref):
        qi = pl.program_id(0)

        @pl.when(qi == 0)
        def _():
            dk_ref[...] = jnp.zeros_like(dk_ref)
            dv_ref[...] = jnp.zeros_like(dv_ref)

        lane = _lane((bq, LANES))
        lo = lane < 64
        qs = _split_heads(q_ref[...], lo)
        dos = _split_heads(do_ref[...].astype(BF16), lo)
        rs_blk = rs_ref[...]
        ri = lax.broadcasted_iota(jnp.int32, (bq, bq), 0)
        ci = lax.broadcasted_iota(jnp.int32, (bq, bq), 1)
        tbefore = (ri < ci).astype(BF16)

        def step(kb, carry, masked):
            off = pl.multiple_of(kb * bq, bq)
            ks = _per_head(k_ref[pl.ds(off, bq), :])
            vs = _per_head(v_ref[pl.ds(off, bq), :])
            r_rights = [jnp.sum(jnp.where(lane == 64 * (h % 2) + kb, rs_blk[:, LANES * (h // 2):LANES * (h // 2 + 1)], 0.0),
                                axis=1, keepdims=True) for h in range(4)]
            _, sigs, wbs, dws = _sb_blocks(qs, ks, r_rights, masked, bq,
                                           after_scores=lambda: [_dot(dos[h], vs[h], "nt") for h in range(4)])
            gs = [wbs[h].astype(F32) * dws[h] for h in range(4)]
            gbs = [g.astype(BF16) for g in gs]
            gbefores = [_dot(gb, tbefore) for gb in gbs]
            dv_acc = [_dot(wbs[2 * p], dos[2 * p], "tn") + _dot(wbs[2 * p + 1], dos[2 * p + 1], "tn") for p in range(2)]
            dzbs = []
            for h in range(4):
                dz = gs[h] - sigs[h].astype(F32) * (gs[h] + gbefores[h] + carry[h][1])
                if masked:
                    dz = jnp.where(ci < ri, dz, 0.0)
                dzbs.append(dz.astype(BF16))
            dqs = [_dot(dzbs[h], ks[h]) for h in range(4)]
            dk_acc = [_dot(dzbs[2 * p], qs[2 * p], "tn") + _dot(dzbs[2 * p + 1], qs[2 * p + 1], "tn") for p in range(2)]
            dk_ref[pl.ds(off, bq), :] += jnp.concatenate(dk_acc, axis=1)
            dv_ref[pl.ds(off, bq), :] += jnp.concatenate(dv_acc, axis=1)
            return tuple((carry[h][0] + dqs[h], carry[h][1] + jnp.sum(gs[h], axis=1, keepdims=True)) for h in range(4))

        zero = (jnp.zeros((bq, LANES), F32), jnp.zeros((bq, 1), F32))
        def several(t, c):
            for r in range(KEY_UNROLL):
                c = step(KEY_UNROLL * t + r, c, False)
            return c

        carry = lax.fori_loop(0, qi // KEY_UNROLL, several, (zero,) * 4)
        carry = lax.fori_loop(qi - qi % KEY_UNROLL, qi, lambda t, c: step(t, c, False), carry)
        carry = step(qi, carry, True)
        dq_ref[...] = jnp.concatenate([jnp.where(lo, carry[0][0], carry[1][0]), jnp.where(lo, carry[2][0], carry[3][0])],
                                      axis=1).astype(dq_ref.dtype)

    blk = pl.BlockSpec((bq, 2 * LANES), lambda i: (i, 0))
    return pl.pallas_call(
        body, name=name, grid=(nq,),
        in_specs=[blk, _resident((lp, 2 * LANES), 1), _resident((lp, 2 * LANES), 2), blk, blk],
        out_specs=[blk, _resident((lp, 2 * LANES), 0), _resident((lp, 2 * LANES), 0)],
        out_shape=[jax.ShapeDtypeStruct((lp, SB_WIDTH), BF16), jax.ShapeDtypeStruct((lp, SB_WIDTH), F32),
                   jax.ShapeDtypeStruct((lp, SB_WIDTH), F32)],
        compiler_params=_cp("arbitrary"),
    )(qkv, qkv, qkv, rs, d_o)


def _mla_masks(bq):
    lane = _lane((bq, 2 * LANES))
    out = []
    for h in range(4):
        j = h % 2
        nope = jnp.logical_and(lane >= 64 * j, lane < 64 * (j + 1))
        rope = jnp.logical_and(lane >= LANES + MLA_ROPE * h, lane < LANES + MLA_ROPE * (h + 1))
        out.append(jnp.logical_or(nope, rope))
    return out


def _mla_split_q(q, masks):
    zero = jnp.zeros((q.shape[0], 2 * LANES), q.dtype)
    return [jnp.where(masks[h], q[:, 2 * LANES * (h // 2):2 * LANES * (h // 2 + 1)], zero) for h in range(4)]


def _mla_per_head_k(k):
    return [k[:, :2 * LANES], k[:, :2 * LANES], k[:, 2 * LANES:], k[:, 2 * LANES:]]


def mla_attn_fwd(qc, kc, v, name):
    lp = qc.shape[0]
    bq = min(ROW_TILE, lp)
    nq = lp // bq

    def body(q_ref, k_ref, v_ref, o_ref, lse_ref):
        qi = pl.program_id(0)
        qs = _mla_split_q(q_ref[...], _mla_masks(bq))
        lo = _lane((bq, LANES)) < 64
        ri = lax.broadcasted_iota(jnp.int32, (bq, bq), 0)
        ci = lax.broadcasted_iota(jnp.int32, (bq, bq), 1)

        def blocks(kbs, carry, masked):
            offs = [pl.multiple_of(kb * bq, bq) for kb in kbs]
            ks = [_mla_per_head_k(k_ref[pl.ds(o, bq), :]) for o in offs]
            ones = jnp.ones((bq, LANES), BF16)
            vs = []
            for o in offs:
                vp = _per_head(v_ref[pl.ds(o, bq), :])
                vs.append([jnp.where(lo, vp[h], ones) if h % 2 == 0 else jnp.where(lo, ones, vp[h]) for h in range(4)])
            ss = [[_dot(qs[h], ks[b][h], "nt") for h in range(4)] for b in range(len(kbs))]
            if masked:
                ss = [[jnp.where(ci <= ri, s, NEG) for s in row] for row in ss]
            prs, alphas, ms = [], [], []
            for h in range(4):
                top = ss[0][h]
                for b in range(1, len(kbs)):
                    top = jnp.maximum(top, ss[b][h])
                m_new = jnp.maximum(carry[h][1], jnp.max(top, axis=1, keepdims=True))
                alphas.append(jnp.exp(carry[h][1] - m_new))
                ms.append(m_new)
                prs.append([jnp.exp(ss[b][h] - m_new).astype(BF16) for b in range(len(kbs))])
            out = []
            for h in range(4):
                acc = carry[h][0] * alphas[h]
                for b in range(len(kbs)):
                    acc = acc + _dot(prs[h][b], vs[b][h])
                out.append((acc, ms[h]))
            return tuple(out)

        zero = (jnp.zeros((bq, LANES), F32), jnp.full((bq, 1), NEG, F32))
        carry = blocks([qi], (zero,) * 4, True)
        carry = lax.fori_loop(0, qi // KEY_UNROLL,
                              lambda t, c: blocks([qi - 1 - r - KEY_UNROLL * t for r in range(KEY_UNROLL)], c, False), carry)
        rem = qi % KEY_UNROLL
        carry = lax.fori_loop(0, rem, lambda t, c: blocks([rem - 1 - t], c, False), carry)
        outs, lses = [], []
        for h in range(4):
            acc, m = carry[h]
            l = acc[:, 64:65] if h % 2 == 0 else acc[:, 0:1]
            outs.append(acc / l)
            lses.append(m + jnp.log(l))
        o_ref[...] = jnp.concatenate([jnp.where(lo, outs[0], outs[1]), jnp.where(lo, outs[2], outs[3])], axis=1)
        lse_ref[...] = jnp.concatenate([jnp.where(lo, lses[0], lses[1]), jnp.where(lo, lses[2], lses[3])], axis=1)

    blk = pl.BlockSpec((bq, 2 * LANES), lambda i: (i, 0))
    return pl.pallas_call(
        body, name=name, grid=(nq,),
        in_specs=[pl.BlockSpec((bq, 4 * LANES), lambda i: (i, 0)), _resident((lp, 4 * LANES), 0), _resident((lp, 2 * LANES), 0)],
        out_specs=[blk, blk],
        out_shape=[jax.ShapeDtypeStruct((lp, 2 * LANES), F32), jax.ShapeDtypeStruct((lp, 2 * LANES), F32)],
        compiler_params=_cp("arbitrary"),
    )(qc, kc, v)


def mla_attn_bwd(qc, kc, v, o, lse, d_o, name):
    lp = qc.shape[0]
    bq = min(ROW_TILE, lp)
    nq = lp // bq

    def body(q_ref, k_ref, v_ref, o_ref, lse_ref, do_ref, dq_ref, dk_ref, dv_ref):
        qi = pl.program_id(0)

        @pl.when(qi == 0)
        def _():
            dk_ref[...] = jnp.zeros_like(dk_ref)
            dv_ref[...] = jnp.zeros_like(dv_ref)

        d_o = do_ref[...]
        masks = _mla_masks(bq)
        qs = _mla_split_q(q_ref[...], masks)
        lo = _lane((bq, LANES)) < 64
        dos = _split_heads(d_o.astype(BF16), lo)
        od = o_ref[...] * d_o
        lse_blk = lse_ref[...]
        delta, lses = [], []
        for h in range(4):
            odp = od[:, LANES * (h // 2):LANES * (h // 2 + 1)]
            delta.append(jnp.sum(jnp.where(lo, odp, 0.0) if h % 2 == 0 else jnp.where(lo, 0.0, odp), axis=1, keepdims=True))
            c0 = LANES * (h // 2) + 64 * (h % 2)
            lses.append(lse_blk[:, c0:c0 + 1])
        ri = lax.broadcasted_iota(jnp.int32, (bq, bq), 0)
        ci = lax.broadcasted_iota(jnp.int32, (bq, bq), 1)

        def step(kb, carry, masked):
            off = pl.multiple_of(kb * bq, bq)
            ks = _mla_per_head_k(k_ref[pl.ds(off, bq), :])
            vs = _per_head(v_ref[pl.ds(off, bq), :])
            ss = [_dot(qs[h], ks[h], "nt") for h in range(4)]
            dps = [_dot(dos[h], vs[h], "nt") for h in range(4)]
            prbs, dss = [], []
            for h in range(4):
                s = ss[h]
                if masked:
                    s = jnp.where(ci <= ri, s, NEG)
                pr = jnp.exp(s - lses[h])
                prbs.append(pr.astype(BF16))
                dss.append((pr * (dps[h] - delta[h])).astype(BF16))
            dv_acc = [_dot(prbs[2 * p], dos[2 * p], "tn") + _dot(prbs[2 * p + 1], dos[2 * p + 1], "tn") for p in range(2)]
            dqs = [_dot(dss[h], ks[h]) for h in range(4)]
            dk_acc = [_dot(dss[2 * p], qs[2 * p], "tn") + _dot(dss[2 * p + 1], qs[2 * p + 1], "tn") for p in range(2)]
            dk_ref[pl.ds(off, bq), :] += jnp.concatenate(dk_acc, axis=1)
            dv_ref[pl.ds(off, bq), :] += jnp.concatenate(dv_acc, axis=1)
            return tuple(carry[h] + dqs[h] for h in range(4))

        zero = jnp.zeros((bq, 2 * LANES), F32)
        carry = step(qi, (zero,) * 4, True)
        def several(t, c):
            for r in range(KEY_UNROLL):
                c = step(qi - 1 - r - KEY_UNROLL * t, c, False)
            return c

        carry = lax.fori_loop(0, qi // KEY_UNROLL, several, carry)
        rem = qi % KEY_UNROLL
        carry = lax.fori_loop(0, rem, lambda t, c: step(rem - 1 - t, c, False), carry)
        dq_ref[...] = jnp.concatenate([jnp.where(masks[0], carry[0], 0.0) + jnp.where(masks[1], carry[1], 0.0),
                                       jnp.where(masks[2], carry[2], 0.0) + jnp.where(masks[3], carry[3], 0.0)], axis=1)

    blk = pl.BlockSpec((bq, 2 * LANES), lambda i: (i, 0))
    wide = pl.BlockSpec((bq, 4 * LANES), lambda i: (i, 0))
    return pl.pallas_call(
        body, name=name, grid=(nq,),
        in_specs=[wide, _resident((lp, 4 * LANES), 0), _resident((lp, 2 * LANES), 0), blk, blk, blk],
        out_specs=[wide, _resident((lp, 4 * LANES), 0), _resident((lp, 2 * LANES), 0)],
        out_shape=[jax.ShapeDtypeStruct((lp, 4 * LANES), F32), jax.ShapeDtypeStruct((lp, 4 * LANES), F32),
                   jax.ShapeDtypeStruct((lp, 2 * LANES), F32)],
        compiler_params=_cp("arbitrary"),
    )(qc, kc, v, o, lse, d_o)


def _mix_out(y_pre, z, o_sb, o_mla, g_ssd, g_sb, g_mla):
    return jnp.concatenate([_rms(y_pre * _silu(z), g_ssd), _rms(o_sb, g_sb), _rms(o_mla, g_mla)], axis=1)


def _ffn_act(up_a, up_b, halo_a, halo_b, w_a, w_b, b_a, b_b, i):
    ca = conv_fwd(up_a, halo_a, w_a, i) + b_a
    cb_ = conv_fwd(up_b, halo_b, w_b, i) + b_b
    return ca, cb_


def layer_fwd(h, w, cs, sn, l):
    lp = h.shape[0]
    nm = f"l{l}_"
    hn = rowwise(lambda i, n, x, g: _rms(x, g), [RI(h), PA(w["norm_mix_g"])], [RO(D_MODEL, BF16)], nm + "rms_mix", lp, tm=WIDE_TILE)[0]
    u = mm(hn, w["w_main"], "nn", F32, nm + "in_main")
    qkv = mm(hn, w["w_sb"], "nn", BF16, nm + "in_sb")
    xbc_c = rowwise(lambda i, n, x, hl, cw, cb_: _silu(conv_fwd(x, hl, cw, i) + cb_),
                    [RI(u, SSD_XBC, 0), HP(u, SSD_XBC, 0), PA(w["ssd_conv_w"]), PA(w["ssd_conv_b"])],
                    [RO(SSD_XBC, F32)], nm + "ssd_conv", lp, tm=WIDE_TILE)[0]
    y_pre, hprev = ssd_fwd(xbc_c, u, w["dt_bias"], w["a_log"], w["d_skip"], nm + "ssd_fwd")
    o_sb, rs_sb = sb_attn_fwd(qkv, nm + "sb_fwd")
    qn, kvn = rowwise(lambda i, n, qa, ckv, gq, gkv: (_rms(qa, gq, MLA_Q_RANK), _rms(ckv, gkv)),
                      [RI(u, 256, U_QA // 256), RI(u, LANES, U_CKV // LANES), PA(w["q_norm_g"]), PA(w["kv_norm_g"])],
                      [RO(256, BF16), RO(LANES, BF16)], nm + "mla_rms", lp, tm=WIDE_TILE)
    qf = mm(qn, w["w_uq"], "nn", F32, nm + "mla_uq")
    kvf = mm(kvn, w["w_ukv"], "nn", F32, nm + "mla_ukv")

    def pack(i, n, qf_, kvf_, kr4, cos, sin):
        qf_ = qf_ * MLA_SCALE
        qr = qf_[:, 256:384]
        qr = qr * cos + rope_rot(qr) * sin
        kr = kr4 * cos + rope_rot(kr4) * sin
        qc = jnp.concatenate([qf_[:, 0:128], qr, qf_[:, 128:256], qr], axis=1)
        kc = jnp.concatenate([kvf_[:, 0:128], kr, kvf_[:, 128:256], kr], axis=1)
        return qc, kc, kvf_[:, 256:512]

    qc, kc, vv = rowwise(pack, [RI(qf), RI(kvf), RI(u, LANES, U_KR4 // LANES), RI(cs), RI(sn)],
                         [RO(512, BF16), RO(512, BF16), RO(256, BF16)], nm + "mla_pack", lp, tm=WIDE_TILE)
    o_mla, lse = mla_attn_fwd(qc, kc, vv, nm + "mla_fwd")
    cat = rowwise(lambda i, n, *a: _mix_out(*a),
                  [RI(y_pre), RI(u, SSD_WIDTH, U_Z // SSD_WIDTH), RI(o_sb), RI(o_mla),
                   PA(w["ssd_norm_g"]), PA(w["sb_norm_g"]), PA(w["mla_norm_g"])],
                  [RO(D_MODEL, BF16)], nm + "mix_out", lp, tm=WIDE_TILE)[0]
    h_mid = mm(cat, w["w_out"], "nn", F32, nm + "out_proj", add=h)
    hn2 = rowwise(lambda i, n, x, g: _rms(x, g), [RI(h_mid), PA(w["norm_ffn_g"])], [RO(D_MODEL, BF16)], nm + "rms_ffn", lp, tm=WIDE_TILE)[0]
    up_a = mm(hn2, w["w_up_a"], "nn", F32, nm + "up_a")
    up_b = mm(hn2, w["w_up_b"], "nn", F32, nm + "up_b")
    wc = 1408

    def act(i, n, ua, ub, ha, hb_, wa, wb, ba, bb_):
        ca, cb_ = _ffn_act(ua, ub, ha, hb_, wa, wb, ba, bb_, i)
        return _silu(ca) * cb_

    a_t = rowwise(act, [RI(up_a, wc, 0, True), RI(up_b, wc, 0, True), HP(up_a, wc, 0, True), HP(up_b, wc, 0, True),
                        PA(w["ffn_conv_w_a"], wc, 0, True), PA(w["ffn_conv_w_b"], wc, 0, True),
                        PA(w["ffn_conv_b_a"], wc, 0, True), PA(w["ffn_conv_b_b"], wc, 0, True)],
                  [RO(D_FF, BF16, wc, True)], nm + "ffn_act", lp, tm=WIDE_TILE, ncol=D_FF // wc)[0]
    h_out = mm(a_t, w["w_down"], "nn", F32, nm + "down", add=h_mid)
    saved = dict(h=h, hn=hn, u=u, qkv=qkv, xbc_c=xbc_c, y_pre=y_pre, hprev=hprev, o_sb=o_sb, rs_sb=rs_sb, qn=qn, kvn=kvn,
                 qc=qc, kc=kc, vv=vv, o_mla=o_mla, lse=lse, cat=cat, h_mid=h_mid, hn2=hn2, up_a=up_a, up_b=up_b, a_t=a_t)
    return h_out, saved


def layer_bwd(dh_out, w, s, cs, sn, l):
    lp = dh_out.shape[0]
    nm = f"l{l}b_"
    g = {}
    wc = 1408
    ncolf = D_FF // wc
    g["w_down"] = mm(s["a_t"], dh_out, "tn", BF16, nm + "dw_down")
    d_act = mm(dh_out, w["w_down"], "nt", F32, nm + "d_act")

    def act_bwd(i, n, ua, ub, ha, hb_, wa, wb, ba, bb_, da_):
        ca, cb_ = _ffn_act(ua, ub, ha, hb_, wa, wb, ba, bb_, i)
        sg = jax.nn.sigmoid(ca)
        dca = da_ * cb_ * (sg * (1.0 + ca * (1.0 - sg)))
        dcb = da_ * (ca * sg)
        return (dca, dcb, conv_bwd_w(dca, ua, ha, i, FFN_CONV), conv_bwd_w(dcb, ub, hb_, i, FFN_CONV),
                jnp.sum(dca, axis=0, keepdims=True), jnp.sum(dcb, axis=0, keepdims=True))

    dca, dcb, g["ffn_conv_w_a"], g["ffn_conv_w_b"], g["ffn_conv_b_a"], g["ffn_conv_b_b"] = rowwise(
        act_bwd, [RI(s["up_a"], wc, 0, True), RI(s["up_b"], wc, 0, True), HP(s["up_a"], wc, 0, True),
                  HP(s["up_b"], wc, 0, True), PA(w["ffn_conv_w_a"], wc, 0, True), PA(w["ffn_conv_w_b"], wc, 0, True),
                  PA(w["ffn_conv_b_a"], wc, 0, True), PA(w["ffn_conv_b_b"], wc, 0, True), RI(d_act, wc, 0, True)],
        [RO(D_FF, F32, wc, True), RO(D_FF, F32, wc, True), AO(FFN_CONV, D_FF, wc, True), AO(FFN_CONV, D_FF, wc, True),
         AO(1, D_FF, wc, True), AO(1, D_FF, wc, True)], nm + "ffn_act_bwd", lp, tm=WIDE_TILE // 2, ncol=ncolf)

    def conv_t(i, n, da_, db_, ha, hb_, wa, wb):
        return conv_bwd_data(da_, ha, wa, i, n), conv_bwd_data(db_, hb_, wb, i, n)

    dup_a, dup_b = rowwise(conv_t, [RI(dca, wc, 0, True), RI(dcb, wc, 0, True), HN(dca, wc, 0, True), HN(dcb, wc, 0, True),
                                    PA(w["ffn_conv_w_a"], wc, 0, True), PA(w["ffn_conv_w_b"], wc, 0, True)],
                           [RO(D_FF, BF16, wc, True), RO(D_FF, BF16, wc, True)], nm + "ffn_conv_t", lp, tm=WIDE_TILE, ncol=ncolf)
    g["w_up_a"] = mm(s["hn2"], dup_a, "tn", BF16, nm + "dw_up_a")
    g["w_up_b"] = mm(s["hn2"], dup_b, "tn", BF16, nm + "dw_up_b")
    dhn2 = mm(dup_a, w["w_up_a"], "nt", F32, nm + "dhn2_a")
    dhn2 = mm(dup_b, w["w_up_b"], "nt", F32, nm + "dhn2_b", add=dhn2)

    def rms_bwd(i, n, x, gg, dy, dres):
        _, vjp = jax.vjp(_rms, x, gg)
        dx, dg = vjp(dy)
        return dres + dx, dg

    dh_mid, g["norm_ffn_g"] = rowwise(rms_bwd, [RI(s["h_mid"]), PA(w["norm_ffn_g"]), RI(dhn2), RI(dh_out)],
                                      [RO(D_MODEL, F32), AO(1, D_MODEL)], nm + "rms_ffn_bwd", lp, tm=WIDE_TILE)
    g["w_out"] = mm(s["cat"], dh_mid, "tn", BF16, nm + "dw_out")
    d_cat = mm(dh_mid, w["w_out"], "nt", F32, nm + "d_cat")
    u = s["u"]

    def mix_bwd(i, n, y_pre, z, o_sb, o_mla, g1, g2, g3, dcat):
        _, vjp = jax.vjp(_mix_out, y_pre, z, o_sb, o_mla, g1, g2, g3)
        return vjp(dcat)

    dy_pre, dz, do_sb, do_mla, g["ssd_norm_g"], g["sb_norm_g"], g["mla_norm_g"] = rowwise(
        mix_bwd, [RI(s["y_pre"]), RI(u, SSD_WIDTH, U_Z // SSD_WIDTH), RI(s["o_sb"]), RI(s["o_mla"]),
                  PA(w["ssd_norm_g"]), PA(w["sb_norm_g"]), PA(w["mla_norm_g"]), RI(d_cat)],
        [RO(SSD_WIDTH, F32), RO(SSD_WIDTH, BF16), RO(SB_WIDTH, F32), RO(256, F32),
         AO(1, SSD_WIDTH), AO(1, SB_WIDTH), AO(1, 256)], nm + "mix_out_bwd", lp, tm=WIDE_TILE)
    dxs, dbp, dcp, ddtp, pg = ssd_bwd(s["xbc_c"], u, w["dt_bias"], w["a_log"], w["d_skip"], s["hprev"], dy_pre, nm + "ssd_bwd")
    pg = pg.reshape(2, SUBLANES, LANES).sum(axis=0)
    g["dt_bias"], g["a_log"], g["d_skip"] = pg[0:1], pg[1:2], pg[2:3]

    def conv4_bwd(i, n, x, hl, cw, cb_, dxs_, dbp_, dcp_, ddtp_):
        pre = conv_fwd(x, hl, cw, i) + cb_
        d_out = jnp.concatenate([dxs_, dbp_, dcp_], axis=1)
        sg = jax.nn.sigmoid(pre)
        d_pre = d_out * (sg * (1.0 + pre * (1.0 - sg)))
        ddt = ddtp_[:, 0:128] + ddtp_[:, 128:256]
        return d_pre, ddt, conv_bwd_w(d_pre, x, hl, i, SSD_CONV), jnp.sum(d_pre, axis=0, keepdims=True)

    d_pre, ddt, g["ssd_conv_w"], g["ssd_conv_b"] = rowwise(
        conv4_bwd, [RI(u, SSD_XBC, 0), HP(u, SSD_XBC, 0), PA(w["ssd_conv_w"]), PA(w["ssd_conv_b"]),
                    RI(dxs), RI(dbp), RI(dcp), RI(ddtp)],
        [RO(SSD_XBC, F32), RO(LANES, BF16), AO(SSD_CONV, SSD_XBC), AO(1, SSD_XBC)], nm + "ssd_conv_bwd", lp, tm=WIDE_TILE)
    d_xbc = rowwise(lambda i, n, d, hn_, cw: conv_bwd_data(d, hn_, cw, i, n),
                    [RI(d_pre), HN(d_pre), PA(w["ssd_conv_w"])], [RO(SSD_XBC, BF16)], nm + "ssd_conv_t", lp, tm=WIDE_TILE)[0]
    dq_sb, dk_sb, dv_sb = sb_attn_bwd(s["qkv"], s["rs_sb"], do_sb, nm + "sb_bwd")
    dqkv = jnp.concatenate([dq_sb, dk_sb.astype(BF16), dv_sb.astype(BF16)], axis=1)
    dqc, dkc, dvv = mla_attn_bwd(s["qc"], s["kc"], s["vv"], s["o_mla"], s["lse"], do_mla, nm + "mla_bwd")

    def unpack(i, n, dqc_, dkc_, dvv_, cos, sin):
        dqr = dqc_[:, 128:256] + dqc_[:, 384:512]
        dqr = dqr * cos + rope_rot_t(dqr * sin)
        dkr = dkc_[:, 128:256] + dkc_[:, 384:512]
        dkr = dkr * cos + rope_rot_t(dkr * sin)
        dq = jnp.concatenate([dqc_[:, 0:128], dqc_[:, 256:384], dqr], axis=1) * MLA_SCALE
        dkv = jnp.concatenate([dkc_[:, 0:128], dkc_[:, 256:384], dvv_], axis=1)
        return dq, dkv, dkr

    dq, dkv, dkr4 = rowwise(unpack, [RI(dqc), RI(dkc), RI(dvv), RI(cs), RI(sn)],
                            [RO(384, BF16), RO(512, BF16), RO(LANES, BF16)], nm + "mla_unpack", lp, tm=WIDE_TILE)
    g["w_uq"] = mm(s["qn"], dq, "tn", F32, nm + "dw_uq")
    g["w_ukv"] = mm(s["kvn"], dkv, "tn", F32, nm + "dw_ukv")
    dqn = mm(dq, w["w_uq"], "nt", F32, nm + "dqn")
    dkvn = mm(dkv, w["w_ukv"], "nt", F32, nm + "dkvn")

    def mla_rms_bwd(i, n, qa, ckv, gq, gkv, dqn_, dkvn_):
        _, vjp = jax.vjp(lambda a, b, c, d: (_rms(a, c, MLA_Q_RANK), _rms(b, d)), qa, ckv, gq, gkv)
        return vjp((dqn_, dkvn_))

    dqa, dckv, g["q_norm_g"], g["kv_norm_g"] = rowwise(
        mla_rms_bwd, [RI(u, 256, U_QA // 256), RI(u, LANES, U_CKV // LANES), PA(w["q_norm_g"]), PA(w["kv_norm_g"]),
                      RI(dqn), RI(dkvn)],
        [RO(256, BF16), RO(LANES, BF16), AO(1, 256), AO(1, LANES)], nm + "mla_rms_bwd", lp, tm=WIDE_TILE)
    du = jnp.concatenate([d_xbc, dz, dqa, dckv, dkr4, ddt, jnp.zeros((lp, LANES), BF16)], axis=1)
    g["w_main"] = mm(s["hn"], du, "tn", F32, nm + "dw_main")
    g["w_sb"] = mm(s["hn"], dqkv, "tn", F32, nm + "dw_sb")
    dhn = mm(du, w["w_main"], "nt", F32, nm + "dhn_main")
    dhn = mm(dqkv, w["w_sb"], "nt", F32, nm + "dhn_sb", add=dhn)
    dh_in, g["norm_mix_g"] = rowwise(rms_bwd, [RI(s["h"]), PA(w["norm_mix_g"]), RI(dhn), RI(dh_mid)],
                                     [RO(D_MODEL, F32), AO(1, D_MODEL)], nm + "rms_mix_bwd", lp, tm=WIDE_TILE)
    return dh_in, g


_IN_CUTS = np.cumsum((512, 1024, 8, 256, 256, 256, 192, 128, 32))


def _pad_cols(a, n):
    return jnp.pad(a, ((0, 0), (0, n - a.shape[1])))


def prep_layer_weights(full, l):
    w_in = full["w_in"][l]
    c = _IN_CUTS
    z, xbc, dtr = w_in[:, :c[0]], w_in[:, c[0]:c[1]], w_in[:, c[1]:c[2]]
    q_sb, k_sb, v_sb = w_in[:, c[2]:c[3]], w_in[:, c[3]:c[4]], w_in[:, c[4]:c[5]]
    q_a, c_kv, k_r = w_in[:, c[5]:c[6]], w_in[:, c[6]:c[7]], w_in[:, c[7]:c[8]]
    w_main = jnp.concatenate([xbc, z, _pad_cols(q_a, 256), c_kv, k_r, k_r, k_r, k_r, _pad_cols(dtr, 256)], axis=1)
    assert w_main.shape[1] == U_MAIN
    row = lambda v, n=None: _pad_cols(v.reshape(1, -1).astype(F32), v.size if n is None else n)
    uq = full["mla_w_uq"][l].reshape(MLA_Q_RANK, 4, 96)
    w_uq = jnp.concatenate([uq[:, :, :64].reshape(MLA_Q_RANK, 256), uq[:, :, 64:].reshape(MLA_Q_RANK, 128)], axis=1)
    w_uq = jnp.pad(w_uq, ((0, 256 - MLA_Q_RANK), (0, 0)))
    ukv = full["mla_w_ukv"][l].reshape(MLA_KV_RANK, 4, 128)
    w_ukv = jnp.concatenate([ukv[:, :, :64].reshape(MLA_KV_RANK, 256), ukv[:, :, 64:].reshape(MLA_KV_RANK, 256)], axis=1)
    return dict(
        norm_mix_g=row(full["norm_mix_g"][l]), w_main=w_main, w_sb=jnp.concatenate([q_sb * SB_SCALE, k_sb, v_sb], axis=1),
        ssd_conv_w=full["ssd_conv_w"][l], ssd_conv_b=row(full["ssd_conv_b"][l]),
        dt_bias=row(full["ssd_dt_bias"][l], LANES), a_log=row(full["ssd_a_log"][l], LANES), d_skip=row(full["ssd_d"][l], LANES),
        ssd_norm_g=row(full["ssd_norm_g"][l]), sb_norm_g=row(full["sb_norm_g"][l]),
        q_norm_g=row(full["mla_q_norm_g"][l], 256), kv_norm_g=row(full["mla_kv_norm_g"][l]),
        w_uq=w_uq, w_ukv=w_ukv, mla_norm_g=row(full["mla_norm_g"][l]),
        w_out=full["w_out"][l], norm_ffn_g=row(full["norm_ffn_g"][l]),
        w_up_a=full["ffn_w_up"][l][:, :D_FF], w_up_b=full["ffn_w_up"][l][:, D_FF:],
        ffn_conv_w_a=full["ffn_conv_w"][l][:, :D_FF], ffn_conv_w_b=full["ffn_conv_w"][l][:, D_FF:],
        ffn_conv_b_a=row(full["ffn_conv_b"][l][:D_FF]), ffn_conv_b_b=row(full["ffn_conv_b"][l][D_FF:]),
        w_down=full["ffn_w_down"][l],
    )


def unprep_layer_grads(g):
    wm = g["w_main"]
    xbc, z = wm[:, U_XBC:U_XBC + 1024], wm[:, U_Z:U_Z + 512]
    q_a, c_kv = wm[:, U_QA:U_QA + MLA_Q_RANK], wm[:, U_CKV:U_CKV + 128]
    k_r = (wm[:, U_KR4:U_KR4 + 32] + wm[:, U_KR4 + 32:U_KR4 + 64] + wm[:, U_KR4 + 64:U_KR4 + 96] + wm[:, U_KR4 + 96:U_KR4 + 128])
    dtr = wm[:, U_DT:U_DT + SSD_HEADS]
    w_sb = g["w_sb"]
    w_in = jnp.concatenate([z, xbc, dtr, w_sb[:, :SB_WIDTH] * SB_SCALE, w_sb[:, SB_WIDTH:], q_a, c_kv, k_r], axis=1)
    guq = g["w_uq"][:MLA_Q_RANK]
    guq = jnp.concatenate([guq[:, :256].reshape(MLA_Q_RANK, 4, 64), guq[:, 256:].reshape(MLA_Q_RANK, 4, 32)], axis=2)
    gukv = g["w_ukv"]
    gukv = jnp.concatenate([gukv[:, :256].reshape(MLA_KV_RANK, 4, 64), gukv[:, 256:].reshape(MLA_KV_RANK, 4, 64)], axis=2)
    return dict(
        norm_mix_g=g["norm_mix_g"][0], w_in=w_in, ssd_conv_w=g["ssd_conv_w"], ssd_conv_b=g["ssd_conv_b"][0],
        ssd_dt_bias=g["dt_bias"][0, :SSD_HEADS], ssd_a_log=g["a_log"][0, :SSD_HEADS], ssd_d=g["d_skip"][0, :SSD_HEADS],
        ssd_norm_g=g["ssd_norm_g"][0], sb_norm_g=g["sb_norm_g"][0], mla_q_norm_g=g["q_norm_g"][0, :MLA_Q_RANK],
        mla_kv_norm_g=g["kv_norm_g"][0], mla_w_uq=guq.reshape(MLA_Q_RANK, 384), mla_w_ukv=gukv.reshape(MLA_KV_RANK, 512),
        mla_norm_g=g["mla_norm_g"][0], w_out=g["w_out"], norm_ffn_g=g["norm_ffn_g"][0],
        ffn_w_up=jnp.concatenate([g["w_up_a"], g["w_up_b"]], axis=1),
        ffn_conv_w=jnp.concatenate([g["ffn_conv_w_a"], g["ffn_conv_w_b"]], axis=1),
        ffn_conv_b=jnp.concatenate([g["ffn_conv_b_a"][0], g["ffn_conv_b_b"][0]], axis=0),
        ffn_w_down=g["w_down"],
    )


def rope_tables(lp):
    pos = jnp.arange(lp, dtype=F32)
    inv = 1.0 / (ROPE_BASE ** (jnp.arange(0, MLA_ROPE, 2, dtype=F32) / MLA_ROPE))
    ang = pos[:, None] * inv[None, :]
    ang = jnp.concatenate([ang, ang] * 4, axis=-1)
    return jnp.cos(ang), jnp.sin(ang)


def local_step(x_seq, target, full):
    seq = x_seq.shape[0]
    length = seq + N_META
    lp = -(-length // ROW_TILE) * ROW_TILE
    cs, sn = rope_tables(lp)
    h = jnp.concatenate([full["meta_tokens"].astype(F32), x_seq, jnp.zeros((lp - length, D_MODEL), F32)], axis=0)
    tgt = jnp.pad(target, ((N_META, lp - length), (0, 0)))
    ws = [prep_layer_weights(full, l) for l in range(DEPTH)]
    saved = []
    for l in range(DEPTH):
        h, s = layer_fwd(h, ws[l], cs, sn, l)
        saved.append(s)
    fg = full["final_norm_g"].reshape(1, D_MODEL).astype(F32)
    tm = min(WIDE_TILE, lp)

    def loss_fn(i, n, x, g, t):
        rows = _rows_iota(x) + i * tm
        valid = jnp.logical_and(rows >= N_META, rows < length)

        def f(x_, g_):
            err = jnp.where(valid, _rms(x_, g_) - t, 0.0)
            return 0.5 * jnp.sum(err * err) * (1.0 / D_MODEL)

        val, (dx, dg) = jax.value_and_grad(f, argnums=(0, 1))(x, g)
        return dx, jnp.full((1, LANES), val, F32), dg

    dh, loss_row, g_final = rowwise(loss_fn, [RI(h), PA(fg), RI(tgt)], [RO(D_MODEL, F32), AO(1, LANES), AO(1, D_MODEL)],
                                    "loss_head", lp, tm=WIDE_TILE)
    grads = {}
    per_layer = [None] * DEPTH
    for l in reversed(range(DEPTH)):
        dh, g = layer_bwd(dh, ws[l], saved[l], cs, sn, l)
        per_layer[l] = unprep_layer_grads(g)
    for k in per_layer[0]:
        grads[k] = jnp.stack([per_layer[l][k] for l in range(DEPTH)], axis=0)
    grads["final_norm_g"] = g_final[0]
    grads["meta_tokens"] = dh[:N_META]
    return loss_row[0, 0], dh[N_META:length], grads


_ANY = pl.BlockSpec(memory_space=pl.ANY)


def chip_exchange(srcs, modes, name):
    n = len(srcs)
    flips = ((1, 0), (0, 1), (1, 1))

    def body(*refs):
        ins, outs = refs[:n], refs[n:2 * n]
        send_sems, recv_sems, fwd_send_sems, fwd_recv_sems, loc_sems = refs[2 * n:]
        x, y, c = lax.axis_index("x"), lax.axis_index("y"), lax.axis_index("c")
        me = 2 * x + y
        waits, forwards = [], []
        for a in range(n):
            whole = modes[a] != "slab"
            cp = pltpu.make_async_copy(ins[a] if whole else ins[a].at[me], outs[a].at[me], loc_sems.at[a])
            cp.start()
            waits.append(cp.wait)
            half = ins[a].shape[0] // 2 if modes[a] == "bcast_split" else None
            for k, (fx, fy) in enumerate(flips):
                px = 1 - x if fx else x
                py = 1 - y if fy else y
                peer = 2 * px + py
                if half is None:
                    src = ins[a] if whole else ins[a].at[peer]
                    dst = outs[a].at[me]
                else:
                    src = ins[a].at[pl.ds(c * half, half)]
                    dst = outs[a].at[me, pl.ds(c * half, half)]
                rc = pltpu.make_async_remote_copy(src_ref=src, dst_ref=dst, send_sem=send_sems.at[a, k],
                                                  recv_sem=recv_sems.at[a, k], device_id=(px, py, c), device_id_type=MESH_ID)
                rc.start()
                if half is None:
                    waits.append(rc.wait)
                else:
                    waits.append(rc.wait_send)
                    landed = outs[a].at[peer, pl.ds(c * half, half)]
                    fw = pltpu.make_async_remote_copy(src_ref=landed, dst_ref=landed, send_sem=fwd_send_sems.at[a, k],
                                                      recv_sem=fwd_recv_sems.at[a, k], device_id=(x, y, 1 - c),
                                                      device_id_type=MESH_ID)
                    forwards.append((rc, fw))
        for rc, fw in forwards:
            rc.wait_recv()
            fw.start()
        for rc, fw in forwards:
            fw.wait()
        for w in waits:
            w()

    out_shape = [jax.ShapeDtypeStruct((N_CHIPS,) + (s.shape if m != "slab" else s.shape[1:]), s.dtype) for s, m in zip(srcs, modes)]
    return pl.pallas_call(
        body, name=name, in_specs=[_ANY] * n, out_specs=[_ANY] * n, out_shape=out_shape,
        scratch_shapes=[pltpu.SemaphoreType.DMA((n, 3)), pltpu.SemaphoreType.DMA((n, 3)), pltpu.SemaphoreType.DMA((n, 3)),
                        pltpu.SemaphoreType.DMA((n, 3)), pltpu.SemaphoreType.DMA((n,))],
    )(*srcs)


def _piece(ref, mode, k):
    if mode == "slab":
        return ref.at[k]
    if mode == "rows":
        rs = ref.shape[1] // N_CHIPS
        return ref.at[:, pl.ds(pl.multiple_of(k * rs, 16), rs), :]
    if mode == "cols":
        cs = ref.shape[2] // N_CHIPS
        return ref.at[:, :, pl.ds(pl.multiple_of(k * cs, LANES), cs)]
    return ref


def _piece_shape(shape, mode):
    if mode == "slab":
        return shape[1:]
    if mode == "rows":
        return (shape[0], shape[1] // N_CHIPS, shape[2])
    if mode == "cols":
        return (shape[0], shape[1], shape[2] // N_CHIPS)
    return shape


def grad_exchange(srcs, modes, name):
    n = len(srcs)
    flips = ((1, 0), (0, 1), (1, 1))

    def body(*refs):
        ins, outs = refs[:n], refs[n:2 * n]
        send_sems, recv_sems, fwd_send_sems, fwd_recv_sems, sib_send_sems, sib_recv_sems, loc_sems = refs[2 * n:]
        x, y, c = lax.axis_index("x"), lax.axis_index("y"), lax.axis_index("c")
        me = 2 * x + y
        sibling = (x, y, 1 - c)
        waits, forwards = [], []
        for a in range(n):
            mine = _piece(ins[a], modes[a], me)
            slot = outs[a].at[4 * c + me]
            cp = pltpu.make_async_copy(mine, slot, loc_sems.at[a])
            cp.start()
            sb = pltpu.make_async_remote_copy(src_ref=mine, dst_ref=slot, send_sem=sib_send_sems.at[a],
                                              recv_sem=sib_recv_sems.at[a], device_id=sibling, device_id_type=MESH_ID)
            sb.start()
            waits += [cp.wait, sb.wait]
            for k, (fx, fy) in enumerate(flips):
                px = 1 - x if fx else x
                py = 1 - y if fy else y
                peer = 2 * px + py
                rc = pltpu.make_async_remote_copy(src_ref=_piece(ins[a], modes[a], peer), dst_ref=slot,
                                                  send_sem=send_sems.at[a, k], recv_sem=recv_sems.at[a, k],
                                                  device_id=(px, py, c), device_id_type=MESH_ID)
                rc.start()
                landed = outs[a].at[4 * c + peer]
                fw = pltpu.make_async_remote_copy(src_ref=landed, dst_ref=landed, send_sem=fwd_send_sems.at[a, k],
                                                  recv_sem=fwd_recv_sems.at[a, k], device_id=sibling, device_id_type=MESH_ID)
                waits.append(rc.wait_send)
                forwards.append((rc, fw))
        for rc, fw in forwards:
            rc.wait_recv()
            fw.start()
        for rc, fw in forwards:
            fw.wait()
        for w in waits:
            w()

    out_shape = [jax.ShapeDtypeStruct((2 * N_CHIPS,) + tuple(_piece_shape(s.shape, m)), s.dtype) for s, m in zip(srcs, modes)]
    dma = pltpu.SemaphoreType.DMA
    return pl.pallas_call(
        body, name=name, in_specs=[_ANY] * n, out_specs=[_ANY] * n, out_shape=out_shape,
        scratch_shapes=[dma((n, 3)), dma((n, 3)), dma((n, 3)), dma((n, 3)), dma((n,)), dma((n,)), dma((n,))],
    )(*srcs)


WEIGHT_NAMES = ("meta_tokens", "norm_mix_g", "w_in", "ssd_conv_w", "ssd_conv_b", "ssd_dt_bias", "ssd_a_log", "ssd_d",
                "ssd_norm_g", "sb_norm_g", "mla_q_norm_g", "mla_kv_norm_g", "mla_w_uq", "mla_w_ukv", "mla_norm_g",
                "w_out", "norm_ffn_g", "ffn_w_up", "ffn_conv_w", "ffn_conv_b", "ffn_w_down", "final_norm_g")
SHARD_AXIS = {"meta_tokens": 1, "w_in": 2, "ssd_conv_w": 2, "mla_w_uq": 2, "mla_w_ukv": 2, "w_out": 1, "ffn_w_up": 2,
              "ffn_conv_w": 2, "ffn_w_down": 1}
SHARDED = tuple(n for n in WEIGHT_NAMES if n in SHARD_AXIS)
REPLICATED = tuple(n for n in WEIGHT_NAMES if n not in SHARD_AXIS)
GATHER_BF16 = ("w_in", "mla_w_uq", "mla_w_ukv", "w_out", "ffn_w_up", "ffn_w_down")
GATHER_F32 = ("meta_tokens", "ssd_conv_w", "ffn_conv_w")
PACK_ROWS = ROW_TILE


def pack(arrs, dtype):
    flat = jnp.concatenate([a.reshape(-1).astype(dtype) for a in arrs])
    per = PACK_ROWS * PACK_W
    total = -(-flat.size // per) * per
    return jnp.pad(flat, (0, total - flat.size)).reshape(total // PACK_W, PACK_W)


def unpack(buf, shapes):
    flat = buf.reshape(-1)
    out, off = [], 0
    for shp in shapes:
        size = int(np.prod(shp))
        out.append(flat[off:off + size].reshape(shp))
        off += size
    return out


def gather_weights(a):
    full = {n: a[n] for n in REPLICATED}
    bufs = [pack([a[n] for n in GATHER_BF16], BF16), pack([a[n] for n in GATHER_F32], F32)]
    got = chip_exchange(bufs, ("bcast_split", "bcast"), "gather_weights")
    for names, g in ((GATHER_BF16, got[0]), (GATHER_F32, got[1])):
        pieces = [unpack(g[k], [a[n].shape for n in names]) for k in range(N_CHIPS)]
        for idx, n in enumerate(names):
            full[n] = jnp.concatenate([pieces[k][idx] for k in range(N_CHIPS)], axis=SHARD_AXIS[n])
    return full


BIG = ("w_in", "w_out", "ffn_w_up", "ffn_w_down")
BIG_MODE = {"w_in": "slab", "w_out": "rows", "ffn_w_up": "cols", "ffn_w_down": "rows"}
SMALL_SHARDED = tuple(n for n in SHARDED if n not in BIG)
ADAM_TILE = 128


def _adamw(i, n, *vals):
    parts, (w, m, v) = vals[:2 * N_CHIPS], vals[2 * N_CHIPS:]
    g = parts[0].astype(F32)
    for p in parts[1:]:
        g = g + p.astype(F32)
    m = ADAM_B1 * m + (1.0 - ADAM_B1) * g
    v = ADAM_B2 * v + (1.0 - ADAM_B2) * jnp.square(g)
    m_hat = m / (1.0 - ADAM_B1 ** ADAM_STEP)
    v_hat = v / (1.0 - ADAM_B2 ** ADAM_STEP)
    delta = -ADAM_LR * (m_hat / (jnp.sqrt(v_hat) + ADAM_EPS) + ADAM_WD * w)
    return g, delta, m, v


def _adamw_call(got, w, m, v, name):
    rows, width = w.shape
    flat = got.reshape(2 * N_CHIPS * rows, width)
    blk = rows // ADAM_TILE
    ins = [RI(flat, rblk=k * blk) for k in range(2 * N_CHIPS)] + [RI(w), RI(m), RI(v)]
    return rowwise(_adamw, ins, [RO(width, F32)] * 4, name, rows, tm=ADAM_TILE)


def reduce_and_update(a, grads):
    srcs, modes = [], []
    for n in BIG:
        g = grads[n].astype(BF16)
        if n == "w_in":
            cs = a[n].shape[2]
            g = g.reshape(DEPTH, D_MODEL, N_CHIPS, cs).transpose(2, 0, 1, 3)
        srcs.append(g)
        modes.append(BIG_MODE[n])
    slabs = []
    for k in range(N_CHIPS):
        parts = []
        for n in SMALL_SHARDED:
            ax = SHARD_AXIS[n]
            size = a[n].shape[ax]
            parts.append(lax.slice_in_dim(grads[n], k * size, (k + 1) * size, axis=ax))
        slabs.append(pack(parts, BF16))
    srcs += [jnp.stack(slabs, axis=0), pack([grads[n] for n in REPLICATED], F32)]
    modes += ["slab", "bcast"]
    got = grad_exchange(srcs, modes, "exchange_grads")
    outs = {}
    kinds = ("grad", "delta", "new_m", "new_v")
    for n, g8 in zip(BIG, got):
        shp = a[n].shape
        rows = shp[0] * shp[1]
        flat = lambda t: t.reshape(rows, shp[2])
        res = _adamw_call(g8.reshape(2 * N_CHIPS, rows, shp[2]), flat(a[n]), flat(a["m_" + n]), flat(a["v_" + n]), "adamw_" + n)
        for kind, val in zip(kinds, res):
            outs[(kind, n)] = val.reshape(shp)
    for tag, names, g8 in (("small", SMALL_SHARDED, got[len(BIG)]), ("rep", REPLICATED, got[len(BIG) + 1])):
        shapes = [a[n].shape for n in names]
        packed = [pack([a[pre + n] for n in names], F32) for pre in ("", "m_", "v_")]
        res = _adamw_call(g8, *packed, "adamw_" + tag)
        for kind, buf in zip(kinds, res):
            for n, val in zip(names, unpack(buf, shapes)):
                outs[(kind, n)] = val
    return outs


INPUT_NAMES = ("x",) + WEIGHT_NAMES + ("loss_target",) + tuple("m_" + n for n in WEIGHT_NAMES) + tuple("v_" + n for n in WEIGHT_NAMES)


def kernel(x, meta_tokens, norm_mix_g, w_in, ssd_conv_w, ssd_conv_b, ssd_dt_bias, ssd_a_log, ssd_d, ssd_norm_g, sb_norm_g, mla_q_norm_g, mla_kv_norm_g, mla_w_uq, mla_w_ukv, mla_norm_g, w_out, norm_ffn_g, ffn_w_up, ffn_conv_w, ffn_conv_b, ffn_w_down, final_norm_g, loss_target, m_meta_tokens, m_norm_mix_g, m_w_in, m_ssd_conv_w, m_ssd_conv_b, m_ssd_dt_bias, m_ssd_a_log, m_ssd_d, m_ssd_norm_g, m_sb_norm_g, m_mla_q_norm_g, m_mla_kv_norm_g, m_mla_w_uq, m_mla_w_ukv, m_mla_norm_g, m_w_out, m_norm_ffn_g, m_ffn_w_up, m_ffn_conv_w, m_ffn_conv_b, m_ffn_w_down, m_final_norm_g, v_meta_tokens, v_norm_mix_g, v_w_in, v_ssd_conv_w, v_ssd_conv_b, v_ssd_dt_bias, v_ssd_a_log, v_ssd_d, v_ssd_norm_g, v_sb_norm_g, v_mla_q_norm_g, v_mla_kv_norm_g, v_mla_w_uq, v_mla_w_ukv, v_mla_norm_g, v_w_out, v_norm_ffn_g, v_ffn_w_up, v_ffn_conv_w, v_ffn_conv_b, v_ffn_w_down, v_final_norm_g):
    args = (x, meta_tokens, norm_mix_g, w_in, ssd_conv_w, ssd_conv_b, ssd_dt_bias, ssd_a_log, ssd_d, ssd_norm_g, sb_norm_g, mla_q_norm_g, mla_kv_norm_g, mla_w_uq, mla_w_ukv, mla_norm_g, w_out, norm_ffn_g, ffn_w_up, ffn_conv_w, ffn_conv_b, ffn_w_down, final_norm_g, loss_target, m_meta_tokens, m_norm_mix_g, m_w_in, m_ssd_conv_w, m_ssd_conv_b, m_ssd_dt_bias, m_ssd_a_log, m_ssd_d, m_ssd_norm_g, m_sb_norm_g, m_mla_q_norm_g, m_mla_kv_norm_g, m_mla_w_uq, m_mla_w_ukv, m_mla_norm_g, m_w_out, m_norm_ffn_g, m_ffn_w_up, m_ffn_conv_w, m_ffn_conv_b, m_ffn_w_down, m_final_norm_g, v_meta_tokens, v_norm_mix_g, v_w_in, v_ssd_conv_w, v_ssd_conv_b, v_ssd_dt_bias, v_ssd_a_log, v_ssd_d, v_ssd_norm_g, v_sb_norm_g, v_mla_q_norm_g, v_mla_kv_norm_g, v_mla_w_uq, v_mla_w_ukv, v_mla_norm_g, v_w_out, v_norm_ffn_g, v_ffn_w_up, v_ffn_conv_w, v_ffn_conv_b, v_ffn_w_down, v_final_norm_g)
    a = dict(zip(INPUT_NAMES, args, strict=True))
    full = gather_weights(a)
    loss, grad_x, grads = local_step(a["x"][0], a["loss_target"][0], full)
    loss = lax.psum(loss, ("x", "y", "c"))
    outs = reduce_and_update(a, grads)
    result = [loss, grad_x[None]]
    for kind in ("grad", "delta", "new_m", "new_v"):
        result += [outs[(kind, n)] for n in WEIGHT_NAMES]
    return tuple(result)
```

```python
import functools
import math

import numpy as np
import jax
import jax.numpy as jnp
from jax import lax
from jax.experimental import pallas as pl
from jax.experimental.pallas import tpu as pltpu

F32 = jnp.float32
BF16 = jnp.bfloat16
MESH_ID = pl.DeviceIdType.MESH

D_MODEL = 1024
DEPTH = 2
N_META = 16
EPS = 1e-6
SSD_HEADS = 8
SSD_WIDTH = 512
SSD_XBC = 1024
SSD_CONV = 4
SB_WIDTH = 256
SB_SCALE = 64 ** -0.5
MLA_Q_RANK = 192
MLA_KV_RANK = 128
MLA_ROPE = 32
MLA_SCALE = 96 ** -0.5
ROPE_BASE = 10000.0
D_FF = 2816
FFN_CONV = 3
IN_COLS = 2664
N_CHIPS = 4

ADAM_LR = 0.001
ADAM_B1 = 0.9
ADAM_B2 = 0.999
ADAM_EPS = 1e-08
ADAM_WD = 0.01
ADAM_STEP = 10

LANES = 128
SUBLANES = 8
ROW_TILE = 256
KEY_UNROLL = 4
WIDE_TILE = 768
VMEM_LIMIT = 56 * 1024 * 1024
PACK_W = 1024

U_XBC, U_Z, U_QA, U_CKV, U_KR4, U_DT, U_MAIN = 0, 1024, 1536, 1792, 1920, 2048, 2304
NEG = -1e30


def _cp(*sem):
    return pltpu.CompilerParams(dimension_semantics=sem if sem else None, vmem_limit_bytes=VMEM_LIMIT)


def _pick(dim, pref):
    if dim <= pref:
        return dim
    best = None
    for t in range(LANES, pref + 1, LANES):
        if dim % t == 0:
            best = t
    assert best is not None, (dim, pref)
    return best


def _dot(a, b, dims="nn", precision=None):
    dn = {"nn": (((1,), (0,)), ((), ())), "nt": (((1,), (1,)), ((), ())), "tn": (((0,), (0,)), ((), ()))}[dims]
    return lax.dot_general(a, b, dn, preferred_element_type=F32, precision=precision)


def _tri_dot(tri, x):
    hi = x.astype(BF16)
    r1 = x - hi.astype(F32)
    mid = r1.astype(BF16)
    lo = (r1 - mid.astype(F32)).astype(BF16)
    t = tri.astype(BF16)
    return _dot(t, hi) + _dot(t, mid) + _dot(t, lo)


def _softplus(x):
    return jnp.maximum(x, 0.0) + jnp.log1p(jnp.exp(-jnp.abs(x)))


def _silu(x):
    return x * jax.nn.sigmoid(x)


def _rms(x, g, n=None):
    n = x.shape[-1] if n is None else n
    ms = jnp.sum(x * x, axis=-1, keepdims=True) * (1.0 / n)
    return x * lax.rsqrt(ms + EPS) * g


def mm(a, b, dims, out_dtype, name, add=None, tm=None, tn=None, tk=None):
    if dims == "nn":
        (m, k), (k2, n) = a.shape, b.shape
    elif dims == "nt":
        (m, k), (n, k2) = a.shape, b.shape
    else:
        (k, m), (k2, n) = a.shape, b.shape
    assert k == k2, (a.shape, b.shape, dims)
    if dims == "tn":
        tm, tn, tk = _pick(m, tm or 1408), _pick(n, tn or 1408), _pick(k, tk or 1408)
    else:
        tm, tn, tk = _pick(m, tm or (1408 if k <= 2304 else 768)), _pick(n, tn or 1408), _pick(k, tk or 2816)
    nk = k // tk
    if dims == "tn":
        a_spec = pl.BlockSpec((tk, tm), lambda j, i, kk: (kk, i))
    else:
        a_spec = pl.BlockSpec((tm, tk), lambda j, i, kk: (i, kk))
    if dims == "nt":
        b_spec = pl.BlockSpec((tn, tk), lambda j, i, kk: (j, kk))
    else:
        b_spec = pl.BlockSpec((tk, tn), lambda j, i, kk: (kk, j))
    o_spec = pl.BlockSpec((tm, tn), lambda j, i, kk: (i, j))
    has_add = add is not None

    def body(*refs):
        a_ref, b_ref = refs[0], refs[1]
        add_ref = refs[2] if has_add else None
        o_ref = refs[3] if has_add else refs[2]
        part = _dot(a_ref[...].astype(BF16), b_ref[...].astype(BF16), dims)

        def finish(r):
            if has_add:
                r = r + add_ref[...].astype(F32)
            o_ref[...] = r.astype(o_ref.dtype)

        if nk == 1:
            finish(part)
            return
        acc_ref = refs[-1]
        kk = pl.program_id(2)

        @pl.when(kk == 0)
        def _():
            acc_ref[...] = part

        @pl.when(jnp.logical_and(kk > 0, kk < nk - 1))
        def _():
            acc_ref[...] += part

        @pl.when(kk == nk - 1)
        def _():
            finish(acc_ref[...] + part)

    in_specs = [a_spec, b_spec] + ([o_spec] if has_add else [])
    args = (a, b) + ((add,) if has_add else ())
    return pl.pallas_call(
        body, name=name, grid=(n // tn, m // tm, nk),
        in_specs=in_specs, out_specs=o_spec,
        out_shape=jax.ShapeDtypeStruct((m, n), out_dtype),
        scratch_shapes=[pltpu.VMEM((tm, tn), F32)] if nk > 1 else [],
        compiler_params=_cp("parallel", "parallel", "arbitrary"),
    )(*args)


def RI(arr, width=None, cidx=0, cv=False, rblk=0):
    return ("row" if rblk == 0 else ("row", rblk), arr, arr.shape[1] if width is None else width, cidx, cv)


def HP(arr, width=None, cidx=0, cv=False):
    return ("prev", arr, arr.shape[1] if width is None else width, cidx, cv)


def HN(arr, width=None, cidx=0, cv=False):
    return ("next", arr, arr.shape[1] if width is None else width, cidx, cv)


def PA(arr, width=None, cidx=0, cv=False):
    return ("par", arr, arr.shape[1] if width is None else width, cidx, cv)


def RO(ncols, dtype, width=None, cv=False):
    return ("row", ncols, dtype, ncols if width is None else width, cv)


def AO(nrows, ncols, width=None, cv=False):
    return ("acc", (nrows, ncols), F32, ncols if width is None else width, cv)


def rowwise(fn, ins, outs, name, rows, tm=ROW_TILE, ncol=1):
    tm = min(tm, rows)
    assert rows % tm == 0
    nrow = rows // tm
    hb = tm // SUBLANES
    last_hb = rows // SUBLANES - 1
    in_specs, args = [], []
    for kind, arr, width, cidx, cv in ins:
        def cmap(j, cidx=cidx, cv=cv):
            return cidx + j if cv else cidx
        if kind == "row":
            spec = pl.BlockSpec((tm, width), lambda j, i, cmap=cmap: (i, cmap(j)))
        elif isinstance(kind, tuple):
            spec = pl.BlockSpec((tm, width), lambda j, i, cmap=cmap, rblk=kind[1]: (i + rblk, cmap(j)))
        elif kind == "prev":
            spec = pl.BlockSpec((SUBLANES, width), lambda j, i, cmap=cmap: (jnp.maximum(i * hb - 1, 0), cmap(j)))
        elif kind == "next":
            spec = pl.BlockSpec((SUBLANES, width), lambda j, i, cmap=cmap: (jnp.minimum((i + 1) * hb, last_hb), cmap(j)))
        else:
            spec = pl.BlockSpec((arr.shape[0], width), lambda j, i, cmap=cmap: (0, cmap(j)))
        in_specs.append(spec)
        args.append(arr)
    out_specs, out_shapes, acc_cv = [], [], []
    for kind, shp, dtype, width, cv in outs:
        if kind == "row":
            out_specs.append(pl.BlockSpec((tm, width), lambda j, i, cv=cv: (i, j if cv else 0)))
            out_shapes.append(jax.ShapeDtypeStruct((rows, shp), dtype))
            acc_cv.append(None)
        else:
            out_specs.append(pl.BlockSpec((shp[0], width), lambda j, i, cv=cv: (0, j if cv else 0)))
            out_shapes.append(jax.ShapeDtypeStruct(shp, dtype))
            acc_cv.append(cv)
    n_in = len(ins)

    def body(*refs):
        j = pl.program_id(0)
        i = pl.program_id(1)
        vals = fn(i, nrow, *[r[...] for r in refs[:n_in]])
        if not isinstance(vals, (tuple, list)):
            vals = (vals,)
        for o_ref, v, cv in zip(refs[n_in:], vals, acc_cv):
            if cv is None:
                o_ref[...] = v.astype(o_ref.dtype)
            else:
                first = (i == 0) if cv else jnp.logical_and(i == 0, j == 0)

                @pl.when(first)
                def _(o_ref=o_ref, v=v):
                    o_ref[...] = v.astype(o_ref.dtype)

                @pl.when(jnp.logical_not(first))
                def _(o_ref=o_ref, v=v):
                    o_ref[...] += v.astype(o_ref.dtype)

    res = pl.pallas_call(
        body, name=name, grid=(ncol, nrow), in_specs=in_specs, out_specs=out_specs, out_shape=out_shapes,
        compiler_params=_cp("arbitrary", "arbitrary"),
    )(*args)
    return res


def _rows_iota(x):
    return lax.broadcasted_iota(jnp.int32, x.shape, 0)


def shift_down(x, halo, s):
    if s == 0:
        return x
    tm = x.shape[0]
    top = pltpu.roll(halo, s, 0)
    if tm > SUBLANES:
        top = jnp.concatenate([top, jnp.zeros((tm - SUBLANES, x.shape[1]), x.dtype)], axis=0)
    return jnp.where(_rows_iota(x) < s, top, pltpu.roll(x, s, 0))


def shift_up(x, halo, s):
    if s == 0:
        return x
    tm = x.shape[0]
    bot = pltpu.roll(halo, SUBLANES - s, 0)
    if tm > SUBLANES:
        bot = jnp.concatenate([jnp.zeros((tm - SUBLANES, x.shape[1]), x.dtype), bot], axis=0)
    return jnp.where(_rows_iota(x) >= tm - s, bot, pltpu.roll(x, tm - s, 0))


def conv_fwd(x, halo, w, i):
    kw = w.shape[0]
    halo = jnp.where(i == 0, 0.0, halo)
    out = None
    for k in range(kw):
        term = w[k:k + 1, :] * shift_down(x, halo, kw - 1 - k)
        out = term if out is None else out + term
    return out


def conv_bwd_data(dy, halo_next, w, i, n):
    kw = w.shape[0]
    halo_next = jnp.where(i == n - 1, 0.0, halo_next)
    out = None
    for k in range(kw):
        term = w[k:k + 1, :] * shift_up(dy, halo_next, kw - 1 - k)
        out = term if out is None else out + term
    return out


def conv_bwd_w(dy, x, halo, i, kw):
    halo = jnp.where(i == 0, 0.0, halo)
    rows = [jnp.sum(dy * shift_down(x, halo, kw - 1 - k), axis=0, keepdims=True) for k in range(kw)]
    return jnp.concatenate(rows, axis=0)


def _lane(shape):
    return lax.broadcasted_iota(jnp.int32, shape, 1)


def rope_rot(x):
    lane = _lane(x.shape) % MLA_ROPE
    return jnp.where(lane < MLA_ROPE // 2, -pltpu.roll(x, LANES - MLA_ROPE // 2, 1), pltpu.roll(x, MLA_ROPE // 2, 1))


def rope_rot_t(g):
    lane = _lane(g.shape) % MLA_ROPE
    return jnp.where(lane < MLA_ROPE // 2, pltpu.roll(g, LANES - MLA_ROPE // 2, 1), -pltpu.roll(g, MLA_ROPE // 2, 1))


def _ssd_common(g, xs, dt_raw, bias, alog, q):
    lane = _lane((q, LANES))
    pre = dt_raw + bias
    dt = jnp.where(lane < SSD_HEADS, _softplus(pre), 0.0)
    a_row = -jnp.exp(alog)
    d_a = dt * a_row
    ri = lax.broadcasted_iota(jnp.int32, (q, q), 0)
    ci = lax.broadcasted_iota(jnp.int32, (q, q), 1)
    causal = ri >= ci
    acs = _tri_dot(causal, d_a)
    acs_t = acs.T
    subl = lax.broadcasted_iota(jnp.int32, (LANES, q), 0)
    heads = [4 * g + i for i in range(4)]
    lo = lane < 64

    def col(arr, h):
        return jnp.sum(jnp.where(lane == h, arr, 0.0), axis=1, keepdims=True)

    def lanes4(v):
        m = lo if v[0].shape[0] == q else lo[0:1, :]
        return jnp.concatenate([jnp.where(m, v[0], v[1]), jnp.where(m, v[2], v[3])], axis=1)

    cols = [col(acs, h) for h in heads]
    rows = [jnp.sum(jnp.where(subl == h, acs_t, 0.0), axis=0, keepdims=True) for h in heads]
    tots = [c_[q - 1:q, :] for c_ in cols]
    acs4 = lanes4(cols)
    dt4 = lanes4([col(dt, h) for h in heads])
    lms = [jnp.exp(jnp.where(causal, cols[i] - rows[i], NEG)) for i in range(4)]
    lane4 = _lane((q, 2 * LANES))
    hm = [jnp.logical_and(lane4 >= 64 * i, lane4 < 64 * (i + 1)) for i in range(4)]
    return dict(lane=lane, lo=lo, pre=pre, dt=dt, a_row=a_row, heads=heads, tots=tots, lms=lms, ri=ri, ci=ci, hm=hm,
                lanes4=lanes4, eacs=jnp.exp(acs4), dte=jnp.exp(lanes4(tots) - acs4), dt4=dt4, x=xs * dt4)


def _pick_lane(row_arr, h):
    return jnp.sum(jnp.where(_lane(row_arr.shape) == h, row_arr, 0.0), axis=1, keepdims=True)


def _etot(tots):
    sub = lax.broadcasted_iota(jnp.int32, (2 * LANES, LANES), 0)
    e = [jnp.exp(t) for t in tots]
    return jnp.where(sub < 64, e[0], jnp.where(sub < 128, e[1], jnp.where(sub < 192, e[2], e[3]))), e


def _half(arr, i, lo):
    slab = arr[:, LANES * (i // 2):LANES * (i // 2 + 1)]
    return jnp.where(lo, slab, 0.0) if i % 2 == 0 else jnp.where(lo, 0.0, slab)


def ssd_fwd(xbc_c, u_main, bias_row, alog_row, d_row, name):
    lp = xbc_c.shape[0]
    q = min(ROW_TILE, lp)
    nc = lp // q
    dt_blk = U_DT // LANES

    def body(xs_ref, b_ref, c_ref, dt_ref, bias_ref, alog_ref, d_ref, y_ref, hp_ref, h_scr):
        g = pl.program_id(0)
        c = pl.program_id(1)

        @pl.when(c == 0)
        def _():
            h_scr[...] = jnp.zeros_like(h_scr)

        xs = xs_ref[...]
        bb = b_ref[...].astype(BF16)
        cb_ = c_ref[...].astype(BF16)
        s = _ssd_common(g, xs, dt_ref[...], bias_ref[...], alog_ref[...], q)
        gmat = _dot(cb_, bb, "nt")
        ms = [(gmat * s["lms"][i]).astype(BF16) for i in range(4)]
        xjs = [_half(s["x"], i, s["lo"]).astype(BF16) for i in range(4)]
        ys = [_dot(ms[i], xjs[i]) for i in range(4)]
        hp = h_scr[...]
        hp_ref[...] = hp
        yoff = _dot(cb_, hp.astype(BF16), "nt") * s["eacs"]
        d4 = s["lanes4"]([_pick_lane(d_ref[...], h) for h in s["heads"]])
        y_ref[...] = jnp.concatenate([ys[0] + ys[1], ys[2] + ys[3]], axis=1) + yoff + d4 * xs
        etot, _ = _etot(s["tots"])
        h_scr[...] = hp * etot + _dot((s["x"] * s["dte"]).astype(BF16), bb, "tn")

    in_specs = [
        pl.BlockSpec((q, 2 * LANES), lambda g, c: (c, g)),
        pl.BlockSpec((q, LANES), lambda g, c: (c, 4 + g)),
        pl.BlockSpec((q, LANES), lambda g, c: (c, 6 + g)),
        pl.BlockSpec((q, LANES), lambda g, c: (c, dt_blk)),
        pl.BlockSpec((1, LANES), lambda g, c: (0, 0)),
        pl.BlockSpec((1, LANES), lambda g, c: (0, 0)),
        pl.BlockSpec((1, LANES), lambda g, c: (0, 0)),
    ]
    out_specs = [
        pl.BlockSpec((q, 2 * LANES), lambda g, c: (c, g)),
        pl.BlockSpec((None, None, 2 * LANES, LANES), lambda g, c: (g, c, 0, 0)),
    ]
    return pl.pallas_call(
        body, name=name, grid=(2, nc), in_specs=in_specs, out_specs=out_specs,
        out_shape=[jax.ShapeDtypeStruct((lp, SSD_WIDTH), F32), jax.ShapeDtypeStruct((2, nc, 2 * LANES, LANES), F32)],
        scratch_shapes=[pltpu.VMEM((2 * LANES, LANES), F32)],
        compiler_params=_cp("arbitrary", "arbitrary"),
    )(xbc_c, xbc_c, xbc_c, u_main, bias_row, alog_row, d_row)


def ssd_bwd(xbc_c, u_main, bias_row, alog_row, d_row, hprev, dy, name):
    lp = xbc_c.shape[0]
    q = min(ROW_TILE, lp)
    nc = lp // q
    dt_blk = U_DT // LANES

    def body(xs_ref, b_ref, c_ref, dt_ref, bias_ref, alog_ref, d_ref, hp_ref, dy_ref,
             dxs_ref, db_ref, dc_ref, ddt_ref, pg_ref, dh_scr):
        g = pl.program_id(0)
        cc = pl.program_id(1)

        @pl.when(cc == 0)
        def _():
            dh_scr[...] = jnp.zeros_like(dh_scr)
            pg_ref[...] = jnp.zeros_like(pg_ref)

        xs = xs_ref[...]
        bb = b_ref[...].astype(BF16)
        cb_ = c_ref[...].astype(BF16)
        s = _ssd_common(g, xs, dt_ref[...], bias_ref[...], alog_ref[...], q)
        lane, lo, x, hm, heads = s["lane"], s["lo"], s["x"], s["hm"], s["heads"]
        d_y = dy_ref[...]
        hp = hp_ref[...]
        hpb = hp.astype(BF16)
        dhn = dh_scr[...]
        dhnb = dhn.astype(BF16)
        xd = x * s["dte"]
        gmat = _dot(cb_, bb, "nt")
        m32s = [gmat * s["lms"][i] for i in range(4)]
        xjs = [_half(x, i, lo).astype(BF16) for i in range(4)]
        dyjs = [_half(d_y, i, lo).astype(BF16) for i in range(4)]
        dxparts = [_dot(m32s[i].astype(BF16), dyjs[i], "tn") for i in range(4)]
        dms = [_dot(dyjs[i], xjs[i], "nt") for i in range(4)]
        dg = dms[0] * s["lms"][0] + dms[1] * s["lms"][1] + dms[2] * s["lms"][2] + dms[3] * s["lms"][3]
        wms = [dms[i] * m32s[i] for i in range(4)]
        row_part = [jnp.sum(wm, axis=1, keepdims=True) for wm in wms]
        col_part = [jnp.sum(wm, axis=0, keepdims=True) for wm in wms]
        dgb = dg.astype(BF16)
        yoff = _dot(cb_, hpb, "nt") * s["eacs"]
        d_t = (d_y * s["eacs"]).astype(BF16)
        d_c = _dot(dgb, bb) + _dot(d_t, hpb)
        d_hp = _dot(d_t, cb_, "tn")
        dxd = _dot(bb, dhnb, "nt")
        d_b = _dot(dgb, cb_, "tn") + _dot(xd.astype(BF16), dhnb)
        d_x = jnp.concatenate([dxparts[0] + dxparts[1], dxparts[2] + dxparts[3]], axis=1) + dxd * s["dte"]
        r = dxd * xd
        a_terms = d_y * yoff - r

        def hsum(arr):
            return [jnp.sum(jnp.where(hm[i], arr, 0.0), axis=1, keepdims=True) for i in range(4)]

        dacs, rs = hsum(a_terms), hsum(r)
        hh = dhn * hp
        sub = lax.broadcasted_iota(jnp.int32, (2 * LANES, LANES), 0)
        hsums = [jnp.sum(jnp.where(jnp.logical_and(sub >= 64 * i, sub < 64 * (i + 1)), hh, 0.0), keepdims=True) for i in range(4)]
        last = lax.broadcasted_iota(jnp.int32, (q, 1), 0) == q - 1
        etot, etots = _etot(s["tots"])
        ddacs = jnp.zeros((q, LANES), F32)
        for i, h in enumerate(heads):
            dtot = jnp.sum(rs[i], keepdims=True) + hsums[i] * etots[i]
            ddacs = ddacs + jnp.where(lane == h, dacs[i] + row_part[i] + jnp.where(last, dtot, 0.0), 0.0)
        subl = lax.broadcasted_iota(jnp.int32, (LANES, q), 0)
        cols_t = jnp.zeros((LANES, q), F32)
        for i, h in enumerate(heads):
            cols_t = cols_t + jnp.where(subl == h, col_part[i], 0.0)
        ddacs = ddacs - cols_t.T
        da = _tri_dot(s["ri"] <= s["ci"], ddacs)
        ddt_own = hsum(d_x * xs)
        ddt = da * s["a_row"]
        for i, h in enumerate(heads):
            ddt = ddt + jnp.where(lane == h, ddt_own[i], 0.0)
        draw = ddt * jax.nn.sigmoid(s["pre"])
        ddt_ref[...] = draw
        d4 = s["lanes4"]([_pick_lane(d_ref[...], h) for h in heads])
        dxs_ref[...] = d4 * d_y + d_x * s["dt4"]
        db_ref[...] = d_b
        dc_ref[...] = d_c
        dds = hsum(d_y * xs)
        lane1 = lane[0:1, :]
        dd_row = jnp.zeros((1, LANES), F32)
        for i, h in enumerate(heads):
            dd_row = dd_row + jnp.where(lane1 == h, jnp.sum(dds[i], keepdims=True), 0.0)
        dbias_row = jnp.sum(draw, axis=0, keepdims=True)
        dalog_row = jnp.sum(da * s["dt"], axis=0, keepdims=True) * s["a_row"]
        sub8 = lax.broadcasted_iota(jnp.int32, (SUBLANES, LANES), 0)
        pg_ref[...] += (jnp.where(sub8 == 0, dbias_row, 0.0) + jnp.where(sub8 == 1, dalog_row, 0.0)
                        + jnp.where(sub8 == 2, dd_row, 0.0))
        dh_scr[...] = d_hp + etot * dhn

    rc = lambda c: nc - 1 - c
    in_specs = [
        pl.BlockSpec((q, 2 * LANES), lambda g, c: (rc(c), g)),
        pl.BlockSpec((q, LANES), lambda g, c: (rc(c), 4 + g)),
        pl.BlockSpec((q, LANES), lambda g, c: (rc(c), 6 + g)),
        pl.BlockSpec((q, LANES), lambda g, c: (rc(c), dt_blk)),
        pl.BlockSpec((1, LANES), lambda g, c: (0, 0)),
        pl.BlockSpec((1, LANES), lambda g, c: (0, 0)),
        pl.BlockSpec((1, LANES), lambda g, c: (0, 0)),
        pl.BlockSpec((None, None, 2 * LANES, LANES), lambda g, c: (g, rc(c), 0, 0)),
        pl.BlockSpec((q, 2 * LANES), lambda g, c: (rc(c), g)),
    ]
    out_specs = [
        pl.BlockSpec((q, 2 * LANES), lambda g, c: (rc(c), g)),
        pl.BlockSpec((q, LANES), lambda g, c: (rc(c), g)),
        pl.BlockSpec((q, LANES), lambda g, c: (rc(c), g)),
        pl.BlockSpec((q, LANES), lambda g, c: (rc(c), g)),
        pl.BlockSpec((SUBLANES, LANES), lambda g, c: (g, 0)),
    ]
    per_group = jax.ShapeDtypeStruct((lp, 2 * LANES), F32)
    return pl.pallas_call(
        body, name=name, grid=(2, nc), in_specs=in_specs, out_specs=out_specs,
        out_shape=[jax.ShapeDtypeStruct((lp, SSD_WIDTH), F32), per_group, per_group, per_group,
                   jax.ShapeDtypeStruct((2 * SUBLANES, LANES), F32)],
        scratch_shapes=[pltpu.VMEM((2 * LANES, LANES), F32)],
        compiler_params=_cp("arbitrary", "arbitrary"),
    )(xbc_c, xbc_c, xbc_c, u_main, bias_row, alog_row, d_row, hprev, dy)


def _sb_blocks(qs, ks, r_runs, masked, bq, after_scores=None):
    ri = lax.broadcasted_iota(jnp.int32, (bq, bq), 0)
    ci = lax.broadcasted_iota(jnp.int32, (bq, bq), 1)
    tri_after = (ri > ci).astype(BF16)
    zs = [_dot(qj, kj, "nt") for qj, kj in zip(qs, ks)]
    extra = after_scores() if after_scores is not None else None
    sigs, ubs = [], []
    for z in zs:
        zb = z.astype(BF16)
        u = -(jnp.maximum(zb, 0) + jnp.log(1 + jnp.exp(-jnp.abs(zb))))
        sigs.append(jnp.exp(zb + u))
        if masked:
            u = jnp.where(ci < ri, u, jnp.zeros_like(u))
        ubs.append(u)
    afters = [_dot(ub, tri_after) for ub in ubs]
    usums = [after[:, 0:1] + ub[:, 0:1].astype(F32) for after, ub in zip(afters, ubs)]
    ws = []
    for sig, after, r_run in zip(sigs, afters, r_runs):
        w = sig * jnp.exp(after + r_run).astype(BF16)
        if masked:
            w = jnp.where(ci < ri, w, jnp.zeros_like(w))
        ws.append(w)
    return usums, sigs, ws, extra


def _split_heads(x, lo):
    out = []
    zero = jnp.zeros((x.shape[0], LANES), x.dtype)
    for p in range(2):
        xp = x[:, LANES * p:LANES * (p + 1)]
        out += [jnp.where(lo, xp, zero), jnp.where(lo, zero, xp)]
    return out


def _per_head(x):
    return [x[:, :LANES], x[:, :LANES], x[:, LANES:], x[:, LANES:]]


def _resident(shape, col):
    return pl.BlockSpec(shape, lambda i: (0, col), pipeline_mode=pl.Buffered(1))


def sb_attn_fwd(qkv, name):
    lp = qkv.shape[0]
    bq = min(ROW_TILE, lp)
    nq = lp // bq
    assert nq <= 64

    def body(q_ref, k_ref, v_ref, o_ref, rs_ref):
        qi = pl.program_id(0)
        lane = _lane((bq, LANES))
        lo = lane < 64
        qs = _split_heads(q_ref[...], lo)

        def step(kb, carry, masked):
            off = pl.multiple_of(kb * bq, bq)
            ks = _per_head(k_ref[pl.ds(off, bq), :])
            vs = _per_head(v_ref[pl.ds(off, bq), :])
            heads, rss = carry
            r_runs = [heads[h][1] for h in range(4)]
            rss = list(rss)
            for h in range(4):
                rss[h // 2] = jnp.where(lane == 64 * (h % 2) + kb, r_runs[h], rss[h // 2])
            usums, _, ws, _ = _sb_blocks(qs, ks, r_runs, masked, bq)
            pvs = [_dot(ws[h], vs[h]) for h in range(4)]
            out = tuple((heads[h][0] + pvs[h], r_runs[h] + usums[h]) for h in range(4))
            return out, tuple(rss)

        zero = (jnp.zeros((bq, LANES), F32), jnp.zeros((bq, 1), F32))
        zr = jnp.zeros((bq, LANES), F32)
        carry = step(qi, ((zero,) * 4, (zr, zr)), True)
        def several(t, c):
            for r in range(KEY_UNROLL):
                c = step(qi - 1 - r - KEY_UNROLL * t, c, False)
            return c

        carry = lax.fori_loop(0, qi // KEY_UNROLL, several, carry)
        rem = qi % KEY_UNROLL
        heads, rss = lax.fori_loop(0, rem, lambda t, c: step(rem - 1 - t, c, False), carry)
        o_ref[...] = jnp.concatenate([jnp.where(lo, heads[0][0], heads[1][0]), jnp.where(lo, heads[2][0], heads[3][0])], axis=1)
        rs_ref[...] = jnp.concatenate(list(rss), axis=1)

    blk = pl.BlockSpec((bq, 2 * LANES), lambda i: (i, 0))
    return pl.pallas_call(
        body, name=name, grid=(nq,),
        in_specs=[blk, _resident((lp, 2 * LANES), 1), _resident((lp, 2 * LANES), 2)],
        out_specs=[blk, blk],
        out_shape=[jax.ShapeDtypeStruct((lp, SB_WIDTH), F32), jax.ShapeDtypeStruct((lp, SB_WIDTH), F32)],
        compiler_params=_cp("arbitrary"),
    )(qkv, qkv, qkv)


def sb_attn_bwd(qkv, rs, d_o, name):
    lp = qkv.shape[0]
    bq = min(ROW_TILE, lp)
    nq = lp // bq

    def body(q_ref, k_ref, v_ref, rs_ref, do_ref, dq_ref, dk_ref, dv_ref):
        qi = pl.program_id(0)

        @pl.when(qi == 0)
        def _():
            dk_ref[...] = jnp.zeros_like(dk_ref)
            dv_ref[...] = jnp.zeros_like(dv_ref)

        lane = _lane((bq, LANES))
        lo = lane < 64
        qs = _split_heads(q_ref[...], lo)
        dos = _split_heads(do_ref[...].astype(BF16), lo)
        rs_blk = rs_ref[...]
        ri = lax.broadcasted_iota(jnp.int32, (bq, bq), 0)
        ci = lax.broadcasted_iota(jnp.int32, (bq, bq), 1)
        tbefore = (ri < ci).astype(BF16)

        def step(kb, carry, masked):
            off = pl.multiple_of(kb * bq, bq)
            ks = _per_head(k_ref[pl.ds(off, bq), :])
            vs = _per_head(v_ref[pl.ds(off, bq), :])
            r_rights = [jnp.sum(jnp.where(lane == 64 * (h % 2) + kb, rs_blk[:, LANES * (h // 2):LANES * (h // 2 + 1)], 0.0),
                                axis=1, keepdims=True) for h in range(4)]
            _, sigs, wbs, dws = _sb_blocks(qs, ks, r_rights, masked, bq,
                                           after_scores=lambda: [_dot(dos[h], vs[h], "nt") for h in range(4)])
            gs = [wbs[h].astype(F32) * dws[h] for h in range(4)]
            gbs = [g.astype(BF16) for g in gs]
            gbefores = [_dot(gb, tbefore) for gb in gbs]
            dv_acc = [_dot(wbs[2 * p], dos[2 * p], "tn") + _dot(wbs[2 * p + 1], dos[2 * p + 1], "tn") for p in range(2)]
            dzbs = []
            for h in range(4):
                dz = gs[h] - sigs[h].astype(F32) * (gs[h] + gbefores[h] + carry[h][1])
                if masked:
                    dz = jnp.where(ci < ri, dz, 0.0)
                dzbs.append(dz.astype(BF16))
            dqs = [_dot(dzbs[h], ks[h]) for h in range(4)]
            dk_acc = [_dot(dzbs[2 * p], qs[2 * p], "tn") + _dot(dzbs[2 * p + 1], qs[2 * p + 1], "tn") for p in range(2)]
            dk_ref[pl.ds(off, bq), :] += jnp.concatenate(dk_acc, axis=1)
            dv_ref[pl.ds(off, bq), :] += jnp.concatenate(dv_acc, axis=1)
            return tuple((carry[h][0] + dqs[h], carry[h][1] + jnp.sum(gs[h], axis=1, keepdims=True)) for h in range(4))

        zero = (jnp.zeros((bq, LANES), F32), jnp.zeros((bq, 1), F32))
        def several(t, c):
            for r in range(KEY_UNROLL):
                c = step(KEY_UNROLL * t + r, c, False)
            return c

        carry = lax.fori_loop(0, qi // KEY_UNROLL, several, (zero,) * 4)
        carry = lax.fori_loop(qi - qi % KEY_UNROLL, qi, lambda t, c: step(t, c, False), carry)
        carry = step(qi, carry, True)
        dq_ref[...] = jnp.concatenate([jnp.where(lo, carry[0][0], carry[1][0]), jnp.where(lo, carry[2][0], carry[3][0])],
                                      axis=1).astype(dq_ref.dtype)

    blk = pl.BlockSpec((bq, 2 * LANES), lambda i: (i, 0))
    return pl.pallas_call(
        body, name=name, grid=(nq,),
        in_specs=[blk, _resident((lp, 2 * LANES), 1), _resident((lp, 2 * LANES), 2), blk, blk],
        out_specs=[blk, _resident((lp, 2 * LANES), 0), _resident((lp, 2 * LANES), 0)],
        out_shape=[jax.ShapeDtypeStruct((lp, SB_WIDTH), BF16), jax.ShapeDtypeStruct((lp, SB_WIDTH), F32),
                   jax.ShapeDtypeStruct((lp, SB_WIDTH), F32)],
        compiler_params=_cp("arbitrary"),
    )(qkv, qkv, qkv, rs, d_o)


def _mla_masks(bq):
    lane = _lane((bq, 2 * LANES))
    out = []
    for h in range(4):
        j = h % 2
        nope = jnp.logical_and(lane >= 64 * j, lane < 64 * (j + 1))
        rope = jnp.logical_and(lane >= LANES + MLA_ROPE * h, lane < LANES + MLA_ROPE * (h + 1))
        out.append(jnp.logical_or(nope, rope))
    return out


def _mla_split_q(q, masks):
    zero = jnp.zeros((q.shape[0], 2 * LANES), q.dtype)
    return [jnp.where(masks[h], q[:, 2 * LANES * (h // 2):2 * LANES * (h // 2 + 1)], zero) for h in range(4)]


def _mla_per_head_k(k):
    return [k[:, :2 * LANES], k[:, :2 * LANES], k[:, 2 * LANES:], k[:, 2 * LANES:]]


def mla_attn_fwd(qc, kc, v, name):
    lp = qc.shape[0]
    bq = min(ROW_TILE, lp)
    nq = lp // bq

    def body(q_ref, k_ref, v_ref, o_ref, lse_ref):
        qi = pl.program_id(0)
        qs = _mla_split_q(q_ref[...], _mla_masks(bq))
        lo = _lane((bq, LANES)) < 64
        ri = lax.broadcasted_iota(jnp.int32, (bq, bq), 0)
        ci = lax.broadcasted_iota(jnp.int32, (bq, bq), 1)

        def blocks(kbs, carry, masked):
            offs = [pl.multiple_of(kb * bq, bq) for kb in kbs]
            ks = [_mla_per_head_k(k_ref[pl.ds(o, bq), :]) for o in offs]
            ones = jnp.ones((bq, LANES), BF16)
            vs = []
            for o in offs:
                vp = _per_head(v_ref[pl.ds(o, bq), :])
                vs.append([jnp.where(lo, vp[h], ones) if h % 2 == 0 else jnp.where(lo, ones, vp[h]) for h in range(4)])
            ss = [[_dot(qs[h], ks[b][h], "nt") for h in range(4)] for b in range(len(kbs))]
            if masked:
                ss = [[jnp.where(ci <= ri, s, NEG) for s in row] for row in ss]
            prs, alphas, ms = [], [], []
            for h in range(4):
                top = ss[0][h]
                for b in range(1, len(kbs)):
                    top = jnp.maximum(top, ss[b][h])
                m_new = jnp.maximum(carry[h][1], jnp.max(top, axis=1, keepdims=True))
                alphas.append(jnp.exp(carry[h][1] - m_new))
                ms.append(m_new)
                prs.append([jnp.exp(ss[b][h] - m_new).astype(BF16) for b in range(len(kbs))])
            out = []
            for h in range(4):
                acc = carry[h][0] * alphas[h]
                for b in range(len(kbs)):
                    acc = acc + _dot(prs[h][b], vs[b][h])
                out.append((acc, ms[h]))
            return tuple(out)

        zero = (jnp.zeros((bq, LANES), F32), jnp.full((bq, 1), NEG, F32))
        carry = blocks([qi], (zero,) * 4, True)
        carry = lax.fori_loop(0, qi // KEY_UNROLL,
                              lambda t, c: blocks([qi - 1 - r - KEY_UNROLL * t for r in range(KEY_UNROLL)], c, False), carry)
        rem = qi % KEY_UNROLL
        carry = lax.fori_loop(0, rem, lambda t, c: blocks([rem - 1 - t], c, False), carry)
        outs, lses = [], []
        for h in range(4):
            acc, m = carry[h]
            l = acc[:, 64:65] if h % 2 == 0 else acc[:, 0:1]
            outs.append(acc / l)
            lses.append(m + jnp.log(l))
        o_ref[...] = jnp.concatenate([jnp.where(lo, outs[0], outs[1]), jnp.where(lo, outs[2], outs[3])], axis=1)
        lse_ref[...] = jnp.concatenate([jnp.where(lo, lses[0], lses[1]), jnp.where(lo, lses[2], lses[3])], axis=1)

    blk = pl.BlockSpec((bq, 2 * LANES), lambda i: (i, 0))
    return pl.pallas_call(
        body, name=name, grid=(nq,),
        in_specs=[pl.BlockSpec((bq, 4 * LANES), lambda i: (i, 0)), _resident((lp, 4 * LANES), 0), _resident((lp, 2 * LANES), 0)],
        out_specs=[blk, blk],
        out_shape=[jax.ShapeDtypeStruct((lp, 2 * LANES), F32), jax.ShapeDtypeStruct((lp, 2 * LANES), F32)],
        compiler_params=_cp("arbitrary"),
    )(qc, kc, v)


def mla_attn_bwd(qc, kc, v, o, lse, d_o, name):
    lp = qc.shape[0]
    bq = min(ROW_TILE, lp)
    nq = lp // bq

    def body(q_ref, k_ref, v_ref, o_ref, lse_ref, do_ref, dq_ref, dk_ref, dv_ref):
        qi = pl.program_id(0)

        @pl.when(qi == 0)
        def _():
            dk_ref[...] = jnp.zeros_like(dk_ref)
            dv_ref[...] = jnp.zeros_like(dv_ref)

        d_o = do_ref[...]
        masks = _mla_masks(bq)
        qs = _mla_split_q(q_ref[...], masks)
        lo = _lane((bq, LANES)) < 64
        dos = _split_heads(d_o.astype(BF16), lo)
        od = o_ref[...] * d_o
        lse_blk = lse_ref[...]
        delta, lses = [], []
        for h in range(4):
            odp = od[:, LANES * (h // 2):LANES * (h // 2 + 1)]
            delta.append(jnp.sum(jnp.where(lo, odp, 0.0) if h % 2 == 0 else jnp.where(lo, 0.0, odp), axis=1, keepdims=True))
            c0 = LANES * (h // 2) + 64 * (h % 2)
            lses.append(lse_blk[:, c0:c0 + 1])
        ri = lax.broadcasted_iota(jnp.int32, (bq, bq), 0)
        ci = lax.broadcasted_iota(jnp.int32, (bq, bq), 1)

        def step(kb, carry, masked):
            off = pl.multiple_of(kb * bq, bq)
            ks = _mla_per_head_k(k_ref[pl.ds(off, bq), :])
            vs = _per_head(v_ref[pl.ds(off, bq), :])
            ss = [_dot(qs[h], ks[h], "nt") for h in range(4)]
            dps = [_dot(dos[h], vs[h], "nt") for h in range(4)]
            prbs, dss = [], []
            for h in range(4):
                s = ss[h]
                if masked:
                    s = jnp.where(ci <= ri, s, NEG)
                pr = jnp.exp(s - lses[h])
                prbs.append(pr.astype(BF16))
                dss.append((pr * (dps[h] - delta[h])).astype(BF16))
            dv_acc = [_dot(prbs[2 * p], dos[2 * p], "tn") + _dot(prbs[2 * p + 1], dos[2 * p + 1], "tn") for p in range(2)]
            dqs = [_dot(dss[h], ks[h]) for h in range(4)]
            dk_acc = [_dot(dss[2 * p], qs[2 * p], "tn") + _dot(dss[2 * p + 1], qs[2 * p + 1], "tn") for p in range(2)]
            dk_ref[pl.ds(off, bq), :] += jnp.concatenate(dk_acc, axis=1)
            dv_ref[pl.ds(off, bq), :] += jnp.concatenate(dv_acc, axis=1)
            return tuple(carry[h] + dqs[h] for h in range(4))

        zero = jnp.zeros((bq, 2 * LANES), F32)
        carry = step(qi, (zero,) * 4, True)
        def several(t, c):
            for r in range(KEY_UNROLL):
                c = step(qi - 1 - r - KEY_UNROLL * t, c, False)
            return c

        carry = lax.fori_loop(0, qi // KEY_UNROLL, several, carry)
        rem = qi % KEY_UNROLL
        carry = lax.fori_loop(0, rem, lambda t, c: step(rem - 1 - t, c, False), carry)
        dq_ref[...] = jnp.concatenate([jnp.where(masks[0], carry[0], 0.0) + jnp.where(masks[1], carry[1], 0.0),
                                       jnp.where(masks[2], carry[2], 0.0) + jnp.where(masks[3], carry[3], 0.0)], axis=1)

    blk = pl.BlockSpec((bq, 2 * LANES), lambda i: (i, 0))
    wide = pl.BlockSpec((bq, 4 * LANES), lambda i: (i, 0))
    return pl.pallas_call(
        body, name=name, grid=(nq,),
        in_specs=[wide, _resident((lp, 4 * LANES), 0), _resident((lp, 2 * LANES), 0), blk, blk, blk],
        out_specs=[wide, _resident((lp, 4 * LANES), 0), _resident((lp, 2 * LANES), 0)],
        out_shape=[jax.ShapeDtypeStruct((lp, 4 * LANES), F32), jax.ShapeDtypeStruct((lp, 4 * LANES), F32),
                   jax.ShapeDtypeStruct((lp, 2 * LANES), F32)],
        compiler_params=_cp("arbitrary"),
    )(qc, kc, v, o, lse, d_o)


def _mix_out(y_pre, z, o_sb, o_mla, g_ssd, g_sb, g_mla):
    return jnp.concatenate([_rms(y_pre * _silu(z), g_ssd), _rms(o_sb, g_sb), _rms(o_mla, g_mla)], axis=1)


def _ffn_act(up_a, up_b, halo_a, halo_b, w_a, w_b, b_a, b_b, i):
    ca = conv_fwd(up_a, halo_a, w_a, i) + b_a
    cb_ = conv_fwd(up_b, halo_b, w_b, i) + b_b
    return ca, cb_


def layer_fwd(h, w, cs, sn, l):
    lp = h.shape[0]
    nm = f"l{l}_"
    hn = rowwise(lambda i, n, x, g: _rms(x, g), [RI(h), PA(w["norm_mix_g"])], [RO(D_MODEL, BF16)], nm + "rms_mix", lp, tm=WIDE_TILE)[0]
    u = mm(hn, w["w_main"], "nn", F32, nm + "in_main")
    qkv = mm(hn, w["w_sb"], "nn", BF16, nm + "in_sb")
    xbc_c = rowwise(lambda i, n, x, hl, cw, cb_: _silu(conv_fwd(x, hl, cw, i) + cb_),
                    [RI(u, SSD_XBC, 0), HP(u, SSD_XBC, 0), PA(w["ssd_conv_w"]), PA(w["ssd_conv_b"])],
                    [RO(SSD_XBC, F32)], nm + "ssd_conv", lp, tm=WIDE_TILE)[0]
    y_pre, hprev = ssd_fwd(xbc_c, u, w["dt_bias"], w["a_log"], w["d_skip"], nm + "ssd_fwd")
    o_sb, rs_sb = sb_attn_fwd(qkv, nm + "sb_fwd")
    qn, kvn = rowwise(lambda i, n, qa, ckv, gq, gkv: (_rms(qa, gq, MLA_Q_RANK), _rms(ckv, gkv)),
                      [RI(u, 256, U_QA // 256), RI(u, LANES, U_CKV // LANES), PA(w["q_norm_g"]), PA(w["kv_norm_g"])],
                      [RO(256, BF16), RO(LANES, BF16)], nm + "mla_rms", lp, tm=WIDE_TILE)
    qf = mm(qn, w["w_uq"], "nn", F32, nm + "mla_uq")
    kvf = mm(kvn, w["w_ukv"], "nn", F32, nm + "mla_ukv")

    def pack(i, n, qf_, kvf_, kr4, cos, sin):
        qf_ = qf_ * MLA_SCALE
        qr = qf_[:, 256:384]
        qr = qr * cos + rope_rot(qr) * sin
        kr = kr4 * cos + rope_rot(kr4) * sin
        qc = jnp.concatenate([qf_[:, 0:128], qr, qf_[:, 128:256], qr], axis=1)
        kc = jnp.concatenate([kvf_[:, 0:128], kr, kvf_[:, 128:256], kr], axis=1)
        return qc, kc, kvf_[:, 256:512]

    qc, kc, vv = rowwise(pack, [RI(qf), RI(kvf), RI(u, LANES, U_KR4 // LANES), RI(cs), RI(sn)],
                         [RO(512, BF16), RO(512, BF16), RO(256, BF16)], nm + "mla_pack", lp, tm=WIDE_TILE)
    o_mla, lse = mla_attn_fwd(qc, kc, vv, nm + "mla_fwd")
    cat = rowwise(lambda i, n, *a: _mix_out(*a),
                  [RI(y_pre), RI(u, SSD_WIDTH, U_Z // SSD_WIDTH), RI(o_sb), RI(o_mla),
                   PA(w["ssd_norm_g"]), PA(w["sb_norm_g"]), PA(w["mla_norm_g"])],
                  [RO(D_MODEL, BF16)], nm + "mix_out", lp, tm=WIDE_TILE)[0]
    h_mid = mm(cat, w["w_out"], "nn", F32, nm + "out_proj", add=h)
    hn2 = rowwise(lambda i, n, x, g: _rms(x, g), [RI(h_mid), PA(w["norm_ffn_g"])], [RO(D_MODEL, BF16)], nm + "rms_ffn", lp, tm=WIDE_TILE)[0]
    up_a = mm(hn2, w["w_up_a"], "nn", F32, nm + "up_a")
    up_b = mm(hn2, w["w_up_b"], "nn", F32, nm + "up_b")
    wc = 1408

    def act(i, n, ua, ub, ha, hb_, wa, wb, ba, bb_):
        ca, cb_ = _ffn_act(ua, ub, ha, hb_, wa, wb, ba, bb_, i)
        return _silu(ca) * cb_

    a_t = rowwise(act, [RI(up_a, wc, 0, True), RI(up_b, wc, 0, True), HP(up_a, wc, 0, True), HP(up_b, wc, 0, True),
                        PA(w["ffn_conv_w_a"], wc, 0, True), PA(w["ffn_conv_w_b"], wc, 0, True),
                        PA(w["ffn_conv_b_a"], wc, 0, True), PA(w["ffn_conv_b_b"], wc, 0, True)],
                  [RO(D_FF, BF16, wc, True)], nm + "ffn_act", lp, tm=WIDE_TILE, ncol=D_FF // wc)[0]
    h_out = mm(a_t, w["w_down"], "nn", F32, nm + "down", add=h_mid)
    saved = dict(h=h, hn=hn, u=u, qkv=qkv, xbc_c=xbc_c, y_pre=y_pre, hprev=hprev, o_sb=o_sb, rs_sb=rs_sb, qn=qn, kvn=kvn,
                 qc=qc, kc=kc, vv=vv, o_mla=o_mla, lse=lse, cat=cat, h_mid=h_mid, hn2=hn2, up_a=up_a, up_b=up_b, a_t=a_t)
    return h_out, saved


def layer_bwd(dh_out, w, s, cs, sn, l):
    lp = dh_out.shape[0]
    nm = f"l{l}b_"
    g = {}
    wc = 1408
    ncolf = D_FF // wc
    g["w_down"] = mm(s["a_t"], dh_out, "tn", BF16, nm + "dw_down")
    d_act = mm(dh_out, w["w_down"], "nt", F32, nm + "d_act")

    def act_bwd(i, n, ua, ub, ha, hb_, wa, wb, ba, bb_, da_):
        ca, cb_ = _ffn_act(ua, ub, ha, hb_, wa, wb, ba, bb_, i)
        sg = jax.nn.sigmoid(ca)
        dca = da_ * cb_ * (sg * (1.0 + ca * (1.0 - sg)))
        dcb = da_ * (ca * sg)
        return (dca, dcb, conv_bwd_w(dca, ua, ha, i, FFN_CONV), conv_bwd_w(dcb, ub, hb_, i, FFN_CONV),
                jnp.sum(dca, axis=0, keepdims=True), jnp.sum(dcb, axis=0, keepdims=True))

    dca, dcb, g["ffn_conv_w_a"], g["ffn_conv_w_b"], g["ffn_conv_b_a"], g["ffn_conv_b_b"] = rowwise(
        act_bwd, [RI(s["up_a"], wc, 0, True), RI(s["up_b"], wc, 0, True), HP(s["up_a"], wc, 0, True),
                  HP(s["up_b"], wc, 0, True), PA(w["ffn_conv_w_a"], wc, 0, True), PA(w["ffn_conv_w_b"], wc, 0, True),
                  PA(w["ffn_conv_b_a"], wc, 0, True), PA(w["ffn_conv_b_b"], wc, 0, True), RI(d_act, wc, 0, True)],
        [RO(D_FF, F32, wc, True), RO(D_FF, F32, wc, True), AO(FFN_CONV, D_FF, wc, True), AO(FFN_CONV, D_FF, wc, True),
         AO(1, D_FF, wc, True), AO(1, D_FF, wc, True)], nm + "ffn_act_bwd", lp, tm=WIDE_TILE // 2, ncol=ncolf)

    def conv_t(i, n, da_, db_, ha, hb_, wa, wb):
        return conv_bwd_data(da_, ha, wa, i, n), conv_bwd_data(db_, hb_, wb, i, n)

    dup_a, dup_b = rowwise(conv_t, [RI(dca, wc, 0, True), RI(dcb, wc, 0, True), HN(dca, wc, 0, True), HN(dcb, wc, 0, True),
                                    PA(w["ffn_conv_w_a"], wc, 0, True), PA(w["ffn_conv_w_b"], wc, 0, True)],
                           [RO(D_FF, BF16, wc, True), RO(D_FF, BF16, wc, True)], nm + "ffn_conv_t", lp, tm=WIDE_TILE, ncol=ncolf)
    g["w_up_a"] = mm(s["hn2"], dup_a, "tn", BF16, nm + "dw_up_a")
    g["w_up_b"] = mm(s["hn2"], dup_b, "tn", BF16, nm + "dw_up_b")
    dhn2 = mm(dup_a, w["w_up_a"], "nt", F32, nm + "dhn2_a")
    dhn2 = mm(dup_b, w["w_up_b"], "nt", F32, nm + "dhn2_b", add=dhn2)

    def rms_bwd(i, n, x, gg, dy, dres):
        _, vjp = jax.vjp(_rms, x, gg)
        dx, dg = vjp(dy)
        return dres + dx, dg

    dh_mid, g["norm_ffn_g"] = rowwise(rms_bwd, [RI(s["h_mid"]), PA(w["norm_ffn_g"]), RI(dhn2), RI(dh_out)],
                                      [RO(D_MODEL, F32), AO(1, D_MODEL)], nm + "rms_ffn_bwd", lp, tm=WIDE_TILE)
    g["w_out"] = mm(s["cat"], dh_mid, "tn", BF16, nm + "dw_out")
    d_cat = mm(dh_mid, w["w_out"], "nt", F32, nm + "d_cat")
    u = s["u"]

    def mix_bwd(i, n, y_pre, z, o_sb, o_mla, g1, g2, g3, dcat):
        _, vjp = jax.vjp(_mix_out, y_pre, z, o_sb, o_mla, g1, g2, g3)
        return vjp(dcat)

    dy_pre, dz, do_sb, do_mla, g["ssd_norm_g"], g["sb_norm_g"], g["mla_norm_g"] = rowwise(
        mix_bwd, [RI(s["y_pre"]), RI(u, SSD_WIDTH, U_Z // SSD_WIDTH), RI(s["o_sb"]), RI(s["o_mla"]),
                  PA(w["ssd_norm_g"]), PA(w["sb_norm_g"]), PA(w["mla_norm_g"]), RI(d_cat)],
        [RO(SSD_WIDTH, F32), RO(SSD_WIDTH, BF16), RO(SB_WIDTH, F32), RO(256, F32),
         AO(1, SSD_WIDTH), AO(1, SB_WIDTH), AO(1, 256)], nm + "mix_out_bwd", lp, tm=WIDE_TILE)
    dxs, dbp, dcp, ddtp, pg = ssd_bwd(s["xbc_c"], u, w["dt_bias"], w["a_log"], w["d_skip"], s["hprev"], dy_pre, nm + "ssd_bwd")
    pg = pg.reshape(2, SUBLANES, LANES).sum(axis=0)
    g["dt_bias"], g["a_log"], g["d_skip"] = pg[0:1], pg[1:2], pg[2:3]

    def conv4_bwd(i, n, x, hl, cw, cb_, dxs_, dbp_, dcp_, ddtp_):
        pre = conv_fwd(x, hl, cw, i) + cb_
        d_out = jnp.concatenate([dxs_, dbp_, dcp_], axis=1)
        sg = jax.nn.sigmoid(pre)
        d_pre = d_out * (sg * (1.0 + pre * (1.0 - sg)))
        ddt = ddtp_[:, 0:128] + ddtp_[:, 128:256]
        return d_pre, ddt, conv_bwd_w(d_pre, x, hl, i, SSD_CONV), jnp.sum(d_pre, axis=0, keepdims=True)

    d_pre, ddt, g["ssd_conv_w"], g["ssd_conv_b"] = rowwise(
        conv4_bwd, [RI(u, SSD_XBC, 0), HP(u, SSD_XBC, 0), PA(w["ssd_conv_w"]), PA(w["ssd_conv_b"]),
                    RI(dxs), RI(dbp), RI(dcp), RI(ddtp)],
        [RO(SSD_XBC, F32), RO(LANES, BF16), AO(SSD_CONV, SSD_XBC), AO(1, SSD_XBC)], nm + "ssd_conv_bwd", lp, tm=WIDE_TILE)
    d_xbc = rowwise(lambda i, n, d, hn_, cw: conv_bwd_data(d, hn_, cw, i, n),
                    [RI(d_pre), HN(d_pre), PA(w["ssd_conv_w"])], [RO(SSD_XBC, BF16)], nm + "ssd_conv_t", lp, tm=WIDE_TILE)[0]
    dq_sb, dk_sb, dv_sb = sb_attn_bwd(s["qkv"], s["rs_sb"], do_sb, nm + "sb_bwd")
    dqkv = jnp.concatenate([dq_sb, dk_sb.astype(BF16), dv_sb.astype(BF16)], axis=1)
    dqc, dkc, dvv = mla_attn_bwd(s["qc"], s["kc"], s["vv"], s["o_mla"], s["lse"], do_mla, nm + "mla_bwd")

    def unpack(i, n, dqc_, dkc_, dvv_, cos, sin):
        dqr = dqc_[:, 128:256] + dqc_[:, 384:512]
        dqr = dqr * cos + rope_rot_t(dqr * sin)
        dkr = dkc_[:, 128:256] + dkc_[:, 384:512]
        dkr = dkr * cos + rope_rot_t(dkr * sin)
        dq = jnp.concatenate([dqc_[:, 0:128], dqc_[:, 256:384], dqr], axis=1) * MLA_SCALE
        dkv = jnp.concatenate([dkc_[:, 0:128], dkc_[:, 256:384], dvv_], axis=1)
        return dq, dkv, dkr

    dq, dkv, dkr4 = rowwise(unpack, [RI(dqc), RI(dkc), RI(dvv), RI(cs), RI(sn)],
                            [RO(384, BF16), RO(512, BF16), RO(LANES, BF16)], nm + "mla_unpack", lp, tm=WIDE_TILE)
    g["w_uq"] = mm(s["qn"], dq, "tn", F32, nm + "dw_uq")
    g["w_ukv"] = mm(s["kvn"], dkv, "tn", F32, nm + "dw_ukv")
    dqn = mm(dq, w["w_uq"], "nt", F32, nm + "dqn")
    dkvn = mm(dkv, w["w_ukv"], "nt", F32, nm + "dkvn")

    def mla_rms_bwd(i, n, qa, ckv, gq, gkv, dqn_, dkvn_):
        _, vjp = jax.vjp(lambda a, b, c, d: (_rms(a, c, MLA_Q_RANK), _rms(b, d)), qa, ckv, gq, gkv)
        return vjp((dqn_, dkvn_))

    dqa, dckv, g["q_norm_g"], g["kv_norm_g"] = rowwise(
        mla_rms_bwd, [RI(u, 256, U_QA // 256), RI(u, LANES, U_CKV // LANES), PA(w["q_norm_g"]), PA(w["kv_norm_g"]),
                      RI(dqn), RI(dkvn)],
        [RO(256, BF16), RO(LANES, BF16), AO(1, 256), AO(1, LANES)], nm + "mla_rms_bwd", lp, tm=WIDE_TILE)
    du = jnp.concatenate([d_xbc, dz, dqa, dckv, dkr4, ddt, jnp.zeros((lp, LANES), BF16)], axis=1)
    g["w_main"] = mm(s["hn"], du, "tn", F32, nm + "dw_main")
    g["w_sb"] = mm(s["hn"], dqkv, "tn", F32, nm + "dw_sb")
    dhn = mm(du, w["w_main"], "nt", F32, nm + "dhn_main")
    dhn = mm(dqkv, w["w_sb"], "nt", F32, nm + "dhn_sb", add=dhn)
    dh_in, g["norm_mix_g"] = rowwise(rms_bwd, [RI(s["h"]), PA(w["norm_mix_g"]), RI(dhn), RI(dh_mid)],
                                     [RO(D_MODEL, F32), AO(1, D_MODEL)], nm + "rms_mix_bwd", lp, tm=WIDE_TILE)
    return dh_in, g


_IN_CUTS = np.cumsum((512, 1024, 8, 256, 256, 256, 192, 128, 32))


def _pad_cols(a, n):
    return jnp.pad(a, ((0, 0), (0, n - a.shape[1])))


def prep_layer_weights(full, l):
    w_in = full["w_in"][l]
    c = _IN_CUTS
    z, xbc, dtr = w_in[:, :c[0]], w_in[:, c[0]:c[1]], w_in[:, c[1]:c[2]]
    q_sb, k_sb, v_sb = w_in[:, c[2]:c[3]], w_in[:, c[3]:c[4]], w_in[:, c[4]:c[5]]
    q_a, c_kv, k_r = w_in[:, c[5]:c[6]], w_in[:, c[6]:c[7]], w_in[:, c[7]:c[8]]
    w_main = jnp.concatenate([xbc, z, _pad_cols(q_a, 256), c_kv, k_r, k_r, k_r, k_r, _pad_cols(dtr, 256)], axis=1)
    assert w_main.shape[1] == U_MAIN
    row = lambda v, n=None: _pad_cols(v.reshape(1, -1).astype(F32), v.size if n is None else n)
    uq = full["mla_w_uq"][l].reshape(MLA_Q_RANK, 4, 96)
    w_uq = jnp.concatenate([uq[:, :, :64].reshape(MLA_Q_RANK, 256), uq[:, :, 64:].reshape(MLA_Q_RANK, 128)], axis=1)
    w_uq = jnp.pad(w_uq, ((0, 256 - MLA_Q_RANK), (0, 0)))
    ukv = full["mla_w_ukv"][l].reshape(MLA_KV_RANK, 4, 128)
    w_ukv = jnp.concatenate([ukv[:, :, :64].reshape(MLA_KV_RANK, 256), ukv[:, :, 64:].reshape(MLA_KV_RANK, 256)], axis=1)
    return dict(
        norm_mix_g=row(full["norm_mix_g"][l]), w_main=w_main, w_sb=jnp.concatenate([q_sb * SB_SCALE, k_sb, v_sb], axis=1),
        ssd_conv_w=full["ssd_conv_w"][l], ssd_conv_b=row(full["ssd_conv_b"][l]),
        dt_bias=row(full["ssd_dt_bias"][l], LANES), a_log=row(full["ssd_a_log"][l], LANES), d_skip=row(full["ssd_d"][l], LANES),
        ssd_norm_g=row(full["ssd_norm_g"][l]), sb_norm_g=row(full["sb_norm_g"][l]),
        q_norm_g=row(full["mla_q_norm_g"][l], 256), kv_norm_g=row(full["mla_kv_norm_g"][l]),
        w_uq=w_uq, w_ukv=w_ukv, mla_norm_g=row(full["mla_norm_g"][l]),
        w_out=full["w_out"][l], norm_ffn_g=row(full["norm_ffn_g"][l]),
        w_up_a=full["ffn_w_up"][l][:, :D_FF], w_up_b=full["ffn_w_up"][l][:, D_FF:],
        ffn_conv_w_a=full["ffn_conv_w"][l][:, :D_FF], ffn_conv_w_b=full["ffn_conv_w"][l][:, D_FF:],
        ffn_conv_b_a=row(full["ffn_conv_b"][l][:D_FF]), ffn_conv_b_b=row(full["ffn_conv_b"][l][D_FF:]),
        w_down=full["ffn_w_down"][l],
    )


def unprep_layer_grads(g):
    wm = g["w_main"]
    xbc, z = wm[:, U_XBC:U_XBC + 1024], wm[:, U_Z:U_Z + 512]
    q_a, c_kv = wm[:, U_QA:U_QA + MLA_Q_RANK], wm[:, U_CKV:U_CKV + 128]
    k_r = (wm[:, U_KR4:U_KR4 + 32] + wm[:, U_KR4 + 32:U_KR4 + 64] + wm[:, U_KR4 + 64:U_KR4 + 96] + wm[:, U_KR4 + 96:U_KR4 + 128])
    dtr = wm[:, U_DT:U_DT + SSD_HEADS]
    w_sb = g["w_sb"]
    w_in = jnp.concatenate([z, xbc, dtr, w_sb[:, :SB_WIDTH] * SB_SCALE, w_sb[:, SB_WIDTH:], q_a, c_kv, k_r], axis=1)
    guq = g["w_uq"][:MLA_Q_RANK]
    guq = jnp.concatenate([guq[:, :256].reshape(MLA_Q_RANK, 4, 64), guq[:, 256:].reshape(MLA_Q_RANK, 4, 32)], axis=2)
    gukv = g["w_ukv"]
    gukv = jnp.concatenate([gukv[:, :256].reshape(MLA_KV_RANK, 4, 64), gukv[:, 256:].reshape(MLA_KV_RANK, 4, 64)], axis=2)
    return dict(
        norm_mix_g=g["norm_mix_g"][0], w_in=w_in, ssd_conv_w=g["ssd_conv_w"], ssd_conv_b=g["ssd_conv_b"][0],
        ssd_dt_bias=g["dt_bias"][0, :SSD_HEADS], ssd_a_log=g["a_log"][0, :SSD_HEADS], ssd_d=g["d_skip"][0, :SSD_HEADS],
        ssd_norm_g=g["ssd_norm_g"][0], sb_norm_g=g["sb_norm_g"][0], mla_q_norm_g=g["q_norm_g"][0, :MLA_Q_RANK],
        mla_kv_norm_g=g["kv_norm_g"][0], mla_w_uq=guq.reshape(MLA_Q_RANK, 384), mla_w_ukv=gukv.reshape(MLA_KV_RANK, 512),
        mla_norm_g=g["mla_norm_g"][0], w_out=g["w_out"], norm_ffn_g=g["norm_ffn_g"][0],
        ffn_w_up=jnp.concatenate([g["w_up_a"], g["w_up_b"]], axis=1),
        ffn_conv_w=jnp.concatenate([g["ffn_conv_w_a"], g["ffn_conv_w_b"]], axis=1),
        ffn_conv_b=jnp.concatenate([g["ffn_conv_b_a"][0], g["ffn_conv_b_b"][0]], axis=0),
        ffn_w_down=g["w_down"],
    )


def rope_tables(lp):
    pos = jnp.arange(lp, dtype=F32)
    inv = 1.0 / (ROPE_BASE ** (jnp.arange(0, MLA_ROPE, 2, dtype=F32) / MLA_ROPE))
    ang = pos[:, None] * inv[None, :]
    ang = jnp.concatenate([ang, ang] * 4, axis=-1)
    return jnp.cos(ang), jnp.sin(ang)


def local_step(x_seq, target, full):
    seq = x_seq.shape[0]
    length = seq + N_META
    lp = -(-length // ROW_TILE) * ROW_TILE
    cs, sn = rope_tables(lp)
    h = jnp.concatenate([full["meta_tokens"].astype(F32), x_seq, jnp.zeros((lp - length, D_MODEL), F32)], axis=0)
    tgt = jnp.pad(target, ((N_META, lp - length), (0, 0)))
    ws = [prep_layer_weights(full, l) for l in range(DEPTH)]
    saved = []
    for l in range(DEPTH):
        h, s = layer_fwd(h, ws[l], cs, sn, l)
        saved.append(s)
    fg = full["final_norm_g"].reshape(1, D_MODEL).astype(F32)
    tm = min(WIDE_TILE, lp)

    def loss_fn(i, n, x, g, t):
        rows = _rows_iota(x) + i * tm
        valid = jnp.logical_and(rows >= N_META, rows < length)

        def f(x_, g_):
            err = jnp.where(valid, _rms(x_, g_) - t, 0.0)
            return 0.5 * jnp.sum(err * err) * (1.0 / D_MODEL)

        val, (dx, dg) = jax.value_and_grad(f, argnums=(0, 1))(x, g)
        return dx, jnp.full((1, LANES), val, F32), dg

    dh, loss_row, g_final = rowwise(loss_fn, [RI(h), PA(fg), RI(tgt)], [RO(D_MODEL, F32), AO(1, LANES), AO(1, D_MODEL)],
                                    "loss_head", lp, tm=WIDE_TILE)
    grads = {}
    per_layer = [None] * DEPTH
    for l in reversed(range(DEPTH)):
        dh, g = layer_bwd(dh, ws[l], saved[l], cs, sn, l)
        per_layer[l] = unprep_layer_grads(g)
    for k in per_layer[0]:
        grads[k] = jnp.stack([per_layer[l][k] for l in range(DEPTH)], axis=0)
    grads["final_norm_g"] = g_final[0]
    grads["meta_tokens"] = dh[:N_META]
    return loss_row[0, 0], dh[N_META:length], grads


_ANY = pl.BlockSpec(memory_space=pl.ANY)


def chip_exchange(srcs, modes, name):
    n = len(srcs)
    flips = ((1, 0), (0, 1), (1, 1))

    def body(*refs):
        ins, outs = refs[:n], refs[n:2 * n]
        send_sems, recv_sems, fwd_send_sems, fwd_recv_sems, loc_sems = refs[2 * n:]
        x, y, c = lax.axis_index("x"), lax.axis_index("y"), lax.axis_index("c")
        me = 2 * x + y
        waits, forwards = [], []
        for a in range(n):
            whole = modes[a] != "slab"
            cp = pltpu.make_async_copy(ins[a] if whole else ins[a].at[me], outs[a].at[me], loc_sems.at[a])
            cp.start()
            waits.append(cp.wait)
            half = ins[a].shape[0] // 2 if modes[a] == "bcast_split" else None
            for k, (fx, fy) in enumerate(flips):
                px = 1 - x if fx else x
                py = 1 - y if fy else y
                peer = 2 * px + py
                if half is None:
                    src = ins[a] if whole else ins[a].at[peer]
                    dst = outs[a].at[me]
                else:
                    src = ins[a].at[pl.ds(c * half, half)]
                    dst = outs[a].at[me, pl.ds(c * half, half)]
                rc = pltpu.make_async_remote_copy(src_ref=src, dst_ref=dst, send_sem=send_sems.at[a, k],
                                                  recv_sem=recv_sems.at[a, k], device_id=(px, py, c), device_id_type=MESH_ID)
                rc.start()
                if half is None:
                    waits.append(rc.wait)
                else:
                    waits.append(rc.wait_send)
                    landed = outs[a].at[peer, pl.ds(c * half, half)]
                    fw = pltpu.make_async_remote_copy(src_ref=landed, dst_ref=landed, send_sem=fwd_send_sems.at[a, k],
                                                      recv_sem=fwd_recv_sems.at[a, k], device_id=(x, y, 1 - c),
                                                      device_id_type=MESH_ID)
                    forwards.append((rc, fw))
        for rc, fw in forwards:
            rc.wait_recv()
            fw.start()
        for rc, fw in forwards:
            fw.wait()
        for w in waits:
            w()

    out_shape = [jax.ShapeDtypeStruct((N_CHIPS,) + (s.shape if m != "slab" else s.shape[1:]), s.dtype) for s, m in zip(srcs, modes)]
    return pl.pallas_call(
        body, name=name, in_specs=[_ANY] * n, out_specs=[_ANY] * n, out_shape=out_shape,
        scratch_shapes=[pltpu.SemaphoreType.DMA((n, 3)), pltpu.SemaphoreType.DMA((n, 3)), pltpu.SemaphoreType.DMA((n, 3)),
                        pltpu.SemaphoreType.DMA((n, 3)), pltpu.SemaphoreType.DMA((n,))],
    )(*srcs)


def _piece(ref, mode, k):
    if mode == "slab":
        return ref.at[k]
    if mode == "rows":
        rs = ref.shape[1] // N_CHIPS
        return ref.at[:, pl.ds(pl.multiple_of(k * rs, 16), rs), :]
    if mode == "cols":
        cs = ref.shape[2] // N_CHIPS
        return ref.at[:, :, pl.ds(pl.multiple_of(k * cs, LANES), cs)]
    return ref


def _piece_shape(shape, mode):
    if mode == "slab":
        return shape[1:]
    if mode == "rows":
        return (shape[0], shape[1] // N_CHIPS, shape[2])
    if mode == "cols":
        return (shape[0], shape[1], shape[2] // N_CHIPS)
    return shape


def grad_exchange(srcs, modes, name):
    n = len(srcs)
    flips = ((1, 0), (0, 1), (1, 1))

    def body(*refs):
        ins, outs = refs[:n], refs[n:2 * n]
        send_sems, recv_sems, fwd_send_sems, fwd_recv_sems, sib_send_sems, sib_recv_sems, loc_sems = refs[2 * n:]
        x, y, c = lax.axis_index("x"), lax.axis_index("y"), lax.axis_index("c")
        me = 2 * x + y
        sibling = (x, y, 1 - c)
        waits, forwards = [], []
        for a in range(n):
            mine = _piece(ins[a], modes[a], me)
            slot = outs[a].at[4 * c + me]
            cp = pltpu.make_async_copy(mine, slot, loc_sems.at[a])
            cp.start()
            sb = pltpu.make_async_remote_copy(src_ref=mine, dst_ref=slot, send_sem=sib_send_sems.at[a],
                                              recv_sem=sib_recv_sems.at[a], device_id=sibling, device_id_type=MESH_ID)
            sb.start()
            waits += [cp.wait, sb.wait]
            for k, (fx, fy) in enumerate(flips):
                px = 1 - x if fx else x
                py = 1 - y if fy else y
                peer = 2 * px + py
                rc = pltpu.make_async_remote_copy(src_ref=_piece(ins[a], modes[a], peer), dst_ref=slot,
                                                  send_sem=send_sems.at[a, k], recv_sem=recv_sems.at[a, k],
                                                  device_id=(px, py, c), device_id_type=MESH_ID)
                rc.start()
                landed = outs[a].at[4 * c + peer]
                fw = pltpu.make_async_remote_copy(src_ref=landed, dst_ref=landed, send_sem=fwd_send_sems.at[a, k],
                                                  recv_sem=fwd_recv_sems.at[a, k], device_id=sibling, device_id_type=MESH_ID)
                waits.append(rc.wait_send)
                forwards.append((rc, fw))
        for rc, fw in forwards:
            rc.wait_recv()
            fw.start()
        for rc, fw in forwards:
            fw.wait()
        for w in waits:
            w()

    out_shape = [jax.ShapeDtypeStruct((2 * N_CHIPS,) + tuple(_piece_shape(s.shape, m)), s.dtype) for s, m in zip(srcs, modes)]
    dma = pltpu.SemaphoreType.DMA
    return pl.pallas_call(
        body, name=name, in_specs=[_ANY] * n, out_specs=[_ANY] * n, out_shape=out_shape,
        scratch_shapes=[dma((n, 3)), dma((n, 3)), dma((n, 3)), dma((n, 3)), dma((n,)), dma((n,)), dma((n,))],
    )(*srcs)


WEIGHT_NAMES = ("meta_tokens", "norm_mix_g", "w_in", "ssd_conv_w", "ssd_conv_b", "ssd_dt_bias", "ssd_a_log", "ssd_d",
                "ssd_norm_g", "sb_norm_g", "mla_q_norm_g", "mla_kv_norm_g", "mla_w_uq", "mla_w_ukv", "mla_norm_g",
                "w_out", "norm_ffn_g", "ffn_w_up", "ffn_conv_w", "ffn_conv_b", "ffn_w_down", "final_norm_g")
SHARD_AXIS = {"meta_tokens": 1, "w_in": 2, "ssd_conv_w": 2, "mla_w_uq": 2, "mla_w_ukv": 2, "w_out": 1, "ffn_w_up": 2,
              "ffn_conv_w": 2, "ffn_w_down": 1}
SHARDED = tuple(n for n in WEIGHT_NAMES if n in SHARD_AXIS)
REPLICATED = tuple(n for n in WEIGHT_NAMES if n not in SHARD_AXIS)
GATHER_BF16 = ("w_in", "mla_w_uq", "mla_w_ukv", "w_out", "ffn_w_up", "ffn_w_down")
GATHER_F32 = ("meta_tokens", "ssd_conv_w", "ffn_conv_w")
PACK_ROWS = ROW_TILE


def pack(arrs, dtype):
    flat = jnp.concatenate([a.reshape(-1).astype(dtype) for a in arrs])
    per = PACK_ROWS * PACK_W
    total = -(-flat.size // per) * per
    return jnp.pad(flat, (0, total - flat.size)).reshape(total // PACK_W, PACK_W)


def unpack(buf, shapes):
    flat = buf.reshape(-1)
    out, off = [], 0
    for shp in shapes:
        size = int(np.prod(shp))
        out.append(flat[off:off + size].reshape(shp))
        off += size
    return out


def gather_weights(a):
    full = {n: a[n] for n in REPLICATED}
    bufs = [pack([a[n] for n in GATHER_BF16], BF16), pack([a[n] for n in GATHER_F32], F32)]
    got = chip_exchange(bufs, ("bcast_split", "bcast"), "gather_weights")
    for names, g in ((GATHER_BF16, got[0]), (GATHER_F32, got[1])):
        pieces = [unpack(g[k], [a[n].shape for n in names]) for k in range(N_CHIPS)]
        for idx, n in enumerate(names):
            full[n] = jnp.concatenate([pieces[k][idx] for k in range(N_CHIPS)], axis=SHARD_AXIS[n])
    return full


BIG = ("w_in", "w_out", "ffn_w_up", "ffn_w_down")
BIG_MODE = {"w_in": "slab", "w_out": "rows", "ffn_w_up": "cols", "ffn_w_down": "rows"}
SMALL_SHARDED = tuple(n for n in SHARDED if n not in BIG)
ADAM_TILE = 128


def _adamw(i, n, *vals):
    parts, (w, m, v) = vals[:2 * N_CHIPS], vals[2 * N_CHIPS:]
    g = parts[0].astype(F32)
    for p in parts[1:]:
        g = g + p.astype(F32)
    m = ADAM_B1 * m + (1.0 - ADAM_B1) * g
    v = ADAM_B2 * v + (1.0 - ADAM_B2) * jnp.square(g)
    m_hat = m / (1.0 - ADAM_B1 ** ADAM_STEP)
    v_hat = v / (1.0 - ADAM_B2 ** ADAM_STEP)
    delta = -ADAM_LR * (m_hat / (jnp.sqrt(v_hat) + ADAM_EPS) + ADAM_WD * w)
    return g, delta, m, v


def _adamw_call(got, w, m, v, name):
    rows, width = w.shape
    flat = got.reshape(2 * N_CHIPS * rows, width)
    blk = rows // ADAM_TILE
    ins = [RI(flat, rblk=k * blk) for k in range(2 * N_CHIPS)] + [RI(w), RI(m), RI(v)]
    return rowwise(_adamw, ins, [RO(width, F32)] * 4, name, rows, tm=ADAM_TILE)


def reduce_and_update(a, grads):
    srcs, modes = [], []
    for n in BIG:
        g = grads[n].astype(BF16)
        if n == "w_in":
            cs = a[n].shape[2]
            g = g.reshape(DEPTH, D_MODEL, N_CHIPS, cs).transpose(2, 0, 1, 3)
        srcs.append(g)
        modes.append(BIG_MODE[n])
    slabs = []
    for k in range(N_CHIPS):
        parts = []
        for n in SMALL_SHARDED:
            ax = SHARD_AXIS[n]
            size = a[n].shape[ax]
            parts.append(lax.slice_in_dim(grads[n], k * size, (k + 1) * size, axis=ax))
        slabs.append(pack(parts, BF16))
    srcs += [jnp.stack(slabs, axis=0), pack([grads[n] for n in REPLICATED], F32)]
    modes += ["slab", "bcast"]
    got = grad_exchange(srcs, modes, "exchange_grads")
    outs = {}
    kinds = ("grad", "delta", "new_m", "new_v")
    for n, g8 in zip(BIG, got):
        shp = a[n].shape
        rows = shp[0] * shp[1]
        flat = lambda t: t.reshape(rows, shp[2])
        res = _adamw_call(g8.reshape(2 * N_CHIPS, rows, shp[2]), flat(a[n]), flat(a["m_" + n]), flat(a["v_" + n]), "adamw_" + n)
        for kind, val in zip(kinds, res):
            outs[(kind, n)] = val.reshape(shp)
    for tag, names, g8 in (("small", SMALL_SHARDED, got[len(BIG)]), ("rep", REPLICATED, got[len(BIG) + 1])):
        shapes = [a[n].shape for n in names]
        packed = [pack([a[pre + n] for n in names], F32) for pre in ("", "m_", "v_")]
        res = _adamw_call(g8, *packed, "adamw_" + tag)
        for kind, buf in zip(kinds, res):
            for n, val in zip(names, unpack(buf, shapes)):
                outs[(kind, n)] = val
    return outs


INPUT_NAMES = ("x",) + WEIGHT_NAMES + ("loss_target",) + tuple("m_" + n for n in WEIGHT_NAMES) + tuple("v_" + n for n in WEIGHT_NAMES)


def kernel(x, meta_tokens, norm_mix_g, w_in, ssd_conv_w, ssd_conv_b, ssd_dt_bias, ssd_a_log, ssd_d, ssd_norm_g, sb_norm_g, mla_q_norm_g, mla_kv_norm_g, mla_w_uq, mla_w_ukv, mla_norm_g, w_out, norm_ffn_g, ffn_w_up, ffn_conv_w, ffn_conv_b, ffn_w_down, final_norm_g, loss_target, m_meta_tokens, m_norm_mix_g, m_w_in, m_ssd_conv_w, m_ssd_conv_b, m_ssd_dt_bias, m_ssd_a_log, m_ssd_d, m_ssd_norm_g, m_sb_norm_g, m_mla_q_norm_g, m_mla_kv_norm_g, m_mla_w_uq, m_mla_w_ukv, m_mla_norm_g, m_w_out, m_norm_ffn_g, m_ffn_w_up, m_ffn_conv_w, m_ffn_conv_b, m_ffn_w_down, m_final_norm_g, v_meta_tokens, v_norm_mix_g, v_w_in, v_ssd_conv_w, v_ssd_conv_b, v_ssd_dt_bias, v_ssd_a_log, v_ssd_d, v_ssd_norm_g, v_sb_norm_g, v_mla_q_norm_g, v_mla_kv_norm_g, v_mla_w_uq, v_mla_w_ukv, v_mla_norm_g, v_w_out, v_norm_ffn_g, v_ffn_w_up, v_ffn_conv_w, v_ffn_conv_b, v_ffn_w_down, v_final_norm_g):
    args = (x, meta_tokens, norm_mix_g, w_in, ssd_conv_w, ssd_conv_b, ssd_dt_bias, ssd_a_log, ssd_d, ssd_norm_g, sb_norm_g, mla_q_norm_g, mla_kv_norm_g, mla_w_uq, mla_w_ukv, mla_norm_g, w_out, norm_ffn_g, ffn_w_up, ffn_conv_w, ffn_conv_b, ffn_w_down, final_norm_g, loss_target, m_meta_tokens, m_norm_mix_g, m_w_in, m_ssd_conv_w, m_ssd_conv_b, m_ssd_dt_bias, m_ssd_a_log, m_ssd_d, m_ssd_norm_g, m_sb_norm_g, m_mla_q_norm_g, m_mla_kv_norm_g, m_mla_w_uq, m_mla_w_ukv, m_mla_norm_g, m_w_out, m_norm_ffn_g, m_ffn_w_up, m_ffn_conv_w, m_ffn_conv_b, m_ffn_w_down, m_final_norm_g, v_meta_tokens, v_norm_mix_g, v_w_in, v_ssd_conv_w, v_ssd_conv_b, v_ssd_dt_bias, v_ssd_a_log, v_ssd_d, v_ssd_norm_g, v_sb_norm_g, v_mla_q_norm_g, v_mla_kv_norm_g, v_mla_w_uq, v_mla_w_ukv, v_mla_norm_g, v_w_out, v_norm_ffn_g, v_ffn_w_up, v_ffn_conv_w, v_ffn_conv_b, v_ffn_w_down, v_final_norm_g)
    a = dict(zip(INPUT_NAMES, args, strict=True))
    full = gather_weights(a)
    loss, grad_x, grads = local_step(a["x"][0], a["loss_target"][0], full)
    loss = lax.psum(loss, ("x", "y", "c"))
    outs = reduce_and_update(a, grads)
    result = [loss, grad_x[None]]
    for kind in ("grad", "delta", "new_m", "new_v"):
        result += [outs[(kind, n)] for n in WEIGHT_NAMES]
    return tuple(result)
```

```python
import functools
import math

import numpy as np
import jax
import jax.numpy as jnp
from jax import lax
from jax.experimental import pallas as pl
from jax.experimental.pallas import tpu as pltpu

F32 = jnp.float32
BF16 = jnp.bfloat16
MESH_ID = pl.DeviceIdType.MESH

D_MODEL = 1024
DEPTH = 2
N_META = 16
EPS = 1e-6
SSD_HEADS = 8
SSD_WIDTH = 512
SSD_XBC = 1024
SSD_CONV = 4
SB_WIDTH = 256
SB_SCALE = 64 ** -0.5
MLA_Q_RANK = 192
MLA_KV_RANK = 128
MLA_ROPE = 32
MLA_SCALE = 96 ** -0.5
ROPE_BASE = 10000.0
D_FF = 2816
FFN_CONV = 3
IN_COLS = 2664
N_CHIPS = 4

ADAM_LR = 0.001
ADAM_B1 = 0.9
ADAM_B2 = 0.999
ADAM_EPS = 1e-08
ADAM_WD = 0.01
ADAM_STEP = 10

LANES = 128
SUBLANES = 8
ROW_TILE = 256
KEY_UNROLL = 4
assert KEY_UNROLL == 4
WIDE_TILE = 768
VMEM_LIMIT = 56 * 1024 * 1024
PACK_W = 1024

U_XBC, U_Z, U_QA, U_CKV, U_KR4, U_DT, U_MAIN = 0, 1024, 1536, 1792, 1920, 2048, 2304
NEG = -1e30


def _cp(*sem):
    return pltpu.CompilerParams(dimension_semantics=sem if sem else None, vmem_limit_bytes=VMEM_LIMIT)


def _pick(dim, pref):
    if dim <= pref:
        return dim
    best = None
    for t in range(LANES, pref + 1, LANES):
        if dim % t == 0:
            best = t
    assert best is not None, (dim, pref)
    return best


def _dot(a, b, dims="nn", precision=None):
    dn = {"nn": (((1,), (0,)), ((), ())), "nt": (((1,), (1,)), ((), ())), "tn": (((0,), (0,)), ((), ()))}[dims]
    return lax.dot_general(a, b, dn, preferred_element_type=F32, precision=precision)


def _tri_dot(tri, x):
    hi = x.astype(BF16)
    r1 = x - hi.astype(F32)
    mid = r1.astype(BF16)
    lo = (r1 - mid.astype(F32)).astype(BF16)
    t = tri.astype(BF16)
    return _dot(t, hi) + _dot(t, mid) + _dot(t, lo)


def _softplus(x):
    return jnp.maximum(x, 0.0) + jnp.log1p(jnp.exp(-jnp.abs(x)))


def _silu(x):
    return x * jax.nn.sigmoid(x)


def _rms(x, g, n=None):
    n = x.shape[-1] if n is None else n
    ms = jnp.sum(x * x, axis=-1, keepdims=True) * (1.0 / n)
    return x * lax.rsqrt(ms + EPS) * g


def mm(a, b, dims, out_dtype, name, add=None, tm=None, tn=None, tk=None):
    if dims == "nn":
        (m, k), (k2, n) = a.shape, b.shape
    elif dims == "nt":
        (m, k), (n, k2) = a.shape, b.shape
    else:
        (k, m), (k2, n) = a.shape, b.shape
    assert k == k2, (a.shape, b.shape, dims)
    if dims == "tn":
        tm, tn, tk = _pick(m, tm or 1408), _pick(n, tn or 1408), _pick(k, tk or 1408)
    else:
        tm, tn, tk = _pick(m, tm or (1408 if k <= 2304 else 768)), _pick(n, tn or 1408), _pick(k, tk or 2816)
    nk = k // tk
    if dims == "tn":
        a_spec = pl.BlockSpec((tk, tm), lambda j, i, kk: (kk, i))
    else:
        a_spec = pl.BlockSpec((tm, tk), lambda j, i, kk: (i, kk))
    if dims == "nt":
        b_spec = pl.BlockSpec((tn, tk), lambda j, i, kk: (j, kk))
    else:
        b_spec = pl.BlockSpec((tk, tn), lambda j, i, kk: (kk, j))
    o_spec = pl.BlockSpec((tm, tn), lambda j, i, kk: (i, j))
    has_add = add is not None

    def body(*refs):
        a_ref, b_ref = refs[0], refs[1]
        add_ref = refs[2] if has_add else None
        o_ref = refs[3] if has_add else refs[2]
        part = _dot(a_ref[...].astype(BF16), b_ref[...].astype(BF16), dims)

        def finish(r):
            if has_add:
                r = r + add_ref[...].astype(F32)
            o_ref[...] = r.astype(o_ref.dtype)

        if nk == 1:
            finish(part)
            return
        acc_ref = refs[-1]
        kk = pl.program_id(2)

        @pl.when(kk == 0)
        def _():
            acc_ref[...] = part

        @pl.when(jnp.logical_and(kk > 0, kk < nk - 1))
        def _():
            acc_ref[...] += part

        @pl.when(kk == nk - 1)
        def _():
            finish(acc_ref[...] + part)

    in_specs = [a_spec, b_spec] + ([o_spec] if has_add else [])
    args = (a, b) + ((add,) if has_add else ())
    return pl.pallas_call(
        body, name=name, grid=(n // tn, m // tm, nk),
        in_specs=in_specs, out_specs=o_spec,
        out_shape=jax.ShapeDtypeStruct((m, n), out_dtype),
        scratch_shapes=[pltpu.VMEM((tm, tn), F32)] if nk > 1 else [],
        compiler_params=_cp("parallel", "parallel", "arbitrary"),
    )(*args)


def RI(arr, width=None, cidx=0, cv=False, rblk=0):
    return ("row" if rblk == 0 else ("row", rblk), arr, arr.shape[1] if width is None else width, cidx, cv)


def HP(arr, width=None, cidx=0, cv=False):
    return ("prev", arr, arr.shape[1] if width is None else width, cidx, cv)


def HN(arr, width=None, cidx=0, cv=False):
    return ("next", arr, arr.shape[1] if width is None else width, cidx, cv)


def PA(arr, width=None, cidx=0, cv=False):
    return ("par", arr, arr.shape[1] if width is None else width, cidx, cv)


def RO(ncols, dtype, width=None, cv=False):
    return ("row", ncols, dtype, ncols if width is None else width, cv)


def AO(nrows, ncols, width=None, cv=False):
    return ("acc", (nrows, ncols), F32, ncols if width is None else width, cv)


def rowwise(fn, ins, outs, name, rows, tm=ROW_TILE, ncol=1):
    tm = min(tm, rows)
    assert rows % tm == 0
    nrow = rows // tm
    hb = tm // SUBLANES
    last_hb = rows // SUBLANES - 1
    in_specs, args = [], []
    for kind, arr, width, cidx, cv in ins:
        def cmap(j, cidx=cidx, cv=cv):
            return cidx + j if cv else cidx
        if kind == "row":
            spec = pl.BlockSpec((tm, width), lambda j, i, cmap=cmap: (i, cmap(j)))
        elif isinstance(kind, tuple):
            spec = pl.BlockSpec((tm, width), lambda j, i, cmap=cmap, rblk=kind[1]: (i + rblk, cmap(j)))
        elif kind == "prev":
            spec = pl.BlockSpec((SUBLANES, width), lambda j, i, cmap=cmap: (jnp.maximum(i * hb - 1, 0), cmap(j)))
        elif kind == "next":
            spec = pl.BlockSpec((SUBLANES, width), lambda j, i, cmap=cmap: (jnp.minimum((i + 1) * hb, last_hb), cmap(j)))
        else:
            spec = pl.BlockSpec((arr.shape[0], width), lambda j, i, cmap=cmap: (0, cmap(j)))
        in_specs.append(spec)
        args.append(arr)
    out_specs, out_shapes, acc_cv = [], [], []
    for kind, shp, dtype, width, cv in outs:
        if kind == "row":
            out_specs.append(pl.BlockSpec((tm, width), lambda j, i, cv=cv: (i, j if cv else 0)))
            out_shapes.append(jax.ShapeDtypeStruct((rows, shp), dtype))
            acc_cv.append(None)
        else:
            out_specs.append(pl.BlockSpec((shp[0], width), lambda j, i, cv=cv: (0, j if cv else 0)))
            out_shapes.append(jax.ShapeDtypeStruct(shp, dtype))
            acc_cv.append(cv)
    n_in = len(ins)

    def body(*refs):
        j = pl.program_id(0)
        i = pl.program_id(1)
        vals = fn(i, nrow, *[r[...] for r in refs[:n_in]])
        if not isinstance(vals, (tuple, list)):
            vals = (vals,)
        for o_ref, v, cv in zip(refs[n_in:], vals, acc_cv):
            if cv is None:
                o_ref[...] = v.astype(o_ref.dtype)
            else:
                first = (i == 0) if cv else jnp.logical_and(i == 0, j == 0)

                @pl.when(first)
                def _(o_ref=o_ref, v=v):
                    o_ref[...] = v.astype(o_ref.dtype)

                @pl.when(jnp.logical_not(first))
                def _(o_ref=o_ref, v=v):
                    o_ref[...] += v.astype(o_ref.dtype)

    res = pl.pallas_call(
        body, name=name, grid=(ncol, nrow), in_specs=in_specs, out_specs=out_specs, out_shape=out_shapes,
        compiler_params=_cp("arbitrary", "arbitrary"),
    )(*args)
    return res


def _rows_iota(x):
    return lax.broadcasted_iota(jnp.int32, x.shape, 0)


def shift_down(x, halo, s):
    if s == 0:
        return x
    tm = x.shape[0]
    top = pltpu.roll(halo, s, 0)
    if tm > SUBLANES:
        top = jnp.concatenate([top, jnp.zeros((tm - SUBLANES, x.shape[1]), x.dtype)], axis=0)
    return jnp.where(_rows_iota(x) < s, top, pltpu.roll(x, s, 0))


def shift_up(x, halo, s):
    if s == 0:
        return x
    tm = x.shape[0]
    bot = pltpu.roll(halo, SUBLANES - s, 0)
    if tm > SUBLANES:
        bot = jnp.concatenate([jnp.zeros((tm - SUBLANES, x.shape[1]), x.dtype), bot], axis=0)
    return jnp.where(_rows_iota(x) >= tm - s, bot, pltpu.roll(x, tm - s, 0))


def conv_fwd(x, halo, w, i):
    kw = w.shape[0]
    halo = jnp.where(i == 0, 0.0, halo)
    out = None
    for k in range(kw):
        term = w[k:k + 1, :] * shift_down(x, halo, kw - 1 - k)
        out = term if out is None else out + term
    return out


def conv_bwd_data(dy, halo_next, w, i, n):
    kw = w.shape[0]
    halo_next = jnp.where(i == n - 1, 0.0, halo_next)
    out = None
    for k in range(kw):
        term = w[k:k + 1, :] * shift_up(dy, halo_next, kw - 1 - k)
        out = term if out is None else out + term
    return out


def conv_bwd_w(dy, x, halo, i, kw):
    halo = jnp.where(i == 0, 0.0, halo)
    rows = [jnp.sum(dy * shift_down(x, halo, kw - 1 - k), axis=0, keepdims=True) for k in range(kw)]
    return jnp.concatenate(rows, axis=0)


def _lane(shape):
    return lax.broadcasted_iota(jnp.int32, shape, 1)


def rope_rot(x):
    lane = _lane(x.shape) % MLA_ROPE
    return jnp.where(lane < MLA_ROPE // 2, -pltpu.roll(x, LANES - MLA_ROPE // 2, 1), pltpu.roll(x, MLA_ROPE // 2, 1))


def rope_rot_t(g):
    lane = _lane(g.shape) % MLA_ROPE
    return jnp.where(lane < MLA_ROPE // 2, pltpu.roll(g, LANES - MLA_ROPE // 2, 1), -pltpu.roll(g, MLA_ROPE // 2, 1))


def _ssd_common(g, xs, dt_raw, bias, alog, q):
    lane = _lane((q, LANES))
    pre = dt_raw + bias
    dt = jnp.where(lane < SSD_HEADS, _softplus(pre), 0.0)
    a_row = -jnp.exp(alog)
    d_a = dt * a_row
    ri = lax.broadcasted_iota(jnp.int32, (q, q), 0)
    ci = lax.broadcasted_iota(jnp.int32, (q, q), 1)
    causal = ri >= ci
    acs = _tri_dot(causal, d_a)
    acs_t = acs.T
    subl = lax.broadcasted_iota(jnp.int32, (LANES, q), 0)
    heads = [4 * g + i for i in range(4)]
    lo = lane < 64

    def col(arr, h):
        return jnp.sum(jnp.where(lane == h, arr, 0.0), axis=1, keepdims=True)

    def lanes4(v):
        m = lo if v[0].shape[0] == q else lo[0:1, :]
        return jnp.concatenate([jnp.where(m, v[0], v[1]), jnp.where(m, v[2], v[3])], axis=1)

    cols = [col(acs, h) for h in heads]
    rows = [jnp.sum(jnp.where(subl == h, acs_t, 0.0), axis=0, keepdims=True) for h in heads]
    tots = [c_[q - 1:q, :] for c_ in cols]
    acs4 = lanes4(cols)
    dt4 = lanes4([col(dt, h) for h in heads])
    lms = [jnp.exp(jnp.where(causal, cols[i] - rows[i], NEG)) for i in range(4)]
    lane4 = _lane((q, 2 * LANES))
    hm = [jnp.logical_and(lane4 >= 64 * i, lane4 < 64 * (i + 1)) for i in range(4)]
    return dict(lane=lane, lo=lo, pre=pre, dt=dt, a_row=a_row, heads=heads, tots=tots, lms=lms, ri=ri, ci=ci, hm=hm,
                lanes4=lanes4, eacs=jnp.exp(acs4), dte=jnp.exp(lanes4(tots) - acs4), dt4=dt4, x=xs * dt4)


def _pick_lane(row_arr, h):
    return jnp.sum(jnp.where(_lane(row_arr.shape) == h, row_arr, 0.0), axis=1, keepdims=True)


def _etot(tots):
    sub = lax.broadcasted_iota(jnp.int32, (2 * LANES, LANES), 0)
    e = [jnp.exp(t) for t in tots]
    return jnp.where(sub < 64, e[0], jnp.where(sub < 128, e[1], jnp.where(sub < 192, e[2], e[3]))), e


def _half(arr, i, lo):
    slab = arr[:, LANES * (i // 2):LANES * (i // 2 + 1)]
    return jnp.where(lo, slab, 0.0) if i % 2 == 0 else jnp.where(lo, 0.0, slab)


def ssd_fwd(xbc_c, u_main, bias_row, alog_row, d_row, name):
    lp = xbc_c.shape[0]
    q = min(ROW_TILE, lp)
    nc = lp // q
    dt_blk = U_DT // LANES

    def body(xs_ref, b_ref, c_ref, dt_ref, bias_ref, alog_ref, d_ref, y_ref, hp_ref, h_scr):
        g = pl.program_id(0)
        c = pl.program_id(1)

        @pl.when(c == 0)
        def _():
            h_scr[...] = jnp.zeros_like(h_scr)

        xs = xs_ref[...]
        bb = b_ref[...].astype(BF16)
        cb_ = c_ref[...].astype(BF16)
        s = _ssd_common(g, xs, dt_ref[...], bias_ref[...], alog_ref[...], q)
        gmat = _dot(cb_, bb, "nt")
        ms = [(gmat * s["lms"][i]).astype(BF16) for i in range(4)]
        xjs = [_half(s["x"], i, s["lo"]).astype(BF16) for i in range(4)]
        ys = [_dot(ms[i], xjs[i]) for i in range(4)]
        hp = h_scr[...]
        hp_ref[...] = hp
        yoff = _dot(cb_, hp.astype(BF16), "nt") * s["eacs"]
        d4 = s["lanes4"]([_pick_lane(d_ref[...], h) for h in s["heads"]])
        y_ref[...] = jnp.concatenate([ys[0] + ys[1], ys[2] + ys[3]], axis=1) + yoff + d4 * xs
        etot, _ = _etot(s["tots"])
        h_scr[...] = hp * etot + _dot((s["x"] * s["dte"]).astype(BF16), bb, "tn")

    in_specs = [
        pl.BlockSpec((q, 2 * LANES), lambda g, c: (c, g)),
        pl.BlockSpec((q, LANES), lambda g, c: (c, 4 + g)),
        pl.BlockSpec((q, LANES), lambda g, c: (c, 6 + g)),
        pl.BlockSpec((q, LANES), lambda g, c: (c, dt_blk)),
        pl.BlockSpec((1, LANES), lambda g, c: (0, 0)),
        pl.BlockSpec((1, LANES), lambda g, c: (0, 0)),
        pl.BlockSpec((1, LANES), lambda g, c: (0, 0)),
    ]
    out_specs = [
        pl.BlockSpec((q, 2 * LANES), lambda g, c: (c, g)),
        pl.BlockSpec((None, None, 2 * LANES, LANES), lambda g, c: (g, c, 0, 0)),
    ]
    return pl.pallas_call(
        body, name=name, grid=(2, nc), in_specs=in_specs, out_specs=out_specs,
        out_shape=[jax.ShapeDtypeStruct((lp, SSD_WIDTH), F32), jax.ShapeDtypeStruct((2, nc, 2 * LANES, LANES), F32)],
        scratch_shapes=[pltpu.VMEM((2 * LANES, LANES), F32)],
        compiler_params=_cp("arbitrary", "arbitrary"),
    )(xbc_c, xbc_c, xbc_c, u_main, bias_row, alog_row, d_row)


def ssd_bwd(xbc_c, u_main, bias_row, alog_row, d_row, hprev, dy, name):
    lp = xbc_c.shape[0]
    q = min(ROW_TILE, lp)
    nc = lp // q
    dt_blk = U_DT // LANES

    def body(xs_ref, b_ref, c_ref, dt_ref, bias_ref, alog_ref, d_ref, hp_ref, dy_ref,
             dxs_ref, db_ref, dc_ref, ddt_ref, pg_ref, dh_scr):
        g = pl.program_id(0)
        cc = pl.program_id(1)

        @pl.when(cc == 0)
        def _():
            dh_scr[...] = jnp.zeros_like(dh_scr)
            pg_ref[...] = jnp.zeros_like(pg_ref)

        xs = xs_ref[...]
        bb = b_ref[...].astype(BF16)
        cb_ = c_ref[...].astype(BF16)
        s = _ssd_common(g, xs, dt_ref[...], bias_ref[...], alog_ref[...], q)
        lane, lo, x, hm, heads = s["lane"], s["lo"], s["x"], s["hm"], s["heads"]
        d_y = dy_ref[...]
        hp = hp_ref[...]
        hpb = hp.astype(BF16)
        dhn = dh_scr[...]
        dhnb = dhn.astype(BF16)
        xd = x * s["dte"]
        gmat = _dot(cb_, bb, "nt")
        m32s = [gmat * s["lms"][i] for i in range(4)]
        xjs = [_half(x, i, lo).astype(BF16) for i in range(4)]
        dyjs = [_half(d_y, i, lo).astype(BF16) for i in range(4)]
        dxparts = [_dot(m32s[i].astype(BF16), dyjs[i], "tn") for i in range(4)]
        dms = [_dot(dyjs[i], xjs[i], "nt") for i in range(4)]
        dg = dms[0] * s["lms"][0] + dms[1] * s["lms"][1] + dms[2] * s["lms"][2] + dms[3] * s["lms"][3]
        wms = [dms[i] * m32s[i] for i in range(4)]
        row_part = [jnp.sum(wm, axis=1, keepdims=True) for wm in wms]
        col_part = [jnp.sum(wm, axis=0, keepdims=True) for wm in wms]
        dgb = dg.astype(BF16)
        yoff = _dot(cb_, hpb, "nt") * s["eacs"]
        d_t = (d_y * s["eacs"]).astype(BF16)
        d_c = _dot(dgb, bb) + _dot(d_t, hpb)
        d_hp = _dot(d_t, cb_, "tn")
        dxd = _dot(bb, dhnb, "nt")
        d_b = _dot(dgb, cb_, "tn") + _dot(xd.astype(BF16), dhnb)
        d_x = jnp.concatenate([dxparts[0] + dxparts[1], dxparts[2] + dxparts[3]], axis=1) + dxd * s["dte"]
        r = dxd * xd
        a_terms = d_y * yoff - r

        def hsum(arr):
            return [jnp.sum(jnp.where(hm[i], arr, 0.0), axis=1, keepdims=True) for i in range(4)]

        dacs, rs = hsum(a_terms), hsum(r)
        hh = dhn * hp
        sub = lax.broadcasted_iota(jnp.int32, (2 * LANES, LANES), 0)
        hsums = [jnp.sum(jnp.where(jnp.logical_and(sub >= 64 * i, sub < 64 * (i + 1)), hh, 0.0), keepdims=True) for i in range(4)]
        last = lax.broadcasted_iota(jnp.int32, (q, 1), 0) == q - 1
        etot, etots = _etot(s["tots"])
        ddacs = jnp.zeros((q, LANES), F32)
        for i, h in enumerate(heads):
            dtot = jnp.sum(rs[i], keepdims=True) + hsums[i] * etots[i]
            ddacs = ddacs + jnp.where(lane == h, dacs[i] + row_part[i] + jnp.where(last, dtot, 0.0), 0.0)
        subl = lax.broadcasted_iota(jnp.int32, (LANES, q), 0)
        cols_t = jnp.zeros((LANES, q), F32)
        for i, h in enumerate(heads):
            cols_t = cols_t + jnp.where(subl == h, col_part[i], 0.0)
        ddacs = ddacs - cols_t.T
        da = _tri_dot(s["ri"] <= s["ci"], ddacs)
        ddt_own = hsum(d_x * xs)
        ddt = da * s["a_row"]
        for i, h in enumerate(heads):
            ddt = ddt + jnp.where(lane == h, ddt_own[i], 0.0)
        draw = ddt * jax.nn.sigmoid(s["pre"])
        ddt_ref[...] = draw
        d4 = s["lanes4"]([_pick_lane(d_ref[...], h) for h in heads])
        dxs_ref[...] = d4 * d_y + d_x * s["dt4"]
        db_ref[...] = d_b
        dc_ref[...] = d_c
        dds = hsum(d_y * xs)
        lane1 = lane[0:1, :]
        dd_row = jnp.zeros((1, LANES), F32)
        for i, h in enumerate(heads):
            dd_row = dd_row + jnp.where(lane1 == h, jnp.sum(dds[i], keepdims=True), 0.0)
        dbias_row = jnp.sum(draw, axis=0, keepdims=True)
        dalog_row = jnp.sum(da * s["dt"], axis=0, keepdims=True) * s["a_row"]
        sub8 = lax.broadcasted_iota(jnp.int32, (SUBLANES, LANES), 0)
        pg_ref[...] += (jnp.where(sub8 == 0, dbias_row, 0.0) + jnp.where(sub8 == 1, dalog_row, 0.0)
                        + jnp.where(sub8 == 2, dd_row, 0.0))
        dh_scr[...] = d_hp + etot * dhn

    rc = lambda c: nc - 1 - c
    in_specs = [
        pl.BlockSpec((q, 2 * LANES), lambda g, c: (rc(c), g)),
        pl.BlockSpec((q, LANES), lambda g, c: (rc(c), 4 + g)),
        pl.BlockSpec((q, LANES), lambda g, c: (rc(c), 6 + g)),
        pl.BlockSpec((q, LANES), lambda g, c: (rc(c), dt_blk)),
        pl.BlockSpec((1, LANES), lambda g, c: (0, 0)),
        pl.BlockSpec((1, LANES), lambda g, c: (0, 0)),
        pl.BlockSpec((1, LANES), lambda g, c: (0, 0)),
        pl.BlockSpec((None, None, 2 * LANES, LANES), lambda g, c: (g, rc(c), 0, 0)),
        pl.BlockSpec((q, 2 * LANES), lambda g, c: (rc(c), g)),
    ]
    out_specs = [
        pl.BlockSpec((q, 2 * LANES), lambda g, c: (rc(c), g)),
        pl.BlockSpec((q, LANES), lambda g, c: (rc(c), g)),
        pl.BlockSpec((q, LANES), lambda g, c: (rc(c), g)),
        pl.BlockSpec((q, LANES), lambda g, c: (rc(c), g)),
        pl.BlockSpec((SUBLANES, LANES), lambda g, c: (g, 0)),
    ]
    per_group = jax.ShapeDtypeStruct((lp, 2 * LANES), F32)
    return pl.pallas_call(
        body, name=name, grid=(2, nc), in_specs=in_specs, out_specs=out_specs,
        out_shape=[jax.ShapeDtypeStruct((lp, SSD_WIDTH), F32), per_group, per_group, per_group,
                   jax.ShapeDtypeStruct((2 * SUBLANES, LANES), F32)],
        scratch_shapes=[pltpu.VMEM((2 * LANES, LANES), F32)],
        compiler_params=_cp("arbitrary", "arbitrary"),
    )(xbc_c, xbc_c, xbc_c, u_main, bias_row, alog_row, d_row, hprev, dy)


def _sb_blocks(qs, ks, r_runs, masked, bq, after_scores=None):
    ri = lax.broadcasted_iota(jnp.int32, (bq, bq), 0)
    ci = lax.broadcasted_iota(jnp.int32, (bq, bq), 1)
    tri_after = (ri > ci).astype(BF16)
    zs = [_dot(qj, kj, "nt") for qj, kj in zip(qs, ks)]
    extra = after_scores() if after_scores is not None else None
    sigs, ubs = [], []
    for z in zs:
        zb = z.astype(BF16)
        u = -(jnp.maximum(zb, 0) + jnp.log(1 + jnp.exp(-jnp.abs(zb))))
        sigs.append(jnp.exp(zb + u))
        if masked:
            u = jnp.where(ci < ri, u, jnp.zeros_like(u))
        ubs.append(u)
    afters = [_dot(ub, tri_after) for ub in ubs]
    usums = [after[:, 0:1] + ub[:, 0:1].astype(F32) for after, ub in zip(afters, ubs)]
    ws = []
    for sig, after, r_run in zip(sigs, afters, r_runs):
        w = sig * jnp.exp(after + r_run).astype(BF16)
        if masked:
            w = jnp.where(ci < ri, w, jnp.zeros_like(w))
        ws.append(w)
    return usums, sigs, ws, extra


def _split_heads(x, lo):
    out = []
    zero = jnp.zeros((x.shape[0], LANES), x.dtype)
    for p in range(2):
        xp = x[:, LANES * p:LANES * (p + 1)]
        out += [jnp.where(lo, xp, zero), jnp.where(lo, zero, xp)]
    return out


def _per_head(x):
    return [x[:, :LANES], x[:, :LANES], x[:, LANES:], x[:, LANES:]]


def _resident(shape, col):
    return pl.BlockSpec(shape, lambda i: (0, col), pipeline_mode=pl.Buffered(1))


def sb_attn_fwd(qkv, name):
    lp = qkv.shape[0]
    bq = min(ROW_TILE, lp)
    nq = lp // bq
    assert nq <= 64

    def body(q_ref, k_ref, v_ref, o_ref, rs_ref):
        qi = pl.program_id(0)
        lane = _lane((bq, LANES))
        lo = lane < 64
        qs = _split_heads(q_ref[...], lo)

        def step(kb, carry, masked):
            off = pl.multiple_of(kb * bq, bq)
            ks = _per_head(k_ref[pl.ds(off, bq), :])
            vs = _per_head(v_ref[pl.ds(off, bq), :])
            heads, rss = carry
            r_runs = [heads[h][1] for h in range(4)]
            rss = list(rss)
            for h in range(4):
                rss[h // 2] = jnp.where(lane == 64 * (h % 2) + kb, r_runs[h], rss[h // 2])
            usums, _, ws, _ = _sb_blocks(qs, ks, r_runs, masked, bq)
            pvs = [_dot(ws[h], vs[h]) for h in range(4)]
            out = tuple((heads[h][0] + pvs[h], r_runs[h] + usums[h]) for h in range(4))
            return out, tuple(rss)

        zero = (jnp.zeros((bq, LANES), F32), jnp.zeros((bq, 1), F32))
        zr = jnp.zeros((bq, LANES), F32)
        carry = step(qi, ((zero,) * 4, (zr, zr)), True)
        def several(t, c):
            for r in range(KEY_UNROLL):
                c = step(qi - 1 - r - KEY_UNROLL * t, c, False)
            return c

        carry = lax.fori_loop(0, qi // KEY_UNROLL, several, carry)
        rem = qi % KEY_UNROLL
        carry = lax.cond(rem >= 2, lambda c: step(rem - 2, step(rem - 1, c, False), False), lambda c: c, carry)
        heads, rss = lax.cond(rem % 2 == 1, lambda c: step(0, c, False), lambda c: c, carry)
        o_ref[...] = jnp.concatenate([jnp.where(lo, heads[0][0], heads[1][0]), jnp.where(lo, heads[2][0], heads[3][0])], axis=1)
        rs_ref[...] = jnp.concatenate(list(rss), axis=1)

    blk = pl.BlockSpec((bq, 2 * LANES), lambda i: (i, 0))
    return pl.pallas_call(
        body, name=name, grid=(nq,),
        in_specs=[blk, _resident((lp, 2 * LANES), 1), _resident((lp, 2 * LANES), 2)],
        out_specs=[blk, blk],
        out_shape=[jax.ShapeDtypeStruct((lp, SB_WIDTH), F32), jax.ShapeDtypeStruct((lp, SB_WIDTH), F32)],
        compiler_params=_cp("arbitrary"),
    )(qkv, qkv, qkv)


def sb_attn_bwd(qkv, rs, d_o, name):
    lp = qkv.shape[0]
    bq = min(ROW_TILE, lp)
    nq = lp // bq

    def body(q_ref, k_ref, v_ref, rs_ref, do_ref, dq_ref, dk_ref, dv_ref):
        qi = pl.program_id(0)

        @pl.when(qi == 0)
        def _():
            dk_ref[...] = jnp.zeros_like(dk_ref)
            dv_ref[...] = jnp.zeros_like(dv_ref)

        lane = _lane((bq, LANES))
        lo = lane < 64
        qs = _split_heads(q_ref[...], lo)
        dos = _split_heads(do_ref[...].astype(BF16), lo)
        rs_blk = rs_ref[...]
        ri = lax.broadcasted_iota(jnp.int32, (bq, bq), 0)
        ci = lax.broadcasted_iota(jnp.int32, (bq, bq), 1)
        tbefore = (ri < ci).astype(BF16)

        def step(kb, carry, masked):
            off = pl.multiple_of(kb * bq, bq)
            ks = _per_head(k_ref[pl.ds(off, bq), :])
            vs = _per_head(v_ref[pl.ds(off, bq), :])
            r_rights = [jnp.sum(jnp.where(lane == 64 * (h % 2) + kb, rs_blk[:, LANES * (h // 2):LANES * (h // 2 + 1)], 0.0),
                                axis=1, keepdims=True) for h in range(4)]
            _, sigs, wbs, dws = _sb_blocks(qs, ks, r_rights, masked, bq,
                                           after_scores=lambda: [_dot(dos[h], vs[h], "nt") for h in range(4)])
            gs = [wbs[h].astype(F32) * dws[h] for h in range(4)]
            gbs = [g.astype(BF16) for g in gs]
            gbefores = [_dot(gb, tbefore) for gb in gbs]
            dv_acc = [_dot(wbs[2 * p], dos[2 * p], "tn") + _dot(wbs[2 * p + 1], dos[2 * p + 1], "tn") for p in range(2)]
            dzbs = []
            for h in range(4):
                dz = gs[h] - sigs[h].astype(F32) * (gs[h] + gbefores[h] + carry[h][1])
                if masked:
                    dz = jnp.where(ci < ri, dz, 0.0)
                dzbs.append(dz.astype(BF16))
            dqs = [_dot(dzbs[h], ks[h]) for h in range(4)]
            dk_acc = [_dot(dzbs[2 * p], qs[2 * p], "tn") + _dot(dzbs[2 * p + 1], qs[2 * p + 1], "tn") for p in range(2)]
            dk_ref[pl.ds(off, bq), :] += jnp.concatenate(dk_acc, axis=1)
            dv_ref[pl.ds(off, bq), :] += jnp.concatenate(dv_acc, axis=1)
            return tuple((carry[h][0] + dqs[h], carry[h][1] + jnp.sum(gs[h], axis=1, keepdims=True)) for h in range(4))

        zero = (jnp.zeros((bq, LANES), F32), jnp.zeros((bq, 1), F32))
        def several(t, c):
            for r in range(KEY_UNROLL):
                c = step(KEY_UNROLL * t + r, c, False)
            return c

        carry = lax.fori_loop(0, qi // KEY_UNROLL, several, (zero,) * 4)
        rem = qi % KEY_UNROLL
        base = qi - rem
        carry = lax.cond(rem >= 2, lambda c: step(base + 1, step(base, c, False), False), lambda c: c, carry)
        carry = lax.cond(rem % 2 == 1, lambda c: step(qi - 1, c, False), lambda c: c, carry)
        carry = step(qi, carry, True)
        dq_ref[...] = jnp.concatenate([jnp.where(lo, carry[0][0], carry[1][0]), jnp.where(lo, carry[2][0], carry[3][0])],
                                      axis=1).astype(dq_ref.dtype)

    blk = pl.BlockSpec((bq, 2 * LANES), lambda i: (i, 0))
    return pl.pallas_call(
        body, name=name, grid=(nq,),
        in_specs=[blk, _resident((lp, 2 * LANES), 1), _resident((lp, 2 * LANES), 2), blk, blk],
        out_specs=[blk, _resident((lp, 2 * LANES), 0), _resident((lp, 2 * LANES), 0)],
        out_shape=[jax.ShapeDtypeStruct((lp, SB_WIDTH), BF16), jax.ShapeDtypeStruct((lp, SB_WIDTH), F32),
                   jax.ShapeDtypeStruct((lp, SB_WIDTH), F32)],
        compiler_params=_cp("arbitrary"),
    )(qkv, qkv, qkv, rs, d_o)


def _mla_masks(bq):
    lane = _lane((bq, 2 * LANES))
    out = []
    for h in range(4):
        j = h % 2
        nope = jnp.logical_and(lane >= 64 * j, lane < 64 * (j + 1))
        rope = jnp.logical_and(lane >= LANES + MLA_ROPE * h, lane < LANES + MLA_ROPE * (h + 1))
        out.append(jnp.logical_or(nope, rope))
    return out


def _mla_split_q(q, masks):
    zero = jnp.zeros((q.shape[0], 2 * LANES), q.dtype)
    return [jnp.where(masks[h], q[:, 2 * LANES * (h // 2):2 * LANES * (h // 2 + 1)], zero) for h in range(4)]


def _mla_per_head_k(k):
    return [k[:, :2 * LANES], k[:, :2 * LANES], k[:, 2 * LANES:], k[:, 2 * LANES:]]


def mla_attn_fwd(qc, kc, v, name):
    lp = qc.shape[0]
    bq = min(ROW_TILE, lp)
    nq = lp // bq

    def body(q_ref, k_ref, v_ref, o_ref, lse_ref):
        qi = pl.program_id(0)
        qs = _mla_split_q(q_ref[...], _mla_masks(bq))
        lo = _lane((bq, LANES)) < 64
        ri = lax.broadcasted_iota(jnp.int32, (bq, bq), 0)
        ci = lax.broadcasted_iota(jnp.int32, (bq, bq), 1)

        def blocks(kbs, carry, masked):
            offs = [pl.multiple_of(kb * bq, bq) for kb in kbs]
            ks = [_mla_per_head_k(k_ref[pl.ds(o, bq), :]) for o in offs]
            ones = jnp.ones((bq, LANES), BF16)
            vs = []
            for o in offs:
                vp = _per_head(v_ref[pl.ds(o, bq), :])
                vs.append([jnp.where(lo, vp[h], ones) if h % 2 == 0 else jnp.where(lo, ones, vp[h]) for h in range(4)])
            ss = [[_dot(qs[h], ks[b][h], "nt") for h in range(4)] for b in range(len(kbs))]
            if masked:
                ss = [[jnp.where(ci <= ri, s, NEG) for s in row] for row in ss]
            prs, alphas, ms = [], [], []
            for h in range(4):
                top = ss[0][h]
                for b in range(1, len(kbs)):
                    top = jnp.maximum(top, ss[b][h])
                m_new = jnp.maximum(carry[h][1], jnp.max(top, axis=1, keepdims=True))
                alphas.append(jnp.exp(carry[h][1] - m_new))
                ms.append(m_new)
                prs.append([jnp.exp(ss[b][h] - m_new).astype(BF16) for b in range(len(kbs))])
            out = []
            for h in range(4):
                acc = carry[h][0] * alphas[h]
                for b in range(len(kbs)):
                    acc = acc + _dot(prs[h][b], vs[b][h])
                out.append((acc, ms[h]))
            return tuple(out)

        zero = (jnp.zeros((bq, LANES), F32), jnp.full((bq, 1), NEG, F32))
        carry = blocks([qi], (zero,) * 4, True)
        carry = lax.fori_loop(0, qi // KEY_UNROLL,
                              lambda t, c: blocks([qi - 1 - r - KEY_UNROLL * t for r in range(KEY_UNROLL)], c, False), carry)
        rem = qi % KEY_UNROLL
        carry = lax.cond(rem >= 2, lambda c: blocks([rem - 1, rem - 2], c, False), lambda c: c, carry)
        carry = lax.cond(rem % 2 == 1, lambda c: blocks([0], c, False), lambda c: c, carry)
        outs, lses = [], []
        for h in range(4):
            acc, m = carry[h]
            l = acc[:, 64:65] if h % 2 == 0 else acc[:, 0:1]
            outs.append(acc / l)
            lses.append(m + jnp.log(l))
        o_ref[...] = jnp.concatenate([jnp.where(lo, outs[0], outs[1]), jnp.where(lo, outs[2], outs[3])], axis=1)
        lse_ref[...] = jnp.concatenate([jnp.where(lo, lses[0], lses[1]), jnp.where(lo, lses[2], lses[3])], axis=1)

    blk = pl.BlockSpec((bq, 2 * LANES), lambda i: (i, 0))
    return pl.pallas_call(
        body, name=name, grid=(nq,),
        in_specs=[pl.BlockSpec((bq, 4 * LANES), lambda i: (i, 0)), _resident((lp, 4 * LANES), 0), _resident((lp, 2 * LANES), 0)],
        out_specs=[blk, blk],
        out_shape=[jax.ShapeDtypeStruct((lp, 2 * LANES), F32), jax.ShapeDtypeStruct((lp, 2 * LANES), F32)],
        compiler_params=_cp("arbitrary"),
    )(qc, kc, v)


def mla_attn_bwd(qc, kc, v, o, lse, d_o, name):
    lp = qc.shape[0]
    bq = min(ROW_TILE, lp)
    nq = lp // bq

    def body(q_ref, k_ref, v_ref, o_ref, lse_ref, do_ref, dq_ref, dk_ref, dv_ref):
        qi = pl.program_id(0)

        @pl.when(qi == 0)
        def _():
            dk_ref[...] = jnp.zeros_like(dk_ref)
            dv_ref[...] = jnp.zeros_like(dv_ref)

        d_o = do_ref[...]
        masks = _mla_masks(bq)
        qs = _mla_split_q(q_ref[...], masks)
        lo = _lane((bq, LANES)) < 64
        dos = _split_heads(d_o.astype(BF16), lo)
        od = o_ref[...] * d_o
        lse_blk = lse_ref[...]
        delta, lses = [], []
        for h in range(4):
            odp = od[:, LANES * (h // 2):LANES * (h // 2 + 1)]
            delta.append(jnp.sum(jnp.where(lo, odp, 0.0) if h % 2 == 0 else jnp.where(lo, 0.0, odp), axis=1, keepdims=True))
            c0 = LANES * (h // 2) + 64 * (h % 2)
            lses.append(lse_blk[:, c0:c0 + 1])
        ri = lax.broadcasted_iota(jnp.int32, (bq, bq), 0)
        ci = lax.broadcasted_iota(jnp.int32, (bq, bq), 1)

        def step(kb, carry, masked):
            off = pl.multiple_of(kb * bq, bq)
            ks = _mla_per_head_k(k_ref[pl.ds(off, bq), :])
            vs = _per_head(v_ref[pl.ds(off, bq), :])
            ss = [_dot(qs[h], ks[h], "nt") for h in range(4)]
            dps = [_dot(dos[h], vs[h], "nt") for h in range(4)]
            prbs, dss = [], []
            for h in range(4):
                s = ss[h]
                if masked:
                    s = jnp.where(ci <= ri, s, NEG)
                pr = jnp.exp(s - lses[h])
                prbs.append(pr.astype(BF16))
                dss.append((pr * (dps[h] - delta[h])).astype(BF16))
            dv_acc = [_dot(prbs[2 * p], dos[2 * p], "tn") + _dot(prbs[2 * p + 1], dos[2 * p + 1], "tn") for p in range(2)]
            dqs = [_dot(dss[h], ks[h]) for h in range(4)]
            dk_acc = [_dot(dss[2 * p], qs[2 * p], "tn") + _dot(dss[2 * p + 1], qs[2 * p + 1], "tn") for p in range(2)]
            dk_ref[pl.ds(off, bq), :] += jnp.concatenate(dk_acc, axis=1)
            dv_ref[pl.ds(off, bq), :] += jnp.concatenate(dv_acc, axis=1)
            return tuple(carry[h] + dqs[h] for h in range(4))

        zero = jnp.zeros((bq, 2 * LANES), F32)
        carry = step(qi, (zero,) * 4, True)
        def several(t, c):
            for r in range(KEY_UNROLL):
                c = step(qi - 1 - r - KEY_UNROLL * t, c, False)
            return c

        carry = lax.fori_loop(0, qi // KEY_UNROLL, several, carry)
        rem = qi % KEY_UNROLL
        carry = lax.cond(rem >= 2, lambda c: step(rem - 2, step(rem - 1, c, False), False), lambda c: c, carry)
        carry = lax.cond(rem % 2 == 1, lambda c: step(0, c, False), lambda c: c, carry)
        dq_ref[...] = jnp.concatenate([jnp.where(masks[0], carry[0], 0.0) + jnp.where(masks[1], carry[1], 0.0),
                                       jnp.where(masks[2], carry[2], 0.0) + jnp.where(masks[3], carry[3], 0.0)], axis=1)

    blk = pl.BlockSpec((bq, 2 * LANES), lambda i: (i, 0))
    wide = pl.BlockSpec((bq, 4 * LANES), lambda i: (i, 0))
    return pl.pallas_call(
        body, name=name, grid=(nq,),
        in_specs=[wide, _resident((lp, 4 * LANES), 0), _resident((lp, 2 * LANES), 0), blk, blk, blk],
        out_specs=[wide, _resident((lp, 4 * LANES), 0), _resident((lp, 2 * LANES), 0)],
        out_shape=[jax.ShapeDtypeStruct((lp, 4 * LANES), F32), jax.ShapeDtypeStruct((lp, 4 * LANES), F32),
                   jax.ShapeDtypeStruct((lp, 2 * LANES), F32)],
        compiler_params=_cp("arbitrary"),
    )(qc, kc, v, o, lse, d_o)


def _mix_out(y_pre, z, o_sb, o_mla, g_ssd, g_sb, g_mla):
    return jnp.concatenate([_rms(y_pre * _silu(z), g_ssd), _rms(o_sb, g_sb), _rms(o_mla, g_mla)], axis=1)


def _ffn_act(up_a, up_b, halo_a, halo_b, w_a, w_b, b_a, b_b, i):
    ca = conv_fwd(up_a, halo_a, w_a, i) + b_a
    cb_ = conv_fwd(up_b, halo_b, w_b, i) + b_b
    return ca, cb_


def layer_fwd(h, w, cs, sn, l):
    lp = h.shape[0]
    nm = f"l{l}_"
    hn = rowwise(lambda i, n, x, g: _rms(x, g), [RI(h), PA(w["norm_mix_g"])], [RO(D_MODEL, BF16)], nm + "rms_mix", lp, tm=WIDE_TILE)[0]
    u = mm(hn, w["w_main"], "nn", F32, nm + "in_main")
    qkv = mm(hn, w["w_sb"], "nn", BF16, nm + "in_sb")
    xbc_c = rowwise(lambda i, n, x, hl, cw, cb_: _silu(conv_fwd(x, hl, cw, i) + cb_),
                    [RI(u, SSD_XBC, 0), HP(u, SSD_XBC, 0), PA(w["ssd_conv_w"]), PA(w["ssd_conv_b"])],
                    [RO(SSD_XBC, F32)], nm + "ssd_conv", lp, tm=WIDE_TILE)[0]
    y_pre, hprev = ssd_fwd(xbc_c, u, w["dt_bias"], w["a_log"], w["d_skip"], nm + "ssd_fwd")
    o_sb, rs_sb = sb_attn_fwd(qkv, nm + "sb_fwd")
    qn, kvn = rowwise(lambda i, n, qa, ckv, gq, gkv: (_rms(qa, gq, MLA_Q_RANK), _rms(ckv, gkv)),
                      [RI(u, 256, U_QA // 256), RI(u, LANES, U_CKV // LANES), PA(w["q_norm_g"]), PA(w["kv_norm_g"])],
                      [RO(256, BF16), RO(LANES, BF16)], nm + "mla_rms", lp, tm=WIDE_TILE)
    qf = mm(qn, w["w_uq"], "nn", F32, nm + "mla_uq")
    kvf = mm(kvn, w["w_ukv"], "nn", F32, nm + "mla_ukv")

    def pack(i, n, qf_, kvf_, kr4, cos, sin):
        qf_ = qf_ * MLA_SCALE
        qr = qf_[:, 256:384]
        qr = qr * cos + rope_rot(qr) * sin
        kr = kr4 * cos + rope_rot(kr4) * sin
        qc = jnp.concatenate([qf_[:, 0:128], qr, qf_[:, 128:256], qr], axis=1)
        kc = jnp.concatenate([kvf_[:, 0:128], kr, kvf_[:, 128:256], kr], axis=1)
        return qc, kc, kvf_[:, 256:512]

    qc, kc, vv = rowwise(pack, [RI(qf), RI(kvf), RI(u, LANES, U_KR4 // LANES), RI(cs), RI(sn)],
                         [RO(512, BF16), RO(512, BF16), RO(256, BF16)], nm + "mla_pack", lp, tm=WIDE_TILE)
    o_mla, lse = mla_attn_fwd(qc, kc, vv, nm + "mla_fwd")
    cat = rowwise(lambda i, n, *a: _mix_out(*a),
                  [RI(y_pre), RI(u, SSD_WIDTH, U_Z // SSD_WIDTH), RI(o_sb), RI(o_mla),
                   PA(w["ssd_norm_g"]), PA(w["sb_norm_g"]), PA(w["mla_norm_g"])],
                  [RO(D_MODEL, BF16)], nm + "mix_out", lp, tm=WIDE_TILE)[0]
    h_mid = mm(cat, w["w_out"], "nn", F32, nm + "out_proj", add=h)
    hn2 = rowwise(lambda i, n, x, g: _rms(x, g), [RI(h_mid), PA(w["norm_ffn_g"])], [RO(D_MODEL, BF16)], nm + "rms_ffn", lp, tm=WIDE_TILE)[0]
    up_a = mm(hn2, w["w_up_a"], "nn", F32, nm + "up_a")
    up_b = mm(hn2, w["w_up_b"], "nn", F32, nm + "up_b")
    wc = 1408

    def act(i, n, ua, ub, ha, hb_, wa, wb, ba, bb_):
        ca, cb_ = _ffn_act(ua, ub, ha, hb_, wa, wb, ba, bb_, i)
        return _silu(ca) * cb_

    a_t = rowwise(act, [RI(up_a, wc, 0, True), RI(up_b, wc, 0, True), HP(up_a, wc, 0, True), HP(up_b, wc, 0, True),
                        PA(w["ffn_conv_w_a"], wc, 0, True), PA(w["ffn_conv_w_b"], wc, 0, True),
                        PA(w["ffn_conv_b_a"], wc, 0, True), PA(w["ffn_conv_b_b"], wc, 0, True)],
                  [RO(D_FF, BF16, wc, True)], nm + "ffn_act", lp, tm=WIDE_TILE, ncol=D_FF // wc)[0]
    h_out = mm(a_t, w["w_down"], "nn", F32, nm + "down", add=h_mid)
    saved = dict(h=h, hn=hn, u=u, qkv=qkv, xbc_c=xbc_c, y_pre=y_pre, hprev=hprev, o_sb=o_sb, rs_sb=rs_sb, qn=qn, kvn=kvn,
                 qc=qc, kc=kc, vv=vv, o_mla=o_mla, lse=lse, cat=cat, h_mid=h_mid, hn2=hn2, up_a=up_a, up_b=up_b, a_t=a_t)
    return h_out, saved


def layer_bwd(dh_out, w, s, cs, sn, l):
    lp = dh_out.shape[0]
    nm = f"l{l}b_"
    g = {}
    wc = 1408
    ncolf = D_FF // wc
    g["w_down"] = mm(s["a_t"], dh_out, "tn", BF16, nm + "dw_down")
    d_act = mm(dh_out, w["w_down"], "nt", F32, nm + "d_act")

    def act_bwd(i, n, ua, ub, ha, hb_, wa, wb, ba, bb_, da_):
        ca, cb_ = _ffn_act(ua, ub, ha, hb_, wa, wb, ba, bb_, i)
        sg = jax.nn.sigmoid(ca)
        dca = da_ * cb_ * (sg * (1.0 + ca * (1.0 - sg)))
        dcb = da_ * (ca * sg)
        return (dca, dcb, conv_bwd_w(dca, ua, ha, i, FFN_CONV), conv_bwd_w(dcb, ub, hb_, i, FFN_CONV),
                jnp.sum(dca, axis=0, keepdims=True), jnp.sum(dcb, axis=0, keepdims=True))

    dca, dcb, g["ffn_conv_w_a"], g["ffn_conv_w_b"], g["ffn_conv_b_a"], g["ffn_conv_b_b"] = rowwise(
        act_bwd, [RI(s["up_a"], wc, 0, True), RI(s["up_b"], wc, 0, True), HP(s["up_a"], wc, 0, True),
                  HP(s["up_b"], wc, 0, True), PA(w["ffn_conv_w_a"], wc, 0, True), PA(w["ffn_conv_w_b"], wc, 0, True),
                  PA(w["ffn_conv_b_a"], wc, 0, True), PA(w["ffn_conv_b_b"], wc, 0, True), RI(d_act, wc, 0, True)],
        [RO(D_FF, F32, wc, True), RO(D_FF, F32, wc, True), AO(FFN_CONV, D_FF, wc, True), AO(FFN_CONV, D_FF, wc, True),
         AO(1, D_FF, wc, True), AO(1, D_FF, wc, True)], nm + "ffn_act_bwd", lp, tm=WIDE_TILE // 2, ncol=ncolf)

    def conv_t(i, n, da_, db_, ha, hb_, wa, wb):
        return conv_bwd_data(da_, ha, wa, i, n), conv_bwd_data(db_, hb_, wb, i, n)

    dup_a, dup_b = rowwise(conv_t, [RI(dca, wc, 0, True), RI(dcb, wc, 0, True), HN(dca, wc, 0, True), HN(dcb, wc, 0, True),
                                    PA(w["ffn_conv_w_a"], wc, 0, True), PA(w["ffn_conv_w_b"], wc, 0, True)],
                           [RO(D_FF, BF16, wc, True), RO(D_FF, BF16, wc, True)], nm + "ffn_conv_t", lp, tm=WIDE_TILE, ncol=ncolf)
    g["w_up_a"] = mm(s["hn2"], dup_a, "tn", BF16, nm + "dw_up_a")
    g["w_up_b"] = mm(s["hn2"], dup_b, "tn", BF16, nm + "dw_up_b")
    dhn2 = mm(dup_a, w["w_up_a"], "nt", F32, nm + "dhn2_a")
    dhn2 = mm(dup_b, w["w_up_b"], "nt", F32, nm + "dhn2_b", add=dhn2)

    def rms_bwd(i, n, x, gg, dy, dres):
        _, vjp = jax.vjp(_rms, x, gg)
        dx, dg = vjp(dy)
        return dres + dx, dg

    dh_mid, g["norm_ffn_g"] = rowwise(rms_bwd, [RI(s["h_mid"]), PA(w["norm_ffn_g"]), RI(dhn2), RI(dh_out)],
                                      [RO(D_MODEL, F32), AO(1, D_MODEL)], nm + "rms_ffn_bwd", lp, tm=WIDE_TILE)
    g["w_out"] = mm(s["cat"], dh_mid, "tn", BF16, nm + "dw_out")
    d_cat = mm(dh_mid, w["w_out"], "nt", F32, nm + "d_cat")
    u = s["u"]

    def mix_bwd(i, n, y_pre, z, o_sb, o_mla, g1, g2, g3, dcat):
        _, vjp = jax.vjp(_mix_out, y_pre, z, o_sb, o_mla, g1, g2, g3)
        return vjp(dcat)

    dy_pre, dz, do_sb, do_mla, g["ssd_norm_g"], g["sb_norm_g"], g["mla_norm_g"] = rowwise(
        mix_bwd, [RI(s["y_pre"]), RI(u, SSD_WIDTH, U_Z // SSD_WIDTH), RI(s["o_sb"]), RI(s["o_mla"]),
                  PA(w["ssd_norm_g"]), PA(w["sb_norm_g"]), PA(w["mla_norm_g"]), RI(d_cat)],
        [RO(SSD_WIDTH, F32), RO(SSD_WIDTH, BF16), RO(SB_WIDTH, F32), RO(256, F32),
         AO(1, SSD_WIDTH), AO(1, SB_WIDTH), AO(1, 256)], nm + "mix_out_bwd", lp, tm=WIDE_TILE)
    dxs, dbp, dcp, ddtp, pg = ssd_bwd(s["xbc_c"], u, w["dt_bias"], w["a_log"], w["d_skip"], s["hprev"], dy_pre, nm + "ssd_bwd")
    pg = pg.reshape(2, SUBLANES, LANES).sum(axis=0)
    g["dt_bias"], g["a_log"], g["d_skip"] = pg[0:1], pg[1:2], pg[2:3]

    def conv4_bwd(i, n, x, hl, cw, cb_, dxs_, dbp_, dcp_, ddtp_):
        pre = conv_fwd(x, hl, cw, i) + cb_
        d_out = jnp.concatenate([dxs_, dbp_, dcp_], axis=1)
        sg = jax.nn.sigmoid(pre)
        d_pre = d_out * (sg * (1.0 + pre * (1.0 - sg)))
        ddt = ddtp_[:, 0:128] + ddtp_[:, 128:256]
        return d_pre, ddt, conv_bwd_w(d_pre, x, hl, i, SSD_CONV), jnp.sum(d_pre, axis=0, keepdims=True)

    d_pre, ddt, g["ssd_conv_w"], g["ssd_conv_b"] = rowwise(
        conv4_bwd, [RI(u, SSD_XBC, 0), HP(u, SSD_XBC, 0), PA(w["ssd_conv_w"]), PA(w["ssd_conv_b"]),
                    RI(dxs), RI(dbp), RI(dcp), RI(ddtp)],
        [RO(SSD_XBC, F32), RO(LANES, BF16), AO(SSD_CONV, SSD_XBC), AO(1, SSD_XBC)], nm + "ssd_conv_bwd", lp, tm=WIDE_TILE)
    d_xbc = rowwise(lambda i, n, d, hn_, cw: conv_bwd_data(d, hn_, cw, i, n),
                    [RI(d_pre), HN(d_pre), PA(w["ssd_conv_w"])], [RO(SSD_XBC, BF16)], nm + "ssd_conv_t", lp, tm=WIDE_TILE)[0]
    dq_sb, dk_sb, dv_sb = sb_attn_bwd(s["qkv"], s["rs_sb"], do_sb, nm + "sb_bwd")
    dqkv = jnp.concatenate([dq_sb, dk_sb.astype(BF16), dv_sb.astype(BF16)], axis=1)
    dqc, dkc, dvv = mla_attn_bwd(s["qc"], s["kc"], s["vv"], s["o_mla"], s["lse"], do_mla, nm + "mla_bwd")

    def unpack(i, n, dqc_, dkc_, dvv_, cos, sin):
        dqr = dqc_[:, 128:256] + dqc_[:, 384:512]
        dqr = dqr * cos + rope_rot_t(dqr * sin)
        dkr = dkc_[:, 128:256] + dkc_[:, 384:512]
        dkr = dkr * cos + rope_rot_t(dkr * sin)
        dq = jnp.concatenate([dqc_[:, 0:128], dqc_[:, 256:384], dqr], axis=1) * MLA_SCALE
        dkv = jnp.concatenate([dkc_[:, 0:128], dkc_[:, 256:384], dvv_], axis=1)
        return dq, dkv, dkr

    dq, dkv, dkr4 = rowwise(unpack, [RI(dqc), RI(dkc), RI(dvv), RI(cs), RI(sn)],
                            [RO(384, BF16), RO(512, BF16), RO(LANES, BF16)], nm + "mla_unpack", lp, tm=WIDE_TILE)
    g["w_uq"] = mm(s["qn"], dq, "tn", F32, nm + "dw_uq")
    g["w_ukv"] = mm(s["kvn"], dkv, "tn", F32, nm + "dw_ukv")
    dqn = mm(dq, w["w_uq"], "nt", F32, nm + "dqn")
    dkvn = mm(dkv, w["w_ukv"], "nt", F32, nm + "dkvn")

    def mla_rms_bwd(i, n, qa, ckv, gq, gkv, dqn_, dkvn_):
        _, vjp = jax.vjp(lambda a, b, c, d: (_rms(a, c, MLA_Q_RANK), _rms(b, d)), qa, ckv, gq, gkv)
        return vjp((dqn_, dkvn_))

    dqa, dckv, g["q_norm_g"], g["kv_norm_g"] = rowwise(
        mla_rms_bwd, [RI(u, 256, U_QA // 256), RI(u, LANES, U_CKV // LANES), PA(w["q_norm_g"]), PA(w["kv_norm_g"]),
                      RI(dqn), RI(dkvn)],
        [RO(256, BF16), RO(LANES, BF16), AO(1, 256), AO(1, LANES)], nm + "mla_rms_bwd", lp, tm=WIDE_TILE)
    du = jnp.concatenate([d_xbc, dz, dqa, dckv, dkr4, ddt, jnp.zeros((lp, LANES), BF16)], axis=1)
    g["w_main"] = mm(s["hn"], du, "tn", F32, nm + "dw_main")
    g["w_sb"] = mm(s["hn"], dqkv, "tn", F32, nm + "dw_sb")
    dhn = mm(du, w["w_main"], "nt", F32, nm + "dhn_main")
    dhn = mm(dqkv, w["w_sb"], "nt", F32, nm + "dhn_sb", add=dhn)
    dh_in, g["norm_mix_g"] = rowwise(rms_bwd, [RI(s["h"]), PA(w["norm_mix_g"]), RI(dhn), RI(dh_mid)],
                                     [RO(D_MODEL, F32), AO(1, D_MODEL)], nm + "rms_mix_bwd", lp, tm=WIDE_TILE)
    return dh_in, g


_IN_CUTS = np.cumsum((512, 1024, 8, 256, 256, 256, 192, 128, 32))


def _pad_cols(a, n):
    return jnp.pad(a, ((0, 0), (0, n - a.shape[1])))


def prep_layer_weights(full, l):
    w_in = full["w_in"][l]
    c = _IN_CUTS
    z, xbc, dtr = w_in[:, :c[0]], w_in[:, c[0]:c[1]], w_in[:, c[1]:c[2]]
    q_sb, k_sb, v_sb = w_in[:, c[2]:c[3]], w_in[:, c[3]:c[4]], w_in[:, c[4]:c[5]]
    q_a, c_kv, k_r = w_in[:, c[5]:c[6]], w_in[:, c[6]:c[7]], w_in[:, c[7]:c[8]]
    w_main = jnp.concatenate([xbc, z, _pad_cols(q_a, 256), c_kv, k_r, k_r, k_r, k_r, _pad_cols(dtr, 256)], axis=1)
    assert w_main.shape[1] == U_MAIN
    row = lambda v, n=None: _pad_cols(v.reshape(1, -1).astype(F32), v.size if n is None else n)
    uq = full["mla_w_uq"][l].reshape(MLA_Q_RANK, 4, 96)
    w_uq = jnp.concatenate([uq[:, :, :64].reshape(MLA_Q_RANK, 256), uq[:, :, 64:].reshape(MLA_Q_RANK, 128)], axis=1)
    w_uq = jnp.pad(w_uq, ((0, 256 - MLA_Q_RANK), (0, 0)))
    ukv = full["mla_w_ukv"][l].reshape(MLA_KV_RANK, 4, 128)
    w_ukv = jnp.concatenate([ukv[:, :, :64].reshape(MLA_KV_RANK, 256), ukv[:, :, 64:].reshape(MLA_KV_RANK, 256)], axis=1)
    return dict(
        norm_mix_g=row(full["norm_mix_g"][l]), w_main=w_main, w_sb=jnp.concatenate([q_sb * SB_SCALE, k_sb, v_sb], axis=1),
        ssd_conv_w=full["ssd_conv_w"][l], ssd_conv_b=row(full["ssd_conv_b"][l]),
        dt_bias=row(full["ssd_dt_bias"][l], LANES), a_log=row(full["ssd_a_log"][l], LANES), d_skip=row(full["ssd_d"][l], LANES),
        ssd_norm_g=row(full["ssd_norm_g"][l]), sb_norm_g=row(full["sb_norm_g"][l]),
        q_norm_g=row(full["mla_q_norm_g"][l], 256), kv_norm_g=row(full["mla_kv_norm_g"][l]),
        w_uq=w_uq, w_ukv=w_ukv, mla_norm_g=row(full["mla_norm_g"][l]),
        w_out=full["w_out"][l], norm_ffn_g=row(full["norm_ffn_g"][l]),
        w_up_a=full["ffn_w_up"][l][:, :D_FF], w_up_b=full["ffn_w_up"][l][:, D_FF:],
        ffn_conv_w_a=full["ffn_conv_w"][l][:, :D_FF], ffn_conv_w_b=full["ffn_conv_w"][l][:, D_FF:],
        ffn_conv_b_a=row(full["ffn_conv_b"][l][:D_FF]), ffn_conv_b_b=row(full["ffn_conv_b"][l][D_FF:]),
        w_down=full["ffn_w_down"][l],
    )


def unprep_layer_grads(g):
    wm = g["w_main"]
    xbc, z = wm[:, U_XBC:U_XBC + 1024], wm[:, U_Z:U_Z + 512]
    q_a, c_kv = wm[:, U_QA:U_QA + MLA_Q_RANK], wm[:, U_CKV:U_CKV + 128]
    k_r = (wm[:, U_KR4:U_KR4 + 32] + wm[:, U_KR4 + 32:U_KR4 + 64] + wm[:, U_KR4 + 64:U_KR4 + 96] + wm[:, U_KR4 + 96:U_KR4 + 128])
    dtr = wm[:, U_DT:U_DT + SSD_HEADS]
    w_sb = g["w_sb"]
    w_in = jnp.concatenate([z, xbc, dtr, w_sb[:, :SB_WIDTH] * SB_SCALE, w_sb[:, SB_WIDTH:], q_a, c_kv, k_r], axis=1)
    guq = g["w_uq"][:MLA_Q_RANK]
    guq = jnp.concatenate([guq[:, :256].reshape(MLA_Q_RANK, 4, 64), guq[:, 256:].reshape(MLA_Q_RANK, 4, 32)], axis=2)
    gukv = g["w_ukv"]
    gukv = jnp.concatenate([gukv[:, :256].reshape(MLA_KV_RANK, 4, 64), gukv[:, 256:].reshape(MLA_KV_RANK, 4, 64)], axis=2)
    return dict(
        norm_mix_g=g["norm_mix_g"][0], w_in=w_in, ssd_conv_w=g["ssd_conv_w"], ssd_conv_b=g["ssd_conv_b"][0],
        ssd_dt_bias=g["dt_bias"][0, :SSD_HEADS], ssd_a_log=g["a_log"][0, :SSD_HEADS], ssd_d=g["d_skip"][0, :SSD_HEADS],
        ssd_norm_g=g["ssd_norm_g"][0], sb_norm_g=g["sb_norm_g"][0], mla_q_norm_g=g["q_norm_g"][0, :MLA_Q_RANK],
        mla_kv_norm_g=g["kv_norm_g"][0], mla_w_uq=guq.reshape(MLA_Q_RANK, 384), mla_w_ukv=gukv.reshape(MLA_KV_RANK, 512),
        mla_norm_g=g["mla_norm_g"][0], w_out=g["w_out"], norm_ffn_g=g["norm_ffn_g"][0],
        ffn_w_up=jnp.concatenate([g["w_up_a"], g["w_up_b"]], axis=1),
        ffn_conv_w=jnp.concatenate([g["ffn_conv_w_a"], g["ffn_conv_w_b"]], axis=1),
        ffn_conv_b=jnp.concatenate([g["ffn_conv_b_a"][0], g["ffn_conv_b_b"][0]], axis=0),
        ffn_w_down=g["w_down"],
    )


def rope_tables(lp):
    pos = jnp.arange(lp, dtype=F32)
    inv = 1.0 / (ROPE_BASE ** (jnp.arange(0, MLA_ROPE, 2, dtype=F32) / MLA_ROPE))
    ang = pos[:, None] * inv[None, :]
    ang = jnp.concatenate([ang, ang] * 4, axis=-1)
    return jnp.cos(ang), jnp.sin(ang)


def local_step(x_seq, target, full):
    seq = x_seq.shape[0]
    length = seq + N_META
    lp = -(-length // ROW_TILE) * ROW_TILE
    cs, sn = rope_tables(lp)
    h = jnp.concatenate([full["meta_tokens"].astype(F32), x_seq, jnp.zeros((lp - length, D_MODEL), F32)], axis=0)
    tgt = jnp.pad(target, ((N_META, lp - length), (0, 0)))
    ws = [prep_layer_weights(full, l) for l in range(DEPTH)]
    saved = []
    for l in range(DEPTH):
        h, s = layer_fwd(h, ws[l], cs, sn, l)
        saved.append(s)
    fg = full["final_norm_g"].reshape(1, D_MODEL).astype(F32)
    tm = min(WIDE_TILE, lp)

    def loss_fn(i, n, x, g, t):
        rows = _rows_iota(x) + i * tm
        valid = jnp.logical_and(rows >= N_META, rows < length)

        def f(x_, g_):
            err = jnp.where(valid, _rms(x_, g_) - t, 0.0)
            return 0.5 * jnp.sum(err * err) * (1.0 / D_MODEL)

        val, (dx, dg) = jax.value_and_grad(f, argnums=(0, 1))(x, g)
        return dx, jnp.full((1, LANES), val, F32), dg

    dh, loss_row, g_final = rowwise(loss_fn, [RI(h), PA(fg), RI(tgt)], [RO(D_MODEL, F32), AO(1, LANES), AO(1, D_MODEL)],
                                    "loss_head", lp, tm=WIDE_TILE)
    grads = {}
    per_layer = [None] * DEPTH
    for l in reversed(range(DEPTH)):
        dh, g = layer_bwd(dh, ws[l], saved[l], cs, sn, l)
        per_layer[l] = unprep_layer_grads(g)
    for k in per_layer[0]:
        grads[k] = jnp.stack([per_layer[l][k] for l in range(DEPTH)], axis=0)
    grads["final_norm_g"] = g_final[0]
    grads["meta_tokens"] = dh[:N_META]
    return loss_row[0, 0], dh[N_META:length], grads


_ANY = pl.BlockSpec(memory_space=pl.ANY)


def chip_exchange(srcs, modes, name):
    n = len(srcs)
    flips = ((1, 0), (0, 1), (1, 1))

    def body(*refs):
        ins, outs = refs[:n], refs[n:2 * n]
        send_sems, recv_sems, fwd_send_sems, fwd_recv_sems, loc_sems = refs[2 * n:]
        x, y, c = lax.axis_index("x"), lax.axis_index("y"), lax.axis_index("c")
        me = 2 * x + y
        waits, forwards = [], []
        for a in range(n):
            whole = modes[a] != "slab"
            cp = pltpu.make_async_copy(ins[a] if whole else ins[a].at[me], outs[a].at[me], loc_sems.at[a])
            cp.start()
            waits.append(cp.wait)
            half = ins[a].shape[0] // 2 if modes[a] == "bcast_split" else None
            for k, (fx, fy) in enumerate(flips):
                px = 1 - x if fx else x
                py = 1 - y if fy else y
                peer = 2 * px + py
                if half is None:
                    src = ins[a] if whole else ins[a].at[peer]
                    dst = outs[a].at[me]
                else:
                    src = ins[a].at[pl.ds(c * half, half)]
                    dst = outs[a].at[me, pl.ds(c * half, half)]
                rc = pltpu.make_async_remote_copy(src_ref=src, dst_ref=dst, send_sem=send_sems.at[a, k],
                                                  recv_sem=recv_sems.at[a, k], device_id=(px, py, c), device_id_type=MESH_ID)
                rc.start()
                if half is None:
                    waits.append(rc.wait)
                else:
                    waits.append(rc.wait_send)
                    landed = outs[a].at[peer, pl.ds(c * half, half)]
                    fw = pltpu.make_async_remote_copy(src_ref=landed, dst_ref=landed, send_sem=fwd_send_sems.at[a, k],
                                                      recv_sem=fwd_recv_sems.at[a, k], device_id=(x, y, 1 - c),
                                                      device_id_type=MESH_ID)
                    forwards.append((rc, fw))
        for rc, fw in forwards:
            rc.wait_recv()
            fw.start()
        for rc, fw in forwards:
            fw.wait()
        for w in waits:
            w()

    out_shape = [jax.ShapeDtypeStruct((N_CHIPS,) + (s.shape if m != "slab" else s.shape[1:]), s.dtype) for s, m in zip(srcs, modes)]
    return pl.pallas_call(
        body, name=name, in_specs=[_ANY] * n, out_specs=[_ANY] * n, out_shape=out_shape,
        scratch_shapes=[pltpu.SemaphoreType.DMA((n, 3)), pltpu.SemaphoreType.DMA((n, 3)), pltpu.SemaphoreType.DMA((n, 3)),
                        pltpu.SemaphoreType.DMA((n, 3)), pltpu.SemaphoreType.DMA((n,))],
    )(*srcs)


def _piece(ref, mode, k):
    if mode == "slab":
        return ref.at[k]
    if mode == "rows":
        rs = ref.shape[1] // N_CHIPS
        return ref.at[:, pl.ds(pl.multiple_of(k * rs, 16), rs), :]
    if mode == "cols":
        cs = ref.shape[2] // N_CHIPS
        return ref.at[:, :, pl.ds(pl.multiple_of(k * cs, LANES), cs)]
    return ref


def _piece_shape(shape, mode):
    if mode == "slab":
        return shape[1:]
    if mode == "rows":
        return (shape[0], shape[1] // N_CHIPS, shape[2])
    if mode == "cols":
        return (shape[0], shape[1], shape[2] // N_CHIPS)
    return shape


def grad_exchange(srcs, modes, name):
    n = len(srcs)
    flips = ((1, 0), (0, 1), (1, 1))

    def body(*refs):
        ins, outs = refs[:n], refs[n:2 * n]
        send_sems, recv_sems, fwd_send_sems, fwd_recv_sems, sib_send_sems, sib_recv_sems, loc_sems = refs[2 * n:]
        x, y, c = lax.axis_index("x"), lax.axis_index("y"), lax.axis_index("c")
        me = 2 * x + y
        sibling = (x, y, 1 - c)
        waits, forwards = [], []
        for a in range(n):
            mine = _piece(ins[a], modes[a], me)
            slot = outs[a].at[4 * c + me]
            cp = pltpu.make_async_copy(mine, slot, loc_sems.at[a])
            cp.start()
            sb = pltpu.make_async_remote_copy(src_ref=mine, dst_ref=slot, send_sem=sib_send_sems.at[a],
                                              recv_sem=sib_recv_sems.at[a], device_id=sibling, device_id_type=MESH_ID)
            sb.start()
            waits += [cp.wait, sb.wait]
            for k, (fx, fy) in enumerate(flips):
                px = 1 - x if fx else x
                py = 1 - y if fy else y
                peer = 2 * px + py
                rc = pltpu.make_async_remote_copy(src_ref=_piece(ins[a], modes[a], peer), dst_ref=slot,
                                                  send_sem=send_sems.at[a, k], recv_sem=recv_sems.at[a, k],
                                                  device_id=(px, py, c), device_id_type=MESH_ID)
                rc.start()
                landed = outs[a].at[4 * c + peer]
                fw = pltpu.make_async_remote_copy(src_ref=landed, dst_ref=landed, send_sem=fwd_send_sems.at[a, k],
                                                  recv_sem=fwd_recv_sems.at[a, k], device_id=sibling, device_id_type=MESH_ID)
                waits.append(rc.wait_send)
                forwards.append((rc, fw))
        for rc, fw in forwards:
            rc.wait_recv()
            fw.start()
        for rc, fw in forwards:
            fw.wait()
        for w in waits:
            w()

    out_shape = [jax.ShapeDtypeStruct((2 * N_CHIPS,) + tuple(_piece_shape(s.shape, m)), s.dtype) for s, m in zip(srcs, modes)]
    dma = pltpu.SemaphoreType.DMA
    return pl.pallas_call(
        body, name=name, in_specs=[_ANY] * n, out_specs=[_ANY] * n, out_shape=out_shape,
        scratch_shapes=[dma((n, 3)), dma((n, 3)), dma((n, 3)), dma((n, 3)), dma((n,)), dma((n,)), dma((n,))],
    )(*srcs)


WEIGHT_NAMES = ("meta_tokens", "norm_mix_g", "w_in", "ssd_conv_w", "ssd_conv_b", "ssd_dt_bias", "ssd_a_log", "ssd_d",
                "ssd_norm_g", "sb_norm_g", "mla_q_norm_g", "mla_kv_norm_g", "mla_w_uq", "mla_w_ukv", "mla_norm_g",
                "w_out", "norm_ffn_g", "ffn_w_up", "ffn_conv_w", "ffn_conv_b", "ffn_w_down", "final_norm_g")
SHARD_AXIS = {"meta_tokens": 1, "w_in": 2, "ssd_conv_w": 2, "mla_w_uq": 2, "mla_w_ukv": 2, "w_out": 1, "ffn_w_up": 2,
              "ffn_conv_w": 2, "ffn_w_down": 1}
SHARDED = tuple(n for n in WEIGHT_NAMES if n in SHARD_AXIS)
REPLICATED = tuple(n for n in WEIGHT_NAMES if n not in SHARD_AXIS)
GATHER_BF16 = ("w_in", "mla_w_uq", "mla_w_ukv", "w_out", "ffn_w_up", "ffn_w_down")
GATHER_F32 = ("meta_tokens", "ssd_conv_w", "ffn_conv_w")
PACK_ROWS = ROW_TILE


def pack(arrs, dtype):
    flat = jnp.concatenate([a.reshape(-1).astype(dtype) for a in arrs])
    per = PACK_ROWS * PACK_W
    total = -(-flat.size // per) * per
    return jnp.pad(flat, (0, total - flat.size)).reshape(total // PACK_W, PACK_W)


def unpack(buf, shapes):
    flat = buf.reshape(-1)
    out, off = [], 0
    for shp in shapes:
        size = int(np.prod(shp))
        out.append(flat[off:off + size].reshape(shp))
        off += size
    return out


def gather_weights(a):
    full = {n: a[n] for n in REPLICATED}
    bufs = [pack([a[n] for n in GATHER_BF16], BF16), pack([a[n] for n in GATHER_F32], F32)]
    got = chip_exchange(bufs, ("bcast_split", "bcast"), "gather_weights")
    for names, g in ((GATHER_BF16, got[0]), (GATHER_F32, got[1])):
        pieces = [unpack(g[k], [a[n].shape for n in names]) for k in range(N_CHIPS)]
        for idx, n in enumerate(names):
            full[n] = jnp.concatenate([pieces[k][idx] for k in range(N_CHIPS)], axis=SHARD_AXIS[n])
    return full


BIG = ("w_in", "w_out", "ffn_w_up", "ffn_w_down")
BIG_MODE = {"w_in": "slab", "w_out": "rows", "ffn_w_up": "cols", "ffn_w_down": "rows"}
SMALL_SHARDED = tuple(n for n in SHARDED if n not in BIG)
ADAM_TILE = 128


def _adamw(i, n, *vals):
    parts, (w, m, v) = vals[:2 * N_CHIPS], vals[2 * N_CHIPS:]
    g = parts[0].astype(F32)
    for p in parts[1:]:
        g = g + p.astype(F32)
    m = ADAM_B1 * m + (1.0 - ADAM_B1) * g
    v = ADAM_B2 * v + (1.0 - ADAM_B2) * jnp.square(g)
    m_hat = m / (1.0 - ADAM_B1 ** ADAM_STEP)
    v_hat = v / (1.0 - ADAM_B2 ** ADAM_STEP)
    delta = -ADAM_LR * (m_hat / (jnp.sqrt(v_hat) + ADAM_EPS) + ADAM_WD * w)
    return g, delta, m, v


def _adamw_call(got, w, m, v, name):
    rows, width = w.shape
    flat = got.reshape(2 * N_CHIPS * rows, width)
    blk = rows // ADAM_TILE
    ins = [RI(flat, rblk=k * blk) for k in range(2 * N_CHIPS)] + [RI(w), RI(m), RI(v)]
    return rowwise(_adamw, ins, [RO(width, F32)] * 4, name, rows, tm=ADAM_TILE)


def reduce_and_update(a, grads):
    srcs, modes = [], []
    for n in BIG:
        g = grads[n].astype(BF16)
        if n == "w_in":
            cs = a[n].shape[2]
            g = g.reshape(DEPTH, D_MODEL, N_CHIPS, cs).transpose(2, 0, 1, 3)
        srcs.append(g)
        modes.append(BIG_MODE[n])
    slabs = []
    for k in range(N_CHIPS):
        parts = []
        for n in SMALL_SHARDED:
            ax = SHARD_AXIS[n]
            size = a[n].shape[ax]
            parts.append(lax.slice_in_dim(grads[n], k * size, (k + 1) * size, axis=ax))
        slabs.append(pack(parts, BF16))
    srcs += [jnp.stack(slabs, axis=0), pack([grads[n] for n in REPLICATED], F32)]
    modes += ["slab", "bcast"]
    got = grad_exchange(srcs, modes, "exchange_grads")
    outs = {}
    kinds = ("grad", "delta", "new_m", "new_v")
    for n, g8 in zip(BIG, got):
        shp = a[n].shape
        rows = shp[0] * shp[1]
        flat = lambda t: t.reshape(rows, shp[2])
        res = _adamw_call(g8.reshape(2 * N_CHIPS, rows, shp[2]), flat(a[n]), flat(a["m_" + n]), flat(a["v_" + n]), "adamw_" + n)
        for kind, val in zip(kinds, res):
            outs[(kind, n)] = val.reshape(shp)
    for tag, names, g8 in (("small", SMALL_SHARDED, got[len(BIG)]), ("rep", REPLICATED, got[len(BIG) + 1])):
        shapes = [a[n].shape for n in names]
        packed = [pack([a[pre + n] for n in names], F32) for pre in ("", "m_", "v_")]
        res = _adamw_call(g8, *packed, "adamw_" + tag)
        for kind, buf in zip(kinds, res):
            for n, val in zip(names, unpack(buf, shapes)):
                outs[(kind, n)] = val
    return outs


INPUT_NAMES = ("x",) + WEIGHT_NAMES + ("loss_target",) + tuple("m_" + n for n in WEIGHT_NAMES) + tuple("v_" + n for n in WEIGHT_NAMES)


def kernel(x, meta_tokens, norm_mix_g, w_in, ssd_conv_w, ssd_conv_b, ssd_dt_bias, ssd_a_log, ssd_d, ssd_norm_g, sb_norm_g, mla_q_norm_g, mla_kv_norm_g, mla_w_uq, mla_w_ukv, mla_norm_g, w_out, norm_ffn_g, ffn_w_up, ffn_conv_w, ffn_conv_b, ffn_w_down, final_norm_g, loss_target, m_meta_tokens, m_norm_mix_g, m_w_in, m_ssd_conv_w, m_ssd_conv_b, m_ssd_dt_bias, m_ssd_a_log, m_ssd_d, m_ssd_norm_g, m_sb_norm_g, m_mla_q_norm_g, m_mla_kv_norm_g, m_mla_w_uq, m_mla_w_ukv, m_mla_norm_g, m_w_out, m_norm_ffn_g, m_ffn_w_up, m_ffn_conv_w, m_ffn_conv_b, m_ffn_w_down, m_final_norm_g, v_meta_tokens, v_norm_mix_g, v_w_in, v_ssd_conv_w, v_ssd_conv_b, v_ssd_dt_bias, v_ssd_a_log, v_ssd_d, v_ssd_norm_g, v_sb_norm_g, v_mla_q_norm_g, v_mla_kv_norm_g, v_mla_w_uq, v_mla_w_ukv, v_mla_norm_g, v_w_out, v_norm_ffn_g, v_ffn_w_up, v_ffn_conv_w, v_ffn_conv_b, v_ffn_w_down, v_final_norm_g):
    args = (x, meta_tokens, norm_mix_g, w_in, ssd_conv_w, ssd_conv_b, ssd_dt_bias, ssd_a_log, ssd_d, ssd_norm_g, sb_norm_g, mla_q_norm_g, mla_kv_norm_g, mla_w_uq, mla_w_ukv, mla_norm_g, w_out, norm_ffn_g, ffn_w_up, ffn_conv_w, ffn_conv_b, ffn_w_down, final_norm_g, loss_target, m_meta_tokens, m_norm_mix_g, m_w_in, m_ssd_conv_w, m_ssd_conv_b, m_ssd_dt_bias, m_ssd_a_log, m_ssd_d, m_ssd_norm_g, m_sb_norm_g, m_mla_q_norm_g, m_mla_kv_norm_g, m_mla_w_uq, m_mla_w_ukv, m_mla_norm_g, m_w_out, m_norm_ffn_g, m_ffn_w_up, m_ffn_conv_w, m_ffn_conv_b, m_ffn_w_down, m_final_norm_g, v_meta_tokens, v_norm_mix_g, v_w_in, v_ssd_conv_w, v_ssd_conv_b, v_ssd_dt_bias, v_ssd_a_log, v_ssd_d, v_ssd_norm_g, v_sb_norm_g, v_mla_q_norm_g, v_mla_kv_norm_g, v_mla_w_uq, v_mla_w_ukv, v_mla_norm_g, v_w_out, v_norm_ffn_g, v_ffn_w_up, v_ffn_conv_w, v_ffn_conv_b, v_ffn_w_down, v_final_norm_g)
    a = dict(zip(INPUT_NAMES, args, strict=True))
    full = gather_weights(a)
    loss, grad_x, grads = local_step(a["x"][0], a["loss_target"][0], full)
    loss = lax.psum(loss, ("x", "y", "c"))
    outs = reduce_and_update(a, grads)
    result = [loss, grad_x[None]]
    for kind in ("grad", "delta", "new_m", "new_v"):
        result += [outs[(kind, n)] for n in WEIGHT_NAMES]
    return tuple(result)
```

```python
import functools
import math

import numpy as np
import jax
import jax.numpy as jnp
from jax import lax
from jax.experimental import pallas as pl
from jax.experimental.pallas import tpu as pltpu

F32 = jnp.float32
BF16 = jnp.bfloat16
MESH_ID = pl.DeviceIdType.MESH

D_MODEL = 1024
DEPTH = 2
N_META = 16
EPS = 1e-6
SSD_HEADS = 8
SSD_WIDTH = 512
SSD_XBC = 1024
SSD_CONV = 4
SB_WIDTH = 256
SB_SCALE = 64 ** -0.5
MLA_Q_RANK = 192
MLA_KV_RANK = 128
MLA_ROPE = 32
MLA_SCALE = 96 ** -0.5
ROPE_BASE = 10000.0
D_FF = 2816
FFN_CONV = 3
IN_COLS = 2664
N_CHIPS = 4

ADAM_LR = 0.001
ADAM_B1 = 0.9
ADAM_B2 = 0.999
ADAM_EPS = 1e-08
ADAM_WD = 0.01
ADAM_STEP = 10

LANES = 128
SUBLANES = 8
ROW_TILE = 256
KEY_UNROLL = 4
WIDE_TILE = 768
VMEM_LIMIT = 56 * 1024 * 1024
PACK_W = 1024

U_XBC, U_Z, U_QA, U_CKV, U_KR4, U_DT, U_MAIN = 0, 1024, 1536, 1792, 1920, 2048, 2304
NEG = -1e30


def _cp(*sem):
    return pltpu.CompilerParams(dimension_semantics=sem if sem else None, vmem_limit_bytes=VMEM_LIMIT)


def _pick(dim, pref):
    if dim <= pref:
        return dim
    best = None
    for t in range(LANES, pref + 1, LANES):
        if dim % t == 0:
            best = t
    assert best is not None, (dim, pref)
    return best


def _dot(a, b, dims="nn", precision=None):
    dn = {"nn": (((1,), (0,)), ((), ())), "nt": (((1,), (1,)), ((), ())), "tn": (((0,), (0,)), ((), ()))}[dims]
    return lax.dot_general(a, b, dn, preferred_element_type=F32, precision=precision)


def _tri_dot(tri, x):
    hi = x.astype(BF16)
    r1 = x - hi.astype(F32)
    mid = r1.astype(BF16)
    lo = (r1 - mid.astype(F32)).astype(BF16)
    t = tri.astype(BF16)
    return _dot(t, hi) + _dot(t, mid) + _dot(t, lo)


def _softplus(x):
    return jnp.maximum(x, 0.0) + jnp.log1p(jnp.exp(-jnp.abs(x)))


def _silu(x):
    return x * jax.nn.sigmoid(x)


def _rms(x, g, n=None):
    n = x.shape[-1] if n is None else n
    ms = jnp.sum(x * x, axis=-1, keepdims=True) * (1.0 / n)
    return x * lax.rsqrt(ms + EPS) * g


def mm(a, b, dims, out_dtype, name, add=None, tm=None, tn=None, tk=None):
    if dims == "nn":
        (m, k), (k2, n) = a.shape, b.shape
    elif dims == "nt":
        (m, k), (n, k2) = a.shape, b.shape
    else:
        (k, m), (k2, n) = a.shape, b.shape
    assert k == k2, (a.shape, b.shape, dims)
    if dims == "tn":
        tm, tn, tk = _pick(m, tm or 1408), _pick(n, tn or 1408), _pick(k, tk or 1408)
    else:
        tm, tn, tk = _pick(m, tm or (1408 if k <= 2304 else 768)), _pick(n, tn or 1408), _pick(k, tk or 2816)
    nk = k // tk
    if dims == "tn":
        a_spec = pl.BlockSpec((tk, tm), lambda j, i, kk: (kk, i))
    else:
        a_spec = pl.BlockSpec((tm, tk), lambda j, i, kk: (i, kk))
    if dims == "nt":
        b_spec = pl.BlockSpec((tn, tk), lambda j, i, kk: (j, kk))
    else:
        b_spec = pl.BlockSpec((tk, tn), lambda j, i, kk: (kk, j))
    o_spec = pl.BlockSpec((tm, tn), lambda j, i, kk: (i, j))
    has_add = add is not None

    def body(*refs):
        a_ref, b_ref = refs[0], refs[1]
        add_ref = refs[2] if has_add else None
        o_ref = refs[3] if has_add else refs[2]
        part = _dot(a_ref[...].astype(BF16), b_ref[...].astype(BF16), dims)

        def finish(r):
            if has_add:
                r = r + add_ref[...].astype(F32)
            o_ref[...] = r.astype(o_ref.dtype)

        if nk == 1:
            finish(part)
            return
        acc_ref = refs[-1]
        kk = pl.program_id(2)

        @pl.when(kk == 0)
        def _():
            acc_ref[...] = part

        @pl.when(jnp.logical_and(kk > 0, kk < nk - 1))
        def _():
            acc_ref[...] += part

        @pl.when(kk == nk - 1)
        def _():
            finish(acc_ref[...] + part)

    in_specs = [a_spec, b_spec] + ([o_spec] if has_add else [])
    args = (a, b) + ((add,) if has_add else ())
    return pl.pallas_call(
        body, name=name, grid=(n // tn, m // tm, nk),
        in_specs=in_specs, out_specs=o_spec,
        out_shape=jax.ShapeDtypeStruct((m, n), out_dtype),
        scratch_shapes=[pltpu.VMEM((tm, tn), F32)] if nk > 1 else [],
        compiler_params=_cp("parallel", "parallel", "arbitrary"),
    )(*args)


def RI(arr, width=None, cidx=0, cv=False, rblk=0):
    return ("row" if rblk == 0 else ("row", rblk), arr, arr.shape[1] if width is None else width, cidx, cv)


def HP(arr, width=None, cidx=0, cv=False):
    return ("prev", arr, arr.shape[1] if width is None else width, cidx, cv)


def HN(arr, width=None, cidx=0, cv=False):
    return ("next", arr, arr.shape[1] if width is None else width, cidx, cv)


def PA(arr, width=None, cidx=0, cv=False):
    return ("par", arr, arr.shape[1] if width is None else width, cidx, cv)


def RO(ncols, dtype, width=None, cv=False):
    return ("row", ncols, dtype, ncols if width is None else width, cv)


def AO(nrows, ncols, width=None, cv=False):
    return ("acc", (nrows, ncols), F32, ncols if width is None else width, cv)


def rowwise(fn, ins, outs, name, rows, tm=ROW_TILE, ncol=1):
    tm = min(tm, rows)
    assert rows % tm == 0
    nrow = rows // tm
    hb = tm // SUBLANES
    last_hb = rows // SUBLANES - 1
    in_specs, args = [], []
    for kind, arr, width, cidx, cv in ins:
        def cmap(j, cidx=cidx, cv=cv):
            return cidx + j if cv else cidx
        if kind == "row":
            spec = pl.BlockSpec((tm, width), lambda j, i, cmap=cmap: (i, cmap(j)))
        elif isinstance(kind, tuple):
            spec = pl.BlockSpec((tm, width), lambda j, i, cmap=cmap, rblk=kind[1]: (i + rblk, cmap(j)))
        elif kind == "prev":
            spec = pl.BlockSpec((SUBLANES, width), lambda j, i, cmap=cmap: (jnp.maximum(i * hb - 1, 0), cmap(j)))
        elif kind == "next":
            spec = pl.BlockSpec((SUBLANES, width), lambda j, i, cmap=cmap: (jnp.minimum((i + 1) * hb, last_hb), cmap(j)))
        else:
            spec = pl.BlockSpec((arr.shape[0], width), lambda j, i, cmap=cmap: (0, cmap(j)))
        in_specs.append(spec)
        args.append(arr)
    out_specs, out_shapes, acc_cv = [], [], []
    for kind, shp, dtype, width, cv in outs:
        if kind == "row":
            out_specs.append(pl.BlockSpec((tm, width), lambda j, i, cv=cv: (i, j if cv else 0)))
            out_shapes.append(jax.ShapeDtypeStruct((rows, shp), dtype))
            acc_cv.append(None)
        else:
            out_specs.append(pl.BlockSpec((shp[0], width), lambda j, i, cv=cv: (0, j if cv else 0)))
            out_shapes.append(jax.ShapeDtypeStruct(shp, dtype))
            acc_cv.append(cv)
    n_in = len(ins)

    def body(*refs):
        j = pl.program_id(0)
        i = pl.program_id(1)
        vals = fn(i, nrow, *[r[...] for r in refs[:n_in]])
        if not isinstance(vals, (tuple, list)):
            vals = (vals,)
        for o_ref, v, cv in zip(refs[n_in:], vals, acc_cv):
            if cv is None:
                o_ref[...] = v.astype(o_ref.dtype)
            else:
                first = (i == 0) if cv else jnp.logical_and(i == 0, j == 0)

                @pl.when(first)
                def _(o_ref=o_ref, v=v):
                    o_ref[...] = v.astype(o_ref.dtype)

                @pl.when(jnp.logical_not(first))
                def _(o_ref=o_ref, v=v):
                    o_ref[...] += v.astype(o_ref.dtype)

    res = pl.pallas_call(
        body, name=name, grid=(ncol, nrow), in_specs=in_specs, out_specs=out_specs, out_shape=out_shapes,
        compiler_params=_cp("arbitrary", "arbitrary"),
    )(*args)
    return res


def _rows_iota(x):
    return lax.broadcasted_iota(jnp.int32, x.shape, 0)


def shift_down(x, halo, s):
    if s == 0:
        return x
    tm = x.shape[0]
    top = pltpu.roll(halo, s, 0)
    if tm > SUBLANES:
        top = jnp.concatenate([top, jnp.zeros((tm - SUBLANES, x.shape[1]), x.dtype)], axis=0)
    return jnp.where(_rows_iota(x) < s, top, pltpu.roll(x, s, 0))


def shift_up(x, halo, s):
    if s == 0:
        return x
    tm = x.shape[0]
    bot = pltpu.roll(halo, SUBLANES - s, 0)
    if tm > SUBLANES:
        bot = jnp.concatenate([jnp.zeros((tm - SUBLANES, x.shape[1]), x.dtype), bot], axis=0)
    return jnp.where(_rows_iota(x) >= tm - s, bot, pltpu.roll(x, tm - s, 0))


def conv_fwd(x, halo, w, i):
    kw = w.shape[0]
    halo = jnp.where(i == 0, 0.0, halo)
    out = None
    for k in range(kw):
        term = w[k:k + 1, :] * shift_down(x, halo, kw - 1 - k)
        out = term if out is None else out + term
    return out


def conv_bwd_data(dy, halo_next, w, i, n):
    kw = w.shape[0]
    halo_next = jnp.where(i == n - 1, 0.0, halo_next)
    out = None
    for k in range(kw):
        term = w[k:k + 1, :] * shift_up(dy, halo_next, kw - 1 - k)
        out = term if out is None else out + term
    return out


def conv_bwd_w(dy, x, halo, i, kw):
    halo = jnp.where(i == 0, 0.0, halo)
    rows = [jnp.sum(dy * shift_down(x, halo, kw - 1 - k), axis=0, keepdims=True) for k in range(kw)]
    return jnp.concatenate(rows, axis=0)


def _lane(shape):
    return lax.broadcasted_iota(jnp.int32, shape, 1)


def rope_rot(x):
    lane = _lane(x.shape) % MLA_ROPE
    return jnp.where(lane < MLA_ROPE // 2, -pltpu.roll(x, LANES - MLA_ROPE // 2, 1), pltpu.roll(x, MLA_ROPE // 2, 1))


def rope_rot_t(g):
    lane = _lane(g.shape) % MLA_ROPE
    return jnp.where(lane < MLA_ROPE // 2, pltpu.roll(g, LANES - MLA_ROPE // 2, 1), -pltpu.roll(g, MLA_ROPE // 2, 1))


def _ssd_common(g, xs, dt_raw, bias, alog, q):
    lane = _lane((q, LANES))
    pre = dt_raw + bias
    dt = jnp.where(lane < SSD_HEADS, _softplus(pre), 0.0)
    a_row = -jnp.exp(alog)
    d_a = dt * a_row
    ri = lax.broadcasted_iota(jnp.int32, (q, q), 0)
    ci = lax.broadcasted_iota(jnp.int32, (q, q), 1)
    causal = ri >= ci
    acs = _tri_dot(causal, d_a)
    acs_t = acs.T
    subl = lax.broadcasted_iota(jnp.int32, (LANES, q), 0)
    heads = [4 * g + i for i in range(4)]
    lo = lane < 64

    def col(arr, h):
        return jnp.sum(jnp.where(lane == h, arr, 0.0), axis=1, keepdims=True)

    def lanes4(v):
        m = lo if v[0].shape[0] == q else lo[0:1, :]
        return jnp.concatenate([jnp.where(m, v[0], v[1]), jnp.where(m, v[2], v[3])], axis=1)

    cols = [col(acs, h) for h in heads]
    rows = [jnp.sum(jnp.where(subl == h, acs_t, 0.0), axis=0, keepdims=True) for h in heads]
    tots = [c_[q - 1:q, :] for c_ in cols]
    acs4 = lanes4(cols)
    dt4 = lanes4([col(dt, h) for h in heads])
    lms = [jnp.exp(jnp.where(causal, cols[i] - rows[i], NEG)) for i in range(4)]
    lane4 = _lane((q, 2 * LANES))
    hm = [jnp.logical_and(lane4 >= 64 * i, lane4 < 64 * (i + 1)) for i in range(4)]
    return dict(lane=lane, lo=lo, pre=pre, dt=dt, a_row=a_row, heads=heads, tots=tots, lms=lms, ri=ri, ci=ci, hm=hm,
                lanes4=lanes4, eacs=jnp.exp(acs4), dte=jnp.exp(lanes4(tots) - acs4), dt4=dt4, x=xs * dt4)


def _pick_lane(row_arr, h):
    return jnp.sum(jnp.where(_lane(row_arr.shape) == h, row_arr, 0.0), axis=1, keepdims=True)


def _etot(tots):
    sub = lax.broadcasted_iota(jnp.int32, (2 * LANES, LANES), 0)
    e = [jnp.exp(t) for t in tots]
    return jnp.where(sub < 64, e[0], jnp.where(sub < 128, e[1], jnp.where(sub < 192, e[2], e[3]))), e


def _half(arr, i, lo):
    slab = arr[:, LANES * (i // 2):LANES * (i // 2 + 1)]
    return jnp.where(lo, slab, 0.0) if i % 2 == 0 else jnp.where(lo, 0.0, slab)


def ssd_fwd(xbc_c, u_main, bias_row, alog_row, d_row, name):
    lp = xbc_c.shape[0]
    q = min(ROW_TILE, lp)
    nc = lp // q
    dt_blk = U_DT // LANES

    def body(xs_ref, b_ref, c_ref, dt_ref, bias_ref, alog_ref, d_ref, y_ref, hp_ref, h_scr):
        c = pl.program_id(0)

        @pl.when(c == 0)
        def _():
            h_scr[...] = jnp.zeros_like(h_scr)

        ss, bbs, cbs, xss = [], [], [], []
        for g in range(2):
            xs = xs_ref[:, 2 * LANES * g:2 * LANES * (g + 1)]
            xss.append(xs)
            bbs.append(b_ref[:, LANES * g:LANES * (g + 1)].astype(BF16))
            cbs.append(c_ref[:, LANES * g:LANES * (g + 1)].astype(BF16))
            ss.append(_ssd_common(g, xs, dt_ref[...], bias_ref[...], alog_ref[...], q))
        gmats = [_dot(cbs[g], bbs[g], "nt") for g in range(2)]
        ms = [[(gmats[g] * ss[g]["lms"][i]).astype(BF16) for i in range(4)] for g in range(2)]
        xjs = [[_half(ss[g]["x"], i, ss[g]["lo"]).astype(BF16) for i in range(4)] for g in range(2)]
        ys = [[_dot(ms[g][i], xjs[g][i]) for i in range(4)] for g in range(2)]
        hps = [h_scr[g] for g in range(2)]
        yoffs = [_dot(cbs[g], hps[g].astype(BF16), "nt") * ss[g]["eacs"] for g in range(2)]
        upd = [_dot((ss[g]["x"] * ss[g]["dte"]).astype(BF16), bbs[g], "tn") for g in range(2)]
        outs = []
        for g in range(2):
            s = ss[g]
            hp_ref[g] = hps[g]
            d4 = s["lanes4"]([_pick_lane(d_ref[...], h) for h in s["heads"]])
            outs.append(jnp.concatenate([ys[g][0] + ys[g][1], ys[g][2] + ys[g][3]], axis=1) + yoffs[g] + d4 * xss[g])
            etot, _ = _etot(s["tots"])
            h_scr[g] = hps[g] * etot + upd[g]
        y_ref[...] = jnp.concatenate(outs, axis=1)

    in_specs = [
        pl.BlockSpec((q, 4 * LANES), lambda c: (c, 0)),
        pl.BlockSpec((q, 2 * LANES), lambda c: (c, 2)),
        pl.BlockSpec((q, 2 * LANES), lambda c: (c, 3)),
        pl.BlockSpec((q, LANES), lambda c: (c, dt_blk)),
        pl.BlockSpec((1, LANES), lambda c: (0, 0)),
        pl.BlockSpec((1, LANES), lambda c: (0, 0)),
        pl.BlockSpec((1, LANES), lambda c: (0, 0)),
    ]
    out_specs = [
        pl.BlockSpec((q, 4 * LANES), lambda c: (c, 0)),
        pl.BlockSpec((2, None, 2 * LANES, LANES), lambda c: (0, c, 0, 0)),
    ]
    return pl.pallas_call(
        body, name=name, grid=(nc,), in_specs=in_specs, out_specs=out_specs,
        out_shape=[jax.ShapeDtypeStruct((lp, SSD_WIDTH), F32), jax.ShapeDtypeStruct((2, nc, 2 * LANES, LANES), F32)],
        scratch_shapes=[pltpu.VMEM((2, 2 * LANES, LANES), F32)],
        compiler_params=_cp("arbitrary"),
    )(xbc_c, xbc_c, xbc_c, u_main, bias_row, alog_row, d_row)


def ssd_bwd(xbc_c, u_main, bias_row, alog_row, d_row, hprev, dy, name):
    lp = xbc_c.shape[0]
    q = min(ROW_TILE, lp)
    nc = lp // q
    dt_blk = U_DT // LANES

    def body(xs_ref, b_ref, c_ref, dt_ref, bias_ref, alog_ref, d_ref, hp_ref, dy_ref,
             dxs_ref, db_ref, dc_ref, ddt_ref, pg_ref, dh_scr):
        g = pl.program_id(0)
        cc = pl.program_id(1)

        @pl.when(cc == 0)
        def _():
            dh_scr[...] = jnp.zeros_like(dh_scr)
            pg_ref[...] = jnp.zeros_like(pg_ref)

        xs = xs_ref[...]
        bb = b_ref[...].astype(BF16)
        cb_ = c_ref[...].astype(BF16)
        s = _ssd_common(g, xs, dt_ref[...], bias_ref[...], alog_ref[...], q)
        lane, lo, x, hm, heads = s["lane"], s["lo"], s["x"], s["hm"], s["heads"]
        d_y = dy_ref[...]
        hp = hp_ref[...]
        hpb = hp.astype(BF16)
        dhn = dh_scr[...]
        dhnb = dhn.astype(BF16)
        xd = x * s["dte"]
        gmat = _dot(cb_, bb, "nt")
        m32s = [gmat * s["lms"][i] for i in range(4)]
        xjs = [_half(x, i, lo).astype(BF16) for i in range(4)]
        dyjs = [_half(d_y, i, lo).astype(BF16) for i in range(4)]
        dxparts = [_dot(m32s[i].astype(BF16), dyjs[i], "tn") for i in range(4)]
        dms = [_dot(dyjs[i], xjs[i], "nt") for i in range(4)]
        dg = dms[0] * s["lms"][0] + dms[1] * s["lms"][1] + dms[2] * s["lms"][2] + dms[3] * s["lms"][3]
        wms = [dms[i] * m32s[i] for i in range(4)]
        row_part = [jnp.sum(wm, axis=1, keepdims=True) for wm in wms]
        col_part = [jnp.sum(wm, axis=0, keepdims=True) for wm in wms]
        dgb = dg.astype(BF16)
        yoff = _dot(cb_, hpb, "nt") * s["eacs"]
        d_t = (d_y * s["eacs"]).astype(BF16)
        d_c = _dot(dgb, bb) + _dot(d_t, hpb)
        d_hp = _dot(d_t, cb_, "tn")
        dxd = _dot(bb, dhnb, "nt")
        d_b = _dot(dgb, cb_, "tn") + _dot(xd.astype(BF16), dhnb)
        d_x = jnp.concatenate([dxparts[0] + dxparts[1], dxparts[2] + dxparts[3]], axis=1) + dxd * s["dte"]
        r = dxd * xd
        a_terms = d_y * yoff - r

        def hsum(arr):
            return [jnp.sum(jnp.where(hm[i], arr, 0.0), axis=1, keepdims=True) for i in range(4)]

        dacs, rs = hsum(a_terms), hsum(r)
        hh = dhn * hp
        sub = lax.broadcasted_iota(jnp.int32, (2 * LANES, LANES), 0)
        hsums = [jnp.sum(jnp.where(jnp.logical_and(sub >= 64 * i, sub < 64 * (i + 1)), hh, 0.0), keepdims=True) for i in range(4)]
        last = lax.broadcasted_iota(jnp.int32, (q, 1), 0) == q - 1
        etot, etots = _etot(s["tots"])
        ddacs = jnp.zeros((q, LANES), F32)
        for i, h in enumerate(heads):
            dtot = jnp.sum(rs[i], keepdims=True) + hsums[i] * etots[i]
            ddacs = ddacs + jnp.where(lane == h, dacs[i] + row_part[i] + jnp.where(last, dtot, 0.0), 0.0)
        subl = lax.broadcasted_iota(jnp.int32, (LANES, q), 0)
        cols_t = jnp.zeros((LANES, q), F32)
        for i, h in enumerate(heads):
            cols_t = cols_t + jnp.where(subl == h, col_part[i], 0.0)
        ddacs = ddacs - cols_t.T
        da = _tri_dot(s["ri"] <= s["ci"], ddacs)
        ddt_own = hsum(d_x * xs)
        ddt = da * s["a_row"]
        for i, h in enumerate(heads):
            ddt = ddt + jnp.where(lane == h, ddt_own[i], 0.0)
        draw = ddt * jax.nn.sigmoid(s["pre"])
        ddt_ref[...] = draw
        d4 = s["lanes4"]([_pick_lane(d_ref[...], h) for h in heads])
        dxs_ref[...] = d4 * d_y + d_x * s["dt4"]
        db_ref[...] = d_b
        dc_ref[...] = d_c
        dds = hsum(d_y * xs)
        lane1 = lane[0:1, :]
        dd_row = jnp.zeros((1, LANES), F32)
        for i, h in enumerate(heads):
            dd_row = dd_row + jnp.where(lane1 == h, jnp.sum(dds[i], keepdims=True), 0.0)
        dbias_row = jnp.sum(draw, axis=0, keepdims=True)
        dalog_row = jnp.sum(da * s["dt"], axis=0, keepdims=True) * s["a_row"]
        sub8 = lax.broadcasted_iota(jnp.int32, (SUBLANES, LANES), 0)
        pg_ref[...] += (jnp.where(sub8 == 0, dbias_row, 0.0) + jnp.where(sub8 == 1, dalog_row, 0.0)
                        + jnp.where(sub8 == 2, dd_row, 0.0))
        dh_scr[...] = d_hp + etot * dhn

    rc = lambda c: nc - 1 - c
    in_specs = [
        pl.BlockSpec((q, 2 * LANES), lambda g, c: (rc(c), g)),
        pl.BlockSpec((q, LANES), lambda g, c: (rc(c), 4 + g)),
        pl.BlockSpec((q, LANES), lambda g, c: (rc(c), 6 + g)),
        pl.BlockSpec((q, LANES), lambda g, c: (rc(c), dt_blk)),
        pl.BlockSpec((1, LANES), lambda g, c: (0, 0)),
        pl.BlockSpec((1, LANES), lambda g, c: (0, 0)),
        pl.BlockSpec((1, LANES), lambda g, c: (0, 0)),
        pl.BlockSpec((None, None, 2 * LANES, LANES), lambda g, c: (g, rc(c), 0, 0)),
        pl.BlockSpec((q, 2 * LANES), lambda g, c: (rc(c), g)),
    ]
    out_specs = [
        pl.BlockSpec((q, 2 * LANES), lambda g, c: (rc(c), g)),
        pl.BlockSpec((q, LANES), lambda g, c: (rc(c), g)),
        pl.BlockSpec((q, LANES), lambda g, c: (rc(c), g)),
        pl.BlockSpec((q, LANES), lambda g, c: (rc(c), g)),
        pl.BlockSpec((SUBLANES, LANES), lambda g, c: (g, 0)),
    ]
    per_group = jax.ShapeDtypeStruct((lp, 2 * LANES), F32)
    return pl.pallas_call(
        body, name=name, grid=(2, nc), in_specs=in_specs, out_specs=out_specs,
        out_shape=[jax.ShapeDtypeStruct((lp, SSD_WIDTH), F32), per_group, per_group, per_group,
                   jax.ShapeDtypeStruct((2 * SUBLANES, LANES), F32)],
        scratch_shapes=[pltpu.VMEM((2 * LANES, LANES), F32)],
        compiler_params=_cp("arbitrary", "arbitrary"),
    )(xbc_c, xbc_c, xbc_c, u_main, bias_row, alog_row, d_row, hprev, dy)


def _sb_blocks(qs, ks, r_runs, masked, bq, after_scores=None):
    ri = lax.broadcasted_iota(jnp.int32, (bq, bq), 0)
    ci = lax.broadcasted_iota(jnp.int32, (bq, bq), 1)
    tri_after = (ri > ci).astype(BF16)
    zs = [_dot(qj, kj, "nt") for qj, kj in zip(qs, ks)]
    extra = after_scores() if after_scores is not None else None
    sigs, ubs = [], []
    for z in zs:
        zb = z.astype(BF16)
        u = -(jnp.maximum(zb, 0) + jnp.log(1 + jnp.exp(-jnp.abs(zb))))
        sigs.append(jnp.exp(zb + u))
        if masked:
            u = jnp.where(ci < ri, u, jnp.zeros_like(u))
        ubs.append(u)
    afters = [_dot(ub, tri_after) for ub in ubs]
    usums = [after[:, 0:1] + ub[:, 0:1].astype(F32) for after, ub in zip(afters, ubs)]
    ws = []
    for sig, after, r_run in zip(sigs, afters, r_runs):
        w = sig * jnp.exp(after + r_run).astype(BF16)
        if masked:
            w = jnp.where(ci < ri, w, jnp.zeros_like(w))
        ws.append(w)
    return usums, sigs, ws, extra


def _split_heads(x, lo):
    out = []
    zero = jnp.zeros((x.shape[0], LANES), x.dtype)
    for p in range(2):
        xp = x[:, LANES * p:LANES * (p + 1)]
        out += [jnp.where(lo, xp, zero), jnp.where(lo, zero, xp)]
    return out


def _per_head(x):
    return [x[:, :LANES], x[:, :LANES], x[:, LANES:], x[:, LANES:]]


def _resident(shape, col):
    return pl.BlockSpec(shape, lambda i: (0, col), pipeline_mode=pl.Buffered(1))


def sb_attn_fwd(qkv, name):
    lp = qkv.shape[0]
    bq = min(ROW_TILE, lp)
    nq = lp // bq
    assert nq <= 64

    def body(q_ref, k_ref, v_ref, o_ref, rs_ref):
        qi = pl.program_id(0)
        lane = _lane((bq, LANES))
        lo = lane < 64
        qs = _split_heads(q_ref[...], lo)

        def step(kb, carry, masked):
            off = pl.multiple_of(kb * bq, bq)
            ks = _per_head(k_ref[pl.ds(off, bq), :])
            vs = _per_head(v_ref[pl.ds(off, bq), :])
            heads, rss = carry
            r_runs = [heads[h][1] for h in range(4)]
            rss = list(rss)
            for h in range(4):
                rss[h // 2] = jnp.where(lane == 64 * (h % 2) + kb, r_runs[h], rss[h // 2])
            usums, _, ws, _ = _sb_blocks(qs, ks, r_runs, masked, bq)
            pvs = [_dot(ws[h], vs[h]) for h in range(4)]
            out = tuple((heads[h][0] + pvs[h], r_runs[h] + usums[h]) for h in range(4))
            return out, tuple(rss)

        zero = (jnp.zeros((bq, LANES), F32), jnp.zeros((bq, 1), F32))
        zr = jnp.zeros((bq, LANES), F32)
        carry = step(qi, ((zero,) * 4, (zr, zr)), True)
        def several(t, c):
            for r in range(KEY_UNROLL):
                c = step(qi - 1 - r - KEY_UNROLL * t, c, False)
            return c

        carry = lax.fori_loop(0, qi // KEY_UNROLL, several, carry)
        rem = qi % KEY_UNROLL
        heads, rss = lax.fori_loop(0, rem, lambda t, c: step(rem - 1 - t, c, False), carry)
        o_ref[...] = jnp.concatenate([jnp.where(lo, heads[0][0], heads[1][0]), jnp.where(lo, heads[2][0], heads[3][0])], axis=1)
        rs_ref[...] = jnp.concatenate(list(rss), axis=1)

    blk = pl.BlockSpec((bq, 2 * LANES), lambda i: (i, 0))
    return pl.pallas_call(
        body, name=name, grid=(nq,),
        in_specs=[blk, _resident((lp, 2 * LANES), 1), _resident((lp, 2 * LANES), 2)],
        out_specs=[blk, blk],
        out_shape=[jax.ShapeDtypeStruct((lp, SB_WIDTH), F32), jax.ShapeDtypeStruct((lp, SB_WIDTH), F32)],
        compiler_params=_cp("arbitrary"),
    )(qkv, qkv, qkv)


def sb_attn_bwd(qkv, rs, d_o, name):
    lp = qkv.shape[0]
    bq = min(ROW_TILE, lp)
    nq = lp // bq

    def body(q_ref, k_ref, v_ref, rs_ref, do_ref, dq_ref, dk_ref, dv_ref):
        qi = pl.program_id(0)

        @pl.when(qi == 0)
        def _():
            dk_ref[...] = jnp.zeros_like(dk_ref)
            dv_ref[...] = jnp.zeros_like(dv_ref)

        lane = _lane((bq, LANES))
        lo = lane < 64
        qs = _split_heads(q_ref[...], lo)
        dos = _split_heads(do_ref[...].astype(BF16), lo)
        rs_blk = rs_ref[...]
        ri = lax.broadcasted_iota(jnp.int32, (bq, bq), 0)
        ci = lax.broadcasted_iota(jnp.int32, (bq, bq), 1)
        tbefore = (ri < ci).astype(BF16)

        def step(kb, carry, masked):
            off = pl.multiple_of(kb * bq, bq)
            ks = _per_head(k_ref[pl.ds(off, bq), :])
            vs = _per_head(v_ref[pl.ds(off, bq), :])
            r_rights = [jnp.sum(jnp.where(lane == 64 * (h % 2) + kb, rs_blk[:, LANES * (h // 2):LANES * (h // 2 + 1)], 0.0),
                                axis=1, keepdims=True) for h in range(4)]
            _, sigs, wbs, dws = _sb_blocks(qs, ks, r_rights, masked, bq,
                                           after_scores=lambda: [_dot(dos[h], vs[h], "nt") for h in range(4)])
            gs = [wbs[h].astype(F32) * dws[h] for h in range(4)]
            gbs = [g.astype(BF16) for g in gs]
            gbefores = [_dot(gb, tbefore) for gb in gbs]
            dv_acc = [_dot(wbs[2 * p], dos[2 * p], "tn") + _dot(wbs[2 * p + 1], dos[2 * p + 1], "tn") for p in range(2)]
            dzbs = []
            for h in range(4):
                dz = gs[h] - sigs[h].astype(F32) * (gs[h] + gbefores[h] + carry[h][1])
                if masked:
                    dz = jnp.where(ci < ri, dz, 0.0)
                dzbs.append(dz.astype(BF16))
            dqs = [_dot(dzbs[h], ks[h]) for h in range(4)]
            dk_acc = [_dot(dzbs[2 * p], qs[2 * p], "tn") + _dot(dzbs[2 * p + 1], qs[2 * p + 1], "tn") for p in range(2)]
            dk_ref[pl.ds(off, bq), :] += jnp.concatenate(dk_acc, axis=1)
            dv_ref[pl.ds(off, bq), :] += jnp.concatenate(dv_acc, axis=1)
            return tuple((carry[h][0] + dqs[h], carry[h][1] + jnp.sum(gs[h], axis=1, keepdims=True)) for h in range(4))

        zero = (jnp.zeros((bq, LANES), F32), jnp.zeros((bq, 1), F32))
        def several(t, c):
            for r in range(KEY_UNROLL):
                c = step(KEY_UNROLL * t + r, c, False)
            return c

        carry = lax.fori_loop(0, qi // KEY_UNROLL, several, (zero,) * 4)
        carry = lax.fori_loop(qi - qi % KEY_UNROLL, qi, lambda t, c: step(t, c, False), carry)
        carry = step(qi, carry, True)
        dq_ref[...] = jnp.concatenate([jnp.where(lo, carry[0][0], carry[1][0]), jnp.where(lo, carry[2][0], carry[3][0])],
                                      axis=1).astype(dq_ref.dtype)

    blk = pl.BlockSpec((bq, 2 * LANES), lambda i: (i, 0))
    return pl.pallas_call(
        body, name=name, grid=(nq,),
        in_specs=[blk, _resident((lp, 2 * LANES), 1), _resident((lp, 2 * LANES), 2), blk, blk],
        out_specs=[blk, _resident((lp, 2 * LANES), 0), _resident((lp, 2 * LANES), 0)],
        out_shape=[jax.ShapeDtypeStruct((lp, SB_WIDTH), BF16), jax.ShapeDtypeStruct((lp, SB_WIDTH), F32),
                   jax.ShapeDtypeStruct((lp, SB_WIDTH), F32)],
        compiler_params=_cp("arbitrary"),
    )(qkv, qkv, qkv, rs, d_o)


def _mla_masks(bq):
    lane = _lane((bq, 2 * LANES))
    out = []
    for h in range(4):
        j = h % 2
        nope = jnp.logical_and(lane >= 64 * j, lane < 64 * (j + 1))
        rope = jnp.logical_and(lane >= LANES + MLA_ROPE * h, lane < LANES + MLA_ROPE * (h + 1))
        out.append(jnp.logical_or(nope, rope))
    return out


def _mla_split_q(q, masks):
    zero = jnp.zeros((q.shape[0], 2 * LANES), q.dtype)
    return [jnp.where(masks[h], q[:, 2 * LANES * (h // 2):2 * LANES * (h // 2 + 1)], zero) for h in range(4)]


def _mla_per_head_k(k):
    return [k[:, :2 * LANES], k[:, :2 * LANES], k[:, 2 * LANES:], k[:, 2 * LANES:]]


def mla_attn_fwd(qc, kc, v, name):
    lp = qc.shape[0]
    bq = min(ROW_TILE, lp)
    nq = lp // bq

    def body(q_ref, k_ref, v_ref, o_ref, lse_ref):
        qi = pl.program_id(0)
        qs = _mla_split_q(q_ref[...], _mla_masks(bq))
        lo = _lane((bq, LANES)) < 64
        ri = lax.broadcasted_iota(jnp.int32, (bq, bq), 0)
        ci = lax.broadcasted_iota(jnp.int32, (bq, bq), 1)

        def blocks(kbs, carry, masked):
            offs = [pl.multiple_of(kb * bq, bq) for kb in kbs]
            ks = [_mla_per_head_k(k_ref[pl.ds(o, bq), :]) for o in offs]
            ones = jnp.ones((bq, LANES), BF16)
            vs = []
            for o in offs:
                vp = _per_head(v_ref[pl.ds(o, bq), :])
                vs.append([jnp.where(lo, vp[h], ones) if h % 2 == 0 else jnp.where(lo, ones, vp[h]) for h in range(4)])
            ss = [[_dot(qs[h], ks[b][h], "nt") for h in range(4)] for b in range(len(kbs))]
            if masked:
                ss = [[jnp.where(ci <= ri, s, NEG) for s in row] for row in ss]
            prs, alphas, ms = [], [], []
            for h in range(4):
                top = ss[0][h]
                for b in range(1, len(kbs)):
                    top = jnp.maximum(top, ss[b][h])
                m_new = jnp.maximum(carry[h][1], jnp.max(top, axis=1, keepdims=True))
                alphas.append(jnp.exp(carry[h][1] - m_new))
                ms.append(m_new)
                prs.append([jnp.exp(ss[b][h] - m_new).astype(BF16) for b in range(len(kbs))])
            out = []
            for h in range(4):
                acc = carry[h][0] * alphas[h]
                for b in range(len(kbs)):
                    acc = acc + _dot(prs[h][b], vs[b][h])
                out.append((acc, ms[h]))
            return tuple(out)

        zero = (jnp.zeros((bq, LANES), F32), jnp.full((bq, 1), NEG, F32))
        carry = blocks([qi], (zero,) * 4, True)
        carry = lax.fori_loop(0, qi // KEY_UNROLL,
                              lambda t, c: blocks([qi - 1 - r - KEY_UNROLL * t for r in range(KEY_UNROLL)], c, False), carry)
        rem = qi % KEY_UNROLL
        carry = lax.fori_loop(0, rem, lambda t, c: blocks([rem - 1 - t], c, False), carry)
        outs, lses = [], []
        for h in range(4):
            acc, m = carry[h]
            l = acc[:, 64:65] if h % 2 == 0 else acc[:, 0:1]
            outs.append(acc / l)
            lses.append(m + jnp.log(l))
        o_ref[...] = jnp.concatenate([jnp.where(lo, outs[0], outs[1]), jnp.where(lo, outs[2], outs[3])], axis=1)
        lse_ref[...] = jnp.concatenate([jnp.where(lo, lses[0], lses[1]), jnp.where(lo, lses[2], lses[3])], axis=1)

    blk = pl.BlockSpec((bq, 2 * LANES), lambda i: (i, 0))
    return pl.pallas_call(
        body, name=name, grid=(nq,),
        in_specs=[pl.BlockSpec((bq, 4 * LANES), lambda i: (i, 0)), _resident((lp, 4 * LANES), 0), _resident((lp, 2 * LANES), 0)],
        out_specs=[blk, blk],
        out_shape=[jax.ShapeDtypeStruct((lp, 2 * LANES), F32), jax.ShapeDtypeStruct((lp, 2 * LANES), F32)],
        compiler_params=_cp("arbitrary"),
    )(qc, kc, v)


def mla_attn_bwd(qc, kc, v, o, lse, d_o, name):
    lp = qc.shape[0]
    bq = min(ROW_TILE, lp)
    nq = lp // bq

    def body(q_ref, k_ref, v_ref, o_ref, lse_ref, do_ref, dq_ref, dk_ref, dv_ref):
        qi = pl.program_id(0)

        @pl.when(qi == 0)
        def _():
            dk_ref[...] = jnp.zeros_like(dk_ref)
            dv_ref[...] = jnp.zeros_like(dv_ref)

        d_o = do_ref[...]
        masks = _mla_masks(bq)
        qs = _mla_split_q(q_ref[...], masks)
        lo = _lane((bq, LANES)) < 64
        dos = _split_heads(d_o.astype(BF16), lo)
        od = o_ref[...] * d_o
        lse_blk = lse_ref[...]
        delta, lses = [], []
        for h in range(4):
            odp = od[:, LANES * (h // 2):LANES * (h // 2 + 1)]
            delta.append(jnp.sum(jnp.where(lo, odp, 0.0) if h % 2 == 0 else jnp.where(lo, 0.0, odp), axis=1, keepdims=True))
            c0 = LANES * (h // 2) + 64 * (h % 2)
            lses.append(lse_blk[:, c0:c0 + 1])
        ri = lax.broadcasted_iota(jnp.int32, (bq, bq), 0)
        ci = lax.broadcasted_iota(jnp.int32, (bq, bq), 1)

        def step(kb, carry, masked):
            off = pl.multiple_of(kb * bq, bq)
            ks = _mla_per_head_k(k_ref[pl.ds(off, bq), :])
            vs = _per_head(v_ref[pl.ds(off, bq), :])
            ss = [_dot(qs[h], ks[h], "nt") for h in range(4)]
            dps = [_dot(dos[h], vs[h], "nt") for h in range(4)]
            prbs, dss = [], []
            for h in range(4):
                s = ss[h]
                if masked:
                    s = jnp.where(ci <= ri, s, NEG)
                pr = jnp.exp(s - lses[h])
                prbs.append(pr.astype(BF16))
                dss.append((pr * (dps[h] - delta[h])).astype(BF16))
            dv_acc = [_dot(prbs[2 * p], dos[2 * p], "tn") + _dot(prbs[2 * p + 1], dos[2 * p + 1], "tn") for p in range(2)]
            dqs = [_dot(dss[h], ks[h]) for h in range(4)]
            dk_acc = [_dot(dss[2 * p], qs[2 * p], "tn") + _dot(dss[2 * p + 1], qs[2 * p + 1], "tn") for p in range(2)]
            dk_ref[pl.ds(off, bq), :] += jnp.concatenate(dk_acc, axis=1)
            dv_ref[pl.ds(off, bq), :] += jnp.concatenate(dv_acc, axis=1)
            return tuple(carry[h] + dqs[h] for h in range(4))

        zero = jnp.zeros((bq, 2 * LANES), F32)
        carry = step(qi, (zero,) * 4, True)
        def several(t, c):
            for r in range(KEY_UNROLL):
                c = step(qi - 1 - r - KEY_UNROLL * t, c, False)
            return c

        carry = lax.fori_loop(0, qi // KEY_UNROLL, several, carry)
        rem = qi % KEY_UNROLL
        carry = lax.fori_loop(0, rem, lambda t, c: step(rem - 1 - t, c, False), carry)
        dq_ref[...] = jnp.concatenate([jnp.where(masks[0], carry[0], 0.0) + jnp.where(masks[1], carry[1], 0.0),
                                       jnp.where(masks[2], carry[2], 0.0) + jnp.where(masks[3], carry[3], 0.0)], axis=1)

    blk = pl.BlockSpec((bq, 2 * LANES), lambda i: (i, 0))
    wide = pl.BlockSpec((bq, 4 * LANES), lambda i: (i, 0))
    return pl.pallas_call(
        body, name=name, grid=(nq,),
        in_specs=[wide, _resident((lp, 4 * LANES), 0), _resident((lp, 2 * LANES), 0), blk, blk, blk],
        out_specs=[wide, _resident((lp, 4 * LANES), 0), _resident((lp, 2 * LANES), 0)],
        out_shape=[jax.ShapeDtypeStruct((lp, 4 * LANES), F32), jax.ShapeDtypeStruct((lp, 4 * LANES), F32),
                   jax.ShapeDtypeStruct((lp, 2 * LANES), F32)],
        compiler_params=_cp("arbitrary"),
    )(qc, kc, v, o, lse, d_o)


def _mix_out(y_pre, z, o_sb, o_mla, g_ssd, g_sb, g_mla):
    return jnp.concatenate([_rms(y_pre * _silu(z), g_ssd), _rms(o_sb, g_sb), _rms(o_mla, g_mla)], axis=1)


def _ffn_act(up_a, up_b, halo_a, halo_b, w_a, w_b, b_a, b_b, i):
    ca = conv_fwd(up_a, halo_a, w_a, i) + b_a
    cb_ = conv_fwd(up_b, halo_b, w_b, i) + b_b
    return ca, cb_


def layer_fwd(h, w, cs, sn, l):
    lp = h.shape[0]
    nm = f"l{l}_"
    hn = rowwise(lambda i, n, x, g: _rms(x, g), [RI(h), PA(w["norm_mix_g"])], [RO(D_MODEL, BF16)], nm + "rms_mix", lp, tm=WIDE_TILE)[0]
    u = mm(hn, w["w_main"], "nn", F32, nm + "in_main")
    qkv = mm(hn, w["w_sb"], "nn", BF16, nm + "in_sb")
    xbc_c = rowwise(lambda i, n, x, hl, cw, cb_: _silu(conv_fwd(x, hl, cw, i) + cb_),
                    [RI(u, SSD_XBC, 0), HP(u, SSD_XBC, 0), PA(w["ssd_conv_w"]), PA(w["ssd_conv_b"])],
                    [RO(SSD_XBC, F32)], nm + "ssd_conv", lp, tm=WIDE_TILE)[0]
    y_pre, hprev = ssd_fwd(xbc_c, u, w["dt_bias"], w["a_log"], w["d_skip"], nm + "ssd_fwd")
    o_sb, rs_sb = sb_attn_fwd(qkv, nm + "sb_fwd")
    qn, kvn = rowwise(lambda i, n, qa, ckv, gq, gkv: (_rms(qa, gq, MLA_Q_RANK), _rms(ckv, gkv)),
                      [RI(u, 256, U_QA // 256), RI(u, LANES, U_CKV // LANES), PA(w["q_norm_g"]), PA(w["kv_norm_g"])],
                      [RO(256, BF16), RO(LANES, BF16)], nm + "mla_rms", lp, tm=WIDE_TILE)
    qf = mm(qn, w["w_uq"], "nn", F32, nm + "mla_uq")
    kvf = mm(kvn, w["w_ukv"], "nn", F32, nm + "mla_ukv")

    def pack(i, n, qf_, kvf_, kr4, cos, sin):
        qf_ = qf_ * MLA_SCALE
        qr = qf_[:, 256:384]
        qr = qr * cos + rope_rot(qr) * sin
        kr = kr4 * cos + rope_rot(kr4) * sin
        qc = jnp.concatenate([qf_[:, 0:128], qr, qf_[:, 128:256], qr], axis=1)
        kc = jnp.concatenate([kvf_[:, 0:128], kr, kvf_[:, 128:256], kr], axis=1)
        return qc, kc, kvf_[:, 256:512]

    qc, kc, vv = rowwise(pack, [RI(qf), RI(kvf), RI(u, LANES, U_KR4 // LANES), RI(cs), RI(sn)],
                         [RO(512, BF16), RO(512, BF16), RO(256, BF16)], nm + "mla_pack", lp, tm=WIDE_TILE)
    o_mla, lse = mla_attn_fwd(qc, kc, vv, nm + "mla_fwd")
    cat = rowwise(lambda i, n, *a: _mix_out(*a),
                  [RI(y_pre), RI(u, SSD_WIDTH, U_Z // SSD_WIDTH), RI(o_sb), RI(o_mla),
                   PA(w["ssd_norm_g"]), PA(w["sb_norm_g"]), PA(w["mla_norm_g"])],
                  [RO(D_MODEL, BF16)], nm + "mix_out", lp, tm=WIDE_TILE)[0]
    h_mid = mm(cat, w["w_out"], "nn", F32, nm + "out_proj", add=h)
    hn2 = rowwise(lambda i, n, x, g: _rms(x, g), [RI(h_mid), PA(w["norm_ffn_g"])], [RO(D_MODEL, BF16)], nm + "rms_ffn", lp, tm=WIDE_TILE)[0]
    up_a = mm(hn2, w["w_up_a"], "nn", F32, nm + "up_a")
    up_b = mm(hn2, w["w_up_b"], "nn", F32, nm + "up_b")
    wc = 1408

    def act(i, n, ua, ub, ha, hb_, wa, wb, ba, bb_):
        ca, cb_ = _ffn_act(ua, ub, ha, hb_, wa, wb, ba, bb_, i)
        return _silu(ca) * cb_

    a_t = rowwise(act, [RI(up_a, wc, 0, True), RI(up_b, wc, 0, True), HP(up_a, wc, 0, True), HP(up_b, wc, 0, True),
                        PA(w["ffn_conv_w_a"], wc, 0, True), PA(w["ffn_conv_w_b"], wc, 0, True),
                        PA(w["ffn_conv_b_a"], wc, 0, True), PA(w["ffn_conv_b_b"], wc, 0, True)],
                  [RO(D_FF, BF16, wc, True)], nm + "ffn_act", lp, tm=WIDE_TILE, ncol=D_FF // wc)[0]
    h_out = mm(a_t, w["w_down"], "nn", F32, nm + "down", add=h_mid)
    saved = dict(h=h, hn=hn, u=u, qkv=qkv, xbc_c=xbc_c, y_pre=y_pre, hprev=hprev, o_sb=o_sb, rs_sb=rs_sb, qn=qn, kvn=kvn,
                 qc=qc, kc=kc, vv=vv, o_mla=o_mla, lse=lse, cat=cat, h_mid=h_mid, hn2=hn2, up_a=up_a, up_b=up_b, a_t=a_t)
    return h_out, saved


def layer_bwd(dh_out, w, s, cs, sn, l):
    lp = dh_out.shape[0]
    nm = f"l{l}b_"
    g = {}
    wc = 1408
    ncolf = D_FF // wc
    g["w_down"] = mm(s["a_t"], dh_out, "tn", BF16, nm + "dw_down")
    d_act = mm(dh_out, w["w_down"], "nt", F32, nm + "d_act")

    def act_bwd(i, n, ua, ub, ha, hb_, wa, wb, ba, bb_, da_):
        ca, cb_ = _ffn_act(ua, ub, ha, hb_, wa, wb, ba, bb_, i)
        sg = jax.nn.sigmoid(ca)
        dca = da_ * cb_ * (sg * (1.0 + ca * (1.0 - sg)))
        dcb = da_ * (ca * sg)
        return (dca, dcb, conv_bwd_w(dca, ua, ha, i, FFN_CONV), conv_bwd_w(dcb, ub, hb_, i, FFN_CONV),
                jnp.sum(dca, axis=0, keepdims=True), jnp.sum(dcb, axis=0, keepdims=True))

    dca, dcb, g["ffn_conv_w_a"], g["ffn_conv_w_b"], g["ffn_conv_b_a"], g["ffn_conv_b_b"] = rowwise(
        act_bwd, [RI(s["up_a"], wc, 0, True), RI(s["up_b"], wc, 0, True), HP(s["up_a"], wc, 0, True),
                  HP(s["up_b"], wc, 0, True), PA(w["ffn_conv_w_a"], wc, 0, True), PA(w["ffn_conv_w_b"], wc, 0, True),
                  PA(w["ffn_conv_b_a"], wc, 0, True), PA(w["ffn_conv_b_b"], wc, 0, True), RI(d_act, wc, 0, True)],
        [RO(D_FF, F32, wc, True), RO(D_FF, F32, wc, True), AO(FFN_CONV, D_FF, wc, True), AO(FFN_CONV, D_FF, wc, True),
         AO(1, D_FF, wc, True), AO(1, D_FF, wc, True)], nm + "ffn_act_bwd", lp, tm=WIDE_TILE // 2, ncol=ncolf)

    def conv_t(i, n, da_, db_, ha, hb_, wa, wb):
        return conv_bwd_data(da_, ha, wa, i, n), conv_bwd_data(db_, hb_, wb, i, n)

    dup_a, dup_b = rowwise(conv_t, [RI(dca, wc, 0, True), RI(dcb, wc, 0, True), HN(dca, wc, 0, True), HN(dcb, wc, 0, True),
                                    PA(w["ffn_conv_w_a"], wc, 0, True), PA(w["ffn_conv_w_b"], wc, 0, True)],
                           [RO(D_FF, BF16, wc, True), RO(D_FF, BF16, wc, True)], nm + "ffn_conv_t", lp, tm=WIDE_TILE, ncol=ncolf)
    g["w_up_a"] = mm(s["hn2"], dup_a, "tn", BF16, nm + "dw_up_a")
    g["w_up_b"] = mm(s["hn2"], dup_b, "tn", BF16, nm + "dw_up_b")
    dhn2 = mm(dup_a, w["w_up_a"], "nt", F32, nm + "dhn2_a")
    dhn2 = mm(dup_b, w["w_up_b"], "nt", F32, nm + "dhn2_b", add=dhn2)

    def rms_bwd(i, n, x, gg, dy, dres):
        _, vjp = jax.vjp(_rms, x, gg)
        dx, dg = vjp(dy)
        return dres + dx, dg

    dh_mid, g["norm_ffn_g"] = rowwise(rms_bwd, [RI(s["h_mid"]), PA(w["norm_ffn_g"]), RI(dhn2), RI(dh_out)],
                                      [RO(D_MODEL, F32), AO(1, D_MODEL)], nm + "rms_ffn_bwd", lp, tm=WIDE_TILE)
    g["w_out"] = mm(s["cat"], dh_mid, "tn", BF16, nm + "dw_out")
    d_cat = mm(dh_mid, w["w_out"], "nt", F32, nm + "d_cat")
    u = s["u"]

    def mix_bwd(i, n, y_pre, z, o_sb, o_mla, g1, g2, g3, dcat):
        _, vjp = jax.vjp(_mix_out, y_pre, z, o_sb, o_mla, g1, g2, g3)
        return vjp(dcat)

    dy_pre, dz, do_sb, do_mla, g["ssd_norm_g"], g["sb_norm_g"], g["mla_norm_g"] = rowwise(
        mix_bwd, [RI(s["y_pre"]), RI(u, SSD_WIDTH, U_Z // SSD_WIDTH), RI(s["o_sb"]), RI(s["o_mla"]),
                  PA(w["ssd_norm_g"]), PA(w["sb_norm_g"]), PA(w["mla_norm_g"]), RI(d_cat)],
        [RO(SSD_WIDTH, F32), RO(SSD_WIDTH, BF16), RO(SB_WIDTH, F32), RO(256, F32),
         AO(1, SSD_WIDTH), AO(1, SB_WIDTH), AO(1, 256)], nm + "mix_out_bwd", lp, tm=WIDE_TILE)
    dxs, dbp, dcp, ddtp, pg = ssd_bwd(s["xbc_c"], u, w["dt_bias"], w["a_log"], w["d_skip"], s["hprev"], dy_pre, nm + "ssd_bwd")
    pg = pg.reshape(2, SUBLANES, LANES).sum(axis=0)
    g["dt_bias"], g["a_log"], g["d_skip"] = pg[0:1], pg[1:2], pg[2:3]

    def conv4_bwd(i, n, x, hl, cw, cb_, dxs_, dbp_, dcp_, ddtp_):
        pre = conv_fwd(x, hl, cw, i) + cb_
        d_out = jnp.concatenate([dxs_, dbp_, dcp_], axis=1)
        sg = jax.nn.sigmoid(pre)
        d_pre = d_out * (sg * (1.0 + pre * (1.0 - sg)))
        ddt = ddtp_[:, 0:128] + ddtp_[:, 128:256]
        return d_pre, ddt, conv_bwd_w(d_pre, x, hl, i, SSD_CONV), jnp.sum(d_pre, axis=0, keepdims=True)

    d_pre, ddt, g["ssd_conv_w"], g["ssd_conv_b"] = rowwise(
        conv4_bwd, [RI(u, SSD_XBC, 0), HP(u, SSD_XBC, 0), PA(w["ssd_conv_w"]), PA(w["ssd_conv_b"]),
                    RI(dxs), RI(dbp), RI(dcp), RI(ddtp)],
        [RO(SSD_XBC, F32), RO(LANES, BF16), AO(SSD_CONV, SSD_XBC), AO(1, SSD_XBC)], nm + "ssd_conv_bwd", lp, tm=WIDE_TILE)
    d_xbc = rowwise(lambda i, n, d, hn_, cw: conv_bwd_data(d, hn_, cw, i, n),
                    [RI(d_pre), HN(d_pre), PA(w["ssd_conv_w"])], [RO(SSD_XBC, BF16)], nm + "ssd_conv_t", lp, tm=WIDE_TILE)[0]
    dq_sb, dk_sb, dv_sb = sb_attn_bwd(s["qkv"], s["rs_sb"], do_sb, nm + "sb_bwd")
    dqkv = jnp.concatenate([dq_sb, dk_sb.astype(BF16), dv_sb.astype(BF16)], axis=1)
    dqc, dkc, dvv = mla_attn_bwd(s["qc"], s["kc"], s["vv"], s["o_mla"], s["lse"], do_mla, nm + "mla_bwd")

    def unpack(i, n, dqc_, dkc_, dvv_, cos, sin):
        dqr = dqc_[:, 128:256] + dqc_[:, 384:512]
        dqr = dqr * cos + rope_rot_t(dqr * sin)
        dkr = dkc_[:, 128:256] + dkc_[:, 384:512]
        dkr = dkr * cos + rope_rot_t(dkr * sin)
        dq = jnp.concatenate([dqc_[:, 0:128], dqc_[:, 256:384], dqr], axis=1) * MLA_SCALE
        dkv = jnp.concatenate([dkc_[:, 0:128], dkc_[:, 256:384], dvv_], axis=1)
        return dq, dkv, dkr

    dq, dkv, dkr4 = rowwise(unpack, [RI(dqc), RI(dkc), RI(dvv), RI(cs), RI(sn)],
                            [RO(384, BF16), RO(512, BF16), RO(LANES, BF16)], nm + "mla_unpack", lp, tm=WIDE_TILE)
    g["w_uq"] = mm(s["qn"], dq, "tn", F32, nm + "dw_uq")
    g["w_ukv"] = mm(s["kvn"], dkv, "tn", F32, nm + "dw_ukv")
    dqn = mm(dq, w["w_uq"], "nt", F32, nm + "dqn")
    dkvn = mm(dkv, w["w_ukv"], "nt", F32, nm + "dkvn")

    def mla_rms_bwd(i, n, qa, ckv, gq, gkv, dqn_, dkvn_):
        _, vjp = jax.vjp(lambda a, b, c, d: (_rms(a, c, MLA_Q_RANK), _rms(b, d)), qa, ckv, gq, gkv)
        return vjp((dqn_, dkvn_))

    dqa, dckv, g["q_norm_g"], g["kv_norm_g"] = rowwise(
        mla_rms_bwd, [RI(u, 256, U_QA // 256), RI(u, LANES, U_CKV // LANES), PA(w["q_norm_g"]), PA(w["kv_norm_g"]),
                      RI(dqn), RI(dkvn)],
        [RO(256, BF16), RO(LANES, BF16), AO(1, 256), AO(1, LANES)], nm + "mla_rms_bwd", lp, tm=WIDE_TILE)
    du = jnp.concatenate([d_xbc, dz, dqa, dckv, dkr4, ddt, jnp.zeros((lp, LANES), BF16)], axis=1)
    g["w_main"] = mm(s["hn"], du, "tn", F32, nm + "dw_main")
    g["w_sb"] = mm(s["hn"], dqkv, "tn", F32, nm + "dw_sb")
    dhn = mm(du, w["w_main"], "nt", F32, nm + "dhn_main")
    dhn = mm(dqkv, w["w_sb"], "nt", F32, nm + "dhn_sb", add=dhn)
    dh_in, g["norm_mix_g"] = rowwise(rms_bwd, [RI(s["h"]), PA(w["norm_mix_g"]), RI(dhn), RI(dh_mid)],
                                     [RO(D_MODEL, F32), AO(1, D_MODEL)], nm + "rms_mix_bwd", lp, tm=WIDE_TILE)
    return dh_in, g


_IN_CUTS = np.cumsum((512, 1024, 8, 256, 256, 256, 192, 128, 32))


def _pad_cols(a, n):
    return jnp.pad(a, ((0, 0), (0, n - a.shape[1])))


def prep_layer_weights(full, l):
    w_in = full["w_in"][l]
    c = _IN_CUTS
    z, xbc, dtr = w_in[:, :c[0]], w_in[:, c[0]:c[1]], w_in[:, c[1]:c[2]]
    q_sb, k_sb, v_sb = w_in[:, c[2]:c[3]], w_in[:, c[3]:c[4]], w_in[:, c[4]:c[5]]
    q_a, c_kv, k_r = w_in[:, c[5]:c[6]], w_in[:, c[6]:c[7]], w_in[:, c[7]:c[8]]
    w_main = jnp.concatenate([xbc, z, _pad_cols(q_a, 256), c_kv, k_r, k_r, k_r, k_r, _pad_cols(dtr, 256)], axis=1)
    assert w_main.shape[1] == U_MAIN
    row = lambda v, n=None: _pad_cols(v.reshape(1, -1).astype(F32), v.size if n is None else n)
    uq = full["mla_w_uq"][l].reshape(MLA_Q_RANK, 4, 96)
    w_uq = jnp.concatenate([uq[:, :, :64].reshape(MLA_Q_RANK, 256), uq[:, :, 64:].reshape(MLA_Q_RANK, 128)], axis=1)
    w_uq = jnp.pad(w_uq, ((0, 256 - MLA_Q_RANK), (0, 0)))
    ukv = full["mla_w_ukv"][l].reshape(MLA_KV_RANK, 4, 128)
    w_ukv = jnp.concatenate([ukv[:, :, :64].reshape(MLA_KV_RANK, 256), ukv[:, :, 64:].reshape(MLA_KV_RANK, 256)], axis=1)
    return dict(
        norm_mix_g=row(full["norm_mix_g"][l]), w_main=w_main, w_sb=jnp.concatenate([q_sb * SB_SCALE, k_sb, v_sb], axis=1),
        ssd_conv_w=full["ssd_conv_w"][l], ssd_conv_b=row(full["ssd_conv_b"][l]),
        dt_bias=row(full["ssd_dt_bias"][l], LANES), a_log=row(full["ssd_a_log"][l], LANES), d_skip=row(full["ssd_d"][l], LANES),
        ssd_norm_g=row(full["ssd_norm_g"][l]), sb_norm_g=row(full["sb_norm_g"][l]),
        q_norm_g=row(full["mla_q_norm_g"][l], 256), kv_norm_g=row(full["mla_kv_norm_g"][l]),
        w_uq=w_uq, w_ukv=w_ukv, mla_norm_g=row(full["mla_norm_g"][l]),
        w_out=full["w_out"][l], norm_ffn_g=row(full["norm_ffn_g"][l]),
        w_up_a=full["ffn_w_up"][l][:, :D_FF], w_up_b=full["ffn_w_up"][l][:, D_FF:],
        ffn_conv_w_a=full["ffn_conv_w"][l][:, :D_FF], ffn_conv_w_b=full["ffn_conv_w"][l][:, D_FF:],
        ffn_conv_b_a=row(full["ffn_conv_b"][l][:D_FF]), ffn_conv_b_b=row(full["ffn_conv_b"][l][D_FF:]),
        w_down=full["ffn_w_down"][l],
    )


def unprep_layer_grads(g):
    wm = g["w_main"]
    xbc, z = wm[:, U_XBC:U_XBC + 1024], wm[:, U_Z:U_Z + 512]
    q_a, c_kv = wm[:, U_QA:U_QA + MLA_Q_RANK], wm[:, U_CKV:U_CKV + 128]
    k_r = (wm[:, U_KR4:U_KR4 + 32] + wm[:, U_KR4 + 32:U_KR4 + 64] + wm[:, U_KR4 + 64:U_KR4 + 96] + wm[:, U_KR4 + 96:U_KR4 + 128])
    dtr = wm[:, U_DT:U_DT + SSD_HEADS]
    w_sb = g["w_sb"]
    w_in = jnp.concatenate([z, xbc, dtr, w_sb[:, :SB_WIDTH] * SB_SCALE, w_sb[:, SB_WIDTH:], q_a, c_kv, k_r], axis=1)
    guq = g["w_uq"][:MLA_Q_RANK]
    guq = jnp.concatenate([guq[:, :256].reshape(MLA_Q_RANK, 4, 64), guq[:, 256:].reshape(MLA_Q_RANK, 4, 32)], axis=2)
    gukv = g["w_ukv"]
    gukv = jnp.concatenate([gukv[:, :256].reshape(MLA_KV_RANK, 4, 64), gukv[:, 256:].reshape(MLA_KV_RANK, 4, 64)], axis=2)
    return dict(
        norm_mix_g=g["norm_mix_g"][0], w_in=w_in, ssd_conv_w=g["ssd_conv_w"], ssd_conv_b=g["ssd_conv_b"][0],
        ssd_dt_bias=g["dt_bias"][0, :SSD_HEADS], ssd_a_log=g["a_log"][0, :SSD_HEADS], ssd_d=g["d_skip"][0, :SSD_HEADS],
        ssd_norm_g=g["ssd_norm_g"][0], sb_norm_g=g["sb_norm_g"][0], mla_q_norm_g=g["q_norm_g"][0, :MLA_Q_RANK],
        mla_kv_norm_g=g["kv_norm_g"][0], mla_w_uq=guq.reshape(MLA_Q_RANK, 384), mla_w_ukv=gukv.reshape(MLA_KV_RANK, 512),
        mla_norm_g=g["mla_norm_g"][0], w_out=g["w_out"], norm_ffn_g=g["norm_ffn_g"][0],
        ffn_w_up=jnp.concatenate([g["w_up_a"], g["w_up_b"]], axis=1),
        ffn_conv_w=jnp.concatenate([g["ffn_conv_w_a"], g["ffn_conv_w_b"]], axis=1),
        ffn_conv_b=jnp.concatenate([g["ffn_conv_b_a"][0], g["ffn_conv_b_b"][0]], axis=0),
        ffn_w_down=g["w_down"],
    )


def rope_tables(lp):
    pos = jnp.arange(lp, dtype=F32)
    inv = 1.0 / (ROPE_BASE ** (jnp.arange(0, MLA_ROPE, 2, dtype=F32) / MLA_ROPE))
    ang = pos[:, None] * inv[None, :]
    ang = jnp.concatenate([ang, ang] * 4, axis=-1)
    return jnp.cos(ang), jnp.sin(ang)


def local_step(x_seq, target, full):
    seq = x_seq.shape[0]
    length = seq + N_META
    lp = -(-length // ROW_TILE) * ROW_TILE
    cs, sn = rope_tables(lp)
    h = jnp.concatenate([full["meta_tokens"].astype(F32), x_seq, jnp.zeros((lp - length, D_MODEL), F32)], axis=0)
    tgt = jnp.pad(target, ((N_META, lp - length), (0, 0)))
    ws = [prep_layer_weights(full, l) for l in range(DEPTH)]
    saved = []
    for l in range(DEPTH):
        h, s = layer_fwd(h, ws[l], cs, sn, l)
        saved.append(s)
    fg = full["final_norm_g"].reshape(1, D_MODEL).astype(F32)
    tm = min(WIDE_TILE, lp)

    def loss_fn(i, n, x, g, t):
        rows = _rows_iota(x) + i * tm
        valid = jnp.logical_and(rows >= N_META, rows < length)

        def f(x_, g_):
            err = jnp.where(valid, _rms(x_, g_) - t, 0.0)
            return 0.5 * jnp.sum(err * err) * (1.0 / D_MODEL)

        val, (dx, dg) = jax.value_and_grad(f, argnums=(0, 1))(x, g)
        return dx, jnp.full((1, LANES), val, F32), dg

    dh, loss_row, g_final = rowwise(loss_fn, [RI(h), PA(fg), RI(tgt)], [RO(D_MODEL, F32), AO(1, LANES), AO(1, D_MODEL)],
                                    "loss_head", lp, tm=WIDE_TILE)
    grads = {}
    per_layer = [None] * DEPTH
    for l in reversed(range(DEPTH)):
        dh, g = layer_bwd(dh, ws[l], saved[l], cs, sn, l)
        per_layer[l] = unprep_layer_grads(g)
    for k in per_layer[0]:
        grads[k] = jnp.stack([per_layer[l][k] for l in range(DEPTH)], axis=0)
    grads["final_norm_g"] = g_final[0]
    grads["meta_tokens"] = dh[:N_META]
    return loss_row[0, 0], dh[N_META:length], grads


_ANY = pl.BlockSpec(memory_space=pl.ANY)


def chip_exchange(srcs, modes, name):
    n = len(srcs)
    flips = ((1, 0), (0, 1), (1, 1))

    def body(*refs):
        ins, outs = refs[:n], refs[n:2 * n]
        send_sems, recv_sems, fwd_send_sems, fwd_recv_sems, loc_sems = refs[2 * n:]
        x, y, c = lax.axis_index("x"), lax.axis_index("y"), lax.axis_index("c")
        me = 2 * x + y
        waits, forwards = [], []
        for a in range(n):
            whole = modes[a] != "slab"
            cp = pltpu.make_async_copy(ins[a] if whole else ins[a].at[me], outs[a].at[me], loc_sems.at[a])
            cp.start()
            waits.append(cp.wait)
            half = ins[a].shape[0] // 2 if modes[a] == "bcast_split" else None
            for k, (fx, fy) in enumerate(flips):
                px = 1 - x if fx else x
                py = 1 - y if fy else y
                peer = 2 * px + py
                if half is None:
                    src = ins[a] if whole else ins[a].at[peer]
                    dst = outs[a].at[me]
                else:
                    src = ins[a].at[pl.ds(c * half, half)]
                    dst = outs[a].at[me, pl.ds(c * half, half)]
                rc = pltpu.make_async_remote_copy(src_ref=src, dst_ref=dst, send_sem=send_sems.at[a, k],
                                                  recv_sem=recv_sems.at[a, k], device_id=(px, py, c), device_id_type=MESH_ID)
                rc.start()
                if half is None:
                    waits.append(rc.wait)
                else:
                    waits.append(rc.wait_send)
                    landed = outs[a].at[peer, pl.ds(c * half, half)]
                    fw = pltpu.make_async_remote_copy(src_ref=landed, dst_ref=landed, send_sem=fwd_send_sems.at[a, k],
                                                      recv_sem=fwd_recv_sems.at[a, k], device_id=(x, y, 1 - c),
                                                      device_id_type=MESH_ID)
                    forwards.append((rc, fw))
        for rc, fw in forwards:
            rc.wait_recv()
            fw.start()
        for rc, fw in forwards:
            fw.wait()
        for w in waits:
            w()

    out_shape = [jax.ShapeDtypeStruct((N_CHIPS,) + (s.shape if m != "slab" else s.shape[1:]), s.dtype) for s, m in zip(srcs, modes)]
    return pl.pallas_call(
        body, name=name, in_specs=[_ANY] * n, out_specs=[_ANY] * n, out_shape=out_shape,
        scratch_shapes=[pltpu.SemaphoreType.DMA((n, 3)), pltpu.SemaphoreType.DMA((n, 3)), pltpu.SemaphoreType.DMA((n, 3)),
                        pltpu.SemaphoreType.DMA((n, 3)), pltpu.SemaphoreType.DMA((n,))],
    )(*srcs)


def _piece(ref, mode, k):
    if mode == "slab":
        return ref.at[k]
    if mode == "rows":
        rs = ref.shape[1] // N_CHIPS
        return ref.at[:, pl.ds(pl.multiple_of(k * rs, 16), rs), :]
    if mode == "cols":
        cs = ref.shape[2] // N_CHIPS
        return ref.at[:, :, pl.ds(pl.multiple_of(k * cs, LANES), cs)]
    return ref


def _piece_shape(shape, mode):
    if mode == "slab":
        return shape[1:]
    if mode == "rows":
        return (shape[0], shape[1] // N_CHIPS, shape[2])
    if mode == "cols":
        return (shape[0], shape[1], shape[2] // N_CHIPS)
    return shape


def grad_exchange(srcs, modes, name):
    n = len(srcs)
    flips = ((1, 0), (0, 1), (1, 1))

    def body(*refs):
        ins, outs = refs[:n], refs[n:2 * n]
        send_sems, recv_sems, fwd_send_sems, fwd_recv_sems, sib_send_sems, sib_recv_sems, loc_sems = refs[2 * n:]
        x, y, c = lax.axis_index("x"), lax.axis_index("y"), lax.axis_index("c")
        me = 2 * x + y
        sibling = (x, y, 1 - c)
        waits, forwards = [], []
        for a in range(n):
            mine = _piece(ins[a], modes[a], me)
            slot = outs[a].at[4 * c + me]
            cp = pltpu.make_async_copy(mine, slot, loc_sems.at[a])
            cp.start()
            sb = pltpu.make_async_remote_copy(src_ref=mine, dst_ref=slot, send_sem=sib_send_sems.at[a],
                                              recv_sem=sib_recv_sems.at[a], device_id=sibling, device_id_type=MESH_ID)
            sb.start()
            waits += [cp.wait, sb.wait]
            for k, (fx, fy) in enumerate(flips):
                px = 1 - x if fx else x
                py = 1 - y if fy else y
                peer = 2 * px + py
                rc = pltpu.make_async_remote_copy(src_ref=_piece(ins[a], modes[a], peer), dst_ref=slot,
                                                  send_sem=send_sems.at[a, k], recv_sem=recv_sems.at[a, k],
                                                  device_id=(px, py, c), device_id_type=MESH_ID)
                rc.start()
                landed = outs[a].at[4 * c + peer]
                fw = pltpu.make_async_remote_copy(src_ref=landed, dst_ref=landed, send_sem=fwd_send_sems.at[a, k],
                                                  recv_sem=fwd_recv_sems.at[a, k], device_id=sibling, device_id_type=MESH_ID)
                waits.append(rc.wait_send)
                forwards.append((rc, fw))
        for rc, fw in forwards:
            rc.wait_recv()
            fw.start()
        for rc, fw in forwards:
            fw.wait()
        for w in waits:
            w()

    out_shape = [jax.ShapeDtypeStruct((2 * N_CHIPS,) + tuple(_piece_shape(s.shape, m)), s.dtype) for s, m in zip(srcs, modes)]
    dma = pltpu.SemaphoreType.DMA
    return pl.pallas_call(
        body, name=name, in_specs=[_ANY] * n, out_specs=[_ANY] * n, out_shape=out_shape,
        scratch_shapes=[dma((n, 3)), dma((n, 3)), dma((n, 3)), dma((n, 3)), dma((n,)), dma((n,)), dma((n,))],
    )(*srcs)


WEIGHT_NAMES = ("meta_tokens", "norm_mix_g", "w_in", "ssd_conv_w", "ssd_conv_b", "ssd_dt_bias", "ssd_a_log", "ssd_d",
                "ssd_norm_g", "sb_norm_g", "mla_q_norm_g", "mla_kv_norm_g", "mla_w_uq", "mla_w_ukv", "mla_norm_g",
                "w_out", "norm_ffn_g", "ffn_w_up", "ffn_conv_w", "ffn_conv_b", "ffn_w_down", "final_norm_g")
SHARD_AXIS = {"meta_tokens": 1, "w_in": 2, "ssd_conv_w": 2, "mla_w_uq": 2, "mla_w_ukv": 2, "w_out": 1, "ffn_w_up": 2,
              "ffn_conv_w": 2, "ffn_w_down": 1}
SHARDED = tuple(n for n in WEIGHT_NAMES if n in SHARD_AXIS)
REPLICATED = tuple(n for n in WEIGHT_NAMES if n not in SHARD_AXIS)
GATHER_BF16 = ("w_in", "mla_w_uq", "mla_w_ukv", "w_out", "ffn_w_up", "ffn_w_down")
GATHER_F32 = ("meta_tokens", "ssd_conv_w", "ffn_conv_w")
PACK_ROWS = ROW_TILE


def pack(arrs, dtype):
    flat = jnp.concatenate([a.reshape(-1).astype(dtype) for a in arrs])
    per = PACK_ROWS * PACK_W
    total = -(-flat.size // per) * per
    return jnp.pad(flat, (0, total - flat.size)).reshape(total // PACK_W, PACK_W)


def unpack(buf, shapes):
    flat = buf.reshape(-1)
    out, off = [], 0
    for shp in shapes:
        size = int(np.prod(shp))
        out.append(flat[off:off + size].reshape(shp))
        off += size
    return out


def gather_weights(a):
    full = {n: a[n] for n in REPLICATED}
    bufs = [pack([a[n] for n in GATHER_BF16], BF16), pack([a[n] for n in GATHER_F32], F32)]
    got = chip_exchange(bufs, ("bcast_split", "bcast"), "gather_weights")
    for names, g in ((GATHER_BF16, got[0]), (GATHER_F32, got[1])):
        pieces = [unpack(g[k], [a[n].shape for n in names]) for k in range(N_CHIPS)]
        for idx, n in enumerate(names):
            full[n] = jnp.concatenate([pieces[k][idx] for k in range(N_CHIPS)], axis=SHARD_AXIS[n])
    return full


BIG = ("w_in", "w_out", "ffn_w_up", "ffn_w_down")
BIG_MODE = {"w_in": "slab", "w_out": "rows", "ffn_w_up": "cols", "ffn_w_down": "rows"}
SMALL_SHARDED = tuple(n for n in SHARDED if n not in BIG)
ADAM_TILE = 128


def _adamw(i, n, *vals):
    parts, (w, m, v) = vals[:2 * N_CHIPS], vals[2 * N_CHIPS:]
    g = parts[0].astype(F32)
    for p in parts[1:]:
        g = g + p.astype(F32)
    m = ADAM_B1 * m + (1.0 - ADAM_B1) * g
    v = ADAM_B2 * v + (1.0 - ADAM_B2) * jnp.square(g)
    m_hat = m / (1.0 - ADAM_B1 ** ADAM_STEP)
    v_hat = v / (1.0 - ADAM_B2 ** ADAM_STEP)
    delta = -ADAM_LR * (m_hat / (jnp.sqrt(v_hat) + ADAM_EPS) + ADAM_WD * w)
    return g, delta, m, v


def _adamw_call(got, w, m, v, name):
    rows, width = w.shape
    flat = got.reshape(2 * N_CHIPS * rows, width)
    blk = rows // ADAM_TILE
    ins = [RI(flat, rblk=k * blk) for k in range(2 * N_CHIPS)] + [RI(w), RI(m), RI(v)]
    return rowwise(_adamw, ins, [RO(width, F32)] * 4, name, rows, tm=ADAM_TILE)


def reduce_and_update(a, grads):
    srcs, modes = [], []
    for n in BIG:
        g = grads[n].astype(BF16)
        if n == "w_in":
            cs = a[n].shape[2]
            g = g.reshape(DEPTH, D_MODEL, N_CHIPS, cs).transpose(2, 0, 1, 3)
        srcs.append(g)
        modes.append(BIG_MODE[n])
    slabs = []
    for k in range(N_CHIPS):
        parts = []
        for n in SMALL_SHARDED:
            ax = SHARD_AXIS[n]
            size = a[n].shape[ax]
            parts.append(lax.slice_in_dim(grads[n], k * size, (k + 1) * size, axis=ax))
        slabs.append(pack(parts, BF16))
    srcs += [jnp.stack(slabs, axis=0), pack([grads[n] for n in REPLICATED], F32)]
    modes += ["slab", "bcast"]
    got = grad_exchange(srcs, modes, "exchange_grads")
    outs = {}
    kinds = ("grad", "delta", "new_m", "new_v")
    for n, g8 in zip(BIG, got):
        shp = a[n].shape
        rows = shp[0] * shp[1]
        flat = lambda t: t.reshape(rows, shp[2])
        res = _adamw_call(g8.reshape(2 * N_CHIPS, rows, shp[2]), flat(a[n]), flat(a["m_" + n]), flat(a["v_" + n]), "adamw_" + n)
        for kind, val in zip(kinds, res):
            outs[(kind, n)] = val.reshape(shp)
    for tag, names, g8 in (("small", SMALL_SHARDED, got[len(BIG)]), ("rep", REPLICATED, got[len(BIG) + 1])):
        shapes = [a[n].shape for n in names]
        packed = [pack([a[pre + n] for n in names], F32) for pre in ("", "m_", "v_")]
        res = _adamw_call(g8, *packed, "adamw_" + tag)
        for kind, buf in zip(kinds, res):
            for n, val in zip(names, unpack(buf, shapes)):
                outs[(kind, n)] = val
    return outs


INPUT_NAMES = ("x",) + WEIGHT_NAMES + ("loss_target",) + tuple("m_" + n for n in WEIGHT_NAMES) + tuple("v_" + n for n in WEIGHT_NAMES)


def kernel(x, meta_tokens, norm_mix_g, w_in, ssd_conv_w, ssd_conv_b, ssd_dt_bias, ssd_a_log, ssd_d, ssd_norm_g, sb_norm_g, mla_q_norm_g, mla_kv_norm_g, mla_w_uq, mla_w_ukv, mla_norm_g, w_out, norm_ffn_g, ffn_w_up, ffn_conv_w, ffn_conv_b, ffn_w_down, final_norm_g, loss_target, m_meta_tokens, m_norm_mix_g, m_w_in, m_ssd_conv_w, m_ssd_conv_b, m_ssd_dt_bias, m_ssd_a_log, m_ssd_d, m_ssd_norm_g, m_sb_norm_g, m_mla_q_norm_g, m_mla_kv_norm_g, m_mla_w_uq, m_mla_w_ukv, m_mla_norm_g, m_w_out, m_norm_ffn_g, m_ffn_w_up, m_ffn_conv_w, m_ffn_conv_b, m_ffn_w_down, m_final_norm_g, v_meta_tokens, v_norm_mix_g, v_w_in, v_ssd_conv_w, v_ssd_conv_b, v_ssd_dt_bias, v_ssd_a_log, v_ssd_d, v_ssd_norm_g, v_sb_norm_g, v_mla_q_norm_g, v_mla_kv_norm_g, v_mla_w_uq, v_mla_w_ukv, v_mla_norm_g, v_w_out, v_norm_ffn_g, v_ffn_w_up, v_ffn_conv_w, v_ffn_conv_b, v_ffn_w_down, v_final_norm_g):
    args = (x, meta_tokens, norm_mix_g, w_in, ssd_conv_w, ssd_conv_b, ssd_dt_bias, ssd_a_log, ssd_d, ssd_norm_g, sb_norm_g, mla_q_norm_g, mla_kv_norm_g, mla_w_uq, mla_w_ukv, mla_norm_g, w_out, norm_ffn_g, ffn_w_up, ffn_conv_w, ffn_conv_b, ffn_w_down, final_norm_g, loss_target, m_meta_tokens, m_norm_mix_g, m_w_in, m_ssd_conv_w, m_ssd_conv_b, m_ssd_dt_bias, m_ssd_a_log, m_ssd_d, m_ssd_norm_g, m_sb_norm_g, m_mla_q_norm_g, m_mla_kv_norm_g, m_mla_w_uq, m_mla_w_ukv, m_mla_norm_g, m_w_out, m_norm_ffn_g, m_ffn_w_up, m_ffn_conv_w, m_ffn_conv_b, m_ffn_w_down, m_final_norm_g, v_meta_tokens, v_norm_mix_g, v_w_in, v_ssd_conv_w, v_ssd_conv_b, v_ssd_dt_bias, v_ssd_a_log, v_ssd_d, v_ssd_norm_g, v_sb_norm_g, v_mla_q_norm_g, v_mla_kv_norm_g, v_mla_w_uq, v_mla_w_ukv, v_mla_norm_g, v_w_out, v_norm_ffn_g, v_ffn_w_up, v_ffn_conv_w, v_ffn_conv_b, v_ffn_w_down, v_final_norm_g)
    a = dict(zip(INPUT_NAMES, args, strict=True))
    full = gather_weights(a)
    loss, grad_x, grads = local_step(a["x"][0], a["loss_target"][0], full)
    loss = lax.psum(loss, ("x", "y", "c"))
    outs = reduce_and_update(a, grads)
    result = [loss, grad_x[None]]
    for kind in ("grad", "delta", "new_m", "new_v"):
        result += [outs[(kind, n)] for n in WEIGHT_NAMES]
    return tuple(result)
```

```python
import functools
import math

import numpy as np
import jax
import jax.numpy as jnp
from jax import lax
from jax.experimental import pallas as pl
from jax.experimental.pallas import tpu as pltpu

F32 = jnp.float32
BF16 = jnp.bfloat16
MESH_ID = pl.DeviceIdType.MESH

D_MODEL = 1024
DEPTH = 2
N_META = 16
EPS = 1e-6
SSD_HEADS = 8
SSD_WIDTH = 512
SSD_XBC = 1024
SSD_CONV = 4
SB_WIDTH = 256
SB_SCALE = 64 ** -0.5
MLA_Q_RANK = 192
MLA_KV_RANK = 128
MLA_ROPE = 32
MLA_SCALE = 96 ** -0.5
ROPE_BASE = 10000.0
D_FF = 2816
FFN_CONV = 3
IN_COLS = 2664
N_CHIPS = 4

ADAM_LR = 0.001
ADAM_B1 = 0.9
ADAM_B2 = 0.999
ADAM_EPS = 1e-08
ADAM_WD = 0.01
ADAM_STEP = 10

LANES = 128
SUBLANES = 8
ROW_TILE = 256
KEY_UNROLL = 4
WIDE_TILE = 768
VMEM_LIMIT = 56 * 1024 * 1024
PACK_W = 1024

U_XBC, U_Z, U_QA, U_CKV, U_KR4, U_DT, U_MAIN = 0, 1024, 1536, 1792, 1920, 2048, 2304
NEG = -1e30


def _cp(*sem):
    return pltpu.CompilerParams(dimension_semantics=sem if sem else None, vmem_limit_bytes=VMEM_LIMIT)


def _pick(dim, pref):
    if dim <= pref:
        return dim
    best = None
    for t in range(LANES, pref + 1, LANES):
        if dim % t == 0:
            best = t
    assert best is not None, (dim, pref)
    return best


def _dot(a, b, dims="nn", precision=None):
    dn = {"nn": (((1,), (0,)), ((), ())), "nt": (((1,), (1,)), ((), ())), "tn": (((0,), (0,)), ((), ()))}[dims]
    return lax.dot_general(a, b, dn, preferred_element_type=F32, precision=precision)


def _tri_dot(tri, x):
    hi = x.astype(BF16)
    r1 = x - hi.astype(F32)
    mid = r1.astype(BF16)
    lo = (r1 - mid.astype(F32)).astype(BF16)
    t = tri.astype(BF16)
    return _dot(t, hi) + _dot(t, mid) + _dot(t, lo)


def _softplus(x):
    return jnp.maximum(x, 0.0) + jnp.log1p(jnp.exp(-jnp.abs(x)))


def _silu(x):
    return x * jax.nn.sigmoid(x)


def _rms(x, g, n=None):
    n = x.shape[-1] if n is None else n
    ms = jnp.sum(x * x, axis=-1, keepdims=True) * (1.0 / n)
    return x * lax.rsqrt(ms + EPS) * g


def mm(a, b, dims, out_dtype, name, add=None, tm=None, tn=None, tk=None):
    if dims == "nn":
        (m, k), (k2, n) = a.shape, b.shape
    elif dims == "nt":
        (m, k), (n, k2) = a.shape, b.shape
    else:
        (k, m), (k2, n) = a.shape, b.shape
    assert k == k2, (a.shape, b.shape, dims)
    if dims == "tn":
        tm, tn, tk = _pick(m, tm or 1408), _pick(n, tn or 1408), _pick(k, tk or 1408)
    else:
        tm, tn, tk = _pick(m, tm or (1408 if k <= 2304 else 768)), _pick(n, tn or 1408), _pick(k, tk or 2816)
    nk = k // tk
    if dims == "tn":
        a_spec = pl.BlockSpec((tk, tm), lambda j, i, kk: (kk, i))
    else:
        a_spec = pl.BlockSpec((tm, tk), lambda j, i, kk: (i, kk))
    if dims == "nt":
        b_spec = pl.BlockSpec((tn, tk), lambda j, i, kk: (j, kk))
    else:
        b_spec = pl.BlockSpec((tk, tn), lambda j, i, kk: (kk, j))
    o_spec = pl.BlockSpec((tm, tn), lambda j, i, kk: (i, j))
    has_add = add is not None

    def body(*refs):
        a_ref, b_ref = refs[0], refs[1]
        add_ref = refs[2] if has_add else None
        o_ref = refs[3] if has_add else refs[2]
        part = _dot(a_ref[...].astype(BF16), b_ref[...].astype(BF16), dims)

        def finish(r):
            if has_add:
                r = r + add_ref[...].astype(F32)
            o_ref[...] = r.astype(o_ref.dtype)

        if nk == 1:
            finish(part)
            return
        acc_ref = refs[-1]
        kk = pl.program_id(2)

        @pl.when(kk == 0)
        def _():
            acc_ref[...] = part

        @pl.when(jnp.logical_and(kk > 0, kk < nk - 1))
        def _():
            acc_ref[...] += part

        @pl.when(kk == nk - 1)
        def _():
            finish(acc_ref[...] + part)

    in_specs = [a_spec, b_spec] + ([o_spec] if has_add else [])
    args = (a, b) + ((add,) if has_add else ())
    return pl.pallas_call(
        body, name=name, grid=(n // tn, m // tm, nk),
        in_specs=in_specs, out_specs=o_spec,
        out_shape=jax.ShapeDtypeStruct((m, n), out_dtype),
        scratch_shapes=[pltpu.VMEM((tm, tn), F32)] if nk > 1 else [],
        compiler_params=_cp("parallel", "parallel", "arbitrary"),
    )(*args)


def RI(arr, width=None, cidx=0, cv=False, rblk=0):
    return ("row" if rblk == 0 else ("row", rblk), arr, arr.shape[1] if width is None else width, cidx, cv)


def HP(arr, width=None, cidx=0, cv=False):
    return ("prev", arr, arr.shape[1] if width is None else width, cidx, cv)


def HN(arr, width=None, cidx=0, cv=False):
    return ("next", arr, arr.shape[1] if width is None else width, cidx, cv)


def PA(arr, width=None, cidx=0, cv=False):
    return ("par", arr, arr.shape[1] if width is None else width, cidx, cv)


def RO(ncols, dtype, width=None, cv=False):
    return ("row", ncols, dtype, ncols if width is None else width, cv)


def AO(nrows, ncols, width=None, cv=False):
    return ("acc", (nrows, ncols), F32, ncols if width is None else width, cv)


def rowwise(fn, ins, outs, name, rows, tm=ROW_TILE, ncol=1):
    tm = min(tm, rows)
    assert rows % tm == 0
    nrow = rows // tm
    hb = tm // SUBLANES
    last_hb = rows // SUBLANES - 1
    in_specs, args = [], []
    for kind, arr, width, cidx, cv in ins:
        def cmap(j, cidx=cidx, cv=cv):
            return cidx + j if cv else cidx
        if kind == "row":
            spec = pl.BlockSpec((tm, width), lambda j, i, cmap=cmap: (i, cmap(j)))
        elif isinstance(kind, tuple):
            spec = pl.BlockSpec((tm, width), lambda j, i, cmap=cmap, rblk=kind[1]: (i + rblk, cmap(j)))
        elif kind == "prev":
            spec = pl.BlockSpec((SUBLANES, width), lambda j, i, cmap=cmap: (jnp.maximum(i * hb - 1, 0), cmap(j)))
        elif kind == "next":
            spec = pl.BlockSpec((SUBLANES, width), lambda j, i, cmap=cmap: (jnp.minimum((i + 1) * hb, last_hb), cmap(j)))
        else:
            spec = pl.BlockSpec((arr.shape[0], width), lambda j, i, cmap=cmap: (0, cmap(j)))
        in_specs.append(spec)
        args.append(arr)
    out_specs, out_shapes, acc_cv = [], [], []
    for kind, shp, dtype, width, cv in outs:
        if kind == "row":
            out_specs.append(pl.BlockSpec((tm, width), lambda j, i, cv=cv: (i, j if cv else 0)))
            out_shapes.append(jax.ShapeDtypeStruct((rows, shp), dtype))
            acc_cv.append(None)
        else:
            out_specs.append(pl.BlockSpec((shp[0], width), lambda j, i, cv=cv: (0, j if cv else 0)))
            out_shapes.append(jax.ShapeDtypeStruct(shp, dtype))
            acc_cv.append(cv)
    n_in = len(ins)

    def body(*refs):
        j = pl.program_id(0)
        i = pl.program_id(1)
        vals = fn(i, nrow, *[r[...] for r in refs[:n_in]])
        if not isinstance(vals, (tuple, list)):
            vals = (vals,)
        for o_ref, v, cv in zip(refs[n_in:], vals, acc_cv):
            if cv is None:
                o_ref[...] = v.astype(o_ref.dtype)
            else:
                first = (i == 0) if cv else jnp.logical_and(i == 0, j == 0)

                @pl.when(first)
                def _(o_ref=o_ref, v=v):
                    o_ref[...] = v.astype(o_ref.dtype)

                @pl.when(jnp.logical_not(first))
                def _(o_ref=o_ref, v=v):
                    o_ref[...] += v.astype(o_ref.dtype)

    res = pl.pallas_call(
        body, name=name, grid=(ncol, nrow), in_specs=in_specs, out_specs=out_specs, out_shape=out_shapes,
        compiler_params=_cp("arbitrary", "arbitrary"),
    )(*args)
    return res


def _rows_iota(x):
    return lax.broadcasted_iota(jnp.int32, x.shape, 0)


def shift_down(x, halo, s):
    if s == 0:
        return x
    tm = x.shape[0]
    top = pltpu.roll(halo, s, 0)
    if tm > SUBLANES:
        top = jnp.concatenate([top, jnp.zeros((tm - SUBLANES, x.shape[1]), x.dtype)], axis=0)
    return jnp.where(_rows_iota(x) < s, top, pltpu.roll(x, s, 0))


def shift_up(x, halo, s):
    if s == 0:
        return x
    tm = x.shape[0]
    bot = pltpu.roll(halo, SUBLANES - s, 0)
    if tm > SUBLANES:
        bot = jnp.concatenate([jnp.zeros((tm - SUBLANES, x.shape[1]), x.dtype), bot], axis=0)
    return jnp.where(_rows_iota(x) >= tm - s, bot, pltpu.roll(x, tm - s, 0))


def conv_fwd(x, halo, w, i):
    kw = w.shape[0]
    halo = jnp.where(i == 0, 0.0, halo)
    out = None
    for k in range(kw):
        term = w[k:k + 1, :] * shift_down(x, halo, kw - 1 - k)
        out = term if out is None else out + term
    return out


def conv_bwd_data(dy, halo_next, w, i, n):
    kw = w.shape[0]
    halo_next = jnp.where(i == n - 1, 0.0, halo_next)
    out = None
    for k in range(kw):
        term = w[k:k + 1, :] * shift_up(dy, halo_next, kw - 1 - k)
        out = term if out is None else out + term
    return out


def conv_bwd_w(dy, x, halo, i, kw):
    halo = jnp.where(i == 0, 0.0, halo)
    rows = [jnp.sum(dy * shift_down(x, halo, kw - 1 - k), axis=0, keepdims=True) for k in range(kw)]
    return jnp.concatenate(rows, axis=0)


def _lane(shape):
    return lax.broadcasted_iota(jnp.int32, shape, 1)


def rope_rot(x):
    lane = _lane(x.shape) % MLA_ROPE
    return jnp.where(lane < MLA_ROPE // 2, -pltpu.roll(x, LANES - MLA_ROPE // 2, 1), pltpu.roll(x, MLA_ROPE // 2, 1))


def rope_rot_t(g):
    lane = _lane(g.shape) % MLA_ROPE
    return jnp.where(lane < MLA_ROPE // 2, pltpu.roll(g, LANES - MLA_ROPE // 2, 1), -pltpu.roll(g, MLA_ROPE // 2, 1))


def _ssd_common(g, xs, dt_raw, bias, alog, q):
    lane = _lane((q, LANES))
    pre = dt_raw + bias
    dt = jnp.where(lane < SSD_HEADS, _softplus(pre), 0.0)
    a_row = -jnp.exp(alog)
    d_a = dt * a_row
    ri = lax.broadcasted_iota(jnp.int32, (q, q), 0)
    ci = lax.broadcasted_iota(jnp.int32, (q, q), 1)
    causal = ri >= ci
    acs = _tri_dot(causal, d_a)
    acs_t = acs.T
    subl = lax.broadcasted_iota(jnp.int32, (LANES, q), 0)
    heads = [4 * g + i for i in range(4)]
    lo = lane < 64

    def col(arr, h):
        return jnp.sum(jnp.where(lane == h, arr, 0.0), axis=1, keepdims=True)

    def lanes4(v):
        m = lo if v[0].shape[0] == q else lo[0:1, :]
        return jnp.concatenate([jnp.where(m, v[0], v[1]), jnp.where(m, v[2], v[3])], axis=1)

    cols = [col(acs, h) for h in heads]
    rows = [jnp.sum(jnp.where(subl == h, acs_t, 0.0), axis=0, keepdims=True) for h in heads]
    tots = [c_[q - 1:q, :] for c_ in cols]
    acs4 = lanes4(cols)
    dt4 = lanes4([col(dt, h) for h in heads])
    lms = [jnp.exp(jnp.where(causal, cols[i] - rows[i], NEG)) for i in range(4)]
    lane4 = _lane((q, 2 * LANES))
    hm = [jnp.logical_and(lane4 >= 64 * i, lane4 < 64 * (i + 1)) for i in range(4)]
    return dict(lane=lane, lo=lo, pre=pre, dt=dt, a_row=a_row, heads=heads, tots=tots, lms=lms, ri=ri, ci=ci, hm=hm,
                lanes4=lanes4, eacs=jnp.exp(acs4), dte=jnp.exp(lanes4(tots) - acs4), dt4=dt4, x=xs * dt4)


def _pick_lane(row_arr, h):
    return jnp.sum(jnp.where(_lane(row_arr.shape) == h, row_arr, 0.0), axis=1, keepdims=True)


def _etot(tots):
    sub = lax.broadcasted_iota(jnp.int32, (2 * LANES, LANES), 0)
    e = [jnp.exp(t) for t in tots]
    return jnp.where(sub < 64, e[0], jnp.where(sub < 128, e[1], jnp.where(sub < 192, e[2], e[3]))), e


def _half(arr, i, lo):
    slab = arr[:, LANES * (i // 2):LANES * (i // 2 + 1)]
    return jnp.where(lo, slab, 0.0) if i % 2 == 0 else jnp.where(lo, 0.0, slab)


def ssd_fwd(xbc_c, u_main, bias_row, alog_row, d_row, name):
    lp = xbc_c.shape[0]
    q = min(ROW_TILE, lp)
    nc = lp // q
    dt_blk = U_DT // LANES

    def body(xs_ref, b_ref, c_ref, dt_ref, bias_ref, alog_ref, d_ref, y_ref, hp_ref, h_scr):
        c = pl.program_id(0)

        @pl.when(c == 0)
        def _():
            h_scr[...] = jnp.zeros_like(h_scr)

        ss, bbs, cbs, xss = [], [], [], []
        for g in range(2):
            xs = xs_ref[:, 2 * LANES * g:2 * LANES * (g + 1)]
            xss.append(xs)
            bbs.append(b_ref[:, LANES * g:LANES * (g + 1)].astype(BF16))
            cbs.append(c_ref[:, LANES * g:LANES * (g + 1)].astype(BF16))
            ss.append(_ssd_common(g, xs, dt_ref[...], bias_ref[...], alog_ref[...], q))
        gmats = [_dot(cbs[g], bbs[g], "nt") for g in range(2)]
        ms = [[(gmats[g] * ss[g]["lms"][i]).astype(BF16) for i in range(4)] for g in range(2)]
        xjs = [[_half(ss[g]["x"], i, ss[g]["lo"]).astype(BF16) for i in range(4)] for g in range(2)]
        ys = [[_dot(ms[g][i], xjs[g][i]) for i in range(4)] for g in range(2)]
        hps = [h_scr[g] for g in range(2)]
        yoffs = [_dot(cbs[g], hps[g].astype(BF16), "nt") * ss[g]["eacs"] for g in range(2)]
        upd = [_dot((ss[g]["x"] * ss[g]["dte"]).astype(BF16), bbs[g], "tn") for g in range(2)]
        outs = []
        for g in range(2):
            s = ss[g]
            hp_ref[g] = hps[g]
            d4 = s["lanes4"]([_pick_lane(d_ref[...], h) for h in s["heads"]])
            outs.append(jnp.concatenate([ys[g][0] + ys[g][1], ys[g][2] + ys[g][3]], axis=1) + yoffs[g] + d4 * xss[g])
            etot, _ = _etot(s["tots"])
            h_scr[g] = hps[g] * etot + upd[g]
        y_ref[...] = jnp.concatenate(outs, axis=1)

    in_specs = [
        pl.BlockSpec((q, 4 * LANES), lambda c: (c, 0)),
        pl.BlockSpec((q, 2 * LANES), lambda c: (c, 2)),
        pl.BlockSpec((q, 2 * LANES), lambda c: (c, 3)),
        pl.BlockSpec((q, LANES), lambda c: (c, dt_blk)),
        pl.BlockSpec((1, LANES), lambda c: (0, 0)),
        pl.BlockSpec((1, LANES), lambda c: (0, 0)),
        pl.BlockSpec((1, LANES), lambda c: (0, 0)),
    ]
    out_specs = [
        pl.BlockSpec((q, 4 * LANES), lambda c: (c, 0)),
        pl.BlockSpec((2, None, 2 * LANES, LANES), lambda c: (0, c, 0, 0)),
    ]
    return pl.pallas_call(
        body, name=name, grid=(nc,), in_specs=in_specs, out_specs=out_specs,
        out_shape=[jax.ShapeDtypeStruct((lp, SSD_WIDTH), F32), jax.ShapeDtypeStruct((2, nc, 2 * LANES, LANES), F32)],
        scratch_shapes=[pltpu.VMEM((2, 2 * LANES, LANES), F32)],
        compiler_params=_cp("arbitrary"),
    )(xbc_c, xbc_c, xbc_c, u_main, bias_row, alog_row, d_row)


def ssd_bwd(xbc_c, u_main, bias_row, alog_row, d_row, hprev, dy, name):
    lp = xbc_c.shape[0]
    q = min(ROW_TILE, lp)
    nc = lp // q
    dt_blk = U_DT // LANES
    G = range(2)

    def body(xs_ref, b_ref, c_ref, dt_ref, bias_ref, alog_ref, d_ref, hp_ref, dy_ref,
             dxs_ref, db_ref, dc_ref, ddt_ref, pg_ref, dh_scr):
        cc = pl.program_id(0)

        @pl.when(cc == 0)
        def _():
            dh_scr[...] = jnp.zeros_like(dh_scr)
            pg_ref[...] = jnp.zeros_like(pg_ref)

        wide = [slice(2 * LANES * g, 2 * LANES * (g + 1)) for g in G]
        one = [slice(LANES * g, LANES * (g + 1)) for g in G]
        xs = [xs_ref[:, wide[g]] for g in G]
        bb = [b_ref[:, one[g]].astype(BF16) for g in G]
        cb_ = [c_ref[:, one[g]].astype(BF16) for g in G]
        s = [_ssd_common(g, xs[g], dt_ref[...], bias_ref[...], alog_ref[...], q) for g in G]
        d_y = [dy_ref[:, wide[g]] for g in G]
        hp = [hp_ref[g] for g in G]
        hpb = [hp[g].astype(BF16) for g in G]
        dhn = [dh_scr[g] for g in G]
        dhnb = [dhn[g].astype(BF16) for g in G]
        x = [s[g]["x"] for g in G]
        xd = [x[g] * s[g]["dte"] for g in G]
        gmat = [_dot(cb_[g], bb[g], "nt") for g in G]
        m32s = [[gmat[g] * s[g]["lms"][i] for i in range(4)] for g in G]
        xjs = [[_half(x[g], i, s[g]["lo"]).astype(BF16) for i in range(4)] for g in G]
        dyjs = [[_half(d_y[g], i, s[g]["lo"]).astype(BF16) for i in range(4)] for g in G]
        dxparts = [[_dot(m32s[g][i].astype(BF16), dyjs[g][i], "tn") for i in range(4)] for g in G]
        dms = [[_dot(dyjs[g][i], xjs[g][i], "nt") for i in range(4)] for g in G]
        yoff = [_dot(cb_[g], hpb[g], "nt") * s[g]["eacs"] for g in G]
        d_t = [(d_y[g] * s[g]["eacs"]).astype(BF16) for g in G]
        d_hp = [_dot(d_t[g], cb_[g], "tn") for g in G]
        dxd = [_dot(bb[g], dhnb[g], "nt") for g in G]
        for g in G:
            sg, lane, hm, heads = s[g], s[g]["lane"], s[g]["hm"], s[g]["heads"]
            dg = (dms[g][0] * sg["lms"][0] + dms[g][1] * sg["lms"][1] + dms[g][2] * sg["lms"][2] + dms[g][3] * sg["lms"][3])
            wms = [dms[g][i] * m32s[g][i] for i in range(4)]
            row_part = [jnp.sum(wm, axis=1, keepdims=True) for wm in wms]
            col_part = [jnp.sum(wm, axis=0, keepdims=True) for wm in wms]
            dgb = dg.astype(BF16)
            d_c = _dot(dgb, bb[g]) + _dot(d_t[g], hpb[g])
            d_b = _dot(dgb, cb_[g], "tn") + _dot(xd[g].astype(BF16), dhnb[g])
            d_x = jnp.concatenate([dxparts[g][0] + dxparts[g][1], dxparts[g][2] + dxparts[g][3]], axis=1) + dxd[g] * sg["dte"]
            r = dxd[g] * xd[g]
            a_terms = d_y[g] * yoff[g] - r

            def hsum(arr):
                return [jnp.sum(jnp.where(hm[i], arr, 0.0), axis=1, keepdims=True) for i in range(4)]

            dacs, rs = hsum(a_terms), hsum(r)
            hh = dhn[g] * hp[g]
            sub = lax.broadcasted_iota(jnp.int32, (2 * LANES, LANES), 0)
            hsums = [jnp.sum(jnp.where(jnp.logical_and(sub >= 64 * i, sub < 64 * (i + 1)), hh, 0.0), keepdims=True) for i in range(4)]
            last = lax.broadcasted_iota(jnp.int32, (q, 1), 0) == q - 1
            etot, etots = _etot(sg["tots"])
            ddacs = jnp.zeros((q, LANES), F32)
            for i, h in enumerate(heads):
                dtot = jnp.sum(rs[i], keepdims=True) + hsums[i] * etots[i]
                ddacs = ddacs + jnp.where(lane == h, dacs[i] + row_part[i] + jnp.where(last, dtot, 0.0), 0.0)
            subl = lax.broadcasted_iota(jnp.int32, (LANES, q), 0)
            cols_t = jnp.zeros((LANES, q), F32)
            for i, h in enumerate(heads):
                cols_t = cols_t + jnp.where(subl == h, col_part[i], 0.0)
            ddacs = ddacs - cols_t.T
            da = _tri_dot(sg["ri"] <= sg["ci"], ddacs)
            ddt_own = hsum(d_x * xs[g])
            ddt = da * sg["a_row"]
            for i, h in enumerate(heads):
                ddt = ddt + jnp.where(lane == h, ddt_own[i], 0.0)
            draw = ddt * jax.nn.sigmoid(sg["pre"])
            ddt_ref[:, one[g]] = draw
            d4 = sg["lanes4"]([_pick_lane(d_ref[...], h) for h in heads])
            dxs_ref[:, wide[g]] = d4 * d_y[g] + d_x * sg["dt4"]
            db_ref[:, one[g]] = d_b
            dc_ref[:, one[g]] = d_c
            dds = hsum(d_y[g] * xs[g])
            lane1 = lane[0:1, :]
            dd_row = jnp.zeros((1, LANES), F32)
            for i, h in enumerate(heads):
                dd_row = dd_row + jnp.where(lane1 == h, jnp.sum(dds[i], keepdims=True), 0.0)
            dbias_row = jnp.sum(draw, axis=0, keepdims=True)
            dalog_row = jnp.sum(da * sg["dt"], axis=0, keepdims=True) * sg["a_row"]
            sub8 = lax.broadcasted_iota(jnp.int32, (SUBLANES, LANES), 0)
            pg_ref[SUBLANES * g:SUBLANES * (g + 1), :] += (jnp.where(sub8 == 0, dbias_row, 0.0) + jnp.where(sub8 == 1, dalog_row, 0.0)
                                                          + jnp.where(sub8 == 2, dd_row, 0.0))
            dh_scr[g] = d_hp[g] + etot * dhn[g]

    rc = lambda c: nc - 1 - c
    in_specs = [
        pl.BlockSpec((q, 4 * LANES), lambda c: (rc(c), 0)),
        pl.BlockSpec((q, 2 * LANES), lambda c: (rc(c), 2)),
        pl.BlockSpec((q, 2 * LANES), lambda c: (rc(c), 3)),
        pl.BlockSpec((q, LANES), lambda c: (rc(c), dt_blk)),
        pl.BlockSpec((1, LANES), lambda c: (0, 0)),
        pl.BlockSpec((1, LANES), lambda c: (0, 0)),
        pl.BlockSpec((1, LANES), lambda c: (0, 0)),
        pl.BlockSpec((2, None, 2 * LANES, LANES), lambda c: (0, rc(c), 0, 0)),
        pl.BlockSpec((q, 4 * LANES), lambda c: (rc(c), 0)),
    ]
    out_specs = [
        pl.BlockSpec((q, 4 * LANES), lambda c: (rc(c), 0)),
        pl.BlockSpec((q, 2 * LANES), lambda c: (rc(c), 0)),
        pl.BlockSpec((q, 2 * LANES), lambda c: (rc(c), 0)),
        pl.BlockSpec((q, 2 * LANES), lambda c: (rc(c), 0)),
        pl.BlockSpec((2 * SUBLANES, LANES), lambda c: (0, 0)),
    ]
    per_group = jax.ShapeDtypeStruct((lp, 2 * LANES), F32)
    return pl.pallas_call(
        body, name=name, grid=(nc,), in_specs=in_specs, out_specs=out_specs,
        out_shape=[jax.ShapeDtypeStruct((lp, SSD_WIDTH), F32), per_group, per_group, per_group,
                   jax.ShapeDtypeStruct((2 * SUBLANES, LANES), F32)],
        scratch_shapes=[pltpu.VMEM((2, 2 * LANES, LANES), F32)],
        compiler_params=_cp("arbitrary"),
    )(xbc_c, xbc_c, xbc_c, u_main, bias_row, alog_row, d_row, hprev, dy)


def _sb_blocks(qs, ks, r_runs, masked, bq, after_scores=None):
    ri = lax.broadcasted_iota(jnp.int32, (bq, bq), 0)
    ci = lax.broadcasted_iota(jnp.int32, (bq, bq), 1)
    tri_after = (ri > ci).astype(BF16)
    zs = [_dot(qj, kj, "nt") for qj, kj in zip(qs, ks)]
    extra = after_scores() if after_scores is not None else None
    sigs, ubs = [], []
    for z in zs:
        zb = z.astype(BF16)
        u = -(jnp.maximum(zb, 0) + jnp.log(1 + jnp.exp(-jnp.abs(zb))))
        sigs.append(jnp.exp(zb + u))
        if masked:
            u = jnp.where(ci < ri, u, jnp.zeros_like(u))
        ubs.append(u)
    afters = [_dot(ub, tri_after) for ub in ubs]
    usums = [after[:, 0:1] + ub[:, 0:1].astype(F32) for after, ub in zip(afters, ubs)]
    ws = []
    for sig, after, r_run in zip(sigs, afters, r_runs):
        w = sig * jnp.exp(after + r_run).astype(BF16)
        if masked:
            w = jnp.where(ci < ri, w, jnp.zeros_like(w))
        ws.append(w)
    return usums, sigs, ws, extra


def _split_heads(x, lo):
    out = []
    zero = jnp.zeros((x.shape[0], LANES), x.dtype)
    for p in range(2):
        xp = x[:, LANES * p:LANES * (p + 1)]
        out += [jnp.where(lo, xp, zero), jnp.where(lo, zero, xp)]
    return out


def _per_head(x):
    return [x[:, :LANES], x[:, :LANES], x[:, LANES:], x[:, LANES:]]


def _resident(shape, col):
    return pl.BlockSpec(shape, lambda i: (0, col), pipeline_mode=pl.Buffered(1))


def sb_attn_fwd(qkv, name):
    lp = qkv.shape[0]
    bq = min(ROW_TILE, lp)
    nq = lp // bq
    assert nq <= 64

    def body(q_ref, k_ref, v_ref, o_ref, rs_ref):
        qi = pl.program_id(0)
        lane = _lane((bq, LANES))
        lo = lane < 64
        qs = _split_heads(q_ref[...], lo)

        def step(kb, carry, masked):
            off = pl.multiple_of(kb * bq, bq)
            ks = _per_head(k_ref[pl.ds(off, bq), :])
            vs = _per_head(v_ref[pl.ds(off, bq), :])
            heads, rss = carry
            r_runs = [heads[h][1] for h in range(4)]
            rss = list(rss)
            for h in range(4):
                rss[h // 2] = jnp.where(lane == 64 * (h % 2) + kb, r_runs[h], rss[h // 2])
            usums, _, ws, _ = _sb_blocks(qs, ks, r_runs, masked, bq)
            pvs = [_dot(ws[h], vs[h]) for h in range(4)]
            out = tuple((heads[h][0] + pvs[h], r_runs[h] + usums[h]) for h in range(4))
            return out, tuple(rss)

        zero = (jnp.zeros((bq, LANES), F32), jnp.zeros((bq, 1), F32))
        zr = jnp.zeros((bq, LANES), F32)
        carry = step(qi, ((zero,) * 4, (zr, zr)), True)
        def several(t, c):
            for r in range(KEY_UNROLL):
                c = step(qi - 1 - r - KEY_UNROLL * t, c, False)
            return c

        carry = lax.fori_loop(0, qi // KEY_UNROLL, several, carry)
        rem = qi % KEY_UNROLL
        heads, rss = lax.fori_loop(0, rem, lambda t, c: step(rem - 1 - t, c, False), carry)
        o_ref[...] = jnp.concatenate([jnp.where(lo, heads[0][0], heads[1][0]), jnp.where(lo, heads[2][0], heads[3][0])], axis=1)
        rs_ref[...] = jnp.concatenate(list(rss), axis=1)

    blk = pl.BlockSpec((bq, 2 * LANES), lambda i: (i, 0))
    return pl.pallas_call(
        body, name=name, grid=(nq,),
        in_specs=[blk, _resident((lp, 2 * LANES), 1), _resident((lp, 2 * LANES), 2)],
        out_specs=[blk, blk],
        out_shape=[jax.ShapeDtypeStruct((lp, SB_WIDTH), F32), jax.ShapeDtypeStruct((lp, SB_WIDTH), F32)],
        compiler_params=_cp("arbitrary"),
    )(qkv, qkv, qkv)


def sb_attn_bwd(qkv, rs, d_o, name):
    lp = qkv.shape[0]
    bq = min(ROW_TILE, lp)
    nq = lp // bq

    def body(q_ref, k_ref, v_ref, rs_ref, do_ref, dq_ref, dk_ref, dv_ref):
        qi = pl.program_id(0)

        @pl.when(qi == 0)
        def _():
            dk_ref[...] = jnp.zeros_like(dk_ref)
            dv_ref[...] = jnp.zeros_like(dv_ref)

        lane = _lane((bq, LANES))
        lo = lane < 64
        qs = _split_heads(q_ref[...], lo)
        dos = _split_heads(do_ref[...].astype(BF16), lo)
        rs_blk = rs_ref[...]
        ri = lax.broadcasted_iota(jnp.int32, (bq, bq), 0)
        ci = lax.broadcasted_iota(jnp.int32, (bq, bq), 1)
        tbefore = (ri < ci).astype(BF16)

        def step(kb, carry, masked):
            off = pl.multiple_of(kb * bq, bq)
            ks = _per_head(k_ref[pl.ds(off, bq), :])
            vs = _per_head(v_ref[pl.ds(off, bq), :])
            r_rights = [jnp.sum(jnp.where(lane == 64 * (h % 2) + kb, rs_blk[:, LANES * (h // 2):LANES * (h // 2 + 1)], 0.0),
                                axis=1, keepdims=True) for h in range(4)]
            _, sigs, wbs, dws = _sb_blocks(qs, ks, r_rights, masked, bq,
                                           after_scores=lambda: [_dot(dos[h], vs[h], "nt") for h in range(4)])
            gs = [wbs[h].astype(F32) * dws[h] for h in range(4)]
            gbs = [g.astype(BF16) for g in gs]
            gbefores = [_dot(gb, tbefore) for gb in gbs]
            dv_acc = [_dot(wbs[2 * p], dos[2 * p], "tn") + _dot(wbs[2 * p + 1], dos[2 * p + 1], "tn") for p in range(2)]
            dzbs = []
            for h in range(4):
                dz = gs[h] - sigs[h].astype(F32) * (gs[h] + gbefores[h] + carry[h][1])
                if masked:
                    dz = jnp.where(ci < ri, dz, 0.0)
                dzbs.append(dz.astype(BF16))
            dqs = [_dot(dzbs[h], ks[h]) for h in range(4)]
            dk_acc = [_dot(dzbs[2 * p], qs[2 * p], "tn") + _dot(dzbs[2 * p + 1], qs[2 * p + 1], "tn") for p in range(2)]
            dk_ref[pl.ds(off, bq), :] += jnp.concatenate(dk_acc, axis=1)
            dv_ref[pl.ds(off, bq), :] += jnp.concatenate(dv_acc, axis=1)
            return tuple((carry[h][0] + dqs[h], carry[h][1] + jnp.sum(gs[h], axis=1, keepdims=True)) for h in range(4))

        zero = (jnp.zeros((bq, LANES), F32), jnp.zeros((bq, 1), F32))
        def several(t, c):
            for r in range(KEY_UNROLL):
                c = step(KEY_UNROLL * t + r, c, False)
            return c

        carry = lax.fori_loop(0, qi // KEY_UNROLL, several, (zero,) * 4)
        carry = lax.fori_loop(qi - qi % KEY_UNROLL, qi, lambda t, c: step(t, c, False), carry)
        carry = step(qi, carry, True)
        dq_ref[...] = jnp.concatenate([jnp.where(lo, carry[0][0], carry[1][0]), jnp.where(lo, carry[2][0], carry[3][0])],
                                      axis=1).astype(dq_ref.dtype)

    blk = pl.BlockSpec((bq, 2 * LANES), lambda i: (i, 0))
    return pl.pallas_call(
        body, name=name, grid=(nq,),
        in_specs=[blk, _resident((lp, 2 * LANES), 1), _resident((lp, 2 * LANES), 2), blk, blk],
        out_specs=[blk, _resident((lp, 2 * LANES), 0), _resident((lp, 2 * LANES), 0)],
        out_shape=[jax.ShapeDtypeStruct((lp, SB_WIDTH), BF16), jax.ShapeDtypeStruct((lp, SB_WIDTH), F32),
                   jax.ShapeDtypeStruct((lp, SB_WIDTH), F32)],
        compiler_params=_cp("arbitrary"),
    )(qkv, qkv, qkv, rs, d_o)


def _mla_masks(bq):
    lane = _lane((bq, 2 * LANES))
    out = []
    for h in range(4):
        j = h % 2
        nope = jnp.logical_and(lane >= 64 * j, lane < 64 * (j + 1))
        rope = jnp.logical_and(lane >= LANES + MLA_ROPE * h, lane < LANES + MLA_ROPE * (h + 1))
        out.append(jnp.logical_or(nope, rope))
    return out


def _mla_split_q(q, masks):
    zero = jnp.zeros((q.shape[0], 2 * LANES), q.dtype)
    return [jnp.where(masks[h], q[:, 2 * LANES * (h // 2):2 * LANES * (h // 2 + 1)], zero) for h in range(4)]


def _mla_per_head_k(k):
    return [k[:, :2 * LANES], k[:, :2 * LANES], k[:, 2 * LANES:], k[:, 2 * LANES:]]


def mla_attn_fwd(qc, kc, v, name):
    lp = qc.shape[0]
    bq = min(ROW_TILE, lp)
    nq = lp // bq

    def body(q_ref, k_ref, v_ref, o_ref, lse_ref):
        qi = pl.program_id(0)
        qs = _mla_split_q(q_ref[...], _mla_masks(bq))
        lo = _lane((bq, LANES)) < 64
        ri = lax.broadcasted_iota(jnp.int32, (bq, bq), 0)
        ci = lax.broadcasted_iota(jnp.int32, (bq, bq), 1)

        def blocks(kbs, carry, masked):
            offs = [pl.multiple_of(kb * bq, bq) for kb in kbs]
            ks = [_mla_per_head_k(k_ref[pl.ds(o, bq), :]) for o in offs]
            ones = jnp.ones((bq, LANES), BF16)
            vs = []
            for o in offs:
                vp = _per_head(v_ref[pl.ds(o, bq), :])
                vs.append([jnp.where(lo, vp[h], ones) if h % 2 == 0 else jnp.where(lo, ones, vp[h]) for h in range(4)])
            ss = [[_dot(qs[h], ks[b][h], "nt") for h in range(4)] for b in range(len(kbs))]
            if masked:
                ss = [[jnp.where(ci <= ri, s, NEG) for s in row] for row in ss]
            prs, alphas, ms = [], [], []
            for h in range(4):
                top = ss[0][h]
                for b in range(1, len(kbs)):
                    top = jnp.maximum(top, ss[b][h])
                m_new = jnp.maximum(carry[h][1], jnp.max(top, axis=1, keepdims=True))
                alphas.append(jnp.exp(carry[h][1] - m_new))
                ms.append(m_new)
                prs.append([jnp.exp(ss[b][h] - m_new).astype(BF16) for b in range(len(kbs))])
            out = []
            for h in range(4):
                acc = carry[h][0] * alphas[h]
                for b in range(len(kbs)):
                    acc = acc + _dot(prs[h][b], vs[b][h])
                out.append((acc, ms[h]))
            return tuple(out)

        zero = (jnp.zeros((bq, LANES), F32), jnp.full((bq, 1), NEG, F32))
        carry = blocks([qi], (zero,) * 4, True)
        carry = lax.fori_loop(0, qi // KEY_UNROLL,
                              lambda t, c: blocks([qi - 1 - r - KEY_UNROLL * t for r in range(KEY_UNROLL)], c, False), carry)
        rem = qi % KEY_UNROLL
        carry = lax.fori_loop(0, rem, lambda t, c: blocks([rem - 1 - t], c, False), carry)
        outs, lses = [], []
        for h in range(4):
            acc, m = carry[h]
            l = acc[:, 64:65] if h % 2 == 0 else acc[:, 0:1]
            outs.append(acc / l)
            lses.append(m + jnp.log(l))
        o_ref[...] = jnp.concatenate([jnp.where(lo, outs[0], outs[1]), jnp.where(lo, outs[2], outs[3])], axis=1)
        lse_ref[...] = jnp.concatenate([jnp.where(lo, lses[0], lses[1]), jnp.where(lo, lses[2], lses[3])], axis=1)

    blk = pl.BlockSpec((bq, 2 * LANES), lambda i: (i, 0))
    return pl.pallas_call(
        body, name=name, grid=(nq,),
        in_specs=[pl.BlockSpec((bq, 4 * LANES), lambda i: (i, 0)), _resident((lp, 4 * LANES), 0), _resident((lp, 2 * LANES), 0)],
        out_specs=[blk, blk],
        out_shape=[jax.ShapeDtypeStruct((lp, 2 * LANES), F32), jax.ShapeDtypeStruct((lp, 2 * LANES), F32)],
        compiler_params=_cp("arbitrary"),
    )(qc, kc, v)


def mla_attn_bwd(qc, kc, v, o, lse, d_o, name):
    lp = qc.shape[0]
    bq = min(ROW_TILE, lp)
    nq = lp // bq

    def body(q_ref, k_ref, v_ref, o_ref, lse_ref, do_ref, dq_ref, dk_ref, dv_ref):
        qi = pl.program_id(0)

        @pl.when(qi == 0)
        def _():
            dk_ref[...] = jnp.zeros_like(dk_ref)
            dv_ref[...] = jnp.zeros_like(dv_ref)

        d_o = do_ref[...]
        masks = _mla_masks(bq)
        qs = _mla_split_q(q_ref[...], masks)
        lo = _lane((bq, LANES)) < 64
        dos = _split_heads(d_o.astype(BF16), lo)
        od = o_ref[...] * d_o
        lse_blk = lse_ref[...]
        delta, lses = [], []
        for h in range(4):
            odp = od[:, LANES * (h // 2):LANES * (h // 2 + 1)]
            delta.append(jnp.sum(jnp.where(lo, odp, 0.0) if h % 2 == 0 else jnp.where(lo, 0.0, odp), axis=1, keepdims=True))
            c0 = LANES * (h // 2) + 64 * (h % 2)
            lses.append(lse_blk[:, c0:c0 + 1])
        ri = lax.broadcasted_iota(jnp.int32, (bq, bq), 0)
        ci = lax.broadcasted_iota(jnp.int32, (bq, bq), 1)

        def step(kb, carry, masked):
            off = pl.multiple_of(kb * bq, bq)
            ks = _mla_per_head_k(k_ref[pl.ds(off, bq), :])
            vs = _per_head(v_ref[pl.ds(off, bq), :])
            ss = [_dot(qs[h], ks[h], "nt") for h in range(4)]
            dps = [_dot(dos[h], vs[h], "nt") for h in range(4)]
            prbs, dss = [], []
            for h in range(4):
                s = ss[h]
                if masked:
                    s = jnp.where(ci <= ri, s, NEG)
                pr = jnp.exp(s - lses[h])
                prbs.append(pr.astype(BF16))
                dss.append((pr * (dps[h] - delta[h])).astype(BF16))
            dv_acc = [_dot(prbs[2 * p], dos[2 * p], "tn") + _dot(prbs[2 * p + 1], dos[2 * p + 1], "tn") for p in range(2)]
            dqs = [_dot(dss[h], ks[h]) for h in range(4)]
            dk_acc = [_dot(dss[2 * p], qs[2 * p], "tn") + _dot(dss[2 * p + 1], qs[2 * p + 1], "tn") for p in range(2)]
            dk_ref[pl.ds(off, bq), :] += jnp.concatenate(dk_acc, axis=1)
            dv_ref[pl.ds(off, bq), :] += jnp.concatenate(dv_acc, axis=1)
            return tuple(carry[h] + dqs[h] for h in range(4))

        zero = jnp.zeros((bq, 2 * LANES), F32)
        carry = step(qi, (zero,) * 4, True)
        def several(t, c):
            for r in range(KEY_UNROLL):
                c = step(qi - 1 - r - KEY_UNROLL * t, c, False)
            return c

        carry = lax.fori_loop(0, qi // KEY_UNROLL, several, carry)
        rem = qi % KEY_UNROLL
        carry = lax.fori_loop(0, rem, lambda t, c: step(rem - 1 - t, c, False), carry)
        dq_ref[...] = jnp.concatenate([jnp.where(masks[0], carry[0], 0.0) + jnp.where(masks[1], carry[1], 0.0),
                                       jnp.where(masks[2], carry[2], 0.0) + jnp.where(masks[3], carry[3], 0.0)], axis=1)

    blk = pl.BlockSpec((bq, 2 * LANES), lambda i: (i, 0))
    wide = pl.BlockSpec((bq, 4 * LANES), lambda i: (i, 0))
    return pl.pallas_call(
        body, name=name, grid=(nq,),
        in_specs=[wide, _resident((lp, 4 * LANES), 0), _resident((lp, 2 * LANES), 0), blk, blk, blk],
        out_specs=[wide, _resident((lp, 4 * LANES), 0), _resident((lp, 2 * LANES), 0)],
        out_shape=[jax.ShapeDtypeStruct((lp, 4 * LANES), F32), jax.ShapeDtypeStruct((lp, 4 * LANES), F32),
                   jax.ShapeDtypeStruct((lp, 2 * LANES), F32)],
        compiler_params=_cp("arbitrary"),
    )(qc, kc, v, o, lse, d_o)


def _mix_out(y_pre, z, o_sb, o_mla, g_ssd, g_sb, g_mla):
    return jnp.concatenate([_rms(y_pre * _silu(z), g_ssd), _rms(o_sb, g_sb), _rms(o_mla, g_mla)], axis=1)


def _ffn_act(up_a, up_b, halo_a, halo_b, w_a, w_b, b_a, b_b, i):
    ca = conv_fwd(up_a, halo_a, w_a, i) + b_a
    cb_ = conv_fwd(up_b, halo_b, w_b, i) + b_b
    return ca, cb_


def layer_fwd(h, w, cs, sn, l):
    lp = h.shape[0]
    nm = f"l{l}_"
    hn = rowwise(lambda i, n, x, g: _rms(x, g), [RI(h), PA(w["norm_mix_g"])], [RO(D_MODEL, BF16)], nm + "rms_mix", lp, tm=WIDE_TILE)[0]
    u = mm(hn, w["w_main"], "nn", F32, nm + "in_main")
    qkv = mm(hn, w["w_sb"], "nn", BF16, nm + "in_sb")
    xbc_c = rowwise(lambda i, n, x, hl, cw, cb_: _silu(conv_fwd(x, hl, cw, i) + cb_),
                    [RI(u, SSD_XBC, 0), HP(u, SSD_XBC, 0), PA(w["ssd_conv_w"]), PA(w["ssd_conv_b"])],
                    [RO(SSD_XBC, F32)], nm + "ssd_conv", lp, tm=WIDE_TILE)[0]
    y_pre, hprev = ssd_fwd(xbc_c, u, w["dt_bias"], w["a_log"], w["d_skip"], nm + "ssd_fwd")
    o_sb, rs_sb = sb_attn_fwd(qkv, nm + "sb_fwd")
    qn, kvn = rowwise(lambda i, n, qa, ckv, gq, gkv: (_rms(qa, gq, MLA_Q_RANK), _rms(ckv, gkv)),
                      [RI(u, 256, U_QA // 256), RI(u, LANES, U_CKV // LANES), PA(w["q_norm_g"]), PA(w["kv_norm_g"])],
                      [RO(256, BF16), RO(LANES, BF16)], nm + "mla_rms", lp, tm=WIDE_TILE)
    qf = mm(qn, w["w_uq"], "nn", F32, nm + "mla_uq")
    kvf = mm(kvn, w["w_ukv"], "nn", F32, nm + "mla_ukv")

    def pack(i, n, qf_, kvf_, kr4, cos, sin):
        qf_ = qf_ * MLA_SCALE
        qr = qf_[:, 256:384]
        qr = qr * cos + rope_rot(qr) * sin
        kr = kr4 * cos + rope_rot(kr4) * sin
        qc = jnp.concatenate([qf_[:, 0:128], qr, qf_[:, 128:256], qr], axis=1)
        kc = jnp.concatenate([kvf_[:, 0:128], kr, kvf_[:, 128:256], kr], axis=1)
        return qc, kc, kvf_[:, 256:512]

    qc, kc, vv = rowwise(pack, [RI(qf), RI(kvf), RI(u, LANES, U_KR4 // LANES), RI(cs), RI(sn)],
                         [RO(512, BF16), RO(512, BF16), RO(256, BF16)], nm + "mla_pack", lp, tm=WIDE_TILE)
    o_mla, lse = mla_attn_fwd(qc, kc, vv, nm + "mla_fwd")
    cat = rowwise(lambda i, n, *a: _mix_out(*a),
                  [RI(y_pre), RI(u, SSD_WIDTH, U_Z // SSD_WIDTH), RI(o_sb), RI(o_mla),
                   PA(w["ssd_norm_g"]), PA(w["sb_norm_g"]), PA(w["mla_norm_g"])],
                  [RO(D_MODEL, BF16)], nm + "mix_out", lp, tm=WIDE_TILE)[0]
    h_mid = mm(cat, w["w_out"], "nn", F32, nm + "out_proj", add=h)
    hn2 = rowwise(lambda i, n, x, g: _rms(x, g), [RI(h_mid), PA(w["norm_ffn_g"])], [RO(D_MODEL, BF16)], nm + "rms_ffn", lp, tm=WIDE_TILE)[0]
    up_a = mm(hn2, w["w_up_a"], "nn", F32, nm + "up_a")
    up_b = mm(hn2, w["w_up_b"], "nn", F32, nm + "up_b")
    wc = 1408

    def act(i, n, ua, ub, ha, hb_, wa, wb, ba, bb_):
        ca, cb_ = _ffn_act(ua, ub, ha, hb_, wa, wb, ba, bb_, i)
        return _silu(ca) * cb_

    a_t = rowwise(act, [RI(up_a, wc, 0, True), RI(up_b, wc, 0, True), HP(up_a, wc, 0, True), HP(up_b, wc, 0, True),
                        PA(w["ffn_conv_w_a"], wc, 0, True), PA(w["ffn_conv_w_b"], wc, 0, True),
                        PA(w["ffn_conv_b_a"], wc, 0, True), PA(w["ffn_conv_b_b"], wc, 0, True)],
                  [RO(D_FF, BF16, wc, True)], nm + "ffn_act", lp, tm=WIDE_TILE, ncol=D_FF // wc)[0]
    h_out = mm(a_t, w["w_down"], "nn", F32, nm + "down", add=h_mid)
    saved = dict(h=h, hn=hn, u=u, qkv=qkv, xbc_c=xbc_c, y_pre=y_pre, hprev=hprev, o_sb=o_sb, rs_sb=rs_sb, qn=qn, kvn=kvn,
                 qc=qc, kc=kc, vv=vv, o_mla=o_mla, lse=lse, cat=cat, h_mid=h_mid, hn2=hn2, up_a=up_a, up_b=up_b, a_t=a_t)
    return h_out, saved


def layer_bwd(dh_out, w, s, cs, sn, l):
    lp = dh_out.shape[0]
    nm = f"l{l}b_"
    g = {}
    wc = 1408
    ncolf = D_FF // wc
    g["w_down"] = mm(s["a_t"], dh_out, "tn", BF16, nm + "dw_down")
    d_act = mm(dh_out, w["w_down"], "nt", F32, nm + "d_act")

    def act_bwd(i, n, ua, ub, ha, hb_, wa, wb, ba, bb_, da_):
        ca, cb_ = _ffn_act(ua, ub, ha, hb_, wa, wb, ba, bb_, i)
        sg = jax.nn.sigmoid(ca)
        dca = da_ * cb_ * (sg * (1.0 + ca * (1.0 - sg)))
        dcb = da_ * (ca * sg)
        return (dca, dcb, conv_bwd_w(dca, ua, ha, i, FFN_CONV), conv_bwd_w(dcb, ub, hb_, i, FFN_CONV),
                jnp.sum(dca, axis=0, keepdims=True), jnp.sum(dcb, axis=0, keepdims=True))

    dca, dcb, g["ffn_conv_w_a"], g["ffn_conv_w_b"], g["ffn_conv_b_a"], g["ffn_conv_b_b"] = rowwise(
        act_bwd, [RI(s["up_a"], wc, 0, True), RI(s["up_b"], wc, 0, True), HP(s["up_a"], wc, 0, True),
                  HP(s["up_b"], wc, 0, True), PA(w["ffn_conv_w_a"], wc, 0, True), PA(w["ffn_conv_w_b"], wc, 0, True),
                  PA(w["ffn_conv_b_a"], wc, 0, True), PA(w["ffn_conv_b_b"], wc, 0, True), RI(d_act, wc, 0, True)],
        [RO(D_FF, F32, wc, True), RO(D_FF, F32, wc, True), AO(FFN_CONV, D_FF, wc, True), AO(FFN_CONV, D_FF, wc, True),
         AO(1, D_FF, wc, True), AO(1, D_FF, wc, True)], nm + "ffn_act_bwd", lp, tm=WIDE_TILE // 2, ncol=ncolf)

    def conv_t(i, n, da_, db_, ha, hb_, wa, wb):
        return conv_bwd_data(da_, ha, wa, i, n), conv_bwd_data(db_, hb_, wb, i, n)

    dup_a, dup_b = rowwise(conv_t, [RI(dca, wc, 0, True), RI(dcb, wc, 0, True), HN(dca, wc, 0, True), HN(dcb, wc, 0, True),
                                    PA(w["ffn_conv_w_a"], wc, 0, True), PA(w["ffn_conv_w_b"], wc, 0, True)],
                           [RO(D_FF, BF16, wc, True), RO(D_FF, BF16, wc, True)], nm + "ffn_conv_t", lp, tm=WIDE_TILE, ncol=ncolf)
    g["w_up_a"] = mm(s["hn2"], dup_a, "tn", BF16, nm + "dw_up_a")
    g["w_up_b"] = mm(s["hn2"], dup_b, "tn", BF16, nm + "dw_up_b")
    dhn2 = mm(dup_a, w["w_up_a"], "nt", F32, nm + "dhn2_a")
    dhn2 = mm(dup_b, w["w_up_b"], "nt", F32, nm + "dhn2_b", add=dhn2)

    def rms_bwd(i, n, x, gg, dy, dres):
        _, vjp = jax.vjp(_rms, x, gg)
        dx, dg = vjp(dy)
        return dres + dx, dg

    dh_mid, g["norm_ffn_g"] = rowwise(rms_bwd, [RI(s["h_mid"]), PA(w["norm_ffn_g"]), RI(dhn2), RI(dh_out)],
                                      [RO(D_MODEL, F32), AO(1, D_MODEL)], nm + "rms_ffn_bwd", lp, tm=WIDE_TILE)
    g["w_out"] = mm(s["cat"], dh_mid, "tn", BF16, nm + "dw_out")
    d_cat = mm(dh_mid, w["w_out"], "nt", F32, nm + "d_cat")
    u = s["u"]

    def mix_bwd(i, n, y_pre, z, o_sb, o_mla, g1, g2, g3, dcat):
        _, vjp = jax.vjp(_mix_out, y_pre, z, o_sb, o_mla, g1, g2, g3)
        return vjp(dcat)

    dy_pre, dz, do_sb, do_mla, g["ssd_norm_g"], g["sb_norm_g"], g["mla_norm_g"] = rowwise(
        mix_bwd, [RI(s["y_pre"]), RI(u, SSD_WIDTH, U_Z // SSD_WIDTH), RI(s["o_sb"]), RI(s["o_mla"]),
                  PA(w["ssd_norm_g"]), PA(w["sb_norm_g"]), PA(w["mla_norm_g"]), RI(d_cat)],
        [RO(SSD_WIDTH, F32), RO(SSD_WIDTH, BF16), RO(SB_WIDTH, F32), RO(256, F32),
         AO(1, SSD_WIDTH), AO(1, SB_WIDTH), AO(1, 256)], nm + "mix_out_bwd", lp, tm=WIDE_TILE)
    dxs, dbp, dcp, ddtp, pg = ssd_bwd(s["xbc_c"], u, w["dt_bias"], w["a_log"], w["d_skip"], s["hprev"], dy_pre, nm + "ssd_bwd")
    pg = pg.reshape(2, SUBLANES, LANES).sum(axis=0)
    g["dt_bias"], g["a_log"], g["d_skip"] = pg[0:1], pg[1:2], pg[2:3]

    def conv4_bwd(i, n, x, hl, cw, cb_, dxs_, dbp_, dcp_, ddtp_):
        pre = conv_fwd(x, hl, cw, i) + cb_
        d_out = jnp.concatenate([dxs_, dbp_, dcp_], axis=1)
        sg = jax.nn.sigmoid(pre)
        d_pre = d_out * (sg * (1.0 + pre * (1.0 - sg)))
        ddt = ddtp_[:, 0:128] + ddtp_[:, 128:256]
        return d_pre, ddt, conv_bwd_w(d_pre, x, hl, i, SSD_CONV), jnp.sum(d_pre, axis=0, keepdims=True)

    d_pre, ddt, g["ssd_conv_w"], g["ssd_conv_b"] = rowwise(
        conv4_bwd, [RI(u, SSD_XBC, 0), HP(u, SSD_XBC, 0), PA(w["ssd_conv_w"]), PA(w["ssd_conv_b"]),
                    RI(dxs), RI(dbp), RI(dcp), RI(ddtp)],
        [RO(SSD_XBC, F32), RO(LANES, BF16), AO(SSD_CONV, SSD_XBC), AO(1, SSD_XBC)], nm + "ssd_conv_bwd", lp, tm=WIDE_TILE)
    d_xbc = rowwise(lambda i, n, d, hn_, cw: conv_bwd_data(d, hn_, cw, i, n),
                    [RI(d_pre), HN(d_pre), PA(w["ssd_conv_w"])], [RO(SSD_XBC, BF16)], nm + "ssd_conv_t", lp, tm=WIDE_TILE)[0]
    dq_sb, dk_sb, dv_sb = sb_attn_bwd(s["qkv"], s["rs_sb"], do_sb, nm + "sb_bwd")
    dqkv = jnp.concatenate([dq_sb, dk_sb.astype(BF16), dv_sb.astype(BF16)], axis=1)
    dqc, dkc, dvv = mla_attn_bwd(s["qc"], s["kc"], s["vv"], s["o_mla"], s["lse"], do_mla, nm + "mla_bwd")

    def unpack(i, n, dqc_, dkc_, dvv_, cos, sin):
        dqr = dqc_[:, 128:256] + dqc_[:, 384:512]
        dqr = dqr * cos + rope_rot_t(dqr * sin)
        dkr = dkc_[:, 128:256] + dkc_[:, 384:512]
        dkr = dkr * cos + rope_rot_t(dkr * sin)
        dq = jnp.concatenate([dqc_[:, 0:128], dqc_[:, 256:384], dqr], axis=1) * MLA_SCALE
        dkv = jnp.concatenate([dkc_[:, 0:128], dkc_[:, 256:384], dvv_], axis=1)
        return dq, dkv, dkr

    dq, dkv, dkr4 = rowwise(unpack, [RI(dqc), RI(dkc), RI(dvv), RI(cs), RI(sn)],
                            [RO(384, BF16), RO(512, BF16), RO(LANES, BF16)], nm + "mla_unpack", lp, tm=WIDE_TILE)
    g["w_uq"] = mm(s["qn"], dq, "tn", F32, nm + "dw_uq")
    g["w_ukv"] = mm(s["kvn"], dkv, "tn", F32, nm + "dw_ukv")
    dqn = mm(dq, w["w_uq"], "nt", F32, nm + "dqn")
    dkvn = mm(dkv, w["w_ukv"], "nt", F32, nm + "dkvn")

    def mla_rms_bwd(i, n, qa, ckv, gq, gkv, dqn_, dkvn_):
        _, vjp = jax.vjp(lambda a, b, c, d: (_rms(a, c, MLA_Q_RANK), _rms(b, d)), qa, ckv, gq, gkv)
        return vjp((dqn_, dkvn_))

    dqa, dckv, g["q_norm_g"], g["kv_norm_g"] = rowwise(
        mla_rms_bwd, [RI(u, 256, U_QA // 256), RI(u, LANES, U_CKV // LANES), PA(w["q_norm_g"]), PA(w["kv_norm_g"]),
                      RI(dqn), RI(dkvn)],
        [RO(256, BF16), RO(LANES, BF16), AO(1, 256), AO(1, LANES)], nm + "mla_rms_bwd", lp, tm=WIDE_TILE)
    du = jnp.concatenate([d_xbc, dz, dqa, dckv, dkr4, ddt, jnp.zeros((lp, LANES), BF16)], axis=1)
    g["w_main"] = mm(s["hn"], du, "tn", F32, nm + "dw_main")
    g["w_sb"] = mm(s["hn"], dqkv, "tn", F32, nm + "dw_sb")
    dhn = mm(du, w["w_main"], "nt", F32, nm + "dhn_main")
    dhn = mm(dqkv, w["w_sb"], "nt", F32, nm + "dhn_sb", add=dhn)
    dh_in, g["norm_mix_g"] = rowwise(rms_bwd, [RI(s["h"]), PA(w["norm_mix_g"]), RI(dhn), RI(dh_mid)],
                                     [RO(D_MODEL, F32), AO(1, D_MODEL)], nm + "rms_mix_bwd", lp, tm=WIDE_TILE)
    return dh_in, g


_IN_CUTS = np.cumsum((512, 1024, 8, 256, 256, 256, 192, 128, 32))


def _pad_cols(a, n):
    return jnp.pad(a, ((0, 0), (0, n - a.shape[1])))


def prep_layer_weights(full, l):
    w_in = full["w_in"][l]
    c = _IN_CUTS
    z, xbc, dtr = w_in[:, :c[0]], w_in[:, c[0]:c[1]], w_in[:, c[1]:c[2]]
    q_sb, k_sb, v_sb = w_in[:, c[2]:c[3]], w_in[:, c[3]:c[4]], w_in[:, c[4]:c[5]]
    q_a, c_kv, k_r = w_in[:, c[5]:c[6]], w_in[:, c[6]:c[7]], w_in[:, c[7]:c[8]]
    w_main = jnp.concatenate([xbc, z, _pad_cols(q_a, 256), c_kv, k_r, k_r, k_r, k_r, _pad_cols(dtr, 256)], axis=1)
    assert w_main.shape[1] == U_MAIN
    row = lambda v, n=None: _pad_cols(v.reshape(1, -1).astype(F32), v.size if n is None else n)
    uq = full["mla_w_uq"][l].reshape(MLA_Q_RANK, 4, 96)
    w_uq = jnp.concatenate([uq[:, :, :64].reshape(MLA_Q_RANK, 256), uq[:, :, 64:].reshape(MLA_Q_RANK, 128)], axis=1)
    w_uq = jnp.pad(w_uq, ((0, 256 - MLA_Q_RANK), (0, 0)))
    ukv = full["mla_w_ukv"][l].reshape(MLA_KV_RANK, 4, 128)
    w_ukv = jnp.concatenate([ukv[:, :, :64].reshape(MLA_KV_RANK, 256), ukv[:, :, 64:].reshape(MLA_KV_RANK, 256)], axis=1)
    return dict(
        norm_mix_g=row(full["norm_mix_g"][l]), w_main=w_main, w_sb=jnp.concatenate([q_sb * SB_SCALE, k_sb, v_sb], axis=1),
        ssd_conv_w=full["ssd_conv_w"][l], ssd_conv_b=row(full["ssd_conv_b"][l]),
        dt_bias=row(full["ssd_dt_bias"][l], LANES), a_log=row(full["ssd_a_log"][l], LANES), d_skip=row(full["ssd_d"][l], LANES),
        ssd_norm_g=row(full["ssd_norm_g"][l]), sb_norm_g=row(full["sb_norm_g"][l]),
        q_norm_g=row(full["mla_q_norm_g"][l], 256), kv_norm_g=row(full["mla_kv_norm_g"][l]),
        w_uq=w_uq, w_ukv=w_ukv, mla_norm_g=row(full["mla_norm_g"][l]),
        w_out=full["w_out"][l], norm_ffn_g=row(full["norm_ffn_g"][l]),
        w_up_a=full["ffn_w_up"][l][:, :D_FF], w_up_b=full["ffn_w_up"][l][:, D_FF:],
        ffn_conv_w_a=full["ffn_conv_w"][l][:, :D_FF], ffn_conv_w_b=full["ffn_conv_w"][l][:, D_FF:],
        ffn_conv_b_a=row(full["ffn_conv_b"][l][:D_FF]), ffn_conv_b_b=row(full["ffn_conv_b"][l][D_FF:]),
        w_down=full["ffn_w_down"][l],
    )


def unprep_layer_grads(g):
    wm = g["w_main"]
    xbc, z = wm[:, U_XBC:U_XBC + 1024], wm[:, U_Z:U_Z + 512]
    q_a, c_kv = wm[:, U_QA:U_QA + MLA_Q_RANK], wm[:, U_CKV:U_CKV + 128]
    k_r = (wm[:, U_KR4:U_KR4 + 32] + wm[:, U_KR4 + 32:U_KR4 + 64] + wm[:, U_KR4 + 64:U_KR4 + 96] + wm[:, U_KR4 + 96:U_KR4 + 128])
    dtr = wm[:, U_DT:U_DT + SSD_HEADS]
    w_sb = g["w_sb"]
    w_in = jnp.concatenate([z, xbc, dtr, w_sb[:, :SB_WIDTH] * SB_SCALE, w_sb[:, SB_WIDTH:], q_a, c_kv, k_r], axis=1)
    guq = g["w_uq"][:MLA_Q_RANK]
    guq = jnp.concatenate([guq[:, :256].reshape(MLA_Q_RANK, 4, 64), guq[:, 256:].reshape(MLA_Q_RANK, 4, 32)], axis=2)
    gukv = g["w_ukv"]
    gukv = jnp.concatenate([gukv[:, :256].reshape(MLA_KV_RANK, 4, 64), gukv[:, 256:].reshape(MLA_KV_RANK, 4, 64)], axis=2)
    return dict(
        norm_mix_g=g["norm_mix_g"][0], w_in=w_in, ssd_conv_w=g["ssd_conv_w"], ssd_conv_b=g["ssd_conv_b"][0],
        ssd_dt_bias=g["dt_bias"][0, :SSD_HEADS], ssd_a_log=g["a_log"][0, :SSD_HEADS], ssd_d=g["d_skip"][0, :SSD_HEADS],
        ssd_norm_g=g["ssd_norm_g"][0], sb_norm_g=g["sb_norm_g"][0], mla_q_norm_g=g["q_norm_g"][0, :MLA_Q_RANK],
        mla_kv_norm_g=g["kv_norm_g"][0], mla_w_uq=guq.reshape(MLA_Q_RANK, 384), mla_w_ukv=gukv.reshape(MLA_KV_RANK, 512),
        mla_norm_g=g["mla_norm_g"][0], w_out=g["w_out"], norm_ffn_g=g["norm_ffn_g"][0],
        ffn_w_up=jnp.concatenate([g["w_up_a"], g["w_up_b"]], axis=1),
        ffn_conv_w=jnp.concatenate([g["ffn_conv_w_a"], g["ffn_conv_w_b"]], axis=1),
        ffn_conv_b=jnp.concatenate([g["ffn_conv_b_a"][0], g["ffn_conv_b_b"][0]], axis=0),
        ffn_w_down=g["w_down"],
    )


def rope_tables(lp):
    pos = jnp.arange(lp, dtype=F32)
    inv = 1.0 / (ROPE_BASE ** (jnp.arange(0, MLA_ROPE, 2, dtype=F32) / MLA_ROPE))
    ang = pos[:, None] * inv[None, :]
    ang = jnp.concatenate([ang, ang] * 4, axis=-1)
    return jnp.cos(ang), jnp.sin(ang)


def local_step(x_seq, target, full):
    seq = x_seq.shape[0]
    length = seq + N_META
    lp = -(-length // ROW_TILE) * ROW_TILE
    cs, sn = rope_tables(lp)
    h = jnp.concatenate([full["meta_tokens"].astype(F32), x_seq, jnp.zeros((lp - length, D_MODEL), F32)], axis=0)
    tgt = jnp.pad(target, ((N_META, lp - length), (0, 0)))
    ws = [prep_layer_weights(full, l) for l in range(DEPTH)]
    saved = []
    for l in range(DEPTH):
        h, s = layer_fwd(h, ws[l], cs, sn, l)
        saved.append(s)
    fg = full["final_norm_g"].reshape(1, D_MODEL).astype(F32)
    tm = min(WIDE_TILE, lp)

    def loss_fn(i, n, x, g, t):
        rows = _rows_iota(x) + i * tm
        valid = jnp.logical_and(rows >= N_META, rows < length)

        def f(x_, g_):
            err = jnp.where(valid, _rms(x_, g_) - t, 0.0)
            return 0.5 * jnp.sum(err * err) * (1.0 / D_MODEL)

        val, (dx, dg) = jax.value_and_grad(f, argnums=(0, 1))(x, g)
        return dx, jnp.full((1, LANES), val, F32), dg

    dh, loss_row, g_final = rowwise(loss_fn, [RI(h), PA(fg), RI(tgt)], [RO(D_MODEL, F32), AO(1, LANES), AO(1, D_MODEL)],
                                    "loss_head", lp, tm=WIDE_TILE)
    grads = {}
    per_layer = [None] * DEPTH
    for l in reversed(range(DEPTH)):
        dh, g = layer_bwd(dh, ws[l], saved[l], cs, sn, l)
        per_layer[l] = unprep_layer_grads(g)
    for k in per_layer[0]:
        grads[k] = jnp.stack([per_layer[l][k] for l in range(DEPTH)], axis=0)
    grads["final_norm_g"] = g_final[0]
    grads["meta_tokens"] = dh[:N_META]
    return loss_row[0, 0], dh[N_META:length], grads


_ANY = pl.BlockSpec(memory_space=pl.ANY)


def chip_exchange(srcs, modes, name):
    n = len(srcs)
    flips = ((1, 0), (0, 1), (1, 1))

    def body(*refs):
        ins, outs = refs[:n], refs[n:2 * n]
        send_sems, recv_sems, fwd_send_sems, fwd_recv_sems, loc_sems = refs[2 * n:]
        x, y, c = lax.axis_index("x"), lax.axis_index("y"), lax.axis_index("c")
        me = 2 * x + y
        waits, forwards = [], []
        for a in range(n):
            whole = modes[a] != "slab"
            cp = pltpu.make_async_copy(ins[a] if whole else ins[a].at[me], outs[a].at[me], loc_sems.at[a])
            cp.start()
            waits.append(cp.wait)
            half = ins[a].shape[0] // 2 if modes[a] == "bcast_split" else None
            for k, (fx, fy) in enumerate(flips):
                px = 1 - x if fx else x
                py = 1 - y if fy else y
                peer = 2 * px + py
                if half is None:
                    src = ins[a] if whole else ins[a].at[peer]
                    dst = outs[a].at[me]
                else:
                    src = ins[a].at[pl.ds(c * half, half)]
                    dst = outs[a].at[me, pl.ds(c * half, half)]
                rc = pltpu.make_async_remote_copy(src_ref=src, dst_ref=dst, send_sem=send_sems.at[a, k],
                                                  recv_sem=recv_sems.at[a, k], device_id=(px, py, c), device_id_type=MESH_ID)
                rc.start()
                if half is None:
                    waits.append(rc.wait)
                else:
                    waits.append(rc.wait_send)
                    landed = outs[a].at[peer, pl.ds(c * half, half)]
                    fw = pltpu.make_async_remote_copy(src_ref=landed, dst_ref=landed, send_sem=fwd_send_sems.at[a, k],
                                                      recv_sem=fwd_recv_sems.at[a, k], device_id=(x, y, 1 - c),
                                                      device_id_type=MESH_ID)
                    forwards.append((rc, fw))
        for rc, fw in forwards:
            rc.wait_recv()
            fw.start()
        for rc, fw in forwards:
            fw.wait()
        for w in waits:
            w()

    out_shape = [jax.ShapeDtypeStruct((N_CHIPS,) + (s.shape if m != "slab" else s.shape[1:]), s.dtype) for s, m in zip(srcs, modes)]
    return pl.pallas_call(
        body, name=name, in_specs=[_ANY] * n, out_specs=[_ANY] * n, out_shape=out_shape,
        scratch_shapes=[pltpu.SemaphoreType.DMA((n, 3)), pltpu.SemaphoreType.DMA((n, 3)), pltpu.SemaphoreType.DMA((n, 3)),
                        pltpu.SemaphoreType.DMA((n, 3)), pltpu.SemaphoreType.DMA((n,))],
    )(*srcs)


def _piece(ref, mode, k):
    if mode == "slab":
        return ref.at[k]
    if mode == "rows":
        rs = ref.shape[1] // N_CHIPS
        return ref.at[:, pl.ds(pl.multiple_of(k * rs, 16), rs), :]
    if mode == "cols":
        cs = ref.shape[2] // N_CHIPS
        return ref.at[:, :, pl.ds(pl.multiple_of(k * cs, LANES), cs)]
    return ref


def _piece_shape(shape, mode):
    if mode == "slab":
        return shape[1:]
    if mode == "rows":
        return (shape[0], shape[1] // N_CHIPS, shape[2])
    if mode == "cols":
        return (shape[0], shape[1], shape[2] // N_CHIPS)
    return shape


def grad_exchange(srcs, modes, name):
    n = len(srcs)
    flips = ((1, 0), (0, 1), (1, 1))

    def body(*refs):
        ins, outs = refs[:n], refs[n:2 * n]
        send_sems, recv_sems, fwd_send_sems, fwd_recv_sems, sib_send_sems, sib_recv_sems, loc_sems = refs[2 * n:]
        x, y, c = lax.axis_index("x"), lax.axis_index("y"), lax.axis_index("c")
        me = 2 * x + y
        sibling = (x, y, 1 - c)
        waits, forwards = [], []
        for a in range(n):
            mine = _piece(ins[a], modes[a], me)
            slot = outs[a].at[4 * c + me]
            cp = pltpu.make_async_copy(mine, slot, loc_sems.at[a])
            cp.start()
            sb = pltpu.make_async_remote_copy(src_ref=mine, dst_ref=slot, send_sem=sib_send_sems.at[a],
                                              recv_sem=sib_recv_sems.at[a], device_id=sibling, device_id_type=MESH_ID)
            sb.start()
            waits += [cp.wait, sb.wait]
            for k, (fx, fy) in enumerate(flips):
                px = 1 - x if fx else x
                py = 1 - y if fy else y
                peer = 2 * px + py
                rc = pltpu.make_async_remote_copy(src_ref=_piece(ins[a], modes[a], peer), dst_ref=slot,
                                                  send_sem=send_sems.at[a, k], recv_sem=recv_sems.at[a, k],
                                                  device_id=(px, py, c), device_id_type=MESH_ID)
                rc.start()
                landed = outs[a].at[4 * c + peer]
                fw = pltpu.make_async_remote_copy(src_ref=landed, dst_ref=landed, send_sem=fwd_send_sems.at[a, k],
                                                  recv_sem=fwd_recv_sems.at[a, k], device_id=sibling, device_id_type=MESH_ID)
                waits.append(rc.wait_send)
                forwards.append((rc, fw))
        for rc, fw in forwards:
            rc.wait_recv()
            fw.start()
        for rc, fw in forwards:
            fw.wait()
        for w in waits:
            w()

    out_shape = [jax.ShapeDtypeStruct((2 * N_CHIPS,) + tuple(_piece_shape(s.shape, m)), s.dtype) for s, m in zip(srcs, modes)]
    dma = pltpu.SemaphoreType.DMA
    return pl.pallas_call(
        body, name=name, in_specs=[_ANY] * n, out_specs=[_ANY] * n, out_shape=out_shape,
        scratch_shapes=[dma((n, 3)), dma((n, 3)), dma((n, 3)), dma((n, 3)), dma((n,)), dma((n,)), dma((n,))],
    )(*srcs)


WEIGHT_NAMES = ("meta_tokens", "norm_mix_g", "w_in", "ssd_conv_w", "ssd_conv_b", "ssd_dt_bias", "ssd_a_log", "ssd_d",
                "ssd_norm_g", "sb_norm_g", "mla_q_norm_g", "mla_kv_norm_g", "mla_w_uq", "mla_w_ukv", "mla_norm_g",
                "w_out", "norm_ffn_g", "ffn_w_up", "ffn_conv_w", "ffn_conv_b", "ffn_w_down", "final_norm_g")
SHARD_AXIS = {"meta_tokens": 1, "w_in": 2, "ssd_conv_w": 2, "mla_w_uq": 2, "mla_w_ukv": 2, "w_out": 1, "ffn_w_up": 2,
              "ffn_conv_w": 2, "ffn_w_down": 1}
SHARDED = tuple(n for n in WEIGHT_NAMES if n in SHARD_AXIS)
REPLICATED = tuple(n for n in WEIGHT_NAMES if n not in SHARD_AXIS)
GATHER_BF16 = ("w_in", "mla_w_uq", "mla_w_ukv", "w_out", "ffn_w_up", "ffn_w_down")
GATHER_F32 = ("meta_tokens", "ssd_conv_w", "ffn_conv_w")
PACK_ROWS = ROW_TILE


def pack(arrs, dtype):
    flat = jnp.concatenate([a.reshape(-1).astype(dtype) for a in arrs])
    per = PACK_ROWS * PACK_W
    total = -(-flat.size // per) * per
    return jnp.pad(flat, (0, total - flat.size)).reshape(total // PACK_W, PACK_W)


def unpack(buf, shapes):
    flat = buf.reshape(-1)
    out, off = [], 0
    for shp in shapes:
        size = int(np.prod(shp))
        out.append(flat[off:off + size].reshape(shp))
        off += size
    return out


def gather_weights(a):
    full = {n: a[n] for n in REPLICATED}
    bufs = [pack([a[n] for n in GATHER_BF16], BF16), pack([a[n] for n in GATHER_F32], F32)]
    got = chip_exchange(bufs, ("bcast_split", "bcast"), "gather_weights")
    for names, g in ((GATHER_BF16, got[0]), (GATHER_F32, got[1])):
        pieces = [unpack(g[k], [a[n].shape for n in names]) for k in range(N_CHIPS)]
        for idx, n in enumerate(names):
            full[n] = jnp.concatenate([pieces[k][idx] for k in range(N_CHIPS)], axis=SHARD_AXIS[n])
    return full


BIG = ("w_in", "w_out", "ffn_w_up", "ffn_w_down")
BIG_MODE = {"w_in": "slab", "w_out": "rows", "ffn_w_up": "cols", "ffn_w_down": "rows"}
SMALL_SHARDED = tuple(n for n in SHARDED if n not in BIG)
ADAM_TILE = 128


def _adamw(i, n, *vals):
    parts, (w, m, v) = vals[:2 * N_CHIPS], vals[2 * N_CHIPS:]
    g = parts[0].astype(F32)
    for p in parts[1:]:
        g = g + p.astype(F32)
    m = ADAM_B1 * m + (1.0 - ADAM_B1) * g
    v = ADAM_B2 * v + (1.0 - ADAM_B2) * jnp.square(g)
    m_hat = m / (1.0 - ADAM_B1 ** ADAM_STEP)
    v_hat = v / (1.0 - ADAM_B2 ** ADAM_STEP)
    delta = -ADAM_LR * (m_hat / (jnp.sqrt(v_hat) + ADAM_EPS) + ADAM_WD * w)
    return g, delta, m, v


def _adamw_call(got, w, m, v, name):
    rows, width = w.shape
    flat = got.reshape(2 * N_CHIPS * rows, width)
    blk = rows // ADAM_TILE
    ins = [RI(flat, rblk=k * blk) for k in range(2 * N_CHIPS)] + [RI(w), RI(m), RI(v)]
    return rowwise(_adamw, ins, [RO(width, F32)] * 4, name, rows, tm=ADAM_TILE)


def reduce_and_update(a, grads):
    srcs, modes = [], []
    for n in BIG:
        g = grads[n].astype(BF16)
        if n == "w_in":
            cs = a[n].shape[2]
            g = g.reshape(DEPTH, D_MODEL, N_CHIPS, cs).transpose(2, 0, 1, 3)
        srcs.append(g)
        modes.append(BIG_MODE[n])
    slabs = []
    for k in range(N_CHIPS):
        parts = []
        for n in SMALL_SHARDED:
            ax = SHARD_AXIS[n]
            size = a[n].shape[ax]
            parts.append(lax.slice_in_dim(grads[n], k * size, (k + 1) * size, axis=ax))
        slabs.append(pack(parts, BF16))
    srcs += [jnp.stack(slabs, axis=0), pack([grads[n] for n in REPLICATED], F32)]
    modes += ["slab", "bcast"]
    got = grad_exchange(srcs, modes, "exchange_grads")
    outs = {}
    kinds = ("grad", "delta", "new_m", "new_v")
    for n, g8 in zip(BIG, got):
        shp = a[n].shape
        rows = shp[0] * shp[1]
        flat = lambda t: t.reshape(rows, shp[2])
        res = _adamw_call(g8.reshape(2 * N_CHIPS, rows, shp[2]), flat(a[n]), flat(a["m_" + n]), flat(a["v_" + n]), "adamw_" + n)
        for kind, val in zip(kinds, res):
            outs[(kind, n)] = val.reshape(shp)
    for tag, names, g8 in (("small", SMALL_SHARDED, got[len(BIG)]), ("rep", REPLICATED, got[len(BIG) + 1])):
        shapes = [a[n].shape for n in names]
        packed = [pack([a[pre + n] for n in names], F32) for pre in ("", "m_", "v_")]
        res = _adamw_call(g8, *packed, "adamw_" + tag)
        for kind, buf in zip(kinds, res):
            for n, val in zip(names, unpack(buf, shapes)):
                outs[(kind, n)] = val
    return outs


INPUT_NAMES = ("x",) + WEIGHT_NAMES + ("loss_target",) + tuple("m_" + n for n in WEIGHT_NAMES) + tuple("v_" + n for n in WEIGHT_NAMES)


def kernel(x, meta_tokens, norm_mix_g, w_in, ssd_conv_w, ssd_conv_b, ssd_dt_bias, ssd_a_log, ssd_d, ssd_norm_g, sb_norm_g, mla_q_norm_g, mla_kv_norm_g, mla_w_uq, mla_w_ukv, mla_norm_g, w_out, norm_ffn_g, ffn_w_up, ffn_conv_w, ffn_conv_b, ffn_w_down, final_norm_g, loss_target, m_meta_tokens, m_norm_mix_g, m_w_in, m_ssd_conv_w, m_ssd_conv_b, m_ssd_dt_bias, m_ssd_a_log, m_ssd_d, m_ssd_norm_g, m_sb_norm_g, m_mla_q_norm_g, m_mla_kv_norm_g, m_mla_w_uq, m_mla_w_ukv, m_mla_norm_g, m_w_out, m_norm_ffn_g, m_ffn_w_up, m_ffn_conv_w, m_ffn_conv_b, m_ffn_w_down, m_final_norm_g, v_meta_tokens, v_norm_mix_g, v_w_in, v_ssd_conv_w, v_ssd_conv_b, v_ssd_dt_bias, v_ssd_a_log, v_ssd_d, v_ssd_norm_g, v_sb_norm_g, v_mla_q_norm_g, v_mla_kv_norm_g, v_mla_w_uq, v_mla_w_ukv, v_mla_norm_g, v_w_out, v_norm_ffn_g, v_ffn_w_up, v_ffn_conv_w, v_ffn_conv_b, v_ffn_w_down, v_final_norm_g):
    args = (x, meta_tokens, norm_mix_g, w_in, ssd_conv_w, ssd_conv_b, ssd_dt_bias, ssd_a_log, ssd_d, ssd_norm_g, sb_norm_g, mla_q_norm_g, mla_kv_norm_g, mla_w_uq, mla_w_ukv, mla_norm_g, w_out, norm_ffn_g, ffn_w_up, ffn_conv_w, ffn_conv_b, ffn_w_down, final_norm_g, loss_target, m_meta_tokens, m_norm_mix_g, m_w_in, m_ssd_conv_w, m_ssd_conv_b, m_ssd_dt_bias, m_ssd_a_log, m_ssd_d, m_ssd_norm_g, m_sb_norm_g, m_mla_q_norm_g, m_mla_kv_norm_g, m_mla_w_uq, m_mla_w_ukv, m_mla_norm_g, m_w_out, m_norm_ffn_g, m_ffn_w_up, m_ffn_conv_w, m_ffn_conv_b, m_ffn_w_down, m_final_norm_g, v_meta_tokens, v_norm_mix_g, v_w_in, v_ssd_conv_w, v_ssd_conv_b, v_ssd_dt_bias, v_ssd_a_log, v_ssd_d, v_ssd_norm_g, v_sb_norm_g, v_mla_q_norm_g, v_mla_kv_norm_g, v_mla_w_uq, v_mla_w_ukv, v_mla_norm_g, v_w_out, v_norm_ffn_g, v_ffn_w_up, v_ffn_conv_w, v_ffn_conv_b, v_ffn_w_down, v_final_norm_g)
    a = dict(zip(INPUT_NAMES, args, strict=True))
    full = gather_weights(a)
    loss, grad_x, grads = local_step(a["x"][0], a["loss_target"][0], full)
    loss = lax.psum(loss, ("x", "y", "c"))
    outs = reduce_and_update(a, grads)
    result = [loss, grad_x[None]]
    for kind in ("grad", "delta", "new_m", "new_v"):
        result += [outs[(kind, n)] for n in WEIGHT_NAMES]
    return tuple(result)
```

```python
import functools
import math

import numpy as np
import jax
import jax.numpy as jnp
from jax import lax
from jax.experimental import pallas as pl
from jax.experimental.pallas import tpu as pltpu

F32 = jnp.float32
BF16 = jnp.bfloat16
MESH_ID = pl.DeviceIdType.MESH

D_MODEL = 1024
DEPTH = 2
N_META = 16
EPS = 1e-6
SSD_HEADS = 8
SSD_WIDTH = 512
SSD_XBC = 1024
SSD_CONV = 4
SB_WIDTH = 256
SB_SCALE = 64 ** -0.5
MLA_Q_RANK = 192
MLA_KV_RANK = 128
MLA_ROPE = 32
MLA_SCALE = 96 ** -0.5
ROPE_BASE = 10000.0
D_FF = 2816
FFN_CONV = 3
IN_COLS = 2664
N_CHIPS = 4

ADAM_LR = 0.001
ADAM_B1 = 0.9
ADAM_B2 = 0.999
ADAM_EPS = 1e-08
ADAM_WD = 0.01
ADAM_STEP = 10

LANES = 128
SUBLANES = 8
ROW_TILE = 256
KEY_UNROLL = 4
WIDE_TILE = 768
VMEM_LIMIT = 56 * 1024 * 1024
PACK_W = 1024

U_XBC, U_Z, U_QA, U_CKV, U_KR4, U_DT, U_MAIN = 0, 1024, 1536, 1792, 1920, 2048, 2304
NEG = -1e30


def _cp(*sem):
    return pltpu.CompilerParams(dimension_semantics=sem if sem else None, vmem_limit_bytes=VMEM_LIMIT)


def _pick(dim, pref):
    if dim <= pref:
        return dim
    best = None
    for t in range(LANES, pref + 1, LANES):
        if dim % t == 0:
            best = t
    assert best is not None, (dim, pref)
    return best


def _dot(a, b, dims="nn", precision=None):
    dn = {"nn": (((1,), (0,)), ((), ())), "nt": (((1,), (1,)), ((), ())), "tn": (((0,), (0,)), ((), ()))}[dims]
    return lax.dot_general(a, b, dn, preferred_element_type=F32, precision=precision)


def _tri_dot(tri, x):
    hi = x.astype(BF16)
    r1 = x - hi.astype(F32)
    mid = r1.astype(BF16)
    lo = (r1 - mid.astype(F32)).astype(BF16)
    t = tri.astype(BF16)
    return _dot(t, hi) + _dot(t, mid) + _dot(t, lo)


def _softplus(x):
    return jnp.maximum(x, 0.0) + jnp.log1p(jnp.exp(-jnp.abs(x)))


def _silu(x):
    return x * jax.nn.sigmoid(x)


def _rms(x, g, n=None):
    n = x.shape[-1] if n is None else n
    ms = jnp.sum(x * x, axis=-1, keepdims=True) * (1.0 / n)
    return x * lax.rsqrt(ms + EPS) * g


def mm(a, b, dims, out_dtype, name, add=None, tm=None, tn=None, tk=None):
    if dims == "nn":
        (m, k), (k2, n) = a.shape, b.shape
    elif dims == "nt":
        (m, k), (n, k2) = a.shape, b.shape
    else:
        (k, m), (k2, n) = a.shape, b.shape
    assert k == k2, (a.shape, b.shape, dims)
    if dims == "tn":
        tm, tn, tk = _pick(m, tm or 1408), _pick(n, tn or 1408), _pick(k, tk or 1408)
    else:
        tm, tn, tk = _pick(m, tm or (1408 if k <= 2304 else 768)), _pick(n, tn or 1408), _pick(k, tk or 2816)
    nk = k // tk
    if dims == "tn":
        a_spec = pl.BlockSpec((tk, tm), lambda j, i, kk: (kk, i))
    else:
        a_spec = pl.BlockSpec((tm, tk), lambda j, i, kk: (i, kk))
    if dims == "nt":
        b_spec = pl.BlockSpec((tn, tk), lambda j, i, kk: (j, kk))
    else:
        b_spec = pl.BlockSpec((tk, tn), lambda j, i, kk: (kk, j))
    o_spec = pl.BlockSpec((tm, tn), lambda j, i, kk: (i, j))
    has_add = add is not None

    def body(*refs):
        a_ref, b_ref = refs[0], refs[1]
        add_ref = refs[2] if has_add else None
        o_ref = refs[3] if has_add else refs[2]
        part = _dot(a_ref[...].astype(BF16), b_ref[...].astype(BF16), dims)

        def finish(r):
            if has_add:
                r = r + add_ref[...].astype(F32)
            o_ref[...] = r.astype(o_ref.dtype)

        if nk == 1:
            finish(part)
            return
        acc_ref = refs[-1]
        kk = pl.program_id(2)

        @pl.when(kk == 0)
        def _():
            acc_ref[...] = part

        @pl.when(jnp.logical_and(kk > 0, kk < nk - 1))
        def _():
            acc_ref[...] += part

        @pl.when(kk == nk - 1)
        def _():
            finish(acc_ref[...] + part)

    in_specs = [a_spec, b_spec] + ([o_spec] if has_add else [])
    args = (a, b) + ((add,) if has_add else ())
    return pl.pallas_call(
        body, name=name, grid=(n // tn, m // tm, nk),
        in_specs=in_specs, out_specs=o_spec,
        out_shape=jax.ShapeDtypeStruct((m, n), out_dtype),
        scratch_shapes=[pltpu.VMEM((tm, tn), F32)] if nk > 1 else [],
        compiler_params=_cp("parallel", "parallel", "arbitrary"),
    )(*args)


def RI(arr, width=None, cidx=0, cv=False, rblk=0):
    return ("row" if rblk == 0 else ("row", rblk), arr, arr.shape[1] if width is None else width, cidx, cv)


def HP(arr, width=None, cidx=0, cv=False):
    return ("prev", arr, arr.shape[1] if width is None else width, cidx, cv)


def HN(arr, width=None, cidx=0, cv=False):
    return ("next", arr, arr.shape[1] if width is None else width, cidx, cv)


def PA(arr, width=None, cidx=0, cv=False):
    return ("par", arr, arr.shape[1] if width is None else width, cidx, cv)


def RO(ncols, dtype, width=None, cv=False):
    return ("row", ncols, dtype, ncols if width is None else width, cv)


def AO(nrows, ncols, width=None, cv=False):
    return ("acc", (nrows, ncols), F32, ncols if width is None else width, cv)


def rowwise(fn, ins, outs, name, rows, tm=ROW_TILE, ncol=1):
    tm = min(tm, rows)
    assert rows % tm == 0
    nrow = rows // tm
    hb = tm // SUBLANES
    last_hb = rows // SUBLANES - 1
    in_specs, args = [], []
    for kind, arr, width, cidx, cv in ins:
        def cmap(j, cidx=cidx, cv=cv):
            return cidx + j if cv else cidx
        if kind == "row":
            spec = pl.BlockSpec((tm, width), lambda j, i, cmap=cmap: (i, cmap(j)))
        elif isinstance(kind, tuple):
            spec = pl.BlockSpec((tm, width), lambda j, i, cmap=cmap, rblk=kind[1]: (i + rblk, cmap(j)))
        elif kind == "prev":
            spec = pl.BlockSpec((SUBLANES, width), lambda j, i, cmap=cmap: (jnp.maximum(i * hb - 1, 0), cmap(j)))
        elif kind == "next":
            spec = pl.BlockSpec((SUBLANES, width), lambda j, i, cmap=cmap: (jnp.minimum((i + 1) * hb, last_hb), cmap(j)))
        else:
            spec = pl.BlockSpec((arr.shape[0], width), lambda j, i, cmap=cmap: (0, cmap(j)))
        in_specs.append(spec)
        args.append(arr)
    out_specs, out_shapes, acc_cv = [], [], []
    for kind, shp, dtype, width, cv in outs:
        if kind == "row":
            out_specs.append(pl.BlockSpec((tm, width), lambda j, i, cv=cv: (i, j if cv else 0)))
            out_shapes.append(jax.ShapeDtypeStruct((rows, shp), dtype))
            acc_cv.append(None)
        else:
            out_specs.append(pl.BlockSpec((shp[0], width), lambda j, i, cv=cv: (0, j if cv else 0)))
            out_shapes.append(jax.ShapeDtypeStruct(shp, dtype))
            acc_cv.append(cv)
    n_in = len(ins)

    def body(*refs):
        j = pl.program_id(0)
        i = pl.program_id(1)
        vals = fn(i, nrow, *[r[...] for r in refs[:n_in]])
        if not isinstance(vals, (tuple, list)):
            vals = (vals,)
        for o_ref, v, cv in zip(refs[n_in:], vals, acc_cv):
            if cv is None:
                o_ref[...] = v.astype(o_ref.dtype)
            else:
                first = (i == 0) if cv else jnp.logical_and(i == 0, j == 0)

                @pl.when(first)
                def _(o_ref=o_ref, v=v):
                    o_ref[...] = v.astype(o_ref.dtype)

                @pl.when(jnp.logical_not(first))
                def _(o_ref=o_ref, v=v):
                    o_ref[...] += v.astype(o_ref.dtype)

    res = pl.pallas_call(
        body, name=name, grid=(ncol, nrow), in_specs=in_specs, out_specs=out_specs, out_shape=out_shapes,
        compiler_params=_cp("arbitrary", "arbitrary"),
    )(*args)
    return res


def _rows_iota(x):
    return lax.broadcasted_iota(jnp.int32, x.shape, 0)


def shift_down(x, halo, s):
    if s == 0:
        return x
    tm = x.shape[0]
    top = pltpu.roll(halo, s, 0)
    if tm > SUBLANES:
        top = jnp.concatenate([top, jnp.zeros((tm - SUBLANES, x.shape[1]), x.dtype)], axis=0)
    return jnp.where(_rows_iota(x) < s, top, pltpu.roll(x, s, 0))


def shift_up(x, halo, s):
    if s == 0:
        return x
    tm = x.shape[0]
    bot = pltpu.roll(halo, SUBLANES - s, 0)
    if tm > SUBLANES:
        bot = jnp.concatenate([jnp.zeros((tm - SUBLANES, x.shape[1]), x.dtype), bot], axis=0)
    return jnp.where(_rows_iota(x) >= tm - s, bot, pltpu.roll(x, tm - s, 0))


def conv_fwd(x, halo, w, i):
    kw = w.shape[0]
    halo = jnp.where(i == 0, 0.0, halo)
    out = None
    for k in range(kw):
        term = w[k:k + 1, :] * shift_down(x, halo, kw - 1 - k)
        out = term if out is None else out + term
    return out


def conv_bwd_data(dy, halo_next, w, i, n):
    kw = w.shape[0]
    halo_next = jnp.where(i == n - 1, 0.0, halo_next)
    out = None
    for k in range(kw):
        term = w[k:k + 1, :] * shift_up(dy, halo_next, kw - 1 - k)
        out = term if out is None else out + term
    return out


def conv_bwd_w(dy, x, halo, i, kw):
    halo = jnp.where(i == 0, 0.0, halo)
    rows = [jnp.sum(dy * shift_down(x, halo, kw - 1 - k), axis=0, keepdims=True) for k in range(kw)]
    return jnp.concatenate(rows, axis=0)


def _lane(shape):
    return lax.broadcasted_iota(jnp.int32, shape, 1)


def rope_rot(x):
    lane = _lane(x.shape) % MLA_ROPE
    return jnp.where(lane < MLA_ROPE // 2, -pltpu.roll(x, LANES - MLA_ROPE // 2, 1), pltpu.roll(x, MLA_ROPE // 2, 1))


def rope_rot_t(g):
    lane = _lane(g.shape) % MLA_ROPE
    return jnp.where(lane < MLA_ROPE // 2, pltpu.roll(g, LANES - MLA_ROPE // 2, 1), -pltpu.roll(g, MLA_ROPE // 2, 1))


def _ssd_common(g, xs, dt_raw, bias, alog, q):
    lane = _lane((q, LANES))
    pre = dt_raw + bias
    dt = jnp.where(lane < SSD_HEADS, _softplus(pre), 0.0)
    a_row = -jnp.exp(alog)
    d_a = dt * a_row
    ri = lax.broadcasted_iota(jnp.int32, (q, q), 0)
    ci = lax.broadcasted_iota(jnp.int32, (q, q), 1)
    causal = ri >= ci
    acs = _tri_dot(causal, d_a)
    acs_t = acs.T
    subl = lax.broadcasted_iota(jnp.int32, (LANES, q), 0)
    heads = [4 * g + i for i in range(4)]
    lo = lane < 64

    def col(arr, h):
        return jnp.sum(jnp.where(lane == h, arr, 0.0), axis=1, keepdims=True)

    def lanes4(v):
        m = lo if v[0].shape[0] == q else lo[0:1, :]
        return jnp.concatenate([jnp.where(m, v[0], v[1]), jnp.where(m, v[2], v[3])], axis=1)

    cols = [col(acs, h) for h in heads]
    rows = [jnp.sum(jnp.where(subl == h, acs_t, 0.0), axis=0, keepdims=True) for h in heads]
    tots = [c_[q - 1:q, :] for c_ in cols]
    acs4 = lanes4(cols)
    dt4 = lanes4([col(dt, h) for h in heads])
    lms = [jnp.exp(jnp.where(causal, cols[i] - rows[i], NEG)) for i in range(4)]
    lane4 = _lane((q, 2 * LANES))
    hm = [jnp.logical_and(lane4 >= 64 * i, lane4 < 64 * (i + 1)) for i in range(4)]
    return dict(lane=lane, lo=lo, pre=pre, dt=dt, a_row=a_row, heads=heads, tots=tots, lms=lms, ri=ri, ci=ci, hm=hm,
                lanes4=lanes4, eacs=jnp.exp(acs4), dte=jnp.exp(lanes4(tots) - acs4), dt4=dt4, x=xs * dt4)


def _pick_lane(row_arr, h):
    return jnp.sum(jnp.where(_lane(row_arr.shape) == h, row_arr, 0.0), axis=1, keepdims=True)


def _etot(tots):
    sub = lax.broadcasted_iota(jnp.int32, (2 * LANES, LANES), 0)
    e = [jnp.exp(t) for t in tots]
    return jnp.where(sub < 64, e[0], jnp.where(sub < 128, e[1], jnp.where(sub < 192, e[2], e[3]))), e


def _half(arr, i, lo):
    slab = arr[:, LANES * (i // 2):LANES * (i // 2 + 1)]
    return jnp.where(lo, slab, 0.0) if i % 2 == 0 else jnp.where(lo, 0.0, slab)


def ssd_fwd(xbc_c, u_main, bias_row, alog_row, d_row, name):
    lp = xbc_c.shape[0]
    q = min(ROW_TILE, lp)
    nc = lp // q
    dt_blk = U_DT // LANES

    def body(xs_ref, b_ref, c_ref, dt_ref, bias_ref, alog_ref, d_ref, y_ref, hp_ref, h_scr):
        c = pl.program_id(0)

        @pl.when(c == 0)
        def _():
            h_scr[...] = jnp.zeros_like(h_scr)

        ss, bbs, cbs, xss = [], [], [], []
        for g in range(2):
            xs = xs_ref[:, 2 * LANES * g:2 * LANES * (g + 1)]
            xss.append(xs)
            bbs.append(b_ref[:, LANES * g:LANES * (g + 1)].astype(BF16))
            cbs.append(c_ref[:, LANES * g:LANES * (g + 1)].astype(BF16))
            ss.append(_ssd_common(g, xs, dt_ref[...], bias_ref[...], alog_ref[...], q))
        gmats = [_dot(cbs[g], bbs[g], "nt") for g in range(2)]
        ms = [[(gmats[g] * ss[g]["lms"][i]).astype(BF16) for i in range(4)] for g in range(2)]
        xjs = [[_half(ss[g]["x"], i, ss[g]["lo"]).astype(BF16) for i in range(4)] for g in range(2)]
        ys = [[_dot(ms[g][i], xjs[g][i]) for i in range(4)] for g in range(2)]
        hps = [h_scr[g] for g in range(2)]
        yoffs = [_dot(cbs[g], hps[g].astype(BF16), "nt") * ss[g]["eacs"] for g in range(2)]
        upd = [_dot((ss[g]["x"] * ss[g]["dte"]).astype(BF16), bbs[g], "tn") for g in range(2)]
        outs = []
        for g in range(2):
            s = ss[g]
            hp_ref[g] = hps[g]
            d4 = s["lanes4"]([_pick_lane(d_ref[...], h) for h in s["heads"]])
            outs.append(jnp.concatenate([ys[g][0] + ys[g][1], ys[g][2] + ys[g][3]], axis=1) + yoffs[g] + d4 * xss[g])
            etot, _ = _etot(s["tots"])
            h_scr[g] = hps[g] * etot + upd[g]
        y_ref[...] = jnp.concatenate(outs, axis=1)

    in_specs = [
        pl.BlockSpec((q, 4 * LANES), lambda c: (c, 0)),
        pl.BlockSpec((q, 2 * LANES), lambda c: (c, 2)),
        pl.BlockSpec((q, 2 * LANES), lambda c: (c, 3)),
        pl.BlockSpec((q, LANES), lambda c: (c, dt_blk)),
        pl.BlockSpec((1, LANES), lambda c: (0, 0)),
        pl.BlockSpec((1, LANES), lambda c: (0, 0)),
        pl.BlockSpec((1, LANES), lambda c: (0, 0)),
    ]
    out_specs = [
        pl.BlockSpec((q, 4 * LANES), lambda c: (c, 0)),
        pl.BlockSpec((2, None, 2 * LANES, LANES), lambda c: (0, c, 0, 0)),
    ]
    return pl.pallas_call(
        body, name=name, grid=(nc,), in_specs=in_specs, out_specs=out_specs,
        out_shape=[jax.ShapeDtypeStruct((lp, SSD_WIDTH), F32), jax.ShapeDtypeStruct((2, nc, 2 * LANES, LANES), F32)],
        scratch_shapes=[pltpu.VMEM((2, 2 * LANES, LANES), F32)],
        compiler_params=_cp("arbitrary"),
    )(xbc_c, xbc_c, xbc_c, u_main, bias_row, alog_row, d_row)


def ssd_bwd(xbc_c, u_main, bias_row, alog_row, d_row, hprev, dy, name):
    lp = xbc_c.shape[0]
    q = min(ROW_TILE, lp)
    nc = lp // q
    dt_blk = U_DT // LANES
    G = range(2)

    def body(xs_ref, b_ref, c_ref, dt_ref, bias_ref, alog_ref, d_ref, hp_ref, dy_ref,
             dxs_ref, db_ref, dc_ref, ddt_ref, pg_ref, dh_scr):
        cc = pl.program_id(0)

        @pl.when(cc == 0)
        def _():
            dh_scr[...] = jnp.zeros_like(dh_scr)
            pg_ref[...] = jnp.zeros_like(pg_ref)

        wide = [slice(2 * LANES * g, 2 * LANES * (g + 1)) for g in G]
        one = [slice(LANES * g, LANES * (g + 1)) for g in G]
        xs = [xs_ref[:, wide[g]] for g in G]
        bb = [b_ref[:, one[g]].astype(BF16) for g in G]
        cb_ = [c_ref[:, one[g]].astype(BF16) for g in G]
        s = [_ssd_common(g, xs[g], dt_ref[...], bias_ref[...], alog_ref[...], q) for g in G]
        d_y = [dy_ref[:, wide[g]] for g in G]
        hp = [hp_ref[g] for g in G]
        hpb = [hp[g].astype(BF16) for g in G]
        dhn = [dh_scr[g] for g in G]
        dhnb = [dhn[g].astype(BF16) for g in G]
        x = [s[g]["x"] for g in G]
        xd = [x[g] * s[g]["dte"] for g in G]
        gmat = [_dot(cb_[g], bb[g], "nt") for g in G]
        m32s = [[gmat[g] * s[g]["lms"][i] for i in range(4)] for g in G]
        xjs = [[_half(x[g], i, s[g]["lo"]).astype(BF16) for i in range(4)] for g in G]
        dyjs = [[_half(d_y[g], i, s[g]["lo"]).astype(BF16) for i in range(4)] for g in G]
        dxparts = [[_dot(m32s[g][i].astype(BF16), dyjs[g][i], "tn") for i in range(4)] for g in G]
        dms = [[_dot(dyjs[g][i], xjs[g][i], "nt") for i in range(4)] for g in G]
        yoff = [_dot(cb_[g], hpb[g], "nt") * s[g]["eacs"] for g in G]
        d_t = [(d_y[g] * s[g]["eacs"]).astype(BF16) for g in G]
        d_hp = [_dot(d_t[g], cb_[g], "tn") for g in G]
        dxd = [_dot(bb[g], dhnb[g], "nt") for g in G]
        for g in G:
            sg, lane, hm, heads = s[g], s[g]["lane"], s[g]["hm"], s[g]["heads"]
            dg = (dms[g][0] * sg["lms"][0] + dms[g][1] * sg["lms"][1] + dms[g][2] * sg["lms"][2] + dms[g][3] * sg["lms"][3])
            wms = [dms[g][i] * m32s[g][i] for i in range(4)]
            row_part = [jnp.sum(wm, axis=1, keepdims=True) for wm in wms]
            col_part = [jnp.sum(wm, axis=0, keepdims=True) for wm in wms]
            dgb = dg.astype(BF16)
            d_c = _dot(dgb, bb[g]) + _dot(d_t[g], hpb[g])
            d_b = _dot(dgb, cb_[g], "tn") + _dot(xd[g].astype(BF16), dhnb[g])
            d_x = jnp.concatenate([dxparts[g][0] + dxparts[g][1], dxparts[g][2] + dxparts[g][3]], axis=1) + dxd[g] * sg["dte"]
            r = dxd[g] * xd[g]
            a_terms = d_y[g] * yoff[g] - r

            def hsum(arr):
                return [jnp.sum(jnp.where(hm[i], arr, 0.0), axis=1, keepdims=True) for i in range(4)]

            dacs, rs = hsum(a_terms), hsum(r)
            hh = dhn[g] * hp[g]
            sub = lax.broadcasted_iota(jnp.int32, (2 * LANES, LANES), 0)
            hsums = [jnp.sum(jnp.where(jnp.logical_and(sub >= 64 * i, sub < 64 * (i + 1)), hh, 0.0), keepdims=True) for i in range(4)]
            last = lax.broadcasted_iota(jnp.int32, (q, 1), 0) == q - 1
            etot, etots = _etot(sg["tots"])
            ddacs = jnp.zeros((q, LANES), F32)
            for i, h in enumerate(heads):
                dtot = jnp.sum(rs[i], keepdims=True) + hsums[i] * etots[i]
                ddacs = ddacs + jnp.where(lane == h, dacs[i] + row_part[i] + jnp.where(last, dtot, 0.0), 0.0)
            subl = lax.broadcasted_iota(jnp.int32, (LANES, q), 0)
            cols_t = jnp.zeros((LANES, q), F32)
            for i, h in enumerate(heads):
                cols_t = cols_t + jnp.where(subl == h, col_part[i], 0.0)
            ddacs = ddacs - cols_t.T
            da = _tri_dot(sg["ri"] <= sg["ci"], ddacs)
            ddt_own = hsum(d_x * xs[g])
            ddt = da * sg["a_row"]
            for i, h in enumerate(heads):
                ddt = ddt + jnp.where(lane == h, ddt_own[i], 0.0)
            draw = ddt * jax.nn.sigmoid(sg["pre"])
            ddt_ref[:, one[g]] = draw
            d4 = sg["lanes4"]([_pick_lane(d_ref[...], h) for h in heads])
            dxs_ref[:, wide[g]] = d4 * d_y[g] + d_x * sg["dt4"]
            db_ref[:, one[g]] = d_b
            dc_ref[:, one[g]] = d_c
            dds = hsum(d_y[g] * xs[g])
            lane1 = lane[0:1, :]
            dd_row = jnp.zeros((1, LANES), F32)
            for i, h in enumerate(heads):
                dd_row = dd_row + jnp.where(lane1 == h, jnp.sum(dds[i], keepdims=True), 0.0)
            dbias_row = jnp.sum(draw, axis=0, keepdims=True)
            dalog_row = jnp.sum(da * sg["dt"], axis=0, keepdims=True) * sg["a_row"]
            sub8 = lax.broadcasted_iota(jnp.int32, (SUBLANES, LANES), 0)
            pg_ref[SUBLANES * g:SUBLANES * (g + 1), :] += (jnp.where(sub8 == 0, dbias_row, 0.0) + jnp.where(sub8 == 1, dalog_row, 0.0)
                                                          + jnp.where(sub8 == 2, dd_row, 0.0))
            dh_scr[g] = d_hp[g] + etot * dhn[g]

    rc = lambda c: nc - 1 - c
    in_specs = [
        pl.BlockSpec((q, 4 * LANES), lambda c: (rc(c), 0)),
        pl.BlockSpec((q, 2 * LANES), lambda c: (rc(c), 2)),
        pl.BlockSpec((q, 2 * LANES), lambda c: (rc(c), 3)),
        pl.BlockSpec((q, LANES), lambda c: (rc(c), dt_blk)),
        pl.BlockSpec((1, LANES), lambda c: (0, 0)),
        pl.BlockSpec((1, LANES), lambda c: (0, 0)),
        pl.BlockSpec((1, LANES), lambda c: (0, 0)),
        pl.BlockSpec((2, None, 2 * LANES, LANES), lambda c: (0, rc(c), 0, 0)),
        pl.BlockSpec((q, 4 * LANES), lambda c: (rc(c), 0)),
    ]
    out_specs = [
        pl.BlockSpec((q, 4 * LANES), lambda c: (rc(c), 0)),
        pl.BlockSpec((q, 2 * LANES), lambda c: (rc(c), 0)),
        pl.BlockSpec((q, 2 * LANES), lambda c: (rc(c), 0)),
        pl.BlockSpec((q, 2 * LANES), lambda c: (rc(c), 0)),
        pl.BlockSpec((2 * SUBLANES, LANES), lambda c: (0, 0)),
    ]
    per_group = jax.ShapeDtypeStruct((lp, 2 * LANES), F32)
    return pl.pallas_call(
        body, name=name, grid=(nc,), in_specs=in_specs, out_specs=out_specs,
        out_shape=[jax.ShapeDtypeStruct((lp, SSD_WIDTH), F32), per_group, per_group, per_group,
                   jax.ShapeDtypeStruct((2 * SUBLANES, LANES), F32)],
        scratch_shapes=[pltpu.VMEM((2, 2 * LANES, LANES), F32)],
        compiler_params=_cp("arbitrary"),
    )(xbc_c, xbc_c, xbc_c, u_main, bias_row, alog_row, d_row, hprev, dy)


def _sb_blocks(qs, ks, r_runs, masked, bq, after_scores=None):
    ri = lax.broadcasted_iota(jnp.int32, (bq, bq), 0)
    ci = lax.broadcasted_iota(jnp.int32, (bq, bq), 1)
    tri_after = (ri > ci).astype(BF16)
    zs = [_dot(qj, kj, "nt") for qj, kj in zip(qs, ks)]
    extra = after_scores() if after_scores is not None else None
    sigs, ubs = [], []
    for z in zs:
        zb = z.astype(BF16)
        u = -(jnp.maximum(zb, 0) + jnp.log(1 + jnp.exp(-jnp.abs(zb))))
        sigs.append(jnp.exp(zb + u))
        if masked:
            u = jnp.where(ci < ri, u, jnp.zeros_like(u))
        ubs.append(u)
    afters = [_dot(ub, tri_after) for ub in ubs]
    usums = [after[:, 0:1] + ub[:, 0:1].astype(F32) for after, ub in zip(afters, ubs)]
    ws = []
    for sig, after, r_run in zip(sigs, afters, r_runs):
        w = sig * jnp.exp(after + r_run).astype(BF16)
        if masked:
            w = jnp.where(ci < ri, w, jnp.zeros_like(w))
        ws.append(w)
    return usums, sigs, ws, extra


def _split_heads(x, lo):
    out = []
    zero = jnp.zeros((x.shape[0], LANES), x.dtype)
    for p in range(2):
        xp = x[:, LANES * p:LANES * (p + 1)]
        out += [jnp.where(lo, xp, zero), jnp.where(lo, zero, xp)]
    return out


def _per_head(x):
    return [x[:, :LANES], x[:, :LANES], x[:, LANES:], x[:, LANES:]]


def _resident(shape, col):
    return pl.BlockSpec(shape, lambda i: (0, col), pipeline_mode=pl.Buffered(1))


def sb_attn_fwd(qkv, name):
    lp = qkv.shape[0]
    bq = min(ROW_TILE, lp)
    nq = lp // bq
    assert nq <= 64

    def body(q_ref, k_ref, v_ref, o_ref, rs_ref):
        qi = pl.program_id(0)
        lane = _lane((bq, LANES))
        lo = lane < 64
        qs = _split_heads(q_ref[...], lo)

        def step(kb, carry, masked):
            off = pl.multiple_of(kb * bq, bq)
            ks = _per_head(k_ref[pl.ds(off, bq), :])
            vs = _per_head(v_ref[pl.ds(off, bq), :])
            heads, rss = carry
            r_runs = [heads[h][1] for h in range(4)]
            rss = list(rss)
            for h in range(4):
                rss[h // 2] = jnp.where(lane == 64 * (h % 2) + kb, r_runs[h], rss[h // 2])
            usums, _, ws, _ = _sb_blocks(qs, ks, r_runs, masked, bq)
            pvs = [_dot(ws[h], vs[h]) for h in range(4)]
            out = tuple((heads[h][0] + pvs[h], r_runs[h] + usums[h]) for h in range(4))
            return out, tuple(rss)

        zero = (jnp.zeros((bq, LANES), F32), jnp.zeros((bq, 1), F32))
        zr = jnp.zeros((bq, LANES), F32)
        carry = step(qi, ((zero,) * 4, (zr, zr)), True)
        def several(t, c):
            for r in range(KEY_UNROLL):
                c = step(qi - 1 - r - KEY_UNROLL * t, c, False)
            return c

        carry = lax.fori_loop(0, qi // KEY_UNROLL, several, carry)
        rem = qi % KEY_UNROLL
        heads, rss = lax.fori_loop(0, rem, lambda t, c: step(rem - 1 - t, c, False), carry)
        o_ref[...] = jnp.concatenate([jnp.where(lo, heads[0][0], heads[1][0]), jnp.where(lo, heads[2][0], heads[3][0])], axis=1)
        rs_ref[...] = jnp.concatenate(list(rss), axis=1)

    blk = pl.BlockSpec((bq, 2 * LANES), lambda i: (i, 0))
    return pl.pallas_call(
        body, name=name, grid=(nq,),
        in_specs=[blk, _resident((lp, 2 * LANES), 1), _resident((lp, 2 * LANES), 2)],
        out_specs=[blk, blk],
        out_shape=[jax.ShapeDtypeStruct((lp, SB_WIDTH), F32), jax.ShapeDtypeStruct((lp, SB_WIDTH), F32)],
        compiler_params=_cp("arbitrary"),
    )(qkv, qkv, qkv)


def sb_attn_bwd(qkv, rs, d_o, name):
    lp = qkv.shape[0]
    bq = min(ROW_TILE, lp)
    nq = lp // bq

    def body(q_ref, k_ref, v_ref, rs_ref, do_ref, dq_ref, dk_ref, dv_ref):
        qi = pl.program_id(0)

        @pl.when(qi == 0)
        def _():
            dk_ref[...] = jnp.zeros_like(dk_ref)
            dv_ref[...] = jnp.zeros_like(dv_ref)

        lane = _lane((bq, LANES))
        lo = lane < 64
        qs = _split_heads(q_ref[...], lo)
        dos = _split_heads(do_ref[...].astype(BF16), lo)
        rs_blk = rs_ref[...]
        ri = lax.broadcasted_iota(jnp.int32, (bq, bq), 0)
        ci = lax.broadcasted_iota(jnp.int32, (bq, bq), 1)
        tbefore = (ri < ci).astype(BF16)

        def step(kb, carry, masked):
            off = pl.multiple_of(kb * bq, bq)
            ks = _per_head(k_ref[pl.ds(off, bq), :])
            vs = _per_head(v_ref[pl.ds(off, bq), :])
            r_rights = [jnp.sum(jnp.where(lane == 64 * (h % 2) + kb, rs_blk[:, LANES * (h // 2):LANES * (h // 2 + 1)], 0.0),
                                axis=1, keepdims=True) for h in range(4)]
            _, sigs, wbs, dws = _sb_blocks(qs, ks, r_rights, masked, bq,
                                           after_scores=lambda: [_dot(dos[h], vs[h], "nt") for h in range(4)])
            gs = [wbs[h].astype(F32) * dws[h] for h in range(4)]
            gbs = [g.astype(BF16) for g in gs]
            gbefores = [_dot(gb, tbefore) for gb in gbs]
            dv_acc = [_dot(wbs[2 * p], dos[2 * p], "tn") + _dot(wbs[2 * p + 1], dos[2 * p + 1], "tn") for p in range(2)]
            dzbs = []
            for h in range(4):
                dz = gs[h] - sigs[h].astype(F32) * (gs[h] + gbefores[h] + carry[h][1])
                if masked:
                    dz = jnp.where(ci < ri, dz, 0.0)
                dzbs.append(dz.astype(BF16))
            dqs = [_dot(dzbs[h], ks[h]) for h in range(4)]
            dk_acc = [_dot(dzbs[2 * p], qs[2 * p], "tn") + _dot(dzbs[2 * p + 1], qs[2 * p + 1], "tn") for p in range(2)]
            dk_ref[pl.ds(off, bq), :] += jnp.concatenate(dk_acc, axis=1)
            dv_ref[pl.ds(off, bq), :] += jnp.concatenate(dv_acc, axis=1)
            return tuple((carry[h][0] + dqs[h], carry[h][1] + jnp.sum(gs[h], axis=1, keepdims=True)) for h in range(4))

        zero = (jnp.zeros((bq, LANES), F32), jnp.zeros((bq, 1), F32))
        def several(t, c):
            for r in range(KEY_UNROLL):
                c = step(KEY_UNROLL * t + r, c, False)
            return c

        carry = lax.fori_loop(0, qi // KEY_UNROLL, several, (zero,) * 4)
        carry = lax.fori_loop(qi - qi % KEY_UNROLL, qi, lambda t, c: step(t, c, False), carry)
        carry = step(qi, carry, True)
        dq_ref[...] = jnp.concatenate([jnp.where(lo, carry[0][0], carry[1][0]), jnp.where(lo, carry[2][0], carry[3][0])],
                                      axis=1).astype(dq_ref.dtype)

    blk = pl.BlockSpec((bq, 2 * LANES), lambda i: (i, 0))
    return pl.pallas_call(
        body, name=name, grid=(nq,),
        in_specs=[blk, _resident((lp, 2 * LANES), 1), _resident((lp, 2 * LANES), 2), blk, blk],
        out_specs=[blk, _resident((lp, 2 * LANES), 0), _resident((lp, 2 * LANES), 0)],
        out_shape=[jax.ShapeDtypeStruct((lp, SB_WIDTH), BF16), jax.ShapeDtypeStruct((lp, SB_WIDTH), F32),
                   jax.ShapeDtypeStruct((lp, SB_WIDTH), F32)],
        compiler_params=_cp("arbitrary"),
    )(qkv, qkv, qkv, rs, d_o)


def _mla_masks(bq):
    lane = _lane((bq, 2 * LANES))
    out = []
    for h in range(4):
        j = h % 2
        nope = jnp.logical_and(lane >= 64 * j, lane < 64 * (j + 1))
        rope = jnp.logical_and(lane >= LANES + MLA_ROPE * h, lane < LANES + MLA_ROPE * (h + 1))
        out.append(jnp.logical_or(nope, rope))
    return out


def _mla_split_q(q, masks):
    zero = jnp.zeros((q.shape[0], 2 * LANES), q.dtype)
    return [jnp.where(masks[h], q[:, 2 * LANES * (h // 2):2 * LANES * (h // 2 + 1)], zero) for h in range(4)]


def _mla_per_head_k(k):
    return [k[:, :2 * LANES], k[:, :2 * LANES], k[:, 2 * LANES:], k[:, 2 * LANES:]]


def mla_attn_fwd(qc, kc, v, name):
    lp = qc.shape[0]
    bq = min(ROW_TILE, lp)
    nq = lp // bq

    def body(q_ref, k_ref, v_ref, o_ref, lse_ref):
        qi = pl.program_id(0)
        qs = _mla_split_q(q_ref[...], _mla_masks(bq))
        lo = _lane((bq, LANES)) < 64
        ri = lax.broadcasted_iota(jnp.int32, (bq, bq), 0)
        ci = lax.broadcasted_iota(jnp.int32, (bq, bq), 1)

        def blocks(kbs, carry, masked):
            offs = [pl.multiple_of(kb * bq, bq) for kb in kbs]
            ks = [_mla_per_head_k(k_ref[pl.ds(o, bq), :]) for o in offs]
            ones = jnp.ones((bq, LANES), BF16)
            vs = []
            for o in offs:
                vp = _per_head(v_ref[pl.ds(o, bq), :])
                vs.append([jnp.where(lo, vp[h], ones) if h % 2 == 0 else jnp.where(lo, ones, vp[h]) for h in range(4)])
            ss = [[_dot(qs[h], ks[b][h], "nt") for h in range(4)] for b in range(len(kbs))]
            if masked:
                ss = [[jnp.where(ci <= ri, s, NEG) for s in row] for row in ss]
            prs, alphas, ms = [], [], []
            for h in range(4):
                top = ss[0][h]
                for b in range(1, len(kbs)):
                    top = jnp.maximum(top, ss[b][h])
                m_new = jnp.maximum(carry[h][1], jnp.max(top, axis=1, keepdims=True))
                alphas.append(jnp.exp(carry[h][1] - m_new))
                ms.append(m_new)
                prs.append([jnp.exp(ss[b][h] - m_new).astype(BF16) for b in range(len(kbs))])
            out = []
            for h in range(4):
                acc = carry[h][0] * alphas[h]
                for b in range(len(kbs)):
                    acc = acc + _dot(prs[h][b], vs[b][h])
                out.append((acc, ms[h]))
            return tuple(out)

        zero = (jnp.zeros((bq, LANES), F32), jnp.full((bq, 1), NEG, F32))
        carry = blocks([qi], (zero,) * 4, True)
        carry = lax.fori_loop(0, qi // KEY_UNROLL,
                              lambda t, c: blocks([qi - 1 - r - KEY_UNROLL * t for r in range(KEY_UNROLL)], c, False), carry)
        rem = qi % KEY_UNROLL
        carry = lax.fori_loop(0, rem, lambda t, c: blocks([rem - 1 - t], c, False), carry)
        outs, lses = [], []
        for h in range(4):
            acc, m = carry[h]
            l = acc[:, 64:65] if h % 2 == 0 else acc[:, 0:1]
            outs.append(acc / l)
            lses.append(m + jnp.log(l))
        o_ref[...] = jnp.concatenate([jnp.where(lo, outs[0], outs[1]), jnp.where(lo, outs[2], outs[3])], axis=1)
        lse_ref[...] = jnp.concatenate([jnp.where(lo, lses[0], lses[1]), jnp.where(lo, lses[2], lses[3])], axis=1)

    blk = pl.BlockSpec((bq, 2 * LANES), lambda i: (i, 0))
    return pl.pallas_call(
        body, name=name, grid=(nq,),
        in_specs=[pl.BlockSpec((bq, 4 * LANES), lambda i: (i, 0)), _resident((lp, 4 * LANES), 0), _resident((lp, 2 * LANES), 0)],
        out_specs=[blk, blk],
        out_shape=[jax.ShapeDtypeStruct((lp, 2 * LANES), F32), jax.ShapeDtypeStruct((lp, 2 * LANES), F32)],
        compiler_params=_cp("arbitrary"),
    )(qc, kc, v)


def mla_attn_bwd(qc, kc, v, o, lse, d_o, name):
    lp = qc.shape[0]
    bq = min(ROW_TILE, lp)
    nq = lp // bq

    def body(q_ref, k_ref, v_ref, o_ref, lse_ref, do_ref, dq_ref, dk_ref, dv_ref):
        qi = pl.program_id(0)

        @pl.when(qi == 0)
        def _():
            dk_ref[...] = jnp.zeros_like(dk_ref)
            dv_ref[...] = jnp.zeros_like(dv_ref)

        d_o = do_ref[...]
        masks = _mla_masks(bq)
        qs = _mla_split_q(q_ref[...], masks)
        lo = _lane((bq, LANES)) < 64
        dos = _split_heads(d_o.astype(BF16), lo)
        od = o_ref[...] * d_o
        lse_blk = lse_ref[...]
        delta, lses = [], []
        for h in range(4):
            odp = od[:, LANES * (h // 2):LANES * (h // 2 + 1)]
            delta.append(jnp.sum(jnp.where(lo, odp, 0.0) if h % 2 == 0 else jnp.where(lo, 0.0, odp), axis=1, keepdims=True))
            c0 = LANES * (h // 2) + 64 * (h % 2)
            lses.append(lse_blk[:, c0:c0 + 1])
        ri = lax.broadcasted_iota(jnp.int32, (bq, bq), 0)
        ci = lax.broadcasted_iota(jnp.int32, (bq, bq), 1)

        def step(kb, carry, masked):
            off = pl.multiple_of(kb * bq, bq)
            ks = _mla_per_head_k(k_ref[pl.ds(off, bq), :])
            vs = _per_head(v_ref[pl.ds(off, bq), :])
            ss = [_dot(qs[h], ks[h], "nt") for h in range(4)]
            dps = [_dot(dos[h], vs[h], "nt") for h in range(4)]
            prbs, dss = [], []
            for h in range(4):
                s = ss[h]
                if masked:
                    s = jnp.where(ci <= ri, s, NEG)
                pr = jnp.exp(s - lses[h])
                prbs.append(pr.astype(BF16))
                dss.append((pr * (dps[h] - delta[h])).astype(BF16))
            dv_acc = [_dot(prbs[2 * p], dos[2 * p], "tn") + _dot(prbs[2 * p + 1], dos[2 * p + 1], "tn") for p in range(2)]
            dqs = [_dot(dss[h], ks[h]) for h in range(4)]
            dk_acc = [_dot(dss[2 * p], qs[2 * p], "tn") + _dot(dss[2 * p + 1], qs[2 * p + 1], "tn") for p in range(2)]
            dk_ref[pl.ds(off, bq), :] += jnp.concatenate(dk_acc, axis=1)
            dv_ref[pl.ds(off, bq), :] += jnp.concatenate(dv_acc, axis=1)
            return tuple(carry[h] + dqs[h] for h in range(4))

        zero = jnp.zeros((bq, 2 * LANES), F32)
        carry = step(qi, (zero,) * 4, True)
        def several(t, c):
            for r in range(KEY_UNROLL):
                c = step(qi - 1 - r - KEY_UNROLL * t, c, False)
            return c

        carry = lax.fori_loop(0, qi // KEY_UNROLL, several, carry)
        rem = qi % KEY_UNROLL
        carry = lax.fori_loop(0, rem, lambda t, c: step(rem - 1 - t, c, False), carry)
        dq_ref[...] = jnp.concatenate([jnp.where(masks[0], carry[0], 0.0) + jnp.where(masks[1], carry[1], 0.0),
                                       jnp.where(masks[2], carry[2], 0.0) + jnp.where(masks[3], carry[3], 0.0)], axis=1)

    blk = pl.BlockSpec((bq, 2 * LANES), lambda i: (i, 0))
    wide = pl.BlockSpec((bq, 4 * LANES), lambda i: (i, 0))
    return pl.pallas_call(
        body, name=name, grid=(nq,),
        in_specs=[wide, _resident((lp, 4 * LANES), 0), _resident((lp, 2 * LANES), 0), blk, blk, blk],
        out_specs=[wide, _resident((lp, 4 * LANES), 0), _resident((lp, 2 * LANES), 0)],
        out_shape=[jax.ShapeDtypeStruct((lp, 4 * LANES), F32), jax.ShapeDtypeStruct((lp, 4 * LANES), F32),
                   jax.ShapeDtypeStruct((lp, 2 * LANES), F32)],
        compiler_params=_cp("arbitrary"),
    )(qc, kc, v, o, lse, d_o)


def _mix_out(y_pre, z, o_sb, o_mla, g_ssd, g_sb, g_mla):
    return jnp.concatenate([_rms(y_pre * _silu(z), g_ssd), _rms(o_sb, g_sb), _rms(o_mla, g_mla)], axis=1)


def _ffn_act(up_a, up_b, halo_a, halo_b, w_a, w_b, b_a, b_b, i):
    ca = conv_fwd(up_a, halo_a, w_a, i) + b_a
    cb_ = conv_fwd(up_b, halo_b, w_b, i) + b_b
    return ca, cb_


def layer_fwd(h, w, cs, sn, l):
    lp = h.shape[0]
    nm = f"l{l}_"
    hn = rowwise(lambda i, n, x, g: _rms(x, g), [RI(h), PA(w["norm_mix_g"])], [RO(D_MODEL, BF16)], nm + "rms_mix", lp, tm=WIDE_TILE)[0]
    u = mm(hn, w["w_main"], "nn", F32, nm + "in_main")
    qkv = mm(hn, w["w_sb"], "nn", BF16, nm + "in_sb")
    xbc_c = rowwise(lambda i, n, x, hl, cw, cb_: _silu(conv_fwd(x, hl, cw, i) + cb_),
                    [RI(u, SSD_XBC, 0), HP(u, SSD_XBC, 0), PA(w["ssd_conv_w"]), PA(w["ssd_conv_b"])],
                    [RO(SSD_XBC, F32)], nm + "ssd_conv", lp, tm=WIDE_TILE)[0]
    y_pre, hprev = ssd_fwd(xbc_c, u, w["dt_bias"], w["a_log"], w["d_skip"], nm + "ssd_fwd")
    o_sb, rs_sb = sb_attn_fwd(qkv, nm + "sb_fwd")
    qn, kvn = rowwise(lambda i, n, qa, ckv, gq, gkv: (_rms(qa, gq, MLA_Q_RANK), _rms(ckv, gkv)),
                      [RI(u, 256, U_QA // 256), RI(u, LANES, U_CKV // LANES), PA(w["q_norm_g"]), PA(w["kv_norm_g"])],
                      [RO(256, BF16), RO(LANES, BF16)], nm + "mla_rms", lp, tm=WIDE_TILE)
    qf = mm(qn, w["w_uq"], "nn", F32, nm + "mla_uq")
    kvf = mm(kvn, w["w_ukv"], "nn", F32, nm + "mla_ukv")

    def pack(i, n, qf_, kvf_, kr4, cos, sin):
        qf_ = qf_ * MLA_SCALE
        qr = qf_[:, 256:384]
        qr = qr * cos + rope_rot(qr) * sin
        kr = kr4 * cos + rope_rot(kr4) * sin
        qc = jnp.concatenate([qf_[:, 0:128], qr, qf_[:, 128:256], qr], axis=1)
        kc = jnp.concatenate([kvf_[:, 0:128], kr, kvf_[:, 128:256], kr], axis=1)
        return qc, kc, kvf_[:, 256:512]

    qc, kc, vv = rowwise(pack, [RI(qf), RI(kvf), RI(u, LANES, U_KR4 // LANES), RI(cs), RI(sn)],
                         [RO(512, BF16), RO(512, BF16), RO(256, BF16)], nm + "mla_pack", lp, tm=WIDE_TILE)
    o_mla, lse = mla_attn_fwd(qc, kc, vv, nm + "mla_fwd")
    cat = rowwise(lambda i, n, *a: _mix_out(*a),
                  [RI(y_pre), RI(u, SSD_WIDTH, U_Z // SSD_WIDTH), RI(o_sb), RI(o_mla),
                   PA(w["ssd_norm_g"]), PA(w["sb_norm_g"]), PA(w["mla_norm_g"])],
                  [RO(D_MODEL, BF16)], nm + "mix_out", lp, tm=WIDE_TILE)[0]
    h_mid = mm(cat, w["w_out"], "nn", F32, nm + "out_proj", add=h)
    hn2 = rowwise(lambda i, n, x, g: _rms(x, g), [RI(h_mid), PA(w["norm_ffn_g"])], [RO(D_MODEL, BF16)], nm + "rms_ffn", lp, tm=WIDE_TILE)[0]
    up_a = mm(hn2, w["w_up_a"], "nn", F32, nm + "up_a")
    up_b = mm(hn2, w["w_up_b"], "nn", F32, nm + "up_b")
    wc = 1408

    def act(i, n, ua, ub, ha, hb_, wa, wb, ba, bb_):
        ca, cb_ = _ffn_act(ua, ub, ha, hb_, wa, wb, ba, bb_, i)
        return _silu(ca) * cb_

    a_t = rowwise(act, [RI(up_a, wc, 0, True), RI(up_b, wc, 0, True), HP(up_a, wc, 0, True), HP(up_b, wc, 0, True),
                        PA(w["ffn_conv_w_a"], wc, 0, True), PA(w["ffn_conv_w_b"], wc, 0, True),
                        PA(w["ffn_conv_b_a"], wc, 0, True), PA(w["ffn_conv_b_b"], wc, 0, True)],
                  [RO(D_FF, BF16, wc, True)], nm + "ffn_act", lp, tm=WIDE_TILE, ncol=D_FF // wc)[0]
    h_out = mm(a_t, w["w_down"], "nn", F32, nm + "down", add=h_mid)
    saved = dict(h=h, hn=hn, u=u, qkv=qkv, xbc_c=xbc_c, y_pre=y_pre, hprev=hprev, o_sb=o_sb, rs_sb=rs_sb, qn=qn, kvn=kvn,
                 qc=qc, kc=kc, vv=vv, o_mla=o_mla, lse=lse, cat=cat, h_mid=h_mid, hn2=hn2, up_a=up_a, up_b=up_b, a_t=a_t)
    return h_out, saved


def layer_bwd(dh_out, dh_out_b, w, s, cs, sn, l):
    lp = dh_out.shape[0]
    nm = f"l{l}b_"
    g = {}
    wc = 1408
    ncolf = D_FF // wc
    g["w_down"] = mm(s["a_t"], dh_out_b, "tn", BF16, nm + "dw_down")
    d_act = mm(dh_out_b, w["w_down"], "nt", F32, nm + "d_act")

    def act_bwd(i, n, ua, ub, ha, hb_, wa, wb, ba, bb_, da_):
        ca, cb_ = _ffn_act(ua, ub, ha, hb_, wa, wb, ba, bb_, i)
        sg = jax.nn.sigmoid(ca)
        dca = da_ * cb_ * (sg * (1.0 + ca * (1.0 - sg)))
        dcb = da_ * (ca * sg)
        return (dca, dcb, conv_bwd_w(dca, ua, ha, i, FFN_CONV), conv_bwd_w(dcb, ub, hb_, i, FFN_CONV),
                jnp.sum(dca, axis=0, keepdims=True), jnp.sum(dcb, axis=0, keepdims=True))

    dca, dcb, g["ffn_conv_w_a"], g["ffn_conv_w_b"], g["ffn_conv_b_a"], g["ffn_conv_b_b"] = rowwise(
        act_bwd, [RI(s["up_a"], wc, 0, True), RI(s["up_b"], wc, 0, True), HP(s["up_a"], wc, 0, True),
                  HP(s["up_b"], wc, 0, True), PA(w["ffn_conv_w_a"], wc, 0, True), PA(w["ffn_conv_w_b"], wc, 0, True),
                  PA(w["ffn_conv_b_a"], wc, 0, True), PA(w["ffn_conv_b_b"], wc, 0, True), RI(d_act, wc, 0, True)],
        [RO(D_FF, F32, wc, True), RO(D_FF, F32, wc, True), AO(FFN_CONV, D_FF, wc, True), AO(FFN_CONV, D_FF, wc, True),
         AO(1, D_FF, wc, True), AO(1, D_FF, wc, True)], nm + "ffn_act_bwd", lp, tm=WIDE_TILE // 2, ncol=ncolf)

    def conv_t(i, n, da_, db_, ha, hb_, wa, wb):
        return conv_bwd_data(da_, ha, wa, i, n), conv_bwd_data(db_, hb_, wb, i, n)

    dup_a, dup_b = rowwise(conv_t, [RI(dca, wc, 0, True), RI(dcb, wc, 0, True), HN(dca, wc, 0, True), HN(dcb, wc, 0, True),
                                    PA(w["ffn_conv_w_a"], wc, 0, True), PA(w["ffn_conv_w_b"], wc, 0, True)],
                           [RO(D_FF, BF16, wc, True), RO(D_FF, BF16, wc, True)], nm + "ffn_conv_t", lp, tm=WIDE_TILE, ncol=ncolf)
    g["w_up_a"] = mm(s["hn2"], dup_a, "tn", BF16, nm + "dw_up_a")
    g["w_up_b"] = mm(s["hn2"], dup_b, "tn", BF16, nm + "dw_up_b")
    dhn2 = mm(dup_a, w["w_up_a"], "nt", F32, nm + "dhn2_a")
    dhn2 = mm(dup_b, w["w_up_b"], "nt", F32, nm + "dhn2_b", add=dhn2)

    def rms_bwd(i, n, x, gg, dy, dres):
        _, vjp = jax.vjp(_rms, x, gg)
        dx, dg = vjp(dy)
        d = dres + dx
        return d, d, dg

    dh_mid, dh_mid_b, g["norm_ffn_g"] = rowwise(
        rms_bwd, [RI(s["h_mid"]), PA(w["norm_ffn_g"]), RI(dhn2), RI(dh_out)],
        [RO(D_MODEL, F32), RO(D_MODEL, BF16), AO(1, D_MODEL)], nm + "rms_ffn_bwd", lp, tm=WIDE_TILE)
    g["w_out"] = mm(s["cat"], dh_mid_b, "tn", BF16, nm + "dw_out")
    d_cat = mm(dh_mid_b, w["w_out"], "nt", F32, nm + "d_cat")
    u = s["u"]

    def mix_bwd(i, n, y_pre, z, o_sb, o_mla, g1, g2, g3, dcat):
        _, vjp = jax.vjp(_mix_out, y_pre, z, o_sb, o_mla, g1, g2, g3)
        return vjp(dcat)

    dy_pre, dz, do_sb, do_mla, g["ssd_norm_g"], g["sb_norm_g"], g["mla_norm_g"] = rowwise(
        mix_bwd, [RI(s["y_pre"]), RI(u, SSD_WIDTH, U_Z // SSD_WIDTH), RI(s["o_sb"]), RI(s["o_mla"]),
                  PA(w["ssd_norm_g"]), PA(w["sb_norm_g"]), PA(w["mla_norm_g"]), RI(d_cat)],
        [RO(SSD_WIDTH, F32), RO(SSD_WIDTH, BF16), RO(SB_WIDTH, F32), RO(256, F32),
         AO(1, SSD_WIDTH), AO(1, SB_WIDTH), AO(1, 256)], nm + "mix_out_bwd", lp, tm=WIDE_TILE)
    dxs, dbp, dcp, ddtp, pg = ssd_bwd(s["xbc_c"], u, w["dt_bias"], w["a_log"], w["d_skip"], s["hprev"], dy_pre, nm + "ssd_bwd")
    pg = pg.reshape(2, SUBLANES, LANES).sum(axis=0)
    g["dt_bias"], g["a_log"], g["d_skip"] = pg[0:1], pg[1:2], pg[2:3]

    def conv4_bwd(i, n, x, hl, cw, cb_, dxs_, dbp_, dcp_, ddtp_):
        pre = conv_fwd(x, hl, cw, i) + cb_
        d_out = jnp.concatenate([dxs_, dbp_, dcp_], axis=1)
        sg = jax.nn.sigmoid(pre)
        d_pre = d_out * (sg * (1.0 + pre * (1.0 - sg)))
        ddt = ddtp_[:, 0:128] + ddtp_[:, 128:256]
        return d_pre, ddt, conv_bwd_w(d_pre, x, hl, i, SSD_CONV), jnp.sum(d_pre, axis=0, keepdims=True)

    d_pre, ddt, g["ssd_conv_w"], g["ssd_conv_b"] = rowwise(
        conv4_bwd, [RI(u, SSD_XBC, 0), HP(u, SSD_XBC, 0), PA(w["ssd_conv_w"]), PA(w["ssd_conv_b"]),
                    RI(dxs), RI(dbp), RI(dcp), RI(ddtp)],
        [RO(SSD_XBC, F32), RO(LANES, BF16), AO(SSD_CONV, SSD_XBC), AO(1, SSD_XBC)], nm + "ssd_conv_bwd", lp, tm=WIDE_TILE)
    d_xbc = rowwise(lambda i, n, d, hn_, cw: conv_bwd_data(d, hn_, cw, i, n),
                    [RI(d_pre), HN(d_pre), PA(w["ssd_conv_w"])], [RO(SSD_XBC, BF16)], nm + "ssd_conv_t", lp, tm=WIDE_TILE)[0]
    dq_sb, dk_sb, dv_sb = sb_attn_bwd(s["qkv"], s["rs_sb"], do_sb, nm + "sb_bwd")
    dqkv = jnp.concatenate([dq_sb, dk_sb.astype(BF16), dv_sb.astype(BF16)], axis=1)
    dqc, dkc, dvv = mla_attn_bwd(s["qc"], s["kc"], s["vv"], s["o_mla"], s["lse"], do_mla, nm + "mla_bwd")

    def unpack(i, n, dqc_, dkc_, dvv_, cos, sin):
        dqr = dqc_[:, 128:256] + dqc_[:, 384:512]
        dqr = dqr * cos + rope_rot_t(dqr * sin)
        dkr = dkc_[:, 128:256] + dkc_[:, 384:512]
        dkr = dkr * cos + rope_rot_t(dkr * sin)
        dq = jnp.concatenate([dqc_[:, 0:128], dqc_[:, 256:384], dqr], axis=1) * MLA_SCALE
        dkv = jnp.concatenate([dkc_[:, 0:128], dkc_[:, 256:384], dvv_], axis=1)
        return dq, dkv, dkr

    dq, dkv, dkr4 = rowwise(unpack, [RI(dqc), RI(dkc), RI(dvv), RI(cs), RI(sn)],
                            [RO(384, BF16), RO(512, BF16), RO(LANES, BF16)], nm + "mla_unpack", lp, tm=WIDE_TILE)
    g["w_uq"] = mm(s["qn"], dq, "tn", F32, nm + "dw_uq")
    g["w_ukv"] = mm(s["kvn"], dkv, "tn", F32, nm + "dw_ukv")
    dqn = mm(dq, w["w_uq"], "nt", F32, nm + "dqn")
    dkvn = mm(dkv, w["w_ukv"], "nt", F32, nm + "dkvn")

    def mla_rms_bwd(i, n, qa, ckv, gq, gkv, dqn_, dkvn_):
        _, vjp = jax.vjp(lambda a, b, c, d: (_rms(a, c, MLA_Q_RANK), _rms(b, d)), qa, ckv, gq, gkv)
        return vjp((dqn_, dkvn_))

    dqa, dckv, g["q_norm_g"], g["kv_norm_g"] = rowwise(
        mla_rms_bwd, [RI(u, 256, U_QA // 256), RI(u, LANES, U_CKV // LANES), PA(w["q_norm_g"]), PA(w["kv_norm_g"]),
                      RI(dqn), RI(dkvn)],
        [RO(256, BF16), RO(LANES, BF16), AO(1, 256), AO(1, LANES)], nm + "mla_rms_bwd", lp, tm=WIDE_TILE)
    du = jnp.concatenate([d_xbc, dz, dqa, dckv, dkr4, ddt, jnp.zeros((lp, LANES), BF16)], axis=1)
    g["w_main"] = mm(s["hn"], du, "tn", F32, nm + "dw_main")
    g["w_sb"] = mm(s["hn"], dqkv, "tn", F32, nm + "dw_sb")
    dhn = mm(du, w["w_main"], "nt", F32, nm + "dhn_main")
    dhn = mm(dqkv, w["w_sb"], "nt", F32, nm + "dhn_sb", add=dhn)
    dh_in, dh_in_b, g["norm_mix_g"] = rowwise(
        rms_bwd, [RI(s["h"]), PA(w["norm_mix_g"]), RI(dhn), RI(dh_mid)],
        [RO(D_MODEL, F32), RO(D_MODEL, BF16), AO(1, D_MODEL)], nm + "rms_mix_bwd", lp, tm=WIDE_TILE)
    return dh_in, dh_in_b, g


_IN_CUTS = np.cumsum((512, 1024, 8, 256, 256, 256, 192, 128, 32))


def _pad_cols(a, n):
    return jnp.pad(a, ((0, 0), (0, n - a.shape[1])))


def prep_layer_weights(full, l):
    w_in = full["w_in"][l]
    c = _IN_CUTS
    z, xbc, dtr = w_in[:, :c[0]], w_in[:, c[0]:c[1]], w_in[:, c[1]:c[2]]
    q_sb, k_sb, v_sb = w_in[:, c[2]:c[3]], w_in[:, c[3]:c[4]], w_in[:, c[4]:c[5]]
    q_a, c_kv, k_r = w_in[:, c[5]:c[6]], w_in[:, c[6]:c[7]], w_in[:, c[7]:c[8]]
    w_main = jnp.concatenate([xbc, z, _pad_cols(q_a, 256), c_kv, k_r, k_r, k_r, k_r, _pad_cols(dtr, 256)], axis=1)
    assert w_main.shape[1] == U_MAIN
    row = lambda v, n=None: _pad_cols(v.reshape(1, -1).astype(F32), v.size if n is None else n)
    uq = full["mla_w_uq"][l].reshape(MLA_Q_RANK, 4, 96)
    w_uq = jnp.concatenate([uq[:, :, :64].reshape(MLA_Q_RANK, 256), uq[:, :, 64:].reshape(MLA_Q_RANK, 128)], axis=1)
    w_uq = jnp.pad(w_uq, ((0, 256 - MLA_Q_RANK), (0, 0)))
    ukv = full["mla_w_ukv"][l].reshape(MLA_KV_RANK, 4, 128)
    w_ukv = jnp.concatenate([ukv[:, :, :64].reshape(MLA_KV_RANK, 256), ukv[:, :, 64:].reshape(MLA_KV_RANK, 256)], axis=1)
    return dict(
        norm_mix_g=row(full["norm_mix_g"][l]), w_main=w_main, w_sb=jnp.concatenate([q_sb * SB_SCALE, k_sb, v_sb], axis=1),
        ssd_conv_w=full["ssd_conv_w"][l], ssd_conv_b=row(full["ssd_conv_b"][l]),
        dt_bias=row(full["ssd_dt_bias"][l], LANES), a_log=row(full["ssd_a_log"][l], LANES), d_skip=row(full["ssd_d"][l], LANES),
        ssd_norm_g=row(full["ssd_norm_g"][l]), sb_norm_g=row(full["sb_norm_g"][l]),
        q_norm_g=row(full["mla_q_norm_g"][l], 256), kv_norm_g=row(full["mla_kv_norm_g"][l]),
        w_uq=w_uq, w_ukv=w_ukv, mla_norm_g=row(full["mla_norm_g"][l]),
        w_out=full["w_out"][l], norm_ffn_g=row(full["norm_ffn_g"][l]),
        w_up_a=full["ffn_w_up"][l][:, :D_FF], w_up_b=full["ffn_w_up"][l][:, D_FF:],
        ffn_conv_w_a=full["ffn_conv_w"][l][:, :D_FF], ffn_conv_w_b=full["ffn_conv_w"][l][:, D_FF:],
        ffn_conv_b_a=row(full["ffn_conv_b"][l][:D_FF]), ffn_conv_b_b=row(full["ffn_conv_b"][l][D_FF:]),
        w_down=full["ffn_w_down"][l],
    )


def unprep_layer_grads(g):
    wm = g["w_main"]
    xbc, z = wm[:, U_XBC:U_XBC + 1024], wm[:, U_Z:U_Z + 512]
    q_a, c_kv = wm[:, U_QA:U_QA + MLA_Q_RANK], wm[:, U_CKV:U_CKV + 128]
    k_r = (wm[:, U_KR4:U_KR4 + 32] + wm[:, U_KR4 + 32:U_KR4 + 64] + wm[:, U_KR4 + 64:U_KR4 + 96] + wm[:, U_KR4 + 96:U_KR4 + 128])
    dtr = wm[:, U_DT:U_DT + SSD_HEADS]
    w_sb = g["w_sb"]
    w_in = jnp.concatenate([z, xbc, dtr, w_sb[:, :SB_WIDTH] * SB_SCALE, w_sb[:, SB_WIDTH:], q_a, c_kv, k_r], axis=1)
    guq = g["w_uq"][:MLA_Q_RANK]
    guq = jnp.concatenate([guq[:, :256].reshape(MLA_Q_RANK, 4, 64), guq[:, 256:].reshape(MLA_Q_RANK, 4, 32)], axis=2)
    gukv = g["w_ukv"]
    gukv = jnp.concatenate([gukv[:, :256].reshape(MLA_KV_RANK, 4, 64), gukv[:, 256:].reshape(MLA_KV_RANK, 4, 64)], axis=2)
    return dict(
        norm_mix_g=g["norm_mix_g"][0], w_in=w_in, ssd_conv_w=g["ssd_conv_w"], ssd_conv_b=g["ssd_conv_b"][0],
        ssd_dt_bias=g["dt_bias"][0, :SSD_HEADS], ssd_a_log=g["a_log"][0, :SSD_HEADS], ssd_d=g["d_skip"][0, :SSD_HEADS],
        ssd_norm_g=g["ssd_norm_g"][0], sb_norm_g=g["sb_norm_g"][0], mla_q_norm_g=g["q_norm_g"][0, :MLA_Q_RANK],
        mla_kv_norm_g=g["kv_norm_g"][0], mla_w_uq=guq.reshape(MLA_Q_RANK, 384), mla_w_ukv=gukv.reshape(MLA_KV_RANK, 512),
        mla_norm_g=g["mla_norm_g"][0], w_out=g["w_out"], norm_ffn_g=g["norm_ffn_g"][0],
        ffn_w_up=jnp.concatenate([g["w_up_a"], g["w_up_b"]], axis=1),
        ffn_conv_w=jnp.concatenate([g["ffn_conv_w_a"], g["ffn_conv_w_b"]], axis=1),
        ffn_conv_b=jnp.concatenate([g["ffn_conv_b_a"][0], g["ffn_conv_b_b"][0]], axis=0),
        ffn_w_down=g["w_down"],
    )


def rope_tables(lp):
    pos = jnp.arange(lp, dtype=F32)
    inv = 1.0 / (ROPE_BASE ** (jnp.arange(0, MLA_ROPE, 2, dtype=F32) / MLA_ROPE))
    ang = pos[:, None] * inv[None, :]
    ang = jnp.concatenate([ang, ang] * 4, axis=-1)
    return jnp.cos(ang), jnp.sin(ang)


def local_step(x_seq, target, full):
    seq = x_seq.shape[0]
    length = seq + N_META
    lp = -(-length // ROW_TILE) * ROW_TILE
    cs, sn = rope_tables(lp)
    h = jnp.concatenate([full["meta_tokens"].astype(F32), x_seq, jnp.zeros((lp - length, D_MODEL), F32)], axis=0)
    tgt = jnp.pad(target, ((N_META, lp - length), (0, 0)))
    ws = [prep_layer_weights(full, l) for l in range(DEPTH)]
    saved = []
    for l in range(DEPTH):
        h, s = layer_fwd(h, ws[l], cs, sn, l)
        saved.append(s)
    fg = full["final_norm_g"].reshape(1, D_MODEL).astype(F32)
    tm = min(WIDE_TILE, lp)

    def loss_fn(i, n, x, g, t):
        rows = _rows_iota(x) + i * tm
        valid = jnp.logical_and(rows >= N_META, rows < length)

        def f(x_, g_):
            err = jnp.where(valid, _rms(x_, g_) - t, 0.0)
            return 0.5 * jnp.sum(err * err) * (1.0 / D_MODEL)

        val, (dx, dg) = jax.value_and_grad(f, argnums=(0, 1))(x, g)
        return dx, dx, jnp.full((1, LANES), val, F32), dg

    dh, dh_b, loss_row, g_final = rowwise(loss_fn, [RI(h), PA(fg), RI(tgt)],
                                          [RO(D_MODEL, F32), RO(D_MODEL, BF16), AO(1, LANES), AO(1, D_MODEL)],
                                    "loss_head", lp, tm=WIDE_TILE)
    grads = {}
    per_layer = [None] * DEPTH
    for l in reversed(range(DEPTH)):
        dh, dh_b, g = layer_bwd(dh, dh_b, ws[l], saved[l], cs, sn, l)
        per_layer[l] = unprep_layer_grads(g)
    for k in per_layer[0]:
        grads[k] = jnp.stack([per_layer[l][k] for l in range(DEPTH)], axis=0)
    grads["final_norm_g"] = g_final[0]
    grads["meta_tokens"] = dh[:N_META]
    return loss_row[0, 0], dh[N_META:length], grads


_ANY = pl.BlockSpec(memory_space=pl.ANY)


def chip_exchange(srcs, modes, name):
    n = len(srcs)
    flips = ((1, 0), (0, 1), (1, 1))

    def body(*refs):
        ins, outs = refs[:n], refs[n:2 * n]
        send_sems, recv_sems, fwd_send_sems, fwd_recv_sems, loc_sems = refs[2 * n:]
        x, y, c = lax.axis_index("x"), lax.axis_index("y"), lax.axis_index("c")
        me = 2 * x + y
        waits, forwards = [], []
        for a in range(n):
            whole = modes[a] != "slab"
            cp = pltpu.make_async_copy(ins[a] if whole else ins[a].at[me], outs[a].at[me], loc_sems.at[a])
            cp.start()
            waits.append(cp.wait)
            half = ins[a].shape[0] // 2 if modes[a] == "bcast_split" else None
            for k, (fx, fy) in enumerate(flips):
                px = 1 - x if fx else x
                py = 1 - y if fy else y
                peer = 2 * px + py
                if half is None:
                    src = ins[a] if whole else ins[a].at[peer]
                    dst = outs[a].at[me]
                else:
                    src = ins[a].at[pl.ds(c * half, half)]
                    dst = outs[a].at[me, pl.ds(c * half, half)]
                rc = pltpu.make_async_remote_copy(src_ref=src, dst_ref=dst, send_sem=send_sems.at[a, k],
                                                  recv_sem=recv_sems.at[a, k], device_id=(px, py, c), device_id_type=MESH_ID)
                rc.start()
                if half is None:
                    waits.append(rc.wait)
                else:
                    waits.append(rc.wait_send)
                    landed = outs[a].at[peer, pl.ds(c * half, half)]
                    fw = pltpu.make_async_remote_copy(src_ref=landed, dst_ref=landed, send_sem=fwd_send_sems.at[a, k],
                                                      recv_sem=fwd_recv_sems.at[a, k], device_id=(x, y, 1 - c),
                                                      device_id_type=MESH_ID)
                    forwards.append((rc, fw))
        for rc, fw in forwards:
            rc.wait_recv()
            fw.start()
        for rc, fw in forwards:
            fw.wait()
        for w in waits:
            w()

    out_shape = [jax.ShapeDtypeStruct((N_CHIPS,) + (s.shape if m != "slab" else s.shape[1:]), s.dtype) for s, m in zip(srcs, modes)]
    return pl.pallas_call(
        body, name=name, in_specs=[_ANY] * n, out_specs=[_ANY] * n, out_shape=out_shape,
        scratch_shapes=[pltpu.SemaphoreType.DMA((n, 3)), pltpu.SemaphoreType.DMA((n, 3)), pltpu.SemaphoreType.DMA((n, 3)),
                        pltpu.SemaphoreType.DMA((n, 3)), pltpu.SemaphoreType.DMA((n,))],
    )(*srcs)


def _piece(ref, mode, k):
    if mode == "slab":
        return ref.at[k]
    if mode == "rows":
        rs = ref.shape[1] // N_CHIPS
        return ref.at[:, pl.ds(pl.multiple_of(k * rs, 16), rs), :]
    if mode == "cols":
        cs = ref.shape[2] // N_CHIPS
        return ref.at[:, :, pl.ds(pl.multiple_of(k * cs, LANES), cs)]
    return ref


def _piece_shape(shape, mode):
    if mode == "slab":
        return shape[1:]
    if mode == "rows":
        return (shape[0], shape[1] // N_CHIPS, shape[2])
    if mode == "cols":
        return (shape[0], shape[1], shape[2] // N_CHIPS)
    return shape


def grad_exchange(srcs, modes, name):
    n = len(srcs)
    flips = ((1, 0), (0, 1), (1, 1))

    def body(*refs):
        ins, outs = refs[:n], refs[n:2 * n]
        send_sems, recv_sems, fwd_send_sems, fwd_recv_sems, sib_send_sems, sib_recv_sems, loc_sems = refs[2 * n:]
        x, y, c = lax.axis_index("x"), lax.axis_index("y"), lax.axis_index("c")
        me = 2 * x + y
        sibling = (x, y, 1 - c)
        waits, forwards = [], []
        for a in range(n):
            mine = _piece(ins[a], modes[a], me)
            slot = outs[a].at[4 * c + me]
            cp = pltpu.make_async_copy(mine, slot, loc_sems.at[a])
            cp.start()
            sb = pltpu.make_async_remote_copy(src_ref=mine, dst_ref=slot, send_sem=sib_send_sems.at[a],
                                              recv_sem=sib_recv_sems.at[a], device_id=sibling, device_id_type=MESH_ID)
            sb.start()
            waits += [cp.wait, sb.wait]
            for k, (fx, fy) in enumerate(flips):
                px = 1 - x if fx else x
                py = 1 - y if fy else y
                peer = 2 * px + py
                rc = pltpu.make_async_remote_copy(src_ref=_piece(ins[a], modes[a], peer), dst_ref=slot,
                                                  send_sem=send_sems.at[a, k], recv_sem=recv_sems.at[a, k],
                                                  device_id=(px, py, c), device_id_type=MESH_ID)
                rc.start()
                landed = outs[a].at[4 * c + peer]
                fw = pltpu.make_async_remote_copy(src_ref=landed, dst_ref=landed, send_sem=fwd_send_sems.at[a, k],
                                                  recv_sem=fwd_recv_sems.at[a, k], device_id=sibling, device_id_type=MESH_ID)
                waits.append(rc.wait_send)
                forwards.append((rc, fw))
        for rc, fw in forwards:
            rc.wait_recv()
            fw.start()
        for rc, fw in forwards:
            fw.wait()
        for w in waits:
            w()

    out_shape = [jax.ShapeDtypeStruct((2 * N_CHIPS,) + tuple(_piece_shape(s.shape, m)), s.dtype) for s, m in zip(srcs, modes)]
    dma = pltpu.SemaphoreType.DMA
    return pl.pallas_call(
        body, name=name, in_specs=[_ANY] * n, out_specs=[_ANY] * n, out_shape=out_shape,
        scratch_shapes=[dma((n, 3)), dma((n, 3)), dma((n, 3)), dma((n, 3)), dma((n,)), dma((n,)), dma((n,))],
    )(*srcs)


WEIGHT_NAMES = ("meta_tokens", "norm_mix_g", "w_in", "ssd_conv_w", "ssd_conv_b", "ssd_dt_bias", "ssd_a_log", "ssd_d",
                "ssd_norm_g", "sb_norm_g", "mla_q_norm_g", "mla_kv_norm_g", "mla_w_uq", "mla_w_ukv", "mla_norm_g",
                "w_out", "norm_ffn_g", "ffn_w_up", "ffn_conv_w", "ffn_conv_b", "ffn_w_down", "final_norm_g")
SHARD_AXIS = {"meta_tokens": 1, "w_in": 2, "ssd_conv_w": 2, "mla_w_uq": 2, "mla_w_ukv": 2, "w_out": 1, "ffn_w_up": 2,
              "ffn_conv_w": 2, "ffn_w_down": 1}
SHARDED = tuple(n for n in WEIGHT_NAMES if n in SHARD_AXIS)
REPLICATED = tuple(n for n in WEIGHT_NAMES if n not in SHARD_AXIS)
GATHER_BF16 = ("w_in", "mla_w_uq", "mla_w_ukv", "w_out", "ffn_w_up", "ffn_w_down")
GATHER_F32 = ("meta_tokens", "ssd_conv_w", "ffn_conv_w")
PACK_ROWS = ROW_TILE


def pack(arrs, dtype):
    flat = jnp.concatenate([a.reshape(-1).astype(dtype) for a in arrs])
    per = PACK_ROWS * PACK_W
    total = -(-flat.size // per) * per
    return jnp.pad(flat, (0, total - flat.size)).reshape(total // PACK_W, PACK_W)


def unpack(buf, shapes):
    flat = buf.reshape(-1)
    out, off = [], 0
    for shp in shapes:
        size = int(np.prod(shp))
        out.append(flat[off:off + size].reshape(shp))
        off += size
    return out


def gather_weights(a):
    full = {n: a[n] for n in REPLICATED}
    bufs = [pack([a[n] for n in GATHER_BF16], BF16), pack([a[n] for n in GATHER_F32], F32)]
    got = chip_exchange(bufs, ("bcast_split", "bcast"), "gather_weights")
    for names, g in ((GATHER_BF16, got[0]), (GATHER_F32, got[1])):
        pieces = [unpack(g[k], [a[n].shape for n in names]) for k in range(N_CHIPS)]
        for idx, n in enumerate(names):
            full[n] = jnp.concatenate([pieces[k][idx] for k in range(N_CHIPS)], axis=SHARD_AXIS[n])
    return full


BIG = ("w_in", "w_out", "ffn_w_up", "ffn_w_down")
BIG_MODE = {"w_in": "slab", "w_out": "rows", "ffn_w_up": "cols", "ffn_w_down": "rows"}
SMALL_SHARDED = tuple(n for n in SHARDED if n not in BIG)
ADAM_TILE = 128


def _adamw(i, n, *vals):
    parts, (w, m, v) = vals[:2 * N_CHIPS], vals[2 * N_CHIPS:]
    g = parts[0].astype(F32)
    for p in parts[1:]:
        g = g + p.astype(F32)
    m = ADAM_B1 * m + (1.0 - ADAM_B1) * g
    v = ADAM_B2 * v + (1.0 - ADAM_B2) * jnp.square(g)
    m_hat = m / (1.0 - ADAM_B1 ** ADAM_STEP)
    v_hat = v / (1.0 - ADAM_B2 ** ADAM_STEP)
    delta = -ADAM_LR * (m_hat / (jnp.sqrt(v_hat) + ADAM_EPS) + ADAM_WD * w)
    return g, delta, m, v


def _adamw_call(got, w, m, v, name):
    rows, width = w.shape
    flat = got.reshape(2 * N_CHIPS * rows, width)
    blk = rows // ADAM_TILE
    ins = [RI(flat, rblk=k * blk) for k in range(2 * N_CHIPS)] + [RI(w), RI(m), RI(v)]
    return rowwise(_adamw, ins, [RO(width, F32)] * 4, name, rows, tm=ADAM_TILE)


def reduce_and_update(a, grads):
    srcs, modes = [], []
    for n in BIG:
        g = grads[n].astype(BF16)
        if n == "w_in":
            cs = a[n].shape[2]
            g = g.reshape(DEPTH, D_MODEL, N_CHIPS, cs).transpose(2, 0, 1, 3)
        srcs.append(g)
        modes.append(BIG_MODE[n])
    slabs = []
    for k in range(N_CHIPS):
        parts = []
        for n in SMALL_SHARDED:
            ax = SHARD_AXIS[n]
            size = a[n].shape[ax]
            parts.append(lax.slice_in_dim(grads[n], k * size, (k + 1) * size, axis=ax))
        slabs.append(pack(parts, BF16))
    srcs += [jnp.stack(slabs, axis=0), pack([grads[n] for n in REPLICATED], F32)]
    modes += ["slab", "bcast"]
    got = grad_exchange(srcs, modes, "exchange_grads")
    outs = {}
    kinds = ("grad", "delta", "new_m", "new_v")
    for n, g8 in zip(BIG, got):
        shp = a[n].shape
        rows = shp[0] * shp[1]
        flat = lambda t: t.reshape(rows, shp[2])
        res = _adamw_call(g8.reshape(2 * N_CHIPS, rows, shp[2]), flat(a[n]), flat(a["m_" + n]), flat(a["v_" + n]), "adamw_" + n)
        for kind, val in zip(kinds, res):
            outs[(kind, n)] = val.reshape(shp)
    for tag, names, g8 in (("small", SMALL_SHARDED, got[len(BIG)]), ("rep", REPLICATED, got[len(BIG) + 1])):
        shapes = [a[n].shape for n in names]
        packed = [pack([a[pre + n] for n in names], F32) for pre in ("", "m_", "v_")]
        res = _adamw_call(g8, *packed, "adamw_" + tag)
        for kind, buf in zip(kinds, res):
            for n, val in zip(names, unpack(buf, shapes)):
                outs[(kind, n)] = val
    return outs


INPUT_NAMES = ("x",) + WEIGHT_NAMES + ("loss_target",) + tuple("m_" + n for n in WEIGHT_NAMES) + tuple("v_" + n for n in WEIGHT_NAMES)


def kernel(x, meta_tokens, norm_mix_g, w_in, ssd_conv_w, ssd_conv_b, ssd_dt_bias, ssd_a_log, ssd_d, ssd_norm_g, sb_norm_g, mla_q_norm_g, mla_kv_norm_g, mla_w_uq, mla_w_ukv, mla_norm_g, w_out, norm_ffn_g, ffn_w_up, ffn_conv_w, ffn_conv_b, ffn_w_down, final_norm_g, loss_target, m_meta_tokens, m_norm_mix_g, m_w_in, m_ssd_conv_w, m_ssd_conv_b, m_ssd_dt_bias, m_ssd_a_log, m_ssd_d, m_ssd_norm_g, m_sb_norm_g, m_mla_q_norm_g, m_mla_kv_norm_g, m_mla_w_uq, m_mla_w_ukv, m_mla_norm_g, m_w_out, m_norm_ffn_g, m_ffn_w_up, m_ffn_conv_w, m_ffn_conv_b, m_ffn_w_down, m_final_norm_g, v_meta_tokens, v_norm_mix_g, v_w_in, v_ssd_conv_w, v_ssd_conv_b, v_ssd_dt_bias, v_ssd_a_log, v_ssd_d, v_ssd_norm_g, v_sb_norm_g, v_mla_q_norm_g, v_mla_kv_norm_g, v_mla_w_uq, v_mla_w_ukv, v_mla_norm_g, v_w_out, v_norm_ffn_g, v_ffn_w_up, v_ffn_conv_w, v_ffn_conv_b, v_ffn_w_down, v_final_norm_g):
    args = (x, meta_tokens, norm_mix_g, w_in, ssd_conv_w, ssd_conv_b, ssd_dt_bias, ssd_a_log, ssd_d, ssd_norm_g, sb_norm_g, mla_q_norm_g, mla_kv_norm_g, mla_w_uq, mla_w_ukv, mla_norm_g, w_out, norm_ffn_g, ffn_w_up, ffn_conv_w, ffn_conv_b, ffn_w_down, final_norm_g, loss_target, m_meta_tokens, m_norm_mix_g, m_w_in, m_ssd_conv_w, m_ssd_conv_b, m_ssd_dt_bias, m_ssd_a_log, m_ssd_d, m_ssd_norm_g, m_sb_norm_g, m_mla_q_norm_g, m_mla_kv_norm_g, m_mla_w_uq, m_mla_w_ukv, m_mla_norm_g, m_w_out, m_norm_ffn_g, m_ffn_w_up, m_ffn_conv_w, m_ffn_conv_b, m_ffn_w_down, m_final_norm_g, v_meta_tokens, v_norm_mix_g, v_w_in, v_ssd_conv_w, v_ssd_conv_b, v_ssd_dt_bias, v_ssd_a_log, v_ssd_d, v_ssd_norm_g, v_sb_norm_g, v_mla_q_norm_g, v_mla_kv_norm_g, v_mla_w_uq, v_mla_w_ukv, v_mla_norm_g, v_w_out, v_norm_ffn_g, v_ffn_w_up, v_ffn_conv_w, v_ffn_conv_b, v_ffn_w_down, v_final_norm_g)
    a = dict(zip(INPUT_NAMES, args, strict=True))
    full = gather_weights(a)
    loss, grad_x, grads = local_step(a["x"][0], a["loss_target"][0], full)
    loss = lax.psum(loss, ("x", "y", "c"))
    outs = reduce_and_update(a, grads)
    result = [loss, grad_x[None]]
    for kind in ("grad", "delta", "new_m", "new_v"):
        result += [outs[(kind, n)] for n in WEIGHT_NAMES]
    return tuple(result)
```
